```python
import functools
import jax, jax.numpy as jnp
from jax import lax
import numpy as np

D_MODEL = 2048
BATCH = 4
SEQ = 2048
DEPTH = 1
DEC_BATCH = 32
DEC_SEQ = 1
PAST_LEN = 16384
PAGE_SIZE = 128

C_CONV = D_MODEL // 2
CONV_WIDTH = 31
HEAD_DIM = D_MODEL // 16
N_KV = 4
DIL_GROUPS = ((128, 1), (512, 4), (2048, 16))
N_GROUPS = len(DIL_GROUPS)
N_QH = N_GROUPS * N_KV
WINDOW_MAX = 2048
N_MEM = 256
N_XHEADS = 4
X_HEAD_DIM = D_MODEL // 16
D_FF = -(-8 * D_MODEL // (3 * 256)) * 256
EPS = 1e-6
NEG = -1e30
SCALE = HEAD_DIM ** -0.5
X_SCALE = X_HEAD_DIM ** -0.5

W_CONV_IN = 2 * C_CONV
W_QD = N_QH * HEAD_DIM
W_KV = N_KV * HEAD_DIM
W_QX = N_XHEADS * X_HEAD_DIM
W_GATE = 3 * D_MODEL
N_IN = W_CONV_IN + W_QD + 2 * W_KV + W_QX + W_GATE
SPLITS = (W_CONV_IN, W_CONV_IN + W_QD, W_CONV_IN + W_QD + W_KV,
          W_CONV_IN + W_QD + 2 * W_KV, W_CONV_IN + W_QD + 2 * W_KV + W_QX)

kernel_name = 'hybrid_conv_dilated_mem_decoder_step'


def _rmsnorm(x, g):
    xf = x.astype(jnp.float32)
    y = xf * lax.rsqrt(jnp.mean(xf * xf, axis=-1, keepdims=True) + EPS)
    return (y * g.astype(jnp.float32)).astype(x.dtype)


def _layernorm(x, g, b):
    xf = x.astype(jnp.float32)
    mu = jnp.mean(xf, axis=-1, keepdims=True)
    var = jnp.mean(jnp.square(xf - mu), axis=-1, keepdims=True)
    y = (xf - mu) * lax.rsqrt(var + EPS)
    return (y * g.astype(jnp.float32) + b.astype(jnp.float32)).astype(x.dtype)


def _alibi_slopes():
    i = jnp.arange(1, N_QH + 1, dtype=jnp.float32)
    return jnp.exp2(-8.0 * i / N_QH).reshape(N_GROUPS, N_KV)


def _conformer_conv(u_pre, prefix, conv_w, conv_b, ln_g, ln_b, w_conv_out):
    a, b = jnp.split(u_pre, 2, axis=-1)
    u = a * jax.nn.sigmoid(b)
    full = jnp.concatenate([prefix.astype(u.dtype), u], axis=1)
    y = lax.conv_general_dilated(full, conv_w[:, None, :].astype(u.dtype), window_strides=(1,),
                                 padding='VALID', dimension_numbers=('NWC', 'WIO', 'NWC'),
                                 feature_group_count=C_CONV)
    y = _layernorm(y + conv_b, ln_g, ln_b)
    return jax.nn.silu(y) @ w_conv_out, full[:, -(CONV_WIDTH - 1):]


def _mix_by_denominator(outs, lses):
    wts = jax.nn.softmax(jnp.stack(lses), axis=0)
    o = jnp.stack(outs)
    return jnp.sum(wts[..., None].astype(o.dtype) * o, axis=0)


def _dilated_group_prompt(q, k, v, slopes, window, dilation):
    B, S, H, Dh = q.shape
    L = S // dilation
    span = window // dilation
    nb = -(-L // span)
    Lp = nb * span

    def to_blocks(t):
        t = jnp.moveaxis(t.reshape(B, L, dilation, H, Dh), 2, 1)
        t = jnp.pad(t, ((0, 0), (0, 0), (0, Lp - L), (0, 0), (0, 0)))
        return t.reshape(B, dilation, nb, span, H, Dh)

    def with_prev(t):
        prev = jnp.pad(t, ((0, 0), (0, 0), (1, 0), (0, 0), (0, 0), (0, 0)))[:, :, :-1]
        return jnp.concatenate([prev, t], axis=3)

    qb = to_blocks(q)
    kk = with_prev(to_blocks(k))
    vv = with_prev(to_blocks(v))
    s = jnp.einsum('brnqhe,brnkhe->brnhqk', qb, kk).astype(jnp.float32) * SCALE
    qi = jnp.arange(span)[:, None] + span
    ki = jnp.arange(2 * span)[None, :]
    dist = qi - ki
    blk_start = jnp.arange(nb)[:, None, None] * span - span
    valid = (dist >= 0) & (dist <= span) & (blk_start + ki >= 0)
    bias = -slopes.astype(jnp.float32)[:, None, None] * (dist * dilation).astype(jnp.float32)
    s = jnp.where(valid[:, None], s + bias, NEG)
    m = jnp.max(s, axis=-1, keepdims=True)
    p = jnp.exp(s - m)
    l = jnp.sum(p, axis=-1, keepdims=True)
    o = jnp.einsum('brnhqk,brnkhe->brnqhe', (p / l).astype(v.dtype), vv)
    lse = jnp.swapaxes((m + jnp.log(l))[..., 0], -1, -2)

    def from_blocks(t):
        t = t.reshape((B, dilation, Lp) + t.shape[4:])[:, :, :L]
        return jnp.moveaxis(t, 1, 2).reshape((B, S) + t.shape[3:])

    return from_blocks(o), from_blocks(lse)


def _dilated_prompt(q, k, v, slopes):
    outs, lses = [], []
    for g, (window, dilation) in enumerate(DIL_GROUPS):
        o, lse = _dilated_group_prompt(q[:, :, g], k, v, slopes[g], window, dilation)
        outs.append(o)
        lses.append(lse)
    return _mix_by_denominator(outs, lses)


def _dilated_sample(q, k, v, slopes, cache_k, cache_v):
    N, T = q.shape[:2]
    win = cache_k.shape[1]
    kk = jnp.concatenate([cache_k.astype(k.dtype), k], axis=1)
    vv = jnp.concatenate([cache_v.astype(v.dtype), v], axis=1)
    t_idx = win + jnp.arange(T)
    outs, lses = [], []
    for g, (window, dilation) in enumerate(DIL_GROUPS):
        J = window // dilation + 1
        j = jnp.arange(J)
        idx = t_idx[:, None] - j[None, :] * dilation
        valid = idx >= 0
        flat = jnp.clip(idx, 0).reshape(-1)
        kg = jnp.take(kk, flat, axis=1).reshape(N, T, J, N_KV, HEAD_DIM)
        vg = jnp.take(vv, flat, axis=1).reshape(N, T, J, N_KV, HEAD_DIM)
        s = jnp.einsum('nthe,ntjhe->nthj', q[:, :, g], kg).astype(jnp.float32) * SCALE
        s = s - slopes[g].astype(jnp.float32)[:, None] * (j * dilation).astype(jnp.float32)
        s = jnp.where(valid[None, :, None, :], s, NEG)
        lse = jax.nn.logsumexp(s, axis=-1)
        p = jnp.exp(s - lse[..., None])
        outs.append(jnp.einsum('nthj,ntjhe->nthe', p.astype(vg.dtype), vg))
        lses.append(lse)
    return _mix_by_denominator(outs, lses)


def _mem_kv(mem, g_mem, w_mem_kv):
    N, M, _ = mem.shape
    kv = _rmsnorm(mem, g_mem) @ w_mem_kv
    mk, mv = jnp.split(kv, 2, axis=-1)
    return mk.reshape(N, M, N_XHEADS, X_HEAD_DIM), mv.reshape(N, M, N_XHEADS, X_HEAD_DIM)


def _mem_attend(qx, mk, mv):
    s = jnp.einsum('nthe,nmhe->nhtm', qx, mk.astype(qx.dtype)).astype(jnp.float32) * X_SCALE
    p = jax.nn.softmax(s, axis=-1)
    return jnp.einsum('nhtm,nmhe->nthe', p.astype(mv.dtype), mv)


def _layer(x, conv_prefix, dil_attend, mem_k, mem_v, slopes,
           g_pre_mix, w_in, b_gate, conv_w, conv_b, conv_ln_g, conv_ln_b, w_conv_out,
           w_dil_o, w_x_o, w_out, g_post_mix, g_pre_ffn, w_ffn_gate, w_ffn_up, w_ffn_down, g_post_ffn):
    N, T, _ = x.shape
    h = _rmsnorm(x, g_pre_mix)
    proj = h @ w_in
    u_pre, q_d, k_d, v_d, q_x, gate_logits = jnp.split(proj, SPLITS, axis=-1)
    y_conv, conv_state = _conformer_conv(u_pre, conv_prefix, conv_w, conv_b, conv_ln_g, conv_ln_b, w_conv_out)
    q_d = q_d.reshape(N, T, N_GROUPS, N_KV, HEAD_DIM)
    k_d = k_d.reshape(N, T, N_KV, HEAD_DIM)
    v_d = v_d.reshape(N, T, N_KV, HEAD_DIM)
    y_dil = dil_attend(q_d, k_d, v_d, slopes).reshape(N, T, W_KV) @ w_dil_o
    y_mem = _mem_attend(q_x.reshape(N, T, N_XHEADS, X_HEAD_DIM), mem_k, mem_v).reshape(N, T, W_QX) @ w_x_o
    gates = jax.nn.sigmoid(gate_logits + b_gate).reshape(N, T, 3, D_MODEL)
    merged = gates[:, :, 0] * y_conv + gates[:, :, 1] * y_dil + gates[:, :, 2] * y_mem
    x = x + _rmsnorm(merged @ w_out, g_post_mix)
    h2 = _rmsnorm(x, g_pre_ffn)
    f = (jax.nn.silu(h2 @ w_ffn_gate) * (h2 @ w_ffn_up)) @ w_ffn_down
    x = x + _rmsnorm(f, g_post_ffn)
    return x, conv_state, k_d, v_d


def setup_inputs(seed: int = 0) -> dict:
    key = jax.random.key(seed)
    ks = jax.random.split(key, 32)
    win = min(WINDOW_MAX, PAST_LEN)

    def nrm(k, shape, scale):
        return jax.random.normal(k, shape, jnp.float32) * scale

    def gain(k, n):
        return 1.0 + 0.05 * jax.random.normal(k, (DEPTH, n), jnp.float32)

    return {
        'x_prompt': nrm(ks[0], (BATCH, SEQ, D_MODEL), 1.0),
        'x_sample': nrm(ks[1], (DEC_BATCH, DEC_SEQ, D_MODEL), 1.0),
        'cache_win_k': nrm(ks[2], (DEPTH, DEC_BATCH, win, N_KV, HEAD_DIM), 1.0),
        'cache_win_v': nrm(ks[3], (DEPTH, DEC_BATCH, win, N_KV, HEAD_DIM), 1.0),
        'state_conv': nrm(ks[4], (DEPTH, DEC_BATCH, CONV_WIDTH - 1, C_CONV), 0.5),
        'cache_mem_k': nrm(ks[5], (DEPTH, DEC_BATCH, N_MEM, N_XHEADS, X_HEAD_DIM), 1.0),
        'cache_mem_v': nrm(ks[6], (DEPTH, DEC_BATCH, N_MEM, N_XHEADS, X_HEAD_DIM), 1.0),
        'mem_prompt': nrm(ks[7], (BATCH, N_MEM, D_MODEL), 1.0),
        'g_pre_mix': gain(ks[8], D_MODEL),
        'w_in': nrm(ks[9], (DEPTH, D_MODEL, N_IN), D_MODEL ** -0.5),
        'b_gate': nrm(ks[10], (DEPTH, W_GATE), 0.1),
        'conv_w': nrm(ks[11], (DEPTH, CONV_WIDTH, C_CONV), CONV_WIDTH ** -0.5),
        'conv_b': nrm(ks[12], (DEPTH, C_CONV), 0.01),
        'conv_ln_g': gain(ks[13], C_CONV),
        'conv_ln_b': nrm(ks[14], (DEPTH, C_CONV), 0.01),
        'w_conv_out': nrm(ks[15], (DEPTH, C_CONV, D_MODEL), C_CONV ** -0.5),
        'w_dil_o': nrm(ks[16], (DEPTH, W_KV, D_MODEL), W_KV ** -0.5),
        'g_mem': gain(ks[17], D_MODEL),
        'w_mem_kv': nrm(ks[18], (DEPTH, D_MODEL, 2 * W_QX), D_MODEL ** -0.5),
        'w_x_o': nrm(ks[19], (DEPTH, W_QX, D_MODEL), W_QX ** -0.5),
        'w_out': nrm(ks[20], (DEPTH, D_MODEL, D_MODEL), D_MODEL ** -0.5),
        'g_post_mix': gain(ks[21], D_MODEL),
        'g_pre_ffn': gain(ks[22], D_MODEL),
        'w_ffn_gate': nrm(ks[23], (DEPTH, D_MODEL, D_FF), D_MODEL ** -0.5),
        'w_ffn_up': nrm(ks[24], (DEPTH, D_MODEL, D_FF), D_MODEL ** -0.5),
        'w_ffn_down': nrm(ks[25], (DEPTH, D_FF, D_MODEL), D_FF ** -0.5),
        'g_post_ffn': gain(ks[26], D_MODEL),
    }


def reference(x_prompt, x_sample, cache_win_k, cache_win_v, state_conv, cache_mem_k, cache_mem_v, mem_prompt,
              g_pre_mix, w_in, b_gate, conv_w, conv_b, conv_ln_g, conv_ln_b, w_conv_out, w_dil_o,
              g_mem, w_mem_kv, w_x_o, w_out, g_post_mix, g_pre_ffn, w_ffn_gate, w_ffn_up, w_ffn_down, g_post_ffn):
    slopes = _alibi_slopes()
    B, S, _ = x_prompt.shape
    win_p = min(WINDOW_MAX, S)
    xp, xs = x_prompt, x_sample
    wkp, wvp, cvp, mkp_l, mvp_l, wks, wvs, cvs = [], [], [], [], [], [], [], []
    for l in range(DEPTH):
        weights = (g_pre_mix[l], w_in[l], b_gate[l], conv_w[l], conv_b[l], conv_ln_g[l], conv_ln_b[l],
                   w_conv_out[l], w_dil_o[l], w_x_o[l], w_out[l], g_post_mix[l], g_pre_ffn[l],
                   w_ffn_gate[l], w_ffn_up[l], w_ffn_down[l], g_post_ffn[l])
        mkp, mvp = _mem_kv(mem_prompt, g_mem[l], w_mem_kv[l])
        conv_zero = jnp.zeros((B, CONV_WIDTH - 1, C_CONV), xp.dtype)
        xp, conv_p, kp, vp = _layer(xp, conv_zero, _dilated_prompt, mkp, mvp, slopes, *weights)
        dil_s = functools.partial(_dilated_sample, cache_k=cache_win_k[l], cache_v=cache_win_v[l])
        xs, conv_s, ks_new, vs_new = _layer(xs, state_conv[l], dil_s, cache_mem_k[l], cache_mem_v[l], slopes, *weights)
        wkp.append(kp[:, S - win_p:])
        wvp.append(vp[:, S - win_p:])
        cvp.append(conv_p)
        mkp_l.append(mkp)
        mvp_l.append(mvp)
        wks.append(ks_new)
        wvs.append(vs_new)
        cvs.append(conv_s)
    return (xp, xs, jnp.stack(wkp), jnp.stack(wvp), jnp.stack(cvp), jnp.stack(mkp_l), jnp.stack(mvp_l),
            jnp.stack(wks), jnp.stack(wvs), jnp.stack(cvs))
```

```python
import functools

import jax
import jax.numpy as jnp
from jax import lax
from jax.experimental import pallas as pl
from jax.experimental.pallas import tpu as pltpu

F32 = jnp.float32
BF16 = jnp.bfloat16

D_MODEL = 2048
C_CONV = D_MODEL // 2
CONV_WIDTH = 31
CONV_PREFIX = CONV_WIDTH - 1
HEAD_DIM = 128
N_KV = 4
DIL_GROUPS = ((128, 1), (512, 4), (2048, 16))
N_GROUPS = len(DIL_GROUPS)
N_QH = N_GROUPS * N_KV
SPAN = 128
N_MEM = 256
N_XHEADS = 4
W_QD = N_QH * HEAD_DIM
W_KV = N_KV * HEAD_DIM
W_QX = N_XHEADS * HEAD_DIM
D_FF = 5632
EPS = 1e-6
NEG = -1e30
SCALE = HEAD_DIM ** -0.5
LANES = 128
N_SLABS = C_CONV // LANES
MIB = 1024 * 1024

TN_IN = 512
_A_TILES = C_CONV // TN_IN
_Q_TILES = W_QD // TN_IN
_STEP_Q0 = _A_TILES
_STEP_K = _STEP_Q0 + _Q_TILES
_STEP_V = _STEP_K + 1
_STEP_QX = _STEP_V + 1
_IN_STEPS = _STEP_QX + 1
_GATE_COL0 = 2 * C_CONV + W_QD + 2 * W_KV + W_QX


def _params(semantics, vmem_mib):
    return pltpu.CompilerParams(dimension_semantics=semantics, vmem_limit_bytes=vmem_mib * MIB)


def _rms_rows(x, g):
    return x * lax.rsqrt(jnp.mean(x * x, axis=-1, keepdims=True) + EPS) * g


def _in_proj_kernel(x_ref, g_ref, w1_ref, w2_ref, h_ref, u_ref, q_ref, k_ref, v_ref, qx_ref, *, slabs):
    s = pl.program_id(1)

    @pl.when(s == 0)
    def _():
        h_ref[...] = _rms_rows(x_ref[...], g_ref[...]).astype(BF16)

    @pl.when(s < _STEP_Q0)
    def _():
        h = h_ref[...]
        a = jnp.dot(h, w1_ref[...], preferred_element_type=F32)
        b = jnp.dot(h, w2_ref[...], preferred_element_type=F32)
        u = a * jax.nn.sigmoid(b)
        if slabs:
            for c in range(TN_IN // LANES):
                u_ref[c] = u[:, c * LANES:(c + 1) * LANES]
        else:
            u_ref[...] = u

    @pl.when((s >= _STEP_Q0) & (s < _STEP_K))
    def _():
        q_ref[...] = jnp.dot(h_ref[...], w1_ref[...], preferred_element_type=F32)

    @pl.when(s == _STEP_K)
    def _():
        k_ref[...] = jnp.dot(h_ref[...], w1_ref[...], preferred_element_type=F32)

    @pl.when(s == _STEP_V)
    def _():
        v_ref[...] = jnp.dot(h_ref[...], w1_ref[...], preferred_element_type=F32)

    @pl.when(s == _STEP_QX)
    def _():
        qx_ref[...] = jnp.dot(h_ref[...], w1_ref[...], preferred_element_type=F32).astype(qx_ref.dtype)


def _in_proj(x, g, w_in, *, tm, slabs):
    m = x.shape[0]
    last_a = _A_TILES - 1
    if slabs:
        u_shape = jax.ShapeDtypeStruct((N_SLABS, m, LANES), F32)
        u_spec = pl.BlockSpec((TN_IN // LANES, tm, LANES), lambda i, s: (jnp.minimum(s, last_a), i, 0))
    else:
        u_shape = jax.ShapeDtypeStruct((m, C_CONV), F32)
        u_spec = pl.BlockSpec((tm, TN_IN), lambda i, s: (i, jnp.minimum(s, last_a)))
    return pl.pallas_call(
        functools.partial(_in_proj_kernel, slabs=slabs),
        grid=(m // tm, _IN_STEPS),
        in_specs=[
            pl.BlockSpec((tm, D_MODEL), lambda i, s: (i, 0)),
            pl.BlockSpec((1, D_MODEL), lambda i, s: (0, 0)),
            pl.BlockSpec((D_MODEL, TN_IN), lambda i, s: (0, jnp.where(s < _A_TILES, s, s + _A_TILES))),
            pl.BlockSpec((D_MODEL, TN_IN), lambda i, s: (0, jnp.minimum(s, last_a) + _A_TILES)),
        ],
        out_specs=[
            pl.BlockSpec((tm, D_MODEL), lambda i, s: (i, 0)),
            u_spec,
            pl.BlockSpec((tm, TN_IN), lambda i, s: (i, jnp.clip(s - _STEP_Q0, 0, _Q_TILES - 1))),
            pl.BlockSpec((tm, W_KV), lambda i, s: (i, 0)),
            pl.BlockSpec((tm, W_KV), lambda i, s: (i, 0)),
            pl.BlockSpec((tm, W_QX), lambda i, s: (i, 0)),
        ],
        out_shape=[
            jax.ShapeDtypeStruct((m, D_MODEL), BF16),
            u_shape,
            jax.ShapeDtypeStruct((m, W_QD), F32),
            jax.ShapeDtypeStruct((m, W_KV), F32),
            jax.ShapeDtypeStruct((m, W_KV), F32),
            jax.ShapeDtypeStruct((m, W_QX), BF16 if slabs else F32),
        ],
        compiler_params=_params(("parallel", "arbitrary"), 40),
        name="in_proj",
    )(x, g, w_in, w_in)


def _mem_kv_kernel(x_ref, g_ref, w_ref, mk_ref, mv_ref, h_ref):
    s = pl.program_id(1)

    @pl.when(s == 0)
    def _():
        h_ref[...] = _rms_rows(x_ref[...], g_ref[...]).astype(BF16)
        mk_ref[...] = jnp.dot(h_ref[...], w_ref[...], preferred_element_type=F32)

    @pl.when(s == 1)
    def _():
        mv_ref[...] = jnp.dot(h_ref[...], w_ref[...], preferred_element_type=F32)


def _mem_kv(mem, g, w, *, tm):
    m = mem.shape[0]
    return pl.pallas_call(
        _mem_kv_kernel,
        grid=(m // tm, 2),
        in_specs=[
            pl.BlockSpec((tm, D_MODEL), lambda i, s: (i, 0)),
            pl.BlockSpec((1, D_MODEL), lambda i, s: (0, 0)),
            pl.BlockSpec((D_MODEL, W_QX), lambda i, s: (0, s)),
        ],
        out_specs=[
            pl.BlockSpec((tm, W_QX), lambda i, s: (i, 0)),
            pl.BlockSpec((tm, W_QX), lambda i, s: (i, 0)),
        ],
        out_shape=[jax.ShapeDtypeStruct((m, W_QX), F32)] * 2,
        scratch_shapes=[pltpu.VMEM((tm, D_MODEL), BF16)],
        compiler_params=_params(("parallel", "arbitrary"), 32),
        name="mem_kv",
    )(mem, g, w)


def _ln_swish_slabs(y_slabs, cb_ref, lng_ref, lnb_ref, out_dtype):
    ys = [y + cb_ref[:, c * LANES:(c + 1) * LANES] for c, y in enumerate(y_slabs)]
    tot = ys[0]
    for y in ys[1:]:
        tot = tot + y
    mu = jnp.sum(tot, axis=-1, keepdims=True) * (1.0 / C_CONV)
    ds = [y - mu for y in ys]
    sq = ds[0] * ds[0]
    for d in ds[1:]:
        sq = sq + d * d
    rs = lax.rsqrt(jnp.sum(sq, axis=-1, keepdims=True) * (1.0 / C_CONV) + EPS)
    outs = []
    for c, d in enumerate(ds):
        z = d * rs * lng_ref[:, c * LANES:(c + 1) * LANES] + lnb_ref[:, c * LANES:(c + 1) * LANES]
        outs.append((z * jax.nn.sigmoid(z)).astype(out_dtype))
    return outs


def _conv_prompt_kernel(u_ref, w_ref, cb_ref, lng_ref, lnb_ref, c_ref, st_ref, head_ref, *, seq, tt):
    pad = 32
    head_ref[:, 0:pad, :] = jnp.zeros((N_SLABS, pad, LANES), F32)
    head_ref[:, pad:pad + tt, :] = u_ref[:, 0:tt, :]

    def tile(src_ref, base, t0):
        ys = []
        for c in range(N_SLABS):
            acc = jnp.zeros((tt, LANES), F32)
            for k in range(CONV_WIDTH):
                acc = acc + src_ref[c, pl.ds(base + k, tt), :] * w_ref[c, k:k + 1, :]
            ys.append(acc)
        outs = _ln_swish_slabs(ys, cb_ref, lng_ref, lnb_ref, BF16)
        for c in range(N_SLABS):
            c_ref[pl.ds(t0, tt), c * LANES:(c + 1) * LANES] = outs[c]

    tile(head_ref, pad - CONV_PREFIX, 0)

    def body(j, carry):
        t0 = pl.multiple_of(j * tt, tt)
        tile(u_ref, t0 - CONV_PREFIX, t0)
        return carry

    lax.fori_loop(1, seq // tt, body, 0)
    for c in range(N_SLABS):
        st_ref[:, c * LANES:(c + 1) * LANES] = u_ref[c, seq - CONV_PREFIX:seq, :]


def _conv_prompt(u_slabs, w_slabs, conv_b, ln_g, ln_b, *, batch, seq):
    tt = 64
    return pl.pallas_call(
        functools.partial(_conv_prompt_kernel, seq=seq, tt=tt),
        grid=(batch,),
        in_specs=[
            pl.BlockSpec((N_SLABS, seq, LANES), lambda b: (0, b, 0)),
            pl.BlockSpec((N_SLABS, CONV_WIDTH, LANES), lambda b: (0, 0, 0)),
            pl.BlockSpec((1, C_CONV), lambda b: (0, 0)),
            pl.BlockSpec((1, C_CONV), lambda b: (0, 0)),
            pl.BlockSpec((1, C_CONV), lambda b: (0, 0)),
        ],
        out_specs=[
            pl.BlockSpec((seq, C_CONV), lambda b: (b, 0)),
            pl.BlockSpec((None, None, CONV_PREFIX, C_CONV), lambda b: (0, b, 0, 0)),
        ],
        out_shape=[
            jax.ShapeDtypeStruct((batch * seq, C_CONV), BF16),
            jax.ShapeDtypeStruct((1, batch, CONV_PREFIX, C_CONV), F32),
        ],
        scratch_shapes=[pltpu.VMEM((N_SLABS, 32 + tt, LANES), F32)],
        compiler_params=_params(("parallel",), 40),
        name="conv_prompt",
    )(u_slabs, w_slabs, conv_b, ln_g, ln_b)


def _conv_sample_kernel(state_ref, u_ref, w_ref, cb_ref, lng_ref, lnb_ref, c_ref, st_ref, *, n_seq):
    w_hist = w_ref[0:CONV_PREFIX, :]
    w_new = w_ref[CONV_PREFIX:CONV_WIDTH, :]
    for n in range(n_seq):
        hist = state_ref[n]
        new = u_ref[n:n + 1, :]
        y = jnp.sum(hist * w_hist, axis=0, keepdims=True) + new * w_new
        ys = [y[:, c * LANES:(c + 1) * LANES] for c in range(N_SLABS)]
        outs = _ln_swish_slabs(ys, cb_ref, lng_ref, lnb_ref, F32)
        for c in range(N_SLABS):
            c_ref[n:n + 1, c * LANES:(c + 1) * LANES] = outs[c]
        st_ref[n, 0:CONV_PREFIX - 1, :] = state_ref[n, 1:CONV_PREFIX, :]
        st_ref[n, CONV_PREFIX - 1:CONV_PREFIX, :] = new


def _conv_sample(state, u, conv_w, conv_b, ln_g, ln_b):
    n_seq = state.shape[0]
    return pl.pallas_call(
        functools.partial(_conv_sample_kernel, n_seq=n_seq),
        out_shape=[
            jax.ShapeDtypeStruct((n_seq, C_CONV), F32),
            jax.ShapeDtypeStruct((n_seq, CONV_PREFIX, C_CONV), F32),
        ],
        compiler_params=pltpu.CompilerParams(vmem_limit_bytes=32 * MIB),
        name="conv_sample",
    )(state, u, conv_w, conv_b, ln_g, ln_b)


def _softmax_pv(s, v):
    m = jnp.max(s, axis=-1, keepdims=True)
    p = jnp.exp(s - m)
    l = jnp.sum(p, axis=-1, keepdims=True)
    o = jnp.dot(p.astype(BF16), v, preferred_element_type=F32) / l
    return o, m + jnp.log(l)


def _qk(q, k):
    return lax.dot_general(q, k, (((1,), (1,)), ((), ())), preferred_element_type=F32)


def _attn_prompt_kernel(slopes_ref, q0_ref, q1_ref, q2_ref, k_ref, v_ref, qx_ref, mk_ref, mv_ref,
                        mix_ref, xm_ref, bias_ref, o_ref, lse_ref, *, seq):
    head = pl.program_id(1)
    qi = lax.broadcasted_iota(jnp.int32, (SPAN, 2 * SPAN), 0)
    ci = lax.broadcasted_iota(jnp.int32, (SPAN, 2 * SPAN), 1)
    dist = qi + SPAN - ci
    valid = (dist >= 0) & (dist <= SPAN)
    for g, (_, dil) in enumerate(DIL_GROUPS):
        bias = -slopes_ref[g, head] * (dist * dil).astype(F32)
        bias_ref[g] = jnp.where(valid, bias, NEG)

    def rows(ref, start, size, dil):
        if dil == 1:
            return ref[pl.ds(start, size), :]
        return ref[pl.ds(start, size, stride=dil), :]

    def put(ref, g, start, dil, val):
        if dil == 1:
            ref[g, pl.ds(start, SPAN), :] = val
        else:
            ref[g, pl.ds(start, SPAN, stride=dil), :] = val

    q_refs = (q0_ref, q1_ref, q2_ref)
    for g, (window, dil) in enumerate(DIL_GROUPS):
        n_blocks = seq // dil // SPAN
        q_ref = q_refs[g]

        def unit(start, n_keys, g=g, dil=dil, q_ref=q_ref):
            k_start = start - (n_keys - SPAN) * dil
            q = rows(q_ref, start, SPAN, dil).astype(BF16)
            k = rows(k_ref, k_start, n_keys, dil).astype(BF16)
            v = rows(v_ref, k_start, n_keys, dil).astype(BF16)
            s = _qk(q, k) * SCALE + bias_ref[g, :, 2 * SPAN - n_keys:2 * SPAN]
            o, lse = _softmax_pv(s, v)
            put(o_ref, g, start, dil, o)
            put(lse_ref, g, start, dil, jnp.broadcast_to(lse, (SPAN, LANES)))

        def residue(r, carry, dil=dil, n_blocks=n_blocks, unit=unit):
            unit(r, SPAN)

            def block(n, c2):
                unit(r + n * (SPAN * dil), 2 * SPAN)
                return c2

            if n_blocks > 1:
                lax.fori_loop(1, n_blocks, block, 0)
            return carry

        lax.fori_loop(0, dil, residue, 0)

    tq = 256
    mk = mk_ref[...].astype(BF16)
    mv = mv_ref[...].astype(BF16)

    def finish(j, carry):
        t0 = pl.multiple_of(j * tq, tq)
        l0 = lse_ref[0, pl.ds(t0, tq), :]
        l1 = lse_ref[1, pl.ds(t0, tq), :]
        l2 = lse_ref[2, pl.ds(t0, tq), :]
        top = jnp.maximum(jnp.maximum(l0, l1), l2)
        w0 = jnp.exp(l0 - top)
        w1 = jnp.exp(l1 - top)
        w2 = jnp.exp(l2 - top)
        num = (w0 * o_ref[0, pl.ds(t0, tq), :] + w1 * o_ref[1, pl.ds(t0, tq), :]
               + w2 * o_ref[2, pl.ds(t0, tq), :])
        mix_ref[pl.ds(t0, tq), :] = (num / (w0 + w1 + w2)).astype(BF16)
        sx = _qk(qx_ref[pl.ds(t0, tq), :], mk) * SCALE
        ox, _ = _softmax_pv(sx, mv)
        xm_ref[pl.ds(t0, tq), :] = ox.astype(BF16)
        return carry

    lax.fori_loop(0, seq // tq, finish, 0)


def _attn_prompt(slopes, q, k, v, qx, mk, mv, *, batch, seq):
    def col(c):
        return pl.BlockSpec((None, seq, LANES), lambda b, h, c=c: (b, 0, c * N_KV + h))

    head_spec = pl.BlockSpec((None, seq, LANES), lambda b, h: (b, 0, h))
    mem_spec = pl.BlockSpec((None, N_MEM, LANES), lambda b, h: (b, 0, h))
    return pl.pallas_call(
        functools.partial(_attn_prompt_kernel, seq=seq),
        grid=(batch, N_KV),
        in_specs=[
            pl.BlockSpec(memory_space=pltpu.SMEM),
            col(0), col(1), col(2), head_spec, head_spec, head_spec, mem_spec, mem_spec,
        ],
        out_specs=[head_spec, head_spec],
        out_shape=[jax.ShapeDtypeStruct((batch, seq, W_KV), BF16),
                   jax.ShapeDtypeStruct((batch, seq, W_QX), BF16)],
        scratch_shapes=[
            pltpu.VMEM((N_GROUPS, SPAN, 2 * SPAN), F32),
            pltpu.VMEM((N_GROUPS, seq, LANES), F32),
            pltpu.VMEM((N_GROUPS, seq, LANES), F32),
        ],
        compiler_params=_params(("parallel", "parallel"), 40),
        name="attn_prompt",
    )(slopes, q, q, q, k, v, qx, mk, mv)


def _attn_sample_kernel(slopes_ref, q_ref, kn_ref, vn_ref, qx_ref, ck0_ref, ck1_ref, ck2_ref,
                        cv0_ref, cv1_ref, cv2_ref, mk_ref, mv_ref, mix_ref, xm_ref, *, bs):
    ck_refs = (ck0_ref, ck1_ref, ck2_ref)
    cv_refs = (cv0_ref, cv1_ref, cv2_ref)
    back = (SPAN - lax.broadcasted_iota(jnp.int32, (SPAN, 1), 0)).astype(F32)

    def one(n, carry):
        for h in range(N_KV):
            hs = slice(h * LANES, (h + 1) * LANES)
            kn = kn_ref[n, :, hs]
            vn = vn_ref[n, :, hs]
            scores, s_new = [], []
            for g, (_, dil) in enumerate(DIL_GROUPS):
                qs = slice(g * W_KV + h * LANES, g * W_KV + (h + 1) * LANES)
                q = q_ref[n, :, qs]
                s = jnp.sum(ck_refs[g][n, :, hs] * q, axis=-1, keepdims=True) * SCALE
                scores.append(s - slopes_ref[g, h] * (back * float(dil)))
                s_new.append(jnp.sum(kn * q, axis=-1, keepdims=True) * SCALE)
            top = s_new[0]
            for g in range(N_GROUPS):
                top = jnp.maximum(top, jnp.maximum(s_new[g], jnp.max(scores[g], axis=0, keepdims=True)))
            num = jnp.zeros((1, LANES), F32)
            den = jnp.zeros((1, 1), F32)
            for g in range(N_GROUPS):
                p = jnp.exp(scores[g] - top)
                p_new = jnp.exp(s_new[g] - top)
                num = num + jnp.sum(p * cv_refs[g][n, :, hs], axis=0, keepdims=True) + p_new * vn
                den = den + jnp.sum(p, axis=0, keepdims=True) + p_new
            mix_ref[n, :, hs] = num / den
            qx = qx_ref[n, :, hs]
            sx = jnp.sum(mk_ref[n, :, hs] * qx, axis=-1, keepdims=True) * SCALE
            px = jnp.exp(sx - jnp.max(sx, axis=0, keepdims=True))
            ox = jnp.sum(px * mv_ref[n, :, hs], axis=0, keepdims=True) / jnp.sum(px, axis=0, keepdims=True)
            xm_ref[n, :, hs] = ox
        return carry

    lax.fori_loop(0, bs, one, 0)


def _attn_sample(slopes, q, kn, vn, qx, cache_k, cache_v, mem_k, mem_v):
    n_seq, win = cache_k.shape[0], cache_k.shape[1]
    bs = 4
    row = lambda w: pl.BlockSpec((bs, 1, w), lambda i: (i, 0, 0))
    as_rows = lambda a: a.reshape(n_seq, 1, a.shape[-1])
    views_k, views_v, cache_specs = [], [], []
    for _, dil in DIL_GROUPS:
        views_k.append(cache_k.reshape(n_seq, win // dil, dil * W_KV))
        views_v.append(cache_v.reshape(n_seq, win // dil, dil * W_KV))
        cache_specs.append(pl.BlockSpec((bs, SPAN, W_KV), lambda i, nb=win // dil // SPAN: (i, nb - 1, 0)))
    mem_spec = pl.BlockSpec((bs, N_MEM, W_QX), lambda i: (i, 0, 0))
    return pl.pallas_call(
        functools.partial(_attn_sample_kernel, bs=bs),
        grid=(n_seq // bs,),
        in_specs=[pl.BlockSpec(memory_space=pltpu.SMEM), row(W_QD), row(W_KV), row(W_KV), row(W_QX)]
        + cache_specs + cache_specs + [mem_spec, mem_spec],
        out_specs=[row(W_KV), row(W_QX)],
        out_shape=[jax.ShapeDtypeStruct((n_seq, 1, W_KV), F32), jax.ShapeDtypeStruct((n_seq, 1, W_QX), F32)],
        compiler_params=_params(("parallel",), 40),
        name="attn_sample",
    )(slopes, as_rows(q), as_rows(kn), as_rows(vn), as_rows(qx), *views_k, *views_v, mem_k, mem_v)


def _merge_kernel(h_ref, c_ref, mix_ref, xm_ref, x_ref, wg0_ref, wg1_ref, wg2_ref, bg_ref, wc_ref, wd_ref,
                  wx_ref, wo_ref, gp_ref, out_ref, merged_ref, z_ref, *, n1, n2, tn, tn2):
    s = pl.program_id(1)

    @pl.when(s < n1)
    def _():
        h = h_ref[...]
        acc = None
        for br, (wg_ref, y_ref, wy_ref) in enumerate(
                ((wg0_ref, c_ref, wc_ref), (wg1_ref, mix_ref, wd_ref), (wg2_ref, xm_ref, wx_ref))):
            logit = jnp.dot(h, wg_ref[...], preferred_element_type=F32) + bg_ref[br]
            y = jnp.dot(y_ref[...].astype(BF16), wy_ref[...], preferred_element_type=F32)
            term = jax.nn.sigmoid(logit) * y
            acc = term if acc is None else acc + term
        merged_ref[s] = acc.astype(BF16)

    @pl.when(s >= n1)
    def _():
        z = None
        for j in range(n1):
            part = jnp.dot(merged_ref[j], wo_ref[j * tn:(j + 1) * tn, :], preferred_element_type=F32)
            z = part if z is None else z + part
        z_ref[s - n1] = z

    @pl.when(s == n1 + n2 - 1)
    def _():
        ss = None
        for j in range(n2):
            zj = z_ref[j]
            p = jnp.sum(zj * zj, axis=-1, keepdims=True)
            ss = p if ss is None else ss + p
        rs = lax.rsqrt(ss * (1.0 / D_MODEL) + EPS)
        for j in range(n2):
            cols = slice(j * tn2, (j + 1) * tn2)
            out_ref[:, cols] = x_ref[:, cols] + z_ref[j] * rs * gp_ref[:, cols]


def _merge(h, c, mix, xm, x, w_in, b_gate, w_conv_out, w_dil_o, w_x_o, w_out, g_post, *, tm, tn, tn2):
    m = h.shape[0]
    n1, n2 = D_MODEL // tn, D_MODEL // tn2
    gate0 = _GATE_COL0 // tn
    per_gate = D_MODEL // tn

    def gate_spec(br):
        return pl.BlockSpec((D_MODEL, tn), lambda i, s, br=br: (0, gate0 + br * per_gate + jnp.minimum(s, n1 - 1)))

    def colw(kdim):
        return pl.BlockSpec((kdim, tn), lambda i, s: (0, jnp.minimum(s, n1 - 1)))

    def rowblk(w):
        return pl.BlockSpec((tm, w), lambda i, s: (i, 0))

    b3 = b_gate.reshape(3, 1, D_MODEL)
    return pl.pallas_call(
        functools.partial(_merge_kernel, n1=n1, n2=n2, tn=tn, tn2=tn2),
        grid=(m // tm, n1 + n2),
        in_specs=[
            rowblk(D_MODEL), rowblk(C_CONV), rowblk(W_KV), rowblk(W_QX), rowblk(D_MODEL),
            gate_spec(0), gate_spec(1), gate_spec(2),
            pl.BlockSpec((3, 1, tn), lambda i, s: (0, 0, jnp.minimum(s, n1 - 1))),
            colw(C_CONV), colw(W_KV), colw(W_QX),
            pl.BlockSpec((D_MODEL, tn2), lambda i, s: (0, jnp.maximum(s - n1, 0))),
            pl.BlockSpec((1, D_MODEL), lambda i, s: (0, 0)),
        ],
        out_specs=rowblk(D_MODEL),
        out_shape=jax.ShapeDtypeStruct((m, D_MODEL), F32),
        scratch_shapes=[pltpu.VMEM((n1, tm, tn), BF16), pltpu.VMEM((n2, tm, tn2), F32)],
        compiler_params=_params(("parallel", "arbitrary"), 52),
        name="merge",
    )(h, c, mix, xm, x, w_in, w_in, w_in, b3, w_conv_out, w_dil_o, w_x_o, w_out, g_post)


def _ffn_kernel(x_ref, gpre_ref, wg_ref, wu_ref, wd_ref, gpost_ref, out_ref, h_ref, acc_ref, *, n_steps):
    s = pl.program_id(1)

    @pl.when(s == 0)
    def _():
        h_ref[...] = _rms_rows(x_ref[...], gpre_ref[...]).astype(BF16)

    h = h_ref[...]
    gate = jnp.dot(h, wg_ref[...], preferred_element_type=F32)
    up = jnp.dot(h, wu_ref[...], preferred_element_type=F32)
    act = (gate * jax.nn.sigmoid(gate) * up).astype(BF16)
    part = jnp.dot(act, wd_ref[...], preferred_element_type=F32)

    @pl.when(s == 0)
    def _():
        acc_ref[...] = part

    @pl.when(s > 0)
    def _():
        acc_ref[...] += part

    @pl.when(s == n_steps - 1)
    def _():
        out_ref[...] = x_ref[...] + _rms_rows(acc_ref[...], gpost_ref[...])


def _ffn(x, g_pre, w_gate, w_up, w_down, g_post, *, tm, tf):
    m = x.shape[0]
    n_steps = D_FF // tf
    return pl.pallas_call(
        functools.partial(_ffn_kernel, n_steps=n_steps),
        grid=(m // tm, n_steps),
        in_specs=[
            pl.BlockSpec((tm, D_MODEL), lambda i, s: (i, 0)),
            pl.BlockSpec((1, D_MODEL), lambda i, s: (0, 0)),
            pl.BlockSpec((D_MODEL, tf), lambda i, s: (0, s)),
            pl.BlockSpec((D_MODEL, tf), lambda i, s: (0, s)),
            pl.BlockSpec((tf, D_MODEL), lambda i, s: (s, 0)),
            pl.BlockSpec((1, D_MODEL), lambda i, s: (0, 0)),
        ],
        out_specs=pl.BlockSpec((tm, D_MODEL), lambda i, s: (i, 0)),
        out_shape=jax.ShapeDtypeStruct((m, D_MODEL), F32),
        scratch_shapes=[pltpu.VMEM((tm, D_MODEL), BF16), pltpu.VMEM((tm, D_MODEL), F32)],
        compiler_params=_params(("parallel", "arbitrary"), 48),
        name="ffn",
    )(x, g_pre, w_gate, w_up, w_down, g_post)


def kernel(x_prompt, x_sample, cache_win_k, cache_win_v, state_conv, cache_mem_k, cache_mem_v, mem_prompt,
           g_pre_mix, w_in, b_gate, conv_w, conv_b, conv_ln_g, conv_ln_b, w_conv_out, w_dil_o, g_mem, w_mem_kv,
           w_x_o, w_out, g_post_mix, g_pre_ffn, w_ffn_gate, w_ffn_up, w_ffn_down, g_post_ffn):
    batch, seq, _ = x_prompt.shape
    n_seq = x_sample.shape[0]
    depth = w_in.shape[0]
    assert depth == 1 and x_sample.shape[1] == 1 and seq % (DIL_GROUPS[-1][1] * SPAN) == 0
    assert cache_win_k.shape[2] == DIL_GROUPS[-1][0]

    idx = jnp.arange(1, N_QH + 1, dtype=F32)
    slopes = jnp.exp2(-8.0 * idx / N_QH).reshape(N_GROUPS, N_KV)

    xp, xs = x_prompt.reshape(batch * seq, D_MODEL), x_sample.reshape(n_seq, D_MODEL)
    l = 0
    w_in_b = w_in[l].astype(BF16)
    w_conv_out_b = w_conv_out[l].astype(BF16)
    w_dil_o_b = w_dil_o[l].astype(BF16)
    w_x_o_b = w_x_o[l].astype(BF16)
    w_out_b = w_out[l].astype(BF16)
    w_mem_kv_b = w_mem_kv[l].astype(BF16)
    w_gate_b = w_ffn_gate[l].astype(BF16)
    w_up_b = w_ffn_up[l].astype(BF16)
    w_down_b = w_ffn_down[l].astype(BF16)
    row = lambda a: a[l].reshape(1, -1)
    g_pre, g_post, g_ffn_pre, g_ffn_post = row(g_pre_mix), row(g_post_mix), row(g_pre_ffn), row(g_post_ffn)
    cb, lng, lnb = row(conv_b), row(conv_ln_g), row(conv_ln_b)
    conv_w_slabs = conv_w[l].reshape(CONV_WIDTH, N_SLABS, LANES).transpose(1, 0, 2)

    mk, mv = _mem_kv(mem_prompt.reshape(batch * N_MEM, D_MODEL), row(g_mem), w_mem_kv_b, tm=512)
    h_p, u_p, q_p, k_p, v_p, qx_p = _in_proj(xp, g_pre, w_in_b, tm=512, slabs=True)
    c_p, conv_p = _conv_prompt(u_p, conv_w_slabs, cb, lng, lnb, batch=batch, seq=seq)
    mix_p, xm_p = _attn_prompt(slopes, q_p.reshape(batch, seq, W_QD), k_p.reshape(batch, seq, W_KV),
                               v_p.reshape(batch, seq, W_KV), qx_p.reshape(batch, seq, W_QX),
                               mk.reshape(batch, N_MEM, W_QX), mv.reshape(batch, N_MEM, W_QX),
                               batch=batch, seq=seq)
    x1_p = _merge(h_p, c_p, mix_p.reshape(batch * seq, W_KV), xm_p.reshape(batch * seq, W_QX), xp, w_in_b,
                  b_gate[l], w_conv_out_b, w_dil_o_b, w_x_o_b, w_out_b, g_post, tm=512, tn=256, tn2=512)
    y_p = _ffn(x1_p, g_ffn_pre, w_gate_b, w_up_b, w_down_b, g_ffn_post, tm=512, tf=512)

    h_s, u_s, q_s, k_s, v_s, qx_s = _in_proj(xs, g_pre, w_in_b, tm=n_seq, slabs=False)
    c_s, conv_s = _conv_sample(state_conv[l], u_s, conv_w[l], cb, lng, lnb)
    mix_s, xm_s = _attn_sample(slopes, q_s, k_s, v_s, qx_s,
                               cache_win_k[l].reshape(n_seq, -1, W_KV), cache_win_v[l].reshape(n_seq, -1, W_KV),
                               cache_mem_k[l].reshape(n_seq, N_MEM, W_QX), cache_mem_v[l].reshape(n_seq, N_MEM, W_QX))
    x1_s = _merge(h_s, c_s, mix_s.reshape(n_seq, W_KV), xm_s.reshape(n_seq, W_QX), xs, w_in_b, b_gate[l], w_conv_out_b, w_dil_o_b, w_x_o_b, w_out_b,
                  g_post, tm=n_seq, tn=256, tn2=512)
    y_s = _ffn(x1_s, g_ffn_pre, w_gate_b, w_up_b, w_down_b, g_ffn_post, tm=n_seq, tf=512)

    win = min(DIL_GROUPS[-1][0], seq)
    kv_shape = (1, batch, seq, N_KV, HEAD_DIM)
    mem_shape = (1, batch, N_MEM, N_XHEADS, HEAD_DIM)
    new_shape = (1, n_seq, 1, N_KV, HEAD_DIM)
    return (y_p.reshape(batch, seq, D_MODEL), y_s.reshape(n_seq, 1, D_MODEL),
            k_p.reshape(kv_shape)[:, :, seq - win:], v_p.reshape(kv_shape)[:, :, seq - win:], conv_p,
            mk.reshape(mem_shape), mv.reshape(mem_shape),
            k_s.reshape(new_shape), v_s.reshape(new_shape), conv_s.reshape(1, n_seq, CONV_PREFIX, C_CONV))
```

```python
import functools

import jax
import jax.numpy as jnp
from jax import lax
from jax.experimental import pallas as pl
from jax.experimental.pallas import tpu as pltpu

F32 = jnp.float32
BF16 = jnp.bfloat16

D_MODEL = 2048
C_CONV = D_MODEL // 2
CONV_WIDTH = 31
CONV_PREFIX = CONV_WIDTH - 1
HEAD_DIM = 128
N_KV = 4
DIL_GROUPS = ((128, 1), (512, 4), (2048, 16))
N_GROUPS = len(DIL_GROUPS)
N_QH = N_GROUPS * N_KV
SPAN = 128
N_MEM = 256
N_XHEADS = 4
W_QD = N_QH * HEAD_DIM
W_KV = N_KV * HEAD_DIM
W_QX = N_XHEADS * HEAD_DIM
D_FF = 5632
EPS = 1e-6
NEG = -1e30
SCALE = HEAD_DIM ** -0.5
LANES = 128
N_SLABS = C_CONV // LANES
MIB = 1024 * 1024

TN_IN = 512
_A_TILES = C_CONV // TN_IN
_Q_TILES = W_QD // TN_IN
_STEP_Q0 = _A_TILES
_STEP_K = _STEP_Q0 + _Q_TILES
_STEP_V = _STEP_K + 1
_STEP_QX = _STEP_V + 1
_IN_STEPS = _STEP_QX + 1
_GATE_COL0 = 2 * C_CONV + W_QD + 2 * W_KV + W_QX


def _params(semantics, vmem_mib):
    return pltpu.CompilerParams(dimension_semantics=semantics, vmem_limit_bytes=vmem_mib * MIB)


def _rms_rows(x, g):
    return x * lax.rsqrt(jnp.mean(x * x, axis=-1, keepdims=True) + EPS) * g


def _in_proj_kernel(x_ref, g_ref, w1_ref, w2_ref, h_ref, u_ref, q_ref, k_ref, v_ref, qx_ref, *, slabs):
    s = pl.program_id(1)

    @pl.when(s == 0)
    def _():
        h_ref[...] = _rms_rows(x_ref[...], g_ref[...]).astype(BF16)

    @pl.when(s < _STEP_Q0)
    def _():
        h = h_ref[...]
        a = jnp.dot(h, w1_ref[...], preferred_element_type=F32)
        b = jnp.dot(h, w2_ref[...], preferred_element_type=F32)
        u = a * jax.nn.sigmoid(b)
        if slabs:
            for c in range(TN_IN // LANES):
                u_ref[c] = u[:, c * LANES:(c + 1) * LANES]
        else:
            u_ref[...] = u

    @pl.when((s >= _STEP_Q0) & (s < _STEP_K))
    def _():
        q_ref[...] = jnp.dot(h_ref[...], w1_ref[...], preferred_element_type=F32)

    @pl.when(s == _STEP_K)
    def _():
        k_ref[...] = jnp.dot(h_ref[...], w1_ref[...], preferred_element_type=F32)

    @pl.when(s == _STEP_V)
    def _():
        v_ref[...] = jnp.dot(h_ref[...], w1_ref[...], preferred_element_type=F32)

    @pl.when(s == _STEP_QX)
    def _():
        qx_ref[...] = jnp.dot(h_ref[...], w1_ref[...], preferred_element_type=F32).astype(qx_ref.dtype)


def _in_proj(x, g, w_in, *, tm, slabs):
    m = x.shape[0]
    last_a = _A_TILES - 1
    if slabs:
        u_shape = jax.ShapeDtypeStruct((N_SLABS, m, LANES), F32)
        u_spec = pl.BlockSpec((TN_IN // LANES, tm, LANES), lambda i, s: (jnp.minimum(s, last_a), i, 0))
    else:
        u_shape = jax.ShapeDtypeStruct((m, C_CONV), F32)
        u_spec = pl.BlockSpec((tm, TN_IN), lambda i, s: (i, jnp.minimum(s, last_a)))
    return pl.pallas_call(
        functools.partial(_in_proj_kernel, slabs=slabs),
        grid=(m // tm, _IN_STEPS),
        in_specs=[
            pl.BlockSpec((tm, D_MODEL), lambda i, s: (i, 0)),
            pl.BlockSpec((1, D_MODEL), lambda i, s: (0, 0)),
            pl.BlockSpec((D_MODEL, TN_IN), lambda i, s: (0, jnp.where(s < _A_TILES, s, s + _A_TILES))),
            pl.BlockSpec((D_MODEL, TN_IN), lambda i, s: (0, jnp.minimum(s, last_a) + _A_TILES)),
        ],
        out_specs=[
            pl.BlockSpec((tm, D_MODEL), lambda i, s: (i, 0)),
            u_spec,
            pl.BlockSpec((tm, TN_IN), lambda i, s: (i, jnp.clip(s - _STEP_Q0, 0, _Q_TILES - 1))),
            pl.BlockSpec((tm, W_KV), lambda i, s: (i, 0)),
            pl.BlockSpec((tm, W_KV), lambda i, s: (i, 0)),
            pl.BlockSpec((tm, W_QX), lambda i, s: (i, 0)),
        ],
        out_shape=[
            jax.ShapeDtypeStruct((m, D_MODEL), BF16),
            u_shape,
            jax.ShapeDtypeStruct((m, W_QD), F32),
            jax.ShapeDtypeStruct((m, W_KV), F32),
            jax.ShapeDtypeStruct((m, W_KV), F32),
            jax.ShapeDtypeStruct((m, W_QX), BF16 if slabs else F32),
        ],
        compiler_params=_params(("parallel", "arbitrary"), 40),
        name="in_proj",
    )(x, g, w_in, w_in)


def _mem_kv_kernel(x_ref, g_ref, w_ref, mk_ref, mv_ref, h_ref):
    s = pl.program_id(1)

    @pl.when(s == 0)
    def _():
        h_ref[...] = _rms_rows(x_ref[...], g_ref[...]).astype(BF16)
        mk_ref[...] = jnp.dot(h_ref[...], w_ref[...], preferred_element_type=F32)

    @pl.when(s == 1)
    def _():
        mv_ref[...] = jnp.dot(h_ref[...], w_ref[...], preferred_element_type=F32)


def _mem_kv(mem, g, w, *, tm):
    m = mem.shape[0]
    return pl.pallas_call(
        _mem_kv_kernel,
        grid=(m // tm, 2),
        in_specs=[
            pl.BlockSpec((tm, D_MODEL), lambda i, s: (i, 0)),
            pl.BlockSpec((1, D_MODEL), lambda i, s: (0, 0)),
            pl.BlockSpec((D_MODEL, W_QX), lambda i, s: (0, s)),
        ],
        out_specs=[
            pl.BlockSpec((tm, W_QX), lambda i, s: (i, 0)),
            pl.BlockSpec((tm, W_QX), lambda i, s: (i, 0)),
        ],
        out_shape=[jax.ShapeDtypeStruct((m, W_QX), F32)] * 2,
        scratch_shapes=[pltpu.VMEM((tm, D_MODEL), BF16)],
        compiler_params=_params(("parallel", "arbitrary"), 32),
        name="mem_kv",
    )(mem, g, w)


def _ln_swish_slabs(y_slabs, cb_ref, lng_ref, lnb_ref, out_dtype):
    ys = [y + cb_ref[:, c * LANES:(c + 1) * LANES] for c, y in enumerate(y_slabs)]
    tot = ys[0]
    for y in ys[1:]:
        tot = tot + y
    mu = jnp.sum(tot, axis=-1, keepdims=True) * (1.0 / C_CONV)
    ds = [y - mu for y in ys]
    sq = ds[0] * ds[0]
    for d in ds[1:]:
        sq = sq + d * d
    rs = lax.rsqrt(jnp.sum(sq, axis=-1, keepdims=True) * (1.0 / C_CONV) + EPS)
    outs = []
    for c, d in enumerate(ds):
        z = d * rs * lng_ref[:, c * LANES:(c + 1) * LANES] + lnb_ref[:, c * LANES:(c + 1) * LANES]
        outs.append((z * jax.nn.sigmoid(z)).astype(out_dtype))
    return outs


def _conv_prompt_kernel(u_ref, w_ref, cb_ref, lng_ref, lnb_ref, c_ref, st_ref, head_ref, *, seq, tt):
    pad = 32
    head_ref[:, 0:pad, :] = jnp.zeros((N_SLABS, pad, LANES), F32)
    head_ref[:, pad:pad + tt, :] = u_ref[:, 0:tt, :]

    def tile(src_ref, base, t0):
        ys = []
        for c in range(N_SLABS):
            acc = jnp.zeros((tt, LANES), F32)
            for k in range(CONV_WIDTH):
                acc = acc + src_ref[c, pl.ds(base + k, tt), :] * w_ref[c, k:k + 1, :]
            ys.append(acc)
        outs = _ln_swish_slabs(ys, cb_ref, lng_ref, lnb_ref, BF16)
        for c in range(N_SLABS):
            c_ref[pl.ds(t0, tt), c * LANES:(c + 1) * LANES] = outs[c]

    tile(head_ref, pad - CONV_PREFIX, 0)

    def body(j, carry):
        t0 = pl.multiple_of(j * tt, tt)
        tile(u_ref, t0 - CONV_PREFIX, t0)
        return carry

    lax.fori_loop(1, seq // tt, body, 0)
    for c in range(N_SLABS):
        st_ref[:, c * LANES:(c + 1) * LANES] = u_ref[c, seq - CONV_PREFIX:seq, :]


def _conv_prompt(u_slabs, w_slabs, conv_b, ln_g, ln_b, *, batch, seq):
    tt = 64
    return pl.pallas_call(
        functools.partial(_conv_prompt_kernel, seq=seq, tt=tt),
        grid=(batch,),
        in_specs=[
            pl.BlockSpec((N_SLABS, seq, LANES), lambda b: (0, b, 0)),
            pl.BlockSpec((N_SLABS, CONV_WIDTH, LANES), lambda b: (0, 0, 0)),
            pl.BlockSpec((1, C_CONV), lambda b: (0, 0)),
            pl.BlockSpec((1, C_CONV), lambda b: (0, 0)),
            pl.BlockSpec((1, C_CONV), lambda b: (0, 0)),
        ],
        out_specs=[
            pl.BlockSpec((seq, C_CONV), lambda b: (b, 0)),
            pl.BlockSpec((None, None, CONV_PREFIX, C_CONV), lambda b: (0, b, 0, 0)),
        ],
        out_shape=[
            jax.ShapeDtypeStruct((batch * seq, C_CONV), BF16),
            jax.ShapeDtypeStruct((1, batch, CONV_PREFIX, C_CONV), F32),
        ],
        scratch_shapes=[pltpu.VMEM((N_SLABS, 32 + tt, LANES), F32)],
        compiler_params=_params(("parallel",), 40),
        name="conv_prompt",
    )(u_slabs, w_slabs, conv_b, ln_g, ln_b)


def _conv_sample_kernel(state_ref, u_ref, w_ref, cb_ref, lng_ref, lnb_ref, c_ref, st_ref, *, n_seq):
    w_hist = w_ref[0:CONV_PREFIX, :]
    w_new = w_ref[CONV_PREFIX:CONV_WIDTH, :]
    for n in range(n_seq):
        hist = state_ref[n]
        new = u_ref[n:n + 1, :]
        y = jnp.sum(hist * w_hist, axis=0, keepdims=True) + new * w_new
        ys = [y[:, c * LANES:(c + 1) * LANES] for c in range(N_SLABS)]
        outs = _ln_swish_slabs(ys, cb_ref, lng_ref, lnb_ref, F32)
        for c in range(N_SLABS):
            c_ref[n:n + 1, c * LANES:(c + 1) * LANES] = outs[c]
        st_ref[n, 0:CONV_PREFIX - 1, :] = state_ref[n, 1:CONV_PREFIX, :]
        st_ref[n, CONV_PREFIX - 1:CONV_PREFIX, :] = new


def _conv_sample(state, u, conv_w, conv_b, ln_g, ln_b):
    n_seq = state.shape[0]
    return pl.pallas_call(
        functools.partial(_conv_sample_kernel, n_seq=n_seq),
        out_shape=[
            jax.ShapeDtypeStruct((n_seq, C_CONV), F32),
            jax.ShapeDtypeStruct((n_seq, CONV_PREFIX, C_CONV), F32),
        ],
        compiler_params=pltpu.CompilerParams(vmem_limit_bytes=32 * MIB),
        name="conv_sample",
    )(state, u, conv_w, conv_b, ln_g, ln_b)


def _softmax_pv(s, v):
    m = jnp.max(s, axis=-1, keepdims=True)
    p = jnp.exp(s - m)
    l = jnp.sum(p, axis=-1, keepdims=True)
    o = jnp.dot(p.astype(BF16), v, preferred_element_type=F32) / l
    return o, m + jnp.log(l)


def _qk(q, k):
    return lax.dot_general(q, k, (((1,), (1,)), ((), ())), preferred_element_type=F32)


def _attn_prompt_kernel(slopes_ref, q0_ref, q1_ref, q2_ref, k_ref, v_ref, qx_ref, mk_ref, mv_ref,
                        mix_ref, xm_ref, bias_ref, o_ref, lse_ref, *, seq):
    head = pl.program_id(1)
    qi = lax.broadcasted_iota(jnp.int32, (SPAN, 2 * SPAN), 0)
    ci = lax.broadcasted_iota(jnp.int32, (SPAN, 2 * SPAN), 1)
    dist = qi + SPAN - ci
    valid = (dist >= 0) & (dist <= SPAN)
    for g, (_, dil) in enumerate(DIL_GROUPS):
        bias = -slopes_ref[g, head] * (dist * dil).astype(F32)
        bias_ref[g] = jnp.where(valid, bias, NEG)

    def rows(ref, start, size, dil):
        if dil == 1:
            return ref[pl.ds(start, size), :]
        return ref[pl.ds(start, size, stride=dil), :]

    def put(ref, g, start, dil, val):
        if dil == 1:
            ref[g, pl.ds(start, SPAN), :] = val
        else:
            ref[g, pl.ds(start, SPAN, stride=dil), :] = val

    q_refs = (q0_ref, q1_ref, q2_ref)
    for g, (window, dil) in enumerate(DIL_GROUPS):
        n_blocks = seq // dil // SPAN
        q_ref = q_refs[g]

        def unit(start, n_keys, g=g, dil=dil, q_ref=q_ref):
            k_start = start - (n_keys - SPAN) * dil
            q = rows(q_ref, start, SPAN, dil).astype(BF16)
            k = rows(k_ref, k_start, n_keys, dil).astype(BF16)
            v = rows(v_ref, k_start, n_keys, dil).astype(BF16)
            s = _qk(q, k) * SCALE + bias_ref[g, :, 2 * SPAN - n_keys:2 * SPAN]
            o, lse = _softmax_pv(s, v)
            put(o_ref, g, start, dil, o)
            put(lse_ref, g, start, dil, jnp.broadcast_to(lse, (SPAN, LANES)))

        def residue(r, carry, dil=dil, n_blocks=n_blocks, unit=unit):
            unit(r, SPAN)

            def block(n, c2):
                unit(r + n * (SPAN * dil), 2 * SPAN)
                return c2

            if n_blocks > 1:
                lax.fori_loop(1, n_blocks, block, 0)
            return carry

        lax.fori_loop(0, dil, residue, 0)

    tq = 256
    mk = mk_ref[...].astype(BF16)
    mv = mv_ref[...].astype(BF16)

    def finish(j, carry):
        t0 = pl.multiple_of(j * tq, tq)
        l0 = lse_ref[0, pl.ds(t0, tq), :]
        l1 = lse_ref[1, pl.ds(t0, tq), :]
        l2 = lse_ref[2, pl.ds(t0, tq), :]
        top = jnp.maximum(jnp.maximum(l0, l1), l2)
        w0 = jnp.exp(l0 - top)
        w1 = jnp.exp(l1 - top)
        w2 = jnp.exp(l2 - top)
        num = (w0 * o_ref[0, pl.ds(t0, tq), :] + w1 * o_ref[1, pl.ds(t0, tq), :]
               + w2 * o_ref[2, pl.ds(t0, tq), :])
        mix_ref[pl.ds(t0, tq), :] = (num / (w0 + w1 + w2)).astype(BF16)
        sx = _qk(qx_ref[pl.ds(t0, tq), :], mk) * SCALE
        ox, _ = _softmax_pv(sx, mv)
        xm_ref[pl.ds(t0, tq), :] = ox.astype(BF16)
        return carry

    lax.fori_loop(0, seq // tq, finish, 0)


def _attn_prompt(slopes, q, k, v, qx, mk, mv, *, batch, seq):
    def col(c):
        return pl.BlockSpec((None, seq, LANES), lambda b, h, c=c: (b, 0, c * N_KV + h))

    head_spec = pl.BlockSpec((None, seq, LANES), lambda b, h: (b, 0, h))
    mem_spec = pl.BlockSpec((None, N_MEM, LANES), lambda b, h: (b, 0, h))
    return pl.pallas_call(
        functools.partial(_attn_prompt_kernel, seq=seq),
        grid=(batch, N_KV),
        in_specs=[
            pl.BlockSpec(memory_space=pltpu.SMEM),
            col(0), col(1), col(2), head_spec, head_spec, head_spec, mem_spec, mem_spec,
        ],
        out_specs=[head_spec, head_spec],
        out_shape=[jax.ShapeDtypeStruct((batch, seq, W_KV), BF16),
                   jax.ShapeDtypeStruct((batch, seq, W_QX), BF16)],
        scratch_shapes=[
            pltpu.VMEM((N_GROUPS, SPAN, 2 * SPAN), F32),
            pltpu.VMEM((N_GROUPS, seq, LANES), F32),
            pltpu.VMEM((N_GROUPS, seq, LANES), F32),
        ],
        compiler_params=_params(("parallel", "parallel"), 40),
        name="attn_prompt",
    )(slopes, q, q, q, k, v, qx, mk, mv)


def _attn_sample_kernel(slopes_ref, q_ref, kn_ref, vn_ref, qx_ref, ck0_ref, ck1_ref, ck2_ref,
                        cv0_ref, cv1_ref, cv2_ref, mk_ref, mv_ref, mix_ref, xm_ref, *, bs):
    ck_refs = (ck0_ref, ck1_ref, ck2_ref)
    cv_refs = (cv0_ref, cv1_ref, cv2_ref)
    back = (SPAN - lax.broadcasted_iota(jnp.int32, (SPAN, 1), 0)).astype(F32)

    def one(n, carry):
        for h in range(N_KV):
            hs = slice(h * LANES, (h + 1) * LANES)
            kn = kn_ref[n, :, hs]
            vn = vn_ref[n, :, hs]
            scores, s_new = [], []
            for g, (_, dil) in enumerate(DIL_GROUPS):
                qs = slice(g * W_KV + h * LANES, g * W_KV + (h + 1) * LANES)
                q = q_ref[n, :, qs]
                s = jnp.sum(ck_refs[g][n, :, h, :] * q, axis=-1, keepdims=True) * SCALE
                scores.append(s - slopes_ref[g, h] * (back * float(dil)))
                s_new.append(jnp.sum(kn * q, axis=-1, keepdims=True) * SCALE)
            top = s_new[0]
            for g in range(N_GROUPS):
                top = jnp.maximum(top, jnp.maximum(s_new[g], jnp.max(scores[g], axis=0, keepdims=True)))
            num = jnp.zeros((1, LANES), F32)
            den = jnp.zeros((1, 1), F32)
            for g in range(N_GROUPS):
                p = jnp.exp(scores[g] - top)
                p_new = jnp.exp(s_new[g] - top)
                num = num + jnp.sum(p * cv_refs[g][n, :, h, :], axis=0, keepdims=True) + p_new * vn
                den = den + jnp.sum(p, axis=0, keepdims=True) + p_new
            mix_ref[n, :, hs] = num / den
            qx = qx_ref[n, :, hs]
            sx = jnp.sum(mk_ref[n, :, h, :] * qx, axis=-1, keepdims=True) * SCALE
            px = jnp.exp(sx - jnp.max(sx, axis=0, keepdims=True))
            ox = jnp.sum(px * mv_ref[n, :, h, :], axis=0, keepdims=True) / jnp.sum(px, axis=0, keepdims=True)
            xm_ref[n, :, hs] = ox
        return carry

    lax.fori_loop(0, bs, one, 0)


def _attn_sample(slopes, q, kn, vn, qx, cache_k, cache_v, mem_k, mem_v):
    n_seq, win = cache_k.shape[0], cache_k.shape[1]
    bs = 2
    sublanes = 8
    row = lambda w: pl.BlockSpec((bs, 1, w), lambda i: (i, 0, 0))
    as_rows = lambda a: a.reshape(n_seq, 1, a.shape[-1])
    views_k, views_v, cache_specs = [], [], []
    for _, dil in DIL_GROUPS:
        views_k.append(cache_k.reshape(n_seq, win // dil, dil * N_KV, HEAD_DIM))
        views_v.append(cache_v.reshape(n_seq, win // dil, dil * N_KV, HEAD_DIM))
        rows = N_KV if dil == 1 else sublanes
        cache_specs.append(pl.BlockSpec((bs, SPAN, rows, HEAD_DIM),
                                        lambda i, nb=win // dil // SPAN: (i, nb - 1, 0, 0)))
    mem_spec = pl.BlockSpec((bs, N_MEM, N_XHEADS, HEAD_DIM), lambda i: (i, 0, 0, 0))
    return pl.pallas_call(
        functools.partial(_attn_sample_kernel, bs=bs),
        grid=(n_seq // bs,),
        in_specs=[pl.BlockSpec(memory_space=pltpu.SMEM), row(W_QD), row(W_KV), row(W_KV), row(W_QX)]
        + cache_specs + cache_specs + [mem_spec, mem_spec],
        out_specs=[row(W_KV), row(W_QX)],
        out_shape=[jax.ShapeDtypeStruct((n_seq, 1, W_KV), F32), jax.ShapeDtypeStruct((n_seq, 1, W_QX), F32)],
        compiler_params=_params(("parallel",), 40),
        name="attn_sample",
    )(slopes, as_rows(q), as_rows(kn), as_rows(vn), as_rows(qx), *views_k, *views_v, mem_k, mem_v)


def _merge_kernel(h_ref, c_ref, mix_ref, xm_ref, x_ref, wg0_ref, wg1_ref, wg2_ref, bg_ref, wc_ref, wd_ref,
                  wx_ref, wo_ref, gp_ref, out_ref, merged_ref, z_ref, *, n1, n2, tn, tn2):
    s = pl.program_id(1)

    @pl.when(s < n1)
    def _():
        h = h_ref[...]
        acc = None
        for br, (wg_ref, y_ref, wy_ref) in enumerate(
                ((wg0_ref, c_ref, wc_ref), (wg1_ref, mix_ref, wd_ref), (wg2_ref, xm_ref, wx_ref))):
            logit = jnp.dot(h, wg_ref[...], preferred_element_type=F32) + bg_ref[br]
            y = jnp.dot(y_ref[...].astype(BF16), wy_ref[...], preferred_element_type=F32)
            term = jax.nn.sigmoid(logit) * y
            acc = term if acc is None else acc + term
        merged_ref[s] = acc.astype(BF16)

    @pl.when(s >= n1)
    def _():
        z = None
        for j in range(n1):
            part = jnp.dot(merged_ref[j], wo_ref[j * tn:(j + 1) * tn, :], preferred_element_type=F32)
            z = part if z is None else z + part
        z_ref[s - n1] = z

    @pl.when(s == n1 + n2 - 1)
    def _():
        ss = None
        for j in range(n2):
            zj = z_ref[j]
            p = jnp.sum(zj * zj, axis=-1, keepdims=True)
            ss = p if ss is None else ss + p
        rs = lax.rsqrt(ss * (1.0 / D_MODEL) + EPS)
        for j in range(n2):
            cols = slice(j * tn2, (j + 1) * tn2)
            out_ref[:, cols] = x_ref[:, cols] + z_ref[j] * rs * gp_ref[:, cols]


def _merge(h, c, mix, xm, x, w_in, b_gate, w_conv_out, w_dil_o, w_x_o, w_out, g_post, *, tm, tn, tn2):
    m = h.shape[0]
    n1, n2 = D_MODEL // tn, D_MODEL // tn2
    gate0 = _GATE_COL0 // tn
    per_gate = D_MODEL // tn

    def gate_spec(br):
        return pl.BlockSpec((D_MODEL, tn), lambda i, s, br=br: (0, gate0 + br * per_gate + jnp.minimum(s, n1 - 1)))

    def colw(kdim):
        return pl.BlockSpec((kdim, tn), lambda i, s: (0, jnp.minimum(s, n1 - 1)))

    def rowblk(w):
        return pl.BlockSpec((tm, w), lambda i, s: (i, 0))

    b3 = b_gate.reshape(3, 1, D_MODEL)
    return pl.pallas_call(
        functools.partial(_merge_kernel, n1=n1, n2=n2, tn=tn, tn2=tn2),
        grid=(m // tm, n1 + n2),
        in_specs=[
            rowblk(D_MODEL), rowblk(C_CONV), rowblk(W_KV), rowblk(W_QX), rowblk(D_MODEL),
            gate_spec(0), gate_spec(1), gate_spec(2),
            pl.BlockSpec((3, 1, tn), lambda i, s: (0, 0, jnp.minimum(s, n1 - 1))),
            colw(C_CONV), colw(W_KV), colw(W_QX),
            pl.BlockSpec((D_MODEL, tn2), lambda i, s: (0, jnp.maximum(s - n1, 0))),
            pl.BlockSpec((1, D_MODEL), lambda i, s: (0, 0)),
        ],
        out_specs=rowblk(D_MODEL),
        out_shape=jax.ShapeDtypeStruct((m, D_MODEL), F32),
        scratch_shapes=[pltpu.VMEM((n1, tm, tn), BF16), pltpu.VMEM((n2, tm, tn2), F32)],
        compiler_params=_params(("parallel", "arbitrary"), 52),
        name="merge",
    )(h, c, mix, xm, x, w_in, w_in, w_in, b3, w_conv_out, w_dil_o, w_x_o, w_out, g_post)


def _ffn_kernel(x_ref, gpre_ref, wg_ref, wu_ref, wd_ref, gpost_ref, out_ref, h_ref, acc_ref, *, n_steps):
    s = pl.program_id(1)

    @pl.when(s == 0)
    def _():
        h_ref[...] = _rms_rows(x_ref[...], gpre_ref[...]).astype(BF16)

    h = h_ref[...]
    gate = jnp.dot(h, wg_ref[...], preferred_element_type=F32)
    up = jnp.dot(h, wu_ref[...], preferred_element_type=F32)
    act = (gate * jax.nn.sigmoid(gate) * up).astype(BF16)
    part = jnp.dot(act, wd_ref[...], preferred_element_type=F32)

    @pl.when(s == 0)
    def _():
        acc_ref[...] = part

    @pl.when(s > 0)
    def _():
        acc_ref[...] += part

    @pl.when(s == n_steps - 1)
    def _():
        out_ref[...] = x_ref[...] + _rms_rows(acc_ref[...], gpost_ref[...])


def _ffn(x, g_pre, w_gate, w_up, w_down, g_post, *, tm, tf):
    m = x.shape[0]
    n_steps = D_FF // tf
    return pl.pallas_call(
        functools.partial(_ffn_kernel, n_steps=n_steps),
        grid=(m // tm, n_steps),
        in_specs=[
            pl.BlockSpec((tm, D_MODEL), lambda i, s: (i, 0)),
            pl.BlockSpec((1, D_MODEL), lambda i, s: (0, 0)),
            pl.BlockSpec((D_MODEL, tf), lambda i, s: (0, s)),
            pl.BlockSpec((D_MODEL, tf), lambda i, s: (0, s)),
            pl.BlockSpec((tf, D_MODEL), lambda i, s: (s, 0)),
            pl.BlockSpec((1, D_MODEL), lambda i, s: (0, 0)),
        ],
        out_specs=pl.BlockSpec((tm, D_MODEL), lambda i, s: (i, 0)),
        out_shape=jax.ShapeDtypeStruct((m, D_MODEL), F32),
        scratch_shapes=[pltpu.VMEM((tm, D_MODEL), BF16), pltpu.VMEM((tm, D_MODEL), F32)],
        compiler_params=_params(("parallel", "arbitrary"), 48),
        name="ffn",
    )(x, g_pre, w_gate, w_up, w_down, g_post)


def kernel(x_prompt, x_sample, cache_win_k, cache_win_v, state_conv, cache_mem_k, cache_mem_v, mem_prompt,
           g_pre_mix, w_in, b_gate, conv_w, conv_b, conv_ln_g, conv_ln_b, w_conv_out, w_dil_o, g_mem, w_mem_kv,
           w_x_o, w_out, g_post_mix, g_pre_ffn, w_ffn_gate, w_ffn_up, w_ffn_down, g_post_ffn):
    batch, seq, _ = x_prompt.shape
    n_seq = x_sample.shape[0]
    depth = w_in.shape[0]
    assert depth == 1 and x_sample.shape[1] == 1 and seq % (DIL_GROUPS[-1][1] * SPAN) == 0
    assert cache_win_k.shape[2] == DIL_GROUPS[-1][0]

    idx = jnp.arange(1, N_QH + 1, dtype=F32)
    slopes = jnp.exp2(-8.0 * idx / N_QH).reshape(N_GROUPS, N_KV)

    xp, xs = x_prompt.reshape(batch * seq, D_MODEL), x_sample.reshape(n_seq, D_MODEL)
    l = 0
    w_in_b = w_in[l].astype(BF16)
    w_conv_out_b = w_conv_out[l].astype(BF16)
    w_dil_o_b = w_dil_o[l].astype(BF16)
    w_x_o_b = w_x_o[l].astype(BF16)
    w_out_b = w_out[l].astype(BF16)
    w_mem_kv_b = w_mem_kv[l].astype(BF16)
    w_gate_b = w_ffn_gate[l].astype(BF16)
    w_up_b = w_ffn_up[l].astype(BF16)
    w_down_b = w_ffn_down[l].astype(BF16)
    row = lambda a: a[l].reshape(1, -1)
    g_pre, g_post, g_ffn_pre, g_ffn_post = row(g_pre_mix), row(g_post_mix), row(g_pre_ffn), row(g_post_ffn)
    cb, lng, lnb = row(conv_b), row(conv_ln_g), row(conv_ln_b)
    conv_w_slabs = conv_w[l].reshape(CONV_WIDTH, N_SLABS, LANES).transpose(1, 0, 2)

    mk, mv = _mem_kv(mem_prompt.reshape(batch * N_MEM, D_MODEL), row(g_mem), w_mem_kv_b, tm=512)
    h_p, u_p, q_p, k_p, v_p, qx_p = _in_proj(xp, g_pre, w_in_b, tm=512, slabs=True)
    c_p, conv_p = _conv_prompt(u_p, conv_w_slabs, cb, lng, lnb, batch=batch, seq=seq)
    mix_p, xm_p = _attn_prompt(slopes, q_p.reshape(batch, seq, W_QD), k_p.reshape(batch, seq, W_KV),
                               v_p.reshape(batch, seq, W_KV), qx_p.reshape(batch, seq, W_QX),
                               mk.reshape(batch, N_MEM, W_QX), mv.reshape(batch, N_MEM, W_QX),
                               batch=batch, seq=seq)
    x1_p = _merge(h_p, c_p, mix_p.reshape(batch * seq, W_KV), xm_p.reshape(batch * seq, W_QX), xp, w_in_b,
                  b_gate[l], w_conv_out_b, w_dil_o_b, w_x_o_b, w_out_b, g_post, tm=512, tn=256, tn2=512)
    y_p = _ffn(x1_p, g_ffn_pre, w_gate_b, w_up_b, w_down_b, g_ffn_post, tm=512, tf=512)

    h_s, u_s, q_s, k_s, v_s, qx_s = _in_proj(xs, g_pre, w_in_b, tm=n_seq, slabs=False)
    c_s, conv_s = _conv_sample(state_conv[l], u_s, conv_w[l], cb, lng, lnb)
    mix_s, xm_s = _attn_sample(slopes, q_s, k_s, v_s, qx_s,
                               cache_win_k[l], cache_win_v[l], cache_mem_k[l], cache_mem_v[l])
    x1_s = _merge(h_s, c_s, mix_s.reshape(n_seq, W_KV), xm_s.reshape(n_seq, W_QX), xs, w_in_b, b_gate[l], w_conv_out_b, w_dil_o_b, w_x_o_b, w_out_b,
                  g_post, tm=n_seq, tn=256, tn2=512)
    y_s = _ffn(x1_s, g_ffn_pre, w_gate_b, w_up_b, w_down_b, g_ffn_post, tm=n_seq, tf=512)

    win = min(DIL_GROUPS[-1][0], seq)
    kv_shape = (1, batch, seq, N_KV, HEAD_DIM)
    mem_shape = (1, batch, N_MEM, N_XHEADS, HEAD_DIM)
    new_shape = (1, n_seq, 1, N_KV, HEAD_DIM)
    return (y_p.reshape(batch, seq, D_MODEL), y_s.reshape(n_seq, 1, D_MODEL),
            k_p.reshape(kv_shape)[:, :, seq - win:], v_p.reshape(kv_shape)[:, :, seq - win:], conv_p,
            mk.reshape(mem_shape), mv.reshape(mem_shape),
            k_s.reshape(new_shape), v_s.reshape(new_shape), conv_s.reshape(1, n_seq, CONV_PREFIX, C_CONV))
```

```python
import functools

import jax
import jax.numpy as jnp
from jax import lax
from jax.experimental import pallas as pl
from jax.experimental.pallas import tpu as pltpu

F32 = jnp.float32
BF16 = jnp.bfloat16

D_MODEL = 2048
C_CONV = D_MODEL // 2
CONV_WIDTH = 31
CONV_PREFIX = CONV_WIDTH - 1
HEAD_DIM = 128
N_KV = 4
DIL_GROUPS = ((128, 1), (512, 4), (2048, 16))
N_GROUPS = len(DIL_GROUPS)
N_QH = N_GROUPS * N_KV
SPAN = 128
N_MEM = 256
N_XHEADS = 4
W_QD = N_QH * HEAD_DIM
W_KV = N_KV * HEAD_DIM
W_QX = N_XHEADS * HEAD_DIM
D_FF = 5632
EPS = 1e-6
NEG = -1e30
SCALE = HEAD_DIM ** -0.5
LANES = 128
SUBLANES = 8
N_SLABS = C_CONV // LANES
MIB = 1024 * 1024

TN = 256
_A_TILES = C_CONV // TN
_Q_TILES = W_QD // TN
_KV_TILES = W_KV // TN
_QX_TILES = W_QX // TN
_STEP_Q0 = _A_TILES
_STEP_K0 = _STEP_Q0 + _Q_TILES
_STEP_V0 = _STEP_K0 + _KV_TILES
_STEP_X0 = _STEP_V0 + _KV_TILES
_IN_STEPS = _STEP_X0 + _QX_TILES
_GATE_TILE0 = 2 * _A_TILES + _Q_TILES + 2 * _KV_TILES + _QX_TILES
_D_TILES = D_MODEL // TN
TN_OUT = 512
TF = 512


def _params(semantics, vmem_mib):
    return pltpu.CompilerParams(dimension_semantics=semantics, vmem_limit_bytes=vmem_mib * MIB)


def _rms_rows(x, g):
    return x * lax.rsqrt(jnp.mean(x * x, axis=-1, keepdims=True) + EPS) * g


def _col_tiles(w, tn):
    k, n = w.shape
    return w.astype(BF16).reshape(k, n // tn, tn).transpose(1, 0, 2)


def _dot(a, b):
    return jnp.dot(a, b, preferred_element_type=F32)


def _in_proj_kernel(x_ref, g_ref, w1_ref, w2_ref, h_ref, u_ref, q_ref, k_ref, v_ref, qx_ref, *, slabs):
    s = pl.program_id(1)

    @pl.when(s == 0)
    def _():
        h_ref[...] = _rms_rows(x_ref[...], g_ref[...]).astype(BF16)

    @pl.when(s < _STEP_Q0)
    def _():
        h = h_ref[...]
        u = _dot(h, w1_ref[...]) * jax.nn.sigmoid(_dot(h, w2_ref[...]))
        if slabs:
            for c in range(TN // LANES):
                u_ref[c] = u[:, c * LANES:(c + 1) * LANES]
        else:
            u_ref[...] = u

    @pl.when((s >= _STEP_Q0) & (s < _STEP_K0))
    def _():
        q_ref[...] = _dot(h_ref[...], w1_ref[...])

    @pl.when((s >= _STEP_K0) & (s < _STEP_V0))
    def _():
        k_ref[...] = _dot(h_ref[...], w1_ref[...])

    @pl.when((s >= _STEP_V0) & (s < _STEP_X0))
    def _():
        v_ref[...] = _dot(h_ref[...], w1_ref[...])

    @pl.when(s >= _STEP_X0)
    def _():
        qx_ref[...] = _dot(h_ref[...], w1_ref[...]).astype(qx_ref.dtype)


def _in_proj(x, g, w_in_t, *, tm, slabs):
    m = x.shape[0]
    last_a = _A_TILES - 1
    if slabs:
        u_shape = jax.ShapeDtypeStruct((N_SLABS, m, LANES), F32)
        u_spec = pl.BlockSpec((TN // LANES, tm, LANES), lambda i, s: (jnp.minimum(s, last_a), i, 0))
    else:
        u_shape = jax.ShapeDtypeStruct((m, C_CONV), F32)
        u_spec = pl.BlockSpec((tm, TN), lambda i, s: (i, jnp.minimum(s, last_a)))

    def out_tiles(step0, n_tiles):
        return pl.BlockSpec((tm, TN), lambda i, s: (i, jnp.clip(s - step0, 0, n_tiles - 1)))

    return pl.pallas_call(
        functools.partial(_in_proj_kernel, slabs=slabs),
        grid=(m // tm, _IN_STEPS),
        in_specs=[
            pl.BlockSpec((tm, D_MODEL), lambda i, s: (i, 0)),
            pl.BlockSpec((1, D_MODEL), lambda i, s: (0, 0)),
            pl.BlockSpec((None, D_MODEL, TN), lambda i, s: (jnp.where(s < _A_TILES, s, s + _A_TILES), 0, 0)),
            pl.BlockSpec((None, D_MODEL, TN), lambda i, s: (jnp.minimum(s, last_a) + _A_TILES, 0, 0)),
        ],
        out_specs=[
            pl.BlockSpec((tm, D_MODEL), lambda i, s: (i, 0)),
            u_spec,
            out_tiles(_STEP_Q0, _Q_TILES),
            out_tiles(_STEP_K0, _KV_TILES),
            out_tiles(_STEP_V0, _KV_TILES),
            out_tiles(_STEP_X0, _QX_TILES),
        ],
        out_shape=[
            jax.ShapeDtypeStruct((m, D_MODEL), BF16),
            u_shape,
            jax.ShapeDtypeStruct((m, W_QD), F32),
            jax.ShapeDtypeStruct((m, W_KV), F32),
            jax.ShapeDtypeStruct((m, W_KV), F32),
            jax.ShapeDtypeStruct((m, W_QX), BF16 if slabs else F32),
        ],
        compiler_params=_params(("parallel", "arbitrary"), 48),
        name="in_proj",
    )(x, g, w_in_t, w_in_t)


def _mem_kv_kernel(x_ref, g_ref, w_ref, mk_ref, mv_ref, h_ref):
    s = pl.program_id(1)

    @pl.when(s == 0)
    def _():
        h_ref[...] = _rms_rows(x_ref[...], g_ref[...]).astype(BF16)
        mk_ref[...] = _dot(h_ref[...], w_ref[...])

    @pl.when(s == 1)
    def _():
        mv_ref[...] = _dot(h_ref[...], w_ref[...])


def _mem_kv(mem, g, w_t, *, tm):
    m = mem.shape[0]
    return pl.pallas_call(
        _mem_kv_kernel,
        grid=(m // tm, 2),
        in_specs=[
            pl.BlockSpec((tm, D_MODEL), lambda i, s: (i, 0)),
            pl.BlockSpec((1, D_MODEL), lambda i, s: (0, 0)),
            pl.BlockSpec((None, D_MODEL, W_QX), lambda i, s: (s, 0, 0)),
        ],
        out_specs=[
            pl.BlockSpec((tm, W_QX), lambda i, s: (i, 0)),
            pl.BlockSpec((tm, W_QX), lambda i, s: (i, 0)),
        ],
        out_shape=[jax.ShapeDtypeStruct((m, W_QX), F32)] * 2,
        scratch_shapes=[pltpu.VMEM((tm, D_MODEL), BF16)],
        compiler_params=_params(("parallel", "arbitrary"), 32),
        name="mem_kv",
    )(mem, g, w_t)


def _ln_swish_slabs(y_slabs, cb_ref, lng_ref, lnb_ref, out_dtype):
    ys = [y + cb_ref[:, c * LANES:(c + 1) * LANES] for c, y in enumerate(y_slabs)]
    tot = ys[0]
    for y in ys[1:]:
        tot = tot + y
    mu = jnp.sum(tot, axis=-1, keepdims=True) * (1.0 / C_CONV)
    ds = [y - mu for y in ys]
    sq = ds[0] * ds[0]
    for d in ds[1:]:
        sq = sq + d * d
    rs = lax.rsqrt(jnp.sum(sq, axis=-1, keepdims=True) * (1.0 / C_CONV) + EPS)
    outs = []
    for c, d in enumerate(ds):
        z = d * rs * lng_ref[:, c * LANES:(c + 1) * LANES] + lnb_ref[:, c * LANES:(c + 1) * LANES]
        outs.append((z * jax.nn.sigmoid(z)).astype(out_dtype))
    return outs


def _conv_prompt_kernel(u_ref, w_ref, cb_ref, lng_ref, lnb_ref, c_ref, st_ref, head_ref, *, seq, tt):
    pad = 32
    head_ref[:, 0:pad, :] = jnp.zeros((N_SLABS, pad, LANES), F32)
    head_ref[:, pad:pad + tt, :] = u_ref[:, 0:tt, :]

    def tile(src_ref, base, t0):
        ys = []
        for c in range(N_SLABS):
            acc = jnp.zeros((tt, LANES), F32)
            for j in range(SUBLANES):
                taps = range(j, CONV_WIDTH, SUBLANES)
                win = src_ref[c, pl.ds(base + j, tt + taps[-1] - j), :]
                for k in taps:
                    acc = acc + win[k - j:k - j + tt] * w_ref[c, k:k + 1, :]
            ys.append(acc)
        outs = _ln_swish_slabs(ys, cb_ref, lng_ref, lnb_ref, BF16)
        for c in range(N_SLABS):
            c_ref[pl.ds(t0, tt), c * LANES:(c + 1) * LANES] = outs[c]

    tile(head_ref, pad - CONV_PREFIX, 0)

    def body(j, carry):
        t0 = pl.multiple_of(j * tt, tt)
        tile(u_ref, t0 - CONV_PREFIX, t0)
        return carry

    lax.fori_loop(1, seq // tt, body, 0)
    for c in range(N_SLABS):
        st_ref[:, c * LANES:(c + 1) * LANES] = u_ref[c, seq - CONV_PREFIX:seq, :]


def _conv_prompt(u_slabs, w_slabs, conv_b, ln_g, ln_b, *, batch, seq):
    tt = 64
    return pl.pallas_call(
        functools.partial(_conv_prompt_kernel, seq=seq, tt=tt),
        grid=(batch,),
        in_specs=[
            pl.BlockSpec((N_SLABS, seq, LANES), lambda b: (0, b, 0)),
            pl.BlockSpec((N_SLABS, CONV_WIDTH, LANES), lambda b: (0, 0, 0)),
            pl.BlockSpec((1, C_CONV), lambda b: (0, 0)),
            pl.BlockSpec((1, C_CONV), lambda b: (0, 0)),
            pl.BlockSpec((1, C_CONV), lambda b: (0, 0)),
        ],
        out_specs=[
            pl.BlockSpec((seq, C_CONV), lambda b: (b, 0)),
            pl.BlockSpec((None, None, CONV_PREFIX, C_CONV), lambda b: (0, b, 0, 0)),
        ],
        out_shape=[
            jax.ShapeDtypeStruct((batch * seq, C_CONV), BF16),
            jax.ShapeDtypeStruct((1, batch, CONV_PREFIX, C_CONV), F32),
        ],
        scratch_shapes=[pltpu.VMEM((N_SLABS, 32 + tt, LANES), F32)],
        compiler_params=_params(("parallel",), 40),
        name="conv_prompt",
    )(u_slabs, w_slabs, conv_b, ln_g, ln_b)


def _conv_sample_kernel(state_ref, u_ref, w_ref, cb_ref, lng_ref, lnb_ref, c_ref, st_ref, *, n_seq):
    w_hist = w_ref[0:CONV_PREFIX, :]
    w_new = w_ref[CONV_PREFIX:CONV_WIDTH, :]
    for n in range(n_seq):
        hist = state_ref[n]
        new = u_ref[n:n + 1, :]
        y = jnp.sum(hist * w_hist, axis=0, keepdims=True) + new * w_new
        ys = [y[:, c * LANES:(c + 1) * LANES] for c in range(N_SLABS)]
        outs = _ln_swish_slabs(ys, cb_ref, lng_ref, lnb_ref, F32)
        for c in range(N_SLABS):
            c_ref[n:n + 1, c * LANES:(c + 1) * LANES] = outs[c]
        st_ref[n, 0:CONV_PREFIX - 1, :] = state_ref[n, 1:CONV_PREFIX, :]
        st_ref[n, CONV_PREFIX - 1:CONV_PREFIX, :] = new


def _conv_sample(state, u, conv_w, conv_b, ln_g, ln_b):
    n_seq = state.shape[0]
    return pl.pallas_call(
        functools.partial(_conv_sample_kernel, n_seq=n_seq),
        out_shape=[
            jax.ShapeDtypeStruct((n_seq, C_CONV), F32),
            jax.ShapeDtypeStruct((n_seq, CONV_PREFIX, C_CONV), F32),
        ],
        compiler_params=pltpu.CompilerParams(vmem_limit_bytes=32 * MIB),
        name="conv_sample",
    )(state, u, conv_w, conv_b, ln_g, ln_b)


def _softmax_pv(s, v):
    m = jnp.max(s, axis=-1, keepdims=True)
    p = jnp.exp(s - m)
    l = jnp.sum(p, axis=-1, keepdims=True)
    o = _dot(p.astype(BF16), v) * (1.0 / l)
    return o, m + jnp.log(l)


def _qk(q, k):
    return lax.dot_general(q, k, (((1,), (1,)), ((), ())), preferred_element_type=F32)


def _attn_prompt_kernel(slopes_ref, q0_ref, q1_ref, q2_ref, k_ref, v_ref, qx_ref, mk_ref, mv_ref,
                        mix_ref, xm_ref, bias_ref, o_ref, lse_ref, *, seq):
    head = pl.program_id(1)
    qi = lax.broadcasted_iota(jnp.int32, (SPAN, 2 * SPAN), 0)
    ci = lax.broadcasted_iota(jnp.int32, (SPAN, 2 * SPAN), 1)
    dist = qi + SPAN - ci
    valid = (dist >= 0) & (dist <= SPAN)
    for g, (_, dil) in enumerate(DIL_GROUPS):
        bias = -slopes_ref[g, head] * (dist * dil).astype(F32)
        bias_ref[g] = jnp.where(valid, bias, NEG)

    def rows(ref, start, size, dil):
        if dil == 1:
            return ref[pl.ds(start, size), :]
        return ref[pl.ds(start, size, stride=dil), :]

    def put(ref, g, start, dil, val):
        if dil == 1:
            ref[g, pl.ds(start, SPAN), :] = val
        else:
            ref[g, pl.ds(start, SPAN, stride=dil), :] = val

    def unit(g, dil, q_ref, start, n_keys):
        k_start = start - (n_keys - SPAN) * dil
        q = rows(q_ref, start, SPAN, dil).astype(BF16)
        k = rows(k_ref, k_start, n_keys, dil).astype(BF16)
        v = rows(v_ref, k_start, n_keys, dil).astype(BF16)
        s = _qk(q, k) * SCALE + bias_ref[g, :, 2 * SPAN - n_keys:2 * SPAN]
        o, lse = _softmax_pv(s, v)
        put(o_ref, g, start, dil, o)
        put(lse_ref, g, start, dil, jnp.broadcast_to(lse, (SPAN, LANES)))

    q_refs = (q0_ref, q1_ref, q2_ref)
    for g, (_, dil) in enumerate(DIL_GROUPS):
        n_blocks = seq // dil // SPAN
        q_ref = q_refs[g]
        if n_blocks == 1:
            def first_blocks(r, carry, g=g, dil=dil, q_ref=q_ref):
                unit(g, dil, q_ref, r, SPAN)
                return carry
            lax.fori_loop(0, dil, first_blocks, 0, unroll=4)
        elif dil == 1:
            unit(g, dil, q_ref, 0, SPAN)

            def later_blocks(n, carry, g=g, dil=dil, q_ref=q_ref):
                unit(g, dil, q_ref, pl.multiple_of(n * SPAN, SPAN), 2 * SPAN)
                return carry
            lax.fori_loop(1, n_blocks, later_blocks, 0, unroll=3)
        else:
            def residue(r, carry, g=g, dil=dil, q_ref=q_ref, n_blocks=n_blocks):
                unit(g, dil, q_ref, r, SPAN)
                for n in range(1, n_blocks):
                    unit(g, dil, q_ref, r + n * (SPAN * dil), 2 * SPAN)
                return carry
            lax.fori_loop(0, dil, residue, 0)

    tq = 256
    mk = mk_ref[...].astype(BF16)
    mv = mv_ref[...].astype(BF16)

    def finish(j, carry):
        t0 = pl.multiple_of(j * tq, tq)
        l0 = lse_ref[0, pl.ds(t0, tq), :]
        l1 = lse_ref[1, pl.ds(t0, tq), :]
        l2 = lse_ref[2, pl.ds(t0, tq), :]
        top = jnp.maximum(jnp.maximum(l0, l1), l2)
        w0 = jnp.exp(l0 - top)
        w1 = jnp.exp(l1 - top)
        w2 = jnp.exp(l2 - top)
        num = (w0 * o_ref[0, pl.ds(t0, tq), :] + w1 * o_ref[1, pl.ds(t0, tq), :]
               + w2 * o_ref[2, pl.ds(t0, tq), :])
        mix_ref[pl.ds(t0, tq), :] = (num / (w0 + w1 + w2)).astype(BF16)
        sx = _qk(qx_ref[pl.ds(t0, tq), :], mk) * SCALE
        ox, _ = _softmax_pv(sx, mv)
        xm_ref[pl.ds(t0, tq), :] = ox.astype(BF16)
        return carry

    lax.fori_loop(0, seq // tq, finish, 0, unroll=2)


def _attn_prompt(slopes, q, k, v, qx, mk, mv, *, batch, seq):
    def col(c):
        return pl.BlockSpec((None, seq, LANES), lambda b, h, c=c: (b, 0, c * N_KV + h))

    head_spec = pl.BlockSpec((None, seq, LANES), lambda b, h: (b, 0, h))
    mem_spec = pl.BlockSpec((None, N_MEM, LANES), lambda b, h: (b, 0, h))
    return pl.pallas_call(
        functools.partial(_attn_prompt_kernel, seq=seq),
        grid=(batch, N_KV),
        in_specs=[
            pl.BlockSpec(memory_space=pltpu.SMEM),
            col(0), col(1), col(2), head_spec, head_spec, head_spec, mem_spec, mem_spec,
        ],
        out_specs=[head_spec, head_spec],
        out_shape=[jax.ShapeDtypeStruct((batch, seq, W_KV), BF16),
                   jax.ShapeDtypeStruct((batch, seq, W_QX), BF16)],
        scratch_shapes=[
            pltpu.VMEM((N_GROUPS, SPAN, 2 * SPAN), F32),
            pltpu.VMEM((N_GROUPS, seq, LANES), F32),
            pltpu.VMEM((N_GROUPS, seq, LANES), F32),
        ],
        compiler_params=_params(("parallel", "parallel"), 40),
        name="attn_prompt",
    )(slopes, q, q, q, k, v, qx, mk, mv)


def _attn_sample_kernel(slopes_ref, q_ref, kn_ref, vn_ref, qx_ref, ck0_ref, ck1_ref, ck2_ref,
                        cv0_ref, cv1_ref, cv2_ref, mk_ref, mv_ref, mix_ref, xm_ref, *, bs):
    ck_refs = (ck0_ref, ck1_ref, ck2_ref)
    cv_refs = (cv0_ref, cv1_ref, cv2_ref)
    back = (SPAN - lax.broadcasted_iota(jnp.int32, (SPAN, 1), 0)).astype(F32)

    def one(n, carry):
        for h in range(N_KV):
            hs = slice(h * LANES, (h + 1) * LANES)
            kn = kn_ref[n, :, hs]
            vn = vn_ref[n, :, hs]
            scores, s_new = [], []
            for g, (_, dil) in enumerate(DIL_GROUPS):
                qs = slice(g * W_KV + h * LANES, g * W_KV + (h + 1) * LANES)
                q = q_ref[n, :, qs]
                s = jnp.sum(ck_refs[g][n, :, h, :] * q, axis=-1, keepdims=True) * SCALE
                scores.append(s - slopes_ref[g, h] * (back * float(dil)))
                s_new.append(jnp.sum(kn * q, axis=-1, keepdims=True) * SCALE)
            top = s_new[0]
            for g in range(N_GROUPS):
                top = jnp.maximum(top, jnp.maximum(s_new[g], jnp.max(scores[g], axis=0, keepdims=True)))
            num = jnp.zeros((1, LANES), F32)
            den = jnp.zeros((1, 1), F32)
            for g in range(N_GROUPS):
                p = jnp.exp(scores[g] - top)
                p_new = jnp.exp(s_new[g] - top)
                num = num + jnp.sum(p * cv_refs[g][n, :, h, :], axis=0, keepdims=True) + p_new * vn
                den = den + jnp.sum(p, axis=0, keepdims=True) + p_new
            mix_ref[n, :, hs] = num / den
            qx = qx_ref[n, :, hs]
            sx = jnp.sum(mk_ref[n, :, h, :] * qx, axis=-1, keepdims=True) * SCALE
            px = jnp.exp(sx - jnp.max(sx, axis=0, keepdims=True))
            ox = jnp.sum(px * mv_ref[n, :, h, :], axis=0, keepdims=True) / jnp.sum(px, axis=0, keepdims=True)
            xm_ref[n, :, hs] = ox
        return carry

    lax.fori_loop(0, bs, one, 0)


def _attn_sample(slopes, q, kn, vn, qx, cache_k, cache_v, mem_k, mem_v):
    n_seq, win = cache_k.shape[0], cache_k.shape[1]
    bs = 2
    row = lambda w: pl.BlockSpec((bs, 1, w), lambda i: (i, 0, 0))
    as_rows = lambda a: a.reshape(n_seq, 1, a.shape[-1])
    views_k, views_v, cache_specs = [], [], []
    for _, dil in DIL_GROUPS:
        views_k.append(cache_k.reshape(n_seq, win // dil, dil * N_KV, HEAD_DIM))
        views_v.append(cache_v.reshape(n_seq, win // dil, dil * N_KV, HEAD_DIM))
        rows = N_KV if dil == 1 else SUBLANES
        cache_specs.append(pl.BlockSpec((bs, SPAN, rows, HEAD_DIM),
                                        lambda i, nb=win // dil // SPAN: (i, nb - 1, 0, 0)))
    mem_spec = pl.BlockSpec((bs, N_MEM, N_XHEADS, HEAD_DIM), lambda i: (i, 0, 0, 0))
    return pl.pallas_call(
        functools.partial(_attn_sample_kernel, bs=bs),
        grid=(n_seq // bs,),
        in_specs=[pl.BlockSpec(memory_space=pltpu.SMEM), row(W_QD), row(W_KV), row(W_KV), row(W_QX)]
        + cache_specs + cache_specs + [mem_spec, mem_spec],
        out_specs=[row(W_KV), row(W_QX)],
        out_shape=[jax.ShapeDtypeStruct((n_seq, 1, W_KV), F32), jax.ShapeDtypeStruct((n_seq, 1, W_QX), F32)],
        compiler_params=_params(("parallel",), 40),
        name="attn_sample",
    )(slopes, as_rows(q), as_rows(kn), as_rows(vn), as_rows(qx), *views_k, *views_v, mem_k, mem_v)


def _merge_kernel(h_ref, c_ref, mix_ref, xm_ref, wg0_ref, wg1_ref, wg2_ref, bg_ref, wc_ref, wd_ref, wx_ref,
                  wo_ref, z_ref, merged_ref):
    s = pl.program_id(1)

    @pl.when(s < _D_TILES)
    def _():
        h = h_ref[...]
        acc = None
        for br, (wg_ref, y_ref, wy_ref) in enumerate(
                ((wg0_ref, c_ref, wc_ref), (wg1_ref, mix_ref, wd_ref), (wg2_ref, xm_ref, wx_ref))):
            gate = jax.nn.sigmoid(_dot(h, wg_ref[...]) + bg_ref[br])
            term = gate * _dot(y_ref[...].astype(BF16), wy_ref[...])
            acc = term if acc is None else acc + term
        merged_ref[s] = acc.astype(BF16)

    @pl.when(s >= _D_TILES)
    def _():
        z = None
        for j in range(_D_TILES):
            part = _dot(merged_ref[j], wo_ref[j * TN:(j + 1) * TN, :])
            z = part if z is None else z + part
        z_ref[...] = z


def _merge(h, c, mix, xm, w_in_t, b_gate, w_conv_out_t, w_dil_o_t, w_x_o_t, w_out_t, *, tm):
    m = h.shape[0]
    n1, n2 = _D_TILES, D_MODEL // TN_OUT
    last = n1 - 1

    def gate_spec(br):
        return pl.BlockSpec((None, D_MODEL, TN),
                            lambda i, s, br=br: (_GATE_TILE0 + br * _D_TILES + jnp.minimum(s, last), 0, 0))

    def colw(kdim):
        return pl.BlockSpec((None, kdim, TN), lambda i, s: (jnp.minimum(s, last), 0, 0))

    def rowblk(w):
        return pl.BlockSpec((tm, w), lambda i, s: (i, 0))

    return pl.pallas_call(
        _merge_kernel,
        grid=(m // tm, n1 + n2),
        in_specs=[
            rowblk(D_MODEL), rowblk(C_CONV), rowblk(W_KV), rowblk(W_QX),
            gate_spec(0), gate_spec(1), gate_spec(2),
            pl.BlockSpec((3, 1, TN), lambda i, s: (0, 0, jnp.minimum(s, last))),
            colw(C_CONV), colw(W_KV), colw(W_QX),
            pl.BlockSpec((None, D_MODEL, TN_OUT), lambda i, s: (jnp.maximum(s - n1, 0), 0, 0)),
        ],
        out_specs=pl.BlockSpec((tm, TN_OUT), lambda i, s: (i, jnp.maximum(s - n1, 0))),
        out_shape=jax.ShapeDtypeStruct((m, D_MODEL), F32),
        scratch_shapes=[pltpu.VMEM((n1, tm, TN), BF16)],
        compiler_params=_params(("parallel", "arbitrary"), 52),
        name="merge",
    )(h, c, mix, xm, w_in_t, w_in_t, w_in_t, b_gate.reshape(3, 1, D_MODEL), w_conv_out_t, w_dil_o_t, w_x_o_t,
      w_out_t)


def _ffn_kernel(x_ref, z_ref, gmix_ref, gpre_ref, wg_ref, wu_ref, wd_ref, gpost_ref, out_ref, x1_ref, h_ref,
                *, n_steps):
    s = pl.program_id(1)

    @pl.when(s == 0)
    def _():
        x1 = x_ref[...] + _rms_rows(z_ref[...], gmix_ref[...])
        x1_ref[...] = x1
        h_ref[...] = _rms_rows(x1, gpre_ref[...]).astype(BF16)
        out_ref[...] = jnp.zeros_like(out_ref)

    h = h_ref[...]
    gate = _dot(h, wg_ref[...])
    act = (gate * jax.nn.sigmoid(gate) * _dot(h, wu_ref[...])).astype(BF16)
    out_ref[...] += _dot(act, wd_ref[...])

    @pl.when(s == n_steps - 1)
    def _():
        out_ref[...] = x1_ref[...] + _rms_rows(out_ref[...], gpost_ref[...])


def _ffn(x, z, g_mix, g_pre, w_gate_t, w_up_t, w_down, g_post, *, tm):
    m = x.shape[0]
    n_steps = D_FF // TF
    rowblk = pl.BlockSpec((tm, D_MODEL), lambda i, s: (i, 0))
    gain = pl.BlockSpec((1, D_MODEL), lambda i, s: (0, 0))
    return pl.pallas_call(
        functools.partial(_ffn_kernel, n_steps=n_steps),
        grid=(m // tm, n_steps),
        in_specs=[
            rowblk, rowblk, gain, gain,
            pl.BlockSpec((None, D_MODEL, TF), lambda i, s: (s, 0, 0)),
            pl.BlockSpec((None, D_MODEL, TF), lambda i, s: (s, 0, 0)),
            pl.BlockSpec((TF, D_MODEL), lambda i, s: (s, 0)),
            gain,
        ],
        out_specs=rowblk,
        out_shape=jax.ShapeDtypeStruct((m, D_MODEL), F32),
        scratch_shapes=[pltpu.VMEM((tm, D_MODEL), F32), pltpu.VMEM((tm, D_MODEL), BF16)],
        compiler_params=_params(("parallel", "arbitrary"), 52),
        name="ffn",
    )(x, z, g_mix, g_pre, w_gate_t, w_up_t, w_down, g_post)


def kernel(x_prompt, x_sample, cache_win_k, cache_win_v, state_conv, cache_mem_k, cache_mem_v, mem_prompt,
           g_pre_mix, w_in, b_gate, conv_w, conv_b, conv_ln_g, conv_ln_b, w_conv_out, w_dil_o, g_mem, w_mem_kv,
           w_x_o, w_out, g_post_mix, g_pre_ffn, w_ffn_gate, w_ffn_up, w_ffn_down, g_post_ffn):
    batch, seq, _ = x_prompt.shape
    n_seq = x_sample.shape[0]
    depth = w_in.shape[0]
    assert depth == 1 and x_sample.shape[1] == 1 and seq % (DIL_GROUPS[-1][1] * SPAN) == 0
    assert cache_win_k.shape[2] == DIL_GROUPS[-1][0]

    idx = jnp.arange(1, N_QH + 1, dtype=F32)
    slopes = jnp.exp2(-8.0 * idx / N_QH).reshape(N_GROUPS, N_KV)

    xp, xs = x_prompt.reshape(batch * seq, D_MODEL), x_sample.reshape(n_seq, D_MODEL)
    l = 0
    w_in_t = _col_tiles(w_in[l], TN)
    w_conv_out_t = _col_tiles(w_conv_out[l], TN)
    w_dil_o_t = _col_tiles(w_dil_o[l], TN)
    w_x_o_t = _col_tiles(w_x_o[l], TN)
    w_out_t = _col_tiles(w_out[l], TN_OUT)
    w_mem_kv_t = _col_tiles(w_mem_kv[l], W_QX)
    w_gate_t = _col_tiles(w_ffn_gate[l], TF)
    w_up_t = _col_tiles(w_ffn_up[l], TF)
    w_down_b = w_ffn_down[l].astype(BF16)
    row = lambda a: a[l].reshape(1, -1)
    g_pre, g_post, g_ffn_pre, g_ffn_post = row(g_pre_mix), row(g_post_mix), row(g_pre_ffn), row(g_post_ffn)
    cb, lng, lnb = row(conv_b), row(conv_ln_g), row(conv_ln_b)
    conv_w_slabs = conv_w[l].reshape(CONV_WIDTH, N_SLABS, LANES).transpose(1, 0, 2)

    def dense_tail(h, c, mix, xm, x, tm_merge, tm_ffn):
        z = _merge(h, c, mix, xm, w_in_t, b_gate[l], w_conv_out_t, w_dil_o_t, w_x_o_t, w_out_t, tm=tm_merge)
        return _ffn(x, z, g_post, g_ffn_pre, w_gate_t, w_up_t, w_down_b, g_ffn_post, tm=tm_ffn)

    mk, mv = _mem_kv(mem_prompt.reshape(batch * N_MEM, D_MODEL), row(g_mem), w_mem_kv_t, tm=512)
    h_p, u_p, q_p, k_p, v_p, qx_p = _in_proj(xp, g_pre, w_in_t, tm=1024, slabs=True)
    c_p, conv_p = _conv_prompt(u_p, conv_w_slabs, cb, lng, lnb, batch=batch, seq=seq)
    mix_p, xm_p = _attn_prompt(slopes, q_p.reshape(batch, seq, W_QD), k_p.reshape(batch, seq, W_KV),
                               v_p.reshape(batch, seq, W_KV), qx_p.reshape(batch, seq, W_QX),
                               mk.reshape(batch, N_MEM, W_QX), mv.reshape(batch, N_MEM, W_QX),
                               batch=batch, seq=seq)
    y_p = dense_tail(h_p, c_p, mix_p.reshape(batch * seq, W_KV), xm_p.reshape(batch * seq, W_QX), xp, 1024, 512)

    h_s, u_s, q_s, k_s, v_s, qx_s = _in_proj(xs, g_pre, w_in_t, tm=n_seq, slabs=False)
    c_s, conv_s = _conv_sample(state_conv[l], u_s, conv_w[l], cb, lng, lnb)
    mix_s, xm_s = _attn_sample(slopes, q_s, k_s, v_s, qx_s,
                               cache_win_k[l], cache_win_v[l], cache_mem_k[l], cache_mem_v[l])
    y_s = dense_tail(h_s, c_s, mix_s.reshape(n_seq, W_KV), xm_s.reshape(n_seq, W_QX), xs, n_seq, n_seq)

    win = min(DIL_GROUPS[-1][0], seq)
    kv_shape = (1, batch, seq, N_KV, HEAD_DIM)
    mem_shape = (1, batch, N_MEM, N_XHEADS, HEAD_DIM)
    new_shape = (1, n_seq, 1, N_KV, HEAD_DIM)
    return (y_p.reshape(batch, seq, D_MODEL), y_s.reshape(n_seq, 1, D_MODEL),
            k_p.reshape(kv_shape)[:, :, seq - win:], v_p.reshape(kv_shape)[:, :, seq - win:], conv_p,
            mk.reshape(mem_shape), mv.reshape(mem_shape),
            k_s.reshape(new_shape), v_s.reshape(new_shape), conv_s.reshape(1, n_seq, CONV_PREFIX, C_CONV))
```

```python
import functools

import jax
import jax.numpy as jnp
from jax import lax
from jax.experimental import pallas as pl
from jax.experimental.pallas import tpu as pltpu

F32 = jnp.float32
BF16 = jnp.bfloat16

D_MODEL = 2048
C_CONV = D_MODEL // 2
CONV_WIDTH = 31
CONV_PREFIX = CONV_WIDTH - 1
HEAD_DIM = 128
N_KV = 4
DIL_GROUPS = ((128, 1), (512, 4), (2048, 16))
N_GROUPS = len(DIL_GROUPS)
N_QH = N_GROUPS * N_KV
SPAN = 128
N_MEM = 256
N_XHEADS = 4
W_QD = N_QH * HEAD_DIM
W_KV = N_KV * HEAD_DIM
W_QX = N_XHEADS * HEAD_DIM
D_FF = 5632
EPS = 1e-6
NEG = -1e30
SCALE = HEAD_DIM ** -0.5
LANES = 128
SUBLANES = 8
N_SLABS = C_CONV // LANES
MIB = 1024 * 1024

TN = 512
_A_TILES = C_CONV // TN
_Q_TILES = W_QD // TN
_KV_TILES = W_KV // TN
_QX_TILES = W_QX // TN
_STEP_Q0 = _A_TILES
_STEP_K0 = _STEP_Q0 + _Q_TILES
_STEP_V0 = _STEP_K0 + _KV_TILES
_STEP_X0 = _STEP_V0 + _KV_TILES
_IN_STEPS = _STEP_X0 + _QX_TILES
_GATE_TILE0 = 2 * _A_TILES + _Q_TILES + 2 * _KV_TILES + _QX_TILES
_D_TILES = D_MODEL // TN
TN_OUT = 512
TF = 512


def _params(semantics, vmem_mib):
    return pltpu.CompilerParams(dimension_semantics=semantics, vmem_limit_bytes=vmem_mib * MIB)


def _rms_rows(x, g):
    return x * lax.rsqrt(jnp.mean(x * x, axis=-1, keepdims=True) + EPS) * g


def _dot(a, b):
    return jnp.dot(a, b, preferred_element_type=F32)


def _in_proj_kernel(x_ref, g_ref, w1_ref, w2_ref, h_ref, u_ref, q_ref, k_ref, v_ref, qx_ref, *, slabs):
    s = pl.program_id(1)

    @pl.when(s == 0)
    def _():
        h_ref[...] = _rms_rows(x_ref[...], g_ref[...]).astype(BF16)

    @pl.when(s < _STEP_Q0)
    def _():
        h = h_ref[...]
        u = _dot(h, w1_ref[...]) * jax.nn.sigmoid(_dot(h, w2_ref[...]))
        if slabs:
            for c in range(TN // LANES):
                u_ref[c] = u[:, c * LANES:(c + 1) * LANES]
        else:
            u_ref[...] = u

    @pl.when((s >= _STEP_Q0) & (s < _STEP_K0))
    def _():
        q_ref[...] = _dot(h_ref[...], w1_ref[...])

    @pl.when((s >= _STEP_K0) & (s < _STEP_V0))
    def _():
        k_ref[...] = _dot(h_ref[...], w1_ref[...])

    @pl.when((s >= _STEP_V0) & (s < _STEP_X0))
    def _():
        v_ref[...] = _dot(h_ref[...], w1_ref[...])

    @pl.when(s >= _STEP_X0)
    def _():
        qx_ref[...] = _dot(h_ref[...], w1_ref[...]).astype(qx_ref.dtype)


def _in_proj(x, g, w_in_t, *, tm, slabs):
    m = x.shape[0]
    last_a = _A_TILES - 1
    if slabs:
        u_shape = jax.ShapeDtypeStruct((N_SLABS, m, LANES), F32)
        u_spec = pl.BlockSpec((TN // LANES, tm, LANES), lambda i, s: (jnp.minimum(s, last_a), i, 0))
    else:
        u_shape = jax.ShapeDtypeStruct((m, C_CONV), F32)
        u_spec = pl.BlockSpec((tm, TN), lambda i, s: (i, jnp.minimum(s, last_a)))

    def out_tiles(step0, n_tiles):
        return pl.BlockSpec((tm, TN), lambda i, s: (i, jnp.clip(s - step0, 0, n_tiles - 1)))

    return pl.pallas_call(
        functools.partial(_in_proj_kernel, slabs=slabs),
        grid=(m // tm, _IN_STEPS),
        in_specs=[
            pl.BlockSpec((tm, D_MODEL), lambda i, s: (i, 0)),
            pl.BlockSpec((1, D_MODEL), lambda i, s: (0, 0)),
            pl.BlockSpec((D_MODEL, TN), lambda i, s: (0, jnp.where(s < _A_TILES, s, s + _A_TILES))),
            pl.BlockSpec((D_MODEL, TN), lambda i, s: (0, jnp.minimum(s, last_a) + _A_TILES)),
        ],
        out_specs=[
            pl.BlockSpec((tm, D_MODEL), lambda i, s: (i, 0)),
            u_spec,
            out_tiles(_STEP_Q0, _Q_TILES),
            out_tiles(_STEP_K0, _KV_TILES),
            out_tiles(_STEP_V0, _KV_TILES),
            out_tiles(_STEP_X0, _QX_TILES),
        ],
        out_shape=[
            jax.ShapeDtypeStruct((m, D_MODEL), BF16),
            u_shape,
            jax.ShapeDtypeStruct((m, W_QD), F32),
            jax.ShapeDtypeStruct((m, W_KV), F32),
            jax.ShapeDtypeStruct((m, W_KV), F32),
            jax.ShapeDtypeStruct((m, W_QX), BF16 if slabs else F32),
        ],
        compiler_params=_params(("parallel", "arbitrary"), 58),
        name="in_proj",
    )(x, g, w_in_t, w_in_t)


def _mem_kv_kernel(x_ref, g_ref, w_ref, mk_ref, mv_ref, h_ref):
    s = pl.program_id(1)

    @pl.when(s == 0)
    def _():
        h_ref[...] = _rms_rows(x_ref[...], g_ref[...]).astype(BF16)
        mk_ref[...] = _dot(h_ref[...], w_ref[...])

    @pl.when(s == 1)
    def _():
        mv_ref[...] = _dot(h_ref[...], w_ref[...])


def _mem_kv(mem, g, w_t, *, tm):
    m = mem.shape[0]
    return pl.pallas_call(
        _mem_kv_kernel,
        grid=(m // tm, 2),
        in_specs=[
            pl.BlockSpec((tm, D_MODEL), lambda i, s: (i, 0)),
            pl.BlockSpec((1, D_MODEL), lambda i, s: (0, 0)),
            pl.BlockSpec((D_MODEL, W_QX), lambda i, s: (0, s)),
        ],
        out_specs=[
            pl.BlockSpec((tm, W_QX), lambda i, s: (i, 0)),
            pl.BlockSpec((tm, W_QX), lambda i, s: (i, 0)),
        ],
        out_shape=[jax.ShapeDtypeStruct((m, W_QX), F32)] * 2,
        scratch_shapes=[pltpu.VMEM((tm, D_MODEL), BF16)],
        compiler_params=_params(("parallel", "arbitrary"), 32),
        name="mem_kv",
    )(mem, g, w_t)


def _ln_swish_slabs(y_slabs, cb_ref, lng_ref, lnb_ref, out_dtype):
    ys = [y + cb_ref[:, c * LANES:(c + 1) * LANES] for c, y in enumerate(y_slabs)]
    tot = ys[0]
    for y in ys[1:]:
        tot = tot + y
    mu = jnp.sum(tot, axis=-1, keepdims=True) * (1.0 / C_CONV)
    ds = [y - mu for y in ys]
    sq = ds[0] * ds[0]
    for d in ds[1:]:
        sq = sq + d * d
    rs = lax.rsqrt(jnp.sum(sq, axis=-1, keepdims=True) * (1.0 / C_CONV) + EPS)
    outs = []
    for c, d in enumerate(ds):
        z = d * rs * lng_ref[:, c * LANES:(c + 1) * LANES] + lnb_ref[:, c * LANES:(c + 1) * LANES]
        outs.append((z * jax.nn.sigmoid(z)).astype(out_dtype))
    return outs


def _conv_prompt_kernel(u_ref, w_ref, cb_ref, lng_ref, lnb_ref, c_ref, st_ref, head_ref, y_ref, *, seq, tt):
    pad = 32
    head_ref[:, 0:pad, :] = jnp.zeros((N_SLABS, pad, LANES), F32)
    head_ref[:, pad:pad + tt, :] = u_ref[:, 0:tt, :]

    def tile(src_ref, base, t0):
        def slab(c, carry):
            acc = jnp.zeros((tt, LANES), F32)
            for j in range(SUBLANES):
                taps = range(j, CONV_WIDTH, SUBLANES)
                win = src_ref[c, pl.ds(base + j, tt + taps[-1] - j), :]
                for k in taps:
                    acc = acc + win[k - j:k - j + tt] * w_ref[c, k:k + 1, :]
            y_ref[c] = acc
            return carry

        lax.fori_loop(0, N_SLABS, slab, 0)
        ys = [y_ref[c] for c in range(N_SLABS)]
        outs = _ln_swish_slabs(ys, cb_ref, lng_ref, lnb_ref, BF16)
        for c in range(N_SLABS):
            c_ref[pl.ds(t0, tt), c * LANES:(c + 1) * LANES] = outs[c]

    tile(head_ref, pad - CONV_PREFIX, 0)

    def body(j, carry):
        t0 = pl.multiple_of(j * tt, tt)
        tile(u_ref, t0 - CONV_PREFIX, t0)
        return carry

    lax.fori_loop(1, seq // tt, body, 0)
    for c in range(N_SLABS):
        st_ref[:, c * LANES:(c + 1) * LANES] = u_ref[c, seq - CONV_PREFIX:seq, :]


def _conv_prompt(u_slabs, w_slabs, conv_b, ln_g, ln_b, *, batch, seq):
    tt = 128
    return pl.pallas_call(
        functools.partial(_conv_prompt_kernel, seq=seq, tt=tt),
        grid=(batch,),
        in_specs=[
            pl.BlockSpec((N_SLABS, seq, LANES), lambda b: (0, b, 0)),
            pl.BlockSpec((N_SLABS, CONV_WIDTH, LANES), lambda b: (0, 0, 0)),
            pl.BlockSpec((1, C_CONV), lambda b: (0, 0)),
            pl.BlockSpec((1, C_CONV), lambda b: (0, 0)),
            pl.BlockSpec((1, C_CONV), lambda b: (0, 0)),
        ],
        out_specs=[
            pl.BlockSpec((seq, C_CONV), lambda b: (b, 0)),
            pl.BlockSpec((None, None, CONV_PREFIX, C_CONV), lambda b: (0, b, 0, 0)),
        ],
        out_shape=[
            jax.ShapeDtypeStruct((batch * seq, C_CONV), BF16),
            jax.ShapeDtypeStruct((1, batch, CONV_PREFIX, C_CONV), F32),
        ],
        scratch_shapes=[pltpu.VMEM((N_SLABS, 32 + tt, LANES), F32), pltpu.VMEM((N_SLABS, tt, LANES), F32)],
        compiler_params=_params(("parallel",), 40),
        name="conv_prompt",
    )(u_slabs, w_slabs, conv_b, ln_g, ln_b)


def _conv_sample_kernel(state_ref, u_ref, w_ref, cb_ref, lng_ref, lnb_ref, c_ref, st_ref, *, n_seq):
    w_hist = w_ref[0:CONV_PREFIX, :]
    w_new = w_ref[CONV_PREFIX:CONV_WIDTH, :]
    for n in range(n_seq):
        hist = state_ref[n]
        new = u_ref[n:n + 1, :]
        y = jnp.sum(hist * w_hist, axis=0, keepdims=True) + new * w_new
        ys = [y[:, c * LANES:(c + 1) * LANES] for c in range(N_SLABS)]
        outs = _ln_swish_slabs(ys, cb_ref, lng_ref, lnb_ref, F32)
        for c in range(N_SLABS):
            c_ref[n:n + 1, c * LANES:(c + 1) * LANES] = outs[c]
        st_ref[n, 0:CONV_PREFIX - 1, :] = state_ref[n, 1:CONV_PREFIX, :]
        st_ref[n, CONV_PREFIX - 1:CONV_PREFIX, :] = new


def _conv_sample(state, u, conv_w, conv_b, ln_g, ln_b):
    n_seq = state.shape[0]
    return pl.pallas_call(
        functools.partial(_conv_sample_kernel, n_seq=n_seq),
        out_shape=[
            jax.ShapeDtypeStruct((n_seq, C_CONV), F32),
            jax.ShapeDtypeStruct((n_seq, CONV_PREFIX, C_CONV), F32),
        ],
        compiler_params=pltpu.CompilerParams(vmem_limit_bytes=32 * MIB),
        name="conv_sample",
    )(state, u, conv_w, conv_b, ln_g, ln_b)


def _softmax_pv(s, v):
    m = jnp.max(s, axis=-1, keepdims=True)
    p = jnp.exp(s - m)
    l = jnp.sum(p, axis=-1, keepdims=True)
    o = _dot(p.astype(BF16), v) * (1.0 / l)
    return o, m + jnp.log(l)


def _qk(q, k):
    return lax.dot_general(q, k, (((1,), (1,)), ((), ())), preferred_element_type=F32)


def _attn_prompt_kernel(slopes_ref, q0_ref, q1_ref, q2_ref, k_ref, v_ref, qx_ref, mk_ref, mv_ref,
                        mix_ref, xm_ref, bias_ref, o_ref, lse_ref, *, seq):
    head = pl.program_id(1)
    qi = lax.broadcasted_iota(jnp.int32, (SPAN, 2 * SPAN), 0)
    ci = lax.broadcasted_iota(jnp.int32, (SPAN, 2 * SPAN), 1)
    dist = qi + SPAN - ci
    valid = (dist >= 0) & (dist <= SPAN)
    for g, (_, dil) in enumerate(DIL_GROUPS):
        bias = -slopes_ref[g, head] * (dist * dil).astype(F32)
        bias_ref[g] = jnp.where(valid, bias, NEG)

    def rows(ref, start, size, dil):
        if dil == 1:
            return ref[pl.ds(start, size), :]
        return ref[pl.ds(start, size, stride=dil), :]

    def put(ref, g, start, dil, val):
        if dil == 1:
            ref[g, pl.ds(start, SPAN), :] = val
        else:
            ref[g, pl.ds(start, SPAN, stride=dil), :] = val

    def unit(g, dil, q_ref, start, n_keys):
        k_start = start - (n_keys - SPAN) * dil
        q = rows(q_ref, start, SPAN, dil).astype(BF16)
        k = rows(k_ref, k_start, n_keys, dil).astype(BF16)
        v = rows(v_ref, k_start, n_keys, dil).astype(BF16)
        s = _qk(q, k) * SCALE + bias_ref[g, :, 2 * SPAN - n_keys:2 * SPAN]
        o, lse = _softmax_pv(s, v)
        put(o_ref, g, start, dil, o)
        put(lse_ref, g, start, dil, jnp.broadcast_to(lse, (SPAN, LANES)))

    q_refs = (q0_ref, q1_ref, q2_ref)
    for g, (_, dil) in enumerate(DIL_GROUPS):
        n_blocks = seq // dil // SPAN
        q_ref = q_refs[g]
        if n_blocks == 1:
            def first_blocks(r, carry, g=g, dil=dil, q_ref=q_ref):
                unit(g, dil, q_ref, r, SPAN)
                return carry
            lax.fori_loop(0, dil, first_blocks, 0, unroll=8)
        elif dil == 1:
            unit(g, dil, q_ref, 0, SPAN)

            def later_blocks(n, carry, g=g, dil=dil, q_ref=q_ref):
                unit(g, dil, q_ref, pl.multiple_of(n * SPAN, SPAN), 2 * SPAN)
                return carry
            lax.fori_loop(1, n_blocks, later_blocks, 0, unroll=5)
        else:
            def residue(r, carry, g=g, dil=dil, q_ref=q_ref, n_blocks=n_blocks):
                unit(g, dil, q_ref, r, SPAN)
                for n in range(1, n_blocks):
                    unit(g, dil, q_ref, r + n * (SPAN * dil), 2 * SPAN)
                return carry
            lax.fori_loop(0, dil, residue, 0, unroll=2)

    tq = 256
    mk = mk_ref[...].astype(BF16)
    mv = mv_ref[...].astype(BF16)

    def finish(j, carry):
        t0 = pl.multiple_of(j * tq, tq)
        l0 = lse_ref[0, pl.ds(t0, tq), :]
        l1 = lse_ref[1, pl.ds(t0, tq), :]
        l2 = lse_ref[2, pl.ds(t0, tq), :]
        top = jnp.maximum(jnp.maximum(l0, l1), l2)
        w0 = jnp.exp(l0 - top)
        w1 = jnp.exp(l1 - top)
        w2 = jnp.exp(l2 - top)
        num = (w0 * o_ref[0, pl.ds(t0, tq), :] + w1 * o_ref[1, pl.ds(t0, tq), :]
               + w2 * o_ref[2, pl.ds(t0, tq), :])
        mix_ref[pl.ds(t0, tq), :] = (num / (w0 + w1 + w2)).astype(BF16)
        sx = _qk(qx_ref[pl.ds(t0, tq), :], mk) * SCALE
        ox, _ = _softmax_pv(sx, mv)
        xm_ref[pl.ds(t0, tq), :] = ox.astype(BF16)
        return carry

    lax.fori_loop(0, seq // tq, finish, 0, unroll=4)


def _attn_prompt(slopes, q, k, v, qx, mk, mv, *, batch, seq):
    def col(c):
        return pl.BlockSpec((None, seq, LANES), lambda b, h, c=c: (b, 0, c * N_KV + h))

    head_spec = pl.BlockSpec((None, seq, LANES), lambda b, h: (b, 0, h))
    mem_spec = pl.BlockSpec((None, N_MEM, LANES), lambda b, h: (b, 0, h))
    return pl.pallas_call(
        functools.partial(_attn_prompt_kernel, seq=seq),
        grid=(batch, N_KV),
        in_specs=[
            pl.BlockSpec(memory_space=pltpu.SMEM),
            col(0), col(1), col(2), head_spec, head_spec, head_spec, mem_spec, mem_spec,
        ],
        out_specs=[head_spec, head_spec],
        out_shape=[jax.ShapeDtypeStruct((batch, seq, W_KV), BF16),
                   jax.ShapeDtypeStruct((batch, seq, W_QX), BF16)],
        scratch_shapes=[
            pltpu.VMEM((N_GROUPS, SPAN, 2 * SPAN), F32),
            pltpu.VMEM((N_GROUPS, seq, LANES), F32),
            pltpu.VMEM((N_GROUPS, seq, LANES), F32),
        ],
        compiler_params=_params(("parallel", "parallel"), 40),
        name="attn_prompt",
    )(slopes, q, q, q, k, v, qx, mk, mv)


def _attn_sample_kernel(slopes_ref, q_ref, kn_ref, vn_ref, qx_ref, ck0_ref, ck1_ref, ck2_ref,
                        cv0_ref, cv1_ref, cv2_ref, mk_ref, mv_ref, mix_ref, xm_ref, *, bs):
    ck_refs = (ck0_ref, ck1_ref, ck2_ref)
    cv_refs = (cv0_ref, cv1_ref, cv2_ref)
    back = (SPAN - lax.broadcasted_iota(jnp.int32, (SPAN, 1), 0)).astype(F32)

    def one(n, carry):
        for h in range(N_KV):
            hs = slice(h * LANES, (h + 1) * LANES)
            kn = kn_ref[n, :, hs]
            vn = vn_ref[n, :, hs]
            scores, s_new = [], []
            for g, (_, dil) in enumerate(DIL_GROUPS):
                qs = slice(g * W_KV + h * LANES, g * W_KV + (h + 1) * LANES)
                q = q_ref[n, :, qs]
                s = jnp.sum(ck_refs[g][n, :, h, :] * q, axis=-1, keepdims=True) * SCALE
                scores.append(s - slopes_ref[g, h] * (back * float(dil)))
                s_new.append(jnp.sum(kn * q, axis=-1, keepdims=True) * SCALE)
            top = s_new[0]
            for g in range(N_GROUPS):
                top = jnp.maximum(top, jnp.maximum(s_new[g], jnp.max(scores[g], axis=0, keepdims=True)))
            num = jnp.zeros((1, LANES), F32)
            den = jnp.zeros((1, 1), F32)
            for g in range(N_GROUPS):
                p = jnp.exp(scores[g] - top)
                p_new = jnp.exp(s_new[g] - top)
                num = num + jnp.sum(p * cv_refs[g][n, :, h, :], axis=0, keepdims=True) + p_new * vn
                den = den + jnp.sum(p, axis=0, keepdims=True) + p_new
            mix_ref[n, :, hs] = num / den
            qx = qx_ref[n, :, hs]
            sx = jnp.sum(mk_ref[n, :, h, :] * qx, axis=-1, keepdims=True) * SCALE
            px = jnp.exp(sx - jnp.max(sx, axis=0, keepdims=True))
            ox = jnp.sum(px * mv_ref[n, :, h, :], axis=0, keepdims=True) / jnp.sum(px, axis=0, keepdims=True)
            xm_ref[n, :, hs] = ox
        return carry

    lax.fori_loop(0, bs, one, 0)


def _attn_sample(slopes, q, kn, vn, qx, cache_k, cache_v, mem_k, mem_v):
    n_seq, win = cache_k.shape[0], cache_k.shape[1]
    bs = 2
    row = lambda w: pl.BlockSpec((bs, 1, w), lambda i: (i, 0, 0))
    as_rows = lambda a: a.reshape(n_seq, 1, a.shape[-1])
    views_k, views_v, cache_specs = [], [], []
    for _, dil in DIL_GROUPS:
        views_k.append(cache_k.reshape(n_seq, win // dil, dil * N_KV, HEAD_DIM))
        views_v.append(cache_v.reshape(n_seq, win // dil, dil * N_KV, HEAD_DIM))
        rows = N_KV if dil == 1 else SUBLANES
        cache_specs.append(pl.BlockSpec((bs, SPAN, rows, HEAD_DIM),
                                        lambda i, nb=win // dil // SPAN: (i, nb - 1, 0, 0)))
    mem_spec = pl.BlockSpec((bs, N_MEM, N_XHEADS, HEAD_DIM), lambda i: (i, 0, 0, 0))
    return pl.pallas_call(
        functools.partial(_attn_sample_kernel, bs=bs),
        grid=(n_seq // bs,),
        in_specs=[pl.BlockSpec(memory_space=pltpu.SMEM), row(W_QD), row(W_KV), row(W_KV), row(W_QX)]
        + cache_specs + cache_specs + [mem_spec, mem_spec],
        out_specs=[row(W_KV), row(W_QX)],
        out_shape=[jax.ShapeDtypeStruct((n_seq, 1, W_KV), F32), jax.ShapeDtypeStruct((n_seq, 1, W_QX), F32)],
        compiler_params=_params(("parallel",), 40),
        name="attn_sample",
    )(slopes, as_rows(q), as_rows(kn), as_rows(vn), as_rows(qx), *views_k, *views_v, mem_k, mem_v)


def _merge_kernel(h_ref, c_ref, mix_ref, xm_ref, wg0_ref, wg1_ref, wg2_ref, bg_ref, wc_ref, wd_ref, wx_ref,
                  wo_ref, z_ref, merged_ref):
    s = pl.program_id(1)

    @pl.when(s < _D_TILES)
    def _():
        h = h_ref[...]
        acc = None
        for br, (wg_ref, y_ref, wy_ref) in enumerate(
                ((wg0_ref, c_ref, wc_ref), (wg1_ref, mix_ref, wd_ref), (wg2_ref, xm_ref, wx_ref))):
            gate = jax.nn.sigmoid(_dot(h, wg_ref[...]) + bg_ref[br])
            term = gate * _dot(y_ref[...].astype(BF16), wy_ref[...])
            acc = term if acc is None else acc + term
        merged_ref[s] = acc.astype(BF16)

    @pl.when(s >= _D_TILES)
    def _():
        z = None
        for j in range(_D_TILES):
            part = _dot(merged_ref[j], wo_ref[j * TN:(j + 1) * TN, :])
            z = part if z is None else z + part
        z_ref[...] = z


def _merge(h, c, mix, xm, w_in_t, b_gate, w_conv_out_t, w_dil_o_t, w_x_o_t, w_out_t, *, tm):
    m = h.shape[0]
    n1, n2 = _D_TILES, D_MODEL // TN_OUT
    last = n1 - 1

    def gate_spec(br):
        return pl.BlockSpec((D_MODEL, TN),
                            lambda i, s, br=br: (0, _GATE_TILE0 + br * _D_TILES + jnp.minimum(s, last)))

    def colw(kdim):
        return pl.BlockSpec((kdim, TN), lambda i, s: (0, jnp.minimum(s, last)))

    def rowblk(w):
        return pl.BlockSpec((tm, w), lambda i, s: (i, 0))

    return pl.pallas_call(
        _merge_kernel,
        grid=(m // tm, n1 + n2),
        in_specs=[
            rowblk(D_MODEL), rowblk(C_CONV), rowblk(W_KV), rowblk(W_QX),
            gate_spec(0), gate_spec(1), gate_spec(2),
            pl.BlockSpec((3, 1, TN), lambda i, s: (0, 0, jnp.minimum(s, last))),
            colw(C_CONV), colw(W_KV), colw(W_QX),
            pl.BlockSpec((D_MODEL, TN_OUT), lambda i, s: (0, jnp.maximum(s - n1, 0))),
        ],
        out_specs=pl.BlockSpec((tm, TN_OUT), lambda i, s: (i, jnp.maximum(s - n1, 0))),
        out_shape=jax.ShapeDtypeStruct((m, D_MODEL), F32),
        scratch_shapes=[pltpu.VMEM((n1, tm, TN), BF16)],
        compiler_params=_params(("parallel", "arbitrary"), 58),
        name="merge",
    )(h, c, mix, xm, w_in_t, w_in_t, w_in_t, b_gate.reshape(3, 1, D_MODEL), w_conv_out_t, w_dil_o_t, w_x_o_t,
      w_out_t)


def _ffn_kernel(x_ref, z_ref, gmix_ref, gpre_ref, wg_ref, wu_ref, wd_ref, gpost_ref, out_ref, x1_ref, h_ref,
                *, n_steps):
    s = pl.program_id(1)

    @pl.when(s == 0)
    def _():
        x1 = x_ref[...] + _rms_rows(z_ref[...], gmix_ref[...])
        x1_ref[...] = x1
        h_ref[...] = _rms_rows(x1, gpre_ref[...]).astype(BF16)
        out_ref[...] = jnp.zeros_like(out_ref)

    h = h_ref[...]
    gate = _dot(h, wg_ref[...])
    act = (gate * jax.nn.sigmoid(gate) * _dot(h, wu_ref[...])).astype(BF16)
    out_ref[...] += _dot(act, wd_ref[...])

    @pl.when(s == n_steps - 1)
    def _():
        out_ref[...] = x1_ref[...] + _rms_rows(out_ref[...], gpost_ref[...])


def _ffn(x, z, g_mix, g_pre, w_gate_t, w_up_t, w_down, g_post, *, tm):
    m = x.shape[0]
    n_steps = D_FF // TF
    rowblk = pl.BlockSpec((tm, D_MODEL), lambda i, s: (i, 0))
    gain = pl.BlockSpec((1, D_MODEL), lambda i, s: (0, 0))
    return pl.pallas_call(
        functools.partial(_ffn_kernel, n_steps=n_steps),
        grid=(m // tm, n_steps),
        in_specs=[
            rowblk, rowblk, gain, gain,
            pl.BlockSpec((D_MODEL, TF), lambda i, s: (0, s)),
            pl.BlockSpec((D_MODEL, TF), lambda i, s: (0, s)),
            pl.BlockSpec((TF, D_MODEL), lambda i, s: (s, 0)),
            gain,
        ],
        out_specs=rowblk,
        out_shape=jax.ShapeDtypeStruct((m, D_MODEL), F32),
        scratch_shapes=[pltpu.VMEM((tm, D_MODEL), F32), pltpu.VMEM((tm, D_MODEL), BF16)],
        compiler_params=_params(("parallel", "arbitrary"), 52),
        name="ffn",
    )(x, z, g_mix, g_pre, w_gate_t, w_up_t, w_down, g_post)


def kernel(x_prompt, x_sample, cache_win_k, cache_win_v, state_conv, cache_mem_k, cache_mem_v, mem_prompt,
           g_pre_mix, w_in, b_gate, conv_w, conv_b, conv_ln_g, conv_ln_b, w_conv_out, w_dil_o, g_mem, w_mem_kv,
           w_x_o, w_out, g_post_mix, g_pre_ffn, w_ffn_gate, w_ffn_up, w_ffn_down, g_post_ffn):
    batch, seq, _ = x_prompt.shape
    n_seq = x_sample.shape[0]
    depth = w_in.shape[0]
    assert depth == 1 and x_sample.shape[1] == 1 and seq % (DIL_GROUPS[-1][1] * SPAN) == 0
    assert cache_win_k.shape[2] == DIL_GROUPS[-1][0]

    idx = jnp.arange(1, N_QH + 1, dtype=F32)
    slopes = jnp.exp2(-8.0 * idx / N_QH).reshape(N_GROUPS, N_KV)

    xp, xs = x_prompt.reshape(batch * seq, D_MODEL), x_sample.reshape(n_seq, D_MODEL)
    l = 0
    cast = lambda w: w[l].astype(BF16)
    w_in_t, w_conv_out_t, w_dil_o_t, w_x_o_t = cast(w_in), cast(w_conv_out), cast(w_dil_o), cast(w_x_o)
    w_out_t, w_mem_kv_t = cast(w_out), cast(w_mem_kv)
    w_gate_t, w_up_t, w_down_b = cast(w_ffn_gate), cast(w_ffn_up), cast(w_ffn_down)
    row = lambda a: a[l].reshape(1, -1)
    g_pre, g_post, g_ffn_pre, g_ffn_post = row(g_pre_mix), row(g_post_mix), row(g_pre_ffn), row(g_post_ffn)
    cb, lng, lnb = row(conv_b), row(conv_ln_g), row(conv_ln_b)
    conv_w_slabs = conv_w[l].reshape(CONV_WIDTH, N_SLABS, LANES).transpose(1, 0, 2)

    def dense_tail(h, c, mix, xm, x, tm_merge, tm_ffn):
        z = _merge(h, c, mix, xm, w_in_t, b_gate[l], w_conv_out_t, w_dil_o_t, w_x_o_t, w_out_t, tm=tm_merge)
        return _ffn(x, z, g_post, g_ffn_pre, w_gate_t, w_up_t, w_down_b, g_ffn_post, tm=tm_ffn)

    mk, mv = _mem_kv(mem_prompt.reshape(batch * N_MEM, D_MODEL), row(g_mem), w_mem_kv_t, tm=512)
    h_p, u_p, q_p, k_p, v_p, qx_p = _in_proj(xp, g_pre, w_in_t, tm=1024, slabs=True)
    c_p, conv_p = _conv_prompt(u_p, conv_w_slabs, cb, lng, lnb, batch=batch, seq=seq)
    mix_p, xm_p = _attn_prompt(slopes, q_p.reshape(batch, seq, W_QD), k_p.reshape(batch, seq, W_KV),
                               v_p.reshape(batch, seq, W_KV), qx_p.reshape(batch, seq, W_QX),
                               mk.reshape(batch, N_MEM, W_QX), mv.reshape(batch, N_MEM, W_QX),
                               batch=batch, seq=seq)
    y_p = dense_tail(h_p, c_p, mix_p.reshape(batch * seq, W_KV), xm_p.reshape(batch * seq, W_QX), xp, 1024, 512)

    h_s, u_s, q_s, k_s, v_s, qx_s = _in_proj(xs, g_pre, w_in_t, tm=n_seq, slabs=False)
    c_s, conv_s = _conv_sample(state_conv[l], u_s, conv_w[l], cb, lng, lnb)
    mix_s, xm_s = _attn_sample(slopes, q_s, k_s, v_s, qx_s,
                               cache_win_k[l], cache_win_v[l], cache_mem_k[l], cache_mem_v[l])
    y_s = dense_tail(h_s, c_s, mix_s.reshape(n_seq, W_KV), xm_s.reshape(n_seq, W_QX), xs, n_seq, n_seq)

    win = min(DIL_GROUPS[-1][0], seq)
    kv_shape = (1, batch, seq, N_KV, HEAD_DIM)
    mem_shape = (1, batch, N_MEM, N_XHEADS, HEAD_DIM)
    new_shape = (1, n_seq, 1, N_KV, HEAD_DIM)
    return (y_p.reshape(batch, seq, D_MODEL), y_s.reshape(n_seq, 1, D_MODEL),
            k_p.reshape(kv_shape)[:, :, seq - win:], v_p.reshape(kv_shape)[:, :, seq - win:], conv_p,
            mk.reshape(mem_shape), mv.reshape(mem_shape),
            k_s.reshape(new_shape), v_s.reshape(new_shape), conv_s.reshape(1, n_seq, CONV_PREFIX, C_CONV))
```

```python
import functools

import jax
import jax.numpy as jnp
from jax import lax
from jax.experimental import pallas as pl
from jax.experimental.pallas import tpu as pltpu

F32 = jnp.float32
BF16 = jnp.bfloat16

D_MODEL = 2048
C_CONV = D_MODEL // 2
CONV_WIDTH = 31
CONV_PREFIX = CONV_WIDTH - 1
HEAD_DIM = 128
N_KV = 4
DIL_GROUPS = ((128, 1), (512, 4), (2048, 16))
N_GROUPS = len(DIL_GROUPS)
N_QH = N_GROUPS * N_KV
SPAN = 128
N_MEM = 256
N_XHEADS = 4
W_QD = N_QH * HEAD_DIM
W_KV = N_KV * HEAD_DIM
W_QX = N_XHEADS * HEAD_DIM
D_FF = 5632
EPS = 1e-6
NEG = -1e30
SCALE = HEAD_DIM ** -0.5
LANES = 128
SUBLANES = 8
N_SLABS = C_CONV // LANES
MIB = 1024 * 1024

TN = 512
_A_TILES = C_CONV // TN
_Q_TILES = W_QD // TN
_KV_TILES = W_KV // TN
_QX_TILES = W_QX // TN
_STEP_Q0 = _A_TILES
_STEP_K0 = _STEP_Q0 + _Q_TILES
_STEP_V0 = _STEP_K0 + _KV_TILES
_STEP_X0 = _STEP_V0 + _KV_TILES
_IN_STEPS = _STEP_X0 + _QX_TILES
_GATE_TILE0 = 2 * _A_TILES + _Q_TILES + 2 * _KV_TILES + _QX_TILES
_D_TILES = D_MODEL // TN
TN_OUT = 512
TF = 512


def _params(semantics, vmem_mib):
    return pltpu.CompilerParams(dimension_semantics=semantics, vmem_limit_bytes=vmem_mib * MIB)


def _rms_rows(x, g):
    return x * lax.rsqrt(jnp.mean(x * x, axis=-1, keepdims=True) + EPS) * g


def _dot(a, b):
    return jnp.dot(a, b, preferred_element_type=F32)


def _in_proj_kernel(x_ref, g_ref, w1_ref, w2_ref, h_ref, u_ref, q_ref, k_ref, v_ref, qx_ref, *, slabs):
    s = pl.program_id(1)

    @pl.when(s == 0)
    def _():
        h_ref[...] = _rms_rows(x_ref[...], g_ref[...]).astype(BF16)

    @pl.when(s < _STEP_Q0)
    def _():
        h = h_ref[...]
        u = _dot(h, w1_ref[...]) * jax.nn.sigmoid(_dot(h, w2_ref[...]))
        if slabs:
            for c in range(TN // LANES):
                u_ref[c] = u[:, c * LANES:(c + 1) * LANES]
        else:
            u_ref[...] = u

    @pl.when((s >= _STEP_Q0) & (s < _STEP_K0))
    def _():
        q_ref[...] = _dot(h_ref[...], w1_ref[...])

    @pl.when((s >= _STEP_K0) & (s < _STEP_V0))
    def _():
        k_ref[...] = _dot(h_ref[...], w1_ref[...])

    @pl.when((s >= _STEP_V0) & (s < _STEP_X0))
    def _():
        v_ref[...] = _dot(h_ref[...], w1_ref[...])

    @pl.when(s >= _STEP_X0)
    def _():
        qx_ref[...] = _dot(h_ref[...], w1_ref[...]).astype(qx_ref.dtype)


def _in_proj(x, g, w_in_t, *, tm, slabs):
    m = x.shape[0]
    last_a = _A_TILES - 1
    if slabs:
        u_shape = jax.ShapeDtypeStruct((N_SLABS, m, LANES), F32)
        u_spec = pl.BlockSpec((TN // LANES, tm, LANES), lambda i, s: (jnp.minimum(s, last_a), i, 0))
    else:
        u_shape = jax.ShapeDtypeStruct((m, C_CONV), F32)
        u_spec = pl.BlockSpec((tm, TN), lambda i, s: (i, jnp.minimum(s, last_a)))

    def out_tiles(step0, n_tiles):
        return pl.BlockSpec((tm, TN), lambda i, s: (i, jnp.clip(s - step0, 0, n_tiles - 1)))

    return pl.pallas_call(
        functools.partial(_in_proj_kernel, slabs=slabs),
        grid=(m // tm, _IN_STEPS),
        in_specs=[
            pl.BlockSpec((tm, D_MODEL), lambda i, s: (i, 0)),
            pl.BlockSpec((1, D_MODEL), lambda i, s: (0, 0)),
            pl.BlockSpec((D_MODEL, TN), lambda i, s: (0, jnp.where(s < _A_TILES, s, s + _A_TILES))),
            pl.BlockSpec((D_MODEL, TN), lambda i, s: (0, jnp.minimum(s, last_a) + _A_TILES)),
        ],
        out_specs=[
            pl.BlockSpec((tm, D_MODEL), lambda i, s: (i, 0)),
            u_spec,
            out_tiles(_STEP_Q0, _Q_TILES),
            out_tiles(_STEP_K0, _KV_TILES),
            out_tiles(_STEP_V0, _KV_TILES),
            out_tiles(_STEP_X0, _QX_TILES),
        ],
        out_shape=[
            jax.ShapeDtypeStruct((m, D_MODEL), BF16),
            u_shape,
            jax.ShapeDtypeStruct((m, W_QD), F32),
            jax.ShapeDtypeStruct((m, W_KV), F32),
            jax.ShapeDtypeStruct((m, W_KV), F32),
            jax.ShapeDtypeStruct((m, W_QX), BF16 if slabs else F32),
        ],
        compiler_params=_params(("parallel", "arbitrary"), 58),
        name="in_proj",
    )(x, g, w_in_t, w_in_t)


def _mem_kv_kernel(x_ref, g_ref, w_ref, mk_ref, mv_ref, h_ref):
    s = pl.program_id(1)

    @pl.when(s == 0)
    def _():
        h_ref[...] = _rms_rows(x_ref[...], g_ref[...]).astype(BF16)
        mk_ref[...] = _dot(h_ref[...], w_ref[...])

    @pl.when(s == 1)
    def _():
        mv_ref[...] = _dot(h_ref[...], w_ref[...])


def _mem_kv(mem, g, w_t, *, tm):
    m = mem.shape[0]
    return pl.pallas_call(
        _mem_kv_kernel,
        grid=(m // tm, 2),
        in_specs=[
            pl.BlockSpec((tm, D_MODEL), lambda i, s: (i, 0)),
            pl.BlockSpec((1, D_MODEL), lambda i, s: (0, 0)),
            pl.BlockSpec((D_MODEL, W_QX), lambda i, s: (0, s)),
        ],
        out_specs=[
            pl.BlockSpec((tm, W_QX), lambda i, s: (i, 0)),
            pl.BlockSpec((tm, W_QX), lambda i, s: (i, 0)),
        ],
        out_shape=[jax.ShapeDtypeStruct((m, W_QX), F32)] * 2,
        scratch_shapes=[pltpu.VMEM((tm, D_MODEL), BF16)],
        compiler_params=_params(("parallel", "arbitrary"), 32),
        name="mem_kv",
    )(mem, g, w_t)


def _ln_swish_slabs(y_slabs, cb_ref, lng_ref, lnb_ref, out_dtype):
    ys = [y + cb_ref[:, c * LANES:(c + 1) * LANES] for c, y in enumerate(y_slabs)]
    tot = ys[0]
    for y in ys[1:]:
        tot = tot + y
    mu = jnp.sum(tot, axis=-1, keepdims=True) * (1.0 / C_CONV)
    ds = [y - mu for y in ys]
    sq = ds[0] * ds[0]
    for d in ds[1:]:
        sq = sq + d * d
    rs = lax.rsqrt(jnp.sum(sq, axis=-1, keepdims=True) * (1.0 / C_CONV) + EPS)
    outs = []
    for c, d in enumerate(ds):
        z = d * rs * lng_ref[:, c * LANES:(c + 1) * LANES] + lnb_ref[:, c * LANES:(c + 1) * LANES]
        outs.append((z * jax.nn.sigmoid(z)).astype(out_dtype))
    return outs


def _conv_prompt_kernel(u_ref, w_ref, cb_ref, lng_ref, lnb_ref, c_ref, st_ref, head_ref, y_ref, *, seq, tt):
    pad = 32
    head_ref[:, 0:pad, :] = jnp.zeros((N_SLABS, pad, LANES), F32)
    head_ref[:, pad:pad + tt, :] = u_ref[:, 0:tt, :]

    def tile(src_ref, base, t0):
        def slab(c, carry):
            acc = jnp.zeros((tt, LANES), F32)
            for j in range(SUBLANES):
                taps = range(j, CONV_WIDTH, SUBLANES)
                win = src_ref[c, pl.ds(base + j, tt + taps[-1] - j), :]
                for k in taps:
                    acc = acc + win[k - j:k - j + tt] * w_ref[c, k:k + 1, :]
            y_ref[c] = acc
            return carry

        lax.fori_loop(0, N_SLABS, slab, 0)
        ys = [y_ref[c] for c in range(N_SLABS)]
        outs = _ln_swish_slabs(ys, cb_ref, lng_ref, lnb_ref, BF16)
        for c in range(N_SLABS):
            c_ref[pl.ds(t0, tt), c * LANES:(c + 1) * LANES] = outs[c]

    tile(head_ref, pad - CONV_PREFIX, 0)

    def body(j, carry):
        t0 = pl.multiple_of(j * tt, tt)
        tile(u_ref, t0 - CONV_PREFIX, t0)
        return carry

    lax.fori_loop(1, seq // tt, body, 0)
    for c in range(N_SLABS):
        st_ref[:, c * LANES:(c + 1) * LANES] = u_ref[c, seq - CONV_PREFIX:seq, :]


def _conv_prompt(u_slabs, w_slabs, conv_b, ln_g, ln_b, *, batch, seq):
    tt = 128
    return pl.pallas_call(
        functools.partial(_conv_prompt_kernel, seq=seq, tt=tt),
        grid=(batch,),
        in_specs=[
            pl.BlockSpec((N_SLABS, seq, LANES), lambda b: (0, b, 0)),
            pl.BlockSpec((N_SLABS, CONV_WIDTH, LANES), lambda b: (0, 0, 0)),
            pl.BlockSpec((1, C_CONV), lambda b: (0, 0)),
            pl.BlockSpec((1, C_CONV), lambda b: (0, 0)),
            pl.BlockSpec((1, C_CONV), lambda b: (0, 0)),
        ],
        out_specs=[
            pl.BlockSpec((seq, C_CONV), lambda b: (b, 0)),
            pl.BlockSpec((None, None, CONV_PREFIX, C_CONV), lambda b: (0, b, 0, 0)),
        ],
        out_shape=[
            jax.ShapeDtypeStruct((batch * seq, C_CONV), BF16),
            jax.ShapeDtypeStruct((1, batch, CONV_PREFIX, C_CONV), F32),
        ],
        scratch_shapes=[pltpu.VMEM((N_SLABS, 32 + tt, LANES), F32), pltpu.VMEM((N_SLABS, tt, LANES), F32)],
        compiler_params=_params(("parallel",), 40),
        name="conv_prompt",
    )(u_slabs, w_slabs, conv_b, ln_g, ln_b)


def _conv_sample_kernel(state_ref, u_ref, w_ref, cb_ref, lng_ref, lnb_ref, c_ref, st_ref, *, n_seq):
    w_hist = w_ref[0:CONV_PREFIX, :]
    w_new = w_ref[CONV_PREFIX:CONV_WIDTH, :]
    for n in range(n_seq):
        hist = state_ref[n]
        new = u_ref[n:n + 1, :]
        y = jnp.sum(hist * w_hist, axis=0, keepdims=True) + new * w_new
        ys = [y[:, c * LANES:(c + 1) * LANES] for c in range(N_SLABS)]
        outs = _ln_swish_slabs(ys, cb_ref, lng_ref, lnb_ref, F32)
        for c in range(N_SLABS):
            c_ref[n:n + 1, c * LANES:(c + 1) * LANES] = outs[c]
        st_ref[n, 0:CONV_PREFIX - 1, :] = state_ref[n, 1:CONV_PREFIX, :]
        st_ref[n, CONV_PREFIX - 1:CONV_PREFIX, :] = new


def _conv_sample(state, u, conv_w, conv_b, ln_g, ln_b):
    n_seq = state.shape[0]
    return pl.pallas_call(
        functools.partial(_conv_sample_kernel, n_seq=n_seq),
        out_shape=[
            jax.ShapeDtypeStruct((n_seq, C_CONV), F32),
            jax.ShapeDtypeStruct((n_seq, CONV_PREFIX, C_CONV), F32),
        ],
        compiler_params=pltpu.CompilerParams(vmem_limit_bytes=32 * MIB),
        name="conv_sample",
    )(state, u, conv_w, conv_b, ln_g, ln_b)


def _softmax_pv(s, v):
    m = jnp.max(s, axis=-1, keepdims=True)
    p = jnp.exp(s - m)
    l = jnp.sum(p, axis=-1, keepdims=True)
    o = _dot(p.astype(BF16), v) * (1.0 / l)
    return o, m + jnp.log(l)


def _qk(q, k):
    return lax.dot_general(q, k, (((1,), (1,)), ((), ())), preferred_element_type=F32)


def _attn_prompt_kernel(slopes_ref, q0_ref, q1_ref, q2_ref, k_ref, v_ref, qx_ref, mk_ref, mv_ref,
                        wg_ref, wu_ref, wd_ref, mix_ref, xm_ref, wg_out_ref, wu_out_ref, wd_out_ref,
                        bias_ref, o_ref, lse_ref, *, seq):
    head = pl.program_id(1)
    wg_out_ref[...] = wg_ref[...].astype(BF16)
    wu_out_ref[...] = wu_ref[...].astype(BF16)
    wd_out_ref[...] = wd_ref[...].astype(BF16)
    qi = lax.broadcasted_iota(jnp.int32, (SPAN, 2 * SPAN), 0)
    ci = lax.broadcasted_iota(jnp.int32, (SPAN, 2 * SPAN), 1)
    dist = qi + SPAN - ci
    valid = (dist >= 0) & (dist <= SPAN)
    for g, (_, dil) in enumerate(DIL_GROUPS):
        bias = -slopes_ref[g, head] * (dist * dil).astype(F32)
        bias_ref[g] = jnp.where(valid, bias, NEG)

    def rows(ref, start, size, dil):
        if dil == 1:
            return ref[pl.ds(start, size), :]
        return ref[pl.ds(start, size, stride=dil), :]

    def put(ref, g, start, dil, val):
        if dil == 1:
            ref[g, pl.ds(start, SPAN), :] = val
        else:
            ref[g, pl.ds(start, SPAN, stride=dil), :] = val

    def unit(g, dil, q_ref, start, n_keys):
        k_start = start - (n_keys - SPAN) * dil
        q = rows(q_ref, start, SPAN, dil).astype(BF16)
        k = rows(k_ref, k_start, n_keys, dil).astype(BF16)
        v = rows(v_ref, k_start, n_keys, dil).astype(BF16)
        s = _qk(q, k) * SCALE + bias_ref[g, :, 2 * SPAN - n_keys:2 * SPAN]
        o, lse = _softmax_pv(s, v)
        put(o_ref, g, start, dil, o)
        put(lse_ref, g, start, dil, jnp.broadcast_to(lse, (SPAN, LANES)))

    q_refs = (q0_ref, q1_ref, q2_ref)
    for g, (_, dil) in enumerate(DIL_GROUPS):
        n_blocks = seq // dil // SPAN
        q_ref = q_refs[g]
        if n_blocks == 1:
            def first_blocks(r, carry, g=g, dil=dil, q_ref=q_ref):
                unit(g, dil, q_ref, r, SPAN)
                return carry
            lax.fori_loop(0, dil, first_blocks, 0, unroll=8)
        elif dil == 1:
            unit(g, dil, q_ref, 0, SPAN)

            def later_blocks(n, carry, g=g, dil=dil, q_ref=q_ref):
                unit(g, dil, q_ref, pl.multiple_of(n * SPAN, SPAN), 2 * SPAN)
                return carry
            lax.fori_loop(1, n_blocks, later_blocks, 0, unroll=5)
        else:
            def residue(r, carry, g=g, dil=dil, q_ref=q_ref, n_blocks=n_blocks):
                unit(g, dil, q_ref, r, SPAN)
                for n in range(1, n_blocks):
                    unit(g, dil, q_ref, r + n * (SPAN * dil), 2 * SPAN)
                return carry
            lax.fori_loop(0, dil, residue, 0, unroll=2)

    tq = 256
    mk = mk_ref[...].astype(BF16)
    mv = mv_ref[...].astype(BF16)

    def finish(j, carry):
        t0 = pl.multiple_of(j * tq, tq)
        l0 = lse_ref[0, pl.ds(t0, tq), :]
        l1 = lse_ref[1, pl.ds(t0, tq), :]
        l2 = lse_ref[2, pl.ds(t0, tq), :]
        top = jnp.maximum(jnp.maximum(l0, l1), l2)
        w0 = jnp.exp(l0 - top)
        w1 = jnp.exp(l1 - top)
        w2 = jnp.exp(l2 - top)
        num = (w0 * o_ref[0, pl.ds(t0, tq), :] + w1 * o_ref[1, pl.ds(t0, tq), :]
               + w2 * o_ref[2, pl.ds(t0, tq), :])
        mix_ref[pl.ds(t0, tq), :] = (num / (w0 + w1 + w2)).astype(BF16)
        sx = _qk(qx_ref[pl.ds(t0, tq), :], mk) * SCALE
        ox, _ = _softmax_pv(sx, mv)
        xm_ref[pl.ds(t0, tq), :] = ox.astype(BF16)
        return carry

    lax.fori_loop(0, seq // tq, finish, 0, unroll=4)


def _attn_prompt(slopes, q, k, v, qx, mk, mv, w_gate, w_up, w_down, *, batch, seq):
    def col(c):
        return pl.BlockSpec((None, seq, LANES), lambda b, h, c=c: (b, 0, c * N_KV + h))

    head_spec = pl.BlockSpec((None, seq, LANES), lambda b, h: (b, 0, h))
    mem_spec = pl.BlockSpec((None, N_MEM, LANES), lambda b, h: (b, 0, h))
    n_steps = batch * N_KV

    def row_slice(w):
        return pl.BlockSpec((w.shape[0] // n_steps, w.shape[1]), lambda b, h: (b * N_KV + h, 0))

    weights = (w_gate, w_up, w_down)
    return pl.pallas_call(
        functools.partial(_attn_prompt_kernel, seq=seq),
        grid=(batch, N_KV),
        in_specs=[
            pl.BlockSpec(memory_space=pltpu.SMEM),
            col(0), col(1), col(2), head_spec, head_spec, head_spec, mem_spec, mem_spec,
        ] + [row_slice(w) for w in weights],
        out_specs=[head_spec, head_spec] + [row_slice(w) for w in weights],
        out_shape=[jax.ShapeDtypeStruct((batch, seq, W_KV), BF16),
                   jax.ShapeDtypeStruct((batch, seq, W_QX), BF16)]
        + [jax.ShapeDtypeStruct(w.shape, BF16) for w in weights],
        scratch_shapes=[
            pltpu.VMEM((N_GROUPS, SPAN, 2 * SPAN), F32),
            pltpu.VMEM((N_GROUPS, seq, LANES), F32),
            pltpu.VMEM((N_GROUPS, seq, LANES), F32),
        ],
        compiler_params=_params(("parallel", "parallel"), 56),
        name="attn_prompt",
    )(slopes, q, q, q, k, v, qx, mk, mv, *weights)


def _attn_sample_kernel(slopes_ref, q_ref, kn_ref, vn_ref, qx_ref, ck0_ref, ck1_ref, ck2_ref,
                        cv0_ref, cv1_ref, cv2_ref, mk_ref, mv_ref, mix_ref, xm_ref, *, bs):
    ck_refs = (ck0_ref, ck1_ref, ck2_ref)
    cv_refs = (cv0_ref, cv1_ref, cv2_ref)
    back = (SPAN - lax.broadcasted_iota(jnp.int32, (SPAN, 1), 0)).astype(F32)

    def one(n, carry):
        for h in range(N_KV):
            hs = slice(h * LANES, (h + 1) * LANES)
            kn = kn_ref[n, :, hs]
            vn = vn_ref[n, :, hs]
            scores, s_new = [], []
            for g, (_, dil) in enumerate(DIL_GROUPS):
                qs = slice(g * W_KV + h * LANES, g * W_KV + (h + 1) * LANES)
                q = q_ref[n, :, qs]
                s = jnp.sum(ck_refs[g][n, :, h, :] * q, axis=-1, keepdims=True) * SCALE
                scores.append(s - slopes_ref[g, h] * (back * float(dil)))
                s_new.append(jnp.sum(kn * q, axis=-1, keepdims=True) * SCALE)
            top = s_new[0]
            for g in range(N_GROUPS):
                top = jnp.maximum(top, jnp.maximum(s_new[g], jnp.max(scores[g], axis=0, keepdims=True)))
            num = jnp.zeros((1, LANES), F32)
            den = jnp.zeros((1, 1), F32)
            for g in range(N_GROUPS):
                p = jnp.exp(scores[g] - top)
                p_new = jnp.exp(s_new[g] - top)
                num = num + jnp.sum(p * cv_refs[g][n, :, h, :], axis=0, keepdims=True) + p_new * vn
                den = den + jnp.sum(p, axis=0, keepdims=True) + p_new
            mix_ref[n, :, hs] = num / den
            qx = qx_ref[n, :, hs]
            sx = jnp.sum(mk_ref[n, :, h, :] * qx, axis=-1, keepdims=True) * SCALE
            px = jnp.exp(sx - jnp.max(sx, axis=0, keepdims=True))
            ox = jnp.sum(px * mv_ref[n, :, h, :], axis=0, keepdims=True) / jnp.sum(px, axis=0, keepdims=True)
            xm_ref[n, :, hs] = ox
        return carry

    lax.fori_loop(0, bs, one, 0)


def _attn_sample(slopes, q, kn, vn, qx, cache_k, cache_v, mem_k, mem_v):
    n_seq, win = cache_k.shape[0], cache_k.shape[1]
    bs = 2
    row = lambda w: pl.BlockSpec((bs, 1, w), lambda i: (i, 0, 0))
    as_rows = lambda a: a.reshape(n_seq, 1, a.shape[-1])
    views_k, views_v, cache_specs = [], [], []
    for _, dil in DIL_GROUPS:
        views_k.append(cache_k.reshape(n_seq, win // dil, dil * N_KV, HEAD_DIM))
        views_v.append(cache_v.reshape(n_seq, win // dil, dil * N_KV, HEAD_DIM))
        rows = N_KV if dil == 1 else SUBLANES
        cache_specs.append(pl.BlockSpec((bs, SPAN, rows, HEAD_DIM),
                                        lambda i, nb=win // dil // SPAN: (i, nb - 1, 0, 0)))
    mem_spec = pl.BlockSpec((bs, N_MEM, N_XHEADS, HEAD_DIM), lambda i: (i, 0, 0, 0))
    return pl.pallas_call(
        functools.partial(_attn_sample_kernel, bs=bs),
        grid=(n_seq // bs,),
        in_specs=[pl.BlockSpec(memory_space=pltpu.SMEM), row(W_QD), row(W_KV), row(W_KV), row(W_QX)]
        + cache_specs + cache_specs + [mem_spec, mem_spec],
        out_specs=[row(W_KV), row(W_QX)],
        out_shape=[jax.ShapeDtypeStruct((n_seq, 1, W_KV), F32), jax.ShapeDtypeStruct((n_seq, 1, W_QX), F32)],
        compiler_params=_params(("parallel",), 40),
        name="attn_sample",
    )(slopes, as_rows(q), as_rows(kn), as_rows(vn), as_rows(qx), *views_k, *views_v, mem_k, mem_v)


def _merge_kernel(h_ref, c_ref, mix_ref, xm_ref, wg0_ref, wg1_ref, wg2_ref, bg_ref, wc_ref, wd_ref, wx_ref,
                  wo_ref, z_ref, merged_ref):
    s = pl.program_id(1)

    @pl.when(s < _D_TILES)
    def _():
        h = h_ref[...]
        acc = None
        for br, (wg_ref, y_ref, wy_ref) in enumerate(
                ((wg0_ref, c_ref, wc_ref), (wg1_ref, mix_ref, wd_ref), (wg2_ref, xm_ref, wx_ref))):
            gate = jax.nn.sigmoid(_dot(h, wg_ref[...]) + bg_ref[br])
            term = gate * _dot(y_ref[...].astype(BF16), wy_ref[...])
            acc = term if acc is None else acc + term
        merged_ref[s] = acc.astype(BF16)

    @pl.when(s >= _D_TILES)
    def _():
        z = None
        for j in range(_D_TILES):
            part = _dot(merged_ref[j], wo_ref[j * TN:(j + 1) * TN, :])
            z = part if z is None else z + part
        z_ref[...] = z


def _merge(h, c, mix, xm, w_in_t, b_gate, w_conv_out_t, w_dil_o_t, w_x_o_t, w_out_t, *, tm):
    m = h.shape[0]
    n1, n2 = _D_TILES, D_MODEL // TN_OUT
    last = n1 - 1

    def gate_spec(br):
        return pl.BlockSpec((D_MODEL, TN),
                            lambda i, s, br=br: (0, _GATE_TILE0 + br * _D_TILES + jnp.minimum(s, last)))

    def colw(kdim):
        return pl.BlockSpec((kdim, TN), lambda i, s: (0, jnp.minimum(s, last)))

    def rowblk(w):
        return pl.BlockSpec((tm, w), lambda i, s: (i, 0))

    return pl.pallas_call(
        _merge_kernel,
        grid=(m // tm, n1 + n2),
        in_specs=[
            rowblk(D_MODEL), rowblk(C_CONV), rowblk(W_KV), rowblk(W_QX),
            gate_spec(0), gate_spec(1), gate_spec(2),
            pl.BlockSpec((3, 1, TN), lambda i, s: (0, 0, jnp.minimum(s, last))),
            colw(C_CONV), colw(W_KV), colw(W_QX),
            pl.BlockSpec((D_MODEL, TN_OUT), lambda i, s: (0, jnp.maximum(s - n1, 0))),
        ],
        out_specs=pl.BlockSpec((tm, TN_OUT), lambda i, s: (i, jnp.maximum(s - n1, 0))),
        out_shape=jax.ShapeDtypeStruct((m, D_MODEL), F32),
        scratch_shapes=[pltpu.VMEM((n1, tm, TN), BF16)],
        compiler_params=_params(("parallel", "arbitrary"), 58),
        name="merge",
    )(h, c, mix, xm, w_in_t, w_in_t, w_in_t, b_gate.reshape(3, 1, D_MODEL), w_conv_out_t, w_dil_o_t, w_x_o_t,
      w_out_t)


def _ffn_kernel(x_ref, z_ref, gmix_ref, gpre_ref, wg_ref, wu_ref, wd_ref, gpost_ref, out_ref, x1_ref, h_ref,
                *, n_steps):
    s = pl.program_id(1)

    @pl.when(s == 0)
    def _():
        x1 = x_ref[...] + _rms_rows(z_ref[...], gmix_ref[...])
        x1_ref[...] = x1
        h_ref[...] = _rms_rows(x1, gpre_ref[...]).astype(BF16)
        out_ref[...] = jnp.zeros_like(out_ref)

    h = h_ref[...]
    gate = _dot(h, wg_ref[...])
    act = (gate * jax.nn.sigmoid(gate) * _dot(h, wu_ref[...])).astype(BF16)
    out_ref[...] += _dot(act, wd_ref[...])

    @pl.when(s == n_steps - 1)
    def _():
        out_ref[...] = x1_ref[...] + _rms_rows(out_ref[...], gpost_ref[...])


def _ffn(x, z, g_mix, g_pre, w_gate_t, w_up_t, w_down, g_post, *, tm):
    m = x.shape[0]
    n_steps = D_FF // TF
    rowblk = pl.BlockSpec((tm, D_MODEL), lambda i, s: (i, 0))
    gain = pl.BlockSpec((1, D_MODEL), lambda i, s: (0, 0))
    return pl.pallas_call(
        functools.partial(_ffn_kernel, n_steps=n_steps),
        grid=(m // tm, n_steps),
        in_specs=[
            rowblk, rowblk, gain, gain,
            pl.BlockSpec((D_MODEL, TF), lambda i, s: (0, s)),
            pl.BlockSpec((D_MODEL, TF), lambda i, s: (0, s)),
            pl.BlockSpec((TF, D_MODEL), lambda i, s: (s, 0)),
            gain,
        ],
        out_specs=rowblk,
        out_shape=jax.ShapeDtypeStruct((m, D_MODEL), F32),
        scratch_shapes=[pltpu.VMEM((tm, D_MODEL), F32), pltpu.VMEM((tm, D_MODEL), BF16)],
        compiler_params=_params(("parallel", "arbitrary"), 52),
        name="ffn",
    )(x, z, g_mix, g_pre, w_gate_t, w_up_t, w_down, g_post)


def kernel(x_prompt, x_sample, cache_win_k, cache_win_v, state_conv, cache_mem_k, cache_mem_v, mem_prompt,
           g_pre_mix, w_in, b_gate, conv_w, conv_b, conv_ln_g, conv_ln_b, w_conv_out, w_dil_o, g_mem, w_mem_kv,
           w_x_o, w_out, g_post_mix, g_pre_ffn, w_ffn_gate, w_ffn_up, w_ffn_down, g_post_ffn):
    batch, seq, _ = x_prompt.shape
    n_seq = x_sample.shape[0]
    depth = w_in.shape[0]
    assert depth == 1 and x_sample.shape[1] == 1 and seq % (DIL_GROUPS[-1][1] * SPAN) == 0
    assert cache_win_k.shape[2] == DIL_GROUPS[-1][0]

    idx = jnp.arange(1, N_QH + 1, dtype=F32)
    slopes = jnp.exp2(-8.0 * idx / N_QH).reshape(N_GROUPS, N_KV)

    xp, xs = x_prompt.reshape(batch * seq, D_MODEL), x_sample.reshape(n_seq, D_MODEL)
    l = 0
    cast = lambda w: w[l].astype(BF16)
    w_in_t, w_conv_out_t, w_dil_o_t, w_x_o_t = cast(w_in), cast(w_conv_out), cast(w_dil_o), cast(w_x_o)
    w_out_t, w_mem_kv_t = cast(w_out), cast(w_mem_kv)
    row = lambda a: a[l].reshape(1, -1)
    g_pre, g_post, g_ffn_pre, g_ffn_post = row(g_pre_mix), row(g_post_mix), row(g_pre_ffn), row(g_post_ffn)
    cb, lng, lnb = row(conv_b), row(conv_ln_g), row(conv_ln_b)
    conv_w_slabs = conv_w[l].reshape(CONV_WIDTH, N_SLABS, LANES).transpose(1, 0, 2)

    def dense_tail(h, c, mix, xm, x, ffn_weights, tm_merge, tm_ffn):
        z = _merge(h, c, mix, xm, w_in_t, b_gate[l], w_conv_out_t, w_dil_o_t, w_x_o_t, w_out_t, tm=tm_merge)
        return _ffn(x, z, g_post, g_ffn_pre, *ffn_weights, g_ffn_post, tm=tm_ffn)

    mk, mv = _mem_kv(mem_prompt.reshape(batch * N_MEM, D_MODEL), row(g_mem), w_mem_kv_t, tm=512)
    h_p, u_p, q_p, k_p, v_p, qx_p = _in_proj(xp, g_pre, w_in_t, tm=1024, slabs=True)
    c_p, conv_p = _conv_prompt(u_p, conv_w_slabs, cb, lng, lnb, batch=batch, seq=seq)
    mix_p, xm_p, *ffn_weights = _attn_prompt(
        slopes, q_p.reshape(batch, seq, W_QD), k_p.reshape(batch, seq, W_KV), v_p.reshape(batch, seq, W_KV),
        qx_p.reshape(batch, seq, W_QX), mk.reshape(batch, N_MEM, W_QX), mv.reshape(batch, N_MEM, W_QX),
        w_ffn_gate[l], w_ffn_up[l], w_ffn_down[l], batch=batch, seq=seq)
    y_p = dense_tail(h_p, c_p, mix_p.reshape(batch * seq, W_KV), xm_p.reshape(batch * seq, W_QX), xp,
                     ffn_weights, 1024, 512)

    h_s, u_s, q_s, k_s, v_s, qx_s = _in_proj(xs, g_pre, w_in_t, tm=n_seq, slabs=False)
    c_s, conv_s = _conv_sample(state_conv[l], u_s, conv_w[l], cb, lng, lnb)
    mix_s, xm_s = _attn_sample(slopes, q_s, k_s, v_s, qx_s,
                               cache_win_k[l], cache_win_v[l], cache_mem_k[l], cache_mem_v[l])
    y_s = dense_tail(h_s, c_s, mix_s.reshape(n_seq, W_KV), xm_s.reshape(n_seq, W_QX), xs, ffn_weights,
                     n_seq, n_seq)

    win = min(DIL_GROUPS[-1][0], seq)
    kv_shape = (1, batch, seq, N_KV, HEAD_DIM)
    mem_shape = (1, batch, N_MEM, N_XHEADS, HEAD_DIM)
    new_shape = (1, n_seq, 1, N_KV, HEAD_DIM)
    return (y_p.reshape(batch, seq, D_MODEL), y_s.reshape(n_seq, 1, D_MODEL),
            k_p.reshape(kv_shape)[:, :, seq - win:], v_p.reshape(kv_shape)[:, :, seq - win:], conv_p,
            mk.reshape(mem_shape), mv.reshape(mem_shape),
            k_s.reshape(new_shape), v_s.reshape(new_shape), conv_s.reshape(1, n_seq, CONV_PREFIX, C_CONV))
```

```python
import functools

import jax
import jax.numpy as jnp
from jax import lax
from jax.experimental import pallas as pl
from jax.experimental.pallas import tpu as pltpu

F32 = jnp.float32
BF16 = jnp.bfloat16

D_MODEL = 2048
C_CONV = D_MODEL // 2
CONV_WIDTH = 31
CONV_PREFIX = CONV_WIDTH - 1
HEAD_DIM = 128
N_KV = 4
DIL_GROUPS = ((128, 1), (512, 4), (2048, 16))
N_GROUPS = len(DIL_GROUPS)
N_QH = N_GROUPS * N_KV
SPAN = 128
N_MEM = 256
N_XHEADS = 4
W_QD = N_QH * HEAD_DIM
W_KV = N_KV * HEAD_DIM
W_QX = N_XHEADS * HEAD_DIM
D_FF = 5632
EPS = 1e-6
NEG = -1e30
SCALE = HEAD_DIM ** -0.5
LANES = 128
SUBLANES = 8
N_SLABS = C_CONV // LANES
MIB = 1024 * 1024

TN = 512
_A_TILES = C_CONV // TN
_Q_TILES = W_QD // TN
_KV_TILES = W_KV // TN
_QX_TILES = W_QX // TN
_STEP_Q0 = _A_TILES
_STEP_K0 = _STEP_Q0 + _Q_TILES
_STEP_V0 = _STEP_K0 + _KV_TILES
_STEP_X0 = _STEP_V0 + _KV_TILES
_IN_STEPS = _STEP_X0 + _QX_TILES
_GATE_TILE0 = 2 * _A_TILES + _Q_TILES + 2 * _KV_TILES + _QX_TILES
_D_TILES = D_MODEL // TN
TN_OUT = 512
TF = 512


def _params(semantics, vmem_mib):
    return pltpu.CompilerParams(dimension_semantics=semantics, vmem_limit_bytes=vmem_mib * MIB)


def _rms_rows(x, g):
    return x * lax.rsqrt(jnp.mean(x * x, axis=-1, keepdims=True) + EPS) * g


def _dot(a, b):
    return jnp.dot(a, b, preferred_element_type=F32)


def _row_ahead(i, s, free_from, n_rows):
    return jnp.minimum(i + (s >= free_from).astype(jnp.int32), n_rows - 1)


def _in_proj_kernel(x_ref, g_ref, w1_ref, w2_ref, h_ref, u_ref, q_ref, k_ref, v_ref, qx_ref, *, slabs):
    s = pl.program_id(1)

    @pl.when(s == 0)
    def _():
        h_ref[...] = _rms_rows(x_ref[...], g_ref[...]).astype(BF16)

    @pl.when(s < _STEP_Q0)
    def _():
        h = h_ref[...]
        u = _dot(h, w1_ref[...]) * jax.nn.sigmoid(_dot(h, w2_ref[...]))
        if slabs:
            for c in range(TN // LANES):
                u_ref[c] = u[:, c * LANES:(c + 1) * LANES]
        else:
            u_ref[...] = u

    @pl.when((s >= _STEP_Q0) & (s < _STEP_K0))
    def _():
        q_ref[...] = _dot(h_ref[...], w1_ref[...])

    @pl.when((s >= _STEP_K0) & (s < _STEP_V0))
    def _():
        k_ref[...] = _dot(h_ref[...], w1_ref[...])

    @pl.when((s >= _STEP_V0) & (s < _STEP_X0))
    def _():
        v_ref[...] = _dot(h_ref[...], w1_ref[...])

    @pl.when(s >= _STEP_X0)
    def _():
        qx_ref[...] = _dot(h_ref[...], w1_ref[...]).astype(qx_ref.dtype)


def _in_proj(x, g, w_in_t, *, tm, slabs):
    m = x.shape[0]
    last_a = _A_TILES - 1
    if slabs:
        u_shape = jax.ShapeDtypeStruct((N_SLABS, m, LANES), F32)
        u_spec = pl.BlockSpec((TN // LANES, tm, LANES), lambda i, s: (jnp.minimum(s, last_a), i, 0))
    else:
        u_shape = jax.ShapeDtypeStruct((m, C_CONV), F32)
        u_spec = pl.BlockSpec((tm, TN), lambda i, s: (i, jnp.minimum(s, last_a)))

    def out_tiles(step0, n_tiles):
        return pl.BlockSpec((tm, TN), lambda i, s: (i, jnp.clip(s - step0, 0, n_tiles - 1)))

    return pl.pallas_call(
        functools.partial(_in_proj_kernel, slabs=slabs),
        grid=(m // tm, _IN_STEPS),
        in_specs=[
            pl.BlockSpec((tm, D_MODEL), lambda i, s: (_row_ahead(i, s, 1, m // tm), 0)),
            pl.BlockSpec((1, D_MODEL), lambda i, s: (0, 0)),
            pl.BlockSpec((D_MODEL, TN), lambda i, s: (0, jnp.where(s < _A_TILES, s, s + _A_TILES))),
            pl.BlockSpec((D_MODEL, TN), lambda i, s: (0, jnp.where(s < _A_TILES, s, 0) + _A_TILES)),
        ],
        out_specs=[
            pl.BlockSpec((tm, D_MODEL), lambda i, s: (i, 0)),
            u_spec,
            out_tiles(_STEP_Q0, _Q_TILES),
            out_tiles(_STEP_K0, _KV_TILES),
            out_tiles(_STEP_V0, _KV_TILES),
            out_tiles(_STEP_X0, _QX_TILES),
        ],
        out_shape=[
            jax.ShapeDtypeStruct((m, D_MODEL), BF16),
            u_shape,
            jax.ShapeDtypeStruct((m, W_QD), F32),
            jax.ShapeDtypeStruct((m, W_KV), F32),
            jax.ShapeDtypeStruct((m, W_KV), F32),
            jax.ShapeDtypeStruct((m, W_QX), BF16 if slabs else F32),
        ],
        compiler_params=_params(("parallel", "arbitrary"), 58),
        name="in_proj",
    )(x, g, w_in_t, w_in_t)


def _mem_kv_kernel(x_ref, g_ref, w_ref, mk_ref, mv_ref, h_ref):
    s = pl.program_id(1)

    @pl.when(s == 0)
    def _():
        h_ref[...] = _rms_rows(x_ref[...], g_ref[...]).astype(BF16)
        mk_ref[...] = _dot(h_ref[...], w_ref[...])

    @pl.when(s == 1)
    def _():
        mv_ref[...] = _dot(h_ref[...], w_ref[...])


def _mem_kv(mem, g, w_t, *, tm):
    m = mem.shape[0]
    return pl.pallas_call(
        _mem_kv_kernel,
        grid=(m // tm, 2),
        in_specs=[
            pl.BlockSpec((tm, D_MODEL), lambda i, s: (i, 0)),
            pl.BlockSpec((1, D_MODEL), lambda i, s: (0, 0)),
            pl.BlockSpec((D_MODEL, W_QX), lambda i, s: (0, s)),
        ],
        out_specs=[
            pl.BlockSpec((tm, W_QX), lambda i, s: (i, 0)),
            pl.BlockSpec((tm, W_QX), lambda i, s: (i, 0)),
        ],
        out_shape=[jax.ShapeDtypeStruct((m, W_QX), F32)] * 2,
        scratch_shapes=[pltpu.VMEM((tm, D_MODEL), BF16)],
        compiler_params=_params(("parallel", "arbitrary"), 32),
        name="mem_kv",
    )(mem, g, w_t)


def _ln_swish_slabs(y_slabs, cb_ref, lng_ref, lnb_ref, out_dtype):
    ys = [y + cb_ref[:, c * LANES:(c + 1) * LANES] for c, y in enumerate(y_slabs)]
    tot = ys[0]
    for y in ys[1:]:
        tot = tot + y
    mu = jnp.sum(tot, axis=-1, keepdims=True) * (1.0 / C_CONV)
    ds = [y - mu for y in ys]
    sq = ds[0] * ds[0]
    for d in ds[1:]:
        sq = sq + d * d
    rs = lax.rsqrt(jnp.sum(sq, axis=-1, keepdims=True) * (1.0 / C_CONV) + EPS)
    outs = []
    for c, d in enumerate(ds):
        z = d * rs * lng_ref[:, c * LANES:(c + 1) * LANES] + lnb_ref[:, c * LANES:(c + 1) * LANES]
        outs.append((z * jax.nn.sigmoid(z)).astype(out_dtype))
    return outs


def _conv_prompt_kernel(u_ref, w_ref, cb_ref, lng_ref, lnb_ref, c_ref, st_ref, head_ref, y_ref, *, seq, tt):
    pad = 32
    head_ref[:, 0:pad, :] = jnp.zeros((N_SLABS, pad, LANES), F32)
    head_ref[:, pad:pad + tt, :] = u_ref[:, 0:tt, :]

    def tile(src_ref, base, t0):
        def slab(c, carry):
            acc = jnp.zeros((tt, LANES), F32)
            for j in range(SUBLANES):
                taps = range(j, CONV_WIDTH, SUBLANES)
                win = src_ref[c, pl.ds(base + j, tt + taps[-1] - j), :]
                for k in taps:
                    acc = acc + win[k - j:k - j + tt] * w_ref[c, k:k + 1, :]
            y_ref[c] = acc
            return carry

        lax.fori_loop(0, N_SLABS, slab, 0)
        ys = [y_ref[c] for c in range(N_SLABS)]
        outs = _ln_swish_slabs(ys, cb_ref, lng_ref, lnb_ref, BF16)
        for c in range(N_SLABS):
            c_ref[pl.ds(t0, tt), c * LANES:(c + 1) * LANES] = outs[c]

    tile(head_ref, pad - CONV_PREFIX, 0)

    def body(j, carry):
        t0 = pl.multiple_of(j * tt, tt)
        tile(u_ref, t0 - CONV_PREFIX, t0)
        return carry

    lax.fori_loop(1, seq // tt, body, 0)
    for c in range(N_SLABS):
        st_ref[:, c * LANES:(c + 1) * LANES] = u_ref[c, seq - CONV_PREFIX:seq, :]


def _conv_prompt(u_slabs, w_slabs, conv_b, ln_g, ln_b, *, batch, seq):
    tt = 128
    return pl.pallas_call(
        functools.partial(_conv_prompt_kernel, seq=seq, tt=tt),
        grid=(batch,),
        in_specs=[
            pl.BlockSpec((N_SLABS, seq, LANES), lambda b: (0, b, 0)),
            pl.BlockSpec((N_SLABS, CONV_WIDTH, LANES), lambda b: (0, 0, 0)),
            pl.BlockSpec((1, C_CONV), lambda b: (0, 0)),
            pl.BlockSpec((1, C_CONV), lambda b: (0, 0)),
            pl.BlockSpec((1, C_CONV), lambda b: (0, 0)),
        ],
        out_specs=[
            pl.BlockSpec((seq, C_CONV), lambda b: (b, 0)),
            pl.BlockSpec((None, None, CONV_PREFIX, C_CONV), lambda b: (0, b, 0, 0)),
        ],
        out_shape=[
            jax.ShapeDtypeStruct((batch * seq, C_CONV), BF16),
            jax.ShapeDtypeStruct((1, batch, CONV_PREFIX, C_CONV), F32),
        ],
        scratch_shapes=[pltpu.VMEM((N_SLABS, 32 + tt, LANES), F32), pltpu.VMEM((N_SLABS, tt, LANES), F32)],
        compiler_params=_params(("parallel",), 40),
        name="conv_prompt",
    )(u_slabs, w_slabs, conv_b, ln_g, ln_b)


def _conv_sample_kernel(state_ref, u_ref, w_ref, cb_ref, lng_ref, lnb_ref, c_ref, st_ref, *, n_seq):
    w_hist = w_ref[0:CONV_PREFIX, :]
    w_new = w_ref[CONV_PREFIX:CONV_WIDTH, :]
    for n in range(n_seq):
        hist = state_ref[n]
        new = u_ref[n:n + 1, :]
        y = jnp.sum(hist * w_hist, axis=0, keepdims=True) + new * w_new
        ys = [y[:, c * LANES:(c + 1) * LANES] for c in range(N_SLABS)]
        outs = _ln_swish_slabs(ys, cb_ref, lng_ref, lnb_ref, F32)
        for c in range(N_SLABS):
            c_ref[n:n + 1, c * LANES:(c + 1) * LANES] = outs[c]
        st_ref[n, 0:CONV_PREFIX - 1, :] = state_ref[n, 1:CONV_PREFIX, :]
        st_ref[n, CONV_PREFIX - 1:CONV_PREFIX, :] = new


def _conv_sample(state, u, conv_w, conv_b, ln_g, ln_b):
    n_seq = state.shape[0]
    return pl.pallas_call(
        functools.partial(_conv_sample_kernel, n_seq=n_seq),
        out_shape=[
            jax.ShapeDtypeStruct((n_seq, C_CONV), F32),
            jax.ShapeDtypeStruct((n_seq, CONV_PREFIX, C_CONV), F32),
        ],
        compiler_params=pltpu.CompilerParams(vmem_limit_bytes=32 * MIB),
        name="conv_sample",
    )(state, u, conv_w, conv_b, ln_g, ln_b)


def _softmax_pv(s, v):
    m = jnp.max(s, axis=-1, keepdims=True)
    p = jnp.exp(s - m)
    l = jnp.sum(p, axis=-1, keepdims=True)
    o = _dot(p.astype(BF16), v) * (1.0 / l)
    return o, m + jnp.log(l)


def _qk(q, k):
    return lax.dot_general(q, k, (((1,), (1,)), ((), ())), preferred_element_type=F32)


def _attn_prompt_kernel(slopes_ref, q0_ref, q1_ref, q2_ref, k_ref, v_ref, qx_ref, mk_ref, mv_ref,
                        wg_ref, wu_ref, wd_ref, mix_ref, xm_ref, wg_out_ref, wu_out_ref, wd_out_ref,
                        bias_ref, o_ref, lse_ref, *, seq):
    head = pl.program_id(1)
    wg_out_ref[...] = wg_ref[...].astype(BF16)
    wu_out_ref[...] = wu_ref[...].astype(BF16)
    wd_out_ref[...] = wd_ref[...].astype(BF16)
    qi = lax.broadcasted_iota(jnp.int32, (SPAN, 2 * SPAN), 0)
    ci = lax.broadcasted_iota(jnp.int32, (SPAN, 2 * SPAN), 1)
    dist = qi + SPAN - ci
    valid = (dist >= 0) & (dist <= SPAN)
    for g, (_, dil) in enumerate(DIL_GROUPS):
        bias = -slopes_ref[g, head] * (dist * dil).astype(F32)
        bias_ref[g] = jnp.where(valid, bias, NEG)

    def rows(ref, start, size, dil):
        if dil == 1:
            return ref[pl.ds(start, size), :]
        return ref[pl.ds(start, size, stride=dil), :]

    def put(ref, g, start, dil, val):
        if dil == 1:
            ref[g, pl.ds(start, SPAN), :] = val
        else:
            ref[g, pl.ds(start, SPAN, stride=dil), :] = val

    def unit(g, dil, q_ref, start, n_keys):
        k_start = start - (n_keys - SPAN) * dil
        q = rows(q_ref, start, SPAN, dil).astype(BF16)
        k = rows(k_ref, k_start, n_keys, dil).astype(BF16)
        v = rows(v_ref, k_start, n_keys, dil).astype(BF16)
        s = _qk(q, k) * SCALE + bias_ref[g, :, 2 * SPAN - n_keys:2 * SPAN]
        o, lse = _softmax_pv(s, v)
        put(o_ref, g, start, dil, o)
        put(lse_ref, g, start, dil, jnp.broadcast_to(lse, (SPAN, LANES)))

    q_refs = (q0_ref, q1_ref, q2_ref)
    for g, (_, dil) in enumerate(DIL_GROUPS):
        n_blocks = seq // dil // SPAN
        q_ref = q_refs[g]
        if n_blocks == 1:
            def first_blocks(r, carry, g=g, dil=dil, q_ref=q_ref):
                unit(g, dil, q_ref, r, SPAN)
                return carry
            lax.fori_loop(0, dil, first_blocks, 0, unroll=8)
        elif dil == 1:
            unit(g, dil, q_ref, 0, SPAN)

            def later_blocks(n, carry, g=g, dil=dil, q_ref=q_ref):
                unit(g, dil, q_ref, pl.multiple_of(n * SPAN, SPAN), 2 * SPAN)
                return carry
            lax.fori_loop(1, n_blocks, later_blocks, 0, unroll=5)
        else:
            def residue(r, carry, g=g, dil=dil, q_ref=q_ref, n_blocks=n_blocks):
                unit(g, dil, q_ref, r, SPAN)
                for n in range(1, n_blocks):
                    unit(g, dil, q_ref, r + n * (SPAN * dil), 2 * SPAN)
                return carry
            lax.fori_loop(0, dil, residue, 0, unroll=2)

    tq = 256
    mk = mk_ref[...].astype(BF16)
    mv = mv_ref[...].astype(BF16)

    def finish(j, carry):
        t0 = pl.multiple_of(j * tq, tq)
        l0 = lse_ref[0, pl.ds(t0, tq), :]
        l1 = lse_ref[1, pl.ds(t0, tq), :]
        l2 = lse_ref[2, pl.ds(t0, tq), :]
        top = jnp.maximum(jnp.maximum(l0, l1), l2)
        w0 = jnp.exp(l0 - top)
        w1 = jnp.exp(l1 - top)
        w2 = jnp.exp(l2 - top)
        num = (w0 * o_ref[0, pl.ds(t0, tq), :] + w1 * o_ref[1, pl.ds(t0, tq), :]
               + w2 * o_ref[2, pl.ds(t0, tq), :])
        mix_ref[pl.ds(t0, tq), :] = (num / (w0 + w1 + w2)).astype(BF16)
        sx = _qk(qx_ref[pl.ds(t0, tq), :], mk) * SCALE
        ox, _ = _softmax_pv(sx, mv)
        xm_ref[pl.ds(t0, tq), :] = ox.astype(BF16)
        return carry

    lax.fori_loop(0, seq // tq, finish, 0, unroll=4)


def _attn_prompt(slopes, q, k, v, qx, mk, mv, w_gate, w_up, w_down, *, batch, seq):
    def col(c):
        return pl.BlockSpec((None, seq, LANES), lambda b, h, c=c: (b, 0, c * N_KV + h))

    head_spec = pl.BlockSpec((None, seq, LANES), lambda b, h: (b, 0, h))
    mem_spec = pl.BlockSpec((None, N_MEM, LANES), lambda b, h: (b, 0, h))
    n_steps = batch * N_KV

    def row_slice(w):
        return pl.BlockSpec((w.shape[0] // n_steps, w.shape[1]), lambda b, h: (b * N_KV + h, 0))

    weights = (w_gate, w_up, w_down)
    return pl.pallas_call(
        functools.partial(_attn_prompt_kernel, seq=seq),
        grid=(batch, N_KV),
        in_specs=[
            pl.BlockSpec(memory_space=pltpu.SMEM),
            col(0), col(1), col(2), head_spec, head_spec, head_spec, mem_spec, mem_spec,
        ] + [row_slice(w) for w in weights],
        out_specs=[head_spec, head_spec] + [row_slice(w) for w in weights],
        out_shape=[jax.ShapeDtypeStruct((batch, seq, W_KV), BF16),
                   jax.ShapeDtypeStruct((batch, seq, W_QX), BF16)]
        + [jax.ShapeDtypeStruct(w.shape, BF16) for w in weights],
        scratch_shapes=[
            pltpu.VMEM((N_GROUPS, SPAN, 2 * SPAN), F32),
            pltpu.VMEM((N_GROUPS, seq, LANES), F32),
            pltpu.VMEM((N_GROUPS, seq, LANES), F32),
        ],
        compiler_params=_params(("parallel", "parallel"), 56),
        name="attn_prompt",
    )(slopes, q, q, q, k, v, qx, mk, mv, *weights)


def _attn_sample_kernel(slopes_ref, q_ref, kn_ref, vn_ref, qx_ref, ck0_ref, ck1_ref, ck2_ref,
                        cv0_ref, cv1_ref, cv2_ref, mk_ref, mv_ref, mix_ref, xm_ref, *, bs):
    ck_refs = (ck0_ref, ck1_ref, ck2_ref)
    cv_refs = (cv0_ref, cv1_ref, cv2_ref)
    back = (SPAN - lax.broadcasted_iota(jnp.int32, (SPAN, 1), 0)).astype(F32)

    def one(n, carry):
        for h in range(N_KV):
            hs = slice(h * LANES, (h + 1) * LANES)
            kn = kn_ref[n, :, hs]
            vn = vn_ref[n, :, hs]
            scores, s_new = [], []
            for g, (_, dil) in enumerate(DIL_GROUPS):
                qs = slice(g * W_KV + h * LANES, g * W_KV + (h + 1) * LANES)
                q = q_ref[n, :, qs]
                s = jnp.sum(ck_refs[g][n, :, h, :] * q, axis=-1, keepdims=True) * SCALE
                scores.append(s - slopes_ref[g, h] * (back * float(dil)))
                s_new.append(jnp.sum(kn * q, axis=-1, keepdims=True) * SCALE)
            top = s_new[0]
            for g in range(N_GROUPS):
                top = jnp.maximum(top, jnp.maximum(s_new[g], jnp.max(scores[g], axis=0, keepdims=True)))
            num = jnp.zeros((1, LANES), F32)
            den = jnp.zeros((1, 1), F32)
            for g in range(N_GROUPS):
                p = jnp.exp(scores[g] - top)
                p_new = jnp.exp(s_new[g] - top)
                num = num + jnp.sum(p * cv_refs[g][n, :, h, :], axis=0, keepdims=True) + p_new * vn
                den = den + jnp.sum(p, axis=0, keepdims=True) + p_new
            mix_ref[n, :, hs] = num / den
            qx = qx_ref[n, :, hs]
            sx = jnp.sum(mk_ref[n, :, h, :] * qx, axis=-1, keepdims=True) * SCALE
            px = jnp.exp(sx - jnp.max(sx, axis=0, keepdims=True))
            ox = jnp.sum(px * mv_ref[n, :, h, :], axis=0, keepdims=True) / jnp.sum(px, axis=0, keepdims=True)
            xm_ref[n, :, hs] = ox
        return carry

    lax.fori_loop(0, bs, one, 0)


def _attn_sample(slopes, q, kn, vn, qx, cache_k, cache_v, mem_k, mem_v):
    n_seq, win = cache_k.shape[0], cache_k.shape[1]
    bs = 2
    row = lambda w: pl.BlockSpec((bs, 1, w), lambda i: (i, 0, 0))
    as_rows = lambda a: a.reshape(n_seq, 1, a.shape[-1])
    views_k, views_v, cache_specs = [], [], []
    for _, dil in DIL_GROUPS:
        views_k.append(cache_k.reshape(n_seq, win // dil, dil * N_KV, HEAD_DIM))
        views_v.append(cache_v.reshape(n_seq, win // dil, dil * N_KV, HEAD_DIM))
        rows = N_KV if dil == 1 else SUBLANES
        cache_specs.append(pl.BlockSpec((bs, SPAN, rows, HEAD_DIM),
                                        lambda i, nb=win // dil // SPAN: (i, nb - 1, 0, 0)))
    mem_spec = pl.BlockSpec((bs, N_MEM, N_XHEADS, HEAD_DIM), lambda i: (i, 0, 0, 0))
    return pl.pallas_call(
        functools.partial(_attn_sample_kernel, bs=bs),
        grid=(n_seq // bs,),
        in_specs=[pl.BlockSpec(memory_space=pltpu.SMEM), row(W_QD), row(W_KV), row(W_KV), row(W_QX)]
        + cache_specs + cache_specs + [mem_spec, mem_spec],
        out_specs=[row(W_KV), row(W_QX)],
        out_shape=[jax.ShapeDtypeStruct((n_seq, 1, W_KV), F32), jax.ShapeDtypeStruct((n_seq, 1, W_QX), F32)],
        compiler_params=_params(("parallel",), 40),
        name="attn_sample",
    )(slopes, as_rows(q), as_rows(kn), as_rows(vn), as_rows(qx), *views_k, *views_v, mem_k, mem_v)


def _merge_kernel(h_ref, c_ref, mix_ref, xm_ref, wg0_ref, wg1_ref, wg2_ref, bg_ref, wc_ref, wd_ref, wx_ref,
                  wo_ref, z_ref, merged_ref):
    s = pl.program_id(1)

    @pl.when(s < _D_TILES)
    def _():
        h = h_ref[...]
        acc = None
        for br, (wg_ref, y_ref, wy_ref) in enumerate(
                ((wg0_ref, c_ref, wc_ref), (wg1_ref, mix_ref, wd_ref), (wg2_ref, xm_ref, wx_ref))):
            gate = jax.nn.sigmoid(_dot(h, wg_ref[...]) + bg_ref[br])
            term = gate * _dot(y_ref[...].astype(BF16), wy_ref[...])
            acc = term if acc is None else acc + term
        merged_ref[s] = acc.astype(BF16)

    @pl.when(s >= _D_TILES)
    def _():
        z = None
        for j in range(_D_TILES):
            part = _dot(merged_ref[j], wo_ref[j * TN:(j + 1) * TN, :])
            z = part if z is None else z + part
        z_ref[...] = z


def _merge(h, c, mix, xm, w_in_t, b_gate, w_conv_out_t, w_dil_o_t, w_x_o_t, w_out_t, *, tm):
    m = h.shape[0]
    n1, n2 = _D_TILES, D_MODEL // TN_OUT

    n_rows = m // tm

    def tile1(s):
        return jnp.where(s < n1, s, 0)

    def gate_spec(br):
        return pl.BlockSpec((D_MODEL, TN), lambda i, s, br=br: (0, _GATE_TILE0 + br * _D_TILES + tile1(s)))

    def colw(kdim):
        return pl.BlockSpec((kdim, TN), lambda i, s: (0, tile1(s)))

    def rowblk(w):
        return pl.BlockSpec((tm, w), lambda i, s: (_row_ahead(i, s, n1, n_rows), 0))

    return pl.pallas_call(
        _merge_kernel,
        grid=(m // tm, n1 + n2),
        in_specs=[
            rowblk(D_MODEL), rowblk(C_CONV), rowblk(W_KV), rowblk(W_QX),
            gate_spec(0), gate_spec(1), gate_spec(2),
            pl.BlockSpec((3, 1, TN), lambda i, s: (0, 0, tile1(s))),
            colw(C_CONV), colw(W_KV), colw(W_QX),
            pl.BlockSpec((D_MODEL, TN_OUT), lambda i, s: (0, jnp.maximum(s - n1, 0))),
        ],
        out_specs=pl.BlockSpec((tm, TN_OUT), lambda i, s: (i, jnp.maximum(s - n1, 0))),
        out_shape=jax.ShapeDtypeStruct((m, D_MODEL), F32),
        scratch_shapes=[pltpu.VMEM((n1, tm, TN), BF16)],
        compiler_params=_params(("parallel", "arbitrary"), 58),
        name="merge",
    )(h, c, mix, xm, w_in_t, w_in_t, w_in_t, b_gate.reshape(3, 1, D_MODEL), w_conv_out_t, w_dil_o_t, w_x_o_t,
      w_out_t)


def _ffn_kernel(x_ref, z_ref, gmix_ref, gpre_ref, wg_ref, wu_ref, wd_ref, gpost_ref, out_ref, x1_ref, h_ref,
                *, n_steps):
    s = pl.program_id(1)

    @pl.when(s == 0)
    def _():
        x1 = x_ref[...] + _rms_rows(z_ref[...], gmix_ref[...])
        x1_ref[...] = x1
        h_ref[...] = _rms_rows(x1, gpre_ref[...]).astype(BF16)
        out_ref[...] = jnp.zeros_like(out_ref)

    h = h_ref[...]
    gate = _dot(h, wg_ref[...])
    act = (gate * jax.nn.sigmoid(gate) * _dot(h, wu_ref[...])).astype(BF16)
    out_ref[...] += _dot(act, wd_ref[...])

    @pl.when(s == n_steps - 1)
    def _():
        out_ref[...] = x1_ref[...] + _rms_rows(out_ref[...], gpost_ref[...])


def _ffn(x, z, g_mix, g_pre, w_gate_t, w_up_t, w_down, g_post, *, tm):
    m = x.shape[0]
    n_steps = D_FF // TF
    rowblk = pl.BlockSpec((tm, D_MODEL), lambda i, s: (i, 0))
    rowin = pl.BlockSpec((tm, D_MODEL), lambda i, s: (_row_ahead(i, s, 1, m // tm), 0))
    gain = pl.BlockSpec((1, D_MODEL), lambda i, s: (0, 0))
    return pl.pallas_call(
        functools.partial(_ffn_kernel, n_steps=n_steps),
        grid=(m // tm, n_steps),
        in_specs=[
            rowin, rowin, gain, gain,
            pl.BlockSpec((D_MODEL, TF), lambda i, s: (0, s)),
            pl.BlockSpec((D_MODEL, TF), lambda i, s: (0, s)),
            pl.BlockSpec((TF, D_MODEL), lambda i, s: (s, 0)),
            gain,
        ],
        out_specs=rowblk,
        out_shape=jax.ShapeDtypeStruct((m, D_MODEL), F32),
        scratch_shapes=[pltpu.VMEM((tm, D_MODEL), F32), pltpu.VMEM((tm, D_MODEL), BF16)],
        compiler_params=_params(("parallel", "arbitrary"), 52),
        name="ffn",
    )(x, z, g_mix, g_pre, w_gate_t, w_up_t, w_down, g_post)


def kernel(x_prompt, x_sample, cache_win_k, cache_win_v, state_conv, cache_mem_k, cache_mem_v, mem_prompt,
           g_pre_mix, w_in, b_gate, conv_w, conv_b, conv_ln_g, conv_ln_b, w_conv_out, w_dil_o, g_mem, w_mem_kv,
           w_x_o, w_out, g_post_mix, g_pre_ffn, w_ffn_gate, w_ffn_up, w_ffn_down, g_post_ffn):
    batch, seq, _ = x_prompt.shape
    n_seq = x_sample.shape[0]
    depth = w_in.shape[0]
    assert depth == 1 and x_sample.shape[1] == 1 and seq % (DIL_GROUPS[-1][1] * SPAN) == 0
    assert cache_win_k.shape[2] == DIL_GROUPS[-1][0]

    idx = jnp.arange(1, N_QH + 1, dtype=F32)
    slopes = jnp.exp2(-8.0 * idx / N_QH).reshape(N_GROUPS, N_KV)

    xp, xs = x_prompt.reshape(batch * seq, D_MODEL), x_sample.reshape(n_seq, D_MODEL)
    l = 0
    cast = lambda w: w[l].astype(BF16)
    w_in_t, w_conv_out_t, w_dil_o_t, w_x_o_t = cast(w_in), cast(w_conv_out), cast(w_dil_o), cast(w_x_o)
    w_out_t, w_mem_kv_t = cast(w_out), cast(w_mem_kv)
    row = lambda a: a[l].reshape(1, -1)
    g_pre, g_post, g_ffn_pre, g_ffn_post = row(g_pre_mix), row(g_post_mix), row(g_pre_ffn), row(g_post_ffn)
    cb, lng, lnb = row(conv_b), row(conv_ln_g), row(conv_ln_b)
    conv_w_slabs = conv_w[l].reshape(CONV_WIDTH, N_SLABS, LANES).transpose(1, 0, 2)

    def dense_tail(h, c, mix, xm, x, ffn_weights, tm_merge, tm_ffn):
        z = _merge(h, c, mix, xm, w_in_t, b_gate[l], w_conv_out_t, w_dil_o_t, w_x_o_t, w_out_t, tm=tm_merge)
        return _ffn(x, z, g_post, g_ffn_pre, *ffn_weights, g_ffn_post, tm=tm_ffn)

    mk, mv = _mem_kv(mem_prompt.reshape(batch * N_MEM, D_MODEL), row(g_mem), w_mem_kv_t, tm=512)
    h_p, u_p, q_p, k_p, v_p, qx_p = _in_proj(xp, g_pre, w_in_t, tm=1024, slabs=True)
    c_p, conv_p = _conv_prompt(u_p, conv_w_slabs, cb, lng, lnb, batch=batch, seq=seq)
    mix_p, xm_p, *ffn_weights = _attn_prompt(
        slopes, q_p.reshape(batch, seq, W_QD), k_p.reshape(batch, seq, W_KV), v_p.reshape(batch, seq, W_KV),
        qx_p.reshape(batch, seq, W_QX), mk.reshape(batch, N_MEM, W_QX), mv.reshape(batch, N_MEM, W_QX),
        w_ffn_gate[l], w_ffn_up[l], w_ffn_down[l], batch=batch, seq=seq)
    y_p = dense_tail(h_p, c_p, mix_p.reshape(batch * seq, W_KV), xm_p.reshape(batch * seq, W_QX), xp,
                     ffn_weights, 1024, 512)

    h_s, u_s, q_s, k_s, v_s, qx_s = _in_proj(xs, g_pre, w_in_t, tm=n_seq, slabs=False)
    c_s, conv_s = _conv_sample(state_conv[l], u_s, conv_w[l], cb, lng, lnb)
    mix_s, xm_s = _attn_sample(slopes, q_s, k_s, v_s, qx_s,
                               cache_win_k[l], cache_win_v[l], cache_mem_k[l], cache_mem_v[l])
    y_s = dense_tail(h_s, c_s, mix_s.reshape(n_seq, W_KV), xm_s.reshape(n_seq, W_QX), xs, ffn_weights,
                     n_seq, n_seq)

    win = min(DIL_GROUPS[-1][0], seq)
    kv_shape = (1, batch, seq, N_KV, HEAD_DIM)
    mem_shape = (1, batch, N_MEM, N_XHEADS, HEAD_DIM)
    new_shape = (1, n_seq, 1, N_KV, HEAD_DIM)
    return (y_p.reshape(batch, seq, D_MODEL), y_s.reshape(n_seq, 1, D_MODEL),
            k_p.reshape(kv_shape)[:, :, seq - win:], v_p.reshape(kv_shape)[:, :, seq - win:], conv_p,
            mk.reshape(mem_shape), mv.reshape(mem_shape),
            k_s.reshape(new_shape), v_s.reshape(new_shape), conv_s.reshape(1, n_seq, CONV_PREFIX, C_CONV))
```

```python
import functools

import jax
import jax.numpy as jnp
from jax import lax
from jax.experimental import pallas as pl
from jax.experimental.pallas import tpu as pltpu

F32 = jnp.float32
BF16 = jnp.bfloat16

D_MODEL = 2048
C_CONV = D_MODEL // 2
CONV_WIDTH = 31
CONV_PREFIX = CONV_WIDTH - 1
HEAD_DIM = 128
N_KV = 4
DIL_GROUPS = ((128, 1), (512, 4), (2048, 16))
N_GROUPS = len(DIL_GROUPS)
N_QH = N_GROUPS * N_KV
SPAN = 128
N_MEM = 256
N_XHEADS = 4
W_QD = N_QH * HEAD_DIM
W_KV = N_KV * HEAD_DIM
W_QX = N_XHEADS * HEAD_DIM
D_FF = 5632
EPS = 1e-6
NEG = -1e30
SCALE = HEAD_DIM ** -0.5
LANES = 128
SUBLANES = 8
N_SLABS = C_CONV // LANES
MIB = 1024 * 1024

TN = 512
_A_TILES = C_CONV // TN
_Q_TILES = W_QD // TN
_KV_TILES = W_KV // TN
_QX_TILES = W_QX // TN
_STEP_Q0 = _A_TILES
_STEP_K0 = _STEP_Q0 + _Q_TILES
_STEP_V0 = _STEP_K0 + _KV_TILES
_STEP_X0 = _STEP_V0 + _KV_TILES
_IN_STEPS = _STEP_X0 + _QX_TILES
_GATE_TILE0 = 2 * _A_TILES + _Q_TILES + 2 * _KV_TILES + _QX_TILES
_D_TILES = D_MODEL // TN
_GATE_CAST_COLS = 1024
TN_OUT = 512
TF = 512


def _params(semantics, vmem_mib):
    return pltpu.CompilerParams(dimension_semantics=semantics, vmem_limit_bytes=vmem_mib * MIB)


def _rms_rows(x, g):
    return x * lax.rsqrt(jnp.mean(x * x, axis=-1, keepdims=True) + EPS) * g


def _dot(a, b):
    return jnp.dot(a, b, preferred_element_type=F32)


def _row_ahead(i, s, free_from, n_rows):
    return jnp.minimum(i + (s >= free_from).astype(jnp.int32), n_rows - 1)


def _in_proj_kernel(x_ref, g_ref, w1_ref, w2_ref, h_ref, u_ref, q_ref, k_ref, v_ref, qx_ref, *, slabs):
    s = pl.program_id(1)

    @pl.when(s == 0)
    def _():
        h_ref[...] = _rms_rows(x_ref[...], g_ref[...]).astype(BF16)

    @pl.when(s < _STEP_Q0)
    def _():
        h = h_ref[...]
        u = _dot(h, w1_ref[...]) * jax.nn.sigmoid(_dot(h, w2_ref[...]))
        if slabs:
            for c in range(TN // LANES):
                u_ref[c] = u[:, c * LANES:(c + 1) * LANES]
        else:
            u_ref[...] = u

    @pl.when((s >= _STEP_Q0) & (s < _STEP_K0))
    def _():
        q_ref[...] = _dot(h_ref[...], w1_ref[...])

    @pl.when((s >= _STEP_K0) & (s < _STEP_V0))
    def _():
        k_ref[...] = _dot(h_ref[...], w1_ref[...])

    @pl.when((s >= _STEP_V0) & (s < _STEP_X0))
    def _():
        v_ref[...] = _dot(h_ref[...], w1_ref[...])

    @pl.when(s >= _STEP_X0)
    def _():
        qx_ref[...] = _dot(h_ref[...], w1_ref[...]).astype(qx_ref.dtype)


def _in_proj(x, g, w_in_t, *, tm, slabs):
    m = x.shape[0]
    last_a = _A_TILES - 1
    if slabs:
        u_shape = jax.ShapeDtypeStruct((N_SLABS, m, LANES), F32)
        u_spec = pl.BlockSpec((TN // LANES, tm, LANES), lambda i, s: (jnp.minimum(s, last_a), i, 0))
    else:
        u_shape = jax.ShapeDtypeStruct((m, C_CONV), F32)
        u_spec = pl.BlockSpec((tm, TN), lambda i, s: (i, jnp.minimum(s, last_a)))

    def out_tiles(step0, n_tiles):
        return pl.BlockSpec((tm, TN), lambda i, s: (i, jnp.clip(s - step0, 0, n_tiles - 1)))

    return pl.pallas_call(
        functools.partial(_in_proj_kernel, slabs=slabs),
        grid=(m // tm, _IN_STEPS),
        in_specs=[
            pl.BlockSpec((tm, D_MODEL), lambda i, s: (_row_ahead(i, s, 1, m // tm), 0)),
            pl.BlockSpec((1, D_MODEL), lambda i, s: (0, 0)),
            pl.BlockSpec((D_MODEL, TN), lambda i, s: (0, jnp.where(s < _A_TILES, s, s + _A_TILES))),
            pl.BlockSpec((D_MODEL, TN), lambda i, s: (0, jnp.where(s < _A_TILES, s, 0) + _A_TILES)),
        ],
        out_specs=[
            pl.BlockSpec((tm, D_MODEL), lambda i, s: (i, 0)),
            u_spec,
            out_tiles(_STEP_Q0, _Q_TILES),
            out_tiles(_STEP_K0, _KV_TILES),
            out_tiles(_STEP_V0, _KV_TILES),
            out_tiles(_STEP_X0, _QX_TILES),
        ],
        out_shape=[
            jax.ShapeDtypeStruct((m, D_MODEL), BF16),
            u_shape,
            jax.ShapeDtypeStruct((m, W_QD), F32),
            jax.ShapeDtypeStruct((m, W_KV), F32),
            jax.ShapeDtypeStruct((m, W_KV), F32),
            jax.ShapeDtypeStruct((m, W_QX), BF16 if slabs else F32),
        ],
        compiler_params=_params(("parallel", "arbitrary"), 58),
        name="in_proj",
    )(x, g, w_in_t, w_in_t)


def _mem_kv_kernel(x_ref, g_ref, w_ref, mk_ref, mv_ref, h_ref):
    s = pl.program_id(1)

    @pl.when(s == 0)
    def _():
        h_ref[...] = _rms_rows(x_ref[...], g_ref[...]).astype(BF16)
        mk_ref[...] = _dot(h_ref[...], w_ref[...])

    @pl.when(s == 1)
    def _():
        mv_ref[...] = _dot(h_ref[...], w_ref[...])


def _mem_kv(mem, g, w_t, *, tm):
    m = mem.shape[0]
    return pl.pallas_call(
        _mem_kv_kernel,
        grid=(m // tm, 2),
        in_specs=[
            pl.BlockSpec((tm, D_MODEL), lambda i, s: (i, 0)),
            pl.BlockSpec((1, D_MODEL), lambda i, s: (0, 0)),
            pl.BlockSpec((D_MODEL, W_QX), lambda i, s: (0, s)),
        ],
        out_specs=[
            pl.BlockSpec((tm, W_QX), lambda i, s: (i, 0)),
            pl.BlockSpec((tm, W_QX), lambda i, s: (i, 0)),
        ],
        out_shape=[jax.ShapeDtypeStruct((m, W_QX), F32)] * 2,
        scratch_shapes=[pltpu.VMEM((tm, D_MODEL), BF16)],
        compiler_params=_params(("parallel", "arbitrary"), 32),
        name="mem_kv",
    )(mem, g, w_t)


def _ln_swish_slabs(y_slabs, cb_ref, lng_ref, lnb_ref, out_dtype):
    ys = [y + cb_ref[:, c * LANES:(c + 1) * LANES] for c, y in enumerate(y_slabs)]
    tot = ys[0]
    for y in ys[1:]:
        tot = tot + y
    mu = jnp.sum(tot, axis=-1, keepdims=True) * (1.0 / C_CONV)
    ds = [y - mu for y in ys]
    sq = ds[0] * ds[0]
    for d in ds[1:]:
        sq = sq + d * d
    rs = lax.rsqrt(jnp.sum(sq, axis=-1, keepdims=True) * (1.0 / C_CONV) + EPS)
    outs = []
    for c, d in enumerate(ds):
        z = d * rs * lng_ref[:, c * LANES:(c + 1) * LANES] + lnb_ref[:, c * LANES:(c + 1) * LANES]
        outs.append((z * jax.nn.sigmoid(z)).astype(out_dtype))
    return outs


def _conv_prompt_kernel(u_ref, w_ref, cb_ref, lng_ref, lnb_ref, c_ref, st_ref, head_ref, y_ref, *, seq, tt):
    pad = 32
    head_ref[:, 0:pad, :] = jnp.zeros((N_SLABS, pad, LANES), F32)
    head_ref[:, pad:pad + tt, :] = u_ref[:, 0:tt, :]

    def tile(src_ref, base, t0):
        def slab(c, carry):
            acc = jnp.zeros((tt, LANES), F32)
            for j in range(SUBLANES):
                taps = range(j, CONV_WIDTH, SUBLANES)
                win = src_ref[c, pl.ds(base + j, tt + taps[-1] - j), :]
                for k in taps:
                    acc = acc + win[k - j:k - j + tt] * w_ref[c, k:k + 1, :]
            y_ref[c] = acc
            return carry

        lax.fori_loop(0, N_SLABS, slab, 0)
        ys = [y_ref[c] for c in range(N_SLABS)]
        outs = _ln_swish_slabs(ys, cb_ref, lng_ref, lnb_ref, BF16)
        for c in range(N_SLABS):
            c_ref[pl.ds(t0, tt), c * LANES:(c + 1) * LANES] = outs[c]

    tile(head_ref, pad - CONV_PREFIX, 0)

    def body(j, carry):
        t0 = pl.multiple_of(j * tt, tt)
        tile(u_ref, t0 - CONV_PREFIX, t0)
        return carry

    lax.fori_loop(1, seq // tt, body, 0)
    for c in range(N_SLABS):
        st_ref[:, c * LANES:(c + 1) * LANES] = u_ref[c, seq - CONV_PREFIX:seq, :]


def _conv_prompt(u_slabs, w_slabs, conv_b, ln_g, ln_b, *, batch, seq):
    tt = 128
    return pl.pallas_call(
        functools.partial(_conv_prompt_kernel, seq=seq, tt=tt),
        grid=(batch,),
        in_specs=[
            pl.BlockSpec((N_SLABS, seq, LANES), lambda b: (0, b, 0)),
            pl.BlockSpec((N_SLABS, CONV_WIDTH, LANES), lambda b: (0, 0, 0)),
            pl.BlockSpec((1, C_CONV), lambda b: (0, 0)),
            pl.BlockSpec((1, C_CONV), lambda b: (0, 0)),
            pl.BlockSpec((1, C_CONV), lambda b: (0, 0)),
        ],
        out_specs=[
            pl.BlockSpec((seq, C_CONV), lambda b: (b, 0)),
            pl.BlockSpec((None, None, CONV_PREFIX, C_CONV), lambda b: (0, b, 0, 0)),
        ],
        out_shape=[
            jax.ShapeDtypeStruct((batch * seq, C_CONV), BF16),
            jax.ShapeDtypeStruct((1, batch, CONV_PREFIX, C_CONV), F32),
        ],
        scratch_shapes=[pltpu.VMEM((N_SLABS, 32 + tt, LANES), F32), pltpu.VMEM((N_SLABS, tt, LANES), F32)],
        compiler_params=_params(("parallel",), 40),
        name="conv_prompt",
    )(u_slabs, w_slabs, conv_b, ln_g, ln_b)


def _conv_sample_kernel(state_ref, u_ref, w_ref, cb_ref, lng_ref, lnb_ref, c_ref, st_ref, *, n_seq):
    w_hist = w_ref[0:CONV_PREFIX, :]
    w_new = w_ref[CONV_PREFIX:CONV_WIDTH, :]
    for n in range(n_seq):
        hist = state_ref[n]
        new = u_ref[n:n + 1, :]
        y = jnp.sum(hist * w_hist, axis=0, keepdims=True) + new * w_new
        ys = [y[:, c * LANES:(c + 1) * LANES] for c in range(N_SLABS)]
        outs = _ln_swish_slabs(ys, cb_ref, lng_ref, lnb_ref, F32)
        for c in range(N_SLABS):
            c_ref[n:n + 1, c * LANES:(c + 1) * LANES] = outs[c]
        st_ref[n, 0:CONV_PREFIX - 1, :] = state_ref[n, 1:CONV_PREFIX, :]
        st_ref[n, CONV_PREFIX - 1:CONV_PREFIX, :] = new


def _conv_sample(state, u, conv_w, conv_b, ln_g, ln_b):
    n_seq = state.shape[0]
    return pl.pallas_call(
        functools.partial(_conv_sample_kernel, n_seq=n_seq),
        out_shape=[
            jax.ShapeDtypeStruct((n_seq, C_CONV), F32),
            jax.ShapeDtypeStruct((n_seq, CONV_PREFIX, C_CONV), F32),
        ],
        compiler_params=pltpu.CompilerParams(vmem_limit_bytes=32 * MIB),
        name="conv_sample",
    )(state, u, conv_w, conv_b, ln_g, ln_b)


def _softmax_pv(s, v):
    m = jnp.max(s, axis=-1, keepdims=True)
    p = jnp.exp(s - m)
    l = jnp.sum(p, axis=-1, keepdims=True)
    o = _dot(p.astype(BF16), v) * (1.0 / l)
    return o, m + jnp.log(l)


def _qk(q, k):
    return lax.dot_general(q, k, (((1,), (1,)), ((), ())), preferred_element_type=F32)


def _attn_prompt_kernel(slopes_ref, q0_ref, q1_ref, q2_ref, k_ref, v_ref, qx_ref, mk_ref, mv_ref,
                        wg_ref, wu_ref, wd_ref, mix_ref, xm_ref, wg_out_ref, wu_out_ref, wd_out_ref,
                        bias_ref, o_ref, lse_ref, *, seq):
    head = pl.program_id(1)
    wg_out_ref[...] = wg_ref[...].astype(BF16)
    wu_out_ref[...] = wu_ref[...].astype(BF16)
    wd_out_ref[...] = wd_ref[...].astype(BF16)
    qi = lax.broadcasted_iota(jnp.int32, (SPAN, 2 * SPAN), 0)
    ci = lax.broadcasted_iota(jnp.int32, (SPAN, 2 * SPAN), 1)
    dist = qi + SPAN - ci
    valid = (dist >= 0) & (dist <= SPAN)
    for g, (_, dil) in enumerate(DIL_GROUPS):
        bias = -slopes_ref[g, head] * (dist * dil).astype(F32)
        bias_ref[g] = jnp.where(valid, bias, NEG)

    def rows(ref, start, size, dil):
        if dil == 1:
            return ref[pl.ds(start, size), :]
        return ref[pl.ds(start, size, stride=dil), :]

    def put(ref, g, start, dil, val):
        if dil == 1:
            ref[g, pl.ds(start, SPAN), :] = val
        else:
            ref[g, pl.ds(start, SPAN, stride=dil), :] = val

    def unit(g, dil, q_ref, start, n_keys):
        k_start = start - (n_keys - SPAN) * dil
        q = rows(q_ref, start, SPAN, dil).astype(BF16)
        k = rows(k_ref, k_start, n_keys, dil).astype(BF16)
        v = rows(v_ref, k_start, n_keys, dil).astype(BF16)
        s = _qk(q, k) * SCALE + bias_ref[g, :, 2 * SPAN - n_keys:2 * SPAN]
        o, lse = _softmax_pv(s, v)
        put(o_ref, g, start, dil, o)
        put(lse_ref, g, start, dil, jnp.broadcast_to(lse, (SPAN, LANES)))

    q_refs = (q0_ref, q1_ref, q2_ref)
    for g, (_, dil) in enumerate(DIL_GROUPS):
        n_blocks = seq // dil // SPAN
        q_ref = q_refs[g]
        if n_blocks == 1:
            def first_blocks(r, carry, g=g, dil=dil, q_ref=q_ref):
                unit(g, dil, q_ref, r, SPAN)
                return carry
            lax.fori_loop(0, dil, first_blocks, 0, unroll=8)
        elif dil == 1:
            unit(g, dil, q_ref, 0, SPAN)

            def later_blocks(n, carry, g=g, dil=dil, q_ref=q_ref):
                unit(g, dil, q_ref, pl.multiple_of(n * SPAN, SPAN), 2 * SPAN)
                return carry
            lax.fori_loop(1, n_blocks, later_blocks, 0, unroll=5)
        else:
            def residue(r, carry, g=g, dil=dil, q_ref=q_ref, n_blocks=n_blocks):
                unit(g, dil, q_ref, r, SPAN)
                for n in range(1, n_blocks):
                    unit(g, dil, q_ref, r + n * (SPAN * dil), 2 * SPAN)
                return carry
            lax.fori_loop(0, dil, residue, 0, unroll=2)

    tq = 256
    mk = mk_ref[...].astype(BF16)
    mv = mv_ref[...].astype(BF16)

    def finish(j, carry):
        t0 = pl.multiple_of(j * tq, tq)
        l0 = lse_ref[0, pl.ds(t0, tq), :]
        l1 = lse_ref[1, pl.ds(t0, tq), :]
        l2 = lse_ref[2, pl.ds(t0, tq), :]
        top = jnp.maximum(jnp.maximum(l0, l1), l2)
        w0 = jnp.exp(l0 - top)
        w1 = jnp.exp(l1 - top)
        w2 = jnp.exp(l2 - top)
        num = (w0 * o_ref[0, pl.ds(t0, tq), :] + w1 * o_ref[1, pl.ds(t0, tq), :]
               + w2 * o_ref[2, pl.ds(t0, tq), :])
        mix_ref[pl.ds(t0, tq), :] = (num / (w0 + w1 + w2)).astype(BF16)
        sx = _qk(qx_ref[pl.ds(t0, tq), :], mk) * SCALE
        ox, _ = _softmax_pv(sx, mv)
        xm_ref[pl.ds(t0, tq), :] = ox.astype(BF16)
        return carry

    lax.fori_loop(0, seq // tq, finish, 0, unroll=4)


def _attn_prompt(slopes, q, k, v, qx, mk, mv, w_gate, w_up, w_down, *, batch, seq):
    def col(c):
        return pl.BlockSpec((None, seq, LANES), lambda b, h, c=c: (b, 0, c * N_KV + h))

    head_spec = pl.BlockSpec((None, seq, LANES), lambda b, h: (b, 0, h))
    mem_spec = pl.BlockSpec((None, N_MEM, LANES), lambda b, h: (b, 0, h))
    n_steps = batch * N_KV

    def row_slice(w):
        return pl.BlockSpec((w.shape[0] // n_steps, w.shape[1]), lambda b, h: (b * N_KV + h, 0))

    weights = (w_gate, w_up, w_down)
    return pl.pallas_call(
        functools.partial(_attn_prompt_kernel, seq=seq),
        grid=(batch, N_KV),
        in_specs=[
            pl.BlockSpec(memory_space=pltpu.SMEM),
            col(0), col(1), col(2), head_spec, head_spec, head_spec, mem_spec, mem_spec,
        ] + [row_slice(w) for w in weights],
        out_specs=[head_spec, head_spec] + [row_slice(w) for w in weights],
        out_shape=[jax.ShapeDtypeStruct((batch, seq, W_KV), BF16),
                   jax.ShapeDtypeStruct((batch, seq, W_QX), BF16)]
        + [jax.ShapeDtypeStruct(w.shape, BF16) for w in weights],
        scratch_shapes=[
            pltpu.VMEM((N_GROUPS, SPAN, 2 * SPAN), F32),
            pltpu.VMEM((N_GROUPS, seq, LANES), F32),
            pltpu.VMEM((N_GROUPS, seq, LANES), F32),
        ],
        compiler_params=_params(("parallel", "parallel"), 56),
        name="attn_prompt",
    )(slopes, q, q, q, k, v, qx, mk, mv, *weights)


def _attn_sample_kernel(slopes_ref, q_ref, kn_ref, vn_ref, qx_ref, ck0_ref, ck1_ref, ck2_ref,
                        cv0_ref, cv1_ref, cv2_ref, mk_ref, mv_ref, *rest, bs):
    wcol_refs, (mix_ref, xm_ref, wgate_ref) = rest[:-3], rest[-3:]
    ck_refs = (ck0_ref, ck1_ref, ck2_ref)
    cv_refs = (cv0_ref, cv1_ref, cv2_ref)
    back = (SPAN - lax.broadcasted_iota(jnp.int32, (SPAN, 1), 0)).astype(F32)
    for c, wcol_ref in enumerate(wcol_refs):
        wgate_ref[:, c * _GATE_CAST_COLS:(c + 1) * _GATE_CAST_COLS] = wcol_ref[...].astype(BF16)

    def one(n, carry):
        for h in range(N_KV):
            hs = slice(h * LANES, (h + 1) * LANES)
            kn = kn_ref[n, :, hs]
            vn = vn_ref[n, :, hs]
            scores, s_new = [], []
            for g, (_, dil) in enumerate(DIL_GROUPS):
                qs = slice(g * W_KV + h * LANES, g * W_KV + (h + 1) * LANES)
                q = q_ref[n, :, qs]
                s = jnp.sum(ck_refs[g][n, :, h, :] * q, axis=-1, keepdims=True) * SCALE
                scores.append(s - slopes_ref[g, h] * (back * float(dil)))
                s_new.append(jnp.sum(kn * q, axis=-1, keepdims=True) * SCALE)
            top = s_new[0]
            for g in range(N_GROUPS):
                top = jnp.maximum(top, jnp.maximum(s_new[g], jnp.max(scores[g], axis=0, keepdims=True)))
            num = jnp.zeros((1, LANES), F32)
            den = jnp.zeros((1, 1), F32)
            for g in range(N_GROUPS):
                p = jnp.exp(scores[g] - top)
                p_new = jnp.exp(s_new[g] - top)
                num = num + jnp.sum(p * cv_refs[g][n, :, h, :], axis=0, keepdims=True) + p_new * vn
                den = den + jnp.sum(p, axis=0, keepdims=True) + p_new
            mix_ref[n, :, hs] = num / den
            qx = qx_ref[n, :, hs]
            sx = jnp.sum(mk_ref[n, :, h, :] * qx, axis=-1, keepdims=True) * SCALE
            px = jnp.exp(sx - jnp.max(sx, axis=0, keepdims=True))
            ox = jnp.sum(px * mv_ref[n, :, h, :], axis=0, keepdims=True) / jnp.sum(px, axis=0, keepdims=True)
            xm_ref[n, :, hs] = ox
        return carry

    lax.fori_loop(0, bs, one, 0)


def _attn_sample(slopes, q, kn, vn, qx, cache_k, cache_v, mem_k, mem_v, w_in):
    n_seq, win = cache_k.shape[0], cache_k.shape[1]
    bs = 2
    n_steps = n_seq // bs
    wrows = w_in.shape[0] // n_steps
    gate_col0 = _GATE_TILE0 * TN // _GATE_CAST_COLS
    wcol_specs = [pl.BlockSpec((wrows, _GATE_CAST_COLS), lambda i, c=c: (i, gate_col0 + c))
                  for c in range(3 * D_MODEL // _GATE_CAST_COLS)]
    row = lambda w: pl.BlockSpec((bs, 1, w), lambda i: (i, 0, 0))
    as_rows = lambda a: a.reshape(n_seq, 1, a.shape[-1])
    views_k, views_v, cache_specs = [], [], []
    for _, dil in DIL_GROUPS:
        views_k.append(cache_k.reshape(n_seq, win // dil, dil * N_KV, HEAD_DIM))
        views_v.append(cache_v.reshape(n_seq, win // dil, dil * N_KV, HEAD_DIM))
        rows = N_KV if dil == 1 else SUBLANES
        cache_specs.append(pl.BlockSpec((bs, SPAN, rows, HEAD_DIM),
                                        lambda i, nb=win // dil // SPAN: (i, nb - 1, 0, 0)))
    mem_spec = pl.BlockSpec((bs, N_MEM, N_XHEADS, HEAD_DIM), lambda i: (i, 0, 0, 0))
    return pl.pallas_call(
        functools.partial(_attn_sample_kernel, bs=bs),
        grid=(n_seq // bs,),
        in_specs=[pl.BlockSpec(memory_space=pltpu.SMEM), row(W_QD), row(W_KV), row(W_KV), row(W_QX)]
        + cache_specs + cache_specs + [mem_spec, mem_spec] + wcol_specs,
        out_specs=[row(W_KV), row(W_QX), pl.BlockSpec((wrows, 3 * D_MODEL), lambda i: (i, 0))],
        out_shape=[jax.ShapeDtypeStruct((n_seq, 1, W_KV), F32), jax.ShapeDtypeStruct((n_seq, 1, W_QX), F32),
                   jax.ShapeDtypeStruct((w_in.shape[0], 3 * D_MODEL), BF16)],
        compiler_params=_params(("parallel",), 48),
        name="attn_sample",
    )(slopes, as_rows(q), as_rows(kn), as_rows(vn), as_rows(qx), *views_k, *views_v, mem_k, mem_v,
      *([w_in] * len(wcol_specs)))


def _merge_kernel(h_ref, c_ref, mix_ref, xm_ref, wg0_ref, wg1_ref, wg2_ref, bg_ref, wc_ref, wd_ref, wx_ref,
                  wo_ref, z_ref, merged_ref):
    s = pl.program_id(1)

    @pl.when(s < _D_TILES)
    def _():
        h = h_ref[...]
        acc = None
        for br, (wg_ref, y_ref, wy_ref) in enumerate(
                ((wg0_ref, c_ref, wc_ref), (wg1_ref, mix_ref, wd_ref), (wg2_ref, xm_ref, wx_ref))):
            gate = jax.nn.sigmoid(_dot(h, wg_ref[...]) + bg_ref[br])
            term = gate * _dot(y_ref[...].astype(BF16), wy_ref[...])
            acc = term if acc is None else acc + term
        merged_ref[s] = acc.astype(BF16)

    @pl.when(s >= _D_TILES)
    def _():
        z = None
        for j in range(_D_TILES):
            part = _dot(merged_ref[j], wo_ref[j * TN:(j + 1) * TN, :])
            z = part if z is None else z + part
        z_ref[...] = z


def _merge(h, c, mix, xm, w_gates, b_gate, w_conv_out_t, w_dil_o_t, w_x_o_t, w_out_t, *, tm):
    m = h.shape[0]
    n1, n2 = _D_TILES, D_MODEL // TN_OUT

    n_rows = m // tm

    def tile1(s):
        return jnp.where(s < n1, s, 0)

    def gate_spec(br):
        return pl.BlockSpec((D_MODEL, TN), lambda i, s, br=br: (0, br * _D_TILES + tile1(s)))

    def colw(kdim):
        return pl.BlockSpec((kdim, TN), lambda i, s: (0, tile1(s)))

    def rowblk(w):
        return pl.BlockSpec((tm, w), lambda i, s: (_row_ahead(i, s, n1, n_rows), 0))

    return pl.pallas_call(
        _merge_kernel,
        grid=(m // tm, n1 + n2),
        in_specs=[
            rowblk(D_MODEL), rowblk(C_CONV), rowblk(W_KV), rowblk(W_QX),
            gate_spec(0), gate_spec(1), gate_spec(2),
            pl.BlockSpec((3, 1, TN), lambda i, s: (0, 0, tile1(s))),
            colw(C_CONV), colw(W_KV), colw(W_QX),
            pl.BlockSpec((D_MODEL, TN_OUT), lambda i, s: (0, jnp.maximum(s - n1, 0))),
        ],
        out_specs=pl.BlockSpec((tm, TN_OUT), lambda i, s: (i, jnp.maximum(s - n1, 0))),
        out_shape=jax.ShapeDtypeStruct((m, D_MODEL), F32),
        scratch_shapes=[pltpu.VMEM((n1, tm, TN), BF16)],
        compiler_params=_params(("parallel", "arbitrary"), 58),
        name="merge",
    )(h, c, mix, xm, w_gates, w_gates, w_gates, b_gate.reshape(3, 1, D_MODEL), w_conv_out_t, w_dil_o_t, w_x_o_t,
      w_out_t)


def _ffn_kernel(x_ref, z_ref, gmix_ref, gpre_ref, wga_ref, wgb_ref, wua_ref, wub_ref, wda_ref, wdb_ref, gpost_ref,
                out_ref, x1_ref, h_ref, *, n_steps):
    s = pl.program_id(1)
    kh, fh = D_MODEL // 2, TF // 2

    @pl.when(s == 0)
    def _():
        x1 = x_ref[...] + _rms_rows(z_ref[...], gmix_ref[...])
        x1_ref[...] = x1
        h_ref[...] = _rms_rows(x1, gpre_ref[...]).astype(BF16)
        out_ref[...] = jnp.zeros_like(out_ref)

    ha, hb = h_ref[:, :kh], h_ref[:, kh:]
    gate = _dot(ha, wga_ref[...]) + _dot(hb, wgb_ref[...])
    up = _dot(ha, wua_ref[...]) + _dot(hb, wub_ref[...])
    act = (gate * jax.nn.sigmoid(gate) * up).astype(BF16)
    out_ref[...] += _dot(act[:, :fh], wda_ref[...]) + _dot(act[:, fh:], wdb_ref[...])

    @pl.when(s == n_steps - 1)
    def _():
        out_ref[...] = x1_ref[...] + _rms_rows(out_ref[...], gpost_ref[...])


def _ffn(x, z, g_mix, g_pre, w_gate_t, w_up_t, w_down, g_post, *, tm):
    m = x.shape[0]
    n_steps = D_FF // TF
    rowblk = pl.BlockSpec((tm, D_MODEL), lambda i, s: (i, 0))
    gain = pl.BlockSpec((1, D_MODEL), lambda i, s: (0, 0))

    def col_half(half):
        return pl.BlockSpec((D_MODEL // 2, TF), lambda i, s: (half, s))

    def row_half(half):
        return pl.BlockSpec((TF // 2, D_MODEL), lambda i, s: (2 * s + half, 0))

    return pl.pallas_call(
        functools.partial(_ffn_kernel, n_steps=n_steps),
        grid=(m // tm, n_steps),
        in_specs=[
            rowblk, rowblk, gain, gain,
            col_half(0), col_half(1), col_half(0), col_half(1), row_half(0), row_half(1),
            gain,
        ],
        out_specs=rowblk,
        out_shape=jax.ShapeDtypeStruct((m, D_MODEL), F32),
        scratch_shapes=[pltpu.VMEM((tm, D_MODEL), F32), pltpu.VMEM((tm, D_MODEL), BF16)],
        compiler_params=_params(("parallel", "arbitrary"), 52),
        name="ffn",
    )(x, z, g_mix, g_pre, w_gate_t, w_gate_t, w_up_t, w_up_t, w_down, w_down, g_post)


def kernel(x_prompt, x_sample, cache_win_k, cache_win_v, state_conv, cache_mem_k, cache_mem_v, mem_prompt,
           g_pre_mix, w_in, b_gate, conv_w, conv_b, conv_ln_g, conv_ln_b, w_conv_out, w_dil_o, g_mem, w_mem_kv,
           w_x_o, w_out, g_post_mix, g_pre_ffn, w_ffn_gate, w_ffn_up, w_ffn_down, g_post_ffn):
    batch, seq, _ = x_prompt.shape
    n_seq = x_sample.shape[0]
    depth = w_in.shape[0]
    assert depth == 1 and x_sample.shape[1] == 1 and seq % (DIL_GROUPS[-1][1] * SPAN) == 0
    assert cache_win_k.shape[2] == DIL_GROUPS[-1][0]

    idx = jnp.arange(1, N_QH + 1, dtype=F32)
    slopes = jnp.exp2(-8.0 * idx / N_QH).reshape(N_GROUPS, N_KV)

    xp, xs = x_prompt.reshape(batch * seq, D_MODEL), x_sample.reshape(n_seq, D_MODEL)
    l = 0
    cast = lambda w: w[l].astype(BF16)
    w_in_t = w_in[l][:, :_GATE_TILE0 * TN].astype(BF16)
    w_conv_out_t, w_dil_o_t, w_x_o_t = cast(w_conv_out), cast(w_dil_o), cast(w_x_o)
    w_out_t, w_mem_kv_t = cast(w_out), cast(w_mem_kv)
    row = lambda a: a[l].reshape(1, -1)
    g_pre, g_post, g_ffn_pre, g_ffn_post = row(g_pre_mix), row(g_post_mix), row(g_pre_ffn), row(g_post_ffn)
    cb, lng, lnb = row(conv_b), row(conv_ln_g), row(conv_ln_b)
    conv_w_slabs = conv_w[l].reshape(CONV_WIDTH, N_SLABS, LANES).transpose(1, 0, 2)

    def dense_tail(h, c, mix, xm, x, w_gates, ffn_weights, tm_merge, tm_ffn):
        z = _merge(h, c, mix, xm, w_gates, b_gate[l], w_conv_out_t, w_dil_o_t, w_x_o_t, w_out_t, tm=tm_merge)
        return _ffn(x, z, g_post, g_ffn_pre, *ffn_weights, g_ffn_post, tm=tm_ffn)

    mk, mv = _mem_kv(mem_prompt.reshape(batch * N_MEM, D_MODEL), row(g_mem), w_mem_kv_t, tm=512)
    h_p, u_p, q_p, k_p, v_p, qx_p = _in_proj(xp, g_pre, w_in_t, tm=1024, slabs=True)
    c_p, conv_p = _conv_prompt(u_p, conv_w_slabs, cb, lng, lnb, batch=batch, seq=seq)
    mix_p, xm_p, *ffn_weights = _attn_prompt(
        slopes, q_p.reshape(batch, seq, W_QD), k_p.reshape(batch, seq, W_KV), v_p.reshape(batch, seq, W_KV),
        qx_p.reshape(batch, seq, W_QX), mk.reshape(batch, N_MEM, W_QX), mv.reshape(batch, N_MEM, W_QX),
        w_ffn_gate[l], w_ffn_up[l], w_ffn_down[l], batch=batch, seq=seq)

    h_s, u_s, q_s, k_s, v_s, qx_s = _in_proj(xs, g_pre, w_in_t, tm=n_seq, slabs=False)
    c_s, conv_s = _conv_sample(state_conv[l], u_s, conv_w[l], cb, lng, lnb)
    mix_s, xm_s, w_gates = _attn_sample(slopes, q_s, k_s, v_s, qx_s, cache_win_k[l], cache_win_v[l],
                                        cache_mem_k[l], cache_mem_v[l], w_in[l])

    y_p = dense_tail(h_p, c_p, mix_p.reshape(batch * seq, W_KV), xm_p.reshape(batch * seq, W_QX), xp,
                     w_gates, ffn_weights, 1024, 512)
    y_s = dense_tail(h_s, c_s, mix_s.reshape(n_seq, W_KV), xm_s.reshape(n_seq, W_QX), xs,
                     w_gates, ffn_weights, n_seq, n_seq)

    win = min(DIL_GROUPS[-1][0], seq)
    kv_shape = (1, batch, seq, N_KV, HEAD_DIM)
    mem_shape = (1, batch, N_MEM, N_XHEADS, HEAD_DIM)
    new_shape = (1, n_seq, 1, N_KV, HEAD_DIM)
    return (y_p.reshape(batch, seq, D_MODEL), y_s.reshape(n_seq, 1, D_MODEL),
            k_p.reshape(kv_shape)[:, :, seq - win:], v_p.reshape(kv_shape)[:, :, seq - win:], conv_p,
            mk.reshape(mem_shape), mv.reshape(mem_shape),
            k_s.reshape(new_shape), v_s.reshape(new_shape), conv_s.reshape(1, n_seq, CONV_PREFIX, C_CONV))
```

```python
import functools

import jax
import jax.numpy as jnp
from jax import lax
from jax.experimental import pallas as pl
from jax.experimental.pallas import tpu as pltpu

F32 = jnp.float32
BF16 = jnp.bfloat16

D_MODEL = 2048
C_CONV = D_MODEL // 2
CONV_WIDTH = 31
CONV_PREFIX = CONV_WIDTH - 1
HEAD_DIM = 128
N_KV = 4
DIL_GROUPS = ((128, 1), (512, 4), (2048, 16))
N_GROUPS = len(DIL_GROUPS)
N_QH = N_GROUPS * N_KV
SPAN = 128
N_MEM = 256
N_XHEADS = 4
W_QD = N_QH * HEAD_DIM
W_KV = N_KV * HEAD_DIM
W_QX = N_XHEADS * HEAD_DIM
D_FF = 5632
EPS = 1e-6
NEG = -1e30
SCALE = HEAD_DIM ** -0.5
LANES = 128
SUBLANES = 8
N_SLABS = C_CONV // LANES
MIB = 1024 * 1024

TN = 512
_A_TILES = C_CONV // TN
_Q_TILES = W_QD // TN
_KV_TILES = W_KV // TN
_QX_TILES = W_QX // TN
_STEP_Q0 = _A_TILES
_STEP_K0 = _STEP_Q0 + _Q_TILES
_STEP_V0 = _STEP_K0 + _KV_TILES
_STEP_X0 = _STEP_V0 + _KV_TILES
_IN_STEPS = _STEP_X0 + _QX_TILES
_GATE_TILE0 = 2 * _A_TILES + _Q_TILES + 2 * _KV_TILES + _QX_TILES
_D_TILES = D_MODEL // TN
_ATTN_UNITS = 8
_GATE_CAST_COLS = 1024
TN_OUT = 512
TF = 512


def _params(semantics, vmem_mib):
    return pltpu.CompilerParams(dimension_semantics=semantics, vmem_limit_bytes=vmem_mib * MIB)


def _rms_rows(x, g):
    return x * lax.rsqrt(jnp.mean(x * x, axis=-1, keepdims=True) + EPS) * g


def _dot(a, b):
    return jnp.dot(a, b, preferred_element_type=F32)


def _row_ahead(i, s, free_from, n_rows):
    return jnp.minimum(i + (s >= free_from).astype(jnp.int32), n_rows - 1)


def _in_proj_kernel(x_ref, g_ref, w1_ref, w2_ref, h_ref, u_ref, q_ref, k_ref, v_ref, qx_ref, *, slabs):
    s = pl.program_id(1)

    @pl.when(s == 0)
    def _():
        h_ref[...] = _rms_rows(x_ref[...], g_ref[...]).astype(BF16)

    @pl.when(s < _STEP_Q0)
    def _():
        h = h_ref[...]
        u = _dot(h, w1_ref[...]) * jax.nn.sigmoid(_dot(h, w2_ref[...]))
        if slabs:
            for c in range(TN // LANES):
                u_ref[c] = u[:, c * LANES:(c + 1) * LANES]
        else:
            u_ref[...] = u

    @pl.when((s >= _STEP_Q0) & (s < _STEP_K0))
    def _():
        q_ref[...] = _dot(h_ref[...], w1_ref[...])

    @pl.when((s >= _STEP_K0) & (s < _STEP_V0))
    def _():
        k_ref[...] = _dot(h_ref[...], w1_ref[...])

    @pl.when((s >= _STEP_V0) & (s < _STEP_X0))
    def _():
        v_ref[...] = _dot(h_ref[...], w1_ref[...])

    @pl.when(s >= _STEP_X0)
    def _():
        qx_ref[...] = _dot(h_ref[...], w1_ref[...]).astype(qx_ref.dtype)


def _in_proj(x, g, w_in_t, *, tm, slabs):
    m = x.shape[0]
    last_a = _A_TILES - 1
    if slabs:
        u_shape = jax.ShapeDtypeStruct((N_SLABS, m, LANES), F32)
        u_spec = pl.BlockSpec((TN // LANES, tm, LANES), lambda i, s: (jnp.minimum(s, last_a), i, 0))
    else:
        u_shape = jax.ShapeDtypeStruct((m, C_CONV), F32)
        u_spec = pl.BlockSpec((tm, TN), lambda i, s: (i, jnp.minimum(s, last_a)))

    def out_tiles(step0, n_tiles):
        return pl.BlockSpec((tm, TN), lambda i, s: (i, jnp.clip(s - step0, 0, n_tiles - 1)))

    return pl.pallas_call(
        functools.partial(_in_proj_kernel, slabs=slabs),
        grid=(m // tm, _IN_STEPS),
        in_specs=[
            pl.BlockSpec((tm, D_MODEL), lambda i, s: (_row_ahead(i, s, 1, m // tm), 0)),
            pl.BlockSpec((1, D_MODEL), lambda i, s: (0, 0)),
            pl.BlockSpec((D_MODEL, TN), lambda i, s: (0, jnp.where(s < _A_TILES, s, s + _A_TILES))),
            pl.BlockSpec((D_MODEL, TN), lambda i, s: (0, jnp.where(s < _A_TILES, s, 0) + _A_TILES)),
        ],
        out_specs=[
            pl.BlockSpec((tm, D_MODEL), lambda i, s: (i, 0)),
            u_spec,
            out_tiles(_STEP_Q0, _Q_TILES),
            out_tiles(_STEP_K0, _KV_TILES),
            out_tiles(_STEP_V0, _KV_TILES),
            out_tiles(_STEP_X0, _QX_TILES),
        ],
        out_shape=[
            jax.ShapeDtypeStruct((m, D_MODEL), BF16),
            u_shape,
            jax.ShapeDtypeStruct((m, W_QD), F32),
            jax.ShapeDtypeStruct((m, W_KV), F32),
            jax.ShapeDtypeStruct((m, W_KV), F32),
            jax.ShapeDtypeStruct((m, W_QX), BF16 if slabs else F32),
        ],
        compiler_params=_params(("parallel", "arbitrary"), 58),
        name="in_proj",
    )(x, g, w_in_t, w_in_t)


def _mem_kv_kernel(x_ref, g_ref, w_ref, mk_ref, mv_ref, h_ref):
    s = pl.program_id(1)

    @pl.when(s == 0)
    def _():
        h_ref[...] = _rms_rows(x_ref[...], g_ref[...]).astype(BF16)
        mk_ref[...] = _dot(h_ref[...], w_ref[...])

    @pl.when(s == 1)
    def _():
        mv_ref[...] = _dot(h_ref[...], w_ref[...])


def _mem_kv(mem, g, w_t, *, tm):
    m = mem.shape[0]
    return pl.pallas_call(
        _mem_kv_kernel,
        grid=(m // tm, 2),
        in_specs=[
            pl.BlockSpec((tm, D_MODEL), lambda i, s: (i, 0)),
            pl.BlockSpec((1, D_MODEL), lambda i, s: (0, 0)),
            pl.BlockSpec((D_MODEL, W_QX), lambda i, s: (0, s)),
        ],
        out_specs=[
            pl.BlockSpec((tm, W_QX), lambda i, s: (i, 0)),
            pl.BlockSpec((tm, W_QX), lambda i, s: (i, 0)),
        ],
        out_shape=[jax.ShapeDtypeStruct((m, W_QX), F32)] * 2,
        scratch_shapes=[pltpu.VMEM((tm, D_MODEL), BF16)],
        compiler_params=_params(("parallel", "arbitrary"), 32),
        name="mem_kv",
    )(mem, g, w_t)


def _ln_swish_slabs(y_slabs, cb_ref, lng_ref, lnb_ref, out_dtype):
    ys = [y + cb_ref[:, c * LANES:(c + 1) * LANES] for c, y in enumerate(y_slabs)]
    tot = ys[0]
    for y in ys[1:]:
        tot = tot + y
    mu = jnp.sum(tot, axis=-1, keepdims=True) * (1.0 / C_CONV)
    ds = [y - mu for y in ys]
    sq = ds[0] * ds[0]
    for d in ds[1:]:
        sq = sq + d * d
    rs = lax.rsqrt(jnp.sum(sq, axis=-1, keepdims=True) * (1.0 / C_CONV) + EPS)
    outs = []
    for c, d in enumerate(ds):
        z = d * rs * lng_ref[:, c * LANES:(c + 1) * LANES] + lnb_ref[:, c * LANES:(c + 1) * LANES]
        outs.append((z * jax.nn.sigmoid(z)).astype(out_dtype))
    return outs


def _conv_prompt_kernel(u_ref, w_ref, cb_ref, lng_ref, lnb_ref, c_ref, st_ref, head_ref, y_ref, *, seq, tt):
    pad = 32
    head_ref[:, 0:pad, :] = jnp.zeros((N_SLABS, pad, LANES), F32)
    head_ref[:, pad:pad + tt, :] = u_ref[:, 0:tt, :]

    def tile(src_ref, base, t0):
        def slab(c, carry):
            acc = jnp.zeros((tt, LANES), F32)
            for j in range(SUBLANES):
                taps = range(j, CONV_WIDTH, SUBLANES)
                win = src_ref[c, pl.ds(base + j, tt + taps[-1] - j), :]
                for k in taps:
                    acc = acc + win[k - j:k - j + tt] * w_ref[c, k:k + 1, :]
            y_ref[c] = acc
            return carry

        lax.fori_loop(0, N_SLABS, slab, 0)
        ys = [y_ref[c] for c in range(N_SLABS)]
        outs = _ln_swish_slabs(ys, cb_ref, lng_ref, lnb_ref, BF16)
        for c in range(N_SLABS):
            c_ref[pl.ds(t0, tt), c * LANES:(c + 1) * LANES] = outs[c]

    tile(head_ref, pad - CONV_PREFIX, 0)

    def body(j, carry):
        t0 = pl.multiple_of(j * tt, tt)
        tile(u_ref, t0 - CONV_PREFIX, t0)
        return carry

    lax.fori_loop(1, seq // tt, body, 0)
    for c in range(N_SLABS):
        st_ref[:, c * LANES:(c + 1) * LANES] = u_ref[c, seq - CONV_PREFIX:seq, :]


def _conv_prompt(u_slabs, w_slabs, conv_b, ln_g, ln_b, *, batch, seq):
    tt = 128
    return pl.pallas_call(
        functools.partial(_conv_prompt_kernel, seq=seq, tt=tt),
        grid=(batch,),
        in_specs=[
            pl.BlockSpec((N_SLABS, seq, LANES), lambda b: (0, b, 0)),
            pl.BlockSpec((N_SLABS, CONV_WIDTH, LANES), lambda b: (0, 0, 0)),
            pl.BlockSpec((1, C_CONV), lambda b: (0, 0)),
            pl.BlockSpec((1, C_CONV), lambda b: (0, 0)),
            pl.BlockSpec((1, C_CONV), lambda b: (0, 0)),
        ],
        out_specs=[
            pl.BlockSpec((seq, C_CONV), lambda b: (b, 0)),
            pl.BlockSpec((None, None, CONV_PREFIX, C_CONV), lambda b: (0, b, 0, 0)),
        ],
        out_shape=[
            jax.ShapeDtypeStruct((batch * seq, C_CONV), BF16),
            jax.ShapeDtypeStruct((1, batch, CONV_PREFIX, C_CONV), F32),
        ],
        scratch_shapes=[pltpu.VMEM((N_SLABS, 32 + tt, LANES), F32), pltpu.VMEM((N_SLABS, tt, LANES), F32)],
        compiler_params=_params(("parallel",), 40),
        name="conv_prompt",
    )(u_slabs, w_slabs, conv_b, ln_g, ln_b)


def _conv_sample_kernel(state_ref, u_ref, w_ref, cb_ref, lng_ref, lnb_ref, c_ref, st_ref, *, n_seq):
    w_hist = w_ref[0:CONV_PREFIX, :]
    w_new = w_ref[CONV_PREFIX:CONV_WIDTH, :]
    for n in range(n_seq):
        hist = state_ref[n]
        new = u_ref[n:n + 1, :]
        y = jnp.sum(hist * w_hist, axis=0, keepdims=True) + new * w_new
        ys = [y[:, c * LANES:(c + 1) * LANES] for c in range(N_SLABS)]
        outs = _ln_swish_slabs(ys, cb_ref, lng_ref, lnb_ref, F32)
        for c in range(N_SLABS):
            c_ref[n:n + 1, c * LANES:(c + 1) * LANES] = outs[c]
        st_ref[n, 0:CONV_PREFIX - 1, :] = state_ref[n, 1:CONV_PREFIX, :]
        st_ref[n, CONV_PREFIX - 1:CONV_PREFIX, :] = new


def _conv_sample(state, u, conv_w, conv_b, ln_g, ln_b):
    n_seq = state.shape[0]
    return pl.pallas_call(
        functools.partial(_conv_sample_kernel, n_seq=n_seq),
        out_shape=[
            jax.ShapeDtypeStruct((n_seq, C_CONV), F32),
            jax.ShapeDtypeStruct((n_seq, CONV_PREFIX, C_CONV), F32),
        ],
        compiler_params=pltpu.CompilerParams(vmem_limit_bytes=32 * MIB),
        name="conv_sample",
    )(state, u, conv_w, conv_b, ln_g, ln_b)


def _qk(q, k):
    return lax.dot_general(q, k, (((1,), (1,)), ((), ())), preferred_element_type=F32)


def _attn_prompt_kernel(slopes_ref, q0_ref, q1_ref, q2_ref, k_ref, v_ref, qx_ref, mk_ref, mv_ref,
                        wg_ref, wu_ref, wd_ref, mix_ref, xm_ref, wg_out_ref, wu_out_ref, wd_out_ref,
                        bias_ref, o_ref, lse_ref, s_ref, p_ref, m_ref, ones_ref, *, seq):
    head = pl.program_id(1)
    ones_ref[...] = jnp.ones(ones_ref.shape, BF16)
    wg_out_ref[...] = wg_ref[...].astype(BF16)
    wu_out_ref[...] = wu_ref[...].astype(BF16)
    wd_out_ref[...] = wd_ref[...].astype(BF16)
    qi = lax.broadcasted_iota(jnp.int32, (SPAN, 2 * SPAN), 0)
    ci = lax.broadcasted_iota(jnp.int32, (SPAN, 2 * SPAN), 1)
    dist = qi + SPAN - ci
    valid = (dist >= 0) & (dist <= SPAN)
    for g, (_, dil) in enumerate(DIL_GROUPS):
        bias = -slopes_ref[g, head] * (dist * dil).astype(F32)
        bias_ref[g] = jnp.where(valid, bias, NEG)

    def rows(ref, start, size, dil):
        if dil == 1:
            return ref[pl.ds(start, size), :]
        return ref[pl.ds(start, size, stride=dil), :]

    def put(ref, g, start, dil, val):
        if dil == 1:
            ref[g, pl.ds(start, SPAN), :] = val
        else:
            ref[g, pl.ds(start, SPAN, stride=dil), :] = val

    def staged(items, n_keys_of, scores, values, emit):
        for u, item in enumerate(items):
            s_ref[u, :, 0:n_keys_of(item)] = scores(item)
        for u, item in enumerate(items):
            s = s_ref[u, :, 0:n_keys_of(item)]
            m = jnp.max(s, axis=-1, keepdims=True)
            p_ref[u, :, 0:n_keys_of(item)] = jnp.exp(s - m).astype(BF16)
            m_ref[u] = jnp.broadcast_to(m, (SPAN, LANES))
        for u, item in enumerate(items):
            p = p_ref[u, :, 0:n_keys_of(item)]
            l = _dot(p, ones_ref[0:n_keys_of(item), :])
            emit(item, _dot(p, values(item)) * (1.0 / l), m_ref[u] + jnp.log(l))

    def units(g, dil, q_ref, starts_keys):
        def scores(item):
            start, n_keys = item
            q = rows(q_ref, start, SPAN, dil).astype(BF16)
            k = rows(k_ref, start - (n_keys - SPAN) * dil, n_keys, dil).astype(BF16)
            return _qk(q, k) * SCALE + bias_ref[g, :, 2 * SPAN - n_keys:2 * SPAN]

        def values(item):
            start, n_keys = item
            return rows(v_ref, start - (n_keys - SPAN) * dil, n_keys, dil).astype(BF16)

        def emit(item, o, lse):
            put(o_ref, g, item[0], dil, o)
            put(lse_ref, g, item[0], dil, lse)

        staged(starts_keys, lambda item: item[1], scores, values, emit)

    q_refs = (q0_ref, q1_ref, q2_ref)
    for g, (_, dil) in enumerate(DIL_GROUPS):
        n_blocks = seq // dil // SPAN
        q_ref = q_refs[g]
        if n_blocks == 1:
            per_body = 8

            def first_blocks(j, carry, g=g, dil=dil, q_ref=q_ref):
                units(g, dil, q_ref, [(j * per_body + u, SPAN) for u in range(per_body)])
                return carry
            lax.fori_loop(0, dil // per_body, first_blocks, 0)
        elif dil == 1:
            per_body = _ATTN_UNITS

            def later_blocks(j, carry, g=g, dil=dil, q_ref=q_ref):
                units(g, dil, q_ref, [(pl.multiple_of((j * per_body + u) * SPAN, SPAN), 2 * SPAN)
                                      for u in range(per_body)])
                return carry
            units(g, dil, q_ref, [(0, SPAN)] + [(n * SPAN, 2 * SPAN) for n in range(1, per_body)])
            lax.fori_loop(1, n_blocks // per_body, later_blocks, 0)
        else:
            def residue(j, carry, g=g, dil=dil, q_ref=q_ref, n_blocks=n_blocks):
                work = []
                for r in (2 * j, 2 * j + 1):
                    work.append((r, SPAN))
                    work += [(r + n * (SPAN * dil), 2 * SPAN) for n in range(1, n_blocks)]
                units(g, dil, q_ref, work)
                return carry
            lax.fori_loop(0, dil // 2, residue, 0)

    tq = 256

    def mix(j, carry):
        t0 = pl.multiple_of(j * tq, tq)
        l0 = lse_ref[0, pl.ds(t0, tq), :]
        l1 = lse_ref[1, pl.ds(t0, tq), :]
        l2 = lse_ref[2, pl.ds(t0, tq), :]
        top = jnp.maximum(jnp.maximum(l0, l1), l2)
        w0 = jnp.exp(l0 - top)
        w1 = jnp.exp(l1 - top)
        w2 = jnp.exp(l2 - top)
        num = (w0 * o_ref[0, pl.ds(t0, tq), :] + w1 * o_ref[1, pl.ds(t0, tq), :]
               + w2 * o_ref[2, pl.ds(t0, tq), :])
        mix_ref[pl.ds(t0, tq), :] = (num / (w0 + w1 + w2)).astype(BF16)
        return carry

    lax.fori_loop(0, seq // tq, mix, 0, unroll=2)

    def memory(j, carry):
        def scores(t0):
            return _qk(qx_ref[pl.ds(t0, SPAN), :], mk_ref[...].astype(BF16)) * SCALE

        def emit(t0, o, lse):
            xm_ref[pl.ds(t0, SPAN), :] = o.astype(BF16)

        staged([pl.multiple_of((j * _ATTN_UNITS + u) * SPAN, SPAN) for u in range(_ATTN_UNITS)],
               lambda t0: N_MEM, scores, lambda t0: mv_ref[...].astype(BF16), emit)
        return carry

    lax.fori_loop(0, seq // SPAN // _ATTN_UNITS, memory, 0)


def _attn_prompt(slopes, q, k, v, qx, mk, mv, w_gate, w_up, w_down, *, batch, seq):
    def col(c):
        return pl.BlockSpec((None, seq, LANES), lambda b, h, c=c: (b, 0, c * N_KV + h))

    head_spec = pl.BlockSpec((None, seq, LANES), lambda b, h: (b, 0, h))
    mem_spec = pl.BlockSpec((None, N_MEM, LANES), lambda b, h: (b, 0, h))
    n_steps = batch * N_KV

    def row_slice(w):
        return pl.BlockSpec((w.shape[0] // n_steps, w.shape[1]), lambda b, h: (b * N_KV + h, 0))

    weights = (w_gate, w_up, w_down)
    return pl.pallas_call(
        functools.partial(_attn_prompt_kernel, seq=seq),
        grid=(batch, N_KV),
        in_specs=[
            pl.BlockSpec(memory_space=pltpu.SMEM),
            col(0), col(1), col(2), head_spec, head_spec, head_spec, mem_spec, mem_spec,
        ] + [row_slice(w) for w in weights],
        out_specs=[head_spec, head_spec] + [row_slice(w) for w in weights],
        out_shape=[jax.ShapeDtypeStruct((batch, seq, W_KV), BF16),
                   jax.ShapeDtypeStruct((batch, seq, W_QX), BF16)]
        + [jax.ShapeDtypeStruct(w.shape, BF16) for w in weights],
        scratch_shapes=[
            pltpu.VMEM((N_GROUPS, SPAN, 2 * SPAN), F32),
            pltpu.VMEM((N_GROUPS, seq, LANES), F32),
            pltpu.VMEM((N_GROUPS, seq, LANES), F32),
            pltpu.VMEM((_ATTN_UNITS, SPAN, 2 * SPAN), F32),
            pltpu.VMEM((_ATTN_UNITS, SPAN, 2 * SPAN), BF16),
            pltpu.VMEM((_ATTN_UNITS, SPAN, LANES), F32),
            pltpu.VMEM((2 * SPAN, LANES), BF16),
        ],
        compiler_params=_params(("parallel", "parallel"), 56),
        name="attn_prompt",
    )(slopes, q, q, q, k, v, qx, mk, mv, *weights)


def _attn_sample_kernel(slopes_ref, q_ref, kn_ref, vn_ref, qx_ref, ck0_ref, ck1_ref, ck2_ref,
                        cv0_ref, cv1_ref, cv2_ref, mk_ref, mv_ref, *rest, bs):
    wcol_refs, (mix_ref, xm_ref, wgate_ref) = rest[:-3], rest[-3:]
    ck_refs = (ck0_ref, ck1_ref, ck2_ref)
    cv_refs = (cv0_ref, cv1_ref, cv2_ref)
    back = (SPAN - lax.broadcasted_iota(jnp.int32, (SPAN, 1), 0)).astype(F32)
    for c, wcol_ref in enumerate(wcol_refs):
        wgate_ref[:, c * _GATE_CAST_COLS:(c + 1) * _GATE_CAST_COLS] = wcol_ref[...].astype(BF16)

    def one(n, carry):
        for h in range(N_KV):
            hs = slice(h * LANES, (h + 1) * LANES)
            kn = kn_ref[n, :, hs]
            vn = vn_ref[n, :, hs]
            scores, s_new = [], []
            for g, (_, dil) in enumerate(DIL_GROUPS):
                qs = slice(g * W_KV + h * LANES, g * W_KV + (h + 1) * LANES)
                q = q_ref[n, :, qs]
                s = jnp.sum(ck_refs[g][n, :, h, :] * q, axis=-1, keepdims=True) * SCALE
                scores.append(s - slopes_ref[g, h] * (back * float(dil)))
                s_new.append(jnp.sum(kn * q, axis=-1, keepdims=True) * SCALE)
            top = s_new[0]
            for g in range(N_GROUPS):
                top = jnp.maximum(top, jnp.maximum(s_new[g], jnp.max(scores[g], axis=0, keepdims=True)))
            num = jnp.zeros((1, LANES), F32)
            den = jnp.zeros((1, 1), F32)
            for g in range(N_GROUPS):
                p = jnp.exp(scores[g] - top)
                p_new = jnp.exp(s_new[g] - top)
                num = num + jnp.sum(p * cv_refs[g][n, :, h, :], axis=0, keepdims=True) + p_new * vn
                den = den + jnp.sum(p, axis=0, keepdims=True) + p_new
            mix_ref[n, :, hs] = num / den
            qx = qx_ref[n, :, hs]
            sx = jnp.sum(mk_ref[n, :, h, :] * qx, axis=-1, keepdims=True) * SCALE
            px = jnp.exp(sx - jnp.max(sx, axis=0, keepdims=True))
            ox = jnp.sum(px * mv_ref[n, :, h, :], axis=0, keepdims=True) / jnp.sum(px, axis=0, keepdims=True)
            xm_ref[n, :, hs] = ox
        return carry

    lax.fori_loop(0, bs, one, 0)


def _attn_sample(slopes, q, kn, vn, qx, cache_k, cache_v, mem_k, mem_v, w_in):
    n_seq, win = cache_k.shape[0], cache_k.shape[1]
    bs = 2
    n_steps = n_seq // bs
    wrows = w_in.shape[0] // n_steps
    gate_col0 = _GATE_TILE0 * TN // _GATE_CAST_COLS
    wcol_specs = [pl.BlockSpec((wrows, _GATE_CAST_COLS), lambda i, c=c: (i, gate_col0 + c))
                  for c in range(3 * D_MODEL // _GATE_CAST_COLS)]
    row = lambda w: pl.BlockSpec((bs, 1, w), lambda i: (i, 0, 0))
    as_rows = lambda a: a.reshape(n_seq, 1, a.shape[-1])
    views_k, views_v, cache_specs = [], [], []
    for _, dil in DIL_GROUPS:
        views_k.append(cache_k.reshape(n_seq, win // dil, dil * N_KV, HEAD_DIM))
        views_v.append(cache_v.reshape(n_seq, win // dil, dil * N_KV, HEAD_DIM))
        rows = N_KV if dil == 1 else SUBLANES
        cache_specs.append(pl.BlockSpec((bs, SPAN, rows, HEAD_DIM),
                                        lambda i, nb=win // dil // SPAN: (i, nb - 1, 0, 0)))
    mem_spec = pl.BlockSpec((bs, N_MEM, N_XHEADS, HEAD_DIM), lambda i: (i, 0, 0, 0))
    return pl.pallas_call(
        functools.partial(_attn_sample_kernel, bs=bs),
        grid=(n_seq // bs,),
        in_specs=[pl.BlockSpec(memory_space=pltpu.SMEM), row(W_QD), row(W_KV), row(W_KV), row(W_QX)]
        + cache_specs + cache_specs + [mem_spec, mem_spec] + wcol_specs,
        out_specs=[row(W_KV), row(W_QX), pl.BlockSpec((wrows, 3 * D_MODEL), lambda i: (i, 0))],
        out_shape=[jax.ShapeDtypeStruct((n_seq, 1, W_KV), F32), jax.ShapeDtypeStruct((n_seq, 1, W_QX), F32),
                   jax.ShapeDtypeStruct((w_in.shape[0], 3 * D_MODEL), BF16)],
        compiler_params=_params(("parallel",), 48),
        name="attn_sample",
    )(slopes, as_rows(q), as_rows(kn), as_rows(vn), as_rows(qx), *views_k, *views_v, mem_k, mem_v,
      *([w_in] * len(wcol_specs)))


def _merge_kernel(h_ref, c_ref, mix_ref, xm_ref, wg0_ref, wg1_ref, wg2_ref, bg_ref, wc_ref, wd_ref, wx_ref,
                  wo_ref, z_ref, merged_ref):
    s = pl.program_id(1)

    @pl.when(s < _D_TILES)
    def _():
        h = h_ref[...]
        acc = None
        for br, (wg_ref, y_ref, wy_ref) in enumerate(
                ((wg0_ref, c_ref, wc_ref), (wg1_ref, mix_ref, wd_ref), (wg2_ref, xm_ref, wx_ref))):
            gate = jax.nn.sigmoid(_dot(h, wg_ref[...]) + bg_ref[br])
            term = gate * _dot(y_ref[...].astype(BF16), wy_ref[...])
            acc = term if acc is None else acc + term
        merged_ref[s] = acc.astype(BF16)

    @pl.when(s >= _D_TILES)
    def _():
        z = None
        for j in range(_D_TILES):
            part = _dot(merged_ref[j], wo_ref[j * TN:(j + 1) * TN, :])
            z = part if z is None else z + part
        z_ref[...] = z


def _merge(h, c, mix, xm, w_gates, b_gate, w_conv_out_t, w_dil_o_t, w_x_o_t, w_out_t, *, tm):
    m = h.shape[0]
    n1, n2 = _D_TILES, D_MODEL // TN_OUT

    n_rows = m // tm

    def tile1(s):
        return jnp.where(s < n1, s, 0)

    def gate_spec(br):
        return pl.BlockSpec((D_MODEL, TN), lambda i, s, br=br: (0, br * _D_TILES + tile1(s)))

    def colw(kdim):
        return pl.BlockSpec((kdim, TN), lambda i, s: (0, tile1(s)))

    def rowblk(w):
        return pl.BlockSpec((tm, w), lambda i, s: (_row_ahead(i, s, n1, n_rows), 0))

    return pl.pallas_call(
        _merge_kernel,
        grid=(m // tm, n1 + n2),
        in_specs=[
            rowblk(D_MODEL), rowblk(C_CONV), rowblk(W_KV), rowblk(W_QX),
            gate_spec(0), gate_spec(1), gate_spec(2),
            pl.BlockSpec((3, 1, TN), lambda i, s: (0, 0, tile1(s))),
            colw(C_CONV), colw(W_KV), colw(W_QX),
            pl.BlockSpec((D_MODEL, TN_OUT), lambda i, s: (0, jnp.maximum(s - n1, 0))),
        ],
        out_specs=pl.BlockSpec((tm, TN_OUT), lambda i, s: (i, jnp.maximum(s - n1, 0))),
        out_shape=jax.ShapeDtypeStruct((m, D_MODEL), F32),
        scratch_shapes=[pltpu.VMEM((n1, tm, TN), BF16)],
        compiler_params=_params(("parallel", "arbitrary"), 58),
        name="merge",
    )(h, c, mix, xm, w_gates, w_gates, w_gates, b_gate.reshape(3, 1, D_MODEL), w_conv_out_t, w_dil_o_t, w_x_o_t,
      w_out_t)


def _ffn_kernel(x_ref, z_ref, gmix_ref, gpre_ref, wg_ref, wu_ref, wd_ref, gpost_ref, out_ref, x1_ref, h_ref,
                *, n_steps):
    s = pl.program_id(1)

    @pl.when(s == 0)
    def _():
        x1 = x_ref[...] + _rms_rows(z_ref[...], gmix_ref[...])
        x1_ref[...] = x1
        h_ref[...] = _rms_rows(x1, gpre_ref[...]).astype(BF16)
        out_ref[...] = jnp.zeros_like(out_ref)

    h = h_ref[...]
    gate = _dot(h, wg_ref[...])
    act = (gate * jax.nn.sigmoid(gate) * _dot(h, wu_ref[...])).astype(BF16)
    out_ref[...] += _dot(act, wd_ref[...])

    @pl.when(s == n_steps - 1)
    def _():
        out_ref[...] = x1_ref[...] + _rms_rows(out_ref[...], gpost_ref[...])


def _ffn(x, z, g_mix, g_pre, w_gate_t, w_up_t, w_down, g_post, *, tm):
    m = x.shape[0]
    n_steps = D_FF // TF
    rowblk = pl.BlockSpec((tm, D_MODEL), lambda i, s: (i, 0))
    gain = pl.BlockSpec((1, D_MODEL), lambda i, s: (0, 0))
    return pl.pallas_call(
        functools.partial(_ffn_kernel, n_steps=n_steps),
        grid=(m // tm, n_steps),
        in_specs=[
            rowblk, rowblk, gain, gain,
            pl.BlockSpec((D_MODEL, TF), lambda i, s: (0, s)),
            pl.BlockSpec((D_MODEL, TF), lambda i, s: (0, s)),
            pl.BlockSpec((TF, D_MODEL), lambda i, s: (s, 0)),
            gain,
        ],
        out_specs=rowblk,
        out_shape=jax.ShapeDtypeStruct((m, D_MODEL), F32),
        scratch_shapes=[pltpu.VMEM((tm, D_MODEL), F32), pltpu.VMEM((tm, D_MODEL), BF16)],
        compiler_params=_params(("parallel", "arbitrary"), 52),
        name="ffn",
    )(x, z, g_mix, g_pre, w_gate_t, w_up_t, w_down, g_post)


def kernel(x_prompt, x_sample, cache_win_k, cache_win_v, state_conv, cache_mem_k, cache_mem_v, mem_prompt,
           g_pre_mix, w_in, b_gate, conv_w, conv_b, conv_ln_g, conv_ln_b, w_conv_out, w_dil_o, g_mem, w_mem_kv,
           w_x_o, w_out, g_post_mix, g_pre_ffn, w_ffn_gate, w_ffn_up, w_ffn_down, g_post_ffn):
    batch, seq, _ = x_prompt.shape
    n_seq = x_sample.shape[0]
    depth = w_in.shape[0]
    assert depth == 1 and x_sample.shape[1] == 1 and seq % (DIL_GROUPS[-1][1] * SPAN) == 0
    assert cache_win_k.shape[2] == DIL_GROUPS[-1][0]

    idx = jnp.arange(1, N_QH + 1, dtype=F32)
    slopes = jnp.exp2(-8.0 * idx / N_QH).reshape(N_GROUPS, N_KV)

    xp, xs = x_prompt.reshape(batch * seq, D_MODEL), x_sample.reshape(n_seq, D_MODEL)
    l = 0
    cast = lambda w: w[l].astype(BF16)
    w_in_t = w_in[l][:, :_GATE_TILE0 * TN].astype(BF16)
    w_conv_out_t, w_dil_o_t, w_x_o_t = cast(w_conv_out), cast(w_dil_o), cast(w_x_o)
    w_out_t, w_mem_kv_t = cast(w_out), cast(w_mem_kv)
    row = lambda a: a[l].reshape(1, -1)
    g_pre, g_post, g_ffn_pre, g_ffn_post = row(g_pre_mix), row(g_post_mix), row(g_pre_ffn), row(g_post_ffn)
    cb, lng, lnb = row(conv_b), row(conv_ln_g), row(conv_ln_b)
    conv_w_slabs = conv_w[l].reshape(CONV_WIDTH, N_SLABS, LANES).transpose(1, 0, 2)

    def dense_tail(h, c, mix, xm, x, w_gates, ffn_weights, tm_merge, tm_ffn):
        z = _merge(h, c, mix, xm, w_gates, b_gate[l], w_conv_out_t, w_dil_o_t, w_x_o_t, w_out_t, tm=tm_merge)
        return _ffn(x, z, g_post, g_ffn_pre, *ffn_weights, g_ffn_post, tm=tm_ffn)

    mk, mv = _mem_kv(mem_prompt.reshape(batch * N_MEM, D_MODEL), row(g_mem), w_mem_kv_t, tm=512)
    h_p, u_p, q_p, k_p, v_p, qx_p = _in_proj(xp, g_pre, w_in_t, tm=1024, slabs=True)
    c_p, conv_p = _conv_prompt(u_p, conv_w_slabs, cb, lng, lnb, batch=batch, seq=seq)
    mix_p, xm_p, *ffn_weights = _attn_prompt(
        slopes, q_p.reshape(batch, seq, W_QD), k_p.reshape(batch, seq, W_KV), v_p.reshape(batch, seq, W_KV),
        qx_p.reshape(batch, seq, W_QX), mk.reshape(batch, N_MEM, W_QX), mv.reshape(batch, N_MEM, W_QX),
        w_ffn_gate[l], w_ffn_up[l], w_ffn_down[l], batch=batch, seq=seq)

    h_s, u_s, q_s, k_s, v_s, qx_s = _in_proj(xs, g_pre, w_in_t, tm=n_seq, slabs=False)
    c_s, conv_s = _conv_sample(state_conv[l], u_s, conv_w[l], cb, lng, lnb)
    mix_s, xm_s, w_gates = _attn_sample(slopes, q_s, k_s, v_s, qx_s, cache_win_k[l], cache_win_v[l],
                                        cache_mem_k[l], cache_mem_v[l], w_in[l])

    y_p = dense_tail(h_p, c_p, mix_p.reshape(batch * seq, W_KV), xm_p.reshape(batch * seq, W_QX), xp,
                     w_gates, ffn_weights, 1024, 512)
    y_s = dense_tail(h_s, c_s, mix_s.reshape(n_seq, W_KV), xm_s.reshape(n_seq, W_QX), xs,
                     w_gates, ffn_weights, n_seq, n_seq)

    win = min(DIL_GROUPS[-1][0], seq)
    kv_shape = (1, batch, seq, N_KV, HEAD_DIM)
    mem_shape = (1, batch, N_MEM, N_XHEADS, HEAD_DIM)
    new_shape = (1, n_seq, 1, N_KV, HEAD_DIM)
    return (y_p.reshape(batch, seq, D_MODEL), y_s.reshape(n_seq, 1, D_MODEL),
            k_p.reshape(kv_shape)[:, :, seq - win:], v_p.reshape(kv_shape)[:, :, seq - win:], conv_p,
            mk.reshape(mem_shape), mv.reshape(mem_shape),
            k_s.reshape(new_shape), v_s.reshape(new_shape), conv_s.reshape(1, n_seq, CONV_PREFIX, C_CONV))
```

```python
import functools

import jax
import jax.numpy as jnp
from jax import lax
from jax.experimental import pallas as pl
from jax.experimental.pallas import tpu as pltpu

F32 = jnp.float32
BF16 = jnp.bfloat16

D_MODEL = 2048
C_CONV = D_MODEL // 2
CONV_WIDTH = 31
CONV_PREFIX = CONV_WIDTH - 1
HEAD_DIM = 128
N_KV = 4
DIL_GROUPS = ((128, 1), (512, 4), (2048, 16))
N_GROUPS = len(DIL_GROUPS)
N_QH = N_GROUPS * N_KV
SPAN = 128
N_MEM = 256
N_XHEADS = 4
W_QD = N_QH * HEAD_DIM
W_KV = N_KV * HEAD_DIM
W_QX = N_XHEADS * HEAD_DIM
D_FF = 5632
EPS = 1e-6
NEG = -1e30
SCALE = HEAD_DIM ** -0.5
LANES = 128
SUBLANES = 8
N_SLABS = C_CONV // LANES
MIB = 1024 * 1024

TN = 512
_A_TILES = C_CONV // TN
_Q_TILES = W_QD // TN
_KV_TILES = W_KV // TN
_QX_TILES = W_QX // TN
_STEP_Q0 = _A_TILES
_STEP_K0 = _STEP_Q0 + _Q_TILES
_STEP_V0 = _STEP_K0 + _KV_TILES
_STEP_X0 = _STEP_V0 + _KV_TILES
_IN_STEPS = _STEP_X0 + _QX_TILES
_GATE_TILE0 = 2 * _A_TILES + _Q_TILES + 2 * _KV_TILES + _QX_TILES
_D_TILES = D_MODEL // TN
_ATTN_UNITS = 8
_GATE_CAST_COLS = 1024
TN_OUT = 512
TF = 512


def _params(semantics, vmem_mib):
    return pltpu.CompilerParams(dimension_semantics=semantics, vmem_limit_bytes=vmem_mib * MIB)


def _rms_rows(x, g):
    return x * lax.rsqrt(jnp.mean(x * x, axis=-1, keepdims=True) + EPS) * g


def _dot(a, b):
    return jnp.dot(a, b, preferred_element_type=F32)


def _row_ahead(i, s, free_from, n_rows):
    return jnp.minimum(i + (s >= free_from).astype(jnp.int32), n_rows - 1)


def _in_proj_kernel(x_ref, g_ref, w1_ref, w2_ref, h_ref, u_ref, q_ref, k_ref, v_ref, qx_ref, *, slabs):
    s = pl.program_id(1)

    @pl.when(s == 0)
    def _():
        h_ref[...] = _rms_rows(x_ref[...], g_ref[...]).astype(BF16)

    @pl.when(s < _STEP_Q0)
    def _():
        h = h_ref[...]
        u = _dot(h, w1_ref[...]) * jax.nn.sigmoid(_dot(h, w2_ref[...]))
        if slabs:
            for c in range(TN // LANES):
                u_ref[c] = u[:, c * LANES:(c + 1) * LANES]
        else:
            u_ref[...] = u

    @pl.when((s >= _STEP_Q0) & (s < _STEP_K0))
    def _():
        q_ref[...] = _dot(h_ref[...], w1_ref[...])

    @pl.when((s >= _STEP_K0) & (s < _STEP_V0))
    def _():
        k_ref[...] = _dot(h_ref[...], w1_ref[...])

    @pl.when((s >= _STEP_V0) & (s < _STEP_X0))
    def _():
        v_ref[...] = _dot(h_ref[...], w1_ref[...])

    @pl.when(s >= _STEP_X0)
    def _():
        qx_ref[...] = _dot(h_ref[...], w1_ref[...]).astype(qx_ref.dtype)


def _in_proj(x, g, w_in_t, *, tm, slabs):
    m = x.shape[0]
    last_a = _A_TILES - 1
    if slabs:
        u_shape = jax.ShapeDtypeStruct((N_SLABS, m, LANES), F32)
        u_spec = pl.BlockSpec((TN // LANES, tm, LANES), lambda i, s: (jnp.minimum(s, last_a), i, 0))
    else:
        u_shape = jax.ShapeDtypeStruct((m, C_CONV), F32)
        u_spec = pl.BlockSpec((tm, TN), lambda i, s: (i, jnp.minimum(s, last_a)))

    def out_tiles(step0, n_tiles):
        return pl.BlockSpec((tm, TN), lambda i, s: (i, jnp.clip(s - step0, 0, n_tiles - 1)))

    return pl.pallas_call(
        functools.partial(_in_proj_kernel, slabs=slabs),
        grid=(m // tm, _IN_STEPS),
        in_specs=[
            pl.BlockSpec((tm, D_MODEL), lambda i, s: (_row_ahead(i, s, 1, m // tm), 0)),
            pl.BlockSpec((1, D_MODEL), lambda i, s: (0, 0)),
            pl.BlockSpec((D_MODEL, TN), lambda i, s: (0, jnp.where(s < _A_TILES, s, s + _A_TILES))),
            pl.BlockSpec((D_MODEL, TN), lambda i, s: (0, jnp.where(s < _A_TILES, s, 0) + _A_TILES)),
        ],
        out_specs=[
            pl.BlockSpec((tm, D_MODEL), lambda i, s: (i, 0)),
            u_spec,
            out_tiles(_STEP_Q0, _Q_TILES),
            out_tiles(_STEP_K0, _KV_TILES),
            out_tiles(_STEP_V0, _KV_TILES),
            out_tiles(_STEP_X0, _QX_TILES),
        ],
        out_shape=[
            jax.ShapeDtypeStruct((m, D_MODEL), BF16),
            u_shape,
            jax.ShapeDtypeStruct((m, W_QD), F32),
            jax.ShapeDtypeStruct((m, W_KV), F32),
            jax.ShapeDtypeStruct((m, W_KV), F32),
            jax.ShapeDtypeStruct((m, W_QX), BF16 if slabs else F32),
        ],
        compiler_params=_params(("parallel", "arbitrary"), 58),
        name="in_proj",
    )(x, g, w_in_t, w_in_t)


def _mem_kv_kernel(x_ref, g_ref, w_ref, mk_ref, mv_ref, h_ref):
    s = pl.program_id(1)

    @pl.when(s == 0)
    def _():
        h_ref[...] = _rms_rows(x_ref[...], g_ref[...]).astype(BF16)
        mk_ref[...] = _dot(h_ref[...], w_ref[...])

    @pl.when(s == 1)
    def _():
        mv_ref[...] = _dot(h_ref[...], w_ref[...])


def _mem_kv(mem, g, w_t, *, tm):
    m = mem.shape[0]
    return pl.pallas_call(
        _mem_kv_kernel,
        grid=(m // tm, 2),
        in_specs=[
            pl.BlockSpec((tm, D_MODEL), lambda i, s: (i, 0)),
            pl.BlockSpec((1, D_MODEL), lambda i, s: (0, 0)),
            pl.BlockSpec((D_MODEL, W_QX), lambda i, s: (0, s)),
        ],
        out_specs=[
            pl.BlockSpec((tm, W_QX), lambda i, s: (i, 0)),
            pl.BlockSpec((tm, W_QX), lambda i, s: (i, 0)),
        ],
        out_shape=[jax.ShapeDtypeStruct((m, W_QX), F32)] * 2,
        scratch_shapes=[pltpu.VMEM((tm, D_MODEL), BF16)],
        compiler_params=_params(("parallel", "arbitrary"), 32),
        name="mem_kv",
    )(mem, g, w_t)


def _ln_swish_slabs(y_slabs, cb_ref, lng_ref, lnb_ref, out_dtype):
    ys = [y + cb_ref[:, c * LANES:(c + 1) * LANES] for c, y in enumerate(y_slabs)]
    tot = ys[0]
    for y in ys[1:]:
        tot = tot + y
    mu = jnp.sum(tot, axis=-1, keepdims=True) * (1.0 / C_CONV)
    ds = [y - mu for y in ys]
    sq = ds[0] * ds[0]
    for d in ds[1:]:
        sq = sq + d * d
    rs = lax.rsqrt(jnp.sum(sq, axis=-1, keepdims=True) * (1.0 / C_CONV) + EPS)
    outs = []
    for c, d in enumerate(ds):
        z = d * rs * lng_ref[:, c * LANES:(c + 1) * LANES] + lnb_ref[:, c * LANES:(c + 1) * LANES]
        outs.append((z * jax.nn.sigmoid(z)).astype(out_dtype))
    return outs


def _conv_prompt_kernel(u_ref, w_ref, cb_ref, lng_ref, lnb_ref, c_ref, st_ref, head_ref, y_ref, *, seq, tt):
    pad = 32
    head_ref[:, 0:pad, :] = jnp.zeros((N_SLABS, pad, LANES), F32)
    head_ref[:, pad:pad + tt, :] = u_ref[:, 0:tt, :]

    def tile(src_ref, base, t0):
        def slab(c, carry):
            acc = jnp.zeros((tt, LANES), F32)
            for j in range(SUBLANES):
                taps = range(j, CONV_WIDTH, SUBLANES)
                win = src_ref[c, pl.ds(base + j, tt + taps[-1] - j), :]
                for k in taps:
                    acc = acc + win[k - j:k - j + tt] * w_ref[c, k:k + 1, :]
            y_ref[c] = acc
            return carry

        lax.fori_loop(0, N_SLABS, slab, 0)
        ys = [y_ref[c] for c in range(N_SLABS)]
        outs = _ln_swish_slabs(ys, cb_ref, lng_ref, lnb_ref, BF16)
        for c in range(N_SLABS):
            c_ref[pl.ds(t0, tt), c * LANES:(c + 1) * LANES] = outs[c]

    tile(head_ref, pad - CONV_PREFIX, 0)

    def body(j, carry):
        t0 = pl.multiple_of(j * tt, tt)
        tile(u_ref, t0 - CONV_PREFIX, t0)
        return carry

    lax.fori_loop(1, seq // tt, body, 0)
    for c in range(N_SLABS):
        st_ref[:, c * LANES:(c + 1) * LANES] = u_ref[c, seq - CONV_PREFIX:seq, :]


def _conv_prompt(u_slabs, w_slabs, conv_b, ln_g, ln_b, *, batch, seq):
    tt = 128
    return pl.pallas_call(
        functools.partial(_conv_prompt_kernel, seq=seq, tt=tt),
        grid=(batch,),
        in_specs=[
            pl.BlockSpec((N_SLABS, seq, LANES), lambda b: (0, b, 0)),
            pl.BlockSpec((N_SLABS, CONV_WIDTH, LANES), lambda b: (0, 0, 0)),
            pl.BlockSpec((1, C_CONV), lambda b: (0, 0)),
            pl.BlockSpec((1, C_CONV), lambda b: (0, 0)),
            pl.BlockSpec((1, C_CONV), lambda b: (0, 0)),
        ],
        out_specs=[
            pl.BlockSpec((seq, C_CONV), lambda b: (b, 0)),
            pl.BlockSpec((None, None, CONV_PREFIX, C_CONV), lambda b: (0, b, 0, 0)),
        ],
        out_shape=[
            jax.ShapeDtypeStruct((batch * seq, C_CONV), BF16),
            jax.ShapeDtypeStruct((1, batch, CONV_PREFIX, C_CONV), F32),
        ],
        scratch_shapes=[pltpu.VMEM((N_SLABS, 32 + tt, LANES), F32), pltpu.VMEM((N_SLABS, tt, LANES), F32)],
        compiler_params=_params(("parallel",), 40),
        name="conv_prompt",
    )(u_slabs, w_slabs, conv_b, ln_g, ln_b)


def _conv_sample_kernel(state_ref, u_ref, w_ref, cb_ref, lng_ref, lnb_ref, c_ref, st_ref, *, n_seq):
    w_hist = w_ref[0:CONV_PREFIX, :]
    w_new = w_ref[CONV_PREFIX:CONV_WIDTH, :]
    for n in range(n_seq):
        hist = state_ref[n]
        new = u_ref[n:n + 1, :]
        y = jnp.sum(hist * w_hist, axis=0, keepdims=True) + new * w_new
        ys = [y[:, c * LANES:(c + 1) * LANES] for c in range(N_SLABS)]
        outs = _ln_swish_slabs(ys, cb_ref, lng_ref, lnb_ref, F32)
        for c in range(N_SLABS):
            c_ref[n:n + 1, c * LANES:(c + 1) * LANES] = outs[c]
        st_ref[n, 0:CONV_PREFIX - 1, :] = state_ref[n, 1:CONV_PREFIX, :]
        st_ref[n, CONV_PREFIX - 1:CONV_PREFIX, :] = new


def _conv_sample(state, u, conv_w, conv_b, ln_g, ln_b):
    n_seq = state.shape[0]
    return pl.pallas_call(
        functools.partial(_conv_sample_kernel, n_seq=n_seq),
        out_shape=[
            jax.ShapeDtypeStruct((n_seq, C_CONV), F32),
            jax.ShapeDtypeStruct((n_seq, CONV_PREFIX, C_CONV), F32),
        ],
        compiler_params=pltpu.CompilerParams(vmem_limit_bytes=32 * MIB),
        name="conv_sample",
    )(state, u, conv_w, conv_b, ln_g, ln_b)


def _qk(q, k):
    return lax.dot_general(q, k, (((1,), (1,)), ((), ())), preferred_element_type=F32)


def _attn_prompt_kernel(slopes_ref, q0_ref, q1_ref, q2_ref, k_ref, v_ref, qx_ref, mk_ref, mv_ref,
                        wg_ref, wu_ref, wd_ref, mix_ref, xm_ref, wg_out_ref, wu_out_ref, wd_out_ref,
                        bias_ref, o_ref, lse_ref, s_ref, p_ref, m_ref, ones_ref, *, seq):
    head = pl.program_id(1)
    ones_ref[...] = jnp.ones(ones_ref.shape, BF16)
    wg_out_ref[...] = wg_ref[...].astype(BF16)
    wu_out_ref[...] = wu_ref[...].astype(BF16)
    wd_out_ref[...] = wd_ref[...].astype(BF16)
    qi = lax.broadcasted_iota(jnp.int32, (SPAN, 2 * SPAN), 0)
    ci = lax.broadcasted_iota(jnp.int32, (SPAN, 2 * SPAN), 1)
    dist = qi + SPAN - ci
    valid = (dist >= 0) & (dist <= SPAN)
    for g, (_, dil) in enumerate(DIL_GROUPS):
        bias = -slopes_ref[g, head] * (dist * dil).astype(F32)
        bias_ref[g] = jnp.where(valid, bias, NEG)

    def rows(ref, start, size, dil):
        if dil == 1:
            return ref[pl.ds(start, size), :]
        return ref[pl.ds(start, size, stride=dil), :]

    def put(ref, g, start, dil, val):
        if dil == 1:
            ref[g, pl.ds(start, SPAN), :] = val
        else:
            ref[g, pl.ds(start, SPAN, stride=dil), :] = val

    def staged(items, n_keys_of, scores, values, emit):
        for u, item in enumerate(items):
            s_ref[u, :, 0:n_keys_of(item)] = scores(item)
        for u, item in enumerate(items):
            s = s_ref[u, :, 0:n_keys_of(item)]
            m = jnp.max(s, axis=-1, keepdims=True)
            p_ref[u, :, 0:n_keys_of(item)] = jnp.exp(s - m).astype(BF16)
            m_ref[u] = jnp.broadcast_to(m, (SPAN, LANES))
        for u, item in enumerate(items):
            p = p_ref[u, :, 0:n_keys_of(item)]
            l = _dot(p, ones_ref[0:n_keys_of(item), :])
            emit(item, _dot(p, values(item)) * (1.0 / l), m_ref[u] + jnp.log(l))

    def units(g, dil, q_ref, starts_keys):
        def scores(item):
            start, n_keys = item
            q = rows(q_ref, start, SPAN, dil).astype(BF16)
            k = rows(k_ref, start - (n_keys - SPAN) * dil, n_keys, dil).astype(BF16)
            return _qk(q, k) * SCALE + bias_ref[g, :, 2 * SPAN - n_keys:2 * SPAN]

        def values(item):
            start, n_keys = item
            return rows(v_ref, start - (n_keys - SPAN) * dil, n_keys, dil).astype(BF16)

        def emit(item, o, lse):
            put(o_ref, g, item[0], dil, o)
            put(lse_ref, g, item[0], dil, lse)

        staged(starts_keys, lambda item: item[1], scores, values, emit)

    q_refs = (q0_ref, q1_ref, q2_ref)
    for g, (_, dil) in enumerate(DIL_GROUPS):
        n_blocks = seq // dil // SPAN
        q_ref = q_refs[g]
        if n_blocks == 1:
            per_body = 8

            def first_blocks(j, carry, g=g, dil=dil, q_ref=q_ref):
                units(g, dil, q_ref, [(j * per_body + u, SPAN) for u in range(per_body)])
                return carry
            lax.fori_loop(0, dil // per_body, first_blocks, 0)
        elif dil == 1:
            per_body = _ATTN_UNITS

            def later_blocks(j, carry, g=g, dil=dil, q_ref=q_ref):
                units(g, dil, q_ref, [(pl.multiple_of((j * per_body + u) * SPAN, SPAN), 2 * SPAN)
                                      for u in range(per_body)])
                return carry
            units(g, dil, q_ref, [(0, SPAN)] + [(n * SPAN, 2 * SPAN) for n in range(1, per_body)])
            lax.fori_loop(1, n_blocks // per_body, later_blocks, 0)
        else:
            def residue(j, carry, g=g, dil=dil, q_ref=q_ref, n_blocks=n_blocks):
                work = []
                for r in (2 * j, 2 * j + 1):
                    work.append((r, SPAN))
                    work += [(r + n * (SPAN * dil), 2 * SPAN) for n in range(1, n_blocks)]
                units(g, dil, q_ref, work)
                return carry
            lax.fori_loop(0, dil // 2, residue, 0)

    tq = 256

    def mix(j, carry):
        t0 = pl.multiple_of(j * tq, tq)
        l0 = lse_ref[0, pl.ds(t0, tq), :]
        l1 = lse_ref[1, pl.ds(t0, tq), :]
        l2 = lse_ref[2, pl.ds(t0, tq), :]
        top = jnp.maximum(jnp.maximum(l0, l1), l2)
        w0 = jnp.exp(l0 - top)
        w1 = jnp.exp(l1 - top)
        w2 = jnp.exp(l2 - top)
        num = (w0 * o_ref[0, pl.ds(t0, tq), :] + w1 * o_ref[1, pl.ds(t0, tq), :]
               + w2 * o_ref[2, pl.ds(t0, tq), :])
        mix_ref[pl.ds(t0, tq), :] = (num / (w0 + w1 + w2)).astype(BF16)
        return carry

    lax.fori_loop(0, seq // tq, mix, 0, unroll=2)

    def memory(j, carry):
        def scores(t0):
            return _qk(qx_ref[pl.ds(t0, SPAN), :], mk_ref[...].astype(BF16)) * SCALE

        def emit(t0, o, lse):
            xm_ref[pl.ds(t0, SPAN), :] = o.astype(BF16)

        staged([pl.multiple_of((j * _ATTN_UNITS + u) * SPAN, SPAN) for u in range(_ATTN_UNITS)],
               lambda t0: N_MEM, scores, lambda t0: mv_ref[...].astype(BF16), emit)
        return carry

    lax.fori_loop(0, seq // SPAN // _ATTN_UNITS, memory, 0)


def _attn_prompt(slopes, q, k, v, qx, mk, mv, w_gate, w_up, w_down, *, batch, seq):
    def col(c):
        return pl.BlockSpec((None, seq, LANES), lambda b, h, c=c: (b, 0, c * N_KV + h))

    head_spec = pl.BlockSpec((None, seq, LANES), lambda b, h: (b, 0, h))
    mem_spec = pl.BlockSpec((None, N_MEM, LANES), lambda b, h: (b, 0, h))
    n_steps = batch * N_KV

    def row_slice(w):
        return pl.BlockSpec((w.shape[0] // n_steps, w.shape[1]), lambda b, h: (b * N_KV + h, 0))

    weights = (w_gate, w_up, w_down)
    return pl.pallas_call(
        functools.partial(_attn_prompt_kernel, seq=seq),
        grid=(batch, N_KV),
        in_specs=[
            pl.BlockSpec(memory_space=pltpu.SMEM),
            col(0), col(1), col(2), head_spec, head_spec, head_spec, mem_spec, mem_spec,
        ] + [row_slice(w) for w in weights],
        out_specs=[head_spec, head_spec] + [row_slice(w) for w in weights],
        out_shape=[jax.ShapeDtypeStruct((batch, seq, W_KV), BF16),
                   jax.ShapeDtypeStruct((batch, seq, W_QX), BF16)]
        + [jax.ShapeDtypeStruct(w.shape, BF16) for w in weights],
        scratch_shapes=[
            pltpu.VMEM((N_GROUPS, SPAN, 2 * SPAN), F32),
            pltpu.VMEM((N_GROUPS, seq, LANES), F32),
            pltpu.VMEM((N_GROUPS, seq, LANES), F32),
            pltpu.VMEM((_ATTN_UNITS, SPAN, 2 * SPAN), F32),
            pltpu.VMEM((_ATTN_UNITS, SPAN, 2 * SPAN), BF16),
            pltpu.VMEM((_ATTN_UNITS, SPAN, LANES), F32),
            pltpu.VMEM((2 * SPAN, LANES), BF16),
        ],
        compiler_params=_params(("parallel", "parallel"), 56),
        name="attn_prompt",
    )(slopes, q, q, q, k, v, qx, mk, mv, *weights)


def _attn_sample_kernel(slopes_ref, q_ref, kn_ref, vn_ref, qx_ref, ck0_ref, ck1_ref, ck2_ref,
                        cv0_ref, cv1_ref, cv2_ref, mk_ref, mv_ref, *rest, bs):
    wcol_refs, (mix_ref, xm_ref, wgate_ref) = rest[:-3], rest[-3:]

    def by_row(ref):
        return ref if len(ref.shape) == 3 else ref.reshape(ref.shape[0], ref.shape[1] * ref.shape[2], LANES)

    def head_rows(ref, n, h, n_keys):
        return ref[n, pl.ds(h, n_keys, stride=ref.shape[1] // n_keys), :]

    ck_refs = tuple(by_row(r) for r in (ck0_ref, ck1_ref, ck2_ref))
    cv_refs = tuple(by_row(r) for r in (cv0_ref, cv1_ref, cv2_ref))
    back =(SPAN - lax.broadcasted_iota(jnp.int32, (SPAN, 1), 0)).astype(F32)
    for c, wcol_ref in enumerate(wcol_refs):
        wgate_ref[:, c * _GATE_CAST_COLS:(c + 1) * _GATE_CAST_COLS] = wcol_ref[...].astype(BF16)

    def one(n, carry):
        for h in range(N_KV):
            hs = slice(h * LANES, (h + 1) * LANES)
            kn = kn_ref[n, :, hs]
            vn = vn_ref[n, :, hs]
            scores, s_new = [], []
            for g, (_, dil) in enumerate(DIL_GROUPS):
                qs = slice(g * W_KV + h * LANES, g * W_KV + (h + 1) * LANES)
                q = q_ref[n, :, qs]
                s = jnp.sum(head_rows(ck_refs[g], n, h, SPAN) * q, axis=-1, keepdims=True) * SCALE
                scores.append(s - slopes_ref[g, h] * (back * float(dil)))
                s_new.append(jnp.sum(kn * q, axis=-1, keepdims=True) * SCALE)
            top = s_new[0]
            for g in range(N_GROUPS):
                top = jnp.maximum(top, jnp.maximum(s_new[g], jnp.max(scores[g], axis=0, keepdims=True)))
            num = jnp.zeros((1, LANES), F32)
            den = jnp.zeros((1, 1), F32)
            for g in range(N_GROUPS):
                p = jnp.exp(scores[g] - top)
                p_new = jnp.exp(s_new[g] - top)
                num = num + jnp.sum(p * head_rows(cv_refs[g], n, h, SPAN), axis=0, keepdims=True) + p_new * vn
                den = den + jnp.sum(p, axis=0, keepdims=True) + p_new
            mix_ref[n, :, hs] = num / den
            qx = qx_ref[n, :, hs]
            sx = jnp.sum(head_rows(mk_ref, n, h, N_MEM) * qx, axis=-1, keepdims=True) * SCALE
            px = jnp.exp(sx - jnp.max(sx, axis=0, keepdims=True))
            ox = (jnp.sum(px * head_rows(mv_ref, n, h, N_MEM), axis=0, keepdims=True)
                  / jnp.sum(px, axis=0, keepdims=True))
            xm_ref[n, :, hs] = ox
        return carry

    lax.fori_loop(0, bs, one, 0)


def _attn_sample(slopes, q, kn, vn, qx, cache_k, cache_v, mem_k, mem_v, w_in):
    n_seq, win = cache_k.shape[0], cache_k.shape[1]
    bs = 2
    n_steps = n_seq // bs
    wrows = w_in.shape[0] // n_steps
    gate_col0 = _GATE_TILE0 * TN // _GATE_CAST_COLS
    wcol_specs = [pl.BlockSpec((wrows, _GATE_CAST_COLS), lambda i, c=c: (i, gate_col0 + c))
                  for c in range(3 * D_MODEL // _GATE_CAST_COLS)]
    row = lambda w: pl.BlockSpec((bs, 1, w), lambda i: (i, 0, 0))
    as_rows = lambda a: a.reshape(n_seq, 1, a.shape[-1])
    views_k, views_v, cache_specs = [], [], []
    for _, dil in DIL_GROUPS:
        nb = win // dil // SPAN
        if dil == 1:
            shape = (n_seq, win * N_KV, HEAD_DIM)
            cache_specs.append(pl.BlockSpec((bs, SPAN * N_KV, HEAD_DIM), lambda i, nb=nb: (i, nb - 1, 0)))
        else:
            shape = (n_seq, win // dil, dil * N_KV, HEAD_DIM)
            cache_specs.append(pl.BlockSpec((bs, SPAN, SUBLANES, HEAD_DIM), lambda i, nb=nb: (i, nb - 1, 0, 0)))
        views_k.append(cache_k.reshape(shape))
        views_v.append(cache_v.reshape(shape))
    mem_spec = pl.BlockSpec((bs, N_MEM * N_XHEADS, HEAD_DIM), lambda i: (i, 0, 0))
    mem_rows = lambda a: a.reshape(n_seq, N_MEM * N_XHEADS, HEAD_DIM)
    return pl.pallas_call(
        functools.partial(_attn_sample_kernel, bs=bs),
        grid=(n_seq // bs,),
        in_specs=[pl.BlockSpec(memory_space=pltpu.SMEM), row(W_QD), row(W_KV), row(W_KV), row(W_QX)]
        + cache_specs + cache_specs + [mem_spec, mem_spec] + wcol_specs,
        out_specs=[row(W_KV), row(W_QX), pl.BlockSpec((wrows, 3 * D_MODEL), lambda i: (i, 0))],
        out_shape=[jax.ShapeDtypeStruct((n_seq, 1, W_KV), F32), jax.ShapeDtypeStruct((n_seq, 1, W_QX), F32),
                   jax.ShapeDtypeStruct((w_in.shape[0], 3 * D_MODEL), BF16)],
        compiler_params=_params(("parallel",), 48),
        name="attn_sample",
    )(slopes, as_rows(q), as_rows(kn), as_rows(vn), as_rows(qx), *views_k, *views_v, mem_rows(mem_k),
      mem_rows(mem_v), *([w_in] * len(wcol_specs)))


def _merge_kernel(h_ref, c_ref, mix_ref, xm_ref, wg0_ref, wg1_ref, wg2_ref, bg_ref, wc_ref, wd_ref, wx_ref,
                  wo_ref, z_ref, merged_ref):
    s = pl.program_id(1)

    @pl.when(s < _D_TILES)
    def _():
        h = h_ref[...]
        acc = None
        for br, (wg_ref, y_ref, wy_ref) in enumerate(
                ((wg0_ref, c_ref, wc_ref), (wg1_ref, mix_ref, wd_ref), (wg2_ref, xm_ref, wx_ref))):
            gate = jax.nn.sigmoid(_dot(h, wg_ref[...]) + bg_ref[br])
            term = gate * _dot(y_ref[...].astype(BF16), wy_ref[...])
            acc = term if acc is None else acc + term
        merged_ref[s] = acc.astype(BF16)

    @pl.when(s >= _D_TILES)
    def _():
        z = None
        for j in range(_D_TILES):
            part = _dot(merged_ref[j], wo_ref[j * TN:(j + 1) * TN, :])
            z = part if z is None else z + part
        z_ref[...] = z


def _merge(h, c, mix, xm, w_gates, b_gate, w_conv_out_t, w_dil_o_t, w_x_o_t, w_out_t, *, tm):
    m = h.shape[0]
    n1, n2 = _D_TILES, D_MODEL // TN_OUT

    n_rows = m // tm

    def tile1(s):
        return jnp.where(s < n1, s, 0)

    def gate_spec(br):
        return pl.BlockSpec((D_MODEL, TN), lambda i, s, br=br: (0, br * _D_TILES + tile1(s)))

    def colw(kdim):
        return pl.BlockSpec((kdim, TN), lambda i, s: (0, tile1(s)))

    def rowblk(w):
        return pl.BlockSpec((tm, w), lambda i, s: (_row_ahead(i, s, n1, n_rows), 0))

    return pl.pallas_call(
        _merge_kernel,
        grid=(m // tm, n1 + n2),
        in_specs=[
            rowblk(D_MODEL), rowblk(C_CONV), rowblk(W_KV), rowblk(W_QX),
            gate_spec(0), gate_spec(1), gate_spec(2),
            pl.BlockSpec((3, 1, TN), lambda i, s: (0, 0, tile1(s))),
            colw(C_CONV), colw(W_KV), colw(W_QX),
            pl.BlockSpec((D_MODEL, TN_OUT), lambda i, s: (0, jnp.maximum(s - n1, 0))),
        ],
        out_specs=pl.BlockSpec((tm, TN_OUT), lambda i, s: (i, jnp.maximum(s - n1, 0))),
        out_shape=jax.ShapeDtypeStruct((m, D_MODEL), F32),
        scratch_shapes=[pltpu.VMEM((n1, tm, TN), BF16)],
        compiler_params=_params(("parallel", "arbitrary"), 58),
        name="merge",
    )(h, c, mix, xm, w_gates, w_gates, w_gates, b_gate.reshape(3, 1, D_MODEL), w_conv_out_t, w_dil_o_t, w_x_o_t,
      w_out_t)


def _ffn_kernel(x_ref, z_ref, gmix_ref, gpre_ref, wg_ref, wu_ref, wd_ref, gpost_ref, out_ref, x1_ref, h_ref,
                *, n_steps):
    s = pl.program_id(1)

    @pl.when(s == 0)
    def _():
        x1 = x_ref[...] + _rms_rows(z_ref[...], gmix_ref[...])
        x1_ref[...] = x1
        h_ref[...] = _rms_rows(x1, gpre_ref[...]).astype(BF16)
        out_ref[...] = jnp.zeros_like(out_ref)

    h = h_ref[...]
    gate = _dot(h, wg_ref[...])
    act = (gate * jax.nn.sigmoid(gate) * _dot(h, wu_ref[...])).astype(BF16)
    out_ref[...] += _dot(act, wd_ref[...])

    @pl.when(s == n_steps - 1)
    def _():
        out_ref[...] = x1_ref[...] + _rms_rows(out_ref[...], gpost_ref[...])


def _ffn(x, z, g_mix, g_pre, w_gate_t, w_up_t, w_down, g_post, *, tm):
    m = x.shape[0]
    n_steps = D_FF // TF
    rowblk = pl.BlockSpec((tm, D_MODEL), lambda i, s: (i, 0))
    gain = pl.BlockSpec((1, D_MODEL), lambda i, s: (0, 0))
    return pl.pallas_call(
        functools.partial(_ffn_kernel, n_steps=n_steps),
        grid=(m // tm, n_steps),
        in_specs=[
            rowblk, rowblk, gain, gain,
            pl.BlockSpec((D_MODEL, TF), lambda i, s: (0, s)),
            pl.BlockSpec((D_MODEL, TF), lambda i, s: (0, s)),
            pl.BlockSpec((TF, D_MODEL), lambda i, s: (s, 0)),
            gain,
        ],
        out_specs=rowblk,
        out_shape=jax.ShapeDtypeStruct((m, D_MODEL), F32),
        scratch_shapes=[pltpu.VMEM((tm, D_MODEL), F32), pltpu.VMEM((tm, D_MODEL), BF16)],
        compiler_params=_params(("parallel", "arbitrary"), 52),
        name="ffn",
    )(x, z, g_mix, g_pre, w_gate_t, w_up_t, w_down, g_post)


def kernel(x_prompt, x_sample, cache_win_k, cache_win_v, state_conv, cache_mem_k, cache_mem_v, mem_prompt,
           g_pre_mix, w_in, b_gate, conv_w, conv_b, conv_ln_g, conv_ln_b, w_conv_out, w_dil_o, g_mem, w_mem_kv,
           w_x_o, w_out, g_post_mix, g_pre_ffn, w_ffn_gate, w_ffn_up, w_ffn_down, g_post_ffn):
    batch, seq, _ = x_prompt.shape
    n_seq = x_sample.shape[0]
    depth = w_in.shape[0]
    assert depth == 1 and x_sample.shape[1] == 1 and seq % (DIL_GROUPS[-1][1] * SPAN) == 0
    assert cache_win_k.shape[2] == DIL_GROUPS[-1][0]

    idx = jnp.arange(1, N_QH + 1, dtype=F32)
    slopes = jnp.exp2(-8.0 * idx / N_QH).reshape(N_GROUPS, N_KV)

    xp, xs = x_prompt.reshape(batch * seq, D_MODEL), x_sample.reshape(n_seq, D_MODEL)
    l = 0
    cast = lambda w: w[l].astype(BF16)
    w_in_t = w_in[l][:, :_GATE_TILE0 * TN].astype(BF16)
    w_conv_out_t, w_dil_o_t, w_x_o_t = cast(w_conv_out), cast(w_dil_o), cast(w_x_o)
    w_out_t, w_mem_kv_t = cast(w_out), cast(w_mem_kv)
    row = lambda a: a[l].reshape(1, -1)
    g_pre, g_post, g_ffn_pre, g_ffn_post = row(g_pre_mix), row(g_post_mix), row(g_pre_ffn), row(g_post_ffn)
    cb, lng, lnb = row(conv_b), row(conv_ln_g), row(conv_ln_b)
    conv_w_slabs = conv_w[l].reshape(CONV_WIDTH, N_SLABS, LANES).transpose(1, 0, 2)

    def dense_tail(h, c, mix, xm, x, w_gates, ffn_weights, tm_merge, tm_ffn):
        z = _merge(h, c, mix, xm, w_gates, b_gate[l], w_conv_out_t, w_dil_o_t, w_x_o_t, w_out_t, tm=tm_merge)
        return _ffn(x, z, g_post, g_ffn_pre, *ffn_weights, g_ffn_post, tm=tm_ffn)

    mk, mv = _mem_kv(mem_prompt.reshape(batch * N_MEM, D_MODEL), row(g_mem), w_mem_kv_t, tm=512)
    h_p, u_p, q_p, k_p, v_p, qx_p = _in_proj(xp, g_pre, w_in_t, tm=1024, slabs=True)
    c_p, conv_p = _conv_prompt(u_p, conv_w_slabs, cb, lng, lnb, batch=batch, seq=seq)
    mix_p, xm_p, *ffn_weights = _attn_prompt(
        slopes, q_p.reshape(batch, seq, W_QD), k_p.reshape(batch, seq, W_KV), v_p.reshape(batch, seq, W_KV),
        qx_p.reshape(batch, seq, W_QX), mk.reshape(batch, N_MEM, W_QX), mv.reshape(batch, N_MEM, W_QX),
        w_ffn_gate[l], w_ffn_up[l], w_ffn_down[l], batch=batch, seq=seq)

    h_s, u_s, q_s, k_s, v_s, qx_s = _in_proj(xs, g_pre, w_in_t, tm=n_seq, slabs=False)
    c_s, conv_s = _conv_sample(state_conv[l], u_s, conv_w[l], cb, lng, lnb)
    mix_s, xm_s, w_gates = _attn_sample(slopes, q_s, k_s, v_s, qx_s, cache_win_k[l], cache_win_v[l],
                                        cache_mem_k[l], cache_mem_v[l], w_in[l])

    y_p = dense_tail(h_p, c_p, mix_p.reshape(batch * seq, W_KV), xm_p.reshape(batch * seq, W_QX), xp,
                     w_gates, ffn_weights, 1024, 512)
    y_s = dense_tail(h_s, c_s, mix_s.reshape(n_seq, W_KV), xm_s.reshape(n_seq, W_QX), xs,
                     w_gates, ffn_weights, n_seq, n_seq)

    win = min(DIL_GROUPS[-1][0], seq)
    kv_shape = (1, batch, seq, N_KV, HEAD_DIM)
    mem_shape = (1, batch, N_MEM, N_XHEADS, HEAD_DIM)
    new_shape = (1, n_seq, 1, N_KV, HEAD_DIM)
    return (y_p.reshape(batch, seq, D_MODEL), y_s.reshape(n_seq, 1, D_MODEL),
            k_p.reshape(kv_shape)[:, :, seq - win:], v_p.reshape(kv_shape)[:, :, seq - win:], conv_p,
            mk.reshape(mem_shape), mv.reshape(mem_shape),
            k_s.reshape(new_shape), v_s.reshape(new_shape), conv_s.reshape(1, n_seq, CONV_PREFIX, C_CONV))
```

```python
import functools

import jax
import jax.numpy as jnp
from jax import lax
from jax.experimental import pallas as pl
from jax.experimental.pallas import tpu as pltpu

F32 = jnp.float32
BF16 = jnp.bfloat16

D_MODEL = 2048
C_CONV = D_MODEL // 2
CONV_WIDTH = 31
CONV_PREFIX = CONV_WIDTH - 1
HEAD_DIM = 128
N_KV = 4
DIL_GROUPS = ((128, 1), (512, 4), (2048, 16))
N_GROUPS = len(DIL_GROUPS)
N_QH = N_GROUPS * N_KV
SPAN = 128
N_MEM = 256
N_XHEADS = 4
W_QD = N_QH * HEAD_DIM
W_KV = N_KV * HEAD_DIM
W_QX = N_XHEADS * HEAD_DIM
D_FF = 5632
EPS = 1e-6
NEG = -1e30
SCALE = HEAD_DIM ** -0.5
LANES = 128
SUBLANES = 8
N_SLABS = C_CONV // LANES
MIB = 1024 * 1024

TN = 512
_A_TILES = C_CONV // TN
_Q_TILES = W_QD // TN
_KV_TILES = W_KV // TN
_QX_TILES = W_QX // TN
_STEP_Q0 = _A_TILES
_STEP_K0 = _STEP_Q0 + _Q_TILES
_STEP_V0 = _STEP_K0 + _KV_TILES
_STEP_X0 = _STEP_V0 + _KV_TILES
_IN_STEPS = _STEP_X0 + _QX_TILES
_GATE_TILE0 = 2 * _A_TILES + _Q_TILES + 2 * _KV_TILES + _QX_TILES
_D_TILES = D_MODEL // TN
_ATTN_UNITS = 8
_GATE_CAST_COLS = 1024
TN_OUT = 512
TF = 512


def _params(semantics, vmem_mib):
    return pltpu.CompilerParams(dimension_semantics=semantics, vmem_limit_bytes=vmem_mib * MIB)


def _rms_rows(x, g):
    return x * lax.rsqrt(jnp.mean(x * x, axis=-1, keepdims=True) + EPS) * g


def _dot(a, b):
    return jnp.dot(a, b, preferred_element_type=F32)


def _row_ahead(i, s, free_from, n_rows):
    return jnp.minimum(i + (s >= free_from).astype(jnp.int32), n_rows - 1)


def _in_proj_kernel(x_ref, g_ref, w1_ref, w2_ref, h_ref, u_ref, q_ref, k_ref, v_ref, qx_ref, *, slabs):
    s = pl.program_id(1)

    @pl.when(s == 0)
    def _():
        h_ref[...] = _rms_rows(x_ref[...], g_ref[...]).astype(BF16)

    @pl.when(s < _STEP_Q0)
    def _():
        h = h_ref[...]
        u = _dot(h, w1_ref[...]) * jax.nn.sigmoid(_dot(h, w2_ref[...]))
        if slabs:
            for c in range(TN // LANES):
                u_ref[c] = u[:, c * LANES:(c + 1) * LANES]
        else:
            u_ref[...] = u

    @pl.when((s >= _STEP_Q0) & (s < _STEP_K0))
    def _():
        q_ref[...] = _dot(h_ref[...], w1_ref[...])

    @pl.when((s >= _STEP_K0) & (s < _STEP_V0))
    def _():
        k_ref[...] = _dot(h_ref[...], w1_ref[...])

    @pl.when((s >= _STEP_V0) & (s < _STEP_X0))
    def _():
        v_ref[...] = _dot(h_ref[...], w1_ref[...])

    @pl.when(s >= _STEP_X0)
    def _():
        qx_ref[...] = _dot(h_ref[...], w1_ref[...]).astype(qx_ref.dtype)


def _in_proj(x, g, w_in_t, *, tm, slabs):
    m = x.shape[0]
    last_a = _A_TILES - 1
    if slabs:
        u_shape = jax.ShapeDtypeStruct((N_SLABS, m, LANES), F32)
        u_spec = pl.BlockSpec((TN // LANES, tm, LANES), lambda i, s: (jnp.minimum(s, last_a), i, 0))
    else:
        u_shape = jax.ShapeDtypeStruct((m, C_CONV), F32)
        u_spec = pl.BlockSpec((tm, TN), lambda i, s: (i, jnp.minimum(s, last_a)))

    def out_tiles(step0, n_tiles):
        def index(i, s):
            moved = (s >= step0 + n_tiles) & (i < m // tm - 1)
            return (i + moved.astype(jnp.int32), jnp.where(moved, 0, jnp.clip(s - step0, 0, n_tiles - 1)))
        return pl.BlockSpec((tm, TN), index)

    return pl.pallas_call(
        functools.partial(_in_proj_kernel, slabs=slabs),
        grid=(m // tm, _IN_STEPS),
        in_specs=[
            pl.BlockSpec((tm, D_MODEL), lambda i, s: (_row_ahead(i, s, _A_TILES, m // tm), 0)),
            pl.BlockSpec((1, D_MODEL), lambda i, s: (0, 0)),
            pl.BlockSpec((D_MODEL, TN), lambda i, s: (0, jnp.where(s < _A_TILES, s, s + _A_TILES))),
            pl.BlockSpec((D_MODEL, TN), lambda i, s: (0, jnp.where(s < _A_TILES, s, 0) + _A_TILES)),
        ],
        out_specs=[
            pl.BlockSpec((tm, D_MODEL), lambda i, s: (i, 0)),
            u_spec,
            out_tiles(_STEP_Q0, _Q_TILES),
            out_tiles(_STEP_K0, _KV_TILES),
            out_tiles(_STEP_V0, _KV_TILES),
            out_tiles(_STEP_X0, _QX_TILES),
        ],
        out_shape=[
            jax.ShapeDtypeStruct((m, D_MODEL), BF16),
            u_shape,
            jax.ShapeDtypeStruct((m, W_QD), F32),
            jax.ShapeDtypeStruct((m, W_KV), F32),
            jax.ShapeDtypeStruct((m, W_KV), F32),
            jax.ShapeDtypeStruct((m, W_QX), BF16 if slabs else F32),
        ],
        compiler_params=_params(("parallel", "arbitrary"), 58),
        name="in_proj",
    )(x, g, w_in_t, w_in_t)


def _mem_kv_kernel(x_ref, g_ref, w_ref, mk_ref, mv_ref, h_ref):
    s = pl.program_id(1)

    @pl.when(s == 0)
    def _():
        h_ref[...] = _rms_rows(x_ref[...], g_ref[...]).astype(BF16)
        mk_ref[...] = _dot(h_ref[...], w_ref[...])

    @pl.when(s == 1)
    def _():
        mv_ref[...] = _dot(h_ref[...], w_ref[...])


def _mem_kv(mem, g, w_t, *, tm):
    m = mem.shape[0]
    return pl.pallas_call(
        _mem_kv_kernel,
        grid=(m // tm, 2),
        in_specs=[
            pl.BlockSpec((tm, D_MODEL), lambda i, s: (i, 0)),
            pl.BlockSpec((1, D_MODEL), lambda i, s: (0, 0)),
            pl.BlockSpec((D_MODEL, W_QX), lambda i, s: (0, s)),
        ],
        out_specs=[
            pl.BlockSpec((tm, W_QX), lambda i, s: (i, 0)),
            pl.BlockSpec((tm, W_QX), lambda i, s: (i, 0)),
        ],
        out_shape=[jax.ShapeDtypeStruct((m, W_QX), F32)] * 2,
        scratch_shapes=[pltpu.VMEM((tm, D_MODEL), BF16)],
        compiler_params=_params(("parallel", "arbitrary"), 32),
        name="mem_kv",
    )(mem, g, w_t)


def _ln_swish_slabs(y_slabs, cb_ref, lng_ref, lnb_ref, out_dtype):
    ys = [y + cb_ref[:, c * LANES:(c + 1) * LANES] for c, y in enumerate(y_slabs)]
    tot = ys[0]
    for y in ys[1:]:
        tot = tot + y
    mu = jnp.sum(tot, axis=-1, keepdims=True) * (1.0 / C_CONV)
    ds = [y - mu for y in ys]
    sq = ds[0] * ds[0]
    for d in ds[1:]:
        sq = sq + d * d
    rs = lax.rsqrt(jnp.sum(sq, axis=-1, keepdims=True) * (1.0 / C_CONV) + EPS)
    outs = []
    for c, d in enumerate(ds):
        z = d * rs * lng_ref[:, c * LANES:(c + 1) * LANES] + lnb_ref[:, c * LANES:(c + 1) * LANES]
        outs.append((z * jax.nn.sigmoid(z)).astype(out_dtype))
    return outs


def _conv_prompt_kernel(u_ref, w_ref, cb_ref, lng_ref, lnb_ref, c_ref, st_ref, head_ref, y_ref, *, seq, tt):
    pad = 32
    head_ref[:, 0:pad, :] = jnp.zeros((N_SLABS, pad, LANES), F32)
    head_ref[:, pad:pad + tt, :] = u_ref[:, 0:tt, :]

    def tile(src_ref, base, t0):
        def slab(c, carry):
            acc = jnp.zeros((tt, LANES), F32)
            for j in range(SUBLANES):
                taps = range(j, CONV_WIDTH, SUBLANES)
                win = src_ref[c, pl.ds(base + j, tt + taps[-1] - j), :]
                for k in taps:
                    acc = acc + win[k - j:k - j + tt] * w_ref[c, k:k + 1, :]
            y_ref[c] = acc
            return carry

        lax.fori_loop(0, N_SLABS, slab, 0)
        ys = [y_ref[c] for c in range(N_SLABS)]
        outs = _ln_swish_slabs(ys, cb_ref, lng_ref, lnb_ref, BF16)
        for c in range(N_SLABS):
            c_ref[pl.ds(t0, tt), c * LANES:(c + 1) * LANES] = outs[c]

    tile(head_ref, pad - CONV_PREFIX, 0)

    def body(j, carry):
        t0 = pl.multiple_of(j * tt, tt)
        tile(u_ref, t0 - CONV_PREFIX, t0)
        return carry

    lax.fori_loop(1, seq // tt, body, 0)
    for c in range(N_SLABS):
        st_ref[:, c * LANES:(c + 1) * LANES] = u_ref[c, seq - CONV_PREFIX:seq, :]


def _conv_prompt(u_slabs, w_slabs, conv_b, ln_g, ln_b, *, batch, seq):
    tt = 128
    return pl.pallas_call(
        functools.partial(_conv_prompt_kernel, seq=seq, tt=tt),
        grid=(batch,),
        in_specs=[
            pl.BlockSpec((N_SLABS, seq, LANES), lambda b: (0, b, 0)),
            pl.BlockSpec((N_SLABS, CONV_WIDTH, LANES), lambda b: (0, 0, 0)),
            pl.BlockSpec((1, C_CONV), lambda b: (0, 0)),
            pl.BlockSpec((1, C_CONV), lambda b: (0, 0)),
            pl.BlockSpec((1, C_CONV), lambda b: (0, 0)),
        ],
        out_specs=[
            pl.BlockSpec((seq, C_CONV), lambda b: (b, 0)),
            pl.BlockSpec((None, None, CONV_PREFIX, C_CONV), lambda b: (0, b, 0, 0)),
        ],
        out_shape=[
            jax.ShapeDtypeStruct((batch * seq, C_CONV), BF16),
            jax.ShapeDtypeStruct((1, batch, CONV_PREFIX, C_CONV), F32),
        ],
        scratch_shapes=[pltpu.VMEM((N_SLABS, 32 + tt, LANES), F32), pltpu.VMEM((N_SLABS, tt, LANES), F32)],
        compiler_params=_params(("parallel",), 40),
        name="conv_prompt",
    )(u_slabs, w_slabs, conv_b, ln_g, ln_b)


def _conv_sample_kernel(state_ref, u_ref, w_ref, cb_ref, lng_ref, lnb_ref, c_ref, st_ref, *, n_seq):
    w_hist = w_ref[0:CONV_PREFIX, :]
    w_new = w_ref[CONV_PREFIX:CONV_WIDTH, :]
    for n in range(n_seq):
        hist = state_ref[n]
        new = u_ref[n:n + 1, :]
        y = jnp.sum(hist * w_hist, axis=0, keepdims=True) + new * w_new
        ys = [y[:, c * LANES:(c + 1) * LANES] for c in range(N_SLABS)]
        outs = _ln_swish_slabs(ys, cb_ref, lng_ref, lnb_ref, F32)
        for c in range(N_SLABS):
            c_ref[n:n + 1, c * LANES:(c + 1) * LANES] = outs[c]
        st_ref[n, 0:CONV_PREFIX - 1, :] = state_ref[n, 1:CONV_PREFIX, :]
        st_ref[n, CONV_PREFIX - 1:CONV_PREFIX, :] = new


def _conv_sample(state, u, conv_w, conv_b, ln_g, ln_b):
    n_seq = state.shape[0]
    return pl.pallas_call(
        functools.partial(_conv_sample_kernel, n_seq=n_seq),
        out_shape=[
            jax.ShapeDtypeStruct((n_seq, C_CONV), F32),
            jax.ShapeDtypeStruct((n_seq, CONV_PREFIX, C_CONV), F32),
        ],
        compiler_params=pltpu.CompilerParams(vmem_limit_bytes=32 * MIB),
        name="conv_sample",
    )(state, u, conv_w, conv_b, ln_g, ln_b)


def _qk(q, k):
    return lax.dot_general(q, k, (((1,), (1,)), ((), ())), preferred_element_type=F32)


def _attn_prompt_kernel(slopes_ref, q0_ref, q1_ref, q2_ref, k_ref, v_ref, qx_ref, mk_ref, mv_ref,
                        wg_ref, wu_ref, wd_ref, mix_ref, xm_ref, wgu_out_ref, wd_out_ref,
                        bias_ref, o_ref, lse_ref, s_ref, p_ref, m_ref, ones_ref, *, seq):
    head = pl.program_id(1)
    ones_ref[...] = jnp.ones(ones_ref.shape, BF16)
    half = TF // 2
    for j in range(D_FF // half):
        cols = slice(j * half, (j + 1) * half)
        wgu_out_ref[:, 2 * j * half:(2 * j + 1) * half] = wg_ref[:, cols].astype(BF16)
        wgu_out_ref[:, (2 * j + 1) * half:(2 * j + 2) * half] = wu_ref[:, cols].astype(BF16)
    wd_out_ref[...] = wd_ref[...].astype(BF16)
    qi = lax.broadcasted_iota(jnp.int32, (SPAN, 2 * SPAN), 0)
    ci = lax.broadcasted_iota(jnp.int32, (SPAN, 2 * SPAN), 1)
    dist = qi + SPAN - ci
    valid = (dist >= 0) & (dist <= SPAN)
    for g, (_, dil) in enumerate(DIL_GROUPS):
        bias = -slopes_ref[g, head] * (dist * dil).astype(F32)
        bias_ref[g] = jnp.where(valid, bias, NEG)

    def rows(ref, start, size, dil):
        if dil == 1:
            return ref[pl.ds(start, size), :]
        return ref[pl.ds(start, size, stride=dil), :]

    def put(ref, g, start, dil, val):
        if dil == 1:
            ref[g, pl.ds(start, SPAN), :] = val
        else:
            ref[g, pl.ds(start, SPAN, stride=dil), :] = val

    def staged(items, n_keys_of, scores, values, emit):
        for u, item in enumerate(items):
            s_ref[u, :, 0:n_keys_of(item)] = scores(item)
        for u, item in enumerate(items):
            s = s_ref[u, :, 0:n_keys_of(item)]
            m = jnp.max(s, axis=-1, keepdims=True)
            p_ref[u, :, 0:n_keys_of(item)] = jnp.exp(s - m).astype(BF16)
            m_ref[u] = jnp.broadcast_to(m, (SPAN, LANES))
        for u, item in enumerate(items):
            p = p_ref[u, :, 0:n_keys_of(item)]
            l = _dot(p, ones_ref[0:n_keys_of(item), :])
            emit(item, _dot(p, values(item)) * (1.0 / l), m_ref[u] + jnp.log(l))

    def units(g, dil, q_ref, starts_keys):
        def scores(item):
            start, n_keys = item
            q = rows(q_ref, start, SPAN, dil).astype(BF16)
            k = rows(k_ref, start - (n_keys - SPAN) * dil, n_keys, dil).astype(BF16)
            return _qk(q, k) * SCALE + bias_ref[g, :, 2 * SPAN - n_keys:2 * SPAN]

        def values(item):
            start, n_keys = item
            return rows(v_ref, start - (n_keys - SPAN) * dil, n_keys, dil).astype(BF16)

        def emit(item, o, lse):
            put(o_ref, g, item[0], dil, o)
            put(lse_ref, g, item[0], dil, lse)

        staged(starts_keys, lambda item: item[1], scores, values, emit)

    q_refs = (q0_ref, q1_ref, q2_ref)
    for g, (_, dil) in enumerate(DIL_GROUPS):
        n_blocks = seq // dil // SPAN
        q_ref = q_refs[g]
        if n_blocks == 1:
            per_body = 8

            def first_blocks(j, carry, g=g, dil=dil, q_ref=q_ref):
                units(g, dil, q_ref, [(j * per_body + u, SPAN) for u in range(per_body)])
                return carry
            lax.fori_loop(0, dil // per_body, first_blocks, 0)
        elif dil == 1:
            per_body = _ATTN_UNITS

            def later_blocks(j, carry, g=g, dil=dil, q_ref=q_ref):
                units(g, dil, q_ref, [(pl.multiple_of((j * per_body + u) * SPAN, SPAN), 2 * SPAN)
                                      for u in range(per_body)])
                return carry
            units(g, dil, q_ref, [(0, SPAN)] + [(n * SPAN, 2 * SPAN) for n in range(1, per_body)])
            lax.fori_loop(1, n_blocks // per_body, later_blocks, 0)
        else:
            def residue(j, carry, g=g, dil=dil, q_ref=q_ref, n_blocks=n_blocks):
                work = []
                for r in (2 * j, 2 * j + 1):
                    work.append((r, SPAN))
                    work += [(r + n * (SPAN * dil), 2 * SPAN) for n in range(1, n_blocks)]
                units(g, dil, q_ref, work)
                return carry
            lax.fori_loop(0, dil // 2, residue, 0)

    tq = 256

    def mix(j, carry):
        t0 = pl.multiple_of(j * tq, tq)
        l0 = lse_ref[0, pl.ds(t0, tq), :]
        l1 = lse_ref[1, pl.ds(t0, tq), :]
        l2 = lse_ref[2, pl.ds(t0, tq), :]
        top = jnp.maximum(jnp.maximum(l0, l1), l2)
        w0 = jnp.exp(l0 - top)
        w1 = jnp.exp(l1 - top)
        w2 = jnp.exp(l2 - top)
        num = (w0 * o_ref[0, pl.ds(t0, tq), :] + w1 * o_ref[1, pl.ds(t0, tq), :]
               + w2 * o_ref[2, pl.ds(t0, tq), :])
        mix_ref[pl.ds(t0, tq), :] = (num / (w0 + w1 + w2)).astype(BF16)
        return carry

    lax.fori_loop(0, seq // tq, mix, 0, unroll=2)

    def memory(j, carry):
        def scores(t0):
            return _qk(qx_ref[pl.ds(t0, SPAN), :], mk_ref[...].astype(BF16)) * SCALE

        def emit(t0, o, lse):
            xm_ref[pl.ds(t0, SPAN), :] = o.astype(BF16)

        staged([pl.multiple_of((j * _ATTN_UNITS + u) * SPAN, SPAN) for u in range(_ATTN_UNITS)],
               lambda t0: N_MEM, scores, lambda t0: mv_ref[...].astype(BF16), emit)
        return carry

    lax.fori_loop(0, seq // SPAN // _ATTN_UNITS, memory, 0)


def _attn_prompt(slopes, q, k, v, qx, mk, mv, w_gate, w_up, w_down, *, batch, seq):
    def col(c):
        return pl.BlockSpec((None, seq, LANES), lambda b, h, c=c: (b, 0, c * N_KV + h))

    head_spec = pl.BlockSpec((None, seq, LANES), lambda b, h: (b, 0, h))
    mem_spec = pl.BlockSpec((None, N_MEM, LANES), lambda b, h: (b, 0, h))
    n_steps = batch * N_KV

    def row_slice(rows, cols):
        return pl.BlockSpec((rows // n_steps, cols), lambda b, h: (b * N_KV + h, 0))

    weights = (w_gate, w_up, w_down)
    gu_shape = (D_MODEL, 2 * D_FF)
    return pl.pallas_call(
        functools.partial(_attn_prompt_kernel, seq=seq),
        grid=(batch, N_KV),
        in_specs=[
            pl.BlockSpec(memory_space=pltpu.SMEM),
            col(0), col(1), col(2), head_spec, head_spec, head_spec, mem_spec, mem_spec,
        ] + [row_slice(*w.shape) for w in weights],
        out_specs=[head_spec, head_spec, row_slice(*gu_shape), row_slice(*w_down.shape)],
        out_shape=[jax.ShapeDtypeStruct((batch, seq, W_KV), BF16),
                   jax.ShapeDtypeStruct((batch, seq, W_QX), BF16),
                   jax.ShapeDtypeStruct(gu_shape, BF16), jax.ShapeDtypeStruct(w_down.shape, BF16)],
        scratch_shapes=[
            pltpu.VMEM((N_GROUPS, SPAN, 2 * SPAN), F32),
            pltpu.VMEM((N_GROUPS, seq, LANES), F32),
            pltpu.VMEM((N_GROUPS, seq, LANES), F32),
            pltpu.VMEM((_ATTN_UNITS, SPAN, 2 * SPAN), F32),
            pltpu.VMEM((_ATTN_UNITS, SPAN, 2 * SPAN), BF16),
            pltpu.VMEM((_ATTN_UNITS, SPAN, LANES), F32),
            pltpu.VMEM((2 * SPAN, LANES), BF16),
        ],
        compiler_params=_params(("parallel", "parallel"), 56),
        name="attn_prompt",
    )(slopes, q, q, q, k, v, qx, mk, mv, *weights)


def _attn_sample_kernel(slopes_ref, q_ref, kn_ref, vn_ref, qx_ref, ck0_ref, ck1_ref, ck2_ref,
                        cv0_ref, cv1_ref, cv2_ref, mk_ref, mv_ref, *rest, bs):
    wcol_refs, (mix_ref, xm_ref, wgate_ref) = rest[:-3], rest[-3:]

    def by_row(ref):
        return ref if len(ref.shape) == 3 else ref.reshape(ref.shape[0], ref.shape[1] * ref.shape[2], LANES)

    def head_rows(ref, n, h, n_keys):
        return ref[n, pl.ds(h, n_keys, stride=ref.shape[1] // n_keys), :]

    ck_refs = tuple(by_row(r) for r in (ck0_ref, ck1_ref, ck2_ref))
    cv_refs = tuple(by_row(r) for r in (cv0_ref, cv1_ref, cv2_ref))
    back =(SPAN - lax.broadcasted_iota(jnp.int32, (SPAN, 1), 0)).astype(F32)
    for c, wcol_ref in enumerate(wcol_refs):
        wgate_ref[:, c * _GATE_CAST_COLS:(c + 1) * _GATE_CAST_COLS] = wcol_ref[...].astype(BF16)

    def one(n, carry):
        for h in range(N_KV):
            hs = slice(h * LANES, (h + 1) * LANES)
            kn = kn_ref[n, :, hs]
            vn = vn_ref[n, :, hs]
            scores, s_new = [], []
            for g, (_, dil) in enumerate(DIL_GROUPS):
                qs = slice(g * W_KV + h * LANES, g * W_KV + (h + 1) * LANES)
                q = q_ref[n, :, qs]
                s = jnp.sum(head_rows(ck_refs[g], n, h, SPAN) * q, axis=-1, keepdims=True) * SCALE
                scores.append(s - slopes_ref[g, h] * (back * float(dil)))
                s_new.append(jnp.sum(kn * q, axis=-1, keepdims=True) * SCALE)
            top = s_new[0]
            for g in range(N_GROUPS):
                top = jnp.maximum(top, jnp.maximum(s_new[g], jnp.max(scores[g], axis=0, keepdims=True)))
            num = jnp.zeros((1, LANES), F32)
            den = jnp.zeros((1, 1), F32)
            for g in range(N_GROUPS):
                p = jnp.exp(scores[g] - top)
                p_new = jnp.exp(s_new[g] - top)
                num = num + jnp.sum(p * head_rows(cv_refs[g], n, h, SPAN), axis=0, keepdims=True) + p_new * vn
                den = den + jnp.sum(p, axis=0, keepdims=True) + p_new
            mix_ref[n, :, hs] = num / den
            qx = qx_ref[n, :, hs]
            sx = jnp.sum(head_rows(mk_ref, n, h, N_MEM) * qx, axis=-1, keepdims=True) * SCALE
            px = jnp.exp(sx - jnp.max(sx, axis=0, keepdims=True))
            ox = (jnp.sum(px * head_rows(mv_ref, n, h, N_MEM), axis=0, keepdims=True)
                  / jnp.sum(px, axis=0, keepdims=True))
            xm_ref[n, :, hs] = ox
        return carry

    lax.fori_loop(0, bs, one, 0)


def _attn_sample(slopes, q, kn, vn, qx, cache_k, cache_v, mem_k, mem_v, w_in):
    n_seq, win = cache_k.shape[0], cache_k.shape[1]
    bs = 2
    n_steps = n_seq // bs
    wrows = w_in.shape[0] // n_steps
    gate_col0 = _GATE_TILE0 * TN // _GATE_CAST_COLS
    wcol_specs = [pl.BlockSpec((wrows, _GATE_CAST_COLS), lambda i, c=c: (i, gate_col0 + c))
                  for c in range(3 * D_MODEL // _GATE_CAST_COLS)]
    row = lambda w: pl.BlockSpec((bs, 1, w), lambda i: (i, 0, 0))
    as_rows = lambda a: a.reshape(n_seq, 1, a.shape[-1])
    views_k, views_v, cache_specs = [], [], []
    for _, dil in DIL_GROUPS:
        nb = win // dil // SPAN
        if dil == 1:
            shape = (n_seq, win * N_KV, HEAD_DIM)
            cache_specs.append(pl.BlockSpec((bs, SPAN * N_KV, HEAD_DIM), lambda i, nb=nb: (i, nb - 1, 0)))
        else:
            shape = (n_seq, win // dil, dil * N_KV, HEAD_DIM)
            cache_specs.append(pl.BlockSpec((bs, SPAN, SUBLANES, HEAD_DIM), lambda i, nb=nb: (i, nb - 1, 0, 0)))
        views_k.append(cache_k.reshape(shape))
        views_v.append(cache_v.reshape(shape))
    mem_spec = pl.BlockSpec((bs, N_MEM * N_XHEADS, HEAD_DIM), lambda i: (i, 0, 0))
    mem_rows = lambda a: a.reshape(n_seq, N_MEM * N_XHEADS, HEAD_DIM)
    return pl.pallas_call(
        functools.partial(_attn_sample_kernel, bs=bs),
        grid=(n_seq // bs,),
        in_specs=[pl.BlockSpec(memory_space=pltpu.SMEM), row(W_QD), row(W_KV), row(W_KV), row(W_QX)]
        + cache_specs + cache_specs + [mem_spec, mem_spec] + wcol_specs,
        out_specs=[row(W_KV), row(W_QX), pl.BlockSpec((wrows, 3 * D_MODEL), lambda i: (i, 0))],
        out_shape=[jax.ShapeDtypeStruct((n_seq, 1, W_KV), F32), jax.ShapeDtypeStruct((n_seq, 1, W_QX), F32),
                   jax.ShapeDtypeStruct((w_in.shape[0], 3 * D_MODEL), BF16)],
        compiler_params=_params(("parallel",), 48),
        name="attn_sample",
    )(slopes, as_rows(q), as_rows(kn), as_rows(vn), as_rows(qx), *views_k, *views_v, mem_rows(mem_k),
      mem_rows(mem_v), *([w_in] * len(wcol_specs)))


def _merge_kernel(h_ref, c_ref, mix_ref, xm_ref, wg0_ref, wg1_ref, wg2_ref, bg_ref, wc_ref, wd_ref, wx_ref,
                  wo_ref, z_ref, merged_ref):
    s = pl.program_id(1)

    @pl.when(s < _D_TILES)
    def _():
        h = h_ref[...]
        acc = None
        for br, (wg_ref, y_ref, wy_ref) in enumerate(
                ((wg0_ref, c_ref, wc_ref), (wg1_ref, mix_ref, wd_ref), (wg2_ref, xm_ref, wx_ref))):
            gate = jax.nn.sigmoid(_dot(h, wg_ref[...]) + bg_ref[br])
            term = gate * _dot(y_ref[...].astype(BF16), wy_ref[...])
            acc = term if acc is None else acc + term
        merged_ref[s] = acc.astype(BF16)

    @pl.when(s >= _D_TILES)
    def _():
        z = None
        for j in range(_D_TILES):
            part = _dot(merged_ref[j], wo_ref[j * TN:(j + 1) * TN, :])
            z = part if z is None else z + part
        z_ref[...] = z


def _merge(h, c, mix, xm, w_gates, b_gate, w_conv_out_t, w_dil_o_t, w_x_o_t, w_out_t, *, tm):
    m = h.shape[0]
    n1, n2 = _D_TILES, D_MODEL // TN_OUT

    n_rows = m // tm

    def tile1(s):
        return jnp.where(s < n1, s, 0)

    def gate_spec(br):
        return pl.BlockSpec((D_MODEL, TN), lambda i, s, br=br: (0, br * _D_TILES + tile1(s)))

    def colw(kdim):
        return pl.BlockSpec((kdim, TN), lambda i, s: (0, tile1(s)))

    def rowblk(w):
        return pl.BlockSpec((tm, w), lambda i, s: (_row_ahead(i, s, n1, n_rows), 0))

    return pl.pallas_call(
        _merge_kernel,
        grid=(m // tm, n1 + n2),
        in_specs=[
            rowblk(D_MODEL), rowblk(C_CONV), rowblk(W_KV), rowblk(W_QX),
            gate_spec(0), gate_spec(1), gate_spec(2),
            pl.BlockSpec((3, 1, TN), lambda i, s: (0, 0, tile1(s))),
            colw(C_CONV), colw(W_KV), colw(W_QX),
            pl.BlockSpec((D_MODEL, TN_OUT), lambda i, s: (0, jnp.maximum(s - n1, 0))),
        ],
        out_specs=pl.BlockSpec((tm, TN_OUT), lambda i, s: (i, jnp.maximum(s - n1, 0))),
        out_shape=jax.ShapeDtypeStruct((m, D_MODEL), F32),
        scratch_shapes=[pltpu.VMEM((n1, tm, TN), BF16)],
        compiler_params=_params(("parallel", "arbitrary"), 58),
        name="merge",
    )(h, c, mix, xm, w_gates, w_gates, w_gates, b_gate.reshape(3, 1, D_MODEL), w_conv_out_t, w_dil_o_t, w_x_o_t,
      w_out_t)


def _ffn_kernel(x_ref, z_ref, gmix_ref, gpre_ref, wgu_ref, wd_ref, gpost_ref, out_ref, x1_ref, h_ref, *, n_steps):
    s = pl.program_id(1)

    @pl.when(s == 0)
    def _():
        x1 = x_ref[...] + _rms_rows(z_ref[...], gmix_ref[...])
        x1_ref[...] = x1
        h_ref[...] = _rms_rows(x1, gpre_ref[...]).astype(BF16)
        out_ref[...] = jnp.zeros_like(out_ref)

    h = h_ref[...]
    half = TF // 2
    part = None
    for c in range(2):
        gate_up = _dot(h, wgu_ref[:, c * TF:(c + 1) * TF])
        gate, up = gate_up[:, :half], gate_up[:, half:]
        act = (gate * jax.nn.sigmoid(gate) * up).astype(BF16)
        down = _dot(act, wd_ref[c * half:(c + 1) * half, :])
        part = down if part is None else part + down
    out_ref[...] += part

    @pl.when(s == n_steps - 1)
    def _():
        out_ref[...] = x1_ref[...] + _rms_rows(out_ref[...], gpost_ref[...])


def _ffn(x, z, g_mix, g_pre, w_gate_up, w_down, g_post, *, tm):
    m = x.shape[0]
    n_steps = D_FF // TF
    rowblk = pl.BlockSpec((tm, D_MODEL), lambda i, s: (i, 0))
    gain = pl.BlockSpec((1, D_MODEL), lambda i, s: (0, 0))
    return pl.pallas_call(
        functools.partial(_ffn_kernel, n_steps=n_steps),
        grid=(m // tm, n_steps),
        in_specs=[
            rowblk, rowblk, gain, gain,
            pl.BlockSpec((D_MODEL, 2 * TF), lambda i, s: (0, s)),
            pl.BlockSpec((TF, D_MODEL), lambda i, s: (s, 0)),
            gain,
        ],
        out_specs=rowblk,
        out_shape=jax.ShapeDtypeStruct((m, D_MODEL), F32),
        scratch_shapes=[pltpu.VMEM((tm, D_MODEL), F32), pltpu.VMEM((tm, D_MODEL), BF16)],
        compiler_params=_params(("parallel", "arbitrary"), 52),
        name="ffn",
    )(x, z, g_mix, g_pre, w_gate_up, w_down, g_post)


def kernel(x_prompt, x_sample, cache_win_k, cache_win_v, state_conv, cache_mem_k, cache_mem_v, mem_prompt,
           g_pre_mix, w_in, b_gate, conv_w, conv_b, conv_ln_g, conv_ln_b, w_conv_out, w_dil_o, g_mem, w_mem_kv,
           w_x_o, w_out, g_post_mix, g_pre_ffn, w_ffn_gate, w_ffn_up, w_ffn_down, g_post_ffn):
    batch, seq, _ = x_prompt.shape
    n_seq = x_sample.shape[0]
    depth = w_in.shape[0]
    assert depth == 1 and x_sample.shape[1] == 1 and seq % (DIL_GROUPS[-1][1] * SPAN) == 0
    assert cache_win_k.shape[2] == DIL_GROUPS[-1][0]

    idx = jnp.arange(1, N_QH + 1, dtype=F32)
    slopes = jnp.exp2(-8.0 * idx / N_QH).reshape(N_GROUPS, N_KV)

    xp, xs = x_prompt.reshape(batch * seq, D_MODEL), x_sample.reshape(n_seq, D_MODEL)
    l = 0
    cast = lambda w: w[l].astype(BF16)
    w_in_t = w_in[l][:, :_GATE_TILE0 * TN].astype(BF16)
    w_conv_out_t, w_dil_o_t, w_x_o_t = cast(w_conv_out), cast(w_dil_o), cast(w_x_o)
    w_out_t, w_mem_kv_t = cast(w_out), cast(w_mem_kv)
    row = lambda a: a[l].reshape(1, -1)
    g_pre, g_post, g_ffn_pre, g_ffn_post = row(g_pre_mix), row(g_post_mix), row(g_pre_ffn), row(g_post_ffn)
    cb, lng, lnb = row(conv_b), row(conv_ln_g), row(conv_ln_b)
    conv_w_slabs = conv_w[l].reshape(CONV_WIDTH, N_SLABS, LANES).transpose(1, 0, 2)

    def dense_tail(h, c, mix, xm, x, w_gates, ffn_weights, tm_merge, tm_ffn):
        z = _merge(h, c, mix, xm, w_gates, b_gate[l], w_conv_out_t, w_dil_o_t, w_x_o_t, w_out_t, tm=tm_merge)
        return _ffn(x, z, g_post, g_ffn_pre, *ffn_weights, g_ffn_post, tm=tm_ffn)

    mk, mv = _mem_kv(mem_prompt.reshape(batch * N_MEM, D_MODEL), row(g_mem), w_mem_kv_t, tm=512)
    h_p, u_p, q_p, k_p, v_p, qx_p = _in_proj(xp, g_pre, w_in_t, tm=1024, slabs=True)
    c_p, conv_p = _conv_prompt(u_p, conv_w_slabs, cb, lng, lnb, batch=batch, seq=seq)
    mix_p, xm_p, *ffn_weights = _attn_prompt(
        slopes, q_p.reshape(batch, seq, W_QD), k_p.reshape(batch, seq, W_KV), v_p.reshape(batch, seq, W_KV),
        qx_p.reshape(batch, seq, W_QX), mk.reshape(batch, N_MEM, W_QX), mv.reshape(batch, N_MEM, W_QX),
        w_ffn_gate[l], w_ffn_up[l], w_ffn_down[l], batch=batch, seq=seq)

    h_s, u_s, q_s, k_s, v_s, qx_s = _in_proj(xs, g_pre, w_in_t, tm=n_seq, slabs=False)
    c_s, conv_s = _conv_sample(state_conv[l], u_s, conv_w[l], cb, lng, lnb)
    mix_s, xm_s, w_gates = _attn_sample(slopes, q_s, k_s, v_s, qx_s, cache_win_k[l], cache_win_v[l],
                                        cache_mem_k[l], cache_mem_v[l], w_in[l])

    y_p = dense_tail(h_p, c_p, mix_p.reshape(batch * seq, W_KV), xm_p.reshape(batch * seq, W_QX), xp,
                     w_gates, ffn_weights, 1024, 512)
    y_s = dense_tail(h_s, c_s, mix_s.reshape(n_seq, W_KV), xm_s.reshape(n_seq, W_QX), xs,
                     w_gates, ffn_weights, n_seq, n_seq)

    win = min(DIL_GROUPS[-1][0], seq)
    kv_shape = (1, batch, seq, N_KV, HEAD_DIM)
    mem_shape = (1, batch, N_MEM, N_XHEADS, HEAD_DIM)
    new_shape = (1, n_seq, 1, N_KV, HEAD_DIM)
    return (y_p.reshape(batch, seq, D_MODEL), y_s.reshape(n_seq, 1, D_MODEL),
            k_p.reshape(kv_shape)[:, :, seq - win:], v_p.reshape(kv_shape)[:, :, seq - win:], conv_p,
            mk.reshape(mem_shape), mv.reshape(mem_shape),
            k_s.reshape(new_shape), v_s.reshape(new_shape), conv_s.reshape(1, n_seq, CONV_PREFIX, C_CONV))
```

```python
import functools

import jax
import jax.numpy as jnp
from jax import lax
from jax.experimental import pallas as pl
from jax.experimental.pallas import tpu as pltpu

F32 = jnp.float32
BF16 = jnp.bfloat16

D_MODEL = 2048
C_CONV = D_MODEL // 2
CONV_WIDTH = 31
CONV_PREFIX = CONV_WIDTH - 1
HEAD_DIM = 128
N_KV = 4
DIL_GROUPS = ((128, 1), (512, 4), (2048, 16))
N_GROUPS = len(DIL_GROUPS)
N_QH = N_GROUPS * N_KV
SPAN = 128
N_MEM = 256
N_XHEADS = 4
W_QD = N_QH * HEAD_DIM
W_KV = N_KV * HEAD_DIM
W_QX = N_XHEADS * HEAD_DIM
D_FF = 5632
EPS = 1e-6
NEG = -1e30
SCALE = HEAD_DIM ** -0.5
LANES = 128
SUBLANES = 8
N_SLABS = C_CONV // LANES
MIB = 1024 * 1024

TN = 512
_A_TILES = C_CONV // TN
_Q_TILES = W_QD // TN
_KV_TILES = W_KV // TN
_QX_TILES = W_QX // TN
_STEP_Q0 = _A_TILES
_STEP_K0 = _STEP_Q0 + _Q_TILES
_STEP_V0 = _STEP_K0 + _KV_TILES
_STEP_X0 = _STEP_V0 + _KV_TILES
_IN_STEPS = _STEP_X0 + _QX_TILES
_GATE_TILE0 = 2 * _A_TILES + _Q_TILES + 2 * _KV_TILES + _QX_TILES
_D_TILES = D_MODEL // TN
_ATTN_UNITS = 8
_GATE_CAST_COLS = 1024
TN_OUT = 512
TF = 512


def _params(semantics, vmem_mib):
    return pltpu.CompilerParams(dimension_semantics=semantics, vmem_limit_bytes=vmem_mib * MIB)


def _rms_rows(x, g):
    return x * lax.rsqrt(jnp.mean(x * x, axis=-1, keepdims=True) + EPS) * g


def _dot(a, b):
    return jnp.dot(a, b, preferred_element_type=F32)


def _row_ahead(i, s, free_from, n_rows):
    return jnp.minimum(i + (s >= free_from).astype(jnp.int32), n_rows - 1)


def _in_proj_kernel(x_ref, g_ref, w1_ref, w2_ref, h_ref, u_ref, q_ref, k_ref, v_ref, qx_ref, *, slabs):
    s = pl.program_id(1)

    @pl.when(s == 0)
    def _():
        h_ref[...] = _rms_rows(x_ref[...], g_ref[...]).astype(BF16)

    @pl.when(s < _STEP_Q0)
    def _():
        h = h_ref[...]
        u = _dot(h, w1_ref[...]) * jax.nn.sigmoid(_dot(h, w2_ref[...]))
        if slabs:
            for c in range(TN // LANES):
                u_ref[c] = u[:, c * LANES:(c + 1) * LANES]
        else:
            u_ref[...] = u

    @pl.when((s >= _STEP_Q0) & (s < _STEP_K0))
    def _():
        q_ref[...] = _dot(h_ref[...], w1_ref[...])

    @pl.when((s >= _STEP_K0) & (s < _STEP_V0))
    def _():
        k_ref[...] = _dot(h_ref[...], w1_ref[...])

    @pl.when((s >= _STEP_V0) & (s < _STEP_X0))
    def _():
        v_ref[...] = _dot(h_ref[...], w1_ref[...])

    @pl.when(s >= _STEP_X0)
    def _():
        qx_ref[...] = _dot(h_ref[...], w1_ref[...]).astype(qx_ref.dtype)


def _in_proj(x, g, w_in_t, *, tm, slabs):
    m = x.shape[0]
    last_a = _A_TILES - 1
    if slabs:
        u_shape = jax.ShapeDtypeStruct((N_SLABS, m, LANES), F32)
        u_spec = pl.BlockSpec((TN // LANES, tm, LANES), lambda i, s: (jnp.minimum(s, last_a), i, 0))
    else:
        u_shape = jax.ShapeDtypeStruct((m, C_CONV), F32)
        u_spec = pl.BlockSpec((tm, TN), lambda i, s: (i, jnp.minimum(s, last_a)))

    def out_tiles(step0, n_tiles):
        def index(i, s):
            moved = (s >= step0 + n_tiles) & (i < m // tm - 1)
            return (i + moved.astype(jnp.int32), jnp.where(moved, 0, jnp.clip(s - step0, 0, n_tiles - 1)))
        return pl.BlockSpec((tm, TN), index)

    return pl.pallas_call(
        functools.partial(_in_proj_kernel, slabs=slabs),
        grid=(m // tm, _IN_STEPS),
        in_specs=[
            pl.BlockSpec((tm, D_MODEL), lambda i, s: (_row_ahead(i, s, _A_TILES, m // tm), 0)),
            pl.BlockSpec((1, D_MODEL), lambda i, s: (0, 0)),
            pl.BlockSpec((D_MODEL, TN), lambda i, s: (0, jnp.where(s < _A_TILES, s, s + _A_TILES))),
            pl.BlockSpec((D_MODEL, TN), lambda i, s: (0, jnp.where(s < _A_TILES, s, 0) + _A_TILES)),
        ],
        out_specs=[
            pl.BlockSpec((tm, D_MODEL), lambda i, s: (i, 0)),
            u_spec,
            out_tiles(_STEP_Q0, _Q_TILES),
            out_tiles(_STEP_K0, _KV_TILES),
            out_tiles(_STEP_V0, _KV_TILES),
            out_tiles(_STEP_X0, _QX_TILES),
        ],
        out_shape=[
            jax.ShapeDtypeStruct((m, D_MODEL), BF16),
            u_shape,
            jax.ShapeDtypeStruct((m, W_QD), F32),
            jax.ShapeDtypeStruct((m, W_KV), F32),
            jax.ShapeDtypeStruct((m, W_KV), F32),
            jax.ShapeDtypeStruct((m, W_QX), BF16 if slabs else F32),
        ],
        compiler_params=_params(("parallel", "arbitrary"), 58),
        name="in_proj",
    )(x, g, w_in_t, w_in_t)


def _mem_kv_kernel(x_ref, g_ref, w_ref, mk_ref, mv_ref, h_ref):
    s = pl.program_id(1)

    @pl.when(s == 0)
    def _():
        h_ref[...] = _rms_rows(x_ref[...], g_ref[...]).astype(BF16)
        mk_ref[...] = _dot(h_ref[...], w_ref[...])

    @pl.when(s == 1)
    def _():
        mv_ref[...] = _dot(h_ref[...], w_ref[...])


def _mem_kv(mem, g, w_t, *, tm):
    m = mem.shape[0]
    return pl.pallas_call(
        _mem_kv_kernel,
        grid=(m // tm, 2),
        in_specs=[
            pl.BlockSpec((tm, D_MODEL), lambda i, s: (i, 0)),
            pl.BlockSpec((1, D_MODEL), lambda i, s: (0, 0)),
            pl.BlockSpec((D_MODEL, W_QX), lambda i, s: (0, s)),
        ],
        out_specs=[
            pl.BlockSpec((tm, W_QX), lambda i, s: (i, 0)),
            pl.BlockSpec((tm, W_QX), lambda i, s: (i, 0)),
        ],
        out_shape=[jax.ShapeDtypeStruct((m, W_QX), F32)] * 2,
        scratch_shapes=[pltpu.VMEM((tm, D_MODEL), BF16)],
        compiler_params=_params(("parallel", "arbitrary"), 32),
        name="mem_kv",
    )(mem, g, w_t)


def _ln_swish_slabs(y_slabs, cb_ref, lng_ref, lnb_ref, out_dtype):
    ys = [y + cb_ref[:, c * LANES:(c + 1) * LANES] for c, y in enumerate(y_slabs)]
    tot = ys[0]
    for y in ys[1:]:
        tot = tot + y
    mu = jnp.sum(tot, axis=-1, keepdims=True) * (1.0 / C_CONV)
    ds = [y - mu for y in ys]
    sq = ds[0] * ds[0]
    for d in ds[1:]:
        sq = sq + d * d
    rs = lax.rsqrt(jnp.sum(sq, axis=-1, keepdims=True) * (1.0 / C_CONV) + EPS)
    outs = []
    for c, d in enumerate(ds):
        z = d * rs * lng_ref[:, c * LANES:(c + 1) * LANES] + lnb_ref[:, c * LANES:(c + 1) * LANES]
        outs.append((z * jax.nn.sigmoid(z)).astype(out_dtype))
    return outs


def _conv_prompt_kernel(u_ref, w_ref, cb_ref, lng_ref, lnb_ref, k_ref, v_ref, *rest, seq, tt, n_cast):
    cast_in, (c_ref, st_ref, kout_ref, vout_ref) = rest[:n_cast], rest[n_cast:n_cast + 4]
    cast_out, (head_ref, y_ref) = rest[n_cast + 4:2 * n_cast + 4], rest[2 * n_cast + 4:]
    half = pl.program_id(1)
    n_tiles = seq // tt // 2
    pad = 32

    for src_ref, dst_ref in ((k_ref, kout_ref), (v_ref, vout_ref)):
        for h in range(N_KV):
            dst_ref[pl.ds(h, src_ref.shape[0], stride=N_KV), :] = src_ref[:, h * LANES:(h + 1) * LANES]
    for src_ref, dst_ref in zip(cast_in, cast_out):
        dst_ref[...] = src_ref[...].astype(BF16)

    def tile(src_ref, base, t0):
        def slab(c, carry):
            acc = jnp.zeros((tt, LANES), F32)
            for j in range(SUBLANES):
                taps = range(j, CONV_WIDTH, SUBLANES)
                win = src_ref[c, pl.ds(base + j, tt + taps[-1] - j), :]
                for k in taps:
                    acc = acc + win[k - j:k - j + tt] * w_ref[c, k:k + 1, :]
            y_ref[c] = acc
            return carry

        lax.fori_loop(0, N_SLABS, slab, 0)
        ys = [y_ref[c] for c in range(N_SLABS)]
        outs = _ln_swish_slabs(ys, cb_ref, lng_ref, lnb_ref, BF16)
        for c in range(N_SLABS):
            c_ref[pl.ds(t0, tt), c * LANES:(c + 1) * LANES] = outs[c]

    @pl.when(half == 0)
    def _():
        head_ref[:, 0:pad, :] = jnp.zeros((N_SLABS, pad, LANES), F32)
        head_ref[:, pad:pad + tt, :] = u_ref[:, 0:tt, :]
        tile(head_ref, pad - CONV_PREFIX, 0)

    @pl.when(half > 0)
    def _():
        tile(u_ref, pl.multiple_of(half * (n_tiles * tt), tt) - CONV_PREFIX, 0)

    def body(i, carry):
        t0 = pl.multiple_of((half * n_tiles + i) * tt, tt)
        tile(u_ref, t0 - CONV_PREFIX, pl.multiple_of(i * tt, tt))
        return carry

    lax.fori_loop(1, n_tiles, body, 0)
    for c in range(N_SLABS):
        st_ref[:, c * LANES:(c + 1) * LANES] = u_ref[c, seq - CONV_PREFIX:seq, :]


def _conv_prompt(u_slabs, w_slabs, conv_b, ln_g, ln_b, k, v, cast_weights, *, batch, seq):
    tt = 128
    halves = 2
    n_steps = batch * halves
    step = lambda b, j: b * halves + j
    const = lambda *idx: (lambda b, j: idx)

    def row_slice(rows, cols):
        return pl.BlockSpec((rows // n_steps, cols), lambda b, j: (step(b, j), 0))

    kv_in = row_slice(batch * seq, W_KV)
    kv_out = row_slice(batch * seq * N_KV, HEAD_DIM)
    w_specs = [row_slice(*w.shape) for w in cast_weights]
    return pl.pallas_call(
        functools.partial(_conv_prompt_kernel, seq=seq, tt=tt, n_cast=len(cast_weights)),
        grid=(batch, halves),
        in_specs=[
            pl.BlockSpec((N_SLABS, seq, LANES), lambda b, j: (0, b, 0)),
            pl.BlockSpec((N_SLABS, CONV_WIDTH, LANES), const(0, 0, 0)),
            pl.BlockSpec((1, C_CONV), const(0, 0)),
            pl.BlockSpec((1, C_CONV), const(0, 0)),
            pl.BlockSpec((1, C_CONV), const(0, 0)),
            kv_in, kv_in,
        ] + w_specs,
        out_specs=[
            pl.BlockSpec((seq // halves, C_CONV), lambda b, j: (step(b, j), 0)),
            pl.BlockSpec((None, None, CONV_PREFIX, C_CONV), lambda b, j: (0, b, 0, 0)),
            kv_out, kv_out,
        ] + w_specs,
        out_shape=[
            jax.ShapeDtypeStruct((batch * seq, C_CONV), BF16),
            jax.ShapeDtypeStruct((1, batch, CONV_PREFIX, C_CONV), F32),
            jax.ShapeDtypeStruct((batch * seq * N_KV, HEAD_DIM), F32),
            jax.ShapeDtypeStruct((batch * seq * N_KV, HEAD_DIM), F32),
        ] + [jax.ShapeDtypeStruct(w.shape, BF16) for w in cast_weights],
        scratch_shapes=[pltpu.VMEM((N_SLABS, 32 + tt, LANES), F32), pltpu.VMEM((N_SLABS, tt, LANES), F32)],
        compiler_params=_params(("parallel", "arbitrary"), 56),
        name="conv_prompt",
    )(u_slabs, w_slabs, conv_b, ln_g, ln_b, k, v, *cast_weights)


def _conv_sample_kernel(state_ref, u_ref, w_ref, cb_ref, lng_ref, lnb_ref, c_ref, st_ref, *, n_seq):
    w_hist = w_ref[0:CONV_PREFIX, :]
    w_new = w_ref[CONV_PREFIX:CONV_WIDTH, :]
    for n in range(n_seq):
        hist = state_ref[n]
        new = u_ref[n:n + 1, :]
        y = jnp.sum(hist * w_hist, axis=0, keepdims=True) + new * w_new
        ys = [y[:, c * LANES:(c + 1) * LANES] for c in range(N_SLABS)]
        outs = _ln_swish_slabs(ys, cb_ref, lng_ref, lnb_ref, F32)
        for c in range(N_SLABS):
            c_ref[n:n + 1, c * LANES:(c + 1) * LANES] = outs[c]
        st_ref[n, 0:CONV_PREFIX - 1, :] = state_ref[n, 1:CONV_PREFIX, :]
        st_ref[n, CONV_PREFIX - 1:CONV_PREFIX, :] = new


def _conv_sample(state, u, conv_w, conv_b, ln_g, ln_b):
    n_seq = state.shape[0]
    return pl.pallas_call(
        functools.partial(_conv_sample_kernel, n_seq=n_seq),
        out_shape=[
            jax.ShapeDtypeStruct((n_seq, C_CONV), F32),
            jax.ShapeDtypeStruct((n_seq, CONV_PREFIX, C_CONV), F32),
        ],
        compiler_params=pltpu.CompilerParams(vmem_limit_bytes=32 * MIB),
        name="conv_sample",
    )(state, u, conv_w, conv_b, ln_g, ln_b)


def _qk(q, k):
    return lax.dot_general(q, k, (((1,), (1,)), ((), ())), preferred_element_type=F32)


def _attn_prompt_kernel(slopes_ref, q0_ref, q1_ref, q2_ref, k_ref, v_ref, qx_ref, mk_ref, mv_ref,
                        wg_ref, wu_ref, wd_ref, mix_ref, xm_ref, wgu_out_ref, wd_out_ref,
                        bias_ref, o_ref, lse_ref, s_ref, p_ref, m_ref, ones_ref, *, seq):
    head = pl.program_id(1)
    ones_ref[...] = jnp.ones(ones_ref.shape, BF16)
    half = TF // 2
    for j in range(D_FF // half):
        cols = slice(j * half, (j + 1) * half)
        wgu_out_ref[:, 2 * j * half:(2 * j + 1) * half] = wg_ref[:, cols].astype(BF16)
        wgu_out_ref[:, (2 * j + 1) * half:(2 * j + 2) * half] = wu_ref[:, cols].astype(BF16)
    wd_out_ref[...] = wd_ref[...].astype(BF16)
    qi = lax.broadcasted_iota(jnp.int32, (SPAN, 2 * SPAN), 0)
    ci = lax.broadcasted_iota(jnp.int32, (SPAN, 2 * SPAN), 1)
    dist = qi + SPAN - ci
    valid = (dist >= 0) & (dist <= SPAN)
    for g, (_, dil) in enumerate(DIL_GROUPS):
        bias = -slopes_ref[g, head] * (dist * dil).astype(F32)
        bias_ref[g] = jnp.where(valid, bias, NEG)

    def rows(ref, start, size, dil):
        if dil == 1:
            return ref[pl.ds(start, size), :]
        return ref[pl.ds(start, size, stride=dil), :]

    def put(ref, g, start, dil, val):
        if dil == 1:
            ref[g, pl.ds(start, SPAN), :] = val
        else:
            ref[g, pl.ds(start, SPAN, stride=dil), :] = val

    def staged(items, n_keys_of, scores, values, emit):
        for u, item in enumerate(items):
            s_ref[u, :, 0:n_keys_of(item)] = scores(item)
        for u, item in enumerate(items):
            s = s_ref[u, :, 0:n_keys_of(item)]
            m = jnp.max(s, axis=-1, keepdims=True)
            p_ref[u, :, 0:n_keys_of(item)] = jnp.exp(s - m).astype(BF16)
            m_ref[u] = jnp.broadcast_to(m, (SPAN, LANES))
        for u, item in enumerate(items):
            p = p_ref[u, :, 0:n_keys_of(item)]
            l = _dot(p, ones_ref[0:n_keys_of(item), :])
            emit(item, _dot(p, values(item)) * (1.0 / l), m_ref[u] + jnp.log(l))

    def units(g, dil, q_ref, starts_keys):
        def scores(item):
            start, n_keys = item
            q = rows(q_ref, start, SPAN, dil).astype(BF16)
            k = rows(k_ref, start - (n_keys - SPAN) * dil, n_keys, dil).astype(BF16)
            return _qk(q, k) * SCALE + bias_ref[g, :, 2 * SPAN - n_keys:2 * SPAN]

        def values(item):
            start, n_keys = item
            return rows(v_ref, start - (n_keys - SPAN) * dil, n_keys, dil).astype(BF16)

        def emit(item, o, lse):
            put(o_ref, g, item[0], dil, o)
            put(lse_ref, g, item[0], dil, lse)

        staged(starts_keys, lambda item: item[1], scores, values, emit)

    q_refs = (q0_ref, q1_ref, q2_ref)
    for g, (_, dil) in enumerate(DIL_GROUPS):
        n_blocks = seq // dil // SPAN
        q_ref = q_refs[g]
        if n_blocks == 1:
            per_body = 8

            def first_blocks(j, carry, g=g, dil=dil, q_ref=q_ref):
                units(g, dil, q_ref, [(j * per_body + u, SPAN) for u in range(per_body)])
                return carry
            lax.fori_loop(0, dil // per_body, first_blocks, 0)
        elif dil == 1:
            per_body = _ATTN_UNITS

            def later_blocks(j, carry, g=g, dil=dil, q_ref=q_ref):
                units(g, dil, q_ref, [(pl.multiple_of((j * per_body + u) * SPAN, SPAN), 2 * SPAN)
                                      for u in range(per_body)])
                return carry
            units(g, dil, q_ref, [(0, SPAN)] + [(n * SPAN, 2 * SPAN) for n in range(1, per_body)])
            lax.fori_loop(1, n_blocks // per_body, later_blocks, 0)
        else:
            def residue(j, carry, g=g, dil=dil, q_ref=q_ref, n_blocks=n_blocks):
                work = []
                for r in (2 * j, 2 * j + 1):
                    work.append((r, SPAN))
                    work += [(r + n * (SPAN * dil), 2 * SPAN) for n in range(1, n_blocks)]
                units(g, dil, q_ref, work)
                return carry
            lax.fori_loop(0, dil // 2, residue, 0)

    tq = 256

    def mix(j, carry):
        t0 = pl.multiple_of(j * tq, tq)
        l0 = lse_ref[0, pl.ds(t0, tq), :]
        l1 = lse_ref[1, pl.ds(t0, tq), :]
        l2 = lse_ref[2, pl.ds(t0, tq), :]
        top = jnp.maximum(jnp.maximum(l0, l1), l2)
        w0 = jnp.exp(l0 - top)
        w1 = jnp.exp(l1 - top)
        w2 = jnp.exp(l2 - top)
        num = (w0 * o_ref[0, pl.ds(t0, tq), :] + w1 * o_ref[1, pl.ds(t0, tq), :]
               + w2 * o_ref[2, pl.ds(t0, tq), :])
        mix_ref[pl.ds(t0, tq), :] = (num / (w0 + w1 + w2)).astype(BF16)
        return carry

    lax.fori_loop(0, seq // tq, mix, 0, unroll=2)

    def memory(j, carry):
        def scores(t0):
            return _qk(qx_ref[pl.ds(t0, SPAN), :], mk_ref[...].astype(BF16)) * SCALE

        def emit(t0, o, lse):
            xm_ref[pl.ds(t0, SPAN), :] = o.astype(BF16)

        staged([pl.multiple_of((j * _ATTN_UNITS + u) * SPAN, SPAN) for u in range(_ATTN_UNITS)],
               lambda t0: N_MEM, scores, lambda t0: mv_ref[...].astype(BF16), emit)
        return carry

    lax.fori_loop(0, seq // SPAN // _ATTN_UNITS, memory, 0)


def _attn_prompt(slopes, q, k, v, qx, mk, mv, w_gate, w_up, w_down, *, batch, seq):
    def col(c):
        return pl.BlockSpec((None, seq, LANES), lambda b, h, c=c: (b, 0, c * N_KV + h))

    head_spec = pl.BlockSpec((None, seq, LANES), lambda b, h: (b, 0, h))
    mem_spec = pl.BlockSpec((None, N_MEM, LANES), lambda b, h: (b, 0, h))
    n_steps = batch * N_KV

    def row_slice(rows, cols):
        return pl.BlockSpec((rows // n_steps, cols), lambda b, h: (b * N_KV + h, 0))

    weights = (w_gate, w_up, w_down)
    gu_shape = (D_MODEL, 2 * D_FF)
    return pl.pallas_call(
        functools.partial(_attn_prompt_kernel, seq=seq),
        grid=(batch, N_KV),
        in_specs=[
            pl.BlockSpec(memory_space=pltpu.SMEM),
            col(0), col(1), col(2), head_spec, head_spec, head_spec, mem_spec, mem_spec,
        ] + [row_slice(*w.shape) for w in weights],
        out_specs=[head_spec, head_spec, row_slice(*gu_shape), row_slice(*w_down.shape)],
        out_shape=[jax.ShapeDtypeStruct((batch, seq, W_KV), BF16),
                   jax.ShapeDtypeStruct((batch, seq, W_QX), BF16),
                   jax.ShapeDtypeStruct(gu_shape, BF16), jax.ShapeDtypeStruct(w_down.shape, BF16)],
        scratch_shapes=[
            pltpu.VMEM((N_GROUPS, SPAN, 2 * SPAN), F32),
            pltpu.VMEM((N_GROUPS, seq, LANES), F32),
            pltpu.VMEM((N_GROUPS, seq, LANES), F32),
            pltpu.VMEM((_ATTN_UNITS, SPAN, 2 * SPAN), F32),
            pltpu.VMEM((_ATTN_UNITS, SPAN, 2 * SPAN), BF16),
            pltpu.VMEM((_ATTN_UNITS, SPAN, LANES), F32),
            pltpu.VMEM((2 * SPAN, LANES), BF16),
        ],
        compiler_params=_params(("parallel", "parallel"), 56),
        name="attn_prompt",
    )(slopes, q, q, q, k, v, qx, mk, mv, *weights)


def _attn_sample_kernel(slopes_ref, q_ref, kn_ref, vn_ref, qx_ref, ck0_ref, ck1_ref, ck2_ref,
                        cv0_ref, cv1_ref, cv2_ref, mk_ref, mv_ref, *rest, bs):
    wcol_refs, (mix_ref, xm_ref, wgate_ref) = rest[:-3], rest[-3:]

    def by_row(ref):
        return ref if len(ref.shape) == 3 else ref.reshape(ref.shape[0], ref.shape[1] * ref.shape[2], LANES)

    def head_rows(ref, n, h, n_keys):
        return ref[n, pl.ds(h, n_keys, stride=ref.shape[1] // n_keys), :]

    ck_refs = tuple(by_row(r) for r in (ck0_ref, ck1_ref, ck2_ref))
    cv_refs = tuple(by_row(r) for r in (cv0_ref, cv1_ref, cv2_ref))
    back =(SPAN - lax.broadcasted_iota(jnp.int32, (SPAN, 1), 0)).astype(F32)
    for c, wcol_ref in enumerate(wcol_refs):
        wgate_ref[:, c * _GATE_CAST_COLS:(c + 1) * _GATE_CAST_COLS] = wcol_ref[...].astype(BF16)

    def one(n, carry):
        for h in range(N_KV):
            hs = slice(h * LANES, (h + 1) * LANES)
            kn = kn_ref[n, :, hs]
            vn = vn_ref[n, :, hs]
            scores, s_new = [], []
            for g, (_, dil) in enumerate(DIL_GROUPS):
                qs = slice(g * W_KV + h * LANES, g * W_KV + (h + 1) * LANES)
                q = q_ref[n, :, qs]
                s = jnp.sum(head_rows(ck_refs[g], n, h, SPAN) * q, axis=-1, keepdims=True) * SCALE
                scores.append(s - slopes_ref[g, h] * (back * float(dil)))
                s_new.append(jnp.sum(kn * q, axis=-1, keepdims=True) * SCALE)
            top = s_new[0]
            for g in range(N_GROUPS):
                top = jnp.maximum(top, jnp.maximum(s_new[g], jnp.max(scores[g], axis=0, keepdims=True)))
            num = jnp.zeros((1, LANES), F32)
            den = jnp.zeros((1, 1), F32)
            for g in range(N_GROUPS):
                p = jnp.exp(scores[g] - top)
                p_new = jnp.exp(s_new[g] - top)
                num = num + jnp.sum(p * head_rows(cv_refs[g], n, h, SPAN), axis=0, keepdims=True) + p_new * vn
                den = den + jnp.sum(p, axis=0, keepdims=True) + p_new
            mix_ref[n, :, hs] = num / den
            qx = qx_ref[n, :, hs]
            sx = jnp.sum(head_rows(mk_ref, n, h, N_MEM) * qx, axis=-1, keepdims=True) * SCALE
            px = jnp.exp(sx - jnp.max(sx, axis=0, keepdims=True))
            ox = (jnp.sum(px * head_rows(mv_ref, n, h, N_MEM), axis=0, keepdims=True)
                  / jnp.sum(px, axis=0, keepdims=True))
            xm_ref[n, :, hs] = ox
        return carry

    lax.fori_loop(0, bs, one, 0)


def _attn_sample(slopes, q, kn, vn, qx, cache_k, cache_v, mem_k, mem_v, w_in):
    n_seq, win = cache_k.shape[0], cache_k.shape[1]
    bs = 2
    n_steps = n_seq // bs
    wrows = w_in.shape[0] // n_steps
    gate_col0 = _GATE_TILE0 * TN // _GATE_CAST_COLS
    wcol_specs = [pl.BlockSpec((wrows, _GATE_CAST_COLS), lambda i, c=c: (i, gate_col0 + c))
                  for c in range(3 * D_MODEL // _GATE_CAST_COLS)]
    row = lambda w: pl.BlockSpec((bs, 1, w), lambda i: (i, 0, 0))
    as_rows = lambda a: a.reshape(n_seq, 1, a.shape[-1])
    views_k, views_v, cache_specs = [], [], []
    for _, dil in DIL_GROUPS:
        nb = win // dil // SPAN
        if dil == 1:
            shape = (n_seq, win * N_KV, HEAD_DIM)
            cache_specs.append(pl.BlockSpec((bs, SPAN * N_KV, HEAD_DIM), lambda i, nb=nb: (i, nb - 1, 0)))
        else:
            shape = (n_seq, win // dil, dil * N_KV, HEAD_DIM)
            cache_specs.append(pl.BlockSpec((bs, SPAN, SUBLANES, HEAD_DIM), lambda i, nb=nb: (i, nb - 1, 0, 0)))
        views_k.append(cache_k.reshape(shape))
        views_v.append(cache_v.reshape(shape))
    mem_spec = pl.BlockSpec((bs, N_MEM * N_XHEADS, HEAD_DIM), lambda i: (i, 0, 0))
    mem_rows = lambda a: a.reshape(n_seq, N_MEM * N_XHEADS, HEAD_DIM)
    return pl.pallas_call(
        functools.partial(_attn_sample_kernel, bs=bs),
        grid=(n_seq // bs,),
        in_specs=[pl.BlockSpec(memory_space=pltpu.SMEM), row(W_QD), row(W_KV), row(W_KV), row(W_QX)]
        + cache_specs + cache_specs + [mem_spec, mem_spec] + wcol_specs,
        out_specs=[row(W_KV), row(W_QX), pl.BlockSpec((wrows, 3 * D_MODEL), lambda i: (i, 0))],
        out_shape=[jax.ShapeDtypeStruct((n_seq, 1, W_KV), F32), jax.ShapeDtypeStruct((n_seq, 1, W_QX), F32),
                   jax.ShapeDtypeStruct((w_in.shape[0], 3 * D_MODEL), BF16)],
        compiler_params=_params(("parallel",), 48),
        name="attn_sample",
    )(slopes, as_rows(q), as_rows(kn), as_rows(vn), as_rows(qx), *views_k, *views_v, mem_rows(mem_k),
      mem_rows(mem_v), *([w_in] * len(wcol_specs)))


def _merge_kernel(h_ref, c_ref, mix_ref, xm_ref, wg0_ref, wg1_ref, wg2_ref, bg_ref, wc_ref, wd_ref, wx_ref,
                  wo_ref, z_ref, merged_ref):
    s = pl.program_id(1)

    @pl.when(s < _D_TILES)
    def _():
        h = h_ref[...]
        acc = None
        for br, (wg_ref, y_ref, wy_ref) in enumerate(
                ((wg0_ref, c_ref, wc_ref), (wg1_ref, mix_ref, wd_ref), (wg2_ref, xm_ref, wx_ref))):
            gate = jax.nn.sigmoid(_dot(h, wg_ref[...]) + bg_ref[br])
            term = gate * _dot(y_ref[...].astype(BF16), wy_ref[...])
            acc = term if acc is None else acc + term
        merged_ref[s] = acc.astype(BF16)

    @pl.when(s >= _D_TILES)
    def _():
        z = None
        for j in range(_D_TILES):
            part = _dot(merged_ref[j], wo_ref[j * TN:(j + 1) * TN, :])
            z = part if z is None else z + part
        z_ref[...] = z


def _merge(h, c, mix, xm, w_gates, b_gate, w_conv_out_t, w_dil_o_t, w_x_o_t, w_out_t, *, tm):
    m = h.shape[0]
    n1, n2 = _D_TILES, D_MODEL // TN_OUT

    n_rows = m // tm

    def tile1(s):
        return jnp.where(s < n1, s, 0)

    def gate_spec(br):
        return pl.BlockSpec((D_MODEL, TN), lambda i, s, br=br: (0, br * _D_TILES + tile1(s)))

    def colw(kdim):
        return pl.BlockSpec((kdim, TN), lambda i, s: (0, tile1(s)))

    def rowblk(w):
        return pl.BlockSpec((tm, w), lambda i, s: (_row_ahead(i, s, n1, n_rows), 0))

    return pl.pallas_call(
        _merge_kernel,
        grid=(m // tm, n1 + n2),
        in_specs=[
            rowblk(D_MODEL), rowblk(C_CONV), rowblk(W_KV), rowblk(W_QX),
            gate_spec(0), gate_spec(1), gate_spec(2),
            pl.BlockSpec((3, 1, TN), lambda i, s: (0, 0, tile1(s))),
            colw(C_CONV), colw(W_KV), colw(W_QX),
            pl.BlockSpec((D_MODEL, TN_OUT), lambda i, s: (0, jnp.maximum(s - n1, 0))),
        ],
        out_specs=pl.BlockSpec((tm, TN_OUT), lambda i, s: (i, jnp.maximum(s - n1, 0))),
        out_shape=jax.ShapeDtypeStruct((m, D_MODEL), F32),
        scratch_shapes=[pltpu.VMEM((n1, tm, TN), BF16)],
        compiler_params=_params(("parallel", "arbitrary"), 58),
        name="merge",
    )(h, c, mix, xm, w_gates, w_gates, w_gates, b_gate.reshape(3, 1, D_MODEL), w_conv_out_t, w_dil_o_t, w_x_o_t,
      w_out_t)


def _ffn_kernel(x_ref, z_ref, gmix_ref, gpre_ref, wgu_ref, wd_ref, gpost_ref, out_ref, x1_ref, h_ref, *, n_steps):
    s = pl.program_id(1)

    @pl.when(s == 0)
    def _():
        x1 = x_ref[...] + _rms_rows(z_ref[...], gmix_ref[...])
        x1_ref[...] = x1
        h_ref[...] = _rms_rows(x1, gpre_ref[...]).astype(BF16)
        out_ref[...] = jnp.zeros_like(out_ref)

    h = h_ref[...]
    half = TF // 2
    part = None
    for c in range(2):
        gate_up = _dot(h, wgu_ref[:, c * TF:(c + 1) * TF])
        gate, up = gate_up[:, :half], gate_up[:, half:]
        act = (gate * jax.nn.sigmoid(gate) * up).astype(BF16)
        down = _dot(act, wd_ref[c * half:(c + 1) * half, :])
        part = down if part is None else part + down
    out_ref[...] += part

    @pl.when(s == n_steps - 1)
    def _():
        out_ref[...] = x1_ref[...] + _rms_rows(out_ref[...], gpost_ref[...])


def _ffn(x, z, g_mix, g_pre, w_gate_up, w_down, g_post, *, tm):
    m = x.shape[0]
    n_steps = D_FF // TF
    rowblk = pl.BlockSpec((tm, D_MODEL), lambda i, s: (i, 0))
    gain = pl.BlockSpec((1, D_MODEL), lambda i, s: (0, 0))
    return pl.pallas_call(
        functools.partial(_ffn_kernel, n_steps=n_steps),
        grid=(m // tm, n_steps),
        in_specs=[
            rowblk, rowblk, gain, gain,
            pl.BlockSpec((D_MODEL, 2 * TF), lambda i, s: (0, s)),
            pl.BlockSpec((TF, D_MODEL), lambda i, s: (s, 0)),
            gain,
        ],
        out_specs=rowblk,
        out_shape=jax.ShapeDtypeStruct((m, D_MODEL), F32),
        scratch_shapes=[pltpu.VMEM((tm, D_MODEL), F32), pltpu.VMEM((tm, D_MODEL), BF16)],
        compiler_params=_params(("parallel", "arbitrary"), 52),
        name="ffn",
    )(x, z, g_mix, g_pre, w_gate_up, w_down, g_post)


def kernel(x_prompt, x_sample, cache_win_k, cache_win_v, state_conv, cache_mem_k, cache_mem_v, mem_prompt,
           g_pre_mix, w_in, b_gate, conv_w, conv_b, conv_ln_g, conv_ln_b, w_conv_out, w_dil_o, g_mem, w_mem_kv,
           w_x_o, w_out, g_post_mix, g_pre_ffn, w_ffn_gate, w_ffn_up, w_ffn_down, g_post_ffn):
    batch, seq, _ = x_prompt.shape
    n_seq = x_sample.shape[0]
    depth = w_in.shape[0]
    assert depth == 1 and x_sample.shape[1] == 1 and seq % (DIL_GROUPS[-1][1] * SPAN) == 0
    assert cache_win_k.shape[2] == DIL_GROUPS[-1][0]

    idx = jnp.arange(1, N_QH + 1, dtype=F32)
    slopes = jnp.exp2(-8.0 * idx / N_QH).reshape(N_GROUPS, N_KV)

    xp, xs = x_prompt.reshape(batch * seq, D_MODEL), x_sample.reshape(n_seq, D_MODEL)
    l = 0
    w_in_t = w_in[l][:, :_GATE_TILE0 * TN].astype(BF16)
    w_mem_kv_t = w_mem_kv[l].astype(BF16)
    row = lambda a: a[l].reshape(1, -1)
    g_pre, g_post, g_ffn_pre, g_ffn_post = row(g_pre_mix), row(g_post_mix), row(g_pre_ffn), row(g_post_ffn)
    cb, lng, lnb = row(conv_b), row(conv_ln_g), row(conv_ln_b)
    conv_w_slabs = conv_w[l].reshape(CONV_WIDTH, N_SLABS, LANES).transpose(1, 0, 2)

    mk, mv = _mem_kv(mem_prompt.reshape(batch * N_MEM, D_MODEL), row(g_mem), w_mem_kv_t, tm=512)
    h_p, u_p, q_p, k_p, v_p, qx_p = _in_proj(xp, g_pre, w_in_t, tm=1024, slabs=True)
    c_p, conv_p, k_rows, v_rows, *branch_weights = _conv_prompt(
        u_p, conv_w_slabs, cb, lng, lnb, k_p, v_p, (w_conv_out[l], w_dil_o[l], w_x_o[l], w_out[l]),
        batch=batch, seq=seq)

    def dense_tail(h, c, mix, xm, x, w_gates, ffn_weights, tm_merge, tm_ffn):
        z = _merge(h, c, mix, xm, w_gates, b_gate[l], *branch_weights, tm=tm_merge)
        return _ffn(x, z, g_post, g_ffn_pre, *ffn_weights, g_ffn_post, tm=tm_ffn)

    mix_p, xm_p, *ffn_weights = _attn_prompt(
        slopes, q_p.reshape(batch, seq, W_QD), k_p.reshape(batch, seq, W_KV), v_p.reshape(batch, seq, W_KV),
        qx_p.reshape(batch, seq, W_QX), mk.reshape(batch, N_MEM, W_QX), mv.reshape(batch, N_MEM, W_QX),
        w_ffn_gate[l], w_ffn_up[l], w_ffn_down[l], batch=batch, seq=seq)

    h_s, u_s, q_s, k_s, v_s, qx_s = _in_proj(xs, g_pre, w_in_t, tm=n_seq, slabs=False)
    c_s, conv_s = _conv_sample(state_conv[l], u_s, conv_w[l], cb, lng, lnb)
    mix_s, xm_s, w_gates = _attn_sample(slopes, q_s, k_s, v_s, qx_s, cache_win_k[l], cache_win_v[l],
                                        cache_mem_k[l], cache_mem_v[l], w_in[l])

    y_p = dense_tail(h_p, c_p, mix_p.reshape(batch * seq, W_KV), xm_p.reshape(batch * seq, W_QX), xp,
                     w_gates, ffn_weights, 1024, 512)
    y_s = dense_tail(h_s, c_s, mix_s.reshape(n_seq, W_KV), xm_s.reshape(n_seq, W_QX), xs,
                     w_gates, ffn_weights, n_seq, n_seq)

    win = min(DIL_GROUPS[-1][0], seq)
    kv_shape = (1, batch, seq, N_KV, HEAD_DIM)
    mem_shape = (1, batch, N_MEM, N_XHEADS, HEAD_DIM)
    new_shape = (1, n_seq, 1, N_KV, HEAD_DIM)
    return (y_p.reshape(batch, seq, D_MODEL), y_s.reshape(n_seq, 1, D_MODEL),
            k_rows.reshape(kv_shape)[:, :, seq - win:], v_rows.reshape(kv_shape)[:, :, seq - win:], conv_p,
            mk.reshape(mem_shape), mv.reshape(mem_shape),
            k_s.reshape(new_shape), v_s.reshape(new_shape), conv_s.reshape(1, n_seq, CONV_PREFIX, C_CONV))
```

```python
import functools

import jax
import jax.numpy as jnp
from jax import lax
from jax.experimental import pallas as pl
from jax.experimental.pallas import tpu as pltpu

F32 = jnp.float32
BF16 = jnp.bfloat16

D_MODEL = 2048
C_CONV = D_MODEL // 2
CONV_WIDTH = 31
CONV_PREFIX = CONV_WIDTH - 1
HEAD_DIM = 128
N_KV = 4
DIL_GROUPS = ((128, 1), (512, 4), (2048, 16))
N_GROUPS = len(DIL_GROUPS)
N_QH = N_GROUPS * N_KV
SPAN = 128
N_MEM = 256
N_XHEADS = 4
W_QD = N_QH * HEAD_DIM
W_KV = N_KV * HEAD_DIM
W_QX = N_XHEADS * HEAD_DIM
D_FF = 5632
EPS = 1e-6
NEG = -1e30
SCALE = HEAD_DIM ** -0.5
LANES = 128
SUBLANES = 8
N_SLABS = C_CONV // LANES
MIB = 1024 * 1024

TN = 512
_A_TILES = C_CONV // TN
_Q_TILES = W_QD // TN
_KV_TILES = W_KV // TN
_QX_TILES = W_QX // TN
_STEP_Q0 = _A_TILES
_STEP_K0 = _STEP_Q0 + _Q_TILES
_STEP_V0 = _STEP_K0 + _KV_TILES
_STEP_X0 = _STEP_V0 + _KV_TILES
_IN_STEPS = _STEP_X0 + _QX_TILES
_GATE_TILE0 = 2 * _A_TILES + _Q_TILES + 2 * _KV_TILES + _QX_TILES
_D_TILES = D_MODEL // TN
_ATTN_UNITS = 8
_GATE_CAST_COLS = 1024
TN_OUT = 512
TF = 512


def _params(semantics, vmem_mib):
    return pltpu.CompilerParams(dimension_semantics=semantics, vmem_limit_bytes=vmem_mib * MIB)


def _rms_rows(x, g):
    return x * lax.rsqrt(jnp.mean(x * x, axis=-1, keepdims=True) + EPS) * g


def _dot(a, b):
    return jnp.dot(a, b, preferred_element_type=F32)


def _row_ahead(i, s, free_from, n_rows):
    return jnp.minimum(i + (s >= free_from).astype(jnp.int32), n_rows - 1)


def _in_proj_kernel(x_ref, xs_ref, g_ref, w1_ref, w2_ref, h_ref, u_ref, q_ref, k_ref, v_ref, qx_ref,
                    hs_ref, us_ref, qs_ref, ks_ref, vs_ref, qxs_ref):
    i, s = pl.program_id(0), pl.program_id(1)

    def project(on, x_ref, h_ref, store_u, q_ref, k_ref, v_ref, qx_ref):
        def norm():
            h_ref[...] = _rms_rows(x_ref[...], g_ref[...]).astype(BF16)

        def glu():
            h = h_ref[...]
            store_u(_dot(h, w1_ref[...]) * jax.nn.sigmoid(_dot(h, w2_ref[...])))

        def into(out_ref):
            def tile():
                out_ref[...] = _dot(h_ref[...], w1_ref[...]).astype(out_ref.dtype)
            return tile

        pl.when(on & (s == 0))(norm)
        pl.when(on & (s < _STEP_Q0))(glu)
        pl.when(on & (s >= _STEP_Q0) & (s < _STEP_K0))(into(q_ref))
        pl.when(on & (s >= _STEP_K0) & (s < _STEP_V0))(into(k_ref))
        pl.when(on & (s >= _STEP_V0) & (s < _STEP_X0))(into(v_ref))
        pl.when(on & (s >= _STEP_X0))(into(qx_ref))

    def store_slabs(u):
        for c in range(TN // LANES):
            u_ref[c] = u[:, c * LANES:(c + 1) * LANES]

    def store_rows(u):
        us_ref[...] = u

    project(True, x_ref, h_ref, store_slabs, q_ref, k_ref, v_ref, qx_ref)
    project(i == 0, xs_ref, hs_ref, store_rows, qs_ref, ks_ref, vs_ref, qxs_ref)


def _in_proj(x, xs, g, w_in_t, *, tm):
    m, ms = x.shape[0], xs.shape[0]
    last_a = _A_TILES - 1

    def sample_tiles(step0, n_tiles):
        return pl.BlockSpec((ms, TN), lambda i, s: (0, jnp.where(i == 0, jnp.clip(s - step0, 0, n_tiles - 1),
                                                                 n_tiles - 1)))

    def out_tiles(step0, n_tiles):
        def index(i, s):
            moved = (s >= step0 + n_tiles) & (i < m // tm - 1)
            return (i + moved.astype(jnp.int32), jnp.where(moved, 0, jnp.clip(s - step0, 0, n_tiles - 1)))
        return pl.BlockSpec((tm, TN), index)

    return pl.pallas_call(
        _in_proj_kernel,
        grid=(m // tm, _IN_STEPS),
        in_specs=[
            pl.BlockSpec((tm, D_MODEL), lambda i, s: (_row_ahead(i, s, _A_TILES, m // tm), 0)),
            pl.BlockSpec((ms, D_MODEL), lambda i, s: (0, 0)),
            pl.BlockSpec((1, D_MODEL), lambda i, s: (0, 0)),
            pl.BlockSpec((D_MODEL, TN), lambda i, s: (0, jnp.where(s < _A_TILES, s, s + _A_TILES))),
            pl.BlockSpec((D_MODEL, TN), lambda i, s: (0, jnp.where(s < _A_TILES, s, 0) + _A_TILES)),
        ],
        out_specs=[
            pl.BlockSpec((tm, D_MODEL), lambda i, s: (i, 0)),
            pl.BlockSpec((TN // LANES, tm, LANES), lambda i, s: (jnp.minimum(s, last_a), i, 0)),
            out_tiles(_STEP_Q0, _Q_TILES),
            out_tiles(_STEP_K0, _KV_TILES),
            out_tiles(_STEP_V0, _KV_TILES),
            out_tiles(_STEP_X0, _QX_TILES),
            pl.BlockSpec((ms, D_MODEL), lambda i, s: (0, 0)),
            sample_tiles(0, _A_TILES),
            sample_tiles(_STEP_Q0, _Q_TILES),
            sample_tiles(_STEP_K0, _KV_TILES),
            sample_tiles(_STEP_V0, _KV_TILES),
            sample_tiles(_STEP_X0, _QX_TILES),
        ],
        out_shape=[
            jax.ShapeDtypeStruct((m, D_MODEL), BF16),
            jax.ShapeDtypeStruct((N_SLABS, m, LANES), F32),
            jax.ShapeDtypeStruct((m, W_QD), F32),
            jax.ShapeDtypeStruct((m, W_KV), F32),
            jax.ShapeDtypeStruct((m, W_KV), F32),
            jax.ShapeDtypeStruct((m, W_QX), BF16),
            jax.ShapeDtypeStruct((ms, D_MODEL), BF16),
            jax.ShapeDtypeStruct((ms, C_CONV), F32),
            jax.ShapeDtypeStruct((ms, W_QD), F32),
            jax.ShapeDtypeStruct((ms, W_KV), F32),
            jax.ShapeDtypeStruct((ms, W_KV), F32),
            jax.ShapeDtypeStruct((ms, W_QX), F32),
        ],
        compiler_params=_params(("arbitrary", "arbitrary"), 60),
        name="in_proj",
    )(x, xs, g, w_in_t, w_in_t)


def _mem_kv_kernel(x_ref, g_ref, w_ref, mk_ref, mv_ref, h_ref):
    s = pl.program_id(1)

    @pl.when(s == 0)
    def _():
        h_ref[...] = _rms_rows(x_ref[...], g_ref[...]).astype(BF16)
        mk_ref[...] = _dot(h_ref[...], w_ref[...])

    @pl.when(s == 1)
    def _():
        mv_ref[...] = _dot(h_ref[...], w_ref[...])


def _mem_kv(mem, g, w_t, *, tm):
    m = mem.shape[0]
    return pl.pallas_call(
        _mem_kv_kernel,
        grid=(m // tm, 2),
        in_specs=[
            pl.BlockSpec((tm, D_MODEL), lambda i, s: (i, 0)),
            pl.BlockSpec((1, D_MODEL), lambda i, s: (0, 0)),
            pl.BlockSpec((D_MODEL, W_QX), lambda i, s: (0, s)),
        ],
        out_specs=[
            pl.BlockSpec((tm, W_QX), lambda i, s: (i, 0)),
            pl.BlockSpec((tm, W_QX), lambda i, s: (i, 0)),
        ],
        out_shape=[jax.ShapeDtypeStruct((m, W_QX), F32)] * 2,
        scratch_shapes=[pltpu.VMEM((tm, D_MODEL), BF16)],
        compiler_params=_params(("parallel", "arbitrary"), 32),
        name="mem_kv",
    )(mem, g, w_t)


def _ln_swish_slabs(y_slabs, cb_ref, lng_ref, lnb_ref, out_dtype):
    ys = [y + cb_ref[:, c * LANES:(c + 1) * LANES] for c, y in enumerate(y_slabs)]
    tot = ys[0]
    for y in ys[1:]:
        tot = tot + y
    mu = jnp.sum(tot, axis=-1, keepdims=True) * (1.0 / C_CONV)
    ds = [y - mu for y in ys]
    sq = ds[0] * ds[0]
    for d in ds[1:]:
        sq = sq + d * d
    rs = lax.rsqrt(jnp.sum(sq, axis=-1, keepdims=True) * (1.0 / C_CONV) + EPS)
    outs = []
    for c, d in enumerate(ds):
        z = d * rs * lng_ref[:, c * LANES:(c + 1) * LANES] + lnb_ref[:, c * LANES:(c + 1) * LANES]
        outs.append((z * jax.nn.sigmoid(z)).astype(out_dtype))
    return outs


def _conv_prompt_kernel(u_ref, w_ref, cb_ref, lng_ref, lnb_ref, k_ref, v_ref, *rest, seq, tt, n_cast):
    cast_in, (c_ref, st_ref, kout_ref, vout_ref) = rest[:n_cast], rest[n_cast:n_cast + 4]
    cast_out, (head_ref, y_ref) = rest[n_cast + 4:2 * n_cast + 4], rest[2 * n_cast + 4:]
    half = pl.program_id(1)
    n_tiles = seq // tt // 2
    pad = 32

    for src_ref, dst_ref in ((k_ref, kout_ref), (v_ref, vout_ref)):
        for h in range(N_KV):
            dst_ref[pl.ds(h, src_ref.shape[0], stride=N_KV), :] = src_ref[:, h * LANES:(h + 1) * LANES]
    for src_ref, dst_ref in zip(cast_in, cast_out):
        dst_ref[...] = src_ref[...].astype(BF16)

    def tile(src_ref, base, t0):
        def slab(c, carry):
            acc = jnp.zeros((tt, LANES), F32)
            for j in range(SUBLANES):
                taps = range(j, CONV_WIDTH, SUBLANES)
                win = src_ref[c, pl.ds(base + j, tt + taps[-1] - j), :]
                for k in taps:
                    acc = acc + win[k - j:k - j + tt] * w_ref[c, k:k + 1, :]
            y_ref[c] = acc
            return carry

        lax.fori_loop(0, N_SLABS, slab, 0)
        ys = [y_ref[c] for c in range(N_SLABS)]
        outs = _ln_swish_slabs(ys, cb_ref, lng_ref, lnb_ref, BF16)
        for c in range(N_SLABS):
            c_ref[pl.ds(t0, tt), c * LANES:(c + 1) * LANES] = outs[c]

    @pl.when(half == 0)
    def _():
        head_ref[:, 0:pad, :] = jnp.zeros((N_SLABS, pad, LANES), F32)
        head_ref[:, pad:pad + tt, :] = u_ref[:, 0:tt, :]
        tile(head_ref, pad - CONV_PREFIX, 0)

    @pl.when(half > 0)
    def _():
        tile(u_ref, pl.multiple_of(half * (n_tiles * tt), tt) - CONV_PREFIX, 0)

    def body(i, carry):
        t0 = pl.multiple_of((half * n_tiles + i) * tt, tt)
        tile(u_ref, t0 - CONV_PREFIX, pl.multiple_of(i * tt, tt))
        return carry

    lax.fori_loop(1, n_tiles, body, 0)
    for c in range(N_SLABS):
        st_ref[:, c * LANES:(c + 1) * LANES] = u_ref[c, seq - CONV_PREFIX:seq, :]


def _conv_prompt(u_slabs, w_slabs, conv_b, ln_g, ln_b, k, v, cast_weights, *, batch, seq):
    tt = 128
    halves = 2
    n_steps = batch * halves
    step = lambda b, j: b * halves + j
    const = lambda *idx: (lambda b, j: idx)

    def row_slice(rows, cols):
        return pl.BlockSpec((rows // n_steps, cols), lambda b, j: (step(b, j), 0))

    kv_in = row_slice(batch * seq, W_KV)
    kv_out = row_slice(batch * seq * N_KV, HEAD_DIM)
    w_specs = [row_slice(*w.shape) for w in cast_weights]
    return pl.pallas_call(
        functools.partial(_conv_prompt_kernel, seq=seq, tt=tt, n_cast=len(cast_weights)),
        grid=(batch, halves),
        in_specs=[
            pl.BlockSpec((N_SLABS, seq, LANES), lambda b, j: (0, b, 0)),
            pl.BlockSpec((N_SLABS, CONV_WIDTH, LANES), const(0, 0, 0)),
            pl.BlockSpec((1, C_CONV), const(0, 0)),
            pl.BlockSpec((1, C_CONV), const(0, 0)),
            pl.BlockSpec((1, C_CONV), const(0, 0)),
            kv_in, kv_in,
        ] + w_specs,
        out_specs=[
            pl.BlockSpec((seq // halves, C_CONV), lambda b, j: (step(b, j), 0)),
            pl.BlockSpec((None, None, CONV_PREFIX, C_CONV), lambda b, j: (0, b, 0, 0)),
            kv_out, kv_out,
        ] + w_specs,
        out_shape=[
            jax.ShapeDtypeStruct((batch * seq, C_CONV), BF16),
            jax.ShapeDtypeStruct((1, batch, CONV_PREFIX, C_CONV), F32),
            jax.ShapeDtypeStruct((batch * seq * N_KV, HEAD_DIM), F32),
            jax.ShapeDtypeStruct((batch * seq * N_KV, HEAD_DIM), F32),
        ] + [jax.ShapeDtypeStruct(w.shape, BF16) for w in cast_weights],
        scratch_shapes=[pltpu.VMEM((N_SLABS, 32 + tt, LANES), F32), pltpu.VMEM((N_SLABS, tt, LANES), F32)],
        compiler_params=_params(("parallel", "arbitrary"), 56),
        name="conv_prompt",
    )(u_slabs, w_slabs, conv_b, ln_g, ln_b, k, v, *cast_weights)


def _conv_sample_kernel(state_ref, u_ref, w_ref, cb_ref, lng_ref, lnb_ref, c_ref, st_ref, *, n_seq):
    w_hist = w_ref[0:CONV_PREFIX, :]
    w_new = w_ref[CONV_PREFIX:CONV_WIDTH, :]
    for n in range(n_seq):
        hist = state_ref[n]
        new = u_ref[n:n + 1, :]
        y = jnp.sum(hist * w_hist, axis=0, keepdims=True) + new * w_new
        ys = [y[:, c * LANES:(c + 1) * LANES] for c in range(N_SLABS)]
        outs = _ln_swish_slabs(ys, cb_ref, lng_ref, lnb_ref, F32)
        for c in range(N_SLABS):
            c_ref[n:n + 1, c * LANES:(c + 1) * LANES] = outs[c]
        st_ref[n, 0:CONV_PREFIX - 1, :] = state_ref[n, 1:CONV_PREFIX, :]
        st_ref[n, CONV_PREFIX - 1:CONV_PREFIX, :] = new


def _conv_sample(state, u, conv_w, conv_b, ln_g, ln_b):
    n_seq = state.shape[0]
    return pl.pallas_call(
        functools.partial(_conv_sample_kernel, n_seq=n_seq),
        out_shape=[
            jax.ShapeDtypeStruct((n_seq, C_CONV), F32),
            jax.ShapeDtypeStruct((n_seq, CONV_PREFIX, C_CONV), F32),
        ],
        compiler_params=pltpu.CompilerParams(vmem_limit_bytes=32 * MIB),
        name="conv_sample",
    )(state, u, conv_w, conv_b, ln_g, ln_b)


def _qk(q, k):
    return lax.dot_general(q, k, (((1,), (1,)), ((), ())), preferred_element_type=F32)


def _attn_prompt_kernel(slopes_ref, q0_ref, q1_ref, q2_ref, k_ref, v_ref, qx_ref, mk_ref, mv_ref,
                        wg_ref, wu_ref, wd_ref, mix_ref, xm_ref, wgu_out_ref, wd_out_ref,
                        bias_ref, o_ref, lse_ref, s_ref, p_ref, m_ref, ones_ref, *, seq):
    head = pl.program_id(1)
    ones_ref[...] = jnp.ones(ones_ref.shape, BF16)
    half = TF // 2
    for j in range(D_FF // half):
        cols = slice(j * half, (j + 1) * half)
        wgu_out_ref[:, 2 * j * half:(2 * j + 1) * half] = wg_ref[:, cols].astype(BF16)
        wgu_out_ref[:, (2 * j + 1) * half:(2 * j + 2) * half] = wu_ref[:, cols].astype(BF16)
    wd_out_ref[...] = wd_ref[...].astype(BF16)
    qi = lax.broadcasted_iota(jnp.int32, (SPAN, 2 * SPAN), 0)
    ci = lax.broadcasted_iota(jnp.int32, (SPAN, 2 * SPAN), 1)
    dist = qi + SPAN - ci
    valid = (dist >= 0) & (dist <= SPAN)
    for g, (_, dil) in enumerate(DIL_GROUPS):
        bias = -slopes_ref[g, head] * (dist * dil).astype(F32)
        bias_ref[g] = jnp.where(valid, bias, NEG)

    def rows(ref, start, size, dil):
        if dil == 1:
            return ref[pl.ds(start, size), :]
        return ref[pl.ds(start, size, stride=dil), :]

    def put(ref, g, start, dil, val):
        if dil == 1:
            ref[g, pl.ds(start, SPAN), :] = val
        else:
            ref[g, pl.ds(start, SPAN, stride=dil), :] = val

    def staged(items, n_keys_of, scores, values, emit):
        for u, item in enumerate(items):
            s_ref[u, :, 0:n_keys_of(item)] = scores(item)
        for u, item in enumerate(items):
            s = s_ref[u, :, 0:n_keys_of(item)]
            m = jnp.max(s, axis=-1, keepdims=True)
            p_ref[u, :, 0:n_keys_of(item)] = jnp.exp(s - m).astype(BF16)
            m_ref[u] = jnp.broadcast_to(m, (SPAN, LANES))
        for u, item in enumerate(items):
            p = p_ref[u, :, 0:n_keys_of(item)]
            l = _dot(p, ones_ref[0:n_keys_of(item), :])
            emit(item, _dot(p, values(item)) * (1.0 / l), m_ref[u] + jnp.log(l))

    def units(g, dil, q_ref, starts_keys):
        def scores(item):
            start, n_keys = item
            q = rows(q_ref, start, SPAN, dil).astype(BF16)
            k = rows(k_ref, start - (n_keys - SPAN) * dil, n_keys, dil).astype(BF16)
            return _qk(q, k) * SCALE + bias_ref[g, :, 2 * SPAN - n_keys:2 * SPAN]

        def values(item):
            start, n_keys = item
            return rows(v_ref, start - (n_keys - SPAN) * dil, n_keys, dil).astype(BF16)

        def emit(item, o, lse):
            put(o_ref, g, item[0], dil, o)
            put(lse_ref, g, item[0], dil, lse)

        staged(starts_keys, lambda item: item[1], scores, values, emit)

    q_refs = (q0_ref, q1_ref, q2_ref)
    for g, (_, dil) in enumerate(DIL_GROUPS):
        n_blocks = seq // dil // SPAN
        q_ref = q_refs[g]
        if n_blocks == 1:
            per_body = 8

            def first_blocks(j, carry, g=g, dil=dil, q_ref=q_ref):
                units(g, dil, q_ref, [(j * per_body + u, SPAN) for u in range(per_body)])
                return carry
            lax.fori_loop(0, dil // per_body, first_blocks, 0)
        elif dil == 1:
            per_body = _ATTN_UNITS

            def later_blocks(j, carry, g=g, dil=dil, q_ref=q_ref):
                units(g, dil, q_ref, [(pl.multiple_of((j * per_body + u) * SPAN, SPAN), 2 * SPAN)
                                      for u in range(per_body)])
                return carry
            units(g, dil, q_ref, [(0, SPAN)] + [(n * SPAN, 2 * SPAN) for n in range(1, per_body)])
            lax.fori_loop(1, n_blocks // per_body, later_blocks, 0)
        else:
            def residue(j, carry, g=g, dil=dil, q_ref=q_ref, n_blocks=n_blocks):
                work = []
                for r in (2 * j, 2 * j + 1):
                    work.append((r, SPAN))
                    work += [(r + n * (SPAN * dil), 2 * SPAN) for n in range(1, n_blocks)]
                units(g, dil, q_ref, work)
                return carry
            lax.fori_loop(0, dil // 2, residue, 0)

    tq = 256

    def mix(j, carry):
        t0 = pl.multiple_of(j * tq, tq)
        l0 = lse_ref[0, pl.ds(t0, tq), :]
        l1 = lse_ref[1, pl.ds(t0, tq), :]
        l2 = lse_ref[2, pl.ds(t0, tq), :]
        top = jnp.maximum(jnp.maximum(l0, l1), l2)
        w0 = jnp.exp(l0 - top)
        w1 = jnp.exp(l1 - top)
        w2 = jnp.exp(l2 - top)
        num = (w0 * o_ref[0, pl.ds(t0, tq), :] + w1 * o_ref[1, pl.ds(t0, tq), :]
               + w2 * o_ref[2, pl.ds(t0, tq), :])
        mix_ref[pl.ds(t0, tq), :] = (num / (w0 + w1 + w2)).astype(BF16)
        return carry

    lax.fori_loop(0, seq // tq, mix, 0, unroll=2)

    def memory(j, carry):
        def scores(t0):
            return _qk(qx_ref[pl.ds(t0, SPAN), :], mk_ref[...].astype(BF16)) * SCALE

        def emit(t0, o, lse):
            xm_ref[pl.ds(t0, SPAN), :] = o.astype(BF16)

        staged([pl.multiple_of((j * _ATTN_UNITS + u) * SPAN, SPAN) for u in range(_ATTN_UNITS)],
               lambda t0: N_MEM, scores, lambda t0: mv_ref[...].astype(BF16), emit)
        return carry

    lax.fori_loop(0, seq // SPAN // _ATTN_UNITS, memory, 0)


def _attn_prompt(slopes, q, k, v, qx, mk, mv, w_gate, w_up, w_down, *, batch, seq):
    def col(c):
        return pl.BlockSpec((None, seq, LANES), lambda b, h, c=c: (b, 0, c * N_KV + h))

    head_spec = pl.BlockSpec((None, seq, LANES), lambda b, h: (b, 0, h))
    mem_spec = pl.BlockSpec((None, N_MEM, LANES), lambda b, h: (b, 0, h))
    n_steps = batch * N_KV

    def row_slice(rows, cols):
        return pl.BlockSpec((rows // n_steps, cols), lambda b, h: (b * N_KV + h, 0))

    weights = (w_gate, w_up, w_down)
    gu_shape = (D_MODEL, 2 * D_FF)
    return pl.pallas_call(
        functools.partial(_attn_prompt_kernel, seq=seq),
        grid=(batch, N_KV),
        in_specs=[
            pl.BlockSpec(memory_space=pltpu.SMEM),
            col(0), col(1), col(2), head_spec, head_spec, head_spec, mem_spec, mem_spec,
        ] + [row_slice(*w.shape) for w in weights],
        out_specs=[head_spec, head_spec, row_slice(*gu_shape), row_slice(*w_down.shape)],
        out_shape=[jax.ShapeDtypeStruct((batch, seq, W_KV), BF16),
                   jax.ShapeDtypeStruct((batch, seq, W_QX), BF16),
                   jax.ShapeDtypeStruct(gu_shape, BF16), jax.ShapeDtypeStruct(w_down.shape, BF16)],
        scratch_shapes=[
            pltpu.VMEM((N_GROUPS, SPAN, 2 * SPAN), F32),
            pltpu.VMEM((N_GROUPS, seq, LANES), F32),
            pltpu.VMEM((N_GROUPS, seq, LANES), F32),
            pltpu.VMEM((_ATTN_UNITS, SPAN, 2 * SPAN), F32),
            pltpu.VMEM((_ATTN_UNITS, SPAN, 2 * SPAN), BF16),
            pltpu.VMEM((_ATTN_UNITS, SPAN, LANES), F32),
            pltpu.VMEM((2 * SPAN, LANES), BF16),
        ],
        compiler_params=_params(("parallel", "parallel"), 56),
        name="attn_prompt",
    )(slopes, q, q, q, k, v, qx, mk, mv, *weights)


def _attn_sample_kernel(slopes_ref, q_ref, kn_ref, vn_ref, qx_ref, ck0_ref, ck1_ref, ck2_ref,
                        cv0_ref, cv1_ref, cv2_ref, mk_ref, mv_ref, *rest, bs):
    wcol_refs, (mix_ref, xm_ref, wgate_ref) = rest[:-3], rest[-3:]

    def by_row(ref):
        return ref if len(ref.shape) == 3 else ref.reshape(ref.shape[0], ref.shape[1] * ref.shape[2], LANES)

    def head_rows(ref, n, h, n_keys):
        return ref[n, pl.ds(h, n_keys, stride=ref.shape[1] // n_keys), :]

    ck_refs = tuple(by_row(r) for r in (ck0_ref, ck1_ref, ck2_ref))
    cv_refs = tuple(by_row(r) for r in (cv0_ref, cv1_ref, cv2_ref))
    back =(SPAN - lax.broadcasted_iota(jnp.int32, (SPAN, 1), 0)).astype(F32)
    for c, wcol_ref in enumerate(wcol_refs):
        wgate_ref[:, c * _GATE_CAST_COLS:(c + 1) * _GATE_CAST_COLS] = wcol_ref[...].astype(BF16)

    def one(n, carry):
        for h in range(N_KV):
            hs = slice(h * LANES, (h + 1) * LANES)
            kn = kn_ref[n, :, hs]
            vn = vn_ref[n, :, hs]
            scores, s_new = [], []
            for g, (_, dil) in enumerate(DIL_GROUPS):
                qs = slice(g * W_KV + h * LANES, g * W_KV + (h + 1) * LANES)
                q = q_ref[n, :, qs]
                s = jnp.sum(head_rows(ck_refs[g], n, h, SPAN) * q, axis=-1, keepdims=True) * SCALE
                scores.append(s - slopes_ref[g, h] * (back * float(dil)))
                s_new.append(jnp.sum(kn * q, axis=-1, keepdims=True) * SCALE)
            top = s_new[0]
            for g in range(N_GROUPS):
                top = jnp.maximum(top, jnp.maximum(s_new[g], jnp.max(scores[g], axis=0, keepdims=True)))
            num = jnp.zeros((1, LANES), F32)
            den = jnp.zeros((1, 1), F32)
            for g in range(N_GROUPS):
                p = jnp.exp(scores[g] - top)
                p_new = jnp.exp(s_new[g] - top)
                num = num + jnp.sum(p * head_rows(cv_refs[g], n, h, SPAN), axis=0, keepdims=True) + p_new * vn
                den = den + jnp.sum(p, axis=0, keepdims=True) + p_new
            mix_ref[n, :, hs] = num / den
            qx = qx_ref[n, :, hs]
            sx = jnp.sum(head_rows(mk_ref, n, h, N_MEM) * qx, axis=-1, keepdims=True) * SCALE
            px = jnp.exp(sx - jnp.max(sx, axis=0, keepdims=True))
            ox = (jnp.sum(px * head_rows(mv_ref, n, h, N_MEM), axis=0, keepdims=True)
                  / jnp.sum(px, axis=0, keepdims=True))
            xm_ref[n, :, hs] = ox
        return carry

    lax.fori_loop(0, bs, one, 0)


def _attn_sample(slopes, q, kn, vn, qx, cache_k, cache_v, mem_k, mem_v, w_in):
    n_seq, win = cache_k.shape[0], cache_k.shape[1]
    bs = 2
    n_steps = n_seq // bs
    wrows = w_in.shape[0] // n_steps
    gate_col0 = _GATE_TILE0 * TN // _GATE_CAST_COLS
    wcol_specs = [pl.BlockSpec((wrows, _GATE_CAST_COLS), lambda i, c=c: (i, gate_col0 + c))
                  for c in range(3 * D_MODEL // _GATE_CAST_COLS)]
    row = lambda w: pl.BlockSpec((bs, 1, w), lambda i: (i, 0, 0))
    as_rows = lambda a: a.reshape(n_seq, 1, a.shape[-1])
    views_k, views_v, cache_specs = [], [], []
    for _, dil in DIL_GROUPS:
        nb = win // dil // SPAN
        if dil == 1:
            shape = (n_seq, win * N_KV, HEAD_DIM)
            cache_specs.append(pl.BlockSpec((bs, SPAN * N_KV, HEAD_DIM), lambda i, nb=nb: (i, nb - 1, 0)))
        else:
            shape = (n_seq, win // dil, dil * N_KV, HEAD_DIM)
            cache_specs.append(pl.BlockSpec((bs, SPAN, SUBLANES, HEAD_DIM), lambda i, nb=nb: (i, nb - 1, 0, 0)))
        views_k.append(cache_k.reshape(shape))
        views_v.append(cache_v.reshape(shape))
    mem_spec = pl.BlockSpec((bs, N_MEM * N_XHEADS, HEAD_DIM), lambda i: (i, 0, 0))
    mem_rows = lambda a: a.reshape(n_seq, N_MEM * N_XHEADS, HEAD_DIM)
    return pl.pallas_call(
        functools.partial(_attn_sample_kernel, bs=bs),
        grid=(n_seq // bs,),
        in_specs=[pl.BlockSpec(memory_space=pltpu.SMEM), row(W_QD), row(W_KV), row(W_KV), row(W_QX)]
        + cache_specs + cache_specs + [mem_spec, mem_spec] + wcol_specs,
        out_specs=[row(W_KV), row(W_QX), pl.BlockSpec((wrows, 3 * D_MODEL), lambda i: (i, 0))],
        out_shape=[jax.ShapeDtypeStruct((n_seq, 1, W_KV), F32), jax.ShapeDtypeStruct((n_seq, 1, W_QX), F32),
                   jax.ShapeDtypeStruct((w_in.shape[0], 3 * D_MODEL), BF16)],
        compiler_params=_params(("parallel",), 48),
        name="attn_sample",
    )(slopes, as_rows(q), as_rows(kn), as_rows(vn), as_rows(qx), *views_k, *views_v, mem_rows(mem_k),
      mem_rows(mem_v), *([w_in] * len(wcol_specs)))


def _merge_kernel(h_ref, c_ref, mix_ref, xm_ref, hs_ref, cs_ref, mixs_ref, xms_ref, wg0_ref, wg1_ref, wg2_ref,
                  bg_ref, wc_ref, wd_ref, wx_ref, wo_ref, z_ref, zs_ref, merged_ref, mergeds_ref):
    i, s = pl.program_id(0), pl.program_id(1)

    def gated_sum(h_ref, branches, merged_ref):
        h = h_ref[...]
        acc = None
        for br, (wg_ref, y_ref, wy_ref) in enumerate(zip((wg0_ref, wg1_ref, wg2_ref), branches,
                                                         (wc_ref, wd_ref, wx_ref))):
            gate = jax.nn.sigmoid(_dot(h, wg_ref[...]) + bg_ref[br])
            term = gate * _dot(y_ref[...].astype(BF16), wy_ref[...])
            acc = term if acc is None else acc + term
        merged_ref[s] = acc.astype(BF16)

    def project(merged_ref, z_ref):
        z = None
        for j in range(_D_TILES):
            part = _dot(merged_ref[j], wo_ref[j * TN:(j + 1) * TN, :])
            z = part if z is None else z + part
        z_ref[...] = z

    first, second = s < _D_TILES, s >= _D_TILES
    pl.when(first)(lambda: gated_sum(h_ref, (c_ref, mix_ref, xm_ref), merged_ref))
    pl.when(second)(lambda: project(merged_ref, z_ref))
    pl.when(first & (i == 0))(lambda: gated_sum(hs_ref, (cs_ref, mixs_ref, xms_ref), mergeds_ref))
    pl.when(second & (i == 0))(lambda: project(mergeds_ref, zs_ref))


def _merge(h, c, mix, xm, hs, cs, mixs, xms, w_gates, b_gate, w_conv_out_t, w_dil_o_t, w_x_o_t, w_out_t, *, tm):
    m, ms = h.shape[0], hs.shape[0]
    n1, n2 = _D_TILES, D_MODEL // TN_OUT

    n_rows = m // tm
    whole = lambda a: pl.BlockSpec(a.shape, lambda i, s: (0, 0))

    def tile1(s):
        return jnp.where(s < n1, s, 0)

    def gate_spec(br):
        return pl.BlockSpec((D_MODEL, TN), lambda i, s, br=br: (0, br * _D_TILES + tile1(s)))

    def colw(kdim):
        return pl.BlockSpec((kdim, TN), lambda i, s: (0, tile1(s)))

    def rowblk(w):
        return pl.BlockSpec((tm, w), lambda i, s: (_row_ahead(i, s, n1, n_rows), 0))

    return pl.pallas_call(
        _merge_kernel,
        grid=(m // tm, n1 + n2),
        in_specs=[
            rowblk(D_MODEL), rowblk(C_CONV), rowblk(W_KV), rowblk(W_QX),
            whole(hs), whole(cs), whole(mixs), whole(xms),
            gate_spec(0), gate_spec(1), gate_spec(2),
            pl.BlockSpec((3, 1, TN), lambda i, s: (0, 0, tile1(s))),
            colw(C_CONV), colw(W_KV), colw(W_QX),
            pl.BlockSpec((D_MODEL, TN_OUT), lambda i, s: (0, jnp.maximum(s - n1, 0))),
        ],
        out_specs=[
            pl.BlockSpec((tm, TN_OUT), lambda i, s: (i, jnp.maximum(s - n1, 0))),
            pl.BlockSpec((ms, TN_OUT), lambda i, s: (0, jnp.where(i == 0, jnp.maximum(s - n1, 0), n2 - 1))),
        ],
        out_shape=[jax.ShapeDtypeStruct((m, D_MODEL), F32), jax.ShapeDtypeStruct((ms, D_MODEL), F32)],
        scratch_shapes=[pltpu.VMEM((n1, tm, TN), BF16), pltpu.VMEM((n1, ms, TN), BF16)],
        compiler_params=_params(("arbitrary", "arbitrary"), 58),
        name="merge",
    )(h, c, mix, xm, hs, cs, mixs, xms, w_gates, w_gates, w_gates, b_gate.reshape(3, 1, D_MODEL), w_conv_out_t,
      w_dil_o_t, w_x_o_t, w_out_t)


def _ffn_kernel(x_ref, z_ref, xs_ref, zs_ref, gmix_ref, gpre_ref, wgu_ref, wd_ref, gpost_ref, out_ref, outs_ref,
                x1_ref, h_ref, x1s_ref, hs_ref, *, n_steps):
    i, s = pl.program_id(0), pl.program_id(1)

    def prologue(x_ref, z_ref, x1_ref, h_ref, out_ref):
        x1 = x_ref[...] + _rms_rows(z_ref[...], gmix_ref[...])
        x1_ref[...] = x1
        h_ref[...] = _rms_rows(x1, gpre_ref[...]).astype(BF16)
        out_ref[...] = jnp.zeros_like(out_ref)

    def step(h_ref, out_ref):
        h = h_ref[...]
        half = TF // 2
        part = None
        for c in range(2):
            gate_up = _dot(h, wgu_ref[:, c * TF:(c + 1) * TF])
            gate, up = gate_up[:, :half], gate_up[:, half:]
            act = (gate * jax.nn.sigmoid(gate) * up).astype(BF16)
            down = _dot(act, wd_ref[c * half:(c + 1) * half, :])
            part = down if part is None else part + down
        out_ref[...] += part

    def epilogue(x1_ref, out_ref):
        out_ref[...] = x1_ref[...] + _rms_rows(out_ref[...], gpost_ref[...])

    pl.when(s == 0)(lambda: prologue(x_ref, z_ref, x1_ref, h_ref, out_ref))
    step(h_ref, out_ref)
    pl.when(s == n_steps - 1)(lambda: epilogue(x1_ref, out_ref))

    pl.when((i == 0) & (s == 0))(lambda: prologue(xs_ref, zs_ref, x1s_ref, hs_ref, outs_ref))
    pl.when(i == 0)(lambda: step(hs_ref, outs_ref))
    pl.when((i == 0) & (s == n_steps - 1))(lambda: epilogue(x1s_ref, outs_ref))


def _ffn(x, z, xs, zs, g_mix, g_pre, w_gate_up, w_down, g_post, *, tm):
    m, ms = x.shape[0], xs.shape[0]
    n_steps = D_FF // TF
    rowblk = pl.BlockSpec((tm, D_MODEL), lambda i, s: (i, 0))
    sample = pl.BlockSpec((ms, D_MODEL), lambda i, s: (0, 0))
    gain = pl.BlockSpec((1, D_MODEL), lambda i, s: (0, 0))
    return pl.pallas_call(
        functools.partial(_ffn_kernel, n_steps=n_steps),
        grid=(m // tm, n_steps),
        in_specs=[
            rowblk, rowblk, sample, sample, gain, gain,
            pl.BlockSpec((D_MODEL, 2 * TF), lambda i, s: (0, s)),
            pl.BlockSpec((TF, D_MODEL), lambda i, s: (s, 0)),
            gain,
        ],
        out_specs=[rowblk, sample],
        out_shape=[jax.ShapeDtypeStruct((m, D_MODEL), F32), jax.ShapeDtypeStruct((ms, D_MODEL), F32)],
        scratch_shapes=[pltpu.VMEM((tm, D_MODEL), F32), pltpu.VMEM((tm, D_MODEL), BF16),
                        pltpu.VMEM((ms, D_MODEL), F32), pltpu.VMEM((ms, D_MODEL), BF16)],
        compiler_params=_params(("arbitrary", "arbitrary"), 54),
        name="ffn",
    )(x, z, xs, zs, g_mix, g_pre, w_gate_up, w_down, g_post)


def kernel(x_prompt, x_sample, cache_win_k, cache_win_v, state_conv, cache_mem_k, cache_mem_v, mem_prompt,
           g_pre_mix, w_in, b_gate, conv_w, conv_b, conv_ln_g, conv_ln_b, w_conv_out, w_dil_o, g_mem, w_mem_kv,
           w_x_o, w_out, g_post_mix, g_pre_ffn, w_ffn_gate, w_ffn_up, w_ffn_down, g_post_ffn):
    batch, seq, _ = x_prompt.shape
    n_seq = x_sample.shape[0]
    depth = w_in.shape[0]
    assert depth == 1 and x_sample.shape[1] == 1 and seq % (DIL_GROUPS[-1][1] * SPAN) == 0
    assert cache_win_k.shape[2] == DIL_GROUPS[-1][0]

    idx = jnp.arange(1, N_QH + 1, dtype=F32)
    slopes = jnp.exp2(-8.0 * idx / N_QH).reshape(N_GROUPS, N_KV)

    xp, xs = x_prompt.reshape(batch * seq, D_MODEL), x_sample.reshape(n_seq, D_MODEL)
    l = 0
    w_in_t = w_in[l][:, :_GATE_TILE0 * TN].astype(BF16)
    w_mem_kv_t = w_mem_kv[l].astype(BF16)
    row = lambda a: a[l].reshape(1, -1)
    g_pre, g_post, g_ffn_pre, g_ffn_post = row(g_pre_mix), row(g_post_mix), row(g_pre_ffn), row(g_post_ffn)
    cb, lng, lnb = row(conv_b), row(conv_ln_g), row(conv_ln_b)
    conv_w_slabs = conv_w[l].reshape(CONV_WIDTH, N_SLABS, LANES).transpose(1, 0, 2)

    mk, mv = _mem_kv(mem_prompt.reshape(batch * N_MEM, D_MODEL), row(g_mem), w_mem_kv_t, tm=512)
    h_p, u_p, q_p, k_p, v_p, qx_p, h_s, u_s, q_s, k_s, v_s, qx_s = _in_proj(xp, xs, g_pre, w_in_t, tm=1024)

    c_p, conv_p, k_rows, v_rows, *branch_weights = _conv_prompt(
        u_p, conv_w_slabs, cb, lng, lnb, k_p, v_p, (w_conv_out[l], w_dil_o[l], w_x_o[l], w_out[l]),
        batch=batch, seq=seq)
    mix_p, xm_p, *ffn_weights = _attn_prompt(
        slopes, q_p.reshape(batch, seq, W_QD), k_p.reshape(batch, seq, W_KV), v_p.reshape(batch, seq, W_KV),
        qx_p.reshape(batch, seq, W_QX), mk.reshape(batch, N_MEM, W_QX), mv.reshape(batch, N_MEM, W_QX),
        w_ffn_gate[l], w_ffn_up[l], w_ffn_down[l], batch=batch, seq=seq)

    c_s, conv_s = _conv_sample(state_conv[l], u_s, conv_w[l], cb, lng, lnb)
    mix_s, xm_s, w_gates = _attn_sample(slopes, q_s, k_s, v_s, qx_s, cache_win_k[l], cache_win_v[l],
                                        cache_mem_k[l], cache_mem_v[l], w_in[l])

    z_p, z_s = _merge(h_p, c_p, mix_p.reshape(batch * seq, W_KV), xm_p.reshape(batch * seq, W_QX),
                      h_s, c_s, mix_s.reshape(n_seq, W_KV), xm_s.reshape(n_seq, W_QX),
                      w_gates, b_gate[l], *branch_weights, tm=1024)
    y_p, y_s = _ffn(xp, z_p, xs, z_s, g_post, g_ffn_pre, *ffn_weights, g_ffn_post, tm=512)

    win = min(DIL_GROUPS[-1][0], seq)
    kv_shape = (1, batch, seq, N_KV, HEAD_DIM)
    mem_shape = (1, batch, N_MEM, N_XHEADS, HEAD_DIM)
    new_shape = (1, n_seq, 1, N_KV, HEAD_DIM)
    return (y_p.reshape(batch, seq, D_MODEL), y_s.reshape(n_seq, 1, D_MODEL),
            k_rows.reshape(kv_shape)[:, :, seq - win:], v_rows.reshape(kv_shape)[:, :, seq - win:], conv_p,
            mk.reshape(mem_shape), mv.reshape(mem_shape),
            k_s.reshape(new_shape), v_s.reshape(new_shape), conv_s.reshape(1, n_seq, CONV_PREFIX, C_CONV))
```

```python
import functools

import jax
import jax.numpy as jnp
from jax import lax
from jax.experimental import pallas as pl
from jax.experimental.pallas import tpu as pltpu

F32 = jnp.float32
BF16 = jnp.bfloat16

D_MODEL = 2048
C_CONV = D_MODEL // 2
CONV_WIDTH = 31
CONV_PREFIX = CONV_WIDTH - 1
HEAD_DIM = 128
N_KV = 4
DIL_GROUPS = ((128, 1), (512, 4), (2048, 16))
N_GROUPS = len(DIL_GROUPS)
N_QH = N_GROUPS * N_KV
SPAN = 128
N_MEM = 256
N_XHEADS = 4
W_QD = N_QH * HEAD_DIM
W_KV = N_KV * HEAD_DIM
W_QX = N_XHEADS * HEAD_DIM
D_FF = 5632
EPS = 1e-6
NEG = -1e30
SCALE = HEAD_DIM ** -0.5
LANES = 128
SUBLANES = 8
N_SLABS = C_CONV // LANES
MIB = 1024 * 1024

TN = 512
_A_TILES = C_CONV // TN
_Q_TILES = W_QD // TN
_KV_TILES = W_KV // TN
_QX_TILES = W_QX // TN
_STEP_Q0 = _A_TILES
_STEP_K0 = _STEP_Q0 + _Q_TILES
_STEP_V0 = _STEP_K0 + _KV_TILES
_STEP_X0 = _STEP_V0 + _KV_TILES
_IN_STEPS = _STEP_X0 + _QX_TILES
_GATE_TILE0 = 2 * _A_TILES + _Q_TILES + 2 * _KV_TILES + _QX_TILES
_D_TILES = D_MODEL // TN
_ATTN_UNITS = 8
_GATE_CAST_COLS = 1024
TN_OUT = 512
TF = 512


def _params(semantics, vmem_mib):
    return pltpu.CompilerParams(dimension_semantics=semantics, vmem_limit_bytes=vmem_mib * MIB)


def _rms_rows(x, g):
    return x * lax.rsqrt(jnp.mean(x * x, axis=-1, keepdims=True) + EPS) * g


def _dot(a, b):
    return jnp.dot(a, b, preferred_element_type=F32)


def _row_ahead(i, s, free_from, n_rows):
    return jnp.minimum(i + (s >= free_from).astype(jnp.int32), n_rows - 1)


def _in_proj_kernel(x_ref, xs_ref, g_ref, w1_ref, w2_ref, h_ref, u_ref, q_ref, k_ref, v_ref, qx_ref,
                    hs_ref, us_ref, qs_ref, ks_ref, vs_ref, qxs_ref):
    i, s = pl.program_id(0), pl.program_id(1)

    def project(on, x_ref, h_ref, store_u, q_ref, k_ref, v_ref, qx_ref):
        def norm():
            h_ref[...] = _rms_rows(x_ref[...], g_ref[...]).astype(BF16)

        def glu():
            h = h_ref[...]
            store_u(_dot(h, w1_ref[...]) * jax.nn.sigmoid(_dot(h, w2_ref[...])))

        def into(out_ref):
            def tile():
                out_ref[...] = _dot(h_ref[...], w1_ref[...]).astype(out_ref.dtype)
            return tile

        pl.when(on & (s == 0))(norm)
        pl.when(on & (s < _STEP_Q0))(glu)
        pl.when(on & (s >= _STEP_Q0) & (s < _STEP_K0))(into(q_ref))
        pl.when(on & (s >= _STEP_K0) & (s < _STEP_V0))(into(k_ref))
        pl.when(on & (s >= _STEP_V0) & (s < _STEP_X0))(into(v_ref))
        pl.when(on & (s >= _STEP_X0))(into(qx_ref))

    def store_slabs(u):
        for c in range(TN // LANES):
            u_ref[c] = u[:, c * LANES:(c + 1) * LANES]

    def store_rows(u):
        us_ref[...] = u

    project(True, x_ref, h_ref, store_slabs, q_ref, k_ref, v_ref, qx_ref)
    project(i == 0, xs_ref, hs_ref, store_rows, qs_ref, ks_ref, vs_ref, qxs_ref)


def _in_proj(x, xs, g, w_in_t, *, tm):
    m, ms = x.shape[0], xs.shape[0]
    last_a = _A_TILES - 1

    def sample_tiles(step0, n_tiles):
        return pl.BlockSpec((ms, TN), lambda i, s: (0, jnp.where(i == 0, jnp.clip(s - step0, 0, n_tiles - 1),
                                                                 n_tiles - 1)))

    def out_tiles(step0, n_tiles):
        def index(i, s):
            moved = (s >= step0 + n_tiles) & (i < m // tm - 1)
            return (i + moved.astype(jnp.int32), jnp.where(moved, 0, jnp.clip(s - step0, 0, n_tiles - 1)))
        return pl.BlockSpec((tm, TN), index)

    return pl.pallas_call(
        _in_proj_kernel,
        grid=(m // tm, _IN_STEPS),
        in_specs=[
            pl.BlockSpec((tm, D_MODEL), lambda i, s: (_row_ahead(i, s, _A_TILES, m // tm), 0)),
            pl.BlockSpec((ms, D_MODEL), lambda i, s: (0, 0)),
            pl.BlockSpec((1, D_MODEL), lambda i, s: (0, 0)),
            pl.BlockSpec((D_MODEL, TN), lambda i, s: (0, jnp.where(s < _A_TILES, s, s + _A_TILES))),
            pl.BlockSpec((D_MODEL, TN), lambda i, s: (0, jnp.where(s < _A_TILES, s, 0) + _A_TILES)),
        ],
        out_specs=[
            pl.BlockSpec((tm, D_MODEL), lambda i, s: (i, 0)),
            pl.BlockSpec((TN // LANES, tm, LANES), lambda i, s: (jnp.minimum(s, last_a), i, 0)),
            out_tiles(_STEP_Q0, _Q_TILES),
            out_tiles(_STEP_K0, _KV_TILES),
            out_tiles(_STEP_V0, _KV_TILES),
            out_tiles(_STEP_X0, _QX_TILES),
            pl.BlockSpec((ms, D_MODEL), lambda i, s: (0, 0)),
            sample_tiles(0, _A_TILES),
            sample_tiles(_STEP_Q0, _Q_TILES),
            sample_tiles(_STEP_K0, _KV_TILES),
            sample_tiles(_STEP_V0, _KV_TILES),
            sample_tiles(_STEP_X0, _QX_TILES),
        ],
        out_shape=[
            jax.ShapeDtypeStruct((m, D_MODEL), BF16),
            jax.ShapeDtypeStruct((N_SLABS, m, LANES), F32),
            jax.ShapeDtypeStruct((m, W_QD), F32),
            jax.ShapeDtypeStruct((m, W_KV), F32),
            jax.ShapeDtypeStruct((m, W_KV), F32),
            jax.ShapeDtypeStruct((m, W_QX), BF16),
            jax.ShapeDtypeStruct((ms, D_MODEL), BF16),
            jax.ShapeDtypeStruct((ms, C_CONV), F32),
            jax.ShapeDtypeStruct((ms, W_QD), F32),
            jax.ShapeDtypeStruct((ms, W_KV), F32),
            jax.ShapeDtypeStruct((ms, W_KV), F32),
            jax.ShapeDtypeStruct((ms, W_QX), F32),
        ],
        compiler_params=_params(("arbitrary", "arbitrary"), 60),
        name="in_proj",
    )(x, xs, g, w_in_t, w_in_t)


def _mem_kv_kernel(x_ref, g_ref, w_ref, mk_ref, mv_ref, h_ref):
    s = pl.program_id(1)

    @pl.when(s == 0)
    def _():
        h_ref[...] = _rms_rows(x_ref[...], g_ref[...]).astype(BF16)
        mk_ref[...] = _dot(h_ref[...], w_ref[...])

    @pl.when(s == 1)
    def _():
        mv_ref[...] = _dot(h_ref[...], w_ref[...])


def _mem_kv(mem, g, w_t, *, tm):
    m = mem.shape[0]
    return pl.pallas_call(
        _mem_kv_kernel,
        grid=(m // tm, 2),
        in_specs=[
            pl.BlockSpec((tm, D_MODEL), lambda i, s: (i, 0)),
            pl.BlockSpec((1, D_MODEL), lambda i, s: (0, 0)),
            pl.BlockSpec((D_MODEL, W_QX), lambda i, s: (0, s)),
        ],
        out_specs=[
            pl.BlockSpec((tm, W_QX), lambda i, s: (i, 0)),
            pl.BlockSpec((tm, W_QX), lambda i, s: (i, 0)),
        ],
        out_shape=[jax.ShapeDtypeStruct((m, W_QX), F32)] * 2,
        scratch_shapes=[pltpu.VMEM((tm, D_MODEL), BF16)],
        compiler_params=_params(("parallel", "arbitrary"), 32),
        name="mem_kv",
    )(mem, g, w_t)


def _ln_swish_slabs(y_slabs, cb_ref, lng_ref, lnb_ref, out_dtype):
    ys = [y + cb_ref[:, c * LANES:(c + 1) * LANES] for c, y in enumerate(y_slabs)]
    tot = ys[0]
    for y in ys[1:]:
        tot = tot + y
    mu = jnp.sum(tot, axis=-1, keepdims=True) * (1.0 / C_CONV)
    ds = [y - mu for y in ys]
    sq = ds[0] * ds[0]
    for d in ds[1:]:
        sq = sq + d * d
    rs = lax.rsqrt(jnp.sum(sq, axis=-1, keepdims=True) * (1.0 / C_CONV) + EPS)
    outs = []
    for c, d in enumerate(ds):
        z = d * rs * lng_ref[:, c * LANES:(c + 1) * LANES] + lnb_ref[:, c * LANES:(c + 1) * LANES]
        outs.append((z * jax.nn.sigmoid(z)).astype(out_dtype))
    return outs


def _conv_prompt_kernel(u_ref, w_ref, cb_ref, lng_ref, lnb_ref, k_ref, v_ref, *rest, seq, tt, n_cast):
    cast_in, (c_ref, st_ref, kout_ref, vout_ref) = rest[:n_cast], rest[n_cast:n_cast + 4]
    cast_out, (head_ref, y_ref) = rest[n_cast + 4:2 * n_cast + 4], rest[2 * n_cast + 4:]
    half = pl.program_id(1)
    n_tiles = seq // tt // 2
    pad = 32

    for src_ref, dst_ref in ((k_ref, kout_ref), (v_ref, vout_ref)):
        for h in range(N_KV):
            dst_ref[pl.ds(h, src_ref.shape[0], stride=N_KV), :] = src_ref[:, h * LANES:(h + 1) * LANES]
    for src_ref, dst_ref in zip(cast_in, cast_out):
        dst_ref[...] = src_ref[...].astype(BF16)

    def tile(src_ref, base, t0):
        def slab(c, carry):
            acc = jnp.zeros((tt, LANES), F32)
            for j in range(SUBLANES):
                taps = range(j, CONV_WIDTH, SUBLANES)
                win = src_ref[c, pl.ds(base + j, tt + taps[-1] - j), :]
                for k in taps:
                    acc = acc + win[k - j:k - j + tt] * w_ref[c, k:k + 1, :]
            y_ref[c] = acc
            return carry

        lax.fori_loop(0, N_SLABS, slab, 0)
        ys = [y_ref[c] for c in range(N_SLABS)]
        outs = _ln_swish_slabs(ys, cb_ref, lng_ref, lnb_ref, BF16)
        for c in range(N_SLABS):
            c_ref[pl.ds(t0, tt), c * LANES:(c + 1) * LANES] = outs[c]

    @pl.when(half == 0)
    def _():
        head_ref[:, 0:pad, :] = jnp.zeros((N_SLABS, pad, LANES), F32)
        head_ref[:, pad:pad + tt, :] = u_ref[:, 0:tt, :]
        tile(head_ref, pad - CONV_PREFIX, 0)

    @pl.when(half > 0)
    def _():
        tile(u_ref, pl.multiple_of(half * (n_tiles * tt), tt) - CONV_PREFIX, 0)

    def body(i, carry):
        t0 = pl.multiple_of((half * n_tiles + i) * tt, tt)
        tile(u_ref, t0 - CONV_PREFIX, pl.multiple_of(i * tt, tt))
        return carry

    lax.fori_loop(1, n_tiles, body, 0)
    for c in range(N_SLABS):
        st_ref[:, c * LANES:(c + 1) * LANES] = u_ref[c, seq - CONV_PREFIX:seq, :]


def _conv_prompt(u_slabs, w_slabs, conv_b, ln_g, ln_b, k, v, cast_weights, *, batch, seq):
    tt = 128
    halves = 2
    n_steps = batch * halves
    step = lambda b, j: b * halves + j
    const = lambda *idx: (lambda b, j: idx)

    def row_slice(rows, cols):
        return pl.BlockSpec((rows // n_steps, cols), lambda b, j: (step(b, j), 0))

    kv_in = row_slice(batch * seq, W_KV)
    kv_out = row_slice(batch * seq * N_KV, HEAD_DIM)
    w_specs = [row_slice(*w.shape) for w in cast_weights]
    return pl.pallas_call(
        functools.partial(_conv_prompt_kernel, seq=seq, tt=tt, n_cast=len(cast_weights)),
        grid=(batch, halves),
        in_specs=[
            pl.BlockSpec((N_SLABS, seq, LANES), lambda b, j: (0, b, 0)),
            pl.BlockSpec((N_SLABS, CONV_WIDTH, LANES), const(0, 0, 0)),
            pl.BlockSpec((1, C_CONV), const(0, 0)),
            pl.BlockSpec((1, C_CONV), const(0, 0)),
            pl.BlockSpec((1, C_CONV), const(0, 0)),
            kv_in, kv_in,
        ] + w_specs,
        out_specs=[
            pl.BlockSpec((seq // halves, C_CONV), lambda b, j: (step(b, j), 0)),
            pl.BlockSpec((None, None, CONV_PREFIX, C_CONV), lambda b, j: (0, b, 0, 0)),
            kv_out, kv_out,
        ] + w_specs,
        out_shape=[
            jax.ShapeDtypeStruct((batch * seq, C_CONV), BF16),
            jax.ShapeDtypeStruct((1, batch, CONV_PREFIX, C_CONV), F32),
            jax.ShapeDtypeStruct((batch * seq * N_KV, HEAD_DIM), F32),
            jax.ShapeDtypeStruct((batch * seq * N_KV, HEAD_DIM), F32),
        ] + [jax.ShapeDtypeStruct(w.shape, BF16) for w in cast_weights],
        scratch_shapes=[pltpu.VMEM((N_SLABS, 32 + tt, LANES), F32), pltpu.VMEM((N_SLABS, tt, LANES), F32)],
        compiler_params=_params(("parallel", "arbitrary"), 56),
        name="conv_prompt",
    )(u_slabs, w_slabs, conv_b, ln_g, ln_b, k, v, *cast_weights)


def _conv_sample_kernel(state_ref, u_ref, w_ref, cb_ref, lng_ref, lnb_ref, c_ref, st_ref, *, n_seq):
    w_hist = w_ref[0:CONV_PREFIX, :]
    w_new = w_ref[CONV_PREFIX:CONV_WIDTH, :]
    for n in range(n_seq):
        hist = state_ref[n]
        new = u_ref[n:n + 1, :]
        y = jnp.sum(hist * w_hist, axis=0, keepdims=True) + new * w_new
        ys = [y[:, c * LANES:(c + 1) * LANES] for c in range(N_SLABS)]
        outs = _ln_swish_slabs(ys, cb_ref, lng_ref, lnb_ref, F32)
        for c in range(N_SLABS):
            c_ref[n:n + 1, c * LANES:(c + 1) * LANES] = outs[c]
        st_ref[n, 0:CONV_PREFIX - 1, :] = state_ref[n, 1:CONV_PREFIX, :]
        st_ref[n, CONV_PREFIX - 1:CONV_PREFIX, :] = new


def _conv_sample(state, u, conv_w, conv_b, ln_g, ln_b):
    n_seq = state.shape[0]
    return pl.pallas_call(
        functools.partial(_conv_sample_kernel, n_seq=n_seq),
        out_shape=[
            jax.ShapeDtypeStruct((n_seq, C_CONV), F32),
            jax.ShapeDtypeStruct((n_seq, CONV_PREFIX, C_CONV), F32),
        ],
        compiler_params=pltpu.CompilerParams(vmem_limit_bytes=32 * MIB),
        name="conv_sample",
    )(state, u, conv_w, conv_b, ln_g, ln_b)


def _qk(q, k):
    return lax.dot_general(q, k, (((1,), (1,)), ((), ())), preferred_element_type=F32)


def _attn_prompt_kernel(slopes_ref, q0_ref, q1_ref, q2_ref, k_ref, v_ref, qx_ref, mk_ref, mv_ref,
                        wg_ref, wu_ref, wd_ref, mix_ref, xm_ref, wgu_out_ref, wd_out_ref,
                        bias_ref, o_ref, lse_ref, s_ref, p_ref, m_ref, ones_ref, *, seq):
    head = pl.program_id(1)
    ones_ref[...] = jnp.ones(ones_ref.shape, BF16)
    half = TF // 2
    for j in range(D_FF // half):
        cols = slice(j * half, (j + 1) * half)
        wgu_out_ref[:, 2 * j * half:(2 * j + 1) * half] = wg_ref[:, cols].astype(BF16)
        wgu_out_ref[:, (2 * j + 1) * half:(2 * j + 2) * half] = wu_ref[:, cols].astype(BF16)
    wd_out_ref[...] = wd_ref[...].astype(BF16)
    qi = lax.broadcasted_iota(jnp.int32, (SPAN, 2 * SPAN), 0)
    ci = lax.broadcasted_iota(jnp.int32, (SPAN, 2 * SPAN), 1)
    dist = qi + SPAN - ci
    valid = (dist >= 0) & (dist <= SPAN)
    for g, (_, dil) in enumerate(DIL_GROUPS):
        bias = -slopes_ref[g, head] * (dist * dil).astype(F32)
        bias_ref[g] = jnp.where(valid, bias, NEG)

    def rows(ref, start, size, dil):
        if dil == 1:
            return ref[pl.ds(start, size), :]
        return ref[pl.ds(start, size, stride=dil), :]

    def put(ref, g, start, dil, val):
        if dil == 1:
            ref[g, pl.ds(start, SPAN), :] = val
        else:
            ref[g, pl.ds(start, SPAN, stride=dil), :] = val

    def staged(items, n_keys_of, scores, values, emit):
        for u, item in enumerate(items):
            s_ref[u, :, 0:n_keys_of(item)] = scores(item)
        for u, item in enumerate(items):
            s = s_ref[u, :, 0:n_keys_of(item)]
            m = jnp.max(s, axis=-1, keepdims=True)
            p_ref[u, :, 0:n_keys_of(item)] = jnp.exp(s - m).astype(BF16)
            m_ref[u] = jnp.broadcast_to(m, (SPAN, LANES))
        for u, item in enumerate(items):
            p = p_ref[u, :, 0:n_keys_of(item)]
            l = _dot(p, ones_ref[0:n_keys_of(item), :])
            emit(item, _dot(p, values(item)) * (1.0 / l), m_ref[u] + jnp.log(l))

    def units(g, dil, q_ref, starts_keys):
        def scores(item):
            start, n_keys = item
            q = rows(q_ref, start, SPAN, dil).astype(BF16)
            k = rows(k_ref, start - (n_keys - SPAN) * dil, n_keys, dil).astype(BF16)
            return _qk(q, k) * SCALE + bias_ref[g, :, 2 * SPAN - n_keys:2 * SPAN]

        def values(item):
            start, n_keys = item
            return rows(v_ref, start - (n_keys - SPAN) * dil, n_keys, dil).astype(BF16)

        def emit(item, o, lse):
            put(o_ref, g, item[0], dil, o)
            put(lse_ref, g, item[0], dil, lse)

        staged(starts_keys, lambda item: item[1], scores, values, emit)

    q_refs = (q0_ref, q1_ref, q2_ref)
    for g, (_, dil) in enumerate(DIL_GROUPS):
        n_blocks = seq // dil // SPAN
        q_ref = q_refs[g]
        if n_blocks == 1:
            per_body = 8

            def first_blocks(j, carry, g=g, dil=dil, q_ref=q_ref):
                units(g, dil, q_ref, [(j * per_body + u, SPAN) for u in range(per_body)])
                return carry
            lax.fori_loop(0, dil // per_body, first_blocks, 0)
        elif dil == 1:
            per_body = _ATTN_UNITS

            def later_blocks(j, carry, g=g, dil=dil, q_ref=q_ref):
                units(g, dil, q_ref, [(pl.multiple_of((j * per_body + u) * SPAN, SPAN), 2 * SPAN)
                                      for u in range(per_body)])
                return carry
            units(g, dil, q_ref, [(0, SPAN)] + [(n * SPAN, 2 * SPAN) for n in range(1, per_body)])
            lax.fori_loop(1, n_blocks // per_body, later_blocks, 0)
        else:
            def residue(j, carry, g=g, dil=dil, q_ref=q_ref, n_blocks=n_blocks):
                work = []
                for r in (2 * j, 2 * j + 1):
                    work.append((r, SPAN))
                    work += [(r + n * (SPAN * dil), 2 * SPAN) for n in range(1, n_blocks)]
                units(g, dil, q_ref, work)
                return carry
            lax.fori_loop(0, dil // 2, residue, 0)

    tq = 256

    def mix(j, carry):
        t0 = pl.multiple_of(j * tq, tq)
        l0 = lse_ref[0, pl.ds(t0, tq), :]
        l1 = lse_ref[1, pl.ds(t0, tq), :]
        l2 = lse_ref[2, pl.ds(t0, tq), :]
        top = jnp.maximum(jnp.maximum(l0, l1), l2)
        w0 = jnp.exp(l0 - top)
        w1 = jnp.exp(l1 - top)
        w2 = jnp.exp(l2 - top)
        num = (w0 * o_ref[0, pl.ds(t0, tq), :] + w1 * o_ref[1, pl.ds(t0, tq), :]
               + w2 * o_ref[2, pl.ds(t0, tq), :])
        mix_ref[pl.ds(t0, tq), :] = (num / (w0 + w1 + w2)).astype(BF16)
        return carry

    lax.fori_loop(0, seq // tq, mix, 0, unroll=2)

    def memory(j, carry):
        def scores(t0):
            return _qk(qx_ref[pl.ds(t0, SPAN), :], mk_ref[...].astype(BF16)) * SCALE

        def emit(t0, o, lse):
            xm_ref[pl.ds(t0, SPAN), :] = o.astype(BF16)

        staged([pl.multiple_of((j * _ATTN_UNITS + u) * SPAN, SPAN) for u in range(_ATTN_UNITS)],
               lambda t0: N_MEM, scores, lambda t0: mv_ref[...].astype(BF16), emit)
        return carry

    lax.fori_loop(0, seq // SPAN // _ATTN_UNITS, memory, 0)


def _attn_prompt(slopes, q, k, v, qx, mk, mv, w_gate, w_up, w_down, *, batch, seq):
    def col(c):
        return pl.BlockSpec((None, seq, LANES), lambda b, h, c=c: (b, 0, c * N_KV + h))

    head_spec = pl.BlockSpec((None, seq, LANES), lambda b, h: (b, 0, h))
    mem_spec = pl.BlockSpec((None, N_MEM, LANES), lambda b, h: (b, 0, h))
    n_steps = batch * N_KV

    def row_slice(rows, cols):
        return pl.BlockSpec((rows // n_steps, cols), lambda b, h: (b * N_KV + h, 0))

    weights = (w_gate, w_up, w_down)
    gu_shape = (D_MODEL, 2 * D_FF)
    return pl.pallas_call(
        functools.partial(_attn_prompt_kernel, seq=seq),
        grid=(batch, N_KV),
        in_specs=[
            pl.BlockSpec(memory_space=pltpu.SMEM),
            col(0), col(1), col(2), head_spec, head_spec, head_spec, mem_spec, mem_spec,
        ] + [row_slice(*w.shape) for w in weights],
        out_specs=[head_spec, head_spec, row_slice(*gu_shape), row_slice(*w_down.shape)],
        out_shape=[jax.ShapeDtypeStruct((batch, seq, W_KV), BF16),
                   jax.ShapeDtypeStruct((batch, seq, W_QX), BF16),
                   jax.ShapeDtypeStruct(gu_shape, BF16), jax.ShapeDtypeStruct(w_down.shape, BF16)],
        scratch_shapes=[
            pltpu.VMEM((N_GROUPS, SPAN, 2 * SPAN), F32),
            pltpu.VMEM((N_GROUPS, seq, LANES), F32),
            pltpu.VMEM((N_GROUPS, seq, LANES), F32),
            pltpu.VMEM((_ATTN_UNITS, SPAN, 2 * SPAN), F32),
            pltpu.VMEM((_ATTN_UNITS, SPAN, 2 * SPAN), BF16),
            pltpu.VMEM((_ATTN_UNITS, SPAN, LANES), F32),
            pltpu.VMEM((2 * SPAN, LANES), BF16),
        ],
        compiler_params=_params(("parallel", "parallel"), 56),
        name="attn_prompt",
    )(slopes, q, q, q, k, v, qx, mk, mv, *weights)


def _attn_sample_kernel(slopes_ref, q_ref, kn_ref, vn_ref, qx_ref, ck0_ref, ck1_ref, ck2_ref,
                        cv0_ref, cv1_ref, cv2_ref, mk_ref, mv_ref, *rest, bs):
    wcol_refs, (mix_ref, xm_ref, wgate_ref) = rest[:-3], rest[-3:]

    def by_row(ref):
        return ref if len(ref.shape) == 3 else ref.reshape(ref.shape[0], ref.shape[1] * ref.shape[2], LANES)

    def head_rows(ref, n, h, n_keys):
        return ref[n, pl.ds(h, n_keys, stride=ref.shape[1] // n_keys), :]

    ck_refs = tuple(by_row(r) for r in (ck0_ref, ck1_ref, ck2_ref))
    cv_refs = tuple(by_row(r) for r in (cv0_ref, cv1_ref, cv2_ref))
    back =(SPAN - lax.broadcasted_iota(jnp.int32, (SPAN, 1), 0)).astype(F32)
    for c, wcol_ref in enumerate(wcol_refs):
        wgate_ref[:, c * _GATE_CAST_COLS:(c + 1) * _GATE_CAST_COLS] = wcol_ref[...].astype(BF16)

    def one(n, carry):
        for h in range(N_KV):
            hs = slice(h * LANES, (h + 1) * LANES)
            kn = kn_ref[n, :, hs]
            vn = vn_ref[n, :, hs]
            scores, s_new = [], []
            for g, (_, dil) in enumerate(DIL_GROUPS):
                qs = slice(g * W_KV + h * LANES, g * W_KV + (h + 1) * LANES)
                q = q_ref[n, :, qs]
                s = jnp.sum(head_rows(ck_refs[g], n, h, SPAN) * q, axis=-1, keepdims=True) * SCALE
                scores.append(s - slopes_ref[g, h] * (back * float(dil)))
                s_new.append(jnp.sum(kn * q, axis=-1, keepdims=True) * SCALE)
            top = s_new[0]
            for g in range(N_GROUPS):
                top = jnp.maximum(top, jnp.maximum(s_new[g], jnp.max(scores[g], axis=0, keepdims=True)))
            num = jnp.zeros((1, LANES), F32)
            den = jnp.zeros((1, 1), F32)
            for g in range(N_GROUPS):
                p = jnp.exp(scores[g] - top)
                p_new = jnp.exp(s_new[g] - top)
                num = num + jnp.sum(p * head_rows(cv_refs[g], n, h, SPAN), axis=0, keepdims=True) + p_new * vn
                den = den + jnp.sum(p, axis=0, keepdims=True) + p_new
            mix_ref[n, :, hs] = num / den
            qx = qx_ref[n, :, hs]
            sx = jnp.sum(head_rows(mk_ref, n, h, N_MEM) * qx, axis=-1, keepdims=True) * SCALE
            px = jnp.exp(sx - jnp.max(sx, axis=0, keepdims=True))
            ox = (jnp.sum(px * head_rows(mv_ref, n, h, N_MEM), axis=0, keepdims=True)
                  / jnp.sum(px, axis=0, keepdims=True))
            xm_ref[n, :, hs] = ox
        return carry

    lax.fori_loop(0, bs, one, 0)


def _attn_sample(slopes, q, kn, vn, qx, cache_k, cache_v, mem_k, mem_v, w_in):
    n_seq, win = cache_k.shape[0], cache_k.shape[1]
    bs = 2
    n_steps = n_seq // bs
    wrows = w_in.shape[0] // n_steps
    gate_col0 = _GATE_TILE0 * TN // _GATE_CAST_COLS
    wcol_specs = [pl.BlockSpec((wrows, _GATE_CAST_COLS), lambda i, c=c: (i, gate_col0 + c))
                  for c in range(3 * D_MODEL // _GATE_CAST_COLS)]
    row = lambda w: pl.BlockSpec((bs, 1, w), lambda i: (i, 0, 0))
    as_rows = lambda a: a.reshape(n_seq, 1, a.shape[-1])
    views_k, views_v, cache_specs = [], [], []
    for _, dil in DIL_GROUPS:
        nb = win // dil // SPAN
        if dil == 1:
            shape = (n_seq, win * N_KV, HEAD_DIM)
            cache_specs.append(pl.BlockSpec((bs, SPAN * N_KV, HEAD_DIM), lambda i, nb=nb: (i, nb - 1, 0)))
        else:
            shape = (n_seq, win // dil, dil * N_KV, HEAD_DIM)
            cache_specs.append(pl.BlockSpec((bs, SPAN, SUBLANES, HEAD_DIM), lambda i, nb=nb: (i, nb - 1, 0, 0)))
        views_k.append(cache_k.reshape(shape))
        views_v.append(cache_v.reshape(shape))
    mem_spec = pl.BlockSpec((bs, N_MEM * N_XHEADS, HEAD_DIM), lambda i: (i, 0, 0))
    mem_rows = lambda a: a.reshape(n_seq, N_MEM * N_XHEADS, HEAD_DIM)
    return pl.pallas_call(
        functools.partial(_attn_sample_kernel, bs=bs),
        grid=(n_seq // bs,),
        in_specs=[pl.BlockSpec(memory_space=pltpu.SMEM), row(W_QD), row(W_KV), row(W_KV), row(W_QX)]
        + cache_specs + cache_specs + [mem_spec, mem_spec] + wcol_specs,
        out_specs=[row(W_KV), row(W_QX), pl.BlockSpec((wrows, 3 * D_MODEL), lambda i: (i, 0))],
        out_shape=[jax.ShapeDtypeStruct((n_seq, 1, W_KV), F32), jax.ShapeDtypeStruct((n_seq, 1, W_QX), F32),
                   jax.ShapeDtypeStruct((w_in.shape[0], 3 * D_MODEL), BF16)],
        compiler_params=_params(("parallel",), 48),
        name="attn_sample",
    )(slopes, as_rows(q), as_rows(kn), as_rows(vn), as_rows(qx), *views_k, *views_v, mem_rows(mem_k),
      mem_rows(mem_v), *([w_in] * len(wcol_specs)))


def _merge_kernel(h_ref, c_ref, mix_ref, xm_ref, hs_ref, cs_ref, mixs_ref, xms_ref, wg0_ref, wg1_ref, wg2_ref,
                  bg_ref, wc_ref, wd_ref, wx_ref, wo_ref, z_ref, zs_ref, merged_ref, mergeds_ref):
    i, s = pl.program_id(0), pl.program_id(1)

    def gated_sum(h_ref, branches, merged_ref):
        h = h_ref[...]
        acc = None
        for br, (wg_ref, y_ref, wy_ref) in enumerate(zip((wg0_ref, wg1_ref, wg2_ref), branches,
                                                         (wc_ref, wd_ref, wx_ref))):
            gate = jax.nn.sigmoid(_dot(h, wg_ref[...]) + bg_ref[br])
            term = gate * _dot(y_ref[...].astype(BF16), wy_ref[...])
            acc = term if acc is None else acc + term
        merged_ref[s] = acc.astype(BF16)

    def project(merged_ref, z_ref):
        z = None
        for j in range(_D_TILES):
            part = _dot(merged_ref[j], wo_ref[j * TN:(j + 1) * TN, :])
            z = part if z is None else z + part
        z_ref[...] = z

    first, second = s < _D_TILES, s >= _D_TILES
    pl.when(first)(lambda: gated_sum(h_ref, (c_ref, mix_ref, xm_ref), merged_ref))
    pl.when(second)(lambda: project(merged_ref, z_ref))
    pl.when(first & (i == 0))(lambda: gated_sum(hs_ref, (cs_ref, mixs_ref, xms_ref), mergeds_ref))
    pl.when(second & (i == 0))(lambda: project(mergeds_ref, zs_ref))


def _merge(h, c, mix, xm, hs, cs, mixs, xms, w_gates, b_gate, w_conv_out_t, w_dil_o_t, w_x_o_t, w_out_t, *, tm):
    m, ms = h.shape[0], hs.shape[0]
    n1, n2 = _D_TILES, D_MODEL // TN_OUT

    n_rows = m // tm
    whole = lambda a: pl.BlockSpec(a.shape, lambda i, s: (0, 0))

    def tile1(s):
        return jnp.where(s < n1, s, 0)

    def gate_spec(br):
        return pl.BlockSpec((D_MODEL, TN), lambda i, s, br=br: (0, br * _D_TILES + tile1(s)))

    def colw(kdim):
        return pl.BlockSpec((kdim, TN), lambda i, s: (0, tile1(s)))

    def rowblk(w):
        return pl.BlockSpec((tm, w), lambda i, s: (_row_ahead(i, s, n1, n_rows), 0))

    return pl.pallas_call(
        _merge_kernel,
        grid=(m // tm, n1 + n2),
        in_specs=[
            rowblk(D_MODEL), rowblk(C_CONV), rowblk(W_KV), rowblk(W_QX),
            whole(hs), whole(cs), whole(mixs), whole(xms),
            gate_spec(0), gate_spec(1), gate_spec(2),
            pl.BlockSpec((3, 1, TN), lambda i, s: (0, 0, tile1(s))),
            colw(C_CONV), colw(W_KV), colw(W_QX),
            pl.BlockSpec((D_MODEL, TN_OUT), lambda i, s: (0, jnp.maximum(s - n1, 0))),
        ],
        out_specs=[
            pl.BlockSpec((tm, TN_OUT), lambda i, s: (i, jnp.maximum(s - n1, 0))),
            pl.BlockSpec((ms, TN_OUT), lambda i, s: (0, jnp.where(i == 0, jnp.maximum(s - n1, 0), n2 - 1))),
        ],
        out_shape=[jax.ShapeDtypeStruct((m, D_MODEL), F32), jax.ShapeDtypeStruct((ms, D_MODEL), F32)],
        scratch_shapes=[pltpu.VMEM((n1, tm, TN), BF16), pltpu.VMEM((n1, ms, TN), BF16)],
        compiler_params=_params(("arbitrary", "arbitrary"), 58),
        name="merge",
    )(h, c, mix, xm, hs, cs, mixs, xms, w_gates, w_gates, w_gates, b_gate.reshape(3, 1, D_MODEL), w_conv_out_t,
      w_dil_o_t, w_x_o_t, w_out_t)


def _ffn_kernel(x_ref, z_ref, xs_ref, zs_ref, gmix_ref, gpre_ref, wgu_ref, wd_ref, gpost_ref, out_ref, outs_ref,
                x1_ref, h_ref, x1s_ref, *, n_steps):
    i, s = pl.program_id(0), pl.program_id(1)
    tm, ms = x_ref.shape[0], xs_ref.shape[0]

    def prologue(x_ref, z_ref, x1_ref, row0, out_ref):
        x1 = x_ref[...] + _rms_rows(z_ref[...], gmix_ref[...])
        x1_ref[...] = x1
        h_ref[row0:row0 + x_ref.shape[0], :] = _rms_rows(x1, gpre_ref[...]).astype(BF16)
        out_ref[...] = jnp.zeros_like(out_ref)

    def step(rows):
        h = h_ref[0:rows, :]
        half = TF // 2
        part = None
        for c in range(2):
            gate_up = _dot(h, wgu_ref[:, c * TF:(c + 1) * TF])
            gate, up = gate_up[:, :half], gate_up[:, half:]
            act = (gate * jax.nn.sigmoid(gate) * up).astype(BF16)
            down = _dot(act, wd_ref[c * half:(c + 1) * half, :])
            part = down if part is None else part + down
        out_ref[...] += part[0:tm]
        if rows > tm:
            outs_ref[...] += part[tm:rows]

    def epilogue(x1_ref, out_ref):
        out_ref[...] = x1_ref[...] + _rms_rows(out_ref[...], gpost_ref[...])

    pl.when(s == 0)(lambda: prologue(x_ref, z_ref, x1_ref, 0, out_ref))
    pl.when((i == 0) & (s == 0))(lambda: prologue(xs_ref, zs_ref, x1s_ref, tm, outs_ref))
    pl.when(i == 0)(lambda: step(tm + ms))
    pl.when(i > 0)(lambda: step(tm))
    pl.when(s == n_steps - 1)(lambda: epilogue(x1_ref, out_ref))
    pl.when((i == 0) & (s == n_steps - 1))(lambda: epilogue(x1s_ref, outs_ref))


def _ffn(x, z, xs, zs, g_mix, g_pre, w_gate_up, w_down, g_post, *, tm):
    m, ms = x.shape[0], xs.shape[0]
    n_steps = D_FF // TF
    rowblk = pl.BlockSpec((tm, D_MODEL), lambda i, s: (i, 0))
    sample = pl.BlockSpec((ms, D_MODEL), lambda i, s: (0, 0))
    gain = pl.BlockSpec((1, D_MODEL), lambda i, s: (0, 0))
    return pl.pallas_call(
        functools.partial(_ffn_kernel, n_steps=n_steps),
        grid=(m // tm, n_steps),
        in_specs=[
            rowblk, rowblk, sample, sample, gain, gain,
            pl.BlockSpec((D_MODEL, 2 * TF), lambda i, s: (0, s)),
            pl.BlockSpec((TF, D_MODEL), lambda i, s: (s, 0)),
            gain,
        ],
        out_specs=[rowblk, sample],
        out_shape=[jax.ShapeDtypeStruct((m, D_MODEL), F32), jax.ShapeDtypeStruct((ms, D_MODEL), F32)],
        scratch_shapes=[pltpu.VMEM((tm, D_MODEL), F32), pltpu.VMEM((tm + ms, D_MODEL), BF16),
                        pltpu.VMEM((ms, D_MODEL), F32)],
        compiler_params=_params(("arbitrary", "arbitrary"), 54),
        name="ffn",
    )(x, z, xs, zs, g_mix, g_pre, w_gate_up, w_down, g_post)


def kernel(x_prompt, x_sample, cache_win_k, cache_win_v, state_conv, cache_mem_k, cache_mem_v, mem_prompt,
           g_pre_mix, w_in, b_gate, conv_w, conv_b, conv_ln_g, conv_ln_b, w_conv_out, w_dil_o, g_mem, w_mem_kv,
           w_x_o, w_out, g_post_mix, g_pre_ffn, w_ffn_gate, w_ffn_up, w_ffn_down, g_post_ffn):
    batch, seq, _ = x_prompt.shape
    n_seq = x_sample.shape[0]
    depth = w_in.shape[0]
    assert depth == 1 and x_sample.shape[1] == 1 and seq % (DIL_GROUPS[-1][1] * SPAN) == 0
    assert cache_win_k.shape[2] == DIL_GROUPS[-1][0]

    idx = jnp.arange(1, N_QH + 1, dtype=F32)
    slopes = jnp.exp2(-8.0 * idx / N_QH).reshape(N_GROUPS, N_KV)

    xp, xs = x_prompt.reshape(batch * seq, D_MODEL), x_sample.reshape(n_seq, D_MODEL)
    l = 0
    w_in_t = w_in[l][:, :_GATE_TILE0 * TN].astype(BF16)
    w_mem_kv_t = w_mem_kv[l].astype(BF16)
    row = lambda a: a[l].reshape(1, -1)
    g_pre, g_post, g_ffn_pre, g_ffn_post = row(g_pre_mix), row(g_post_mix), row(g_pre_ffn), row(g_post_ffn)
    cb, lng, lnb = row(conv_b), row(conv_ln_g), row(conv_ln_b)
    conv_w_slabs = conv_w[l].reshape(CONV_WIDTH, N_SLABS, LANES).transpose(1, 0, 2)

    mk, mv = _mem_kv(mem_prompt.reshape(batch * N_MEM, D_MODEL), row(g_mem), w_mem_kv_t, tm=512)
    h_p, u_p, q_p, k_p, v_p, qx_p, h_s, u_s, q_s, k_s, v_s, qx_s = _in_proj(xp, xs, g_pre, w_in_t, tm=1024)

    c_p, conv_p, k_rows, v_rows, *branch_weights = _conv_prompt(
        u_p, conv_w_slabs, cb, lng, lnb, k_p, v_p, (w_conv_out[l], w_dil_o[l], w_x_o[l], w_out[l]),
        batch=batch, seq=seq)
    mix_p, xm_p, *ffn_weights = _attn_prompt(
        slopes, q_p.reshape(batch, seq, W_QD), k_p.reshape(batch, seq, W_KV), v_p.reshape(batch, seq, W_KV),
        qx_p.reshape(batch, seq, W_QX), mk.reshape(batch, N_MEM, W_QX), mv.reshape(batch, N_MEM, W_QX),
        w_ffn_gate[l], w_ffn_up[l], w_ffn_down[l], batch=batch, seq=seq)

    c_s, conv_s = _conv_sample(state_conv[l], u_s, conv_w[l], cb, lng, lnb)
    mix_s, xm_s, w_gates = _attn_sample(slopes, q_s, k_s, v_s, qx_s, cache_win_k[l], cache_win_v[l],
                                        cache_mem_k[l], cache_mem_v[l], w_in[l])

    z_p, z_s = _merge(h_p, c_p, mix_p.reshape(batch * seq, W_KV), xm_p.reshape(batch * seq, W_QX),
                      h_s, c_s, mix_s.reshape(n_seq, W_KV), xm_s.reshape(n_seq, W_QX),
                      w_gates, b_gate[l], *branch_weights, tm=1024)
    y_p, y_s = _ffn(xp, z_p, xs, z_s, g_post, g_ffn_pre, *ffn_weights, g_ffn_post, tm=512)

    win = min(DIL_GROUPS[-1][0], seq)
    kv_shape = (1, batch, seq, N_KV, HEAD_DIM)
    mem_shape = (1, batch, N_MEM, N_XHEADS, HEAD_DIM)
    new_shape = (1, n_seq, 1, N_KV, HEAD_DIM)
    return (y_p.reshape(batch, seq, D_MODEL), y_s.reshape(n_seq, 1, D_MODEL),
            k_rows.reshape(kv_shape)[:, :, seq - win:], v_rows.reshape(kv_shape)[:, :, seq - win:], conv_p,
            mk.reshape(mem_shape), mv.reshape(mem_shape),
            k_s.reshape(new_shape), v_s.reshape(new_shape), conv_s.reshape(1, n_seq, CONV_PREFIX, C_CONV))
```

```python
import functools

import jax
import jax.numpy as jnp
from jax import lax
from jax.experimental import pallas as pl
from jax.experimental.pallas import tpu as pltpu

F32 = jnp.float32
BF16 = jnp.bfloat16

D_MODEL = 2048
C_CONV = D_MODEL // 2
CONV_WIDTH = 31
CONV_PREFIX = CONV_WIDTH - 1
HEAD_DIM = 128
N_KV = 4
DIL_GROUPS = ((128, 1), (512, 4), (2048, 16))
N_GROUPS = len(DIL_GROUPS)
N_QH = N_GROUPS * N_KV
SPAN = 128
N_MEM = 256
N_XHEADS = 4
W_QD = N_QH * HEAD_DIM
W_KV = N_KV * HEAD_DIM
W_QX = N_XHEADS * HEAD_DIM
D_FF = 5632
EPS = 1e-6
NEG = -1e30
SCALE = HEAD_DIM ** -0.5
LANES = 128
SUBLANES = 8
N_SLABS = C_CONV // LANES
MIB = 1024 * 1024

TN = 512
_A_TILES = C_CONV // TN
_Q_TILES = W_QD // TN
_KV_TILES = W_KV // TN
_QX_TILES = W_QX // TN
_STEP_Q0 = _A_TILES
_STEP_K0 = _STEP_Q0 + _Q_TILES
_STEP_V0 = _STEP_K0 + _KV_TILES
_STEP_X0 = _STEP_V0 + _KV_TILES
_IN_STEPS = _STEP_X0 + _QX_TILES
_GATE_TILE0 = 2 * _A_TILES + _Q_TILES + 2 * _KV_TILES + _QX_TILES
_D_TILES = D_MODEL // TN
_CONV_PAD = -(-CONV_PREFIX // SUBLANES) * SUBLANES
_ATTN_UNITS = 8
_RIDER_STEPS = 2
_GATE_CAST_COLS = 1024
TN_OUT = 512
TF = 512
TM_DENSE = 1024
TM_FFN = 512
TM_MEM = 512


def _params(semantics, vmem_mib):
    return pltpu.CompilerParams(dimension_semantics=semantics, vmem_limit_bytes=vmem_mib * MIB)


def _rms_rows(x, g):
    return x * lax.rsqrt(jnp.mean(x * x, axis=-1, keepdims=True) + EPS) * g


def _dot(a, b):
    return jnp.dot(a, b, preferred_element_type=F32)


def _row_ahead(i, s, free_from, n_rows):
    return jnp.minimum(i + (s >= free_from).astype(jnp.int32), n_rows - 1)


def _in_proj_kernel(x_ref, xs_ref, g_ref, w1_ref, w2_ref, h_ref, u_ref, q_ref, k_ref, v_ref, qx_ref,
                    hs_ref, tiles_ref):
    i, s = pl.program_id(0), pl.program_id(1)

    def project(on, x_ref, h_ref, store_u, q_ref, k_ref, v_ref, qx_ref):
        def norm():
            h_ref[...] = _rms_rows(x_ref[...], g_ref[...]).astype(BF16)

        def glu():
            h = h_ref[...]
            store_u(_dot(h, w1_ref[...]) * jax.nn.sigmoid(_dot(h, w2_ref[...])))

        def into(out_ref):
            def tile():
                out_ref[...] = _dot(h_ref[...], w1_ref[...]).astype(out_ref.dtype)
            return tile

        pl.when(on & (s == 0))(norm)
        pl.when(on & (s < _STEP_Q0))(glu)
        pl.when(on & (s >= _STEP_Q0) & (s < _STEP_K0))(into(q_ref))
        pl.when(on & (s >= _STEP_K0) & (s < _STEP_V0))(into(k_ref))
        pl.when(on & (s >= _STEP_V0) & (s < _STEP_X0))(into(v_ref))
        pl.when(on & (s >= _STEP_X0))(into(qx_ref))

    def store_slabs(u):
        for c in range(TN // LANES):
            u_ref[c] = u[:, c * LANES:(c + 1) * LANES]

    def store_rows(u):
        tiles_ref[...] = u

    project(True, x_ref, h_ref, store_slabs, q_ref, k_ref, v_ref, qx_ref)
    project(i == 0, xs_ref, hs_ref, store_rows, tiles_ref, tiles_ref, tiles_ref, tiles_ref)


def _in_proj(x, xs, g, w_in_t, *, tm):
    m, ms = x.shape[0], xs.shape[0]
    last_a = _A_TILES - 1

    def out_tiles(step0, n_tiles):
        def index(i, s):
            moved = (s >= step0 + n_tiles) & (i < m // tm - 1)
            return (i + moved.astype(jnp.int32), jnp.where(moved, 0, jnp.clip(s - step0, 0, n_tiles - 1)))
        return pl.BlockSpec((tm, TN), index)

    outs = pl.pallas_call(
        _in_proj_kernel,
        grid=(m // tm, _IN_STEPS),
        in_specs=[
            pl.BlockSpec((tm, D_MODEL), lambda i, s: (_row_ahead(i, s, _A_TILES, m // tm), 0)),
            pl.BlockSpec((ms, D_MODEL), lambda i, s: (0, 0)),
            pl.BlockSpec((1, D_MODEL), lambda i, s: (0, 0)),
            pl.BlockSpec((D_MODEL, TN), lambda i, s: (0, jnp.where(s < _A_TILES, s, s + _A_TILES))),
            pl.BlockSpec((D_MODEL, TN), lambda i, s: (0, jnp.where(s < _A_TILES, s, 0) + _A_TILES)),
        ],
        out_specs=[
            pl.BlockSpec((tm, D_MODEL), lambda i, s: (i, 0)),
            pl.BlockSpec((TN // LANES, tm, LANES), lambda i, s: (jnp.minimum(s, last_a), i, 0)),
            out_tiles(_STEP_Q0, _Q_TILES),
            out_tiles(_STEP_K0, _KV_TILES),
            out_tiles(_STEP_V0, _KV_TILES),
            out_tiles(_STEP_X0, _QX_TILES),
            pl.BlockSpec((ms, D_MODEL), lambda i, s: (0, 0)),
            pl.BlockSpec((ms, TN), lambda i, s: (0, jnp.where(i == 0, s, _IN_STEPS - 1))),
        ],
        out_shape=[
            jax.ShapeDtypeStruct((m, D_MODEL), BF16),
            jax.ShapeDtypeStruct((N_SLABS, m, LANES), F32),
            jax.ShapeDtypeStruct((m, W_QD), F32),
            jax.ShapeDtypeStruct((m, W_KV), F32),
            jax.ShapeDtypeStruct((m, W_KV), F32),
            jax.ShapeDtypeStruct((m, W_QX), BF16),
            jax.ShapeDtypeStruct((ms, D_MODEL), BF16),
            jax.ShapeDtypeStruct((ms, _IN_STEPS * TN), F32),
        ],
        compiler_params=_params(("arbitrary", "arbitrary"), 60),
        name="in_proj",
    )(x, xs, g, w_in_t, w_in_t)
    *prompt, h_s, tiles = outs
    bounds = [step * TN for step in (0, _STEP_Q0, _STEP_K0, _STEP_V0, _STEP_X0, _IN_STEPS)]
    return (*prompt, h_s, *(tiles[:, lo:hi] for lo, hi in zip(bounds, bounds[1:])))


def _mem_kv_kernel(x_ref, g_ref, w_ref, mk_ref, mv_ref, h_ref):
    s = pl.program_id(1)

    @pl.when(s == 0)
    def _():
        h_ref[...] = _rms_rows(x_ref[...], g_ref[...]).astype(BF16)
        mk_ref[...] = _dot(h_ref[...], w_ref[...])

    @pl.when(s == 1)
    def _():
        mv_ref[...] = _dot(h_ref[...], w_ref[...])


def _mem_kv(mem, g, w_t, *, tm):
    m = mem.shape[0]
    return pl.pallas_call(
        _mem_kv_kernel,
        grid=(m // tm, 2),
        in_specs=[
            pl.BlockSpec((tm, D_MODEL), lambda i, s: (i, 0)),
            pl.BlockSpec((1, D_MODEL), lambda i, s: (0, 0)),
            pl.BlockSpec((D_MODEL, W_QX), lambda i, s: (0, s)),
        ],
        out_specs=[
            pl.BlockSpec((tm, W_QX), lambda i, s: (i, 0)),
            pl.BlockSpec((tm, W_QX), lambda i, s: (i, 0)),
        ],
        out_shape=[jax.ShapeDtypeStruct((m, W_QX), F32)] * 2,
        scratch_shapes=[pltpu.VMEM((tm, D_MODEL), BF16)],
        compiler_params=_params(("parallel", "arbitrary"), 32),
        name="mem_kv",
    )(mem, g, w_t)


def _ln_swish_slabs(y_slabs, cb_ref, lng_ref, lnb_ref, out_dtype):
    ys = [y + cb_ref[:, c * LANES:(c + 1) * LANES] for c, y in enumerate(y_slabs)]
    tot = ys[0]
    for y in ys[1:]:
        tot = tot + y
    mu = jnp.sum(tot, axis=-1, keepdims=True) * (1.0 / C_CONV)
    ds = [y - mu for y in ys]
    sq = ds[0] * ds[0]
    for d in ds[1:]:
        sq = sq + d * d
    rs = lax.rsqrt(jnp.sum(sq, axis=-1, keepdims=True) * (1.0 / C_CONV) + EPS)
    outs = []
    for c, d in enumerate(ds):
        z = d * rs * lng_ref[:, c * LANES:(c + 1) * LANES] + lnb_ref[:, c * LANES:(c + 1) * LANES]
        outs.append((z * jax.nn.sigmoid(z)).astype(out_dtype))
    return outs


def _conv_prompt_kernel(u_ref, w_ref, cb_ref, lng_ref, lnb_ref, k_ref, v_ref, *rest, seq, tt, n_cast):
    cast_in, (c_ref, st_ref, kout_ref, vout_ref) = rest[:n_cast], rest[n_cast:n_cast + 4]
    cast_out, (head_ref, y_ref) = rest[n_cast + 4:2 * n_cast + 4], rest[2 * n_cast + 4:]
    half = pl.program_id(1)
    n_tiles = seq // tt // 2
    pad = _CONV_PAD

    for src_ref, dst_ref in ((k_ref, kout_ref), (v_ref, vout_ref)):
        for h in range(N_KV):
            dst_ref[pl.ds(h, src_ref.shape[0], stride=N_KV), :] = src_ref[:, h * LANES:(h + 1) * LANES]
    for src_ref, dst_ref in zip(cast_in, cast_out):
        dst_ref[...] = src_ref[...].astype(BF16)

    def tile(src_ref, base, t0):
        def slab(c, carry):
            acc = jnp.zeros((tt, LANES), F32)
            for j in range(SUBLANES):
                taps = range(j, CONV_WIDTH, SUBLANES)
                win = src_ref[c, pl.ds(base + j, tt + taps[-1] - j), :]
                for k in taps:
                    acc = acc + win[k - j:k - j + tt] * w_ref[c, k:k + 1, :]
            y_ref[c] = acc
            return carry

        lax.fori_loop(0, N_SLABS, slab, 0)
        ys = [y_ref[c] for c in range(N_SLABS)]
        outs = _ln_swish_slabs(ys, cb_ref, lng_ref, lnb_ref, BF16)
        for c in range(N_SLABS):
            c_ref[pl.ds(t0, tt), c * LANES:(c + 1) * LANES] = outs[c]

    @pl.when(half == 0)
    def _():
        head_ref[:, 0:pad, :] = jnp.zeros((N_SLABS, pad, LANES), F32)
        head_ref[:, pad:pad + tt, :] = u_ref[:, 0:tt, :]
        tile(head_ref, pad - CONV_PREFIX, 0)

    @pl.when(half > 0)
    def _():
        tile(u_ref, pl.multiple_of(half * (n_tiles * tt), tt) - CONV_PREFIX, 0)

    def body(i, carry):
        t0 = pl.multiple_of((half * n_tiles + i) * tt, tt)
        tile(u_ref, t0 - CONV_PREFIX, pl.multiple_of(i * tt, tt))
        return carry

    lax.fori_loop(1, n_tiles, body, 0)
    for c in range(N_SLABS):
        st_ref[:, c * LANES:(c + 1) * LANES] = u_ref[c, seq - CONV_PREFIX:seq, :]


def _conv_prompt(u_slabs, w_slabs, conv_b, ln_g, ln_b, k, v, cast_weights, *, batch, seq):
    tt = 128
    halves = 2
    n_steps = batch * halves
    step = lambda b, j: b * halves + j
    const = lambda *idx: (lambda b, j: idx)

    def row_slice(rows, cols):
        return pl.BlockSpec((rows // n_steps, cols), lambda b, j: (step(b, j), 0))

    kv_in = row_slice(batch * seq, W_KV)
    kv_out = row_slice(batch * seq * N_KV, HEAD_DIM)
    w_specs = [row_slice(*w.shape) for w in cast_weights]
    return pl.pallas_call(
        functools.partial(_conv_prompt_kernel, seq=seq, tt=tt, n_cast=len(cast_weights)),
        grid=(batch, halves),
        in_specs=[
            pl.BlockSpec((N_SLABS, seq, LANES), lambda b, j: (0, b, 0)),
            pl.BlockSpec((N_SLABS, CONV_WIDTH, LANES), const(0, 0, 0)),
            pl.BlockSpec((1, C_CONV), const(0, 0)),
            pl.BlockSpec((1, C_CONV), const(0, 0)),
            pl.BlockSpec((1, C_CONV), const(0, 0)),
            kv_in, kv_in,
        ] + w_specs,
        out_specs=[
            pl.BlockSpec((seq // halves, C_CONV), lambda b, j: (step(b, j), 0)),
            pl.BlockSpec((None, None, CONV_PREFIX, C_CONV), lambda b, j: (0, b, 0, 0)),
            kv_out, kv_out,
        ] + w_specs,
        out_shape=[
            jax.ShapeDtypeStruct((batch * seq, C_CONV), BF16),
            jax.ShapeDtypeStruct((1, batch, CONV_PREFIX, C_CONV), F32),
            jax.ShapeDtypeStruct((batch * seq * N_KV, HEAD_DIM), F32),
            jax.ShapeDtypeStruct((batch * seq * N_KV, HEAD_DIM), F32),
        ] + [jax.ShapeDtypeStruct(w.shape, BF16) for w in cast_weights],
        scratch_shapes=[pltpu.VMEM((N_SLABS, _CONV_PAD + tt, LANES), F32), pltpu.VMEM((N_SLABS, tt, LANES), F32)],
        compiler_params=_params(("parallel", "arbitrary"), 56),
        name="conv_prompt",
    )(u_slabs, w_slabs, conv_b, ln_g, ln_b, k, v, *cast_weights)


def _conv_sample_kernel(state_ref, u_ref, w_ref, cb_ref, lng_ref, lnb_ref, c_ref, st_ref, *, n_seq):
    w_hist = w_ref[0:CONV_PREFIX, :]
    w_new = w_ref[CONV_PREFIX:CONV_WIDTH, :]
    for n in range(n_seq):
        hist = state_ref[n]
        new = u_ref[n:n + 1, :]
        y = jnp.sum(hist * w_hist, axis=0, keepdims=True) + new * w_new
        ys = [y[:, c * LANES:(c + 1) * LANES] for c in range(N_SLABS)]
        outs = _ln_swish_slabs(ys, cb_ref, lng_ref, lnb_ref, F32)
        for c in range(N_SLABS):
            c_ref[n:n + 1, c * LANES:(c + 1) * LANES] = outs[c]
        st_ref[n, 0:CONV_PREFIX - 1, :] = state_ref[n, 1:CONV_PREFIX, :]
        st_ref[n, CONV_PREFIX - 1:CONV_PREFIX, :] = new


def _conv_sample(state, u, conv_w, conv_b, ln_g, ln_b):
    n_seq = state.shape[0]
    return pl.pallas_call(
        functools.partial(_conv_sample_kernel, n_seq=n_seq),
        out_shape=[
            jax.ShapeDtypeStruct((n_seq, C_CONV), F32),
            jax.ShapeDtypeStruct((n_seq, CONV_PREFIX, C_CONV), F32),
        ],
        compiler_params=pltpu.CompilerParams(vmem_limit_bytes=32 * MIB),
        name="conv_sample",
    )(state, u, conv_w, conv_b, ln_g, ln_b)


def _qk(q, k):
    return lax.dot_general(q, k, (((1,), (1,)), ((), ())), preferred_element_type=F32)


def _attn_prompt_kernel(slopes_ref, q0_ref, q1_ref, q2_ref, k_ref, v_ref, qx_ref, mk_ref, mv_ref,
                        wg_ref, wu_ref, mix_ref, xm_ref, wgu_out_ref,
                        bias_ref, o_ref, lse_ref, s_ref, p_ref, m_ref, ones_ref, *, seq):
    head = pl.program_id(1)
    ones_ref[...] = jnp.ones(ones_ref.shape, BF16)
    half = TF // 2
    for j in range(D_FF // half):
        cols = slice(j * half, (j + 1) * half)
        wgu_out_ref[:, 2 * j * half:(2 * j + 1) * half] = wg_ref[:, cols].astype(BF16)
        wgu_out_ref[:, (2 * j + 1) * half:(2 * j + 2) * half] = wu_ref[:, cols].astype(BF16)
    qi = lax.broadcasted_iota(jnp.int32, (SPAN, 2 * SPAN), 0)
    ci = lax.broadcasted_iota(jnp.int32, (SPAN, 2 * SPAN), 1)
    dist = qi + SPAN - ci
    valid = (dist >= 0) & (dist <= SPAN)
    for g, (_, dil) in enumerate(DIL_GROUPS):
        bias = -slopes_ref[g, head] * (dist * dil).astype(F32)
        bias_ref[g] = jnp.where(valid, bias, NEG)

    def rows(ref, start, size, dil):
        if dil == 1:
            return ref[pl.ds(start, size), :]
        return ref[pl.ds(start, size, stride=dil), :]

    def put(ref, g, start, dil, val):
        if dil == 1:
            ref[g, pl.ds(start, SPAN), :] = val
        else:
            ref[g, pl.ds(start, SPAN, stride=dil), :] = val

    def staged(items, n_keys_of, scores, values, emit):
        for u, item in enumerate(items):
            s_ref[u, :, 0:n_keys_of(item)] = scores(item)
        for u, item in enumerate(items):
            s = s_ref[u, :, 0:n_keys_of(item)]
            m = jnp.max(s, axis=-1, keepdims=True)
            p_ref[u, :, 0:n_keys_of(item)] = jnp.exp(s - m).astype(BF16)
            m_ref[u] = jnp.broadcast_to(m, (SPAN, LANES))
        for u, item in enumerate(items):
            p = p_ref[u, :, 0:n_keys_of(item)]
            l = _dot(p, ones_ref[0:n_keys_of(item), :])
            emit(item, _dot(p, values(item)) * (1.0 / l), m_ref[u] + jnp.log(l))

    def units(g, dil, q_ref, starts_keys):
        def scores(item):
            start, n_keys = item
            q = rows(q_ref, start, SPAN, dil).astype(BF16)
            k = rows(k_ref, start - (n_keys - SPAN) * dil, n_keys, dil).astype(BF16)
            return _qk(q, k) * SCALE + bias_ref[g, :, 2 * SPAN - n_keys:2 * SPAN]

        def values(item):
            start, n_keys = item
            return rows(v_ref, start - (n_keys - SPAN) * dil, n_keys, dil).astype(BF16)

        def emit(item, o, lse):
            put(o_ref, g, item[0], dil, o)
            put(lse_ref, g, item[0], dil, lse)

        staged(starts_keys, lambda item: item[1], scores, values, emit)

    q_refs = (q0_ref, q1_ref, q2_ref)
    for g, (_, dil) in enumerate(DIL_GROUPS):
        n_blocks = seq // dil // SPAN
        q_ref = q_refs[g]
        if n_blocks == 1:
            per_body = _ATTN_UNITS

            def first_blocks(j, carry, g=g, dil=dil, q_ref=q_ref):
                units(g, dil, q_ref, [(j * per_body + u, SPAN) for u in range(per_body)])
                return carry
            lax.fori_loop(0, dil // per_body, first_blocks, 0)
        elif dil == 1:
            per_body = _ATTN_UNITS

            def later_blocks(j, carry, g=g, dil=dil, q_ref=q_ref):
                units(g, dil, q_ref, [(pl.multiple_of((j * per_body + u) * SPAN, SPAN), 2 * SPAN)
                                      for u in range(per_body)])
                return carry
            units(g, dil, q_ref, [(0, SPAN)] + [(n * SPAN, 2 * SPAN) for n in range(1, per_body)])
            lax.fori_loop(1, n_blocks // per_body, later_blocks, 0)
        else:
            streams = _ATTN_UNITS // n_blocks

            def residue(j, carry, g=g, dil=dil, q_ref=q_ref, n_blocks=n_blocks, streams=streams):
                work = []
                for r in range(streams):
                    start = j * streams + r
                    work.append((start, SPAN))
                    work += [(start + n * (SPAN * dil), 2 * SPAN) for n in range(1, n_blocks)]
                units(g, dil, q_ref, work)
                return carry
            lax.fori_loop(0, dil // streams, residue, 0)

    tq = 256

    def mix(j, carry):
        t0 = pl.multiple_of(j * tq, tq)
        l0 = lse_ref[0, pl.ds(t0, tq), :]
        l1 = lse_ref[1, pl.ds(t0, tq), :]
        l2 = lse_ref[2, pl.ds(t0, tq), :]
        top = jnp.maximum(jnp.maximum(l0, l1), l2)
        w0 = jnp.exp(l0 - top)
        w1 = jnp.exp(l1 - top)
        w2 = jnp.exp(l2 - top)
        num = (w0 * o_ref[0, pl.ds(t0, tq), :] + w1 * o_ref[1, pl.ds(t0, tq), :]
               + w2 * o_ref[2, pl.ds(t0, tq), :])
        mix_ref[pl.ds(t0, tq), :] = (num / (w0 + w1 + w2)).astype(BF16)
        return carry

    lax.fori_loop(0, seq // tq, mix, 0, unroll=2)

    def memory(j, carry):
        def scores(t0):
            return _qk(qx_ref[pl.ds(t0, SPAN), :], mk_ref[...].astype(BF16)) * SCALE

        def emit(t0, o, lse):
            xm_ref[pl.ds(t0, SPAN), :] = o.astype(BF16)

        staged([pl.multiple_of((j * _ATTN_UNITS + u) * SPAN, SPAN) for u in range(_ATTN_UNITS)],
               lambda t0: N_MEM, scores, lambda t0: mv_ref[...].astype(BF16), emit)
        return carry

    lax.fori_loop(0, seq // SPAN // _ATTN_UNITS, memory, 0)


def _attn_prompt(slopes, q, k, v, qx, mk, mv, w_gate, w_up, *, batch, seq):
    def col(c):
        return pl.BlockSpec((None, seq, LANES), lambda b, h, c=c: (b, 0, c * N_KV + h))

    head_spec = pl.BlockSpec((None, seq, LANES), lambda b, h: (b, 0, h))
    mem_spec = pl.BlockSpec((None, N_MEM, LANES), lambda b, h: (b, 0, h))
    n_steps = batch * N_KV

    def row_slice(rows, cols):
        return pl.BlockSpec((rows // n_steps, cols), lambda b, h: (b * N_KV + h, 0))

    weights = (w_gate, w_up)
    gu_shape = (D_MODEL, 2 * D_FF)
    return pl.pallas_call(
        functools.partial(_attn_prompt_kernel, seq=seq),
        grid=(batch, N_KV),
        in_specs=[
            pl.BlockSpec(memory_space=pltpu.SMEM),
            col(0), col(1), col(2), head_spec, head_spec, head_spec, mem_spec, mem_spec,
        ] + [row_slice(*w.shape) for w in weights],
        out_specs=[head_spec, head_spec, row_slice(*gu_shape)],
        out_shape=[jax.ShapeDtypeStruct((batch, seq, W_KV), BF16),
                   jax.ShapeDtypeStruct((batch, seq, W_QX), BF16),
                   jax.ShapeDtypeStruct(gu_shape, BF16)],
        scratch_shapes=[
            pltpu.VMEM((N_GROUPS, SPAN, 2 * SPAN), F32),
            pltpu.VMEM((N_GROUPS, seq, LANES), F32),
            pltpu.VMEM((N_GROUPS, seq, LANES), F32),
            pltpu.VMEM((_ATTN_UNITS, SPAN, 2 * SPAN), F32),
            pltpu.VMEM((_ATTN_UNITS, SPAN, 2 * SPAN), BF16),
            pltpu.VMEM((_ATTN_UNITS, SPAN, LANES), F32),
            pltpu.VMEM((2 * SPAN, LANES), BF16),
        ],
        compiler_params=_params(("parallel", "parallel"), 56),
        name="attn_prompt",
    )(slopes, q, q, q, k, v, qx, mk, mv, *weights)


def _attn_sample_kernel(slopes_ref, q_ref, kn_ref, vn_ref, qx_ref, ck0_ref, ck1_ref, ck2_ref,
                        cv0_ref, cv1_ref, cv2_ref, mk_ref, mv_ref, *rest, bs):
    wcol_refs, (mix_ref, xm_ref, wgate_ref) = rest[:-3], rest[-3:]

    def by_row(ref):
        return ref if len(ref.shape) == 3 else ref.reshape(ref.shape[0], ref.shape[1] * ref.shape[2], LANES)

    def head_rows(ref, n, h, n_keys):
        return ref[n, pl.ds(h, n_keys, stride=ref.shape[1] // n_keys), :]

    ck_refs = tuple(by_row(r) for r in (ck0_ref, ck1_ref, ck2_ref))
    cv_refs = tuple(by_row(r) for r in (cv0_ref, cv1_ref, cv2_ref))
    back =(SPAN - lax.broadcasted_iota(jnp.int32, (SPAN, 1), 0)).astype(F32)
    for c, wcol_ref in enumerate(wcol_refs):
        wgate_ref[:, c * _GATE_CAST_COLS:(c + 1) * _GATE_CAST_COLS] = wcol_ref[...].astype(BF16)

    def one(n, carry):
        for h in range(N_KV):
            hs = slice(h * LANES, (h + 1) * LANES)
            kn = kn_ref[n, :, hs]
            vn = vn_ref[n, :, hs]
            scores, s_new = [], []
            for g, (_, dil) in enumerate(DIL_GROUPS):
                qs = slice(g * W_KV + h * LANES, g * W_KV + (h + 1) * LANES)
                q = q_ref[n, :, qs]
                s = jnp.sum(head_rows(ck_refs[g], n, h, SPAN) * q, axis=-1, keepdims=True) * SCALE
                scores.append(s - slopes_ref[g, h] * (back * float(dil)))
                s_new.append(jnp.sum(kn * q, axis=-1, keepdims=True) * SCALE)
            top = s_new[0]
            for g in range(N_GROUPS):
                top = jnp.maximum(top, jnp.maximum(s_new[g], jnp.max(scores[g], axis=0, keepdims=True)))
            num = jnp.zeros((1, LANES), F32)
            den = jnp.zeros((1, 1), F32)
            for g in range(N_GROUPS):
                p = jnp.exp(scores[g] - top)
                p_new = jnp.exp(s_new[g] - top)
                num = num + jnp.sum(p * head_rows(cv_refs[g], n, h, SPAN), axis=0, keepdims=True) + p_new * vn
                den = den + jnp.sum(p, axis=0, keepdims=True) + p_new
            mix_ref[n, :, hs] = num / den
            qx = qx_ref[n, :, hs]
            sx = jnp.sum(head_rows(mk_ref, n, h, N_MEM) * qx, axis=-1, keepdims=True) * SCALE
            px = jnp.exp(sx - jnp.max(sx, axis=0, keepdims=True))
            ox = (jnp.sum(px * head_rows(mv_ref, n, h, N_MEM), axis=0, keepdims=True)
                  / jnp.sum(px, axis=0, keepdims=True))
            xm_ref[n, :, hs] = ox
        return carry

    lax.fori_loop(0, bs, one, 0)


def _attn_sample(slopes, q, kn, vn, qx, cache_k, cache_v, mem_k, mem_v, w_in):
    n_seq, win = cache_k.shape[0], cache_k.shape[1]
    bs = 2
    n_steps = n_seq // bs
    wrows = w_in.shape[0] // n_steps
    gate_col0 = _GATE_TILE0 * TN // _GATE_CAST_COLS
    wcol_specs = [pl.BlockSpec((wrows, _GATE_CAST_COLS), lambda i, c=c: (i, gate_col0 + c))
                  for c in range(3 * D_MODEL // _GATE_CAST_COLS)]
    row = lambda w: pl.BlockSpec((bs, 1, w), lambda i: (i, 0, 0))
    as_rows = lambda a: a.reshape(n_seq, 1, a.shape[-1])
    views_k, views_v, cache_specs = [], [], []
    for _, dil in DIL_GROUPS:
        nb = win // dil // SPAN
        if dil == 1:
            shape = (n_seq, win * N_KV, HEAD_DIM)
            cache_specs.append(pl.BlockSpec((bs, SPAN * N_KV, HEAD_DIM), lambda i, nb=nb: (i, nb - 1, 0)))
        else:
            shape = (n_seq, win // dil, dil * N_KV, HEAD_DIM)
            cache_specs.append(pl.BlockSpec((bs, SPAN, SUBLANES, HEAD_DIM), lambda i, nb=nb: (i, nb - 1, 0, 0)))
        views_k.append(cache_k.reshape(shape))
        views_v.append(cache_v.reshape(shape))
    mem_spec = pl.BlockSpec((bs, N_MEM * N_XHEADS, HEAD_DIM), lambda i: (i, 0, 0))
    mem_rows = lambda a: a.reshape(n_seq, N_MEM * N_XHEADS, HEAD_DIM)
    return pl.pallas_call(
        functools.partial(_attn_sample_kernel, bs=bs),
        grid=(n_seq // bs,),
        in_specs=[pl.BlockSpec(memory_space=pltpu.SMEM), row(W_QD), row(W_KV), row(W_KV), row(W_QX)]
        + cache_specs + cache_specs + [mem_spec, mem_spec] + wcol_specs,
        out_specs=[row(W_KV), row(W_QX), pl.BlockSpec((wrows, 3 * D_MODEL), lambda i: (i, 0))],
        out_shape=[jax.ShapeDtypeStruct((n_seq, 1, W_KV), F32), jax.ShapeDtypeStruct((n_seq, 1, W_QX), F32),
                   jax.ShapeDtypeStruct((w_in.shape[0], 3 * D_MODEL), BF16)],
        compiler_params=_params(("parallel",), 48),
        name="attn_sample",
    )(slopes, as_rows(q), as_rows(kn), as_rows(vn), as_rows(qx), *views_k, *views_v, mem_rows(mem_k),
      mem_rows(mem_v), *([w_in] * len(wcol_specs)))


def _merge_kernel(h_ref, c_ref, mix_ref, xm_ref, hs_ref, cs_ref, mixs_ref, xms_ref, wg0_ref, wg1_ref, wg2_ref,
                  bg_ref, wc_ref, wd_ref, wx_ref, wo_ref, wffn_ref, z_ref, zs_ref, wffn_out_ref,
                  merged_ref, mergeds_ref):
    i, s = pl.program_id(0), pl.program_id(1)

    @pl.when(s % _RIDER_STEPS == 0)
    def _():
        wffn_out_ref[...] = wffn_ref[...].astype(BF16)

    def gated_sum(h_ref, branches, merged_ref):
        h = h_ref[...]
        acc = None
        for br, (wg_ref, y_ref, wy_ref) in enumerate(zip((wg0_ref, wg1_ref, wg2_ref), branches,
                                                         (wc_ref, wd_ref, wx_ref))):
            gate = jax.nn.sigmoid(_dot(h, wg_ref[...]) + bg_ref[br])
            term = gate * _dot(y_ref[...].astype(BF16), wy_ref[...])
            acc = term if acc is None else acc + term
        merged_ref[s] = acc.astype(BF16)

    def project(merged_ref, z_ref):
        z = None
        for j in range(_D_TILES):
            part = _dot(merged_ref[j], wo_ref[j * TN:(j + 1) * TN, :])
            z = part if z is None else z + part
        z_ref[...] = z

    first, second = s < _D_TILES, s >= _D_TILES
    pl.when(first)(lambda: gated_sum(h_ref, (c_ref, mix_ref, xm_ref), merged_ref))
    pl.when(second)(lambda: project(merged_ref, z_ref))
    pl.when(first & (i == 0))(lambda: gated_sum(hs_ref, (cs_ref, mixs_ref, xms_ref), mergeds_ref))
    pl.when(second & (i == 0))(lambda: project(mergeds_ref, zs_ref))


def _merge(h, c, mix, xm, hs, cs, mixs, xms, w_gates, b_gate, w_conv_out_t, w_dil_o_t, w_x_o_t, w_out_t, w_ffn,
           *, tm):
    m, ms = h.shape[0], hs.shape[0]
    n1, n2 = _D_TILES, D_MODEL // TN_OUT

    n_rows = m // tm
    whole = lambda a: pl.BlockSpec(a.shape, lambda i, s: (0, 0))
    rider_blocks = n_rows * (n1 + n2) // _RIDER_STEPS
    rider = pl.BlockSpec((w_ffn.shape[0] // rider_blocks, w_ffn.shape[1]),
                         lambda i, s: ((i * (n1 + n2) + s) // _RIDER_STEPS, 0))

    def tile1(s):
        return jnp.where(s < n1, s, 0)

    def gate_spec(br):
        return pl.BlockSpec((D_MODEL, TN), lambda i, s, br=br: (0, br * _D_TILES + tile1(s)))

    def colw(kdim):
        return pl.BlockSpec((kdim, TN), lambda i, s: (0, tile1(s)))

    def rowblk(w):
        return pl.BlockSpec((tm, w), lambda i, s: (_row_ahead(i, s, n1, n_rows), 0))

    return pl.pallas_call(
        _merge_kernel,
        grid=(m // tm, n1 + n2),
        in_specs=[
            rowblk(D_MODEL), rowblk(C_CONV), rowblk(W_KV), rowblk(W_QX),
            whole(hs), whole(cs), whole(mixs), whole(xms),
            gate_spec(0), gate_spec(1), gate_spec(2),
            pl.BlockSpec((3, 1, TN), lambda i, s: (0, 0, tile1(s))),
            colw(C_CONV), colw(W_KV), colw(W_QX),
            pl.BlockSpec((D_MODEL, TN_OUT), lambda i, s: (0, jnp.maximum(s - n1, 0))),
            rider,
        ],
        out_specs=[
            pl.BlockSpec((tm, TN_OUT), lambda i, s: (i, jnp.maximum(s - n1, 0))),
            pl.BlockSpec((ms, TN_OUT), lambda i, s: (0, jnp.where(i == 0, jnp.maximum(s - n1, 0), n2 - 1))),
            rider,
        ],
        out_shape=[jax.ShapeDtypeStruct((m, D_MODEL), F32), jax.ShapeDtypeStruct((ms, D_MODEL), F32),
                   jax.ShapeDtypeStruct(w_ffn.shape, BF16)],
        scratch_shapes=[pltpu.VMEM((n1, tm, TN), BF16), pltpu.VMEM((n1, ms, TN), BF16)],
        compiler_params=_params(("arbitrary", "arbitrary"), 60),
        name="merge",
    )(h, c, mix, xm, hs, cs, mixs, xms, w_gates, w_gates, w_gates, b_gate.reshape(3, 1, D_MODEL), w_conv_out_t,
      w_dil_o_t, w_x_o_t, w_out_t, w_ffn)


def _ffn_kernel(x_ref, z_ref, xs_ref, zs_ref, gmix_ref, gpre_ref, wgu_ref, wd_ref, gpost_ref, out_ref, outs_ref,
                x1_ref, h_ref, x1s_ref, *, n_steps):
    i, s = pl.program_id(0), pl.program_id(1)
    tm, ms = x_ref.shape[0], xs_ref.shape[0]

    def prologue(x_ref, z_ref, x1_ref, row0, out_ref):
        x1 = x_ref[...] + _rms_rows(z_ref[...], gmix_ref[...])
        x1_ref[...] = x1
        h_ref[row0:row0 + x_ref.shape[0], :] = _rms_rows(x1, gpre_ref[...]).astype(BF16)
        out_ref[...] = jnp.zeros_like(out_ref)

    def step(rows):
        h = h_ref[0:rows, :]
        half = TF // 2
        part = None
        for c in range(2):
            gate_up = _dot(h, wgu_ref[:, c * TF:(c + 1) * TF])
            gate, up = gate_up[:, :half], gate_up[:, half:]
            act = (gate * jax.nn.sigmoid(gate) * up).astype(BF16)
            down = _dot(act, wd_ref[c * half:(c + 1) * half, :])
            part = down if part is None else part + down
        out_ref[...] += part[0:tm]
        if rows > tm:
            outs_ref[...] += part[tm:rows]

    def epilogue(x1_ref, out_ref):
        out_ref[...] = x1_ref[...] + _rms_rows(out_ref[...], gpost_ref[...])

    pl.when(s == 0)(lambda: prologue(x_ref, z_ref, x1_ref, 0, out_ref))
    pl.when((i == 0) & (s == 0))(lambda: prologue(xs_ref, zs_ref, x1s_ref, tm, outs_ref))
    pl.when(i == 0)(lambda: step(tm + ms))
    pl.when(i > 0)(lambda: step(tm))
    pl.when(s == n_steps - 1)(lambda: epilogue(x1_ref, out_ref))
    pl.when((i == 0) & (s == n_steps - 1))(lambda: epilogue(x1s_ref, outs_ref))


def _ffn(x, z, xs, zs, g_mix, g_pre, w_gate_up, w_down, g_post, *, tm):
    m, ms = x.shape[0], xs.shape[0]
    n_steps = D_FF // TF
    rowblk = pl.BlockSpec((tm, D_MODEL), lambda i, s: (i, 0))
    sample = pl.BlockSpec((ms, D_MODEL), lambda i, s: (0, 0))
    gain = pl.BlockSpec((1, D_MODEL), lambda i, s: (0, 0))
    return pl.pallas_call(
        functools.partial(_ffn_kernel, n_steps=n_steps),
        grid=(m // tm, n_steps),
        in_specs=[
            rowblk, rowblk, sample, sample, gain, gain,
            pl.BlockSpec((D_MODEL, 2 * TF), lambda i, s: (0, s)),
            pl.BlockSpec((TF, D_MODEL), lambda i, s: (s, 0)),
            gain,
        ],
        out_specs=[rowblk, sample],
        out_shape=[jax.ShapeDtypeStruct((m, D_MODEL), F32), jax.ShapeDtypeStruct((ms, D_MODEL), F32)],
        scratch_shapes=[pltpu.VMEM((tm, D_MODEL), F32), pltpu.VMEM((tm + ms, D_MODEL), BF16),
                        pltpu.VMEM((ms, D_MODEL), F32)],
        compiler_params=_params(("arbitrary", "arbitrary"), 54),
        name="ffn",
    )(x, z, xs, zs, g_mix, g_pre, w_gate_up, w_down, g_post)


def kernel(x_prompt, x_sample, cache_win_k, cache_win_v, state_conv, cache_mem_k, cache_mem_v, mem_prompt,
           g_pre_mix, w_in, b_gate, conv_w, conv_b, conv_ln_g, conv_ln_b, w_conv_out, w_dil_o, g_mem, w_mem_kv,
           w_x_o, w_out, g_post_mix, g_pre_ffn, w_ffn_gate, w_ffn_up, w_ffn_down, g_post_ffn):
    batch, seq, _ = x_prompt.shape
    n_seq = x_sample.shape[0]
    depth = w_in.shape[0]
    assert depth == 1 and x_sample.shape[1] == 1 and seq % (DIL_GROUPS[-1][1] * SPAN) == 0
    assert cache_win_k.shape[2] == DIL_GROUPS[-1][0]

    idx = jnp.arange(1, N_QH + 1, dtype=F32)
    slopes = jnp.exp2(-8.0 * idx / N_QH).reshape(N_GROUPS, N_KV)

    xp, xs = x_prompt.reshape(batch * seq, D_MODEL), x_sample.reshape(n_seq, D_MODEL)
    l = 0
    w_in_t = w_in[l][:, :_GATE_TILE0 * TN].astype(BF16)
    w_mem_kv_t = w_mem_kv[l].astype(BF16)
    row = lambda a: a[l].reshape(1, -1)
    g_pre, g_post, g_ffn_pre, g_ffn_post = row(g_pre_mix), row(g_post_mix), row(g_pre_ffn), row(g_post_ffn)
    cb, lng, lnb = row(conv_b), row(conv_ln_g), row(conv_ln_b)
    conv_w_slabs = conv_w[l].reshape(CONV_WIDTH, N_SLABS, LANES).transpose(1, 0, 2)

    mk, mv = _mem_kv(mem_prompt.reshape(batch * N_MEM, D_MODEL), row(g_mem), w_mem_kv_t, tm=TM_MEM)
    h_p, u_p, q_p, k_p, v_p, qx_p, h_s, u_s, q_s, k_s, v_s, qx_s = _in_proj(xp, xs, g_pre, w_in_t, tm=TM_DENSE)

    c_p, conv_p, k_rows, v_rows, *branch_weights = _conv_prompt(
        u_p, conv_w_slabs, cb, lng, lnb, k_p, v_p, (w_conv_out[l], w_dil_o[l], w_x_o[l], w_out[l]),
        batch=batch, seq=seq)
    mix_p, xm_p, w_gate_up = _attn_prompt(
        slopes, q_p.reshape(batch, seq, W_QD), k_p.reshape(batch, seq, W_KV), v_p.reshape(batch, seq, W_KV),
        qx_p.reshape(batch, seq, W_QX), mk.reshape(batch, N_MEM, W_QX), mv.reshape(batch, N_MEM, W_QX),
        w_ffn_gate[l], w_ffn_up[l], batch=batch, seq=seq)

    c_s, conv_s = _conv_sample(state_conv[l], u_s, conv_w[l], cb, lng, lnb)
    mix_s, xm_s, w_gates = _attn_sample(slopes, q_s, k_s, v_s, qx_s, cache_win_k[l], cache_win_v[l],
                                        cache_mem_k[l], cache_mem_v[l], w_in[l])

    z_p, z_s, w_down = _merge(h_p, c_p, mix_p.reshape(batch * seq, W_KV), xm_p.reshape(batch * seq, W_QX),
                              h_s, c_s, mix_s.reshape(n_seq, W_KV), xm_s.reshape(n_seq, W_QX),
                              w_gates, b_gate[l], *branch_weights, w_ffn_down[l], tm=TM_DENSE)
    y_p, y_s = _ffn(xp, z_p, xs, z_s, g_post, g_ffn_pre, w_gate_up, w_down, g_ffn_post, tm=TM_FFN)

    win = min(DIL_GROUPS[-1][0], seq)
    kv_shape = (1, batch, seq, N_KV, HEAD_DIM)
    mem_shape = (1, batch, N_MEM, N_XHEADS, HEAD_DIM)
    new_shape = (1, n_seq, 1, N_KV, HEAD_DIM)
    return (y_p.reshape(batch, seq, D_MODEL), y_s.reshape(n_seq, 1, D_MODEL),
            k_rows.reshape(kv_shape)[:, :, seq - win:], v_rows.reshape(kv_shape)[:, :, seq - win:], conv_p,
            mk.reshape(mem_shape), mv.reshape(mem_shape),
            k_s.reshape(new_shape), v_s.reshape(new_shape), conv_s.reshape(1, n_seq, CONV_PREFIX, C_CONV))
```

```python
import functools

import jax
import jax.numpy as jnp
from jax import lax
from jax.experimental import pallas as pl
from jax.experimental.pallas import tpu as pltpu

F32 = jnp.float32
BF16 = jnp.bfloat16

D_MODEL = 2048
C_CONV = D_MODEL // 2
CONV_WIDTH = 31
CONV_PREFIX = CONV_WIDTH - 1
HEAD_DIM = 128
N_KV = 4
DIL_GROUPS = ((128, 1), (512, 4), (2048, 16))
N_GROUPS = len(DIL_GROUPS)
N_QH = N_GROUPS * N_KV
SPAN = 128
N_MEM = 256
N_XHEADS = 4
W_QD = N_QH * HEAD_DIM
W_KV = N_KV * HEAD_DIM
W_QX = N_XHEADS * HEAD_DIM
D_FF = 5632
EPS = 1e-6
NEG = -1e30
SCALE = HEAD_DIM ** -0.5
LANES = 128
SUBLANES = 8
N_SLABS = C_CONV // LANES
MIB = 1024 * 1024

TN = 512
_A_TILES = C_CONV // TN
_Q_TILES = W_QD // TN
_KV_TILES = W_KV // TN
_QX_TILES = W_QX // TN
_STEP_Q0 = _A_TILES
_STEP_K0 = _STEP_Q0 + _Q_TILES
_STEP_V0 = _STEP_K0 + _KV_TILES
_STEP_X0 = _STEP_V0 + _KV_TILES
_IN_STEPS = _STEP_X0 + _QX_TILES
_GATE_TILE0 = 2 * _A_TILES + _Q_TILES + 2 * _KV_TILES + _QX_TILES
_D_TILES = D_MODEL // TN
_CONV_PAD = -(-CONV_PREFIX // SUBLANES) * SUBLANES
_ATTN_UNITS = 8
_RIDER_STEPS = 2
_GATE_CAST_COLS = 1024
TN_OUT = 512
TF = 512
TM_DENSE = 1024
TM_FFN = 512
TM_MEM = 512


def _params(semantics, vmem_mib):
    return pltpu.CompilerParams(dimension_semantics=semantics, vmem_limit_bytes=vmem_mib * MIB)


def _rms_rows(x, g):
    return x * lax.rsqrt(jnp.mean(x * x, axis=-1, keepdims=True) + EPS) * g


def _dot(a, b):
    return jnp.dot(a, b, preferred_element_type=F32)


def _row_ahead(i, s, free_from, n_rows):
    return jnp.minimum(i + (s >= free_from).astype(jnp.int32), n_rows - 1)


def _in_proj_kernel(x_ref, xs_ref, g_ref, w1_ref, w2_ref, h_ref, u_ref, q_ref, k_ref, v_ref, qx_ref,
                    hs_ref, tiles_ref):
    i, s = pl.program_id(0), pl.program_id(1)

    def project(on, x_ref, h_ref, store, u_ref, q_ref, k_ref, v_ref, qx_ref):
        def norm():
            h_ref[...] = _rms_rows(x_ref[...], g_ref[...]).astype(BF16)

        def glu():
            h = h_ref[...]
            store(u_ref, _dot(h, w1_ref[...]) * jax.nn.sigmoid(_dot(h, w2_ref[...])))

        def into(out_ref):
            return lambda: store(out_ref, _dot(h_ref[...], w1_ref[...]))

        pl.when(on & (s == 0))(norm)
        pl.when(on & (s < _STEP_Q0))(glu)
        pl.when(on & (s >= _STEP_Q0) & (s < _STEP_K0))(into(q_ref))
        pl.when(on & (s >= _STEP_K0) & (s < _STEP_V0))(into(k_ref))
        pl.when(on & (s >= _STEP_V0) & (s < _STEP_X0))(into(v_ref))
        pl.when(on & (s >= _STEP_X0))(into(qx_ref))

    def store_slabs(out_ref, tile):
        for c in range(TN // LANES):
            out_ref[c] = tile[:, c * LANES:(c + 1) * LANES].astype(out_ref.dtype)

    def store_rows(out_ref, tile):
        out_ref[...] = tile

    project(True, x_ref, h_ref, store_slabs, u_ref, q_ref, k_ref, v_ref, qx_ref)
    project(i == 0, xs_ref, hs_ref, store_rows, *([tiles_ref] * 5))


def _in_proj(x, xs, g, w_in_t, *, tm):
    m, ms = x.shape[0], xs.shape[0]
    last_a = _A_TILES - 1
    per_tile = TN // LANES

    def out_tiles(step0, n_tiles):
        def index(i, s):
            moved = (s >= step0 + n_tiles) & (i < m // tm - 1)
            return (jnp.where(moved, 0, jnp.clip(s - step0, 0, n_tiles - 1)), i + moved.astype(jnp.int32), 0)
        return pl.BlockSpec((per_tile, tm, LANES), index)

    def slabs(width, dtype):
        return jax.ShapeDtypeStruct((width // LANES, m, LANES), dtype)

    outs = pl.pallas_call(
        _in_proj_kernel,
        grid=(m // tm, _IN_STEPS),
        in_specs=[
            pl.BlockSpec((tm, D_MODEL), lambda i, s: (_row_ahead(i, s, _A_TILES, m // tm), 0)),
            pl.BlockSpec((ms, D_MODEL), lambda i, s: (0, 0)),
            pl.BlockSpec((1, D_MODEL), lambda i, s: (0, 0)),
            pl.BlockSpec((D_MODEL, TN), lambda i, s: (0, jnp.where(s < _A_TILES, s, s + _A_TILES))),
            pl.BlockSpec((D_MODEL, TN), lambda i, s: (0, jnp.where(s < _A_TILES, s, 0) + _A_TILES)),
        ],
        out_specs=[
            pl.BlockSpec((tm, D_MODEL), lambda i, s: (i, 0)),
            pl.BlockSpec((per_tile, tm, LANES), lambda i, s: (jnp.minimum(s, last_a), i, 0)),
            out_tiles(_STEP_Q0, _Q_TILES),
            out_tiles(_STEP_K0, _KV_TILES),
            out_tiles(_STEP_V0, _KV_TILES),
            out_tiles(_STEP_X0, _QX_TILES),
            pl.BlockSpec((ms, D_MODEL), lambda i, s: (0, 0)),
            pl.BlockSpec((ms, TN), lambda i, s: (0, jnp.where(i == 0, s, _IN_STEPS - 1))),
        ],
        out_shape=[
            jax.ShapeDtypeStruct((m, D_MODEL), BF16),
            slabs(C_CONV, F32), slabs(W_QD, F32), slabs(W_KV, F32), slabs(W_KV, F32), slabs(W_QX, BF16),
            jax.ShapeDtypeStruct((ms, D_MODEL), BF16),
            jax.ShapeDtypeStruct((ms, _IN_STEPS * TN), F32),
        ],
        compiler_params=_params(("arbitrary", "arbitrary"), 60),
        name="in_proj",
    )(x, xs, g, w_in_t, w_in_t)
    *prompt, h_s, tiles = outs
    bounds = [step * TN for step in (0, _STEP_Q0, _STEP_K0, _STEP_V0, _STEP_X0, _IN_STEPS)]
    return (*prompt, h_s, *(tiles[:, lo:hi] for lo, hi in zip(bounds, bounds[1:])))


def _mem_kv_kernel(x_ref, g_ref, w_ref, mk_ref, mv_ref, h_ref):
    s = pl.program_id(1)

    @pl.when(s == 0)
    def _():
        h_ref[...] = _rms_rows(x_ref[...], g_ref[...]).astype(BF16)
        mk_ref[...] = _dot(h_ref[...], w_ref[...])

    @pl.when(s == 1)
    def _():
        mv_ref[...] = _dot(h_ref[...], w_ref[...])


def _mem_kv(mem, g, w_t, *, tm):
    m = mem.shape[0]
    return pl.pallas_call(
        _mem_kv_kernel,
        grid=(m // tm, 2),
        in_specs=[
            pl.BlockSpec((tm, D_MODEL), lambda i, s: (i, 0)),
            pl.BlockSpec((1, D_MODEL), lambda i, s: (0, 0)),
            pl.BlockSpec((D_MODEL, W_QX), lambda i, s: (0, s)),
        ],
        out_specs=[
            pl.BlockSpec((tm, W_QX), lambda i, s: (i, 0)),
            pl.BlockSpec((tm, W_QX), lambda i, s: (i, 0)),
        ],
        out_shape=[jax.ShapeDtypeStruct((m, W_QX), F32)] * 2,
        scratch_shapes=[pltpu.VMEM((tm, D_MODEL), BF16)],
        compiler_params=_params(("parallel", "arbitrary"), 32),
        name="mem_kv",
    )(mem, g, w_t)


def _ln_swish_slabs(y_slabs, cb_ref, lng_ref, lnb_ref, out_dtype):
    ys = [y + cb_ref[:, c * LANES:(c + 1) * LANES] for c, y in enumerate(y_slabs)]
    tot = ys[0]
    for y in ys[1:]:
        tot = tot + y
    mu = jnp.sum(tot, axis=-1, keepdims=True) * (1.0 / C_CONV)
    ds = [y - mu for y in ys]
    sq = ds[0] * ds[0]
    for d in ds[1:]:
        sq = sq + d * d
    rs = lax.rsqrt(jnp.sum(sq, axis=-1, keepdims=True) * (1.0 / C_CONV) + EPS)
    outs = []
    for c, d in enumerate(ds):
        z = d * rs * lng_ref[:, c * LANES:(c + 1) * LANES] + lnb_ref[:, c * LANES:(c + 1) * LANES]
        outs.append((z * jax.nn.sigmoid(z)).astype(out_dtype))
    return outs


def _conv_prompt_kernel(u_ref, w_ref, cb_ref, lng_ref, lnb_ref, k_ref, v_ref, *rest, seq, tt, n_cast):
    cast_in, (c_ref, st_ref, kout_ref, vout_ref) = rest[:n_cast], rest[n_cast:n_cast + 4]
    cast_out, (head_ref, y_ref) = rest[n_cast + 4:2 * n_cast + 4], rest[2 * n_cast + 4:]
    half = pl.program_id(1)
    n_tiles = seq // tt // 2
    pad = _CONV_PAD

    for src_ref, dst_ref in ((k_ref, kout_ref), (v_ref, vout_ref)):
        for h in range(N_KV):
            dst_ref[pl.ds(h, src_ref.shape[1], stride=N_KV), :] = src_ref[h]
    for src_ref, dst_ref in zip(cast_in, cast_out):
        dst_ref[...] = src_ref[...].astype(BF16)

    def tile(src_ref, base, t0):
        def slab(c, carry):
            acc = jnp.zeros((tt, LANES), F32)
            for j in range(SUBLANES):
                taps = range(j, CONV_WIDTH, SUBLANES)
                win = src_ref[c, pl.ds(base + j, tt + taps[-1] - j), :]
                for k in taps:
                    acc = acc + win[k - j:k - j + tt] * w_ref[c, k:k + 1, :]
            y_ref[c] = acc
            return carry

        lax.fori_loop(0, N_SLABS, slab, 0)
        ys = [y_ref[c] for c in range(N_SLABS)]
        outs = _ln_swish_slabs(ys, cb_ref, lng_ref, lnb_ref, BF16)
        for c in range(N_SLABS):
            c_ref[pl.ds(t0, tt), c * LANES:(c + 1) * LANES] = outs[c]

    @pl.when(half == 0)
    def _():
        head_ref[:, 0:pad, :] = jnp.zeros((N_SLABS, pad, LANES), F32)
        head_ref[:, pad:pad + tt, :] = u_ref[:, 0:tt, :]
        tile(head_ref, pad - CONV_PREFIX, 0)

    @pl.when(half > 0)
    def _():
        tile(u_ref, pl.multiple_of(half * (n_tiles * tt), tt) - CONV_PREFIX, 0)

    def body(i, carry):
        t0 = pl.multiple_of((half * n_tiles + i) * tt, tt)
        tile(u_ref, t0 - CONV_PREFIX, pl.multiple_of(i * tt, tt))
        return carry

    lax.fori_loop(1, n_tiles, body, 0)
    for c in range(N_SLABS):
        st_ref[:, c * LANES:(c + 1) * LANES] = u_ref[c, seq - CONV_PREFIX:seq, :]


def _conv_prompt(u_slabs, w_slabs, conv_b, ln_g, ln_b, k, v, cast_weights, *, batch, seq):
    tt = 128
    halves = 2
    n_steps = batch * halves
    step = lambda b, j: b * halves + j
    const = lambda *idx: (lambda b, j: idx)

    def row_slice(rows, cols):
        return pl.BlockSpec((rows // n_steps, cols), lambda b, j: (step(b, j), 0))

    kv_in = pl.BlockSpec((N_KV, batch * seq // n_steps, LANES), lambda b, j: (0, step(b, j), 0))
    kv_out = row_slice(batch * seq * N_KV, HEAD_DIM)
    w_specs = [row_slice(*w.shape) for w in cast_weights]
    return pl.pallas_call(
        functools.partial(_conv_prompt_kernel, seq=seq, tt=tt, n_cast=len(cast_weights)),
        grid=(batch, halves),
        in_specs=[
            pl.BlockSpec((N_SLABS, seq, LANES), lambda b, j: (0, b, 0)),
            pl.BlockSpec((N_SLABS, CONV_WIDTH, LANES), const(0, 0, 0)),
            pl.BlockSpec((1, C_CONV), const(0, 0)),
            pl.BlockSpec((1, C_CONV), const(0, 0)),
            pl.BlockSpec((1, C_CONV), const(0, 0)),
            kv_in, kv_in,
        ] + w_specs,
        out_specs=[
            pl.BlockSpec((seq // halves, C_CONV), lambda b, j: (step(b, j), 0)),
            pl.BlockSpec((None, None, CONV_PREFIX, C_CONV), lambda b, j: (0, b, 0, 0)),
            kv_out, kv_out,
        ] + w_specs,
        out_shape=[
            jax.ShapeDtypeStruct((batch * seq, C_CONV), BF16),
            jax.ShapeDtypeStruct((1, batch, CONV_PREFIX, C_CONV), F32),
            jax.ShapeDtypeStruct((batch * seq * N_KV, HEAD_DIM), F32),
            jax.ShapeDtypeStruct((batch * seq * N_KV, HEAD_DIM), F32),
        ] + [jax.ShapeDtypeStruct(w.shape, BF16) for w in cast_weights],
        scratch_shapes=[pltpu.VMEM((N_SLABS, _CONV_PAD + tt, LANES), F32), pltpu.VMEM((N_SLABS, tt, LANES), F32)],
        compiler_params=_params(("parallel", "arbitrary"), 56),
        name="conv_prompt",
    )(u_slabs, w_slabs, conv_b, ln_g, ln_b, k, v, *cast_weights)


def _conv_sample_kernel(state_ref, u_ref, w_ref, cb_ref, lng_ref, lnb_ref, c_ref, st_ref, *, n_seq):
    w_hist = w_ref[0:CONV_PREFIX, :]
    w_new = w_ref[CONV_PREFIX:CONV_WIDTH, :]
    for n in range(n_seq):
        hist = state_ref[n]
        new = u_ref[n:n + 1, :]
        y = jnp.sum(hist * w_hist, axis=0, keepdims=True) + new * w_new
        ys = [y[:, c * LANES:(c + 1) * LANES] for c in range(N_SLABS)]
        outs = _ln_swish_slabs(ys, cb_ref, lng_ref, lnb_ref, F32)
        for c in range(N_SLABS):
            c_ref[n:n + 1, c * LANES:(c + 1) * LANES] = outs[c]
        st_ref[n, 0:CONV_PREFIX - 1, :] = state_ref[n, 1:CONV_PREFIX, :]
        st_ref[n, CONV_PREFIX - 1:CONV_PREFIX, :] = new


def _conv_sample(state, u, conv_w, conv_b, ln_g, ln_b):
    n_seq = state.shape[0]
    return pl.pallas_call(
        functools.partial(_conv_sample_kernel, n_seq=n_seq),
        out_shape=[
            jax.ShapeDtypeStruct((n_seq, C_CONV), F32),
            jax.ShapeDtypeStruct((n_seq, CONV_PREFIX, C_CONV), F32),
        ],
        compiler_params=pltpu.CompilerParams(vmem_limit_bytes=32 * MIB),
        name="conv_sample",
    )(state, u, conv_w, conv_b, ln_g, ln_b)


def _qk(q, k):
    return lax.dot_general(q, k, (((1,), (1,)), ((), ())), preferred_element_type=F32)


def _attn_prompt_kernel(slopes_ref, q0_ref, q1_ref, q2_ref, k_ref, v_ref, qx_ref, mk_ref, mv_ref,
                        wg_ref, wu_ref, mix_ref, xm_ref, wgu_out_ref,
                        bias_ref, o_ref, lse_ref, s_ref, p_ref, m_ref, ones_ref, *, seq):
    head = pl.program_id(1)
    ones_ref[...] = jnp.ones(ones_ref.shape, BF16)
    half = TF // 2
    for j in range(D_FF // half):
        cols = slice(j * half, (j + 1) * half)
        wgu_out_ref[:, 2 * j * half:(2 * j + 1) * half] = wg_ref[:, cols].astype(BF16)
        wgu_out_ref[:, (2 * j + 1) * half:(2 * j + 2) * half] = wu_ref[:, cols].astype(BF16)
    qi = lax.broadcasted_iota(jnp.int32, (SPAN, 2 * SPAN), 0)
    ci = lax.broadcasted_iota(jnp.int32, (SPAN, 2 * SPAN), 1)
    dist = qi + SPAN - ci
    valid = (dist >= 0) & (dist <= SPAN)
    for g, (_, dil) in enumerate(DIL_GROUPS):
        bias = -slopes_ref[g, head] * (dist * dil).astype(F32)
        bias_ref[g] = jnp.where(valid, bias, NEG)

    def rows(ref, start, size, dil):
        if dil == 1:
            return ref[pl.ds(start, size), :]
        return ref[pl.ds(start, size, stride=dil), :]

    def put(ref, g, start, dil, val):
        if dil == 1:
            ref[g, pl.ds(start, SPAN), :] = val
        else:
            ref[g, pl.ds(start, SPAN, stride=dil), :] = val

    def staged(items, n_keys_of, scores, values, emit):
        for u, item in enumerate(items):
            s_ref[u, :, 0:n_keys_of(item)] = scores(item)
        for u, item in enumerate(items):
            s = s_ref[u, :, 0:n_keys_of(item)]
            m = jnp.max(s, axis=-1, keepdims=True)
            p_ref[u, :, 0:n_keys_of(item)] = jnp.exp(s - m).astype(BF16)
            m_ref[u] = jnp.broadcast_to(m, (SPAN, LANES))
        for u, item in enumerate(items):
            p = p_ref[u, :, 0:n_keys_of(item)]
            l = _dot(p, ones_ref[0:n_keys_of(item), :])
            emit(item, _dot(p, values(item)) * (1.0 / l), m_ref[u] + jnp.log(l))

    def units(g, dil, q_ref, starts_keys):
        def scores(item):
            start, n_keys = item
            q = rows(q_ref, start, SPAN, dil).astype(BF16)
            k = rows(k_ref, start - (n_keys - SPAN) * dil, n_keys, dil).astype(BF16)
            return _qk(q, k) * SCALE + bias_ref[g, :, 2 * SPAN - n_keys:2 * SPAN]

        def values(item):
            start, n_keys = item
            return rows(v_ref, start - (n_keys - SPAN) * dil, n_keys, dil).astype(BF16)

        def emit(item, o, lse):
            put(o_ref, g, item[0], dil, o)
            put(lse_ref, g, item[0], dil, lse)

        staged(starts_keys, lambda item: item[1], scores, values, emit)

    q_refs = (q0_ref, q1_ref, q2_ref)
    for g, (_, dil) in enumerate(DIL_GROUPS):
        n_blocks = seq // dil // SPAN
        q_ref = q_refs[g]
        if n_blocks == 1:
            per_body = _ATTN_UNITS

            def first_blocks(j, carry, g=g, dil=dil, q_ref=q_ref):
                units(g, dil, q_ref, [(j * per_body + u, SPAN) for u in range(per_body)])
                return carry
            lax.fori_loop(0, dil // per_body, first_blocks, 0)
        elif dil == 1:
            per_body = _ATTN_UNITS

            def later_blocks(j, carry, g=g, dil=dil, q_ref=q_ref):
                units(g, dil, q_ref, [(pl.multiple_of((j * per_body + u) * SPAN, SPAN), 2 * SPAN)
                                      for u in range(per_body)])
                return carry
            units(g, dil, q_ref, [(0, SPAN)] + [(n * SPAN, 2 * SPAN) for n in range(1, per_body)])
            lax.fori_loop(1, n_blocks // per_body, later_blocks, 0)
        else:
            streams = _ATTN_UNITS // n_blocks

            def residue(j, carry, g=g, dil=dil, q_ref=q_ref, n_blocks=n_blocks, streams=streams):
                work = []
                for r in range(streams):
                    start = j * streams + r
                    work.append((start, SPAN))
                    work += [(start + n * (SPAN * dil), 2 * SPAN) for n in range(1, n_blocks)]
                units(g, dil, q_ref, work)
                return carry
            lax.fori_loop(0, dil // streams, residue, 0)

    tq = 256

    def mix(j, carry):
        t0 = pl.multiple_of(j * tq, tq)
        l0 = lse_ref[0, pl.ds(t0, tq), :]
        l1 = lse_ref[1, pl.ds(t0, tq), :]
        l2 = lse_ref[2, pl.ds(t0, tq), :]
        top = jnp.maximum(jnp.maximum(l0, l1), l2)
        w0 = jnp.exp(l0 - top)
        w1 = jnp.exp(l1 - top)
        w2 = jnp.exp(l2 - top)
        num = (w0 * o_ref[0, pl.ds(t0, tq), :] + w1 * o_ref[1, pl.ds(t0, tq), :]
               + w2 * o_ref[2, pl.ds(t0, tq), :])
        mix_ref[pl.ds(t0, tq), :] = (num / (w0 + w1 + w2)).astype(BF16)
        return carry

    lax.fori_loop(0, seq // tq, mix, 0, unroll=2)

    def memory(j, carry):
        def scores(t0):
            return _qk(qx_ref[pl.ds(t0, SPAN), :], mk_ref[...].astype(BF16)) * SCALE

        def emit(t0, o, lse):
            xm_ref[pl.ds(t0, SPAN), :] = o.astype(BF16)

        staged([pl.multiple_of((j * _ATTN_UNITS + u) * SPAN, SPAN) for u in range(_ATTN_UNITS)],
               lambda t0: N_MEM, scores, lambda t0: mv_ref[...].astype(BF16), emit)
        return carry

    lax.fori_loop(0, seq // SPAN // _ATTN_UNITS, memory, 0)


def _attn_prompt(slopes, q, k, v, qx, mk, mv, w_gate, w_up, *, batch, seq):
    def col(c):
        return pl.BlockSpec((None, seq, LANES), lambda b, h, c=c: (c * N_KV + h, b, 0))

    slab_spec = col(0)
    head_spec = pl.BlockSpec((None, seq, LANES), lambda b, h: (b, 0, h))
    mem_spec = pl.BlockSpec((None, N_MEM, LANES), lambda b, h: (b, 0, h))
    n_steps = batch * N_KV

    def row_slice(rows, cols):
        return pl.BlockSpec((rows // n_steps, cols), lambda b, h: (b * N_KV + h, 0))

    weights = (w_gate, w_up)
    gu_shape = (D_MODEL, 2 * D_FF)
    return pl.pallas_call(
        functools.partial(_attn_prompt_kernel, seq=seq),
        grid=(batch, N_KV),
        in_specs=[
            pl.BlockSpec(memory_space=pltpu.SMEM),
            col(0), col(1), col(2), slab_spec, slab_spec, slab_spec, mem_spec, mem_spec,
        ] + [row_slice(*w.shape) for w in weights],
        out_specs=[head_spec, head_spec, row_slice(*gu_shape)],
        out_shape=[jax.ShapeDtypeStruct((batch, seq, W_KV), BF16),
                   jax.ShapeDtypeStruct((batch, seq, W_QX), BF16),
                   jax.ShapeDtypeStruct(gu_shape, BF16)],
        scratch_shapes=[
            pltpu.VMEM((N_GROUPS, SPAN, 2 * SPAN), F32),
            pltpu.VMEM((N_GROUPS, seq, LANES), F32),
            pltpu.VMEM((N_GROUPS, seq, LANES), F32),
            pltpu.VMEM((_ATTN_UNITS, SPAN, 2 * SPAN), F32),
            pltpu.VMEM((_ATTN_UNITS, SPAN, 2 * SPAN), BF16),
            pltpu.VMEM((_ATTN_UNITS, SPAN, LANES), F32),
            pltpu.VMEM((2 * SPAN, LANES), BF16),
        ],
        compiler_params=_params(("parallel", "parallel"), 56),
        name="attn_prompt",
    )(slopes, q, q, q, k, v, qx, mk, mv, *weights)


def _attn_sample_kernel(slopes_ref, q_ref, kn_ref, vn_ref, qx_ref, ck0_ref, ck1_ref, ck2_ref,
                        cv0_ref, cv1_ref, cv2_ref, mk_ref, mv_ref, *rest, bs):
    wcol_refs, (mix_ref, xm_ref, wgate_ref) = rest[:-3], rest[-3:]

    def by_row(ref):
        return ref if len(ref.shape) == 3 else ref.reshape(ref.shape[0], ref.shape[1] * ref.shape[2], LANES)

    def head_rows(ref, n, h, n_keys):
        return ref[n, pl.ds(h, n_keys, stride=ref.shape[1] // n_keys), :]

    ck_refs = tuple(by_row(r) for r in (ck0_ref, ck1_ref, ck2_ref))
    cv_refs = tuple(by_row(r) for r in (cv0_ref, cv1_ref, cv2_ref))
    back = (SPAN - lax.broadcasted_iota(jnp.int32, (SPAN, 1), 0)).astype(F32)
    for c, wcol_ref in enumerate(wcol_refs):
        wgate_ref[:, c * _GATE_CAST_COLS:(c + 1) * _GATE_CAST_COLS] = wcol_ref[...].astype(BF16)

    def one(n, carry):
        for h in range(N_KV):
            hs = slice(h * LANES, (h + 1) * LANES)
            kn = kn_ref[n, :, hs]
            vn = vn_ref[n, :, hs]
            scores, s_new = [], []
            for g, (_, dil) in enumerate(DIL_GROUPS):
                qs = slice(g * W_KV + h * LANES, g * W_KV + (h + 1) * LANES)
                q = q_ref[n, :, qs]
                s = jnp.sum(head_rows(ck_refs[g], n, h, SPAN) * q, axis=-1, keepdims=True) * SCALE
                scores.append(s - slopes_ref[g, h] * (back * float(dil)))
                s_new.append(jnp.sum(kn * q, axis=-1, keepdims=True) * SCALE)
            top = s_new[0]
            for g in range(N_GROUPS):
                top = jnp.maximum(top, jnp.maximum(s_new[g], jnp.max(scores[g], axis=0, keepdims=True)))
            num = jnp.zeros((1, LANES), F32)
            den = jnp.zeros((1, 1), F32)
            for g in range(N_GROUPS):
                p = jnp.exp(scores[g] - top)
                p_new = jnp.exp(s_new[g] - top)
                num = num + jnp.sum(p * head_rows(cv_refs[g], n, h, SPAN), axis=0, keepdims=True) + p_new * vn
                den = den + jnp.sum(p, axis=0, keepdims=True) + p_new
            mix_ref[n, :, hs] = num / den
            qx = qx_ref[n, :, hs]
            sx = jnp.sum(head_rows(mk_ref, n, h, N_MEM) * qx, axis=-1, keepdims=True) * SCALE
            px = jnp.exp(sx - jnp.max(sx, axis=0, keepdims=True))
            ox = (jnp.sum(px * head_rows(mv_ref, n, h, N_MEM), axis=0, keepdims=True)
                  / jnp.sum(px, axis=0, keepdims=True))
            xm_ref[n, :, hs] = ox
        return carry

    lax.fori_loop(0, bs, one, 0)


def _attn_sample(slopes, q, kn, vn, qx, cache_k, cache_v, mem_k, mem_v, w_in):
    n_seq, win = cache_k.shape[0], cache_k.shape[1]
    bs = 2
    n_steps = n_seq // bs
    wrows = w_in.shape[0] // n_steps
    gate_col0 = _GATE_TILE0 * TN // _GATE_CAST_COLS
    wcol_specs = [pl.BlockSpec((wrows, _GATE_CAST_COLS), lambda i, c=c: (i, gate_col0 + c))
                  for c in range(3 * D_MODEL // _GATE_CAST_COLS)]
    row = lambda w: pl.BlockSpec((bs, 1, w), lambda i: (i, 0, 0))
    as_rows = lambda a: a.reshape(n_seq, 1, a.shape[-1])
    views_k, views_v, cache_specs = [], [], []
    for _, dil in DIL_GROUPS:
        nb = win // dil // SPAN
        if dil == 1:
            shape = (n_seq, win * N_KV, HEAD_DIM)
            cache_specs.append(pl.BlockSpec((bs, SPAN * N_KV, HEAD_DIM), lambda i, nb=nb: (i, nb - 1, 0)))
        else:
            shape = (n_seq, win // dil, dil * N_KV, HEAD_DIM)
            cache_specs.append(pl.BlockSpec((bs, SPAN, SUBLANES, HEAD_DIM), lambda i, nb=nb: (i, nb - 1, 0, 0)))
        views_k.append(cache_k.reshape(shape))
        views_v.append(cache_v.reshape(shape))
    mem_spec = pl.BlockSpec((bs, N_MEM * N_XHEADS, HEAD_DIM), lambda i: (i, 0, 0))
    mem_rows = lambda a: a.reshape(n_seq, N_MEM * N_XHEADS, HEAD_DIM)
    return pl.pallas_call(
        functools.partial(_attn_sample_kernel, bs=bs),
        grid=(n_seq // bs,),
        in_specs=[pl.BlockSpec(memory_space=pltpu.SMEM), row(W_QD), row(W_KV), row(W_KV), row(W_QX)]
        + cache_specs + cache_specs + [mem_spec, mem_spec] + wcol_specs,
        out_specs=[row(W_KV), row(W_QX), pl.BlockSpec((wrows, 3 * D_MODEL), lambda i: (i, 0))],
        out_shape=[jax.ShapeDtypeStruct((n_seq, 1, W_KV), F32), jax.ShapeDtypeStruct((n_seq, 1, W_QX), F32),
                   jax.ShapeDtypeStruct((w_in.shape[0], 3 * D_MODEL), BF16)],
        compiler_params=_params(("parallel",), 48),
        name="attn_sample",
    )(slopes, as_rows(q), as_rows(kn), as_rows(vn), as_rows(qx), *views_k, *views_v, mem_rows(mem_k),
      mem_rows(mem_v), *([w_in] * len(wcol_specs)))


def _merge_kernel(h_ref, c_ref, mix_ref, xm_ref, hs_ref, cs_ref, mixs_ref, xms_ref, wg0_ref, wg1_ref, wg2_ref,
                  bg_ref, wc_ref, wd_ref, wx_ref, wo_ref, wffn_ref, z_ref, zs_ref, wffn_out_ref,
                  merged_ref, mergeds_ref):
    i, s = pl.program_id(0), pl.program_id(1)

    @pl.when(s % _RIDER_STEPS == 0)
    def _():
        wffn_out_ref[...] = wffn_ref[...].astype(BF16)

    def gated_sum(h_ref, branches, merged_ref):
        h = h_ref[...]
        acc = None
        for br, (wg_ref, y_ref, wy_ref) in enumerate(zip((wg0_ref, wg1_ref, wg2_ref), branches,
                                                         (wc_ref, wd_ref, wx_ref))):
            gate = jax.nn.sigmoid(_dot(h, wg_ref[...]) + bg_ref[br])
            term = gate * _dot(y_ref[...].astype(BF16), wy_ref[...])
            acc = term if acc is None else acc + term
        merged_ref[s] = acc.astype(BF16)

    def project(merged_ref, z_ref):
        z = None
        for j in range(_D_TILES):
            part = _dot(merged_ref[j], wo_ref[j * TN:(j + 1) * TN, :])
            z = part if z is None else z + part
        z_ref[...] = z

    first, second = s < _D_TILES, s >= _D_TILES
    pl.when(first)(lambda: gated_sum(h_ref, (c_ref, mix_ref, xm_ref), merged_ref))
    pl.when(second)(lambda: project(merged_ref, z_ref))
    pl.when(first & (i == 0))(lambda: gated_sum(hs_ref, (cs_ref, mixs_ref, xms_ref), mergeds_ref))
    pl.when(second & (i == 0))(lambda: project(mergeds_ref, zs_ref))


def _merge(h, c, mix, xm, hs, cs, mixs, xms, w_gates, b_gate, w_conv_out_t, w_dil_o_t, w_x_o_t, w_out_t, w_ffn,
           *, tm):
    m, ms = h.shape[0], hs.shape[0]
    n1, n2 = _D_TILES, D_MODEL // TN_OUT

    n_rows = m // tm
    whole = lambda a: pl.BlockSpec(a.shape, lambda i, s: (0, 0))
    rider_blocks = n_rows * (n1 + n2) // _RIDER_STEPS
    rider = pl.BlockSpec((w_ffn.shape[0] // rider_blocks, w_ffn.shape[1]),
                         lambda i, s: ((i * (n1 + n2) + s) // _RIDER_STEPS, 0))

    def tile1(s):
        return jnp.where(s < n1, s, 0)

    def gate_spec(br):
        return pl.BlockSpec((D_MODEL, TN), lambda i, s, br=br: (0, br * _D_TILES + tile1(s)))

    def colw(kdim):
        return pl.BlockSpec((kdim, TN), lambda i, s: (0, tile1(s)))

    def rowblk(w):
        return pl.BlockSpec((tm, w), lambda i, s: (_row_ahead(i, s, n1, n_rows), 0))

    return pl.pallas_call(
        _merge_kernel,
        grid=(m // tm, n1 + n2),
        in_specs=[
            rowblk(D_MODEL), rowblk(C_CONV), rowblk(W_KV), rowblk(W_QX),
            whole(hs), whole(cs), whole(mixs), whole(xms),
            gate_spec(0), gate_spec(1), gate_spec(2),
            pl.BlockSpec((3, 1, TN), lambda i, s: (0, 0, tile1(s))),
            colw(C_CONV), colw(W_KV), colw(W_QX),
            pl.BlockSpec((D_MODEL, TN_OUT), lambda i, s: (0, jnp.maximum(s - n1, 0))),
            rider,
        ],
        out_specs=[
            pl.BlockSpec((tm, TN_OUT), lambda i, s: (i, jnp.maximum(s - n1, 0))),
            pl.BlockSpec((ms, TN_OUT), lambda i, s: (0, jnp.where(i == 0, jnp.maximum(s - n1, 0), n2 - 1))),
            rider,
        ],
        out_shape=[jax.ShapeDtypeStruct((m, D_MODEL), F32), jax.ShapeDtypeStruct((ms, D_MODEL), F32),
                   jax.ShapeDtypeStruct(w_ffn.shape, BF16)],
        scratch_shapes=[pltpu.VMEM((n1, tm, TN), BF16), pltpu.VMEM((n1, ms, TN), BF16)],
        compiler_params=_params(("arbitrary", "arbitrary"), 60),
        name="merge",
    )(h, c, mix, xm, hs, cs, mixs, xms, w_gates, w_gates, w_gates, b_gate.reshape(3, 1, D_MODEL), w_conv_out_t,
      w_dil_o_t, w_x_o_t, w_out_t, w_ffn)


def _ffn_kernel(x_ref, z_ref, xs_ref, zs_ref, gmix_ref, gpre_ref, wgu_ref, wd_ref, gpost_ref, out_ref, outs_ref,
                x1_ref, h_ref, x1s_ref, *, n_steps):
    i, s = pl.program_id(0), pl.program_id(1)
    tm, ms = x_ref.shape[0], xs_ref.shape[0]

    def prologue(x_ref, z_ref, x1_ref, row0, out_ref):
        x1 = x_ref[...] + _rms_rows(z_ref[...], gmix_ref[...])
        x1_ref[...] = x1
        h_ref[row0:row0 + x_ref.shape[0], :] = _rms_rows(x1, gpre_ref[...]).astype(BF16)
        out_ref[...] = jnp.zeros_like(out_ref)

    def step(rows):
        h = h_ref[0:rows, :]
        half = TF // 2
        part = None
        for c in range(2):
            gate_up = _dot(h, wgu_ref[:, c * TF:(c + 1) * TF])
            gate, up = gate_up[:, :half], gate_up[:, half:]
            act = (gate * jax.nn.sigmoid(gate) * up).astype(BF16)
            down = _dot(act, wd_ref[c * half:(c + 1) * half, :])
            part = down if part is None else part + down
        out_ref[...] += part[0:tm]
        if rows > tm:
            outs_ref[...] += part[tm:rows]

    def epilogue(x1_ref, out_ref):
        out_ref[...] = x1_ref[...] + _rms_rows(out_ref[...], gpost_ref[...])

    pl.when(s == 0)(lambda: prologue(x_ref, z_ref, x1_ref, 0, out_ref))
    pl.when((i == 0) & (s == 0))(lambda: prologue(xs_ref, zs_ref, x1s_ref, tm, outs_ref))
    pl.when(i == 0)(lambda: step(tm + ms))
    pl.when(i > 0)(lambda: step(tm))
    pl.when(s == n_steps - 1)(lambda: epilogue(x1_ref, out_ref))
    pl.when((i == 0) & (s == n_steps - 1))(lambda: epilogue(x1s_ref, outs_ref))


def _ffn(x, z, xs, zs, g_mix, g_pre, w_gate_up, w_down, g_post, *, tm):
    m, ms = x.shape[0], xs.shape[0]
    n_steps = D_FF // TF
    rowblk = pl.BlockSpec((tm, D_MODEL), lambda i, s: (i, 0))
    sample = pl.BlockSpec((ms, D_MODEL), lambda i, s: (0, 0))
    gain = pl.BlockSpec((1, D_MODEL), lambda i, s: (0, 0))
    return pl.pallas_call(
        functools.partial(_ffn_kernel, n_steps=n_steps),
        grid=(m // tm, n_steps),
        in_specs=[
            rowblk, rowblk, sample, sample, gain, gain,
            pl.BlockSpec((D_MODEL, 2 * TF), lambda i, s: (0, s)),
            pl.BlockSpec((TF, D_MODEL), lambda i, s: (s, 0)),
            gain,
        ],
        out_specs=[rowblk, sample],
        out_shape=[jax.ShapeDtypeStruct((m, D_MODEL), F32), jax.ShapeDtypeStruct((ms, D_MODEL), F32)],
        scratch_shapes=[pltpu.VMEM((tm, D_MODEL), F32), pltpu.VMEM((tm + ms, D_MODEL), BF16),
                        pltpu.VMEM((ms, D_MODEL), F32)],
        compiler_params=_params(("arbitrary", "arbitrary"), 54),
        name="ffn",
    )(x, z, xs, zs, g_mix, g_pre, w_gate_up, w_down, g_post)


def kernel(x_prompt, x_sample, cache_win_k, cache_win_v, state_conv, cache_mem_k, cache_mem_v, mem_prompt,
           g_pre_mix, w_in, b_gate, conv_w, conv_b, conv_ln_g, conv_ln_b, w_conv_out, w_dil_o, g_mem, w_mem_kv,
           w_x_o, w_out, g_post_mix, g_pre_ffn, w_ffn_gate, w_ffn_up, w_ffn_down, g_post_ffn):
    batch, seq, _ = x_prompt.shape
    n_seq = x_sample.shape[0]
    depth = w_in.shape[0]
    assert depth == 1 and x_sample.shape[1] == 1 and seq % (DIL_GROUPS[-1][1] * SPAN) == 0
    assert cache_win_k.shape[2] == DIL_GROUPS[-1][0]

    idx = jnp.arange(1, N_QH + 1, dtype=F32)
    slopes = jnp.exp2(-8.0 * idx / N_QH).reshape(N_GROUPS, N_KV)

    xp, xs = x_prompt.reshape(batch * seq, D_MODEL), x_sample.reshape(n_seq, D_MODEL)
    l = 0
    w_in_t = w_in[l][:, :_GATE_TILE0 * TN].astype(BF16)
    w_mem_kv_t = w_mem_kv[l].astype(BF16)
    row = lambda a: a[l].reshape(1, -1)
    g_pre, g_post, g_ffn_pre, g_ffn_post = row(g_pre_mix), row(g_post_mix), row(g_pre_ffn), row(g_post_ffn)
    cb, lng, lnb = row(conv_b), row(conv_ln_g), row(conv_ln_b)
    conv_w_slabs = conv_w[l].reshape(CONV_WIDTH, N_SLABS, LANES).transpose(1, 0, 2)

    mk, mv = _mem_kv(mem_prompt.reshape(batch * N_MEM, D_MODEL), row(g_mem), w_mem_kv_t, tm=TM_MEM)
    h_p, u_p, q_p, k_p, v_p, qx_p, h_s, u_s, q_s, k_s, v_s, qx_s = _in_proj(xp, xs, g_pre, w_in_t, tm=TM_DENSE)

    c_p, conv_p, k_rows, v_rows, *branch_weights = _conv_prompt(
        u_p, conv_w_slabs, cb, lng, lnb, k_p, v_p, (w_conv_out[l], w_dil_o[l], w_x_o[l], w_out[l]),
        batch=batch, seq=seq)
    mix_p, xm_p, w_gate_up = _attn_prompt(
        slopes, q_p, k_p, v_p, qx_p, mk.reshape(batch, N_MEM, W_QX), mv.reshape(batch, N_MEM, W_QX),
        w_ffn_gate[l], w_ffn_up[l], batch=batch, seq=seq)

    c_s, conv_s = _conv_sample(state_conv[l], u_s, conv_w[l], cb, lng, lnb)
    mix_s, xm_s, w_gates = _attn_sample(slopes, q_s, k_s, v_s, qx_s, cache_win_k[l], cache_win_v[l],
                                        cache_mem_k[l], cache_mem_v[l], w_in[l])

    z_p, z_s, w_down = _merge(h_p, c_p, mix_p.reshape(batch * seq, W_KV), xm_p.reshape(batch * seq, W_QX),
                              h_s, c_s, mix_s.reshape(n_seq, W_KV), xm_s.reshape(n_seq, W_QX),
                              w_gates, b_gate[l], *branch_weights, w_ffn_down[l], tm=TM_DENSE)
    y_p, y_s = _ffn(xp, z_p, xs, z_s, g_post, g_ffn_pre, w_gate_up, w_down, g_ffn_post, tm=TM_FFN)

    win = min(DIL_GROUPS[-1][0], seq)
    kv_shape = (1, batch, seq, N_KV, HEAD_DIM)
    mem_shape = (1, batch, N_MEM, N_XHEADS, HEAD_DIM)
    new_shape = (1, n_seq, 1, N_KV, HEAD_DIM)
    return (y_p.reshape(batch, seq, D_MODEL), y_s.reshape(n_seq, 1, D_MODEL),
            k_rows.reshape(kv_shape)[:, :, seq - win:], v_rows.reshape(kv_shape)[:, :, seq - win:], conv_p,
            mk.reshape(mem_shape), mv.reshape(mem_shape),
            k_s.reshape(new_shape), v_s.reshape(new_shape), conv_s.reshape(1, n_seq, CONV_PREFIX, C_CONV))
```

```python
import functools

import jax
import jax.numpy as jnp
from jax import lax
from jax.experimental import pallas as pl
from jax.experimental.pallas import tpu as pltpu

F32 = jnp.float32
BF16 = jnp.bfloat16

D_MODEL = 2048
C_CONV = D_MODEL // 2
CONV_WIDTH = 31
CONV_PREFIX = CONV_WIDTH - 1
HEAD_DIM = 128
N_KV = 4
DIL_GROUPS = ((128, 1), (512, 4), (2048, 16))
N_GROUPS = len(DIL_GROUPS)
N_QH = N_GROUPS * N_KV
SPAN = 128
N_MEM = 256
N_XHEADS = 4
W_QD = N_QH * HEAD_DIM
W_KV = N_KV * HEAD_DIM
W_QX = N_XHEADS * HEAD_DIM
D_FF = 5632
EPS = 1e-6
NEG = -1e30
SCALE = HEAD_DIM ** -0.5
LANES = 128
SUBLANES = 8
N_SLABS = C_CONV // LANES
MIB = 1024 * 1024

TN = 512
_A_TILES = C_CONV // TN
_Q_TILES = W_QD // TN
_KV_TILES = W_KV // TN
_QX_TILES = W_QX // TN
_STEP_Q0 = _A_TILES
_STEP_K0 = _STEP_Q0 + _Q_TILES
_STEP_V0 = _STEP_K0 + _KV_TILES
_STEP_X0 = _STEP_V0 + _KV_TILES
_IN_STEPS = _STEP_X0 + _QX_TILES
_GATE_TILE0 = 2 * _A_TILES + _Q_TILES + 2 * _KV_TILES + _QX_TILES
_D_TILES = D_MODEL // TN
_CONV_PAD = -(-CONV_PREFIX // SUBLANES) * SUBLANES
_ATTN_UNITS = 8
_RIDER_STEPS = 2
_GATE_CAST_COLS = 1024
TN_OUT = 512
TF = 512
_FFN_SPLIT = 2
TM_DENSE = 1024
TM_FFN = 512
TM_MEM = 512


def _params(semantics, vmem_mib):
    return pltpu.CompilerParams(dimension_semantics=semantics, vmem_limit_bytes=vmem_mib * MIB)


def _rms_rows(x, g):
    return x * lax.rsqrt(jnp.mean(x * x, axis=-1, keepdims=True) + EPS) * g


def _dot(a, b):
    return jnp.dot(a, b, preferred_element_type=F32)


def _row_ahead(i, s, free_from, n_rows):
    return jnp.minimum(i + (s >= free_from).astype(jnp.int32), n_rows - 1)


def _in_proj_kernel(x_ref, xs_ref, g_ref, w1_ref, w2_ref, h_ref, u_ref, q_ref, k_ref, v_ref, qx_ref,
                    hs_ref, tiles_ref):
    i, s = pl.program_id(0), pl.program_id(1)

    def project(on, x_ref, h_ref, store, u_ref, q_ref, k_ref, v_ref, qx_ref):
        def norm():
            h_ref[...] = _rms_rows(x_ref[...], g_ref[...]).astype(BF16)

        def glu():
            h = h_ref[...]
            store(u_ref, _dot(h, w1_ref[...]) * jax.nn.sigmoid(_dot(h, w2_ref[...])))

        def into(out_ref):
            return lambda: store(out_ref, _dot(h_ref[...], w1_ref[...]))

        pl.when(on & (s == 0))(norm)
        pl.when(on & (s < _STEP_Q0))(glu)
        pl.when(on & (s >= _STEP_Q0) & (s < _STEP_K0))(into(q_ref))
        pl.when(on & (s >= _STEP_K0) & (s < _STEP_V0))(into(k_ref))
        pl.when(on & (s >= _STEP_V0) & (s < _STEP_X0))(into(v_ref))
        pl.when(on & (s >= _STEP_X0))(into(qx_ref))

    def store_slabs(out_ref, tile):
        for c in range(TN // LANES):
            out_ref[c] = tile[:, c * LANES:(c + 1) * LANES].astype(out_ref.dtype)

    def store_rows(out_ref, tile):
        out_ref[...] = tile

    project(True, x_ref, h_ref, store_slabs, u_ref, q_ref, k_ref, v_ref, qx_ref)
    project(i == 0, xs_ref, hs_ref, store_rows, *([tiles_ref] * 5))


def _in_proj(x, xs, g, w_in_t, *, tm):
    m, ms = x.shape[0], xs.shape[0]
    last_a = _A_TILES - 1
    per_tile = TN // LANES

    def out_tiles(step0, n_tiles):
        def index(i, s):
            moved = (s >= step0 + n_tiles) & (i < m // tm - 1)
            return (jnp.where(moved, 0, jnp.clip(s - step0, 0, n_tiles - 1)), i + moved.astype(jnp.int32), 0)
        return pl.BlockSpec((per_tile, tm, LANES), index)

    def slabs(width, dtype):
        return jax.ShapeDtypeStruct((width // LANES, m, LANES), dtype)

    outs = pl.pallas_call(
        _in_proj_kernel,
        grid=(m // tm, _IN_STEPS),
        in_specs=[
            pl.BlockSpec((tm, D_MODEL), lambda i, s: (_row_ahead(i, s, _A_TILES, m // tm), 0)),
            pl.BlockSpec((ms, D_MODEL), lambda i, s: (0, 0)),
            pl.BlockSpec((1, D_MODEL), lambda i, s: (0, 0)),
            pl.BlockSpec((D_MODEL, TN), lambda i, s: (0, jnp.where(s < _A_TILES, s, s + _A_TILES))),
            pl.BlockSpec((D_MODEL, TN), lambda i, s: (0, jnp.where(s < _A_TILES, s, 0) + _A_TILES)),
        ],
        out_specs=[
            pl.BlockSpec((tm, D_MODEL), lambda i, s: (i, 0)),
            pl.BlockSpec((per_tile, tm, LANES), lambda i, s: (jnp.minimum(s, last_a), i, 0)),
            out_tiles(_STEP_Q0, _Q_TILES),
            out_tiles(_STEP_K0, _KV_TILES),
            out_tiles(_STEP_V0, _KV_TILES),
            out_tiles(_STEP_X0, _QX_TILES),
            pl.BlockSpec((ms, D_MODEL), lambda i, s: (0, 0)),
            pl.BlockSpec((ms, TN), lambda i, s: (0, jnp.where(i == 0, s, _IN_STEPS - 1))),
        ],
        out_shape=[
            jax.ShapeDtypeStruct((m, D_MODEL), BF16),
            slabs(C_CONV, F32), slabs(W_QD, F32), slabs(W_KV, F32), slabs(W_KV, F32), slabs(W_QX, BF16),
            jax.ShapeDtypeStruct((ms, D_MODEL), BF16),
            jax.ShapeDtypeStruct((ms, _IN_STEPS * TN), F32),
        ],
        compiler_params=_params(("arbitrary", "arbitrary"), 60),
        name="in_proj",
    )(x, xs, g, w_in_t, w_in_t)
    *prompt, h_s, tiles = outs
    bounds = [step * TN for step in (0, _STEP_Q0, _STEP_K0, _STEP_V0, _STEP_X0, _IN_STEPS)]
    return (*prompt, h_s, *(tiles[:, lo:hi] for lo, hi in zip(bounds, bounds[1:])))


def _mem_kv_kernel(x_ref, g_ref, w_ref, mk_ref, mv_ref, h_ref):
    s = pl.program_id(1)

    @pl.when(s == 0)
    def _():
        h_ref[...] = _rms_rows(x_ref[...], g_ref[...]).astype(BF16)
        mk_ref[...] = _dot(h_ref[...], w_ref[...])

    @pl.when(s == 1)
    def _():
        mv_ref[...] = _dot(h_ref[...], w_ref[...])


def _mem_kv(mem, g, w_t, *, tm):
    m = mem.shape[0]
    return pl.pallas_call(
        _mem_kv_kernel,
        grid=(m // tm, 2),
        in_specs=[
            pl.BlockSpec((tm, D_MODEL), lambda i, s: (i, 0)),
            pl.BlockSpec((1, D_MODEL), lambda i, s: (0, 0)),
            pl.BlockSpec((D_MODEL, W_QX), lambda i, s: (0, s)),
        ],
        out_specs=[
            pl.BlockSpec((tm, W_QX), lambda i, s: (i, 0)),
            pl.BlockSpec((tm, W_QX), lambda i, s: (i, 0)),
        ],
        out_shape=[jax.ShapeDtypeStruct((m, W_QX), F32)] * 2,
        scratch_shapes=[pltpu.VMEM((tm, D_MODEL), BF16)],
        compiler_params=_params(("parallel", "arbitrary"), 32),
        name="mem_kv",
    )(mem, g, w_t)


def _ln_swish_slabs(y_slabs, cb_ref, lng_ref, lnb_ref, out_dtype):
    ys = [y + cb_ref[:, c * LANES:(c + 1) * LANES] for c, y in enumerate(y_slabs)]
    tot = ys[0]
    for y in ys[1:]:
        tot = tot + y
    mu = jnp.sum(tot, axis=-1, keepdims=True) * (1.0 / C_CONV)
    ds = [y - mu for y in ys]
    sq = ds[0] * ds[0]
    for d in ds[1:]:
        sq = sq + d * d
    rs = lax.rsqrt(jnp.sum(sq, axis=-1, keepdims=True) * (1.0 / C_CONV) + EPS)
    outs = []
    for c, d in enumerate(ds):
        z = d * rs * lng_ref[:, c * LANES:(c + 1) * LANES] + lnb_ref[:, c * LANES:(c + 1) * LANES]
        outs.append((z * jax.nn.sigmoid(z)).astype(out_dtype))
    return outs


def _conv_prompt_kernel(u_ref, w_ref, cb_ref, lng_ref, lnb_ref, k_ref, v_ref, *rest, seq, tt, n_cast):
    cast_in, (c_ref, st_ref, kout_ref, vout_ref) = rest[:n_cast], rest[n_cast:n_cast + 4]
    cast_out, (head_ref, y_ref) = rest[n_cast + 4:2 * n_cast + 4], rest[2 * n_cast + 4:]
    half = pl.program_id(1)
    n_tiles = seq // tt // 2
    pad = _CONV_PAD

    for src_ref, dst_ref in ((k_ref, kout_ref), (v_ref, vout_ref)):
        for h in range(N_KV):
            dst_ref[pl.ds(h, src_ref.shape[1], stride=N_KV), :] = src_ref[h]
    for src_ref, dst_ref in zip(cast_in, cast_out):
        dst_ref[...] = src_ref[...].astype(BF16)

    def tile(src_ref, base, t0):
        def slab(c, carry):
            acc = jnp.zeros((tt, LANES), F32)
            for j in range(SUBLANES):
                taps = range(j, CONV_WIDTH, SUBLANES)
                win = src_ref[c, pl.ds(base + j, tt + taps[-1] - j), :]
                for k in taps:
                    acc = acc + win[k - j:k - j + tt] * w_ref[c, k:k + 1, :]
            y_ref[c] = acc
            return carry

        lax.fori_loop(0, N_SLABS, slab, 0, unroll=4)
        ys = [y_ref[c] for c in range(N_SLABS)]
        outs = _ln_swish_slabs(ys, cb_ref, lng_ref, lnb_ref, BF16)
        for c in range(N_SLABS):
            c_ref[pl.ds(t0, tt), c * LANES:(c + 1) * LANES] = outs[c]

    @pl.when(half == 0)
    def _():
        head_ref[:, 0:pad, :] = jnp.zeros((N_SLABS, pad, LANES), F32)
        head_ref[:, pad:pad + tt, :] = u_ref[:, 0:tt, :]
        tile(head_ref, pad - CONV_PREFIX, 0)

    @pl.when(half > 0)
    def _():
        tile(u_ref, pl.multiple_of(half * (n_tiles * tt), tt) - CONV_PREFIX, 0)

    def body(i, carry):
        t0 = pl.multiple_of((half * n_tiles + i) * tt, tt)
        tile(u_ref, t0 - CONV_PREFIX, pl.multiple_of(i * tt, tt))
        return carry

    lax.fori_loop(1, n_tiles, body, 0)
    for c in range(N_SLABS):
        st_ref[:, c * LANES:(c + 1) * LANES] = u_ref[c, seq - CONV_PREFIX:seq, :]


def _conv_prompt(u_slabs, w_slabs, conv_b, ln_g, ln_b, k, v, cast_weights, *, batch, seq):
    tt = 128
    halves = 2
    n_steps = batch * halves
    step = lambda b, j: b * halves + j
    const = lambda *idx: (lambda b, j: idx)

    def row_slice(rows, cols):
        return pl.BlockSpec((rows // n_steps, cols), lambda b, j: (step(b, j), 0))

    kv_in = pl.BlockSpec((N_KV, batch * seq // n_steps, LANES), lambda b, j: (0, step(b, j), 0))
    kv_out = row_slice(batch * seq * N_KV, HEAD_DIM)
    w_specs = [row_slice(*w.shape) for w in cast_weights]
    return pl.pallas_call(
        functools.partial(_conv_prompt_kernel, seq=seq, tt=tt, n_cast=len(cast_weights)),
        grid=(batch, halves),
        in_specs=[
            pl.BlockSpec((N_SLABS, seq, LANES), lambda b, j: (0, b, 0)),
            pl.BlockSpec((N_SLABS, CONV_WIDTH, LANES), const(0, 0, 0)),
            pl.BlockSpec((1, C_CONV), const(0, 0)),
            pl.BlockSpec((1, C_CONV), const(0, 0)),
            pl.BlockSpec((1, C_CONV), const(0, 0)),
            kv_in, kv_in,
        ] + w_specs,
        out_specs=[
            pl.BlockSpec((seq // halves, C_CONV), lambda b, j: (step(b, j), 0)),
            pl.BlockSpec((None, None, CONV_PREFIX, C_CONV), lambda b, j: (0, b, 0, 0)),
            kv_out, kv_out,
        ] + w_specs,
        out_shape=[
            jax.ShapeDtypeStruct((batch * seq, C_CONV), BF16),
            jax.ShapeDtypeStruct((1, batch, CONV_PREFIX, C_CONV), F32),
            jax.ShapeDtypeStruct((batch * seq * N_KV, HEAD_DIM), F32),
            jax.ShapeDtypeStruct((batch * seq * N_KV, HEAD_DIM), F32),
        ] + [jax.ShapeDtypeStruct(w.shape, BF16) for w in cast_weights],
        scratch_shapes=[pltpu.VMEM((N_SLABS, _CONV_PAD + tt, LANES), F32), pltpu.VMEM((N_SLABS, tt, LANES), F32)],
        compiler_params=_params(("parallel", "arbitrary"), 56),
        name="conv_prompt",
    )(u_slabs, w_slabs, conv_b, ln_g, ln_b, k, v, *cast_weights)


def _conv_sample_kernel(state_ref, u_ref, w_ref, cb_ref, lng_ref, lnb_ref, c_ref, st_ref, *, n_seq):
    w_hist = w_ref[0:CONV_PREFIX, :]
    w_new = w_ref[CONV_PREFIX:CONV_WIDTH, :]
    for n in range(n_seq):
        hist = state_ref[n]
        new = u_ref[n:n + 1, :]
        y = jnp.sum(hist * w_hist, axis=0, keepdims=True) + new * w_new
        ys = [y[:, c * LANES:(c + 1) * LANES] for c in range(N_SLABS)]
        outs = _ln_swish_slabs(ys, cb_ref, lng_ref, lnb_ref, F32)
        for c in range(N_SLABS):
            c_ref[n:n + 1, c * LANES:(c + 1) * LANES] = outs[c]
        st_ref[n, 0:CONV_PREFIX - 1, :] = state_ref[n, 1:CONV_PREFIX, :]
        st_ref[n, CONV_PREFIX - 1:CONV_PREFIX, :] = new


def _conv_sample(state, u, conv_w, conv_b, ln_g, ln_b):
    n_seq = state.shape[0]
    return pl.pallas_call(
        functools.partial(_conv_sample_kernel, n_seq=n_seq),
        out_shape=[
            jax.ShapeDtypeStruct((n_seq, C_CONV), F32),
            jax.ShapeDtypeStruct((n_seq, CONV_PREFIX, C_CONV), F32),
        ],
        compiler_params=pltpu.CompilerParams(vmem_limit_bytes=32 * MIB),
        name="conv_sample",
    )(state, u, conv_w, conv_b, ln_g, ln_b)


def _qk(q, k):
    return lax.dot_general(q, k, (((1,), (1,)), ((), ())), preferred_element_type=F32)


def _attn_prompt_kernel(slopes_ref, q0_ref, q1_ref, q2_ref, k_ref, v_ref, qx_ref, mk_ref, mv_ref,
                        wg_ref, wu_ref, mix_ref, xm_ref, wgu_out_ref,
                        bias_ref, o_ref, lse_ref, s_ref, p_ref, m_ref, ones_ref, *, seq):
    head = pl.program_id(1)
    ones_ref[...] = jnp.ones(ones_ref.shape, BF16)
    sub = TF // _FFN_SPLIT
    for j in range(D_FF // sub):
        cols = slice(j * sub, (j + 1) * sub)
        wgu_out_ref[:, 2 * j * sub:(2 * j + 1) * sub] = wg_ref[:, cols].astype(BF16)
        wgu_out_ref[:, (2 * j + 1) * sub:(2 * j + 2) * sub] = wu_ref[:, cols].astype(BF16)
    qi = lax.broadcasted_iota(jnp.int32, (SPAN, 2 * SPAN), 0)
    ci = lax.broadcasted_iota(jnp.int32, (SPAN, 2 * SPAN), 1)
    dist = qi + SPAN - ci
    valid = (dist >= 0) & (dist <= SPAN)
    for g, (_, dil) in enumerate(DIL_GROUPS):
        bias = -slopes_ref[g, head] * (dist * dil).astype(F32)
        bias_ref[g] = jnp.where(valid, bias, NEG)

    def rows(ref, start, size, dil):
        if dil == 1:
            return ref[pl.ds(start, size), :]
        return ref[pl.ds(start, size, stride=dil), :]

    def put(ref, g, start, dil, val):
        if dil == 1:
            ref[g, pl.ds(start, SPAN), :] = val
        else:
            ref[g, pl.ds(start, SPAN, stride=dil), :] = val

    def staged(items, n_keys_of, scores, values, emit):
        for u, item in enumerate(items):
            s_ref[u, :, 0:n_keys_of(item)] = scores(item)
        for u, item in enumerate(items):
            s = s_ref[u, :, 0:n_keys_of(item)]
            m = jnp.max(s, axis=-1, keepdims=True)
            p_ref[u, :, 0:n_keys_of(item)] = jnp.exp(s - m).astype(BF16)
            m_ref[u] = jnp.broadcast_to(m, (SPAN, LANES))
        for u, item in enumerate(items):
            p = p_ref[u, :, 0:n_keys_of(item)]
            l = _dot(p, ones_ref[0:n_keys_of(item), :])
            emit(item, _dot(p, values(item)) * (1.0 / l), m_ref[u] + jnp.log(l))

    def units(g, dil, q_ref, starts_keys):
        def scores(item):
            start, n_keys = item
            q = rows(q_ref, start, SPAN, dil).astype(BF16)
            k = rows(k_ref, start - (n_keys - SPAN) * dil, n_keys, dil).astype(BF16)
            return _qk(q, k) * SCALE + bias_ref[g, :, 2 * SPAN - n_keys:2 * SPAN]

        def values(item):
            start, n_keys = item
            return rows(v_ref, start - (n_keys - SPAN) * dil, n_keys, dil).astype(BF16)

        def emit(item, o, lse):
            put(o_ref, g, item[0], dil, o)
            put(lse_ref, g, item[0], dil, lse)

        staged(starts_keys, lambda item: item[1], scores, values, emit)

    q_refs = (q0_ref, q1_ref, q2_ref)
    for g, (_, dil) in enumerate(DIL_GROUPS):
        n_blocks = seq // dil // SPAN
        q_ref = q_refs[g]
        if n_blocks == 1:
            per_body = _ATTN_UNITS

            def first_blocks(j, carry, g=g, dil=dil, q_ref=q_ref):
                units(g, dil, q_ref, [(j * per_body + u, SPAN) for u in range(per_body)])
                return carry
            lax.fori_loop(0, dil // per_body, first_blocks, 0)
        elif dil == 1:
            per_body = _ATTN_UNITS

            def later_blocks(j, carry, g=g, dil=dil, q_ref=q_ref):
                units(g, dil, q_ref, [(pl.multiple_of((j * per_body + u) * SPAN, SPAN), 2 * SPAN)
                                      for u in range(per_body)])
                return carry
            units(g, dil, q_ref, [(0, SPAN)] + [(n * SPAN, 2 * SPAN) for n in range(1, per_body)])
            lax.fori_loop(1, n_blocks // per_body, later_blocks, 0)
        else:
            streams = _ATTN_UNITS // n_blocks

            def residue(j, carry, g=g, dil=dil, q_ref=q_ref, n_blocks=n_blocks, streams=streams):
                work = []
                for r in range(streams):
                    start = j * streams + r
                    work.append((start, SPAN))
                    work += [(start + n * (SPAN * dil), 2 * SPAN) for n in range(1, n_blocks)]
                units(g, dil, q_ref, work)
                return carry
            lax.fori_loop(0, dil // streams, residue, 0)

    tq = 256

    def mix(j, carry):
        t0 = pl.multiple_of(j * tq, tq)
        l0 = lse_ref[0, pl.ds(t0, tq), :]
        l1 = lse_ref[1, pl.ds(t0, tq), :]
        l2 = lse_ref[2, pl.ds(t0, tq), :]
        top = jnp.maximum(jnp.maximum(l0, l1), l2)
        w0 = jnp.exp(l0 - top)
        w1 = jnp.exp(l1 - top)
        w2 = jnp.exp(l2 - top)
        num = (w0 * o_ref[0, pl.ds(t0, tq), :] + w1 * o_ref[1, pl.ds(t0, tq), :]
               + w2 * o_ref[2, pl.ds(t0, tq), :])
        mix_ref[pl.ds(t0, tq), :] = (num / (w0 + w1 + w2)).astype(BF16)
        return carry

    lax.fori_loop(0, seq // tq, mix, 0, unroll=2)

    def memory(j, carry):
        def scores(t0):
            return _qk(qx_ref[pl.ds(t0, SPAN), :], mk_ref[...].astype(BF16)) * SCALE

        def emit(t0, o, lse):
            xm_ref[pl.ds(t0, SPAN), :] = o.astype(BF16)

        staged([pl.multiple_of((j * _ATTN_UNITS + u) * SPAN, SPAN) for u in range(_ATTN_UNITS)],
               lambda t0: N_MEM, scores, lambda t0: mv_ref[...].astype(BF16), emit)
        return carry

    lax.fori_loop(0, seq // SPAN // _ATTN_UNITS, memory, 0)


def _attn_prompt(slopes, q, k, v, qx, mk, mv, w_gate, w_up, *, batch, seq):
    def col(c):
        return pl.BlockSpec((None, seq, LANES), lambda b, h, c=c: (c * N_KV + h, b, 0))

    slab_spec = col(0)
    head_spec = pl.BlockSpec((None, seq, LANES), lambda b, h: (b, 0, h))
    mem_spec = pl.BlockSpec((None, N_MEM, LANES), lambda b, h: (b, 0, h))
    n_steps = batch * N_KV

    def row_slice(rows, cols):
        return pl.BlockSpec((rows // n_steps, cols), lambda b, h: (b * N_KV + h, 0))

    weights = (w_gate, w_up)
    gu_shape = (D_MODEL, 2 * D_FF)
    return pl.pallas_call(
        functools.partial(_attn_prompt_kernel, seq=seq),
        grid=(batch, N_KV),
        in_specs=[
            pl.BlockSpec(memory_space=pltpu.SMEM),
            col(0), col(1), col(2), slab_spec, slab_spec, slab_spec, mem_spec, mem_spec,
        ] + [row_slice(*w.shape) for w in weights],
        out_specs=[head_spec, head_spec, row_slice(*gu_shape)],
        out_shape=[jax.ShapeDtypeStruct((batch, seq, W_KV), BF16),
                   jax.ShapeDtypeStruct((batch, seq, W_QX), BF16),
                   jax.ShapeDtypeStruct(gu_shape, BF16)],
        scratch_shapes=[
            pltpu.VMEM((N_GROUPS, SPAN, 2 * SPAN), F32),
            pltpu.VMEM((N_GROUPS, seq, LANES), F32),
            pltpu.VMEM((N_GROUPS, seq, LANES), F32),
            pltpu.VMEM((_ATTN_UNITS, SPAN, 2 * SPAN), F32),
            pltpu.VMEM((_ATTN_UNITS, SPAN, 2 * SPAN), BF16),
            pltpu.VMEM((_ATTN_UNITS, SPAN, LANES), F32),
            pltpu.VMEM((2 * SPAN, LANES), BF16),
        ],
        compiler_params=_params(("parallel", "parallel"), 56),
        name="attn_prompt",
    )(slopes, q, q, q, k, v, qx, mk, mv, *weights)


def _attn_sample_kernel(slopes_ref, q_ref, kn_ref, vn_ref, qx_ref, ck0_ref, ck1_ref, ck2_ref,
                        cv0_ref, cv1_ref, cv2_ref, mk_ref, mv_ref, *rest, bs):
    wcol_refs, (mix_ref, xm_ref, wgate_ref) = rest[:-3], rest[-3:]

    def by_row(ref):
        return ref if len(ref.shape) == 3 else ref.reshape(ref.shape[0], ref.shape[1] * ref.shape[2], LANES)

    def head_rows(ref, n, h, n_keys):
        return ref[n, pl.ds(h, n_keys, stride=ref.shape[1] // n_keys), :]

    ck_refs = tuple(by_row(r) for r in (ck0_ref, ck1_ref, ck2_ref))
    cv_refs = tuple(by_row(r) for r in (cv0_ref, cv1_ref, cv2_ref))
    back = (SPAN - lax.broadcasted_iota(jnp.int32, (SPAN, 1), 0)).astype(F32)
    for c, wcol_ref in enumerate(wcol_refs):
        wgate_ref[:, c * _GATE_CAST_COLS:(c + 1) * _GATE_CAST_COLS] = wcol_ref[...].astype(BF16)

    def one(n, carry):
        for h in range(N_KV):
            hs = slice(h * LANES, (h + 1) * LANES)
            kn = kn_ref[n, :, hs]
            vn = vn_ref[n, :, hs]
            scores, s_new = [], []
            for g, (_, dil) in enumerate(DIL_GROUPS):
                qs = slice(g * W_KV + h * LANES, g * W_KV + (h + 1) * LANES)
                q = q_ref[n, :, qs]
                s = jnp.sum(head_rows(ck_refs[g], n, h, SPAN) * q, axis=-1, keepdims=True) * SCALE
                scores.append(s - slopes_ref[g, h] * (back * float(dil)))
                s_new.append(jnp.sum(kn * q, axis=-1, keepdims=True) * SCALE)
            top = s_new[0]
            for g in range(N_GROUPS):
                top = jnp.maximum(top, jnp.maximum(s_new[g], jnp.max(scores[g], axis=0, keepdims=True)))
            num = jnp.zeros((1, LANES), F32)
            den = jnp.zeros((1, 1), F32)
            for g in range(N_GROUPS):
                p = jnp.exp(scores[g] - top)
                p_new = jnp.exp(s_new[g] - top)
                num = num + jnp.sum(p * head_rows(cv_refs[g], n, h, SPAN), axis=0, keepdims=True) + p_new * vn
                den = den + jnp.sum(p, axis=0, keepdims=True) + p_new
            mix_ref[n, :, hs] = num / den
            qx = qx_ref[n, :, hs]
            sx = jnp.sum(head_rows(mk_ref, n, h, N_MEM) * qx, axis=-1, keepdims=True) * SCALE
            px = jnp.exp(sx - jnp.max(sx, axis=0, keepdims=True))
            ox = (jnp.sum(px * head_rows(mv_ref, n, h, N_MEM), axis=0, keepdims=True)
                  / jnp.sum(px, axis=0, keepdims=True))
            xm_ref[n, :, hs] = ox
        return carry

    lax.fori_loop(0, bs, one, 0)


def _attn_sample(slopes, q, kn, vn, qx, cache_k, cache_v, mem_k, mem_v, w_in):
    n_seq, win = cache_k.shape[0], cache_k.shape[1]
    bs = 2
    n_steps = n_seq // bs
    wrows = w_in.shape[0] // n_steps
    gate_col0 = _GATE_TILE0 * TN // _GATE_CAST_COLS
    wcol_specs = [pl.BlockSpec((wrows, _GATE_CAST_COLS), lambda i, c=c: (i, gate_col0 + c))
                  for c in range(3 * D_MODEL // _GATE_CAST_COLS)]
    row = lambda w: pl.BlockSpec((bs, 1, w), lambda i: (i, 0, 0))
    as_rows = lambda a: a.reshape(n_seq, 1, a.shape[-1])
    views_k, views_v, cache_specs = [], [], []
    for _, dil in DIL_GROUPS:
        nb = win // dil // SPAN
        if dil == 1:
            shape = (n_seq, win * N_KV, HEAD_DIM)
            cache_specs.append(pl.BlockSpec((bs, SPAN * N_KV, HEAD_DIM), lambda i, nb=nb: (i, nb - 1, 0)))
        else:
            shape = (n_seq, win // dil, dil * N_KV, HEAD_DIM)
            cache_specs.append(pl.BlockSpec((bs, SPAN, SUBLANES, HEAD_DIM), lambda i, nb=nb: (i, nb - 1, 0, 0)))
        views_k.append(cache_k.reshape(shape))
        views_v.append(cache_v.reshape(shape))
    mem_spec = pl.BlockSpec((bs, N_MEM * N_XHEADS, HEAD_DIM), lambda i: (i, 0, 0))
    mem_rows = lambda a: a.reshape(n_seq, N_MEM * N_XHEADS, HEAD_DIM)
    return pl.pallas_call(
        functools.partial(_attn_sample_kernel, bs=bs),
        grid=(n_seq // bs,),
        in_specs=[pl.BlockSpec(memory_space=pltpu.SMEM), row(W_QD), row(W_KV), row(W_KV), row(W_QX)]
        + cache_specs + cache_specs + [mem_spec, mem_spec] + wcol_specs,
        out_specs=[row(W_KV), row(W_QX), pl.BlockSpec((wrows, 3 * D_MODEL), lambda i: (i, 0))],
        out_shape=[jax.ShapeDtypeStruct((n_seq, 1, W_KV), F32), jax.ShapeDtypeStruct((n_seq, 1, W_QX), F32),
                   jax.ShapeDtypeStruct((w_in.shape[0], 3 * D_MODEL), BF16)],
        compiler_params=_params(("parallel",), 48),
        name="attn_sample",
    )(slopes, as_rows(q), as_rows(kn), as_rows(vn), as_rows(qx), *views_k, *views_v, mem_rows(mem_k),
      mem_rows(mem_v), *([w_in] * len(wcol_specs)))


def _merge_kernel(h_ref, c_ref, mix_ref, xm_ref, hs_ref, cs_ref, mixs_ref, xms_ref, wg0_ref, wg1_ref, wg2_ref,
                  bg_ref, wc_ref, wd_ref, wx_ref, wo_ref, wffn_ref, z_ref, zs_ref, wffn_out_ref,
                  merged_ref, mergeds_ref):
    i, s = pl.program_id(0), pl.program_id(1)

    @pl.when(s % _RIDER_STEPS == 0)
    def _():
        wffn_out_ref[...] = wffn_ref[...].astype(BF16)

    def gated_sum(h_ref, branches, merged_ref):
        h = h_ref[...]
        acc = None
        for br, (wg_ref, y_ref, wy_ref) in enumerate(zip((wg0_ref, wg1_ref, wg2_ref), branches,
                                                         (wc_ref, wd_ref, wx_ref))):
            gate = jax.nn.sigmoid(_dot(h, wg_ref[...]) + bg_ref[br])
            term = gate * _dot(y_ref[...].astype(BF16), wy_ref[...])
            acc = term if acc is None else acc + term
        merged_ref[s] = acc.astype(BF16)

    def project(merged_ref, z_ref):
        z = None
        for j in range(_D_TILES):
            part = _dot(merged_ref[j], wo_ref[j * TN:(j + 1) * TN, :])
            z = part if z is None else z + part
        z_ref[...] = z

    first, second = s < _D_TILES, s >= _D_TILES
    pl.when(first)(lambda: gated_sum(h_ref, (c_ref, mix_ref, xm_ref), merged_ref))
    pl.when(second)(lambda: project(merged_ref, z_ref))
    pl.when(first & (i == 0))(lambda: gated_sum(hs_ref, (cs_ref, mixs_ref, xms_ref), mergeds_ref))
    pl.when(second & (i == 0))(lambda: project(mergeds_ref, zs_ref))


def _merge(h, c, mix, xm, hs, cs, mixs, xms, w_gates, b_gate, w_conv_out_t, w_dil_o_t, w_x_o_t, w_out_t, w_ffn,
           *, tm):
    m, ms = h.shape[0], hs.shape[0]
    n1, n2 = _D_TILES, D_MODEL // TN_OUT

    n_rows = m // tm
    whole = lambda a: pl.BlockSpec(a.shape, lambda i, s: (0, 0))
    rider_blocks = n_rows * (n1 + n2) // _RIDER_STEPS
    rider = pl.BlockSpec((w_ffn.shape[0] // rider_blocks, w_ffn.shape[1]),
                         lambda i, s: ((i * (n1 + n2) + s) // _RIDER_STEPS, 0))

    def tile1(s):
        return jnp.where(s < n1, s, 0)

    def gate_spec(br):
        return pl.BlockSpec((D_MODEL, TN), lambda i, s, br=br: (0, br * _D_TILES + tile1(s)))

    def colw(kdim):
        return pl.BlockSpec((kdim, TN), lambda i, s: (0, tile1(s)))

    def rowblk(w):
        return pl.BlockSpec((tm, w), lambda i, s: (_row_ahead(i, s, n1, n_rows), 0))

    return pl.pallas_call(
        _merge_kernel,
        grid=(m // tm, n1 + n2),
        in_specs=[
            rowblk(D_MODEL), rowblk(C_CONV), rowblk(W_KV), rowblk(W_QX),
            whole(hs), whole(cs), whole(mixs), whole(xms),
            gate_spec(0), gate_spec(1), gate_spec(2),
            pl.BlockSpec((3, 1, TN), lambda i, s: (0, 0, tile1(s))),
            colw(C_CONV), colw(W_KV), colw(W_QX),
            pl.BlockSpec((D_MODEL, TN_OUT), lambda i, s: (0, jnp.maximum(s - n1, 0))),
            rider,
        ],
        out_specs=[
            pl.BlockSpec((tm, TN_OUT), lambda i, s: (i, jnp.maximum(s - n1, 0))),
            pl.BlockSpec((ms, TN_OUT), lambda i, s: (0, jnp.where(i == 0, jnp.maximum(s - n1, 0), n2 - 1))),
            rider,
        ],
        out_shape=[jax.ShapeDtypeStruct((m, D_MODEL), F32), jax.ShapeDtypeStruct((ms, D_MODEL), F32),
                   jax.ShapeDtypeStruct(w_ffn.shape, BF16)],
        scratch_shapes=[pltpu.VMEM((n1, tm, TN), BF16), pltpu.VMEM((n1, ms, TN), BF16)],
        compiler_params=_params(("arbitrary", "arbitrary"), 60),
        name="merge",
    )(h, c, mix, xm, hs, cs, mixs, xms, w_gates, w_gates, w_gates, b_gate.reshape(3, 1, D_MODEL), w_conv_out_t,
      w_dil_o_t, w_x_o_t, w_out_t, w_ffn)


def _ffn_kernel(x_ref, z_ref, xs_ref, zs_ref, gmix_ref, gpre_ref, wgu_ref, wd_ref, gpost_ref, out_ref, outs_ref,
                x1_ref, h_ref, x1s_ref, *, n_steps):
    i, s = pl.program_id(0), pl.program_id(1)
    tm, ms = x_ref.shape[0], xs_ref.shape[0]

    def prologue(x_ref, z_ref, x1_ref, row0, out_ref):
        x1 = x_ref[...] + _rms_rows(z_ref[...], gmix_ref[...])
        x1_ref[...] = x1
        h_ref[row0:row0 + x_ref.shape[0], :] = _rms_rows(x1, gpre_ref[...]).astype(BF16)
        out_ref[...] = jnp.zeros_like(out_ref)

    def step(rows):
        h = h_ref[0:rows, :]
        sub = TF // _FFN_SPLIT
        part = None
        for c in range(_FFN_SPLIT):
            gate_up = _dot(h, wgu_ref[:, 2 * c * sub:2 * (c + 1) * sub])
            gate, up = gate_up[:, :sub], gate_up[:, sub:]
            act = (gate * jax.nn.sigmoid(gate) * up).astype(BF16)
            down = _dot(act, wd_ref[c * sub:(c + 1) * sub, :])
            part = down if part is None else part + down
        out_ref[...] += part[0:tm]
        if rows > tm:
            outs_ref[...] += part[tm:rows]

    def epilogue(x1_ref, out_ref):
        out_ref[...] = x1_ref[...] + _rms_rows(out_ref[...], gpost_ref[...])

    pl.when(s == 0)(lambda: prologue(x_ref, z_ref, x1_ref, 0, out_ref))
    pl.when((i == 0) & (s == 0))(lambda: prologue(xs_ref, zs_ref, x1s_ref, tm, outs_ref))
    pl.when(i == 0)(lambda: step(tm + ms))
    pl.when(i > 0)(lambda: step(tm))
    pl.when(s == n_steps - 1)(lambda: epilogue(x1_ref, out_ref))
    pl.when((i == 0) & (s == n_steps - 1))(lambda: epilogue(x1s_ref, outs_ref))


def _ffn(x, z, xs, zs, g_mix, g_pre, w_gate_up, w_down, g_post, *, tm):
    m, ms = x.shape[0], xs.shape[0]
    n_steps = D_FF // TF
    rowblk = pl.BlockSpec((tm, D_MODEL), lambda i, s: (i, 0))
    sample = pl.BlockSpec((ms, D_MODEL), lambda i, s: (0, 0))
    gain = pl.BlockSpec((1, D_MODEL), lambda i, s: (0, 0))
    return pl.pallas_call(
        functools.partial(_ffn_kernel, n_steps=n_steps),
        grid=(m // tm, n_steps),
        in_specs=[
            rowblk, rowblk, sample, sample, gain, gain,
            pl.BlockSpec((D_MODEL, 2 * TF), lambda i, s: (0, s)),
            pl.BlockSpec((TF, D_MODEL), lambda i, s: (s, 0)),
            gain,
        ],
        out_specs=[rowblk, sample],
        out_shape=[jax.ShapeDtypeStruct((m, D_MODEL), F32), jax.ShapeDtypeStruct((ms, D_MODEL), F32)],
        scratch_shapes=[pltpu.VMEM((tm, D_MODEL), F32), pltpu.VMEM((tm + ms, D_MODEL), BF16),
                        pltpu.VMEM((ms, D_MODEL), F32)],
        compiler_params=_params(("arbitrary", "arbitrary"), 54),
        name="ffn",
    )(x, z, xs, zs, g_mix, g_pre, w_gate_up, w_down, g_post)


def kernel(x_prompt, x_sample, cache_win_k, cache_win_v, state_conv, cache_mem_k, cache_mem_v, mem_prompt,
           g_pre_mix, w_in, b_gate, conv_w, conv_b, conv_ln_g, conv_ln_b, w_conv_out, w_dil_o, g_mem, w_mem_kv,
           w_x_o, w_out, g_post_mix, g_pre_ffn, w_ffn_gate, w_ffn_up, w_ffn_down, g_post_ffn):
    batch, seq, _ = x_prompt.shape
    n_seq = x_sample.shape[0]
    depth = w_in.shape[0]
    assert depth == 1 and x_sample.shape[1] == 1 and seq % (DIL_GROUPS[-1][1] * SPAN) == 0
    assert cache_win_k.shape[2] == DIL_GROUPS[-1][0]

    idx = jnp.arange(1, N_QH + 1, dtype=F32)
    slopes = jnp.exp2(-8.0 * idx / N_QH).reshape(N_GROUPS, N_KV)

    xp, xs = x_prompt.reshape(batch * seq, D_MODEL), x_sample.reshape(n_seq, D_MODEL)
    l = 0
    w_in_t = w_in[l][:, :_GATE_TILE0 * TN].astype(BF16)
    w_mem_kv_t = w_mem_kv[l].astype(BF16)
    row = lambda a: a[l].reshape(1, -1)
    g_pre, g_post, g_ffn_pre, g_ffn_post = row(g_pre_mix), row(g_post_mix), row(g_pre_ffn), row(g_post_ffn)
    cb, lng, lnb = row(conv_b), row(conv_ln_g), row(conv_ln_b)
    conv_w_slabs = conv_w[l].reshape(CONV_WIDTH, N_SLABS, LANES).transpose(1, 0, 2)

    mk, mv = _mem_kv(mem_prompt.reshape(batch * N_MEM, D_MODEL), row(g_mem), w_mem_kv_t, tm=TM_MEM)
    h_p, u_p, q_p, k_p, v_p, qx_p, h_s, u_s, q_s, k_s, v_s, qx_s = _in_proj(xp, xs, g_pre, w_in_t, tm=TM_DENSE)

    c_p, conv_p, k_rows, v_rows, *branch_weights = _conv_prompt(
        u_p, conv_w_slabs, cb, lng, lnb, k_p, v_p, (w_conv_out[l], w_dil_o[l], w_x_o[l], w_out[l]),
        batch=batch, seq=seq)
    mix_p, xm_p, w_gate_up = _attn_prompt(
        slopes, q_p, k_p, v_p, qx_p, mk.reshape(batch, N_MEM, W_QX), mv.reshape(batch, N_MEM, W_QX),
        w_ffn_gate[l], w_ffn_up[l], batch=batch, seq=seq)

    c_s, conv_s = _conv_sample(state_conv[l], u_s, conv_w[l], cb, lng, lnb)
    mix_s, xm_s, w_gates = _attn_sample(slopes, q_s, k_s, v_s, qx_s, cache_win_k[l], cache_win_v[l],
                                        cache_mem_k[l], cache_mem_v[l], w_in[l])

    z_p, z_s, w_down = _merge(h_p, c_p, mix_p.reshape(batch * seq, W_KV), xm_p.reshape(batch * seq, W_QX),
                              h_s, c_s, mix_s.reshape(n_seq, W_KV), xm_s.reshape(n_seq, W_QX),
                              w_gates, b_gate[l], *branch_weights, w_ffn_down[l], tm=TM_DENSE)
    y_p, y_s = _ffn(xp, z_p, xs, z_s, g_post, g_ffn_pre, w_gate_up, w_down, g_ffn_post, tm=TM_FFN)

    win = min(DIL_GROUPS[-1][0], seq)
    kv_shape = (1, batch, seq, N_KV, HEAD_DIM)
    mem_shape = (1, batch, N_MEM, N_XHEADS, HEAD_DIM)
    new_shape = (1, n_seq, 1, N_KV, HEAD_DIM)
    return (y_p.reshape(batch, seq, D_MODEL), y_s.reshape(n_seq, 1, D_MODEL),
            k_rows.reshape(kv_shape)[:, :, seq - win:], v_rows.reshape(kv_shape)[:, :, seq - win:], conv_p,
            mk.reshape(mem_shape), mv.reshape(mem_shape),
            k_s.reshape(new_shape), v_s.reshape(new_shape), conv_s.reshape(1, n_seq, CONV_PREFIX, C_CONV))
```

```python
import functools

import jax
import jax.numpy as jnp
from jax import lax
from jax.experimental import pallas as pl
from jax.experimental.pallas import tpu as pltpu

F32 = jnp.float32
BF16 = jnp.bfloat16

D_MODEL = 2048
C_CONV = D_MODEL // 2
CONV_WIDTH = 31
CONV_PREFIX = CONV_WIDTH - 1
HEAD_DIM = 128
N_KV = 4
DIL_GROUPS = ((128, 1), (512, 4), (2048, 16))
N_GROUPS = len(DIL_GROUPS)
N_QH = N_GROUPS * N_KV
SPAN = 128
N_MEM = 256
N_XHEADS = 4
W_QD = N_QH * HEAD_DIM
W_KV = N_KV * HEAD_DIM
W_QX = N_XHEADS * HEAD_DIM
D_FF = 5632
EPS = 1e-6
NEG = -1e30
SCALE = HEAD_DIM ** -0.5
LANES = 128
SUBLANES = 8
N_SLABS = C_CONV // LANES
MIB = 1024 * 1024

TN = 512
_A_TILES = C_CONV // TN
_Q_TILES = W_QD // TN
_KV_TILES = W_KV // TN
_QX_TILES = W_QX // TN
_STEP_Q0 = _A_TILES
_STEP_K0 = _STEP_Q0 + _Q_TILES
_STEP_V0 = _STEP_K0 + _KV_TILES
_STEP_X0 = _STEP_V0 + _KV_TILES
_IN_STEPS = _STEP_X0 + _QX_TILES
_GATE_TILE0 = 2 * _A_TILES + _Q_TILES + 2 * _KV_TILES + _QX_TILES
_D_TILES = D_MODEL // TN
_CONV_PAD = -(-CONV_PREFIX // SUBLANES) * SUBLANES
_ATTN_UNITS = 8
_RIDER_STEPS = 2
_GATE_CAST_COLS = 1024
TN_OUT = 512
TF = 512
_FFN_SPLIT = 2
TM_DENSE = 1024
TM_FFN = 512
TM_MEM = 512


def _params(semantics, vmem_mib):
    return pltpu.CompilerParams(dimension_semantics=semantics, vmem_limit_bytes=vmem_mib * MIB)


def _rms_rows(x, g):
    return x * lax.rsqrt(jnp.mean(x * x, axis=-1, keepdims=True) + EPS) * g


def _dot(a, b):
    return jnp.dot(a, b, preferred_element_type=F32)


def _row_ahead(i, s, free_from, n_rows):
    return jnp.minimum(i + (s >= free_from).astype(jnp.int32), n_rows - 1)


def _in_proj_kernel(x_ref, xs_ref, g_ref, w1_ref, w2_ref, h_ref, u_ref, q_ref, k_ref, v_ref, qx_ref,
                    hs_ref, tiles_ref):
    i, s = pl.program_id(0), pl.program_id(1)

    def project(on, x_ref, h_ref, store, u_ref, q_ref, k_ref, v_ref, qx_ref):
        def norm():
            h_ref[...] = _rms_rows(x_ref[...], g_ref[...]).astype(BF16)

        def glu():
            h = h_ref[...]
            store(u_ref, _dot(h, w1_ref[...]) * jax.nn.sigmoid(_dot(h, w2_ref[...])))

        def into(out_ref):
            return lambda: store(out_ref, _dot(h_ref[...], w1_ref[...]))

        pl.when(on & (s == 0))(norm)
        pl.when(on & (s < _STEP_Q0))(glu)
        pl.when(on & (s >= _STEP_Q0) & (s < _STEP_K0))(into(q_ref))
        pl.when(on & (s >= _STEP_K0) & (s < _STEP_V0))(into(k_ref))
        pl.when(on & (s >= _STEP_V0) & (s < _STEP_X0))(into(v_ref))
        pl.when(on & (s >= _STEP_X0))(into(qx_ref))

    def store_slabs(out_ref, tile):
        for c in range(TN // LANES):
            out_ref[c] = tile[:, c * LANES:(c + 1) * LANES].astype(out_ref.dtype)

    def store_rows(out_ref, tile):
        out_ref[...] = tile

    project(True, x_ref, h_ref, store_slabs, u_ref, q_ref, k_ref, v_ref, qx_ref)
    project(i == 0, xs_ref, hs_ref, store_rows, *([tiles_ref] * 5))


def _in_proj(x, xs, g, w_in_t, *, tm):
    m, ms = x.shape[0], xs.shape[0]
    last_a = _A_TILES - 1
    per_tile = TN // LANES

    def out_tiles(step0, n_tiles):
        def index(i, s):
            moved = (s >= step0 + n_tiles) & (i < m // tm - 1)
            return (jnp.where(moved, 0, jnp.clip(s - step0, 0, n_tiles - 1)), i + moved.astype(jnp.int32), 0)
        return pl.BlockSpec((per_tile, tm, LANES), index)

    def slabs(width, dtype):
        return jax.ShapeDtypeStruct((width // LANES, m, LANES), dtype)

    outs = pl.pallas_call(
        _in_proj_kernel,
        grid=(m // tm, _IN_STEPS),
        in_specs=[
            pl.BlockSpec((tm, D_MODEL), lambda i, s: (_row_ahead(i, s, _A_TILES, m // tm), 0)),
            pl.BlockSpec((ms, D_MODEL), lambda i, s: (0, 0)),
            pl.BlockSpec((1, D_MODEL), lambda i, s: (0, 0)),
            pl.BlockSpec((D_MODEL, TN), lambda i, s: (0, jnp.where(s < _A_TILES, s, s + _A_TILES))),
            pl.BlockSpec((D_MODEL, TN), lambda i, s: (0, jnp.where(s < _A_TILES, s, 0) + _A_TILES)),
        ],
        out_specs=[
            pl.BlockSpec((tm, D_MODEL), lambda i, s: (i, 0)),
            pl.BlockSpec((per_tile, tm, LANES), lambda i, s: (jnp.minimum(s, last_a), i, 0)),
            out_tiles(_STEP_Q0, _Q_TILES),
            out_tiles(_STEP_K0, _KV_TILES),
            out_tiles(_STEP_V0, _KV_TILES),
            out_tiles(_STEP_X0, _QX_TILES),
            pl.BlockSpec((ms, D_MODEL), lambda i, s: (0, 0)),
            pl.BlockSpec((ms, TN), lambda i, s: (0, jnp.where(i == 0, s, _IN_STEPS - 1))),
        ],
        out_shape=[
            jax.ShapeDtypeStruct((m, D_MODEL), BF16),
            slabs(C_CONV, F32), slabs(W_QD, F32), slabs(W_KV, F32), slabs(W_KV, F32), slabs(W_QX, BF16),
            jax.ShapeDtypeStruct((ms, D_MODEL), BF16),
            jax.ShapeDtypeStruct((ms, _IN_STEPS * TN), F32),
        ],
        compiler_params=_params(("arbitrary", "arbitrary"), 60),
        name="in_proj",
    )(x, xs, g, w_in_t, w_in_t)
    *prompt, h_s, tiles = outs
    bounds = [step * TN for step in (0, _STEP_Q0, _STEP_K0, _STEP_V0, _STEP_X0, _IN_STEPS)]
    return (*prompt, h_s, *(tiles[:, lo:hi] for lo, hi in zip(bounds, bounds[1:])))


def _mem_kv_kernel(x_ref, g_ref, w_ref, mk_ref, mv_ref, h_ref):
    s = pl.program_id(1)

    @pl.when(s == 0)
    def _():
        h_ref[...] = _rms_rows(x_ref[...], g_ref[...]).astype(BF16)
        mk_ref[...] = _dot(h_ref[...], w_ref[...])

    @pl.when(s == 1)
    def _():
        mv_ref[...] = _dot(h_ref[...], w_ref[...])


def _mem_kv(mem, g, w_t, *, tm):
    m = mem.shape[0]
    return pl.pallas_call(
        _mem_kv_kernel,
        grid=(m // tm, 2),
        in_specs=[
            pl.BlockSpec((tm, D_MODEL), lambda i, s: (i, 0)),
            pl.BlockSpec((1, D_MODEL), lambda i, s: (0, 0)),
            pl.BlockSpec((D_MODEL, W_QX), lambda i, s: (0, s)),
        ],
        out_specs=[
            pl.BlockSpec((tm, W_QX), lambda i, s: (i, 0)),
            pl.BlockSpec((tm, W_QX), lambda i, s: (i, 0)),
        ],
        out_shape=[jax.ShapeDtypeStruct((m, W_QX), F32)] * 2,
        scratch_shapes=[pltpu.VMEM((tm, D_MODEL), BF16)],
        compiler_params=_params(("parallel", "arbitrary"), 32),
        name="mem_kv",
    )(mem, g, w_t)


def _ln_swish_slabs(y_slabs, cb_ref, lng_ref, lnb_ref, out_dtype):
    ys = [y + cb_ref[:, c * LANES:(c + 1) * LANES] for c, y in enumerate(y_slabs)]
    tot = ys[0]
    for y in ys[1:]:
        tot = tot + y
    mu = jnp.sum(tot, axis=-1, keepdims=True) * (1.0 / C_CONV)
    ds = [y - mu for y in ys]
    sq = ds[0] * ds[0]
    for d in ds[1:]:
        sq = sq + d * d
    rs = lax.rsqrt(jnp.sum(sq, axis=-1, keepdims=True) * (1.0 / C_CONV) + EPS)
    outs = []
    for c, d in enumerate(ds):
        z = d * rs * lng_ref[:, c * LANES:(c + 1) * LANES] + lnb_ref[:, c * LANES:(c + 1) * LANES]
        outs.append((z * jax.nn.sigmoid(z)).astype(out_dtype))
    return outs


def _conv_prompt_kernel(u_ref, w_ref, cb_ref, lng_ref, lnb_ref, k_ref, v_ref, *rest, seq, tt, n_cast):
    cast_in, (c_ref, st_ref, kout_ref, vout_ref) = rest[:n_cast], rest[n_cast:n_cast + 4]
    cast_out, (head_ref, y_ref) = rest[n_cast + 4:2 * n_cast + 4], rest[2 * n_cast + 4:]
    half = pl.program_id(1)
    n_tiles = seq // tt // 2
    pad = _CONV_PAD

    for src_ref, dst_ref in ((k_ref, kout_ref), (v_ref, vout_ref)):
        for h in range(N_KV):
            dst_ref[pl.ds(h, src_ref.shape[1], stride=N_KV), :] = src_ref[h]
    for src_ref, dst_ref in zip(cast_in, cast_out):
        dst_ref[...] = src_ref[...].astype(BF16)

    def tile(src_ref, base, t0):
        def slab(c, carry):
            acc = jnp.zeros((tt, LANES), F32)
            for j in range(SUBLANES):
                taps = range(j, CONV_WIDTH, SUBLANES)
                win = src_ref[c, pl.ds(base + j, tt + taps[-1] - j), :]
                for k in taps:
                    acc = acc + win[k - j:k - j + tt] * w_ref[c, k:k + 1, :]
            y_ref[c] = acc
            return carry

        lax.fori_loop(0, N_SLABS, slab, 0, unroll=4)
        ys = [y_ref[c] for c in range(N_SLABS)]
        outs = _ln_swish_slabs(ys, cb_ref, lng_ref, lnb_ref, BF16)
        for c in range(N_SLABS):
            c_ref[pl.ds(t0, tt), c * LANES:(c + 1) * LANES] = outs[c]

    @pl.when(half == 0)
    def _():
        head_ref[:, 0:pad, :] = jnp.zeros((N_SLABS, pad, LANES), F32)
        head_ref[:, pad:pad + tt, :] = u_ref[:, 0:tt, :]
        tile(head_ref, pad - CONV_PREFIX, 0)

    @pl.when(half > 0)
    def _():
        tile(u_ref, pl.multiple_of(half * (n_tiles * tt), tt) - CONV_PREFIX, 0)

    def body(i, carry):
        t0 = pl.multiple_of((half * n_tiles + i) * tt, tt)
        tile(u_ref, t0 - CONV_PREFIX, pl.multiple_of(i * tt, tt))
        return carry

    lax.fori_loop(1, n_tiles, body, 0)
    for c in range(N_SLABS):
        st_ref[:, c * LANES:(c + 1) * LANES] = u_ref[c, seq - CONV_PREFIX:seq, :]


def _conv_prompt(u_slabs, w_slabs, conv_b, ln_g, ln_b, k, v, cast_weights, *, batch, seq):
    tt = 128
    halves = 2
    n_steps = batch * halves
    step = lambda b, j: b * halves + j
    const = lambda *idx: (lambda b, j: idx)

    def row_slice(rows, cols):
        return pl.BlockSpec((rows // n_steps, cols), lambda b, j: (step(b, j), 0))

    kv_in = pl.BlockSpec((N_KV, batch * seq // n_steps, LANES), lambda b, j: (0, step(b, j), 0))
    kv_out = row_slice(batch * seq * N_KV, HEAD_DIM)
    w_specs = [row_slice(*w.shape) for w in cast_weights]
    return pl.pallas_call(
        functools.partial(_conv_prompt_kernel, seq=seq, tt=tt, n_cast=len(cast_weights)),
        grid=(batch, halves),
        in_specs=[
            pl.BlockSpec((N_SLABS, seq, LANES), lambda b, j: (0, b, 0)),
            pl.BlockSpec((N_SLABS, CONV_WIDTH, LANES), const(0, 0, 0)),
            pl.BlockSpec((1, C_CONV), const(0, 0)),
            pl.BlockSpec((1, C_CONV), const(0, 0)),
            pl.BlockSpec((1, C_CONV), const(0, 0)),
            kv_in, kv_in,
        ] + w_specs,
        out_specs=[
            pl.BlockSpec((seq // halves, C_CONV), lambda b, j: (step(b, j), 0)),
            pl.BlockSpec((None, None, CONV_PREFIX, C_CONV), lambda b, j: (0, b, 0, 0)),
            kv_out, kv_out,
        ] + w_specs,
        out_shape=[
            jax.ShapeDtypeStruct((batch * seq, C_CONV), BF16),
            jax.ShapeDtypeStruct((1, batch, CONV_PREFIX, C_CONV), F32),
            jax.ShapeDtypeStruct((batch * seq * N_KV, HEAD_DIM), F32),
            jax.ShapeDtypeStruct((batch * seq * N_KV, HEAD_DIM), F32),
        ] + [jax.ShapeDtypeStruct(w.shape, BF16) for w in cast_weights],
        scratch_shapes=[pltpu.VMEM((N_SLABS, _CONV_PAD + tt, LANES), F32), pltpu.VMEM((N_SLABS, tt, LANES), F32)],
        compiler_params=_params(("parallel", "arbitrary"), 56),
        name="conv_prompt",
    )(u_slabs, w_slabs, conv_b, ln_g, ln_b, k, v, *cast_weights)


def _conv_sample_kernel(state_ref, u_ref, w_ref, cb_ref, lng_ref, lnb_ref, c_ref, st_ref, *, n_seq):
    w_hist = w_ref[0:CONV_PREFIX, :]
    w_new = w_ref[CONV_PREFIX:CONV_WIDTH, :]
    for n in range(n_seq):
        hist = state_ref[n]
        new = u_ref[n:n + 1, :]
        y = jnp.sum(hist * w_hist, axis=0, keepdims=True) + new * w_new
        ys = [y[:, c * LANES:(c + 1) * LANES] for c in range(N_SLABS)]
        outs = _ln_swish_slabs(ys, cb_ref, lng_ref, lnb_ref, F32)
        for c in range(N_SLABS):
            c_ref[n:n + 1, c * LANES:(c + 1) * LANES] = outs[c]
        st_ref[n, 0:CONV_PREFIX - 1, :] = state_ref[n, 1:CONV_PREFIX, :]
        st_ref[n, CONV_PREFIX - 1:CONV_PREFIX, :] = new


def _conv_sample(state, u, conv_w, conv_b, ln_g, ln_b):
    n_seq = state.shape[0]
    return pl.pallas_call(
        functools.partial(_conv_sample_kernel, n_seq=n_seq),
        out_shape=[
            jax.ShapeDtypeStruct((n_seq, C_CONV), F32),
            jax.ShapeDtypeStruct((n_seq, CONV_PREFIX, C_CONV), F32),
        ],
        compiler_params=pltpu.CompilerParams(vmem_limit_bytes=32 * MIB),
        name="conv_sample",
    )(state, u, conv_w, conv_b, ln_g, ln_b)


def _qk(q, k):
    return lax.dot_general(q, k, (((1,), (1,)), ((), ())), preferred_element_type=F32)


def _attn_prompt_kernel(slopes_ref, q0_ref, q1_ref, q2_ref, k_ref, v_ref, qx_ref, mk_ref, mv_ref,
                        wg_ref, wu_ref, mix_ref, xm_ref, wgu_out_ref,
                        bias_ref, o_ref, lse_ref, s_ref, p_ref, m_ref, ones_ref, *, seq):
    head = pl.program_id(1)
    ones_ref[...] = jnp.ones(ones_ref.shape, BF16)
    sub = TF // _FFN_SPLIT
    for j in range(D_FF // sub):
        cols = slice(j * sub, (j + 1) * sub)
        wgu_out_ref[:, 2 * j * sub:(2 * j + 1) * sub] = wg_ref[:, cols].astype(BF16)
        wgu_out_ref[:, (2 * j + 1) * sub:(2 * j + 2) * sub] = wu_ref[:, cols].astype(BF16)
    qi = lax.broadcasted_iota(jnp.int32, (SPAN, 2 * SPAN), 0)
    ci = lax.broadcasted_iota(jnp.int32, (SPAN, 2 * SPAN), 1)
    dist = qi + SPAN - ci
    valid = (dist >= 0) & (dist <= SPAN)
    for g, (_, dil) in enumerate(DIL_GROUPS):
        bias = -slopes_ref[g, head] * (dist * dil).astype(F32)
        bias_ref[g] = jnp.where(valid, bias, NEG)

    def rows(ref, start, size, dil):
        if dil == 1:
            return ref[pl.ds(start, size), :]
        return ref[pl.ds(start, size, stride=dil), :]

    def put(ref, g, start, dil, val):
        if dil == 1:
            ref[g, pl.ds(start, SPAN), :] = val
        else:
            ref[g, pl.ds(start, SPAN, stride=dil), :] = val

    def staged(items, n_keys_of, scores, values, emit):
        for u, item in enumerate(items):
            s_ref[u, :, 0:n_keys_of(item)] = scores(item)
        for u, item in enumerate(items):
            s = s_ref[u, :, 0:n_keys_of(item)]
            m = jnp.max(s, axis=-1, keepdims=True)
            p_ref[u, :, 0:n_keys_of(item)] = jnp.exp(s - m).astype(BF16)
            m_ref[u] = jnp.broadcast_to(m, (SPAN, LANES))
        for u, item in enumerate(items):
            p = p_ref[u, :, 0:n_keys_of(item)]
            l = _dot(p, ones_ref[0:n_keys_of(item), :])
            emit(item, _dot(p, values(item)) * (1.0 / l), m_ref[u] + jnp.log(l))

    def units(g, dil, q_ref, starts_keys):
        def scores(item):
            start, n_keys = item
            q = rows(q_ref, start, SPAN, dil).astype(BF16)
            k = rows(k_ref, start - (n_keys - SPAN) * dil, n_keys, dil).astype(BF16)
            return _qk(q, k) * SCALE + bias_ref[g, :, 2 * SPAN - n_keys:2 * SPAN]

        def values(item):
            start, n_keys = item
            return rows(v_ref, start - (n_keys - SPAN) * dil, n_keys, dil).astype(BF16)

        def emit(item, o, lse):
            put(o_ref, g, item[0], dil, o)
            put(lse_ref, g, item[0], dil, lse)

        staged(starts_keys, lambda item: item[1], scores, values, emit)

    q_refs = (q0_ref, q1_ref, q2_ref)
    for g, (_, dil) in enumerate(DIL_GROUPS):
        n_blocks = seq // dil // SPAN
        q_ref = q_refs[g]
        if n_blocks == 1:
            per_body = _ATTN_UNITS

            def first_blocks(j, carry, g=g, dil=dil, q_ref=q_ref):
                units(g, dil, q_ref, [(j * per_body + u, SPAN) for u in range(per_body)])
                return carry
            lax.fori_loop(0, dil // per_body, first_blocks, 0)
        elif dil == 1:
            per_body = _ATTN_UNITS

            def later_blocks(j, carry, g=g, dil=dil, q_ref=q_ref):
                units(g, dil, q_ref, [(pl.multiple_of((j * per_body + u) * SPAN, SPAN), 2 * SPAN)
                                      for u in range(per_body)])
                return carry
            units(g, dil, q_ref, [(0, SPAN)] + [(n * SPAN, 2 * SPAN) for n in range(1, per_body)])
            lax.fori_loop(1, n_blocks // per_body, later_blocks, 0)
        else:
            streams = _ATTN_UNITS // n_blocks

            def residue(j, carry, g=g, dil=dil, q_ref=q_ref, n_blocks=n_blocks, streams=streams):
                work = []
                for r in range(streams):
                    start = j * streams + r
                    work.append((start, SPAN))
                    work += [(start + n * (SPAN * dil), 2 * SPAN) for n in range(1, n_blocks)]
                units(g, dil, q_ref, work)
                return carry
            lax.fori_loop(0, dil // streams, residue, 0)

    tq = 256

    def mix(j, carry):
        t0 = pl.multiple_of(j * tq, tq)
        l0 = lse_ref[0, pl.ds(t0, tq), :]
        l1 = lse_ref[1, pl.ds(t0, tq), :]
        l2 = lse_ref[2, pl.ds(t0, tq), :]
        top = jnp.maximum(jnp.maximum(l0, l1), l2)
        w0 = jnp.exp(l0 - top)
        w1 = jnp.exp(l1 - top)
        w2 = jnp.exp(l2 - top)
        num = (w0 * o_ref[0, pl.ds(t0, tq), :] + w1 * o_ref[1, pl.ds(t0, tq), :]
               + w2 * o_ref[2, pl.ds(t0, tq), :])
        mix_ref[pl.ds(t0, tq), :] = (num / (w0 + w1 + w2)).astype(BF16)
        return carry

    lax.fori_loop(0, seq // tq, mix, 0, unroll=2)

    def memory(j, carry):
        def scores(t0):
            return _qk(qx_ref[pl.ds(t0, SPAN), :], mk_ref[...].astype(BF16)) * SCALE

        def emit(t0, o, lse):
            xm_ref[pl.ds(t0, SPAN), :] = o.astype(BF16)

        staged([pl.multiple_of((j * _ATTN_UNITS + u) * SPAN, SPAN) for u in range(_ATTN_UNITS)],
               lambda t0: N_MEM, scores, lambda t0: mv_ref[...].astype(BF16), emit)
        return carry

    lax.fori_loop(0, seq // SPAN // _ATTN_UNITS, memory, 0)


def _attn_prompt(slopes, q, k, v, qx, mk, mv, w_gate, w_up, *, batch, seq):
    def col(c):
        return pl.BlockSpec((None, seq, LANES), lambda b, h, c=c: (c * N_KV + h, b, 0))

    slab_spec = col(0)
    head_spec = pl.BlockSpec((None, seq, LANES), lambda b, h: (b, 0, h))
    mem_spec = pl.BlockSpec((None, N_MEM, LANES), lambda b, h: (b, 0, h))
    n_steps = batch * N_KV

    def row_slice(rows, cols):
        return pl.BlockSpec((rows // n_steps, cols), lambda b, h: (b * N_KV + h, 0))

    weights = (w_gate, w_up)
    gu_shape = (D_MODEL, 2 * D_FF)
    return pl.pallas_call(
        functools.partial(_attn_prompt_kernel, seq=seq),
        grid=(batch, N_KV),
        in_specs=[
            pl.BlockSpec(memory_space=pltpu.SMEM),
            col(0), col(1), col(2), slab_spec, slab_spec, slab_spec, mem_spec, mem_spec,
        ] + [row_slice(*w.shape) for w in weights],
        out_specs=[head_spec, head_spec, row_slice(*gu_shape)],
        out_shape=[jax.ShapeDtypeStruct((batch, seq, W_KV), BF16),
                   jax.ShapeDtypeStruct((batch, seq, W_QX), BF16),
                   jax.ShapeDtypeStruct(gu_shape, BF16)],
        scratch_shapes=[
            pltpu.VMEM((N_GROUPS, SPAN, 2 * SPAN), F32),
            pltpu.VMEM((N_GROUPS, seq, LANES), F32),
            pltpu.VMEM((N_GROUPS, seq, LANES), F32),
            pltpu.VMEM((_ATTN_UNITS, SPAN, 2 * SPAN), F32),
            pltpu.VMEM((_ATTN_UNITS, SPAN, 2 * SPAN), BF16),
            pltpu.VMEM((_ATTN_UNITS, SPAN, LANES), F32),
            pltpu.VMEM((2 * SPAN, LANES), BF16),
        ],
        compiler_params=_params(("parallel", "parallel"), 56),
        name="attn_prompt",
    )(slopes, q, q, q, k, v, qx, mk, mv, *weights)


def _attn_sample_kernel(slopes_ref, q_ref, kn_ref, vn_ref, qx_ref, ck0_ref, ck1_ref, ck2_ref,
                        cv0_ref, cv1_ref, cv2_ref, mk_ref, mv_ref, *rest, bs):
    wcol_refs, (mix_ref, xm_ref, wgate_ref) = rest[:-3], rest[-3:]

    def by_row(ref):
        return ref if len(ref.shape) == 3 else ref.reshape(ref.shape[0], ref.shape[1] * ref.shape[2], LANES)

    def head_rows(ref, n, h, n_keys):
        return ref[n, pl.ds(h, n_keys, stride=ref.shape[1] // n_keys), :]

    ck_refs = tuple(by_row(r) for r in (ck0_ref, ck1_ref, ck2_ref))
    cv_refs = tuple(by_row(r) for r in (cv0_ref, cv1_ref, cv2_ref))
    back = (SPAN - lax.broadcasted_iota(jnp.int32, (SPAN, 1), 0)).astype(F32)
    for c, wcol_ref in enumerate(wcol_refs):
        wgate_ref[:, c * _GATE_CAST_COLS:(c + 1) * _GATE_CAST_COLS] = wcol_ref[...].astype(BF16)

    def one(n, carry):
        for h in range(N_KV):
            hs = slice(h * LANES, (h + 1) * LANES)
            kn = kn_ref[n, :, hs]
            vn = vn_ref[n, :, hs]
            scores, s_new = [], []
            for g, (_, dil) in enumerate(DIL_GROUPS):
                qs = slice(g * W_KV + h * LANES, g * W_KV + (h + 1) * LANES)
                q = q_ref[n, :, qs]
                s = jnp.sum(head_rows(ck_refs[g], n, h, SPAN) * q, axis=-1, keepdims=True) * SCALE
                scores.append(s - slopes_ref[g, h] * (back * float(dil)))
                s_new.append(jnp.sum(kn * q, axis=-1, keepdims=True) * SCALE)
            top = s_new[0]
            for g in range(N_GROUPS):
                top = jnp.maximum(top, jnp.maximum(s_new[g], jnp.max(scores[g], axis=0, keepdims=True)))
            num = jnp.zeros((1, LANES), F32)
            den = jnp.zeros((1, 1), F32)
            for g in range(N_GROUPS):
                p = jnp.exp(scores[g] - top)
                p_new = jnp.exp(s_new[g] - top)
                num = num + jnp.sum(p * head_rows(cv_refs[g], n, h, SPAN), axis=0, keepdims=True) + p_new * vn
                den = den + jnp.sum(p, axis=0, keepdims=True) + p_new
            mix_ref[n, :, hs] = num / den
            qx = qx_ref[n, :, hs]
            sx = jnp.sum(head_rows(mk_ref, n, h, N_MEM) * qx, axis=-1, keepdims=True) * SCALE
            px = jnp.exp(sx - jnp.max(sx, axis=0, keepdims=True))
            ox = (jnp.sum(px * head_rows(mv_ref, n, h, N_MEM), axis=0, keepdims=True)
                  / jnp.sum(px, axis=0, keepdims=True))
            xm_ref[n, :, hs] = ox
        return carry

    lax.fori_loop(0, bs, one, 0)


def _attn_sample(slopes, q, kn, vn, qx, cache_k, cache_v, mem_k, mem_v, w_in):
    n_seq, win = cache_k.shape[0], cache_k.shape[1]
    bs = 2
    n_steps = n_seq // bs
    wrows = w_in.shape[0] // n_steps
    gate_col0 = _GATE_TILE0 * TN // _GATE_CAST_COLS
    wcol_specs = [pl.BlockSpec((wrows, _GATE_CAST_COLS), lambda i, c=c: (i, gate_col0 + c))
                  for c in range(3 * D_MODEL // _GATE_CAST_COLS)]
    row = lambda w: pl.BlockSpec((bs, 1, w), lambda i: (i, 0, 0))
    as_rows = lambda a: a.reshape(n_seq, 1, a.shape[-1])
    views_k, views_v, cache_specs = [], [], []
    for _, dil in DIL_GROUPS:
        nb = win // dil // SPAN
        if dil == 1:
            shape = (n_seq, win * N_KV, HEAD_DIM)
            cache_specs.append(pl.BlockSpec((bs, SPAN * N_KV, HEAD_DIM), lambda i, nb=nb: (i, nb - 1, 0)))
        else:
            shape = (n_seq, win // dil, dil, N_KV, HEAD_DIM)
            cache_specs.append(pl.BlockSpec((bs, SPAN, None, N_KV, HEAD_DIM),
                                            lambda i, nb=nb: (i, nb - 1, 0, 0, 0)))
        views_k.append(cache_k.reshape(shape))
        views_v.append(cache_v.reshape(shape))
    mem_spec = pl.BlockSpec((bs, N_MEM * N_XHEADS, HEAD_DIM), lambda i: (i, 0, 0))
    mem_rows = lambda a: a.reshape(n_seq, N_MEM * N_XHEADS, HEAD_DIM)
    return pl.pallas_call(
        functools.partial(_attn_sample_kernel, bs=bs),
        grid=(n_seq // bs,),
        in_specs=[pl.BlockSpec(memory_space=pltpu.SMEM), row(W_QD), row(W_KV), row(W_KV), row(W_QX)]
        + cache_specs + cache_specs + [mem_spec, mem_spec] + wcol_specs,
        out_specs=[row(W_KV), row(W_QX), pl.BlockSpec((wrows, 3 * D_MODEL), lambda i: (i, 0))],
        out_shape=[jax.ShapeDtypeStruct((n_seq, 1, W_KV), F32), jax.ShapeDtypeStruct((n_seq, 1, W_QX), F32),
                   jax.ShapeDtypeStruct((w_in.shape[0], 3 * D_MODEL), BF16)],
        compiler_params=_params(("parallel",), 48),
        name="attn_sample",
    )(slopes, as_rows(q), as_rows(kn), as_rows(vn), as_rows(qx), *views_k, *views_v, mem_rows(mem_k),
      mem_rows(mem_v), *([w_in] * len(wcol_specs)))


def _merge_kernel(h_ref, c_ref, mix_ref, xm_ref, hs_ref, cs_ref, mixs_ref, xms_ref, wg0_ref, wg1_ref, wg2_ref,
                  bg_ref, wc_ref, wd_ref, wx_ref, wo_ref, wffn_ref, z_ref, zs_ref, wffn_out_ref,
                  merged_ref, mergeds_ref):
    i, s = pl.program_id(0), pl.program_id(1)

    @pl.when(s % _RIDER_STEPS == 0)
    def _():
        wffn_out_ref[...] = wffn_ref[...].astype(BF16)

    def gated_sum(h_ref, branches, merged_ref):
        h = h_ref[...]
        acc = None
        for br, (wg_ref, y_ref, wy_ref) in enumerate(zip((wg0_ref, wg1_ref, wg2_ref), branches,
                                                         (wc_ref, wd_ref, wx_ref))):
            gate = jax.nn.sigmoid(_dot(h, wg_ref[...]) + bg_ref[br])
            term = gate * _dot(y_ref[...].astype(BF16), wy_ref[...])
            acc = term if acc is None else acc + term
        merged_ref[s] = acc.astype(BF16)

    def project(merged_ref, z_ref):
        z = None
        for j in range(_D_TILES):
            part = _dot(merged_ref[j], wo_ref[j * TN:(j + 1) * TN, :])
            z = part if z is None else z + part
        z_ref[...] = z

    first, second = s < _D_TILES, s >= _D_TILES
    pl.when(first)(lambda: gated_sum(h_ref, (c_ref, mix_ref, xm_ref), merged_ref))
    pl.when(second)(lambda: project(merged_ref, z_ref))
    pl.when(first & (i == 0))(lambda: gated_sum(hs_ref, (cs_ref, mixs_ref, xms_ref), mergeds_ref))
    pl.when(second & (i == 0))(lambda: project(mergeds_ref, zs_ref))


def _merge(h, c, mix, xm, hs, cs, mixs, xms, w_gates, b_gate, w_conv_out_t, w_dil_o_t, w_x_o_t, w_out_t, w_ffn,
           *, tm):
    m, ms = h.shape[0], hs.shape[0]
    n1, n2 = _D_TILES, D_MODEL // TN_OUT

    n_rows = m // tm
    whole = lambda a: pl.BlockSpec(a.shape, lambda i, s: (0, 0))
    rider_blocks = n_rows * (n1 + n2) // _RIDER_STEPS
    rider = pl.BlockSpec((w_ffn.shape[0] // rider_blocks, w_ffn.shape[1]),
                         lambda i, s: ((i * (n1 + n2) + s) // _RIDER_STEPS, 0))

    def tile1(s):
        return jnp.where(s < n1, s, 0)

    def gate_spec(br):
        return pl.BlockSpec((D_MODEL, TN), lambda i, s, br=br: (0, br * _D_TILES + tile1(s)))

    def colw(kdim):
        return pl.BlockSpec((kdim, TN), lambda i, s: (0, tile1(s)))

    def rowblk(w):
        return pl.BlockSpec((tm, w), lambda i, s: (_row_ahead(i, s, n1, n_rows), 0))

    return pl.pallas_call(
        _merge_kernel,
        grid=(m // tm, n1 + n2),
        in_specs=[
            rowblk(D_MODEL), rowblk(C_CONV), rowblk(W_KV), rowblk(W_QX),
            whole(hs), whole(cs), whole(mixs), whole(xms),
            gate_spec(0), gate_spec(1), gate_spec(2),
            pl.BlockSpec((3, 1, TN), lambda i, s: (0, 0, tile1(s))),
            colw(C_CONV), colw(W_KV), colw(W_QX),
            pl.BlockSpec((D_MODEL, TN_OUT), lambda i, s: (0, jnp.maximum(s - n1, 0))),
            rider,
        ],
        out_specs=[
            pl.BlockSpec((tm, TN_OUT), lambda i, s: (i, jnp.maximum(s - n1, 0))),
            pl.BlockSpec((ms, TN_OUT), lambda i, s: (0, jnp.where(i == 0, jnp.maximum(s - n1, 0), n2 - 1))),
            rider,
        ],
        out_shape=[jax.ShapeDtypeStruct((m, D_MODEL), F32), jax.ShapeDtypeStruct((ms, D_MODEL), F32),
                   jax.ShapeDtypeStruct(w_ffn.shape, BF16)],
        scratch_shapes=[pltpu.VMEM((n1, tm, TN), BF16), pltpu.VMEM((n1, ms, TN), BF16)],
        compiler_params=_params(("arbitrary", "arbitrary"), 60),
        name="merge",
    )(h, c, mix, xm, hs, cs, mixs, xms, w_gates, w_gates, w_gates, b_gate.reshape(3, 1, D_MODEL), w_conv_out_t,
      w_dil_o_t, w_x_o_t, w_out_t, w_ffn)


def _ffn_kernel(x_ref, z_ref, xs_ref, zs_ref, gmix_ref, gpre_ref, wgu_ref, wd_ref, gpost_ref, out_ref, outs_ref,
                x1_ref, h_ref, x1s_ref, *, n_steps):
    i, s = pl.program_id(0), pl.program_id(1)
    tm, ms = x_ref.shape[0], xs_ref.shape[0]

    def prologue(x_ref, z_ref, x1_ref, row0, out_ref):
        x1 = x_ref[...] + _rms_rows(z_ref[...], gmix_ref[...])
        x1_ref[...] = x1
        h_ref[row0:row0 + x_ref.shape[0], :] = _rms_rows(x1, gpre_ref[...]).astype(BF16)
        out_ref[...] = jnp.zeros_like(out_ref)

    def step(rows):
        h = h_ref[0:rows, :]
        sub = TF // _FFN_SPLIT
        part = None
        for c in range(_FFN_SPLIT):
            gate_up = _dot(h, wgu_ref[:, 2 * c * sub:2 * (c + 1) * sub])
            gate, up = gate_up[:, :sub], gate_up[:, sub:]
            act = (gate * jax.nn.sigmoid(gate) * up).astype(BF16)
            down = _dot(act, wd_ref[c * sub:(c + 1) * sub, :])
            part = down if part is None else part + down
        out_ref[...] += part[0:tm]
        if rows > tm:
            outs_ref[...] += part[tm:rows]

    def epilogue(x1_ref, out_ref):
        out_ref[...] = x1_ref[...] + _rms_rows(out_ref[...], gpost_ref[...])

    pl.when(s == 0)(lambda: prologue(x_ref, z_ref, x1_ref, 0, out_ref))
    pl.when((i == 0) & (s == 0))(lambda: prologue(xs_ref, zs_ref, x1s_ref, tm, outs_ref))
    pl.when(i == 0)(lambda: step(tm + ms))
    pl.when(i > 0)(lambda: step(tm))
    pl.when(s == n_steps - 1)(lambda: epilogue(x1_ref, out_ref))
    pl.when((i == 0) & (s == n_steps - 1))(lambda: epilogue(x1s_ref, outs_ref))


def _ffn(x, z, xs, zs, g_mix, g_pre, w_gate_up, w_down, g_post, *, tm):
    m, ms = x.shape[0], xs.shape[0]
    n_steps = D_FF // TF
    rowblk = pl.BlockSpec((tm, D_MODEL), lambda i, s: (i, 0))
    sample = pl.BlockSpec((ms, D_MODEL), lambda i, s: (0, 0))
    gain = pl.BlockSpec((1, D_MODEL), lambda i, s: (0, 0))
    return pl.pallas_call(
        functools.partial(_ffn_kernel, n_steps=n_steps),
        grid=(m // tm, n_steps),
        in_specs=[
            rowblk, rowblk, sample, sample, gain, gain,
            pl.BlockSpec((D_MODEL, 2 * TF), lambda i, s: (0, s)),
            pl.BlockSpec((TF, D_MODEL), lambda i, s: (s, 0)),
            gain,
        ],
        out_specs=[rowblk, sample],
        out_shape=[jax.ShapeDtypeStruct((m, D_MODEL), F32), jax.ShapeDtypeStruct((ms, D_MODEL), F32)],
        scratch_shapes=[pltpu.VMEM((tm, D_MODEL), F32), pltpu.VMEM((tm + ms, D_MODEL), BF16),
                        pltpu.VMEM((ms, D_MODEL), F32)],
        compiler_params=_params(("arbitrary", "arbitrary"), 54),
        name="ffn",
    )(x, z, xs, zs, g_mix, g_pre, w_gate_up, w_down, g_post)


def kernel(x_prompt, x_sample, cache_win_k, cache_win_v, state_conv, cache_mem_k, cache_mem_v, mem_prompt,
           g_pre_mix, w_in, b_gate, conv_w, conv_b, conv_ln_g, conv_ln_b, w_conv_out, w_dil_o, g_mem, w_mem_kv,
           w_x_o, w_out, g_post_mix, g_pre_ffn, w_ffn_gate, w_ffn_up, w_ffn_down, g_post_ffn):
    batch, seq, _ = x_prompt.shape
    n_seq = x_sample.shape[0]
    depth = w_in.shape[0]
    assert depth == 1 and x_sample.shape[1] == 1 and seq % (DIL_GROUPS[-1][1] * SPAN) == 0
    assert cache_win_k.shape[2] == DIL_GROUPS[-1][0]

    idx = jnp.arange(1, N_QH + 1, dtype=F32)
    slopes = jnp.exp2(-8.0 * idx / N_QH).reshape(N_GROUPS, N_KV)

    xp, xs = x_prompt.reshape(batch * seq, D_MODEL), x_sample.reshape(n_seq, D_MODEL)
    l = 0
    w_in_t = w_in[l][:, :_GATE_TILE0 * TN].astype(BF16)
    w_mem_kv_t = w_mem_kv[l].astype(BF16)
    row = lambda a: a[l].reshape(1, -1)
    g_pre, g_post, g_ffn_pre, g_ffn_post = row(g_pre_mix), row(g_post_mix), row(g_pre_ffn), row(g_post_ffn)
    cb, lng, lnb = row(conv_b), row(conv_ln_g), row(conv_ln_b)
    conv_w_slabs = conv_w[l].reshape(CONV_WIDTH, N_SLABS, LANES).transpose(1, 0, 2)

    mk, mv = _mem_kv(mem_prompt.reshape(batch * N_MEM, D_MODEL), row(g_mem), w_mem_kv_t, tm=TM_MEM)
    h_p, u_p, q_p, k_p, v_p, qx_p, h_s, u_s, q_s, k_s, v_s, qx_s = _in_proj(xp, xs, g_pre, w_in_t, tm=TM_DENSE)

    c_p, conv_p, k_rows, v_rows, *branch_weights = _conv_prompt(
        u_p, conv_w_slabs, cb, lng, lnb, k_p, v_p, (w_conv_out[l], w_dil_o[l], w_x_o[l], w_out[l]),
        batch=batch, seq=seq)
    mix_p, xm_p, w_gate_up = _attn_prompt(
        slopes, q_p, k_p, v_p, qx_p, mk.reshape(batch, N_MEM, W_QX), mv.reshape(batch, N_MEM, W_QX),
        w_ffn_gate[l], w_ffn_up[l], batch=batch, seq=seq)

    c_s, conv_s = _conv_sample(state_conv[l], u_s, conv_w[l], cb, lng, lnb)
    mix_s, xm_s, w_gates = _attn_sample(slopes, q_s, k_s, v_s, qx_s, cache_win_k[l], cache_win_v[l],
                                        cache_mem_k[l], cache_mem_v[l], w_in[l])

    z_p, z_s, w_down = _merge(h_p, c_p, mix_p.reshape(batch * seq, W_KV), xm_p.reshape(batch * seq, W_QX),
                              h_s, c_s, mix_s.reshape(n_seq, W_KV), xm_s.reshape(n_seq, W_QX),
                              w_gates, b_gate[l], *branch_weights, w_ffn_down[l], tm=TM_DENSE)
    y_p, y_s = _ffn(xp, z_p, xs, z_s, g_post, g_ffn_pre, w_gate_up, w_down, g_ffn_post, tm=TM_FFN)

    win = min(DIL_GROUPS[-1][0], seq)
    kv_shape = (1, batch, seq, N_KV, HEAD_DIM)
    mem_shape = (1, batch, N_MEM, N_XHEADS, HEAD_DIM)
    new_shape = (1, n_seq, 1, N_KV, HEAD_DIM)
    return (y_p.reshape(batch, seq, D_MODEL), y_s.reshape(n_seq, 1, D_MODEL),
            k_rows.reshape(kv_shape)[:, :, seq - win:], v_rows.reshape(kv_shape)[:, :, seq - win:], conv_p,
            mk.reshape(mem_shape), mv.reshape(mem_shape),
            k_s.reshape(new_shape), v_s.reshape(new_shape), conv_s.reshape(1, n_seq, CONV_PREFIX, C_CONV))
```

```python
import functools

import jax
import jax.numpy as jnp
from jax import lax
from jax.experimental import pallas as pl
from jax.experimental.pallas import tpu as pltpu

F32 = jnp.float32
BF16 = jnp.bfloat16

D_MODEL = 2048
C_CONV = D_MODEL // 2
CONV_WIDTH = 31
CONV_PREFIX = CONV_WIDTH - 1
HEAD_DIM = 128
N_KV = 4
DIL_GROUPS = ((128, 1), (512, 4), (2048, 16))
N_GROUPS = len(DIL_GROUPS)
N_QH = N_GROUPS * N_KV
SPAN = 128
N_MEM = 256
N_XHEADS = 4
W_QD = N_QH * HEAD_DIM
W_KV = N_KV * HEAD_DIM
W_QX = N_XHEADS * HEAD_DIM
D_FF = 5632
EPS = 1e-6
NEG = -1e30
SCALE = HEAD_DIM ** -0.5
LANES = 128
SUBLANES = 8
N_SLABS = C_CONV // LANES
MIB = 1024 * 1024

TN = 512
_A_TILES = C_CONV // TN
_Q_TILES = W_QD // TN
_KV_TILES = W_KV // TN
_QX_TILES = W_QX // TN
_STEP_Q0 = _A_TILES
_STEP_K0 = _STEP_Q0 + _Q_TILES
_STEP_V0 = _STEP_K0 + _KV_TILES
_STEP_X0 = _STEP_V0 + _KV_TILES
_IN_STEPS = _STEP_X0 + _QX_TILES
_GATE_TILE0 = 2 * _A_TILES + _Q_TILES + 2 * _KV_TILES + _QX_TILES
_D_TILES = D_MODEL // TN
_CONV_PAD = -(-CONV_PREFIX // SUBLANES) * SUBLANES
_ATTN_UNITS = 8
_RIDER_STEPS = 2
_GATE_CAST_COLS = 1024
TN_OUT = 512
TF = 512
_FFN_SPLIT = 2
TM_DENSE = 1024
TM_FFN = 512
TM_MEM = 512


def _params(semantics, vmem_mib):
    return pltpu.CompilerParams(dimension_semantics=semantics, vmem_limit_bytes=vmem_mib * MIB)


def _rms_rows(x, g):
    return x * lax.rsqrt(jnp.mean(x * x, axis=-1, keepdims=True) + EPS) * g


def _dot(a, b):
    return jnp.dot(a, b, preferred_element_type=F32)


def _row_ahead(i, s, free_from, n_rows):
    return jnp.minimum(i + (s >= free_from).astype(jnp.int32), n_rows - 1)


def _in_proj_kernel(x_ref, xs_ref, g_ref, w1_ref, w2_ref, h_ref, u_ref, q_ref, k_ref, v_ref, qx_ref,
                    hs_ref, tiles_ref):
    i, s = pl.program_id(0), pl.program_id(1)

    def project(on, x_ref, h_ref, store, u_ref, q_ref, k_ref, v_ref, qx_ref):
        def norm():
            h_ref[...] = _rms_rows(x_ref[...], g_ref[...]).astype(BF16)

        def glu():
            h = h_ref[...]
            store(u_ref, _dot(h, w1_ref[...]) * jax.nn.sigmoid(_dot(h, w2_ref[...])))

        def into(out_ref):
            return lambda: store(out_ref, _dot(h_ref[...], w1_ref[...]))

        pl.when(on & (s == 0))(norm)
        pl.when(on & (s < _STEP_Q0))(glu)
        pl.when(on & (s >= _STEP_Q0) & (s < _STEP_K0))(into(q_ref))
        pl.when(on & (s >= _STEP_K0) & (s < _STEP_V0))(into(k_ref))
        pl.when(on & (s >= _STEP_V0) & (s < _STEP_X0))(into(v_ref))
        pl.when(on & (s >= _STEP_X0))(into(qx_ref))

    def store_slabs(out_ref, tile):
        for c in range(TN // LANES):
            out_ref[c] = tile[:, c * LANES:(c + 1) * LANES].astype(out_ref.dtype)

    def store_rows(out_ref, tile):
        out_ref[...] = tile

    project(True, x_ref, h_ref, store_slabs, u_ref, q_ref, k_ref, v_ref, qx_ref)
    project(i == 0, xs_ref, hs_ref, store_rows, *([tiles_ref] * 5))


def _in_proj(x, xs, g, w_in_t, *, tm):
    m, ms = x.shape[0], xs.shape[0]
    last_a = _A_TILES - 1
    per_tile = TN // LANES

    def out_tiles(step0, n_tiles):
        def index(i, s):
            moved = (s >= step0 + n_tiles) & (i < m // tm - 1)
            return (jnp.where(moved, 0, jnp.clip(s - step0, 0, n_tiles - 1)), i + moved.astype(jnp.int32), 0)
        return pl.BlockSpec((per_tile, tm, LANES), index)

    def slabs(width, dtype):
        return jax.ShapeDtypeStruct((width // LANES, m, LANES), dtype)

    outs = pl.pallas_call(
        _in_proj_kernel,
        grid=(m // tm, _IN_STEPS),
        in_specs=[
            pl.BlockSpec((tm, D_MODEL), lambda i, s: (_row_ahead(i, s, _A_TILES, m // tm), 0)),
            pl.BlockSpec((ms, D_MODEL), lambda i, s: (0, 0)),
            pl.BlockSpec((1, D_MODEL), lambda i, s: (0, 0)),
            pl.BlockSpec((D_MODEL, TN), lambda i, s: (0, jnp.where(s < _A_TILES, s, s + _A_TILES))),
            pl.BlockSpec((D_MODEL, TN), lambda i, s: (0, jnp.where(s < _A_TILES, s, 0) + _A_TILES)),
        ],
        out_specs=[
            pl.BlockSpec((tm, D_MODEL), lambda i, s: (i, 0)),
            pl.BlockSpec((per_tile, tm, LANES), lambda i, s: (jnp.minimum(s, last_a), i, 0)),
            out_tiles(_STEP_Q0, _Q_TILES),
            out_tiles(_STEP_K0, _KV_TILES),
            out_tiles(_STEP_V0, _KV_TILES),
            out_tiles(_STEP_X0, _QX_TILES),
            pl.BlockSpec((ms, D_MODEL), lambda i, s: (0, 0)),
            pl.BlockSpec((ms, TN), lambda i, s: (0, jnp.where(i == 0, s, _IN_STEPS - 1))),
        ],
        out_shape=[
            jax.ShapeDtypeStruct((m, D_MODEL), BF16),
            slabs(C_CONV, F32), slabs(W_QD, F32), slabs(W_KV, F32), slabs(W_KV, F32), slabs(W_QX, BF16),
            jax.ShapeDtypeStruct((ms, D_MODEL), BF16),
            jax.ShapeDtypeStruct((ms, _IN_STEPS * TN), F32),
        ],
        compiler_params=_params(("arbitrary", "arbitrary"), 60),
        name="in_proj",
    )(x, xs, g, w_in_t, w_in_t)
    *prompt, h_s, tiles = outs
    bounds = [step * TN for step in (0, _STEP_Q0, _STEP_K0, _STEP_V0, _STEP_X0, _IN_STEPS)]
    return (*prompt, h_s, *(tiles[:, lo:hi] for lo, hi in zip(bounds, bounds[1:])))


def _mem_kv_kernel(x_ref, g_ref, w_ref, mk_ref, mv_ref, h_ref):
    s = pl.program_id(1)

    @pl.when(s == 0)
    def _():
        h_ref[...] = _rms_rows(x_ref[...], g_ref[...]).astype(BF16)
        mk_ref[...] = _dot(h_ref[...], w_ref[...])

    @pl.when(s == 1)
    def _():
        mv_ref[...] = _dot(h_ref[...], w_ref[...])


def _mem_kv(mem, g, w_t, *, tm):
    m = mem.shape[0]
    return pl.pallas_call(
        _mem_kv_kernel,
        grid=(m // tm, 2),
        in_specs=[
            pl.BlockSpec((tm, D_MODEL), lambda i, s: (i, 0)),
            pl.BlockSpec((1, D_MODEL), lambda i, s: (0, 0)),
            pl.BlockSpec((D_MODEL, W_QX), lambda i, s: (0, s)),
        ],
        out_specs=[
            pl.BlockSpec((tm, W_QX), lambda i, s: (i, 0)),
            pl.BlockSpec((tm, W_QX), lambda i, s: (i, 0)),
        ],
        out_shape=[jax.ShapeDtypeStruct((m, W_QX), F32)] * 2,
        scratch_shapes=[pltpu.VMEM((tm, D_MODEL), BF16)],
        compiler_params=_params(("parallel", "arbitrary"), 32),
        name="mem_kv",
    )(mem, g, w_t)


def _ln_swish_slabs(y_slabs, cb_ref, lng_ref, lnb_ref, out_dtype):
    ys = [y + cb_ref[:, c * LANES:(c + 1) * LANES] for c, y in enumerate(y_slabs)]
    tot = ys[0]
    for y in ys[1:]:
        tot = tot + y
    mu = jnp.sum(tot, axis=-1, keepdims=True) * (1.0 / C_CONV)
    ds = [y - mu for y in ys]
    sq = ds[0] * ds[0]
    for d in ds[1:]:
        sq = sq + d * d
    rs = lax.rsqrt(jnp.sum(sq, axis=-1, keepdims=True) * (1.0 / C_CONV) + EPS)
    outs = []
    for c, d in enumerate(ds):
        z = d * rs * lng_ref[:, c * LANES:(c + 1) * LANES] + lnb_ref[:, c * LANES:(c + 1) * LANES]
        outs.append((z * jax.nn.sigmoid(z)).astype(out_dtype))
    return outs


def _conv_prompt_kernel(u_ref, w_ref, cb_ref, lng_ref, lnb_ref, k_ref, v_ref, *rest, seq, tt, n_cast):
    cast_in, (c_ref, st_ref, kout_ref, vout_ref) = rest[:n_cast], rest[n_cast:n_cast + 4]
    cast_out, (head_ref, y_ref) = rest[n_cast + 4:2 * n_cast + 4], rest[2 * n_cast + 4:]
    half = pl.program_id(1)
    n_tiles = seq // tt // 2
    pad = _CONV_PAD

    for src_ref, dst_ref in ((k_ref, kout_ref), (v_ref, vout_ref)):
        for h in range(N_KV):
            dst_ref[pl.ds(h, src_ref.shape[1], stride=N_KV), :] = src_ref[h]
    for src_ref, dst_ref in zip(cast_in, cast_out):
        dst_ref[...] = src_ref[...].astype(BF16)

    def tile(src_ref, base, t0):
        def slab(c, carry):
            acc = jnp.zeros((tt, LANES), F32)
            for j in range(SUBLANES):
                taps = range(j, CONV_WIDTH, SUBLANES)
                win = src_ref[c, pl.ds(base + j, tt + taps[-1] - j), :]
                for k in taps:
                    acc = acc + win[k - j:k - j + tt] * w_ref[c, k:k + 1, :]
            y_ref[c] = acc
            return carry

        lax.fori_loop(0, N_SLABS, slab, 0, unroll=4)
        ys = [y_ref[c] for c in range(N_SLABS)]
        outs = _ln_swish_slabs(ys, cb_ref, lng_ref, lnb_ref, BF16)
        for c in range(N_SLABS):
            c_ref[pl.ds(t0, tt), c * LANES:(c + 1) * LANES] = outs[c]

    @pl.when(half == 0)
    def _():
        head_ref[:, 0:pad, :] = jnp.zeros((N_SLABS, pad, LANES), F32)
        head_ref[:, pad:pad + tt, :] = u_ref[:, 0:tt, :]
        tile(head_ref, pad - CONV_PREFIX, 0)

    @pl.when(half > 0)
    def _():
        tile(u_ref, pl.multiple_of(half * (n_tiles * tt), tt) - CONV_PREFIX, 0)

    def body(i, carry):
        t0 = pl.multiple_of((half * n_tiles + i) * tt, tt)
        tile(u_ref, t0 - CONV_PREFIX, pl.multiple_of(i * tt, tt))
        return carry

    lax.fori_loop(1, n_tiles, body, 0)
    for c in range(N_SLABS):
        st_ref[:, c * LANES:(c + 1) * LANES] = u_ref[c, seq - CONV_PREFIX:seq, :]


def _conv_prompt(u_slabs, w_slabs, conv_b, ln_g, ln_b, k, v, cast_weights, *, batch, seq):
    tt = 128
    halves = 2
    n_steps = batch * halves
    step = lambda b, j: b * halves + j
    const = lambda *idx: (lambda b, j: idx)

    def row_slice(rows, cols):
        return pl.BlockSpec((rows // n_steps, cols), lambda b, j: (step(b, j), 0))

    kv_in = pl.BlockSpec((N_KV, batch * seq // n_steps, LANES), lambda b, j: (0, step(b, j), 0))
    kv_out = row_slice(batch * seq * N_KV, HEAD_DIM)
    w_specs = [row_slice(*w.shape) for w in cast_weights]
    return pl.pallas_call(
        functools.partial(_conv_prompt_kernel, seq=seq, tt=tt, n_cast=len(cast_weights)),
        grid=(batch, halves),
        in_specs=[
            pl.BlockSpec((N_SLABS, seq, LANES), lambda b, j: (0, b, 0)),
            pl.BlockSpec((N_SLABS, CONV_WIDTH, LANES), const(0, 0, 0)),
            pl.BlockSpec((1, C_CONV), const(0, 0)),
            pl.BlockSpec((1, C_CONV), const(0, 0)),
            pl.BlockSpec((1, C_CONV), const(0, 0)),
            kv_in, kv_in,
        ] + w_specs,
        out_specs=[
            pl.BlockSpec((seq // halves, C_CONV), lambda b, j: (step(b, j), 0)),
            pl.BlockSpec((None, None, CONV_PREFIX, C_CONV), lambda b, j: (0, b, 0, 0)),
            kv_out, kv_out,
        ] + w_specs,
        out_shape=[
            jax.ShapeDtypeStruct((batch * seq, C_CONV), BF16),
            jax.ShapeDtypeStruct((1, batch, CONV_PREFIX, C_CONV), F32),
            jax.ShapeDtypeStruct((batch * seq * N_KV, HEAD_DIM), F32),
            jax.ShapeDtypeStruct((batch * seq * N_KV, HEAD_DIM), F32),
        ] + [jax.ShapeDtypeStruct(w.shape, BF16) for w in cast_weights],
        scratch_shapes=[pltpu.VMEM((N_SLABS, _CONV_PAD + tt, LANES), F32), pltpu.VMEM((N_SLABS, tt, LANES), F32)],
        compiler_params=_params(("parallel", "arbitrary"), 56),
        name="conv_prompt",
    )(u_slabs, w_slabs, conv_b, ln_g, ln_b, k, v, *cast_weights)


def _conv_sample_kernel(state_ref, u_ref, w_ref, cb_ref, lng_ref, lnb_ref, c_ref, st_ref, *, n_seq):
    w_hist = w_ref[0:CONV_PREFIX, :]
    w_new = w_ref[CONV_PREFIX:CONV_WIDTH, :]
    for n in range(n_seq):
        hist = state_ref[n]
        new = u_ref[n:n + 1, :]
        y = jnp.sum(hist * w_hist, axis=0, keepdims=True) + new * w_new
        ys = [y[:, c * LANES:(c + 1) * LANES] for c in range(N_SLABS)]
        outs = _ln_swish_slabs(ys, cb_ref, lng_ref, lnb_ref, F32)
        for c in range(N_SLABS):
            c_ref[n:n + 1, c * LANES:(c + 1) * LANES] = outs[c]
        st_ref[n, 0:CONV_PREFIX - 1, :] = state_ref[n, 1:CONV_PREFIX, :]
        st_ref[n, CONV_PREFIX - 1:CONV_PREFIX, :] = new


def _conv_sample(state, u, conv_w, conv_b, ln_g, ln_b):
    n_seq = state.shape[0]
    return pl.pallas_call(
        functools.partial(_conv_sample_kernel, n_seq=n_seq),
        out_shape=[
            jax.ShapeDtypeStruct((n_seq, C_CONV), F32),
            jax.ShapeDtypeStruct((n_seq, CONV_PREFIX, C_CONV), F32),
        ],
        compiler_params=pltpu.CompilerParams(vmem_limit_bytes=32 * MIB),
        name="conv_sample",
    )(state, u, conv_w, conv_b, ln_g, ln_b)


def _qk(q, k):
    return lax.dot_general(q, k, (((1,), (1,)), ((), ())), preferred_element_type=F32)


def _attn_prompt_kernel(slopes_ref, q0_ref, q1_ref, q2_ref, k_ref, v_ref, qx_ref, mk_ref, mv_ref,
                        wg_ref, wu_ref, *rest, seq):
    n_cols = len(rest) - 11
    wcol_refs, (mix_ref, xm_ref, wgu_out_ref, wgate_ref) = rest[:n_cols], rest[n_cols:n_cols + 4]
    bias_ref, o_ref, lse_ref, s_ref, p_ref, m_ref, ones_ref = rest[n_cols + 4:]
    head = pl.program_id(1)
    ones_ref[...] = jnp.ones(ones_ref.shape, BF16)
    for c, wcol_ref in enumerate(wcol_refs):
        wgate_ref[:, c * _GATE_CAST_COLS:(c + 1) * _GATE_CAST_COLS] = wcol_ref[...].astype(BF16)
    sub = TF // _FFN_SPLIT
    for j in range(D_FF // sub):
        cols = slice(j * sub, (j + 1) * sub)
        wgu_out_ref[:, 2 * j * sub:(2 * j + 1) * sub] = wg_ref[:, cols].astype(BF16)
        wgu_out_ref[:, (2 * j + 1) * sub:(2 * j + 2) * sub] = wu_ref[:, cols].astype(BF16)
    qi = lax.broadcasted_iota(jnp.int32, (SPAN, 2 * SPAN), 0)
    ci = lax.broadcasted_iota(jnp.int32, (SPAN, 2 * SPAN), 1)
    dist = qi + SPAN - ci
    valid = (dist >= 0) & (dist <= SPAN)
    for g, (_, dil) in enumerate(DIL_GROUPS):
        bias = -slopes_ref[g, head] * (dist * dil).astype(F32)
        bias_ref[g] = jnp.where(valid, bias, NEG)

    def rows(ref, start, size, dil):
        if dil == 1:
            return ref[pl.ds(start, size), :]
        return ref[pl.ds(start, size, stride=dil), :]

    def put(ref, g, start, dil, val):
        if dil == 1:
            ref[g, pl.ds(start, SPAN), :] = val
        else:
            ref[g, pl.ds(start, SPAN, stride=dil), :] = val

    def staged(items, n_keys_of, scores, values, emit):
        for u, item in enumerate(items):
            s_ref[u, :, 0:n_keys_of(item)] = scores(item)
        for u, item in enumerate(items):
            s = s_ref[u, :, 0:n_keys_of(item)]
            m = jnp.max(s, axis=-1, keepdims=True)
            p_ref[u, :, 0:n_keys_of(item)] = jnp.exp(s - m).astype(BF16)
            m_ref[u] = jnp.broadcast_to(m, (SPAN, LANES))
        for u, item in enumerate(items):
            p = p_ref[u, :, 0:n_keys_of(item)]
            l = _dot(p, ones_ref[0:n_keys_of(item), :])
            emit(item, _dot(p, values(item)) * (1.0 / l), m_ref[u] + jnp.log(l))

    def units(g, dil, q_ref, starts_keys):
        def scores(item):
            start, n_keys = item
            q = rows(q_ref, start, SPAN, dil).astype(BF16)
            k = rows(k_ref, start - (n_keys - SPAN) * dil, n_keys, dil).astype(BF16)
            return _qk(q, k) * SCALE + bias_ref[g, :, 2 * SPAN - n_keys:2 * SPAN]

        def values(item):
            start, n_keys = item
            return rows(v_ref, start - (n_keys - SPAN) * dil, n_keys, dil).astype(BF16)

        def emit(item, o, lse):
            put(o_ref, g, item[0], dil, o)
            put(lse_ref, g, item[0], dil, lse)

        staged(starts_keys, lambda item: item[1], scores, values, emit)

    q_refs = (q0_ref, q1_ref, q2_ref)
    for g, (_, dil) in enumerate(DIL_GROUPS):
        n_blocks = seq // dil // SPAN
        q_ref = q_refs[g]
        if n_blocks == 1:
            per_body = _ATTN_UNITS

            def first_blocks(j, carry, g=g, dil=dil, q_ref=q_ref):
                units(g, dil, q_ref, [(j * per_body + u, SPAN) for u in range(per_body)])
                return carry
            lax.fori_loop(0, dil // per_body, first_blocks, 0)
        elif dil == 1:
            per_body = _ATTN_UNITS

            def later_blocks(j, carry, g=g, dil=dil, q_ref=q_ref):
                units(g, dil, q_ref, [(pl.multiple_of((j * per_body + u) * SPAN, SPAN), 2 * SPAN)
                                      for u in range(per_body)])
                return carry
            units(g, dil, q_ref, [(0, SPAN)] + [(n * SPAN, 2 * SPAN) for n in range(1, per_body)])
            lax.fori_loop(1, n_blocks // per_body, later_blocks, 0)
        else:
            streams = _ATTN_UNITS // n_blocks

            def residue(j, carry, g=g, dil=dil, q_ref=q_ref, n_blocks=n_blocks, streams=streams):
                work = []
                for r in range(streams):
                    start = j * streams + r
                    work.append((start, SPAN))
                    work += [(start + n * (SPAN * dil), 2 * SPAN) for n in range(1, n_blocks)]
                units(g, dil, q_ref, work)
                return carry
            lax.fori_loop(0, dil // streams, residue, 0)

    tq = 256

    def mix(j, carry):
        t0 = pl.multiple_of(j * tq, tq)
        l0 = lse_ref[0, pl.ds(t0, tq), :]
        l1 = lse_ref[1, pl.ds(t0, tq), :]
        l2 = lse_ref[2, pl.ds(t0, tq), :]
        top = jnp.maximum(jnp.maximum(l0, l1), l2)
        w0 = jnp.exp(l0 - top)
        w1 = jnp.exp(l1 - top)
        w2 = jnp.exp(l2 - top)
        num = (w0 * o_ref[0, pl.ds(t0, tq), :] + w1 * o_ref[1, pl.ds(t0, tq), :]
               + w2 * o_ref[2, pl.ds(t0, tq), :])
        mix_ref[pl.ds(t0, tq), :] = (num / (w0 + w1 + w2)).astype(BF16)
        return carry

    lax.fori_loop(0, seq // tq, mix, 0, unroll=2)

    def memory(j, carry):
        def scores(t0):
            return _qk(qx_ref[pl.ds(t0, SPAN), :], mk_ref[...].astype(BF16)) * SCALE

        def emit(t0, o, lse):
            xm_ref[pl.ds(t0, SPAN), :] = o.astype(BF16)

        staged([pl.multiple_of((j * _ATTN_UNITS + u) * SPAN, SPAN) for u in range(_ATTN_UNITS)],
               lambda t0: N_MEM, scores, lambda t0: mv_ref[...].astype(BF16), emit)
        return carry

    lax.fori_loop(0, seq // SPAN // _ATTN_UNITS, memory, 0)


def _attn_prompt(slopes, q, k, v, qx, mk, mv, w_gate, w_up, w_in, *, batch, seq):
    def col(c):
        return pl.BlockSpec((None, seq, LANES), lambda b, h, c=c: (c * N_KV + h, b, 0))

    slab_spec = col(0)
    head_spec = pl.BlockSpec((None, seq, LANES), lambda b, h: (b, 0, h))
    mem_spec = pl.BlockSpec((None, N_MEM, LANES), lambda b, h: (b, 0, h))
    n_steps = batch * N_KV

    def row_slice(rows, cols):
        return pl.BlockSpec((rows // n_steps, cols), lambda b, h: (b * N_KV + h, 0))

    weights = (w_gate, w_up)
    gu_shape = (D_MODEL, 2 * D_FF)
    gates_shape = (w_in.shape[0], 3 * D_MODEL)
    gate_col0 = _GATE_TILE0 * TN // _GATE_CAST_COLS
    wcol_specs = [pl.BlockSpec((w_in.shape[0] // n_steps, _GATE_CAST_COLS),
                               lambda b, h, c=c: (b * N_KV + h, gate_col0 + c))
                  for c in range(gates_shape[1] // _GATE_CAST_COLS)]
    return pl.pallas_call(
        functools.partial(_attn_prompt_kernel, seq=seq),
        grid=(batch, N_KV),
        in_specs=[
            pl.BlockSpec(memory_space=pltpu.SMEM),
            col(0), col(1), col(2), slab_spec, slab_spec, slab_spec, mem_spec, mem_spec,
        ] + [row_slice(*w.shape) for w in weights] + wcol_specs,
        out_specs=[head_spec, head_spec, row_slice(*gu_shape), row_slice(*gates_shape)],
        out_shape=[jax.ShapeDtypeStruct((batch, seq, W_KV), BF16),
                   jax.ShapeDtypeStruct((batch, seq, W_QX), BF16),
                   jax.ShapeDtypeStruct(gu_shape, BF16), jax.ShapeDtypeStruct(gates_shape, BF16)],
        scratch_shapes=[
            pltpu.VMEM((N_GROUPS, SPAN, 2 * SPAN), F32),
            pltpu.VMEM((N_GROUPS, seq, LANES), F32),
            pltpu.VMEM((N_GROUPS, seq, LANES), F32),
            pltpu.VMEM((_ATTN_UNITS, SPAN, 2 * SPAN), F32),
            pltpu.VMEM((_ATTN_UNITS, SPAN, 2 * SPAN), BF16),
            pltpu.VMEM((_ATTN_UNITS, SPAN, LANES), F32),
            pltpu.VMEM((2 * SPAN, LANES), BF16),
        ],
        compiler_params=_params(("parallel", "parallel"), 56),
        name="attn_prompt",
    )(slopes, q, q, q, k, v, qx, mk, mv, *weights, *([w_in] * len(wcol_specs)))


def _attn_sample_kernel(slopes_ref, q_ref, kn_ref, vn_ref, qx_ref, ck0_ref, ck1_ref, ck2_ref,
                        cv0_ref, cv1_ref, cv2_ref, mk_ref, mv_ref, mix_ref, xm_ref, *, bs):

    def by_row(ref):
        return ref if len(ref.shape) == 3 else ref.reshape(ref.shape[0], ref.shape[1] * ref.shape[2], LANES)

    def head_rows(ref, n, h, n_keys):
        return ref[n, pl.ds(h, n_keys, stride=ref.shape[1] // n_keys), :]

    ck_refs = tuple(by_row(r) for r in (ck0_ref, ck1_ref, ck2_ref))
    cv_refs = tuple(by_row(r) for r in (cv0_ref, cv1_ref, cv2_ref))
    back = (SPAN - lax.broadcasted_iota(jnp.int32, (SPAN, 1), 0)).astype(F32)

    def one(n, carry):
        for h in range(N_KV):
            hs = slice(h * LANES, (h + 1) * LANES)
            kn = kn_ref[n, :, hs]
            vn = vn_ref[n, :, hs]
            scores, s_new = [], []
            for g, (_, dil) in enumerate(DIL_GROUPS):
                qs = slice(g * W_KV + h * LANES, g * W_KV + (h + 1) * LANES)
                q = q_ref[n, :, qs]
                s = jnp.sum(head_rows(ck_refs[g], n, h, SPAN) * q, axis=-1, keepdims=True) * SCALE
                scores.append(s - slopes_ref[g, h] * (back * float(dil)))
                s_new.append(jnp.sum(kn * q, axis=-1, keepdims=True) * SCALE)
            top = s_new[0]
            for g in range(N_GROUPS):
                top = jnp.maximum(top, jnp.maximum(s_new[g], jnp.max(scores[g], axis=0, keepdims=True)))
            num = jnp.zeros((1, LANES), F32)
            den = jnp.zeros((1, 1), F32)
            for g in range(N_GROUPS):
                p = jnp.exp(scores[g] - top)
                p_new = jnp.exp(s_new[g] - top)
                num = num + jnp.sum(p * head_rows(cv_refs[g], n, h, SPAN), axis=0, keepdims=True) + p_new * vn
                den = den + jnp.sum(p, axis=0, keepdims=True) + p_new
            mix_ref[n, :, hs] = num / den
            qx = qx_ref[n, :, hs]
            sx = jnp.sum(head_rows(mk_ref, n, h, N_MEM) * qx, axis=-1, keepdims=True) * SCALE
            px = jnp.exp(sx - jnp.max(sx, axis=0, keepdims=True))
            ox = (jnp.sum(px * head_rows(mv_ref, n, h, N_MEM), axis=0, keepdims=True)
                  / jnp.sum(px, axis=0, keepdims=True))
            xm_ref[n, :, hs] = ox
        return carry

    lax.fori_loop(0, bs, one, 0)


def _attn_sample(slopes, q, kn, vn, qx, cache_k, cache_v, mem_k, mem_v):
    n_seq, win = cache_k.shape[0], cache_k.shape[1]
    bs = 2
    row = lambda w: pl.BlockSpec((bs, 1, w), lambda i: (i, 0, 0))
    as_rows = lambda a: a.reshape(n_seq, 1, a.shape[-1])
    views_k, views_v, cache_specs = [], [], []
    for _, dil in DIL_GROUPS:
        nb = win // dil // SPAN
        if dil == 1:
            shape = (n_seq, win * N_KV, HEAD_DIM)
            cache_specs.append(pl.BlockSpec((bs, SPAN * N_KV, HEAD_DIM), lambda i, nb=nb: (i, nb - 1, 0)))
        else:
            shape = (n_seq, win // dil, dil, N_KV, HEAD_DIM)
            cache_specs.append(pl.BlockSpec((bs, SPAN, None, N_KV, HEAD_DIM),
                                            lambda i, nb=nb: (i, nb - 1, 0, 0, 0)))
        views_k.append(cache_k.reshape(shape))
        views_v.append(cache_v.reshape(shape))
    mem_spec = pl.BlockSpec((bs, N_MEM * N_XHEADS, HEAD_DIM), lambda i: (i, 0, 0))
    mem_rows = lambda a: a.reshape(n_seq, N_MEM * N_XHEADS, HEAD_DIM)
    return pl.pallas_call(
        functools.partial(_attn_sample_kernel, bs=bs),
        grid=(n_seq // bs,),
        in_specs=[pl.BlockSpec(memory_space=pltpu.SMEM), row(W_QD), row(W_KV), row(W_KV), row(W_QX)]
        + cache_specs + cache_specs + [mem_spec, mem_spec],
        out_specs=[row(W_KV), row(W_QX)],
        out_shape=[jax.ShapeDtypeStruct((n_seq, 1, W_KV), F32), jax.ShapeDtypeStruct((n_seq, 1, W_QX), F32)],
        compiler_params=_params(("parallel",), 40),
        name="attn_sample",
    )(slopes, as_rows(q), as_rows(kn), as_rows(vn), as_rows(qx), *views_k, *views_v, mem_rows(mem_k),
      mem_rows(mem_v))


def _merge_kernel(h_ref, c_ref, mix_ref, xm_ref, hs_ref, cs_ref, mixs_ref, xms_ref, wg0_ref, wg1_ref, wg2_ref,
                  bg_ref, wc_ref, wd_ref, wx_ref, wo_ref, wffn_ref, z_ref, zs_ref, wffn_out_ref,
                  merged_ref, mergeds_ref):
    i, s = pl.program_id(0), pl.program_id(1)

    @pl.when(s % _RIDER_STEPS == 0)
    def _():
        wffn_out_ref[...] = wffn_ref[...].astype(BF16)

    def gated_sum(h_ref, branches, merged_ref):
        h = h_ref[...]
        acc = None
        for br, (wg_ref, y_ref, wy_ref) in enumerate(zip((wg0_ref, wg1_ref, wg2_ref), branches,
                                                         (wc_ref, wd_ref, wx_ref))):
            gate = jax.nn.sigmoid(_dot(h, wg_ref[...]) + bg_ref[br])
            term = gate * _dot(y_ref[...].astype(BF16), wy_ref[...])
            acc = term if acc is None else acc + term
        merged_ref[s] = acc.astype(BF16)

    def project(merged_ref, z_ref):
        z = None
        for j in range(_D_TILES):
            part = _dot(merged_ref[j], wo_ref[j * TN:(j + 1) * TN, :])
            z = part if z is None else z + part
        z_ref[...] = z

    first, second = s < _D_TILES, s >= _D_TILES
    pl.when(first)(lambda: gated_sum(h_ref, (c_ref, mix_ref, xm_ref), merged_ref))
    pl.when(second)(lambda: project(merged_ref, z_ref))
    pl.when(first & (i == 0))(lambda: gated_sum(hs_ref, (cs_ref, mixs_ref, xms_ref), mergeds_ref))
    pl.when(second & (i == 0))(lambda: project(mergeds_ref, zs_ref))


def _merge(h, c, mix, xm, hs, cs, mixs, xms, w_gates, b_gate, w_conv_out_t, w_dil_o_t, w_x_o_t, w_out_t, w_ffn,
           *, tm):
    m, ms = h.shape[0], hs.shape[0]
    n1, n2 = _D_TILES, D_MODEL // TN_OUT

    n_rows = m // tm
    whole = lambda a: pl.BlockSpec(a.shape, lambda i, s: (0, 0))
    rider_blocks = n_rows * (n1 + n2) // _RIDER_STEPS
    rider = pl.BlockSpec((w_ffn.shape[0] // rider_blocks, w_ffn.shape[1]),
                         lambda i, s: ((i * (n1 + n2) + s) // _RIDER_STEPS, 0))

    def tile1(s):
        return jnp.where(s < n1, s, 0)

    def gate_spec(br):
        return pl.BlockSpec((D_MODEL, TN), lambda i, s, br=br: (0, br * _D_TILES + tile1(s)))

    def colw(kdim):
        return pl.BlockSpec((kdim, TN), lambda i, s: (0, tile1(s)))

    def rowblk(w):
        return pl.BlockSpec((tm, w), lambda i, s: (_row_ahead(i, s, n1, n_rows), 0))

    return pl.pallas_call(
        _merge_kernel,
        grid=(m // tm, n1 + n2),
        in_specs=[
            rowblk(D_MODEL), rowblk(C_CONV), rowblk(W_KV), rowblk(W_QX),
            whole(hs), whole(cs), whole(mixs), whole(xms),
            gate_spec(0), gate_spec(1), gate_spec(2),
            pl.BlockSpec((3, 1, TN), lambda i, s: (0, 0, tile1(s))),
            colw(C_CONV), colw(W_KV), colw(W_QX),
            pl.BlockSpec((D_MODEL, TN_OUT), lambda i, s: (0, jnp.maximum(s - n1, 0))),
            rider,
        ],
        out_specs=[
            pl.BlockSpec((tm, TN_OUT), lambda i, s: (i, jnp.maximum(s - n1, 0))),
            pl.BlockSpec((ms, TN_OUT), lambda i, s: (0, jnp.where(i == 0, jnp.maximum(s - n1, 0), n2 - 1))),
            rider,
        ],
        out_shape=[jax.ShapeDtypeStruct((m, D_MODEL), F32), jax.ShapeDtypeStruct((ms, D_MODEL), F32),
                   jax.ShapeDtypeStruct(w_ffn.shape, BF16)],
        scratch_shapes=[pltpu.VMEM((n1, tm, TN), BF16), pltpu.VMEM((n1, ms, TN), BF16)],
        compiler_params=_params(("arbitrary", "arbitrary"), 60),
        name="merge",
    )(h, c, mix, xm, hs, cs, mixs, xms, w_gates, w_gates, w_gates, b_gate.reshape(3, 1, D_MODEL), w_conv_out_t,
      w_dil_o_t, w_x_o_t, w_out_t, w_ffn)


def _ffn_kernel(x_ref, z_ref, xs_ref, zs_ref, gmix_ref, gpre_ref, wgu_ref, wd_ref, gpost_ref, out_ref, outs_ref,
                x1_ref, h_ref, x1s_ref, *, n_steps):
    i, s = pl.program_id(0), pl.program_id(1)
    tm, ms = x_ref.shape[0], xs_ref.shape[0]

    def prologue(x_ref, z_ref, x1_ref, row0, out_ref):
        x1 = x_ref[...] + _rms_rows(z_ref[...], gmix_ref[...])
        x1_ref[...] = x1
        h_ref[row0:row0 + x_ref.shape[0], :] = _rms_rows(x1, gpre_ref[...]).astype(BF16)
        out_ref[...] = jnp.zeros_like(out_ref)

    def step(rows):
        h = h_ref[0:rows, :]
        sub = TF // _FFN_SPLIT
        part = None
        for c in range(_FFN_SPLIT):
            gate_up = _dot(h, wgu_ref[:, 2 * c * sub:2 * (c + 1) * sub])
            gate, up = gate_up[:, :sub], gate_up[:, sub:]
            act = (gate * jax.nn.sigmoid(gate) * up).astype(BF16)
            down = _dot(act, wd_ref[c * sub:(c + 1) * sub, :])
            part = down if part is None else part + down
        out_ref[...] += part[0:tm]
        if rows > tm:
            outs_ref[...] += part[tm:rows]

    def epilogue(x1_ref, out_ref):
        out_ref[...] = x1_ref[...] + _rms_rows(out_ref[...], gpost_ref[...])

    pl.when(s == 0)(lambda: prologue(x_ref, z_ref, x1_ref, 0, out_ref))
    pl.when((i == 0) & (s == 0))(lambda: prologue(xs_ref, zs_ref, x1s_ref, tm, outs_ref))
    pl.when(i == 0)(lambda: step(tm + ms))
    pl.when(i > 0)(lambda: step(tm))
    pl.when(s == n_steps - 1)(lambda: epilogue(x1_ref, out_ref))
    pl.when((i == 0) & (s == n_steps - 1))(lambda: epilogue(x1s_ref, outs_ref))


def _ffn(x, z, xs, zs, g_mix, g_pre, w_gate_up, w_down, g_post, *, tm):
    m, ms = x.shape[0], xs.shape[0]
    n_steps = D_FF // TF
    rowblk = pl.BlockSpec((tm, D_MODEL), lambda i, s: (i, 0))
    sample = pl.BlockSpec((ms, D_MODEL), lambda i, s: (0, 0))
    gain = pl.BlockSpec((1, D_MODEL), lambda i, s: (0, 0))
    return pl.pallas_call(
        functools.partial(_ffn_kernel, n_steps=n_steps),
        grid=(m // tm, n_steps),
        in_specs=[
            rowblk, rowblk, sample, sample, gain, gain,
            pl.BlockSpec((D_MODEL, 2 * TF), lambda i, s: (0, s)),
            pl.BlockSpec((TF, D_MODEL), lambda i, s: (s, 0)),
            gain,
        ],
        out_specs=[rowblk, sample],
        out_shape=[jax.ShapeDtypeStruct((m, D_MODEL), F32), jax.ShapeDtypeStruct((ms, D_MODEL), F32)],
        scratch_shapes=[pltpu.VMEM((tm, D_MODEL), F32), pltpu.VMEM((tm + ms, D_MODEL), BF16),
                        pltpu.VMEM((ms, D_MODEL), F32)],
        compiler_params=_params(("arbitrary", "arbitrary"), 54),
        name="ffn",
    )(x, z, xs, zs, g_mix, g_pre, w_gate_up, w_down, g_post)


def kernel(x_prompt, x_sample, cache_win_k, cache_win_v, state_conv, cache_mem_k, cache_mem_v, mem_prompt,
           g_pre_mix, w_in, b_gate, conv_w, conv_b, conv_ln_g, conv_ln_b, w_conv_out, w_dil_o, g_mem, w_mem_kv,
           w_x_o, w_out, g_post_mix, g_pre_ffn, w_ffn_gate, w_ffn_up, w_ffn_down, g_post_ffn):
    batch, seq, _ = x_prompt.shape
    n_seq = x_sample.shape[0]
    depth = w_in.shape[0]
    assert depth == 1 and x_sample.shape[1] == 1 and seq % (DIL_GROUPS[-1][1] * SPAN) == 0
    assert cache_win_k.shape[2] == DIL_GROUPS[-1][0]

    idx = jnp.arange(1, N_QH + 1, dtype=F32)
    slopes = jnp.exp2(-8.0 * idx / N_QH).reshape(N_GROUPS, N_KV)

    xp, xs = x_prompt.reshape(batch * seq, D_MODEL), x_sample.reshape(n_seq, D_MODEL)
    l = 0
    w_in_t = w_in[l][:, :_GATE_TILE0 * TN].astype(BF16)
    w_mem_kv_t = w_mem_kv[l].astype(BF16)
    row = lambda a: a[l].reshape(1, -1)
    g_pre, g_post, g_ffn_pre, g_ffn_post = row(g_pre_mix), row(g_post_mix), row(g_pre_ffn), row(g_post_ffn)
    cb, lng, lnb = row(conv_b), row(conv_ln_g), row(conv_ln_b)
    conv_w_slabs = conv_w[l].reshape(CONV_WIDTH, N_SLABS, LANES).transpose(1, 0, 2)

    mk, mv = _mem_kv(mem_prompt.reshape(batch * N_MEM, D_MODEL), row(g_mem), w_mem_kv_t, tm=TM_MEM)
    h_p, u_p, q_p, k_p, v_p, qx_p, h_s, u_s, q_s, k_s, v_s, qx_s = _in_proj(xp, xs, g_pre, w_in_t, tm=TM_DENSE)

    c_p, conv_p, k_rows, v_rows, *branch_weights = _conv_prompt(
        u_p, conv_w_slabs, cb, lng, lnb, k_p, v_p, (w_conv_out[l], w_dil_o[l], w_x_o[l], w_out[l]),
        batch=batch, seq=seq)
    mix_p, xm_p, w_gate_up, w_gates = _attn_prompt(
        slopes, q_p, k_p, v_p, qx_p, mk.reshape(batch, N_MEM, W_QX), mv.reshape(batch, N_MEM, W_QX),
        w_ffn_gate[l], w_ffn_up[l], w_in[l], batch=batch, seq=seq)

    c_s, conv_s = _conv_sample(state_conv[l], u_s, conv_w[l], cb, lng, lnb)
    mix_s, xm_s = _attn_sample(slopes, q_s, k_s, v_s, qx_s, cache_win_k[l], cache_win_v[l],
                               cache_mem_k[l], cache_mem_v[l])

    z_p, z_s, w_down = _merge(h_p, c_p, mix_p.reshape(batch * seq, W_KV), xm_p.reshape(batch * seq, W_QX),
                              h_s, c_s, mix_s.reshape(n_seq, W_KV), xm_s.reshape(n_seq, W_QX),
                              w_gates, b_gate[l], *branch_weights, w_ffn_down[l], tm=TM_DENSE)
    y_p, y_s = _ffn(xp, z_p, xs, z_s, g_post, g_ffn_pre, w_gate_up, w_down, g_ffn_post, tm=TM_FFN)

    win = min(DIL_GROUPS[-1][0], seq)
    kv_shape = (1, batch, seq, N_KV, HEAD_DIM)
    mem_shape = (1, batch, N_MEM, N_XHEADS, HEAD_DIM)
    new_shape = (1, n_seq, 1, N_KV, HEAD_DIM)
    return (y_p.reshape(batch, seq, D_MODEL), y_s.reshape(n_seq, 1, D_MODEL),
            k_rows.reshape(kv_shape)[:, :, seq - win:], v_rows.reshape(kv_shape)[:, :, seq - win:], conv_p,
            mk.reshape(mem_shape), mv.reshape(mem_shape),
            k_s.reshape(new_shape), v_s.reshape(new_shape), conv_s.reshape(1, n_seq, CONV_PREFIX, C_CONV))
```

```python
import functools

import jax
import jax.numpy as jnp
from jax import lax
from jax.experimental import pallas as pl
from jax.experimental.pallas import tpu as pltpu

F32 = jnp.float32
BF16 = jnp.bfloat16

D_MODEL = 2048
C_CONV = D_MODEL // 2
CONV_WIDTH = 31
CONV_PREFIX = CONV_WIDTH - 1
HEAD_DIM = 128
N_KV = 4
DIL_GROUPS = ((128, 1), (512, 4), (2048, 16))
N_GROUPS = len(DIL_GROUPS)
N_QH = N_GROUPS * N_KV
SPAN = 128
N_MEM = 256
N_XHEADS = 4
W_QD = N_QH * HEAD_DIM
W_KV = N_KV * HEAD_DIM
W_QX = N_XHEADS * HEAD_DIM
D_FF = 5632
EPS = 1e-6
NEG = -1e30
SCALE = HEAD_DIM ** -0.5
LANES = 128
SUBLANES = 8
N_SLABS = C_CONV // LANES
MIB = 1024 * 1024

TN = 512
_A_TILES = C_CONV // TN
_Q_TILES = W_QD // TN
_KV_TILES = W_KV // TN
_QX_TILES = W_QX // TN
_STEP_Q0 = _A_TILES
_STEP_K0 = _STEP_Q0 + _Q_TILES
_STEP_V0 = _STEP_K0 + _KV_TILES
_STEP_X0 = _STEP_V0 + _KV_TILES
_IN_STEPS = _STEP_X0 + _QX_TILES
_GATE_TILE0 = 2 * _A_TILES + _Q_TILES + 2 * _KV_TILES + _QX_TILES
_D_TILES = D_MODEL // TN
_CONV_PAD = -(-CONV_PREFIX // SUBLANES) * SUBLANES
_ATTN_UNITS = 8
_RIDER_STEPS = 2
_GATE_CAST_COLS = 1024
TN_OUT = 512
TF = 512
_FFN_SPLIT = 2
TM_DENSE = 1024
TM_FFN = 512
TM_MEM = 512


def _params(semantics, vmem_mib):
    return pltpu.CompilerParams(dimension_semantics=semantics, vmem_limit_bytes=vmem_mib * MIB)


def _rms_rows(x, g):
    return x * lax.rsqrt(jnp.mean(x * x, axis=-1, keepdims=True) + EPS) * g


def _dot(a, b):
    return jnp.dot(a, b, preferred_element_type=F32)


def _row_ahead(i, s, free_from, n_rows):
    return jnp.minimum(i + (s >= free_from).astype(jnp.int32), n_rows - 1)


def _in_proj_kernel(x_ref, xs_ref, g_ref, w1_ref, w2_ref, h_ref, u_ref, q_ref, k_ref, v_ref, qx_ref,
                    hs_ref, tiles_ref):
    i, s = pl.program_id(0), pl.program_id(1)

    def project(on, x_ref, h_ref, store, u_ref, q_ref, k_ref, v_ref, qx_ref):
        def norm():
            h_ref[...] = _rms_rows(x_ref[...], g_ref[...]).astype(BF16)

        def glu():
            h = h_ref[...]
            store(u_ref, _dot(h, w1_ref[...]) * jax.nn.sigmoid(_dot(h, w2_ref[...])))

        def into(out_ref):
            return lambda: store(out_ref, _dot(h_ref[...], w1_ref[...]))

        pl.when(on & (s == 0))(norm)
        pl.when(on & (s < _STEP_Q0))(glu)
        pl.when(on & (s >= _STEP_Q0) & (s < _STEP_K0))(into(q_ref))
        pl.when(on & (s >= _STEP_K0) & (s < _STEP_V0))(into(k_ref))
        pl.when(on & (s >= _STEP_V0) & (s < _STEP_X0))(into(v_ref))
        pl.when(on & (s >= _STEP_X0))(into(qx_ref))

    def store_slabs(out_ref, tile):
        for c in range(TN // LANES):
            out_ref[c] = tile[:, c * LANES:(c + 1) * LANES].astype(out_ref.dtype)

    def store_rows(out_ref, tile):
        out_ref[...] = tile

    project(True, x_ref, h_ref, store_slabs, u_ref, q_ref, k_ref, v_ref, qx_ref)
    project(i == 0, xs_ref, hs_ref, store_rows, *([tiles_ref] * 5))


def _in_proj(x, xs, g, w_in_t, *, tm):
    m, ms = x.shape[0], xs.shape[0]
    last_a = _A_TILES - 1
    per_tile = TN // LANES

    def out_tiles(step0, n_tiles):
        def index(i, s):
            moved = (s >= step0 + n_tiles) & (i < m // tm - 1)
            return (jnp.where(moved, 0, jnp.clip(s - step0, 0, n_tiles - 1)), i + moved.astype(jnp.int32), 0)
        return pl.BlockSpec((per_tile, tm, LANES), index)

    def slabs(width, dtype):
        return jax.ShapeDtypeStruct((width // LANES, m, LANES), dtype)

    outs = pl.pallas_call(
        _in_proj_kernel,
        grid=(m // tm, _IN_STEPS),
        in_specs=[
            pl.BlockSpec((tm, D_MODEL), lambda i, s: (_row_ahead(i, s, _A_TILES, m // tm), 0)),
            pl.BlockSpec((ms, D_MODEL), lambda i, s: (0, 0)),
            pl.BlockSpec((1, D_MODEL), lambda i, s: (0, 0)),
            pl.BlockSpec((D_MODEL, TN), lambda i, s: (0, jnp.where(s < _A_TILES, s, s + _A_TILES))),
            pl.BlockSpec((D_MODEL, TN), lambda i, s: (0, jnp.where(s < _A_TILES, s, 0) + _A_TILES)),
        ],
        out_specs=[
            pl.BlockSpec((tm, D_MODEL), lambda i, s: (i, 0)),
            pl.BlockSpec((per_tile, tm, LANES), lambda i, s: (jnp.minimum(s, last_a), i, 0)),
            out_tiles(_STEP_Q0, _Q_TILES),
            out_tiles(_STEP_K0, _KV_TILES),
            out_tiles(_STEP_V0, _KV_TILES),
            out_tiles(_STEP_X0, _QX_TILES),
            pl.BlockSpec((ms, D_MODEL), lambda i, s: (0, 0)),
            pl.BlockSpec((ms, TN), lambda i, s: (0, jnp.where(i == 0, s, _IN_STEPS - 1))),
        ],
        out_shape=[
            jax.ShapeDtypeStruct((m, D_MODEL), BF16),
            slabs(C_CONV, F32), slabs(W_QD, F32), slabs(W_KV, F32), slabs(W_KV, F32), slabs(W_QX, BF16),
            jax.ShapeDtypeStruct((ms, D_MODEL), BF16),
            jax.ShapeDtypeStruct((ms, _IN_STEPS * TN), F32),
        ],
        compiler_params=_params(("arbitrary", "arbitrary"), 60),
        name="in_proj",
    )(x, xs, g, w_in_t, w_in_t)
    *prompt, h_s, tiles = outs
    bounds = [step * TN for step in (0, _STEP_Q0, _STEP_K0, _STEP_V0, _STEP_X0, _IN_STEPS)]
    return (*prompt, h_s, *(tiles[:, lo:hi] for lo, hi in zip(bounds, bounds[1:])))


def _mem_kv_kernel(x_ref, g_ref, w_ref, mk_ref, mv_ref, h_ref):
    s = pl.program_id(1)

    @pl.when(s == 0)
    def _():
        h_ref[...] = _rms_rows(x_ref[...], g_ref[...]).astype(BF16)
        mk_ref[...] = _dot(h_ref[...], w_ref[...])

    @pl.when(s == 1)
    def _():
        mv_ref[...] = _dot(h_ref[...], w_ref[...])


def _mem_kv(mem, g, w_t, *, tm):
    m = mem.shape[0]
    return pl.pallas_call(
        _mem_kv_kernel,
        grid=(m // tm, 2),
        in_specs=[
            pl.BlockSpec((tm, D_MODEL), lambda i, s: (i, 0)),
            pl.BlockSpec((1, D_MODEL), lambda i, s: (0, 0)),
            pl.BlockSpec((D_MODEL, W_QX), lambda i, s: (0, s)),
        ],
        out_specs=[
            pl.BlockSpec((tm, W_QX), lambda i, s: (i, 0)),
            pl.BlockSpec((tm, W_QX), lambda i, s: (i, 0)),
        ],
        out_shape=[jax.ShapeDtypeStruct((m, W_QX), F32)] * 2,
        scratch_shapes=[pltpu.VMEM((tm, D_MODEL), BF16)],
        compiler_params=_params(("parallel", "arbitrary"), 32),
        name="mem_kv",
    )(mem, g, w_t)


def _ln_swish_slabs(y_slabs, cb_ref, lng_ref, lnb_ref, out_dtype):
    ys = [y + cb_ref[:, c * LANES:(c + 1) * LANES] for c, y in enumerate(y_slabs)]
    tot = ys[0]
    for y in ys[1:]:
        tot = tot + y
    mu = jnp.sum(tot, axis=-1, keepdims=True) * (1.0 / C_CONV)
    ds = [y - mu for y in ys]
    sq = ds[0] * ds[0]
    for d in ds[1:]:
        sq = sq + d * d
    rs = lax.rsqrt(jnp.sum(sq, axis=-1, keepdims=True) * (1.0 / C_CONV) + EPS)
    outs = []
    for c, d in enumerate(ds):
        z = d * rs * lng_ref[:, c * LANES:(c + 1) * LANES] + lnb_ref[:, c * LANES:(c + 1) * LANES]
        outs.append((z * jax.nn.sigmoid(z)).astype(out_dtype))
    return outs


def _conv_prompt_kernel(u_ref, w_ref, cb_ref, lng_ref, lnb_ref, k_ref, v_ref, *rest, seq, tt, n_cast):
    cast_in, (c_ref, st_ref, kout_ref, vout_ref) = rest[:n_cast], rest[n_cast:n_cast + 4]
    cast_out, (head_ref, y_ref) = rest[n_cast + 4:2 * n_cast + 4], rest[2 * n_cast + 4:]
    half = pl.program_id(1)
    n_tiles = seq // tt // 2
    pad = _CONV_PAD

    for src_ref, dst_ref in ((k_ref, kout_ref), (v_ref, vout_ref)):
        for h in range(N_KV):
            dst_ref[pl.ds(h, src_ref.shape[1], stride=N_KV), :] = src_ref[h]
    for src_ref, dst_ref in zip(cast_in, cast_out):
        dst_ref[...] = src_ref[...].astype(BF16)

    def tile(src_ref, base, t0):
        def slab(c, carry):
            acc = jnp.zeros((tt, LANES), F32)
            for j in range(SUBLANES):
                taps = range(j, CONV_WIDTH, SUBLANES)
                win = src_ref[c, pl.ds(base + j, tt + taps[-1] - j), :]
                for k in taps:
                    acc = acc + win[k - j:k - j + tt] * w_ref[c, k:k + 1, :]
            y_ref[c] = acc
            return carry

        lax.fori_loop(0, N_SLABS, slab, 0, unroll=4)
        ys = [y_ref[c] for c in range(N_SLABS)]
        outs = _ln_swish_slabs(ys, cb_ref, lng_ref, lnb_ref, BF16)
        for c in range(N_SLABS):
            c_ref[pl.ds(t0, tt), c * LANES:(c + 1) * LANES] = outs[c]

    @pl.when(half == 0)
    def _():
        head_ref[:, 0:pad, :] = jnp.zeros((N_SLABS, pad, LANES), F32)
        head_ref[:, pad:pad + tt, :] = u_ref[:, 0:tt, :]
        tile(head_ref, pad - CONV_PREFIX, 0)

    @pl.when(half > 0)
    def _():
        tile(u_ref, pl.multiple_of(half * (n_tiles * tt), tt) - CONV_PREFIX, 0)

    def body(i, carry):
        t0 = pl.multiple_of((half * n_tiles + i) * tt, tt)
        tile(u_ref, t0 - CONV_PREFIX, pl.multiple_of(i * tt, tt))
        return carry

    lax.fori_loop(1, n_tiles, body, 0)
    for c in range(N_SLABS):
        st_ref[:, c * LANES:(c + 1) * LANES] = u_ref[c, seq - CONV_PREFIX:seq, :]


def _conv_prompt(u_slabs, w_slabs, conv_b, ln_g, ln_b, k, v, cast_weights, *, batch, seq):
    tt = 128
    halves = 2
    n_steps = batch * halves
    step = lambda b, j: b * halves + j
    const = lambda *idx: (lambda b, j: idx)

    def row_slice(rows, cols):
        return pl.BlockSpec((rows // n_steps, cols), lambda b, j: (step(b, j), 0))

    kv_in = pl.BlockSpec((N_KV, batch * seq // n_steps, LANES), lambda b, j: (0, step(b, j), 0))
    kv_out = row_slice(batch * seq * N_KV, HEAD_DIM)
    w_specs = [row_slice(*w.shape) for w in cast_weights]
    return pl.pallas_call(
        functools.partial(_conv_prompt_kernel, seq=seq, tt=tt, n_cast=len(cast_weights)),
        grid=(batch, halves),
        in_specs=[
            pl.BlockSpec((N_SLABS, seq, LANES), lambda b, j: (0, b, 0)),
            pl.BlockSpec((N_SLABS, CONV_WIDTH, LANES), const(0, 0, 0)),
            pl.BlockSpec((1, C_CONV), const(0, 0)),
            pl.BlockSpec((1, C_CONV), const(0, 0)),
            pl.BlockSpec((1, C_CONV), const(0, 0)),
            kv_in, kv_in,
        ] + w_specs,
        out_specs=[
            pl.BlockSpec((seq // halves, C_CONV), lambda b, j: (step(b, j), 0)),
            pl.BlockSpec((None, None, CONV_PREFIX, C_CONV), lambda b, j: (0, b, 0, 0)),
            kv_out, kv_out,
        ] + w_specs,
        out_shape=[
            jax.ShapeDtypeStruct((batch * seq, C_CONV), BF16),
            jax.ShapeDtypeStruct((1, batch, CONV_PREFIX, C_CONV), F32),
            jax.ShapeDtypeStruct((batch * seq * N_KV, HEAD_DIM), F32),
            jax.ShapeDtypeStruct((batch * seq * N_KV, HEAD_DIM), F32),
        ] + [jax.ShapeDtypeStruct(w.shape, BF16) for w in cast_weights],
        scratch_shapes=[pltpu.VMEM((N_SLABS, _CONV_PAD + tt, LANES), F32), pltpu.VMEM((N_SLABS, tt, LANES), F32)],
        compiler_params=_params(("parallel", "arbitrary"), 56),
        name="conv_prompt",
    )(u_slabs, w_slabs, conv_b, ln_g, ln_b, k, v, *cast_weights)


def _conv_sample_kernel(state_ref, u_ref, w_ref, cb_ref, lng_ref, lnb_ref, c_ref, st_ref):
    new = u_ref[...]
    y = new * w_ref[CONV_PREFIX:CONV_WIDTH, :]
    for k in range(CONV_PREFIX):
        y = y + state_ref[k] * w_ref[k:k + 1, :]
    ys = [y[:, c * LANES:(c + 1) * LANES] for c in range(N_SLABS)]
    outs = _ln_swish_slabs(ys, cb_ref, lng_ref, lnb_ref, F32)
    for c in range(N_SLABS):
        c_ref[:, c * LANES:(c + 1) * LANES] = outs[c]
    st_ref[0:CONV_PREFIX - 1] = state_ref[1:CONV_PREFIX]
    st_ref[CONV_PREFIX - 1] = new


def _conv_sample(state_t, u, conv_w, conv_b, ln_g, ln_b):
    n_seq = state_t.shape[1]
    return pl.pallas_call(
        _conv_sample_kernel,
        out_shape=[
            jax.ShapeDtypeStruct((n_seq, C_CONV), F32),
            jax.ShapeDtypeStruct((CONV_PREFIX, n_seq, C_CONV), F32),
        ],
        compiler_params=pltpu.CompilerParams(vmem_limit_bytes=32 * MIB),
        name="conv_sample",
    )(state_t, u, conv_w, conv_b, ln_g, ln_b)


def _qk(q, k):
    return lax.dot_general(q, k, (((1,), (1,)), ((), ())), preferred_element_type=F32)


def _attn_prompt_kernel(slopes_ref, q0_ref, q1_ref, q2_ref, k_ref, v_ref, qx_ref, mk_ref, mv_ref,
                        wg_ref, wu_ref, *rest, seq):
    n_cols = len(rest) - 11
    wcol_refs, (mix_ref, xm_ref, wgu_out_ref, wgate_ref) = rest[:n_cols], rest[n_cols:n_cols + 4]
    bias_ref, o_ref, lse_ref, s_ref, p_ref, m_ref, ones_ref = rest[n_cols + 4:]
    head = pl.program_id(1)
    ones_ref[...] = jnp.ones(ones_ref.shape, BF16)
    for c, wcol_ref in enumerate(wcol_refs):
        wgate_ref[:, c * _GATE_CAST_COLS:(c + 1) * _GATE_CAST_COLS] = wcol_ref[...].astype(BF16)
    sub = TF // _FFN_SPLIT
    for j in range(D_FF // sub):
        cols = slice(j * sub, (j + 1) * sub)
        wgu_out_ref[:, 2 * j * sub:(2 * j + 1) * sub] = wg_ref[:, cols].astype(BF16)
        wgu_out_ref[:, (2 * j + 1) * sub:(2 * j + 2) * sub] = wu_ref[:, cols].astype(BF16)
    qi = lax.broadcasted_iota(jnp.int32, (SPAN, 2 * SPAN), 0)
    ci = lax.broadcasted_iota(jnp.int32, (SPAN, 2 * SPAN), 1)
    dist = qi + SPAN - ci
    valid = (dist >= 0) & (dist <= SPAN)
    for g, (_, dil) in enumerate(DIL_GROUPS):
        bias = -slopes_ref[g, head] * (dist * dil).astype(F32)
        bias_ref[g] = jnp.where(valid, bias, NEG)

    def rows(ref, start, size, dil):
        if dil == 1:
            return ref[pl.ds(start, size), :]
        return ref[pl.ds(start, size, stride=dil), :]

    def put(ref, g, start, dil, val):
        if dil == 1:
            ref[g, pl.ds(start, SPAN), :] = val
        else:
            ref[g, pl.ds(start, SPAN, stride=dil), :] = val

    def staged(items, n_keys_of, scores, values, emit):
        for u, item in enumerate(items):
            s_ref[u, :, 0:n_keys_of(item)] = scores(item)
        for u, item in enumerate(items):
            s = s_ref[u, :, 0:n_keys_of(item)]
            m = jnp.max(s, axis=-1, keepdims=True)
            p_ref[u, :, 0:n_keys_of(item)] = jnp.exp(s - m).astype(BF16)
            m_ref[u] = jnp.broadcast_to(m, (SPAN, LANES))
        for u, item in enumerate(items):
            p = p_ref[u, :, 0:n_keys_of(item)]
            l = _dot(p, ones_ref[0:n_keys_of(item), :])
            emit(item, _dot(p, values(item)) * (1.0 / l), m_ref[u] + jnp.log(l))

    def units(g, dil, q_ref, starts_keys):
        def scores(item):
            start, n_keys = item
            q = rows(q_ref, start, SPAN, dil).astype(BF16)
            k = rows(k_ref, start - (n_keys - SPAN) * dil, n_keys, dil).astype(BF16)
            return _qk(q, k) * SCALE + bias_ref[g, :, 2 * SPAN - n_keys:2 * SPAN]

        def values(item):
            start, n_keys = item
            return rows(v_ref, start - (n_keys - SPAN) * dil, n_keys, dil).astype(BF16)

        def emit(item, o, lse):
            put(o_ref, g, item[0], dil, o)
            put(lse_ref, g, item[0], dil, lse)

        staged(starts_keys, lambda item: item[1], scores, values, emit)

    q_refs = (q0_ref, q1_ref, q2_ref)
    for g, (_, dil) in enumerate(DIL_GROUPS):
        n_blocks = seq // dil // SPAN
        q_ref = q_refs[g]
        if n_blocks == 1:
            per_body = _ATTN_UNITS

            def first_blocks(j, carry, g=g, dil=dil, q_ref=q_ref):
                units(g, dil, q_ref, [(j * per_body + u, SPAN) for u in range(per_body)])
                return carry
            lax.fori_loop(0, dil // per_body, first_blocks, 0)
        elif dil == 1:
            per_body = _ATTN_UNITS

            def later_blocks(j, carry, g=g, dil=dil, q_ref=q_ref):
                units(g, dil, q_ref, [(pl.multiple_of((j * per_body + u) * SPAN, SPAN), 2 * SPAN)
                                      for u in range(per_body)])
                return carry
            units(g, dil, q_ref, [(0, SPAN)] + [(n * SPAN, 2 * SPAN) for n in range(1, per_body)])
            lax.fori_loop(1, n_blocks // per_body, later_blocks, 0)
        else:
            streams = _ATTN_UNITS // n_blocks

            def residue(j, carry, g=g, dil=dil, q_ref=q_ref, n_blocks=n_blocks, streams=streams):
                work = []
                for r in range(streams):
                    start = j * streams + r
                    work.append((start, SPAN))
                    work += [(start + n * (SPAN * dil), 2 * SPAN) for n in range(1, n_blocks)]
                units(g, dil, q_ref, work)
                return carry
            lax.fori_loop(0, dil // streams, residue, 0)

    tq = 256

    def mix(j, carry):
        t0 = pl.multiple_of(j * tq, tq)
        l0 = lse_ref[0, pl.ds(t0, tq), :]
        l1 = lse_ref[1, pl.ds(t0, tq), :]
        l2 = lse_ref[2, pl.ds(t0, tq), :]
        top = jnp.maximum(jnp.maximum(l0, l1), l2)
        w0 = jnp.exp(l0 - top)
        w1 = jnp.exp(l1 - top)
        w2 = jnp.exp(l2 - top)
        num = (w0 * o_ref[0, pl.ds(t0, tq), :] + w1 * o_ref[1, pl.ds(t0, tq), :]
               + w2 * o_ref[2, pl.ds(t0, tq), :])
        mix_ref[pl.ds(t0, tq), :] = (num / (w0 + w1 + w2)).astype(BF16)
        return carry

    lax.fori_loop(0, seq // tq, mix, 0, unroll=2)

    def memory(j, carry):
        def scores(t0):
            return _qk(qx_ref[pl.ds(t0, SPAN), :], mk_ref[...].astype(BF16)) * SCALE

        def emit(t0, o, lse):
            xm_ref[pl.ds(t0, SPAN), :] = o.astype(BF16)

        staged([pl.multiple_of((j * _ATTN_UNITS + u) * SPAN, SPAN) for u in range(_ATTN_UNITS)],
               lambda t0: N_MEM, scores, lambda t0: mv_ref[...].astype(BF16), emit)
        return carry

    lax.fori_loop(0, seq // SPAN // _ATTN_UNITS, memory, 0)


def _attn_prompt(slopes, q, k, v, qx, mk, mv, w_gate, w_up, w_in, *, batch, seq):
    def col(c):
        return pl.BlockSpec((None, seq, LANES), lambda b, h, c=c: (c * N_KV + h, b, 0))

    slab_spec = col(0)
    head_spec = pl.BlockSpec((None, seq, LANES), lambda b, h: (b, 0, h))
    mem_spec = pl.BlockSpec((None, N_MEM, LANES), lambda b, h: (b, 0, h))
    n_steps = batch * N_KV

    def row_slice(rows, cols):
        return pl.BlockSpec((rows // n_steps, cols), lambda b, h: (b * N_KV + h, 0))

    weights = (w_gate, w_up)
    gu_shape = (D_MODEL, 2 * D_FF)
    gates_shape = (w_in.shape[0], 3 * D_MODEL)
    gate_col0 = _GATE_TILE0 * TN // _GATE_CAST_COLS
    wcol_specs = [pl.BlockSpec((w_in.shape[0] // n_steps, _GATE_CAST_COLS),
                               lambda b, h, c=c: (b * N_KV + h, gate_col0 + c))
                  for c in range(gates_shape[1] // _GATE_CAST_COLS)]
    return pl.pallas_call(
        functools.partial(_attn_prompt_kernel, seq=seq),
        grid=(batch, N_KV),
        in_specs=[
            pl.BlockSpec(memory_space=pltpu.SMEM),
            col(0), col(1), col(2), slab_spec, slab_spec, slab_spec, mem_spec, mem_spec,
        ] + [row_slice(*w.shape) for w in weights] + wcol_specs,
        out_specs=[head_spec, head_spec, row_slice(*gu_shape), row_slice(*gates_shape)],
        out_shape=[jax.ShapeDtypeStruct((batch, seq, W_KV), BF16),
                   jax.ShapeDtypeStruct((batch, seq, W_QX), BF16),
                   jax.ShapeDtypeStruct(gu_shape, BF16), jax.ShapeDtypeStruct(gates_shape, BF16)],
        scratch_shapes=[
            pltpu.VMEM((N_GROUPS, SPAN, 2 * SPAN), F32),
            pltpu.VMEM((N_GROUPS, seq, LANES), F32),
            pltpu.VMEM((N_GROUPS, seq, LANES), F32),
            pltpu.VMEM((_ATTN_UNITS, SPAN, 2 * SPAN), F32),
            pltpu.VMEM((_ATTN_UNITS, SPAN, 2 * SPAN), BF16),
            pltpu.VMEM((_ATTN_UNITS, SPAN, LANES), F32),
            pltpu.VMEM((2 * SPAN, LANES), BF16),
        ],
        compiler_params=_params(("parallel", "parallel"), 56),
        name="attn_prompt",
    )(slopes, q, q, q, k, v, qx, mk, mv, *weights, *([w_in] * len(wcol_specs)))


def _attn_sample_kernel(slopes_ref, q_ref, kn_ref, vn_ref, qx_ref, ck0_ref, ck1_ref, ck2_ref,
                        cv0_ref, cv1_ref, cv2_ref, mk_ref, mv_ref, mix_ref, xm_ref, *, bs):

    def by_row(ref):
        return ref if len(ref.shape) == 3 else ref.reshape(ref.shape[0], ref.shape[1] * ref.shape[2], LANES)

    def head_rows(ref, n, h, n_keys):
        return ref[n, pl.ds(h, n_keys, stride=ref.shape[1] // n_keys), :]

    ck_refs = tuple(by_row(r) for r in (ck0_ref, ck1_ref, ck2_ref))
    cv_refs = tuple(by_row(r) for r in (cv0_ref, cv1_ref, cv2_ref))
    back = (SPAN - lax.broadcasted_iota(jnp.int32, (SPAN, 1), 0)).astype(F32)

    def one(n, carry):
        for h in range(N_KV):
            hs = slice(h * LANES, (h + 1) * LANES)
            kn = kn_ref[n, :, hs]
            vn = vn_ref[n, :, hs]
            scores, s_new = [], []
            for g, (_, dil) in enumerate(DIL_GROUPS):
                qs = slice(g * W_KV + h * LANES, g * W_KV + (h + 1) * LANES)
                q = q_ref[n, :, qs]
                s = jnp.sum(head_rows(ck_refs[g], n, h, SPAN) * q, axis=-1, keepdims=True) * SCALE
                scores.append(s - slopes_ref[g, h] * (back * float(dil)))
                s_new.append(jnp.sum(kn * q, axis=-1, keepdims=True) * SCALE)
            top = s_new[0]
            for g in range(N_GROUPS):
                top = jnp.maximum(top, jnp.maximum(s_new[g], jnp.max(scores[g], axis=0, keepdims=True)))
            num = jnp.zeros((1, LANES), F32)
            den = jnp.zeros((1, 1), F32)
            for g in range(N_GROUPS):
                p = jnp.exp(scores[g] - top)
                p_new = jnp.exp(s_new[g] - top)
                num = num + jnp.sum(p * head_rows(cv_refs[g], n, h, SPAN), axis=0, keepdims=True) + p_new * vn
                den = den + jnp.sum(p, axis=0, keepdims=True) + p_new
            mix_ref[n, :, hs] = num / den
            qx = qx_ref[n, :, hs]
            sx = jnp.sum(head_rows(mk_ref, n, h, N_MEM) * qx, axis=-1, keepdims=True) * SCALE
            px = jnp.exp(sx - jnp.max(sx, axis=0, keepdims=True))
            ox = (jnp.sum(px * head_rows(mv_ref, n, h, N_MEM), axis=0, keepdims=True)
                  / jnp.sum(px, axis=0, keepdims=True))
            xm_ref[n, :, hs] = ox
        return carry

    lax.fori_loop(0, bs, one, 0)


def _attn_sample(slopes, q, kn, vn, qx, cache_k, cache_v, mem_k, mem_v):
    n_seq, win = cache_k.shape[0], cache_k.shape[1]
    bs = 2
    row = lambda w: pl.BlockSpec((bs, 1, w), lambda i: (i, 0, 0))
    as_rows = lambda a: a.reshape(n_seq, 1, a.shape[-1])
    views_k, views_v, cache_specs = [], [], []
    for _, dil in DIL_GROUPS:
        nb = win // dil // SPAN
        if dil == 1:
            shape = (n_seq, win * N_KV, HEAD_DIM)
            cache_specs.append(pl.BlockSpec((bs, SPAN * N_KV, HEAD_DIM), lambda i, nb=nb: (i, nb - 1, 0)))
        else:
            shape = (n_seq, win // dil, dil, N_KV, HEAD_DIM)
            cache_specs.append(pl.BlockSpec((bs, SPAN, None, N_KV, HEAD_DIM),
                                            lambda i, nb=nb: (i, nb - 1, 0, 0, 0)))
        views_k.append(cache_k.reshape(shape))
        views_v.append(cache_v.reshape(shape))
    mem_spec = pl.BlockSpec((bs, N_MEM * N_XHEADS, HEAD_DIM), lambda i: (i, 0, 0))
    mem_rows = lambda a: a.reshape(n_seq, N_MEM * N_XHEADS, HEAD_DIM)
    return pl.pallas_call(
        functools.partial(_attn_sample_kernel, bs=bs),
        grid=(n_seq // bs,),
        in_specs=[pl.BlockSpec(memory_space=pltpu.SMEM), row(W_QD), row(W_KV), row(W_KV), row(W_QX)]
        + cache_specs + cache_specs + [mem_spec, mem_spec],
        out_specs=[row(W_KV), row(W_QX)],
        out_shape=[jax.ShapeDtypeStruct((n_seq, 1, W_KV), F32), jax.ShapeDtypeStruct((n_seq, 1, W_QX), F32)],
        compiler_params=_params(("parallel",), 40),
        name="attn_sample",
    )(slopes, as_rows(q), as_rows(kn), as_rows(vn), as_rows(qx), *views_k, *views_v, mem_rows(mem_k),
      mem_rows(mem_v))


def _merge_kernel(h_ref, c_ref, mix_ref, xm_ref, hs_ref, cs_ref, mixs_ref, xms_ref, wg0_ref, wg1_ref, wg2_ref,
                  bg_ref, wc_ref, wd_ref, wx_ref, wo_ref, wffn_ref, z_ref, zs_ref, wffn_out_ref,
                  merged_ref, mergeds_ref):
    i, s = pl.program_id(0), pl.program_id(1)

    @pl.when(s % _RIDER_STEPS == 0)
    def _():
        wffn_out_ref[...] = wffn_ref[...].astype(BF16)

    def gated_sum(h_ref, branches, merged_ref):
        h = h_ref[...]
        acc = None
        for br, (wg_ref, y_ref, wy_ref) in enumerate(zip((wg0_ref, wg1_ref, wg2_ref), branches,
                                                         (wc_ref, wd_ref, wx_ref))):
            gate = jax.nn.sigmoid(_dot(h, wg_ref[...]) + bg_ref[br])
            term = gate * _dot(y_ref[...].astype(BF16), wy_ref[...])
            acc = term if acc is None else acc + term
        merged_ref[s] = acc.astype(BF16)

    def project(merged_ref, z_ref):
        z = None
        for j in range(_D_TILES):
            part = _dot(merged_ref[j], wo_ref[j * TN:(j + 1) * TN, :])
            z = part if z is None else z + part
        z_ref[...] = z

    first, second = s < _D_TILES, s >= _D_TILES
    pl.when(first)(lambda: gated_sum(h_ref, (c_ref, mix_ref, xm_ref), merged_ref))
    pl.when(second)(lambda: project(merged_ref, z_ref))
    pl.when(first & (i == 0))(lambda: gated_sum(hs_ref, (cs_ref, mixs_ref, xms_ref), mergeds_ref))
    pl.when(second & (i == 0))(lambda: project(mergeds_ref, zs_ref))


def _merge(h, c, mix, xm, hs, cs, mixs, xms, w_gates, b_gate, w_conv_out_t, w_dil_o_t, w_x_o_t, w_out_t, w_ffn,
           *, tm):
    m, ms = h.shape[0], hs.shape[0]
    n1, n2 = _D_TILES, D_MODEL // TN_OUT

    n_rows = m // tm
    whole = lambda a: pl.BlockSpec(a.shape, lambda i, s: (0, 0))
    rider_blocks = n_rows * (n1 + n2) // _RIDER_STEPS
    rider = pl.BlockSpec((w_ffn.shape[0] // rider_blocks, w_ffn.shape[1]),
                         lambda i, s: ((i * (n1 + n2) + s) // _RIDER_STEPS, 0))

    def tile1(s):
        return jnp.where(s < n1, s, 0)

    def gate_spec(br):
        return pl.BlockSpec((D_MODEL, TN), lambda i, s, br=br: (0, br * _D_TILES + tile1(s)))

    def colw(kdim):
        return pl.BlockSpec((kdim, TN), lambda i, s: (0, tile1(s)))

    def rowblk(w):
        return pl.BlockSpec((tm, w), lambda i, s: (_row_ahead(i, s, n1, n_rows), 0))

    return pl.pallas_call(
        _merge_kernel,
        grid=(m // tm, n1 + n2),
        in_specs=[
            rowblk(D_MODEL), rowblk(C_CONV), rowblk(W_KV), rowblk(W_QX),
            whole(hs), whole(cs), whole(mixs), whole(xms),
            gate_spec(0), gate_spec(1), gate_spec(2),
            pl.BlockSpec((3, 1, TN), lambda i, s: (0, 0, tile1(s))),
            colw(C_CONV), colw(W_KV), colw(W_QX),
            pl.BlockSpec((D_MODEL, TN_OUT), lambda i, s: (0, jnp.maximum(s - n1, 0))),
            rider,
        ],
        out_specs=[
            pl.BlockSpec((tm, TN_OUT), lambda i, s: (i, jnp.maximum(s - n1, 0))),
            pl.BlockSpec((ms, TN_OUT), lambda i, s: (0, jnp.where(i == 0, jnp.maximum(s - n1, 0), n2 - 1))),
            rider,
        ],
        out_shape=[jax.ShapeDtypeStruct((m, D_MODEL), F32), jax.ShapeDtypeStruct((ms, D_MODEL), F32),
                   jax.ShapeDtypeStruct(w_ffn.shape, BF16)],
        scratch_shapes=[pltpu.VMEM((n1, tm, TN), BF16), pltpu.VMEM((n1, ms, TN), BF16)],
        compiler_params=_params(("arbitrary", "arbitrary"), 60),
        name="merge",
    )(h, c, mix, xm, hs, cs, mixs, xms, w_gates, w_gates, w_gates, b_gate.reshape(3, 1, D_MODEL), w_conv_out_t,
      w_dil_o_t, w_x_o_t, w_out_t, w_ffn)


def _ffn_kernel(x_ref, z_ref, xs_ref, zs_ref, gmix_ref, gpre_ref, wgu_ref, wd_ref, gpost_ref, out_ref, outs_ref,
                x1_ref, h_ref, x1s_ref, *, n_steps):
    i, s = pl.program_id(0), pl.program_id(1)
    tm, ms = x_ref.shape[0], xs_ref.shape[0]

    def prologue(x_ref, z_ref, x1_ref, row0, out_ref):
        x1 = x_ref[...] + _rms_rows(z_ref[...], gmix_ref[...])
        x1_ref[...] = x1
        h_ref[row0:row0 + x_ref.shape[0], :] = _rms_rows(x1, gpre_ref[...]).astype(BF16)
        out_ref[...] = jnp.zeros_like(out_ref)

    def step(rows):
        h = h_ref[0:rows, :]
        sub = TF // _FFN_SPLIT
        part = None
        for c in range(_FFN_SPLIT):
            gate_up = _dot(h, wgu_ref[:, 2 * c * sub:2 * (c + 1) * sub])
            gate, up = gate_up[:, :sub], gate_up[:, sub:]
            act = (gate * jax.nn.sigmoid(gate) * up).astype(BF16)
            down = _dot(act, wd_ref[c * sub:(c + 1) * sub, :])
            part = down if part is None else part + down
        out_ref[...] += part[0:tm]
        if rows > tm:
            outs_ref[...] += part[tm:rows]

    def epilogue(x1_ref, out_ref):
        out_ref[...] = x1_ref[...] + _rms_rows(out_ref[...], gpost_ref[...])

    pl.when(s == 0)(lambda: prologue(x_ref, z_ref, x1_ref, 0, out_ref))
    pl.when((i == 0) & (s == 0))(lambda: prologue(xs_ref, zs_ref, x1s_ref, tm, outs_ref))
    pl.when(i == 0)(lambda: step(tm + ms))
    pl.when(i > 0)(lambda: step(tm))
    pl.when(s == n_steps - 1)(lambda: epilogue(x1_ref, out_ref))
    pl.when((i == 0) & (s == n_steps - 1))(lambda: epilogue(x1s_ref, outs_ref))


def _ffn(x, z, xs, zs, g_mix, g_pre, w_gate_up, w_down, g_post, *, tm):
    m, ms = x.shape[0], xs.shape[0]
    n_steps = D_FF // TF
    rowblk = pl.BlockSpec((tm, D_MODEL), lambda i, s: (i, 0))
    sample = pl.BlockSpec((ms, D_MODEL), lambda i, s: (0, 0))
    gain = pl.BlockSpec((1, D_MODEL), lambda i, s: (0, 0))
    return pl.pallas_call(
        functools.partial(_ffn_kernel, n_steps=n_steps),
        grid=(m // tm, n_steps),
        in_specs=[
            rowblk, rowblk, sample, sample, gain, gain,
            pl.BlockSpec((D_MODEL, 2 * TF), lambda i, s: (0, s)),
            pl.BlockSpec((TF, D_MODEL), lambda i, s: (s, 0)),
            gain,
        ],
        out_specs=[rowblk, sample],
        out_shape=[jax.ShapeDtypeStruct((m, D_MODEL), F32), jax.ShapeDtypeStruct((ms, D_MODEL), F32)],
        scratch_shapes=[pltpu.VMEM((tm, D_MODEL), F32), pltpu.VMEM((tm + ms, D_MODEL), BF16),
                        pltpu.VMEM((ms, D_MODEL), F32)],
        compiler_params=_params(("arbitrary", "arbitrary"), 54),
        name="ffn",
    )(x, z, xs, zs, g_mix, g_pre, w_gate_up, w_down, g_post)


def kernel(x_prompt, x_sample, cache_win_k, cache_win_v, state_conv, cache_mem_k, cache_mem_v, mem_prompt,
           g_pre_mix, w_in, b_gate, conv_w, conv_b, conv_ln_g, conv_ln_b, w_conv_out, w_dil_o, g_mem, w_mem_kv,
           w_x_o, w_out, g_post_mix, g_pre_ffn, w_ffn_gate, w_ffn_up, w_ffn_down, g_post_ffn):
    batch, seq, _ = x_prompt.shape
    n_seq = x_sample.shape[0]
    depth = w_in.shape[0]
    assert depth == 1 and x_sample.shape[1] == 1 and seq % (DIL_GROUPS[-1][1] * SPAN) == 0
    assert cache_win_k.shape[2] == DIL_GROUPS[-1][0]

    idx = jnp.arange(1, N_QH + 1, dtype=F32)
    slopes = jnp.exp2(-8.0 * idx / N_QH).reshape(N_GROUPS, N_KV)

    xp, xs = x_prompt.reshape(batch * seq, D_MODEL), x_sample.reshape(n_seq, D_MODEL)
    l = 0
    w_in_t = w_in[l][:, :_GATE_TILE0 * TN].astype(BF16)
    w_mem_kv_t = w_mem_kv[l].astype(BF16)
    row = lambda a: a[l].reshape(1, -1)
    g_pre, g_post, g_ffn_pre, g_ffn_post = row(g_pre_mix), row(g_post_mix), row(g_pre_ffn), row(g_post_ffn)
    cb, lng, lnb = row(conv_b), row(conv_ln_g), row(conv_ln_b)
    conv_w_slabs = conv_w[l].reshape(CONV_WIDTH, N_SLABS, LANES).transpose(1, 0, 2)

    mk, mv = _mem_kv(mem_prompt.reshape(batch * N_MEM, D_MODEL), row(g_mem), w_mem_kv_t, tm=TM_MEM)
    h_p, u_p, q_p, k_p, v_p, qx_p, h_s, u_s, q_s, k_s, v_s, qx_s = _in_proj(xp, xs, g_pre, w_in_t, tm=TM_DENSE)

    c_p, conv_p, k_rows, v_rows, *branch_weights = _conv_prompt(
        u_p, conv_w_slabs, cb, lng, lnb, k_p, v_p, (w_conv_out[l], w_dil_o[l], w_x_o[l], w_out[l]),
        batch=batch, seq=seq)
    mix_p, xm_p, w_gate_up, w_gates = _attn_prompt(
        slopes, q_p, k_p, v_p, qx_p, mk.reshape(batch, N_MEM, W_QX), mv.reshape(batch, N_MEM, W_QX),
        w_ffn_gate[l], w_ffn_up[l], w_in[l], batch=batch, seq=seq)

    c_s, conv_s = _conv_sample(state_conv[l].transpose(1, 0, 2), u_s, conv_w[l], cb, lng, lnb)
    mix_s, xm_s = _attn_sample(slopes, q_s, k_s, v_s, qx_s, cache_win_k[l], cache_win_v[l],
                               cache_mem_k[l], cache_mem_v[l])

    z_p, z_s, w_down = _merge(h_p, c_p, mix_p.reshape(batch * seq, W_KV), xm_p.reshape(batch * seq, W_QX),
                              h_s, c_s, mix_s.reshape(n_seq, W_KV), xm_s.reshape(n_seq, W_QX),
                              w_gates, b_gate[l], *branch_weights, w_ffn_down[l], tm=TM_DENSE)
    y_p, y_s = _ffn(xp, z_p, xs, z_s, g_post, g_ffn_pre, w_gate_up, w_down, g_ffn_post, tm=TM_FFN)

    win = min(DIL_GROUPS[-1][0], seq)
    kv_shape = (1, batch, seq, N_KV, HEAD_DIM)
    mem_shape = (1, batch, N_MEM, N_XHEADS, HEAD_DIM)
    new_shape = (1, n_seq, 1, N_KV, HEAD_DIM)
    return (y_p.reshape(batch, seq, D_MODEL), y_s.reshape(n_seq, 1, D_MODEL),
            k_rows.reshape(kv_shape)[:, :, seq - win:], v_rows.reshape(kv_shape)[:, :, seq - win:], conv_p,
            mk.reshape(mem_shape), mv.reshape(mem_shape),
            k_s.reshape(new_shape), v_s.reshape(new_shape), conv_s.transpose(1, 0, 2)[None])
```

```python
import functools

import jax
import jax.numpy as jnp
from jax import lax
from jax.experimental import pallas as pl
from jax.experimental.pallas import tpu as pltpu

F32 = jnp.float32
BF16 = jnp.bfloat16

D_MODEL = 2048
C_CONV = D_MODEL // 2
CONV_WIDTH = 31
CONV_PREFIX = CONV_WIDTH - 1
HEAD_DIM = 128
N_KV = 4
DIL_GROUPS = ((128, 1), (512, 4), (2048, 16))
N_GROUPS = len(DIL_GROUPS)
N_QH = N_GROUPS * N_KV
SPAN = 128
N_MEM = 256
N_XHEADS = 4
W_QD = N_QH * HEAD_DIM
W_KV = N_KV * HEAD_DIM
W_QX = N_XHEADS * HEAD_DIM
D_FF = 5632
EPS = 1e-6
NEG = -1e30
SCALE = HEAD_DIM ** -0.5
LANES = 128
SUBLANES = 8
N_SLABS = C_CONV // LANES
MIB = 1024 * 1024

TN = 512
_A_TILES = C_CONV // TN
_Q_TILES = W_QD // TN
_KV_TILES = W_KV // TN
_QX_TILES = W_QX // TN
_STEP_Q0 = _A_TILES
_STEP_K0 = _STEP_Q0 + _Q_TILES
_STEP_V0 = _STEP_K0 + _KV_TILES
_STEP_X0 = _STEP_V0 + _KV_TILES
_IN_STEPS = _STEP_X0 + _QX_TILES
_GATE_TILE0 = 2 * _A_TILES + _Q_TILES + 2 * _KV_TILES + _QX_TILES
_D_TILES = D_MODEL // TN
_CONV_PAD = -(-CONV_PREFIX // SUBLANES) * SUBLANES
_ATTN_UNITS = 8
_RIDER_STEPS = 2
_GATE_CAST_COLS = 1024
TN_OUT = 512
TF = 512
_FFN_SPLIT = 2
TM_DENSE = 1024
TM_FFN = 512
TM_MEM = 512


def _params(semantics, vmem_mib):
    return pltpu.CompilerParams(dimension_semantics=semantics, vmem_limit_bytes=vmem_mib * MIB)


def _rms_rows(x, g):
    return x * lax.rsqrt(jnp.mean(x * x, axis=-1, keepdims=True) + EPS) * g


def _dot(a, b):
    return jnp.dot(a, b, preferred_element_type=F32)


def _row_ahead(i, s, free_from, n_rows):
    return jnp.minimum(i + (s >= free_from).astype(jnp.int32), n_rows - 1)


def _in_proj_kernel(x_ref, xs_ref, g_ref, w1_ref, w2_ref, h_ref, u_ref, q_ref, k_ref, v_ref, qx_ref,
                    hs_ref, tiles_ref):
    i, s = pl.program_id(0), pl.program_id(1)

    def project(on, x_ref, h_ref, store, u_ref, q_ref, k_ref, v_ref, qx_ref):
        def norm():
            h_ref[...] = _rms_rows(x_ref[...], g_ref[...]).astype(BF16)

        def glu():
            h = h_ref[...]
            store(u_ref, _dot(h, w1_ref[...]) * jax.nn.sigmoid(_dot(h, w2_ref[...])))

        def into(out_ref):
            return lambda: store(out_ref, _dot(h_ref[...], w1_ref[...]))

        pl.when(on & (s == 0))(norm)
        pl.when(on & (s < _STEP_Q0))(glu)
        pl.when(on & (s >= _STEP_Q0) & (s < _STEP_K0))(into(q_ref))
        pl.when(on & (s >= _STEP_K0) & (s < _STEP_V0))(into(k_ref))
        pl.when(on & (s >= _STEP_V0) & (s < _STEP_X0))(into(v_ref))
        pl.when(on & (s >= _STEP_X0))(into(qx_ref))

    def store_slabs(out_ref, tile):
        for c in range(TN // LANES):
            out_ref[c] = tile[:, c * LANES:(c + 1) * LANES].astype(out_ref.dtype)

    def store_rows(out_ref, tile):
        out_ref[...] = tile

    project(True, x_ref, h_ref, store_slabs, u_ref, q_ref, k_ref, v_ref, qx_ref)
    project(i == 0, xs_ref, hs_ref, store_rows, *([tiles_ref] * 5))


def _in_proj(x, xs, g, w_in_t, *, tm):
    m, ms = x.shape[0], xs.shape[0]
    last_a = _A_TILES - 1
    per_tile = TN // LANES

    def out_tiles(step0, n_tiles):
        def index(i, s):
            moved = (s >= step0 + n_tiles) & (i < m // tm - 1)
            return (jnp.where(moved, 0, jnp.clip(s - step0, 0, n_tiles - 1)), i + moved.astype(jnp.int32), 0)
        return pl.BlockSpec((per_tile, tm, LANES), index)

    def slabs(width, dtype):
        return jax.ShapeDtypeStruct((width // LANES, m, LANES), dtype)

    outs = pl.pallas_call(
        _in_proj_kernel,
        grid=(m // tm, _IN_STEPS),
        in_specs=[
            pl.BlockSpec((tm, D_MODEL), lambda i, s: (_row_ahead(i, s, _A_TILES, m // tm), 0)),
            pl.BlockSpec((ms, D_MODEL), lambda i, s: (0, 0)),
            pl.BlockSpec((1, D_MODEL), lambda i, s: (0, 0)),
            pl.BlockSpec((D_MODEL, TN), lambda i, s: (0, jnp.where(s < _A_TILES, s, s + _A_TILES))),
            pl.BlockSpec((D_MODEL, TN), lambda i, s: (0, jnp.where(s < _A_TILES, s, 0) + _A_TILES)),
        ],
        out_specs=[
            pl.BlockSpec((tm, D_MODEL), lambda i, s: (i, 0)),
            pl.BlockSpec((per_tile, tm, LANES), lambda i, s: (jnp.minimum(s, last_a), i, 0)),
            out_tiles(_STEP_Q0, _Q_TILES),
            out_tiles(_STEP_K0, _KV_TILES),
            out_tiles(_STEP_V0, _KV_TILES),
            out_tiles(_STEP_X0, _QX_TILES),
            pl.BlockSpec((ms, D_MODEL), lambda i, s: (0, 0)),
            pl.BlockSpec((ms, TN), lambda i, s: (0, jnp.where(i == 0, s, _IN_STEPS - 1))),
        ],
        out_shape=[
            jax.ShapeDtypeStruct((m, D_MODEL), BF16),
            slabs(C_CONV, F32), slabs(W_QD, F32), slabs(W_KV, F32), slabs(W_KV, F32), slabs(W_QX, BF16),
            jax.ShapeDtypeStruct((ms, D_MODEL), BF16),
            jax.ShapeDtypeStruct((ms, _IN_STEPS * TN), F32),
        ],
        compiler_params=_params(("arbitrary", "arbitrary"), 60),
        name="in_proj",
    )(x, xs, g, w_in_t, w_in_t)
    *prompt, h_s, tiles = outs
    bounds = [step * TN for step in (0, _STEP_Q0, _STEP_K0, _STEP_V0, _STEP_X0, _IN_STEPS)]
    return (*prompt, h_s, *(tiles[:, lo:hi] for lo, hi in zip(bounds, bounds[1:])))


def _mem_kv_kernel(x_ref, g_ref, w_ref, mk_ref, mv_ref, h_ref):
    s = pl.program_id(1)

    @pl.when(s == 0)
    def _():
        h_ref[...] = _rms_rows(x_ref[...], g_ref[...]).astype(BF16)
        mk_ref[...] = _dot(h_ref[...], w_ref[...].astype(BF16))

    @pl.when(s == 1)
    def _():
        mv_ref[...] = _dot(h_ref[...], w_ref[...].astype(BF16))


def _mem_kv(mem, g, w_t, *, tm):
    m = mem.shape[0]
    return pl.pallas_call(
        _mem_kv_kernel,
        grid=(m // tm, 2),
        in_specs=[
            pl.BlockSpec((tm, D_MODEL), lambda i, s: (i, 0)),
            pl.BlockSpec((1, D_MODEL), lambda i, s: (0, 0)),
            pl.BlockSpec((D_MODEL, W_QX), lambda i, s: (0, s)),
        ],
        out_specs=[
            pl.BlockSpec((tm, W_QX), lambda i, s: (i, 0)),
            pl.BlockSpec((tm, W_QX), lambda i, s: (i, 0)),
        ],
        out_shape=[jax.ShapeDtypeStruct((m, W_QX), F32)] * 2,
        scratch_shapes=[pltpu.VMEM((tm, D_MODEL), BF16)],
        compiler_params=_params(("parallel", "arbitrary"), 32),
        name="mem_kv",
    )(mem, g, w_t)


def _ln_swish_slabs(y_slabs, cb_ref, lng_ref, lnb_ref, out_dtype):
    ys = [y + cb_ref[:, c * LANES:(c + 1) * LANES] for c, y in enumerate(y_slabs)]
    tot = ys[0]
    for y in ys[1:]:
        tot = tot + y
    mu = jnp.sum(tot, axis=-1, keepdims=True) * (1.0 / C_CONV)
    ds = [y - mu for y in ys]
    sq = ds[0] * ds[0]
    for d in ds[1:]:
        sq = sq + d * d
    rs = lax.rsqrt(jnp.sum(sq, axis=-1, keepdims=True) * (1.0 / C_CONV) + EPS)
    outs = []
    for c, d in enumerate(ds):
        z = d * rs * lng_ref[:, c * LANES:(c + 1) * LANES] + lnb_ref[:, c * LANES:(c + 1) * LANES]
        outs.append((z * jax.nn.sigmoid(z)).astype(out_dtype))
    return outs


def _conv_prompt_kernel(u_ref, w_ref, cb_ref, lng_ref, lnb_ref, k_ref, v_ref, *rest, seq, tt, n_cast):
    cast_in, (c_ref, st_ref, kout_ref, vout_ref) = rest[:n_cast], rest[n_cast:n_cast + 4]
    cast_out, (head_ref, y_ref) = rest[n_cast + 4:2 * n_cast + 4], rest[2 * n_cast + 4:]
    half = pl.program_id(1)
    n_tiles = seq // tt // 2
    pad = _CONV_PAD

    for src_ref, dst_ref in ((k_ref, kout_ref), (v_ref, vout_ref)):
        for h in range(N_KV):
            dst_ref[pl.ds(h, src_ref.shape[1], stride=N_KV), :] = src_ref[h]
    for src_ref, dst_ref in zip(cast_in, cast_out):
        dst_ref[...] = src_ref[...].astype(BF16)

    def tile(src_ref, base, t0):
        def slab(c, carry):
            acc = jnp.zeros((tt, LANES), F32)
            for j in range(SUBLANES):
                taps = range(j, CONV_WIDTH, SUBLANES)
                win = src_ref[c, pl.ds(base + j, tt + taps[-1] - j), :]
                for k in taps:
                    acc = acc + win[k - j:k - j + tt] * w_ref[c, k:k + 1, :]
            y_ref[c] = acc
            return carry

        lax.fori_loop(0, N_SLABS, slab, 0, unroll=4)
        ys = [y_ref[c] for c in range(N_SLABS)]
        outs = _ln_swish_slabs(ys, cb_ref, lng_ref, lnb_ref, BF16)
        for c in range(N_SLABS):
            c_ref[pl.ds(t0, tt), c * LANES:(c + 1) * LANES] = outs[c]

    @pl.when(half == 0)
    def _():
        head_ref[:, 0:pad, :] = jnp.zeros((N_SLABS, pad, LANES), F32)
        head_ref[:, pad:pad + tt, :] = u_ref[:, 0:tt, :]
        tile(head_ref, pad - CONV_PREFIX, 0)

    @pl.when(half > 0)
    def _():
        tile(u_ref, pl.multiple_of(half * (n_tiles * tt), tt) - CONV_PREFIX, 0)

    def body(i, carry):
        t0 = pl.multiple_of((half * n_tiles + i) * tt, tt)
        tile(u_ref, t0 - CONV_PREFIX, pl.multiple_of(i * tt, tt))
        return carry

    lax.fori_loop(1, n_tiles, body, 0)
    for c in range(N_SLABS):
        st_ref[:, c * LANES:(c + 1) * LANES] = u_ref[c, seq - CONV_PREFIX:seq, :]


def _conv_prompt(u_slabs, w_slabs, conv_b, ln_g, ln_b, k, v, cast_weights, *, batch, seq):
    tt = 128
    halves = 2
    n_steps = batch * halves
    step = lambda b, j: b * halves + j
    const = lambda *idx: (lambda b, j: idx)

    def row_slice(rows, cols):
        return pl.BlockSpec((rows // n_steps, cols), lambda b, j: (step(b, j), 0))

    kv_in = pl.BlockSpec((N_KV, batch * seq // n_steps, LANES), lambda b, j: (0, step(b, j), 0))
    kv_out = row_slice(batch * seq * N_KV, HEAD_DIM)
    w_specs = [row_slice(*w.shape) for w in cast_weights]
    return pl.pallas_call(
        functools.partial(_conv_prompt_kernel, seq=seq, tt=tt, n_cast=len(cast_weights)),
        grid=(batch, halves),
        in_specs=[
            pl.BlockSpec((N_SLABS, seq, LANES), lambda b, j: (0, b, 0)),
            pl.BlockSpec((N_SLABS, CONV_WIDTH, LANES), const(0, 0, 0)),
            pl.BlockSpec((1, C_CONV), const(0, 0)),
            pl.BlockSpec((1, C_CONV), const(0, 0)),
            pl.BlockSpec((1, C_CONV), const(0, 0)),
            kv_in, kv_in,
        ] + w_specs,
        out_specs=[
            pl.BlockSpec((seq // halves, C_CONV), lambda b, j: (step(b, j), 0)),
            pl.BlockSpec((None, None, CONV_PREFIX, C_CONV), lambda b, j: (0, b, 0, 0)),
            kv_out, kv_out,
        ] + w_specs,
        out_shape=[
            jax.ShapeDtypeStruct((batch * seq, C_CONV), BF16),
            jax.ShapeDtypeStruct((1, batch, CONV_PREFIX, C_CONV), F32),
            jax.ShapeDtypeStruct((batch * seq * N_KV, HEAD_DIM), F32),
            jax.ShapeDtypeStruct((batch * seq * N_KV, HEAD_DIM), F32),
        ] + [jax.ShapeDtypeStruct(w.shape, BF16) for w in cast_weights],
        scratch_shapes=[pltpu.VMEM((N_SLABS, _CONV_PAD + tt, LANES), F32), pltpu.VMEM((N_SLABS, tt, LANES), F32)],
        compiler_params=_params(("parallel", "arbitrary"), 56),
        name="conv_prompt",
    )(u_slabs, w_slabs, conv_b, ln_g, ln_b, k, v, *cast_weights)


def _conv_sample_kernel(state_ref, u_ref, w_ref, cb_ref, lng_ref, lnb_ref, c_ref, st_ref):
    new = u_ref[...]
    y = new * w_ref[CONV_PREFIX:CONV_WIDTH, :]
    for k in range(CONV_PREFIX):
        y = y + state_ref[k] * w_ref[k:k + 1, :]
    ys = [y[:, c * LANES:(c + 1) * LANES] for c in range(N_SLABS)]
    outs = _ln_swish_slabs(ys, cb_ref, lng_ref, lnb_ref, F32)
    for c in range(N_SLABS):
        c_ref[:, c * LANES:(c + 1) * LANES] = outs[c]
    st_ref[0:CONV_PREFIX - 1] = state_ref[1:CONV_PREFIX]
    st_ref[CONV_PREFIX - 1] = new


def _conv_sample(state_t, u, conv_w, conv_b, ln_g, ln_b):
    n_seq = state_t.shape[1]
    return pl.pallas_call(
        _conv_sample_kernel,
        out_shape=[
            jax.ShapeDtypeStruct((n_seq, C_CONV), F32),
            jax.ShapeDtypeStruct((CONV_PREFIX, n_seq, C_CONV), F32),
        ],
        compiler_params=pltpu.CompilerParams(vmem_limit_bytes=32 * MIB),
        name="conv_sample",
    )(state_t, u, conv_w, conv_b, ln_g, ln_b)


def _qk(q, k):
    return lax.dot_general(q, k, (((1,), (1,)), ((), ())), preferred_element_type=F32)


def _attn_prompt_kernel(slopes_ref, q0_ref, q1_ref, q2_ref, k_ref, v_ref, qx_ref, mk_ref, mv_ref,
                        wg_ref, wu_ref, *rest, seq):
    n_cols = len(rest) - 11
    wcol_refs, (mix_ref, xm_ref, wgu_out_ref, wgate_ref) = rest[:n_cols], rest[n_cols:n_cols + 4]
    bias_ref, o_ref, lse_ref, s_ref, p_ref, m_ref, ones_ref = rest[n_cols + 4:]
    head = pl.program_id(1)
    ones_ref[...] = jnp.ones(ones_ref.shape, BF16)
    for c, wcol_ref in enumerate(wcol_refs):
        wgate_ref[:, c * _GATE_CAST_COLS:(c + 1) * _GATE_CAST_COLS] = wcol_ref[...].astype(BF16)
    sub = TF // _FFN_SPLIT
    for j in range(D_FF // sub):
        cols = slice(j * sub, (j + 1) * sub)
        wgu_out_ref[:, 2 * j * sub:(2 * j + 1) * sub] = wg_ref[:, cols].astype(BF16)
        wgu_out_ref[:, (2 * j + 1) * sub:(2 * j + 2) * sub] = wu_ref[:, cols].astype(BF16)
    qi = lax.broadcasted_iota(jnp.int32, (SPAN, 2 * SPAN), 0)
    ci = lax.broadcasted_iota(jnp.int32, (SPAN, 2 * SPAN), 1)
    dist = qi + SPAN - ci
    valid = (dist >= 0) & (dist <= SPAN)
    for g, (_, dil) in enumerate(DIL_GROUPS):
        bias = -slopes_ref[g, head] * (dist * dil).astype(F32)
        bias_ref[g] = jnp.where(valid, bias, NEG)

    def rows(ref, start, size, dil):
        if dil == 1:
            return ref[pl.ds(start, size), :]
        return ref[pl.ds(start, size, stride=dil), :]

    def put(ref, g, start, dil, val):
        if dil == 1:
            ref[g, pl.ds(start, SPAN), :] = val
        else:
            ref[g, pl.ds(start, SPAN, stride=dil), :] = val

    def staged(items, n_keys_of, scores, values, emit):
        for u, item in enumerate(items):
            s_ref[u, :, 0:n_keys_of(item)] = scores(item)
        for u, item in enumerate(items):
            s = s_ref[u, :, 0:n_keys_of(item)]
            m = jnp.max(s, axis=-1, keepdims=True)
            p_ref[u, :, 0:n_keys_of(item)] = jnp.exp(s - m).astype(BF16)
            m_ref[u] = jnp.broadcast_to(m, (SPAN, LANES))
        for u, item in enumerate(items):
            p = p_ref[u, :, 0:n_keys_of(item)]
            l = _dot(p, ones_ref[0:n_keys_of(item), :])
            emit(item, _dot(p, values(item)) * (1.0 / l), m_ref[u] + jnp.log(l))

    def units(g, dil, q_ref, starts_keys):
        def scores(item):
            start, n_keys = item
            q = rows(q_ref, start, SPAN, dil).astype(BF16)
            k = rows(k_ref, start - (n_keys - SPAN) * dil, n_keys, dil).astype(BF16)
            return _qk(q, k) * SCALE + bias_ref[g, :, 2 * SPAN - n_keys:2 * SPAN]

        def values(item):
            start, n_keys = item
            return rows(v_ref, start - (n_keys - SPAN) * dil, n_keys, dil).astype(BF16)

        def emit(item, o, lse):
            put(o_ref, g, item[0], dil, o)
            put(lse_ref, g, item[0], dil, lse)

        staged(starts_keys, lambda item: item[1], scores, values, emit)

    q_refs = (q0_ref, q1_ref, q2_ref)
    for g, (_, dil) in enumerate(DIL_GROUPS):
        n_blocks = seq // dil // SPAN
        q_ref = q_refs[g]
        if n_blocks == 1:
            per_body = _ATTN_UNITS

            def first_blocks(j, carry, g=g, dil=dil, q_ref=q_ref):
                units(g, dil, q_ref, [(j * per_body + u, SPAN) for u in range(per_body)])
                return carry
            lax.fori_loop(0, dil // per_body, first_blocks, 0)
        elif dil == 1:
            per_body = _ATTN_UNITS

            def later_blocks(j, carry, g=g, dil=dil, q_ref=q_ref):
                units(g, dil, q_ref, [(pl.multiple_of((j * per_body + u) * SPAN, SPAN), 2 * SPAN)
                                      for u in range(per_body)])
                return carry
            units(g, dil, q_ref, [(0, SPAN)] + [(n * SPAN, 2 * SPAN) for n in range(1, per_body)])
            lax.fori_loop(1, n_blocks // per_body, later_blocks, 0)
        else:
            streams = _ATTN_UNITS // n_blocks

            def residue(j, carry, g=g, dil=dil, q_ref=q_ref, n_blocks=n_blocks, streams=streams):
                work = []
                for r in range(streams):
                    start = j * streams + r
                    work.append((start, SPAN))
                    work += [(start + n * (SPAN * dil), 2 * SPAN) for n in range(1, n_blocks)]
                units(g, dil, q_ref, work)
                return carry
            lax.fori_loop(0, dil // streams, residue, 0)

    tq = 256

    def mix(j, carry):
        t0 = pl.multiple_of(j * tq, tq)
        l0 = lse_ref[0, pl.ds(t0, tq), :]
        l1 = lse_ref[1, pl.ds(t0, tq), :]
        l2 = lse_ref[2, pl.ds(t0, tq), :]
        top = jnp.maximum(jnp.maximum(l0, l1), l2)
        w0 = jnp.exp(l0 - top)
        w1 = jnp.exp(l1 - top)
        w2 = jnp.exp(l2 - top)
        num = (w0 * o_ref[0, pl.ds(t0, tq), :] + w1 * o_ref[1, pl.ds(t0, tq), :]
               + w2 * o_ref[2, pl.ds(t0, tq), :])
        mix_ref[pl.ds(t0, tq), :] = (num / (w0 + w1 + w2)).astype(BF16)
        return carry

    lax.fori_loop(0, seq // tq, mix, 0, unroll=2)

    def memory(j, carry):
        def scores(t0):
            return _qk(qx_ref[pl.ds(t0, SPAN), :], mk_ref[...].astype(BF16)) * SCALE

        def emit(t0, o, lse):
            xm_ref[pl.ds(t0, SPAN), :] = o.astype(BF16)

        staged([pl.multiple_of((j * _ATTN_UNITS + u) * SPAN, SPAN) for u in range(_ATTN_UNITS)],
               lambda t0: N_MEM, scores, lambda t0: mv_ref[...].astype(BF16), emit)
        return carry

    lax.fori_loop(0, seq // SPAN // _ATTN_UNITS, memory, 0)


def _attn_prompt(slopes, q, k, v, qx, mk, mv, w_gate, w_up, w_in, *, batch, seq):
    def col(c):
        return pl.BlockSpec((None, seq, LANES), lambda b, h, c=c: (c * N_KV + h, b, 0))

    slab_spec = col(0)
    head_spec = pl.BlockSpec((None, seq, LANES), lambda b, h: (b, 0, h))
    mem_spec = pl.BlockSpec((None, N_MEM, LANES), lambda b, h: (b, 0, h))
    n_steps = batch * N_KV

    def row_slice(rows, cols):
        return pl.BlockSpec((rows // n_steps, cols), lambda b, h: (b * N_KV + h, 0))

    weights = (w_gate, w_up)
    gu_shape = (D_MODEL, 2 * D_FF)
    gates_shape = (w_in.shape[0], 3 * D_MODEL)
    gate_col0 = _GATE_TILE0 * TN // _GATE_CAST_COLS
    wcol_specs = [pl.BlockSpec((w_in.shape[0] // n_steps, _GATE_CAST_COLS),
                               lambda b, h, c=c: (b * N_KV + h, gate_col0 + c))
                  for c in range(gates_shape[1] // _GATE_CAST_COLS)]
    return pl.pallas_call(
        functools.partial(_attn_prompt_kernel, seq=seq),
        grid=(batch, N_KV),
        in_specs=[
            pl.BlockSpec(memory_space=pltpu.SMEM),
            col(0), col(1), col(2), slab_spec, slab_spec, slab_spec, mem_spec, mem_spec,
        ] + [row_slice(*w.shape) for w in weights] + wcol_specs,
        out_specs=[head_spec, head_spec, row_slice(*gu_shape), row_slice(*gates_shape)],
        out_shape=[jax.ShapeDtypeStruct((batch, seq, W_KV), BF16),
                   jax.ShapeDtypeStruct((batch, seq, W_QX), BF16),
                   jax.ShapeDtypeStruct(gu_shape, BF16), jax.ShapeDtypeStruct(gates_shape, BF16)],
        scratch_shapes=[
            pltpu.VMEM((N_GROUPS, SPAN, 2 * SPAN), F32),
            pltpu.VMEM((N_GROUPS, seq, LANES), F32),
            pltpu.VMEM((N_GROUPS, seq, LANES), F32),
            pltpu.VMEM((_ATTN_UNITS, SPAN, 2 * SPAN), F32),
            pltpu.VMEM((_ATTN_UNITS, SPAN, 2 * SPAN), BF16),
            pltpu.VMEM((_ATTN_UNITS, SPAN, LANES), F32),
            pltpu.VMEM((2 * SPAN, LANES), BF16),
        ],
        compiler_params=_params(("parallel", "parallel"), 56),
        name="attn_prompt",
    )(slopes, q, q, q, k, v, qx, mk, mv, *weights, *([w_in] * len(wcol_specs)))


def _attn_sample_kernel(slopes_ref, q_ref, kn_ref, vn_ref, qx_ref, ck0_ref, ck1_ref, ck2_ref,
                        cv0_ref, cv1_ref, cv2_ref, mk_ref, mv_ref, mix_ref, xm_ref, *, bs):

    def by_row(ref):
        return ref if len(ref.shape) == 3 else ref.reshape(ref.shape[0], ref.shape[1] * ref.shape[2], LANES)

    def head_rows(ref, n, h, n_keys):
        return ref[n, pl.ds(h, n_keys, stride=ref.shape[1] // n_keys), :]

    ck_refs = tuple(by_row(r) for r in (ck0_ref, ck1_ref, ck2_ref))
    cv_refs = tuple(by_row(r) for r in (cv0_ref, cv1_ref, cv2_ref))
    back = (SPAN - lax.broadcasted_iota(jnp.int32, (SPAN, 1), 0)).astype(F32)

    def one(n, carry):
        for h in range(N_KV):
            hs = slice(h * LANES, (h + 1) * LANES)
            kn = kn_ref[n, :, hs]
            vn = vn_ref[n, :, hs]
            scores, s_new = [], []
            for g, (_, dil) in enumerate(DIL_GROUPS):
                qs = slice(g * W_KV + h * LANES, g * W_KV + (h + 1) * LANES)
                q = q_ref[n, :, qs]
                s = jnp.sum(head_rows(ck_refs[g], n, h, SPAN) * q, axis=-1, keepdims=True) * SCALE
                scores.append(s - slopes_ref[g, h] * (back * float(dil)))
                s_new.append(jnp.sum(kn * q, axis=-1, keepdims=True) * SCALE)
            top = s_new[0]
            for g in range(N_GROUPS):
                top = jnp.maximum(top, jnp.maximum(s_new[g], jnp.max(scores[g], axis=0, keepdims=True)))
            num = jnp.zeros((1, LANES), F32)
            den = jnp.zeros((1, 1), F32)
            for g in range(N_GROUPS):
                p = jnp.exp(scores[g] - top)
                p_new = jnp.exp(s_new[g] - top)
                num = num + jnp.sum(p * head_rows(cv_refs[g], n, h, SPAN), axis=0, keepdims=True) + p_new * vn
                den = den + jnp.sum(p, axis=0, keepdims=True) + p_new
            mix_ref[n, :, hs] = num / den
            qx = qx_ref[n, :, hs]
            sx = jnp.sum(head_rows(mk_ref, n, h, N_MEM) * qx, axis=-1, keepdims=True) * SCALE
            px = jnp.exp(sx - jnp.max(sx, axis=0, keepdims=True))
            ox = (jnp.sum(px * head_rows(mv_ref, n, h, N_MEM), axis=0, keepdims=True)
                  / jnp.sum(px, axis=0, keepdims=True))
            xm_ref[n, :, hs] = ox
        return carry

    lax.fori_loop(0, bs, one, 0)


def _attn_sample(slopes, q, kn, vn, qx, cache_k, cache_v, mem_k, mem_v):
    n_seq, win = cache_k.shape[0], cache_k.shape[1]
    bs = 4
    row = lambda w: pl.BlockSpec((bs, 1, w), lambda i: (i, 0, 0))
    as_rows = lambda a: a.reshape(n_seq, 1, a.shape[-1])
    views_k, views_v, cache_specs = [], [], []
    for _, dil in DIL_GROUPS:
        nb = win // dil // SPAN
        if dil == 1:
            shape = (n_seq, win * N_KV, HEAD_DIM)
            cache_specs.append(pl.BlockSpec((bs, SPAN * N_KV, HEAD_DIM), lambda i, nb=nb: (i, nb - 1, 0)))
        else:
            shape = (n_seq, win // dil, dil, N_KV, HEAD_DIM)
            cache_specs.append(pl.BlockSpec((bs, SPAN, None, N_KV, HEAD_DIM),
                                            lambda i, nb=nb: (i, nb - 1, 0, 0, 0)))
        views_k.append(cache_k.reshape(shape))
        views_v.append(cache_v.reshape(shape))
    mem_spec = pl.BlockSpec((bs, N_MEM * N_XHEADS, HEAD_DIM), lambda i: (i, 0, 0))
    mem_rows = lambda a: a.reshape(n_seq, N_MEM * N_XHEADS, HEAD_DIM)
    return pl.pallas_call(
        functools.partial(_attn_sample_kernel, bs=bs),
        grid=(n_seq // bs,),
        in_specs=[pl.BlockSpec(memory_space=pltpu.SMEM), row(W_QD), row(W_KV), row(W_KV), row(W_QX)]
        + cache_specs + cache_specs + [mem_spec, mem_spec],
        out_specs=[row(W_KV), row(W_QX)],
        out_shape=[jax.ShapeDtypeStruct((n_seq, 1, W_KV), F32), jax.ShapeDtypeStruct((n_seq, 1, W_QX), F32)],
        compiler_params=_params(("parallel",), 40),
        name="attn_sample",
    )(slopes, as_rows(q), as_rows(kn), as_rows(vn), as_rows(qx), *views_k, *views_v, mem_rows(mem_k),
      mem_rows(mem_v))


def _merge_kernel(h_ref, c_ref, mix_ref, xm_ref, hs_ref, cs_ref, mixs_ref, xms_ref, wg0_ref, wg1_ref, wg2_ref,
                  bg_ref, wc_ref, wd_ref, wx_ref, wo_ref, wffn_ref, z_ref, zs_ref, wffn_out_ref,
                  merged_ref, mergeds_ref):
    i, s = pl.program_id(0), pl.program_id(1)

    @pl.when(s % _RIDER_STEPS == 0)
    def _():
        wffn_out_ref[...] = wffn_ref[...].astype(BF16)

    def gated_sum(h_ref, branches, merged_ref):
        h = h_ref[...]
        acc = None
        for br, (wg_ref, y_ref, wy_ref) in enumerate(zip((wg0_ref, wg1_ref, wg2_ref), branches,
                                                         (wc_ref, wd_ref, wx_ref))):
            gate = jax.nn.sigmoid(_dot(h, wg_ref[...]) + bg_ref[br])
            term = gate * _dot(y_ref[...].astype(BF16), wy_ref[...])
            acc = term if acc is None else acc + term
        merged_ref[s] = acc.astype(BF16)

    def project(merged_ref, z_ref):
        z = None
        for j in range(_D_TILES):
            part = _dot(merged_ref[j], wo_ref[j * TN:(j + 1) * TN, :])
            z = part if z is None else z + part
        z_ref[...] = z

    first, second = s < _D_TILES, s >= _D_TILES
    pl.when(first)(lambda: gated_sum(h_ref, (c_ref, mix_ref, xm_ref), merged_ref))
    pl.when(second)(lambda: project(merged_ref, z_ref))
    pl.when(first & (i == 0))(lambda: gated_sum(hs_ref, (cs_ref, mixs_ref, xms_ref), mergeds_ref))
    pl.when(second & (i == 0))(lambda: project(mergeds_ref, zs_ref))


def _merge(h, c, mix, xm, hs, cs, mixs, xms, w_gates, b_gate, w_conv_out_t, w_dil_o_t, w_x_o_t, w_out_t, w_ffn,
           *, tm):
    m, ms = h.shape[0], hs.shape[0]
    n1, n2 = _D_TILES, D_MODEL // TN_OUT

    n_rows = m // tm
    whole = lambda a: pl.BlockSpec(a.shape, lambda i, s: (0, 0))
    rider_blocks = n_rows * (n1 + n2) // _RIDER_STEPS
    rider = pl.BlockSpec((w_ffn.shape[0] // rider_blocks, w_ffn.shape[1]),
                         lambda i, s: ((i * (n1 + n2) + s) // _RIDER_STEPS, 0))

    def tile1(s):
        return jnp.where(s < n1, s, 0)

    def gate_spec(br):
        return pl.BlockSpec((D_MODEL, TN), lambda i, s, br=br: (0, br * _D_TILES + tile1(s)))

    def colw(kdim):
        return pl.BlockSpec((kdim, TN), lambda i, s: (0, tile1(s)))

    def rowblk(w):
        return pl.BlockSpec((tm, w), lambda i, s: (_row_ahead(i, s, n1, n_rows), 0))

    return pl.pallas_call(
        _merge_kernel,
        grid=(m // tm, n1 + n2),
        in_specs=[
            rowblk(D_MODEL), rowblk(C_CONV), rowblk(W_KV), rowblk(W_QX),
            whole(hs), whole(cs), whole(mixs), whole(xms),
            gate_spec(0), gate_spec(1), gate_spec(2),
            pl.BlockSpec((3, 1, TN), lambda i, s: (0, 0, tile1(s))),
            colw(C_CONV), colw(W_KV), colw(W_QX),
            pl.BlockSpec((D_MODEL, TN_OUT), lambda i, s: (0, jnp.maximum(s - n1, 0))),
            rider,
        ],
        out_specs=[
            pl.BlockSpec((tm, TN_OUT), lambda i, s: (i, jnp.maximum(s - n1, 0))),
            pl.BlockSpec((ms, TN_OUT), lambda i, s: (0, jnp.where(i == 0, jnp.maximum(s - n1, 0), n2 - 1))),
            rider,
        ],
        out_shape=[jax.ShapeDtypeStruct((m, D_MODEL), F32), jax.ShapeDtypeStruct((ms, D_MODEL), F32),
                   jax.ShapeDtypeStruct(w_ffn.shape, BF16)],
        scratch_shapes=[pltpu.VMEM((n1, tm, TN), BF16), pltpu.VMEM((n1, ms, TN), BF16)],
        compiler_params=_params(("arbitrary", "arbitrary"), 60),
        name="merge",
    )(h, c, mix, xm, hs, cs, mixs, xms, w_gates, w_gates, w_gates, b_gate.reshape(3, 1, D_MODEL), w_conv_out_t,
      w_dil_o_t, w_x_o_t, w_out_t, w_ffn)


def _ffn_kernel(x_ref, z_ref, xs_ref, zs_ref, gmix_ref, gpre_ref, wgu_ref, wd_ref, gpost_ref, out_ref, outs_ref,
                x1_ref, h_ref, x1s_ref, *, n_steps):
    i, s = pl.program_id(0), pl.program_id(1)
    tm, ms = x_ref.shape[0], xs_ref.shape[0]

    def prologue(x_ref, z_ref, x1_ref, row0, out_ref):
        x1 = x_ref[...] + _rms_rows(z_ref[...], gmix_ref[...])
        x1_ref[...] = x1
        h_ref[row0:row0 + x_ref.shape[0], :] = _rms_rows(x1, gpre_ref[...]).astype(BF16)
        out_ref[...] = jnp.zeros_like(out_ref)

    def step(rows):
        h = h_ref[0:rows, :]
        sub = TF // _FFN_SPLIT
        part = None
        for c in range(_FFN_SPLIT):
            gate_up = _dot(h, wgu_ref[:, 2 * c * sub:2 * (c + 1) * sub])
            gate, up = gate_up[:, :sub], gate_up[:, sub:]
            act = (gate * jax.nn.sigmoid(gate) * up).astype(BF16)
            down = _dot(act, wd_ref[c * sub:(c + 1) * sub, :])
            part = down if part is None else part + down
        out_ref[...] += part[0:tm]
        if rows > tm:
            outs_ref[...] += part[tm:rows]

    def epilogue(x1_ref, out_ref):
        out_ref[...] = x1_ref[...] + _rms_rows(out_ref[...], gpost_ref[...])

    pl.when(s == 0)(lambda: prologue(x_ref, z_ref, x1_ref, 0, out_ref))
    pl.when((i == 0) & (s == 0))(lambda: prologue(xs_ref, zs_ref, x1s_ref, tm, outs_ref))
    pl.when(i == 0)(lambda: step(tm + ms))
    pl.when(i > 0)(lambda: step(tm))
    pl.when(s == n_steps - 1)(lambda: epilogue(x1_ref, out_ref))
    pl.when((i == 0) & (s == n_steps - 1))(lambda: epilogue(x1s_ref, outs_ref))


def _ffn(x, z, xs, zs, g_mix, g_pre, w_gate_up, w_down, g_post, *, tm):
    m, ms = x.shape[0], xs.shape[0]
    n_steps = D_FF // TF
    rowblk = pl.BlockSpec((tm, D_MODEL), lambda i, s: (i, 0))
    sample = pl.BlockSpec((ms, D_MODEL), lambda i, s: (0, 0))
    gain = pl.BlockSpec((1, D_MODEL), lambda i, s: (0, 0))
    return pl.pallas_call(
        functools.partial(_ffn_kernel, n_steps=n_steps),
        grid=(m // tm, n_steps),
        in_specs=[
            rowblk, rowblk, sample, sample, gain, gain,
            pl.BlockSpec((D_MODEL, 2 * TF), lambda i, s: (0, s)),
            pl.BlockSpec((TF, D_MODEL), lambda i, s: (s, 0)),
            gain,
        ],
        out_specs=[rowblk, sample],
        out_shape=[jax.ShapeDtypeStruct((m, D_MODEL), F32), jax.ShapeDtypeStruct((ms, D_MODEL), F32)],
        scratch_shapes=[pltpu.VMEM((tm, D_MODEL), F32), pltpu.VMEM((tm + ms, D_MODEL), BF16),
                        pltpu.VMEM((ms, D_MODEL), F32)],
        compiler_params=_params(("arbitrary", "arbitrary"), 54),
        name="ffn",
    )(x, z, xs, zs, g_mix, g_pre, w_gate_up, w_down, g_post)


def kernel(x_prompt, x_sample, cache_win_k, cache_win_v, state_conv, cache_mem_k, cache_mem_v, mem_prompt,
           g_pre_mix, w_in, b_gate, conv_w, conv_b, conv_ln_g, conv_ln_b, w_conv_out, w_dil_o, g_mem, w_mem_kv,
           w_x_o, w_out, g_post_mix, g_pre_ffn, w_ffn_gate, w_ffn_up, w_ffn_down, g_post_ffn):
    batch, seq, _ = x_prompt.shape
    n_seq = x_sample.shape[0]
    depth = w_in.shape[0]
    assert depth == 1 and x_sample.shape[1] == 1 and seq % (DIL_GROUPS[-1][1] * SPAN) == 0
    assert cache_win_k.shape[2] == DIL_GROUPS[-1][0]

    idx = jnp.arange(1, N_QH + 1, dtype=F32)
    slopes = jnp.exp2(-8.0 * idx / N_QH).reshape(N_GROUPS, N_KV)

    xp, xs = x_prompt.reshape(batch * seq, D_MODEL), x_sample.reshape(n_seq, D_MODEL)
    l = 0
    w_in_t = w_in[l][:, :_GATE_TILE0 * TN].astype(BF16)
    row = lambda a: a[l].reshape(1, -1)
    g_pre, g_post, g_ffn_pre, g_ffn_post = row(g_pre_mix), row(g_post_mix), row(g_pre_ffn), row(g_post_ffn)
    cb, lng, lnb = row(conv_b), row(conv_ln_g), row(conv_ln_b)
    conv_w_slabs = conv_w[l].reshape(CONV_WIDTH, N_SLABS, LANES).transpose(1, 0, 2)

    mk, mv = _mem_kv(mem_prompt.reshape(batch * N_MEM, D_MODEL), row(g_mem), w_mem_kv[l], tm=TM_MEM)
    h_p, u_p, q_p, k_p, v_p, qx_p, h_s, u_s, q_s, k_s, v_s, qx_s = _in_proj(xp, xs, g_pre, w_in_t, tm=TM_DENSE)

    c_p, conv_p, k_rows, v_rows, *branch_weights = _conv_prompt(
        u_p, conv_w_slabs, cb, lng, lnb, k_p, v_p, (w_conv_out[l], w_dil_o[l], w_x_o[l], w_out[l]),
        batch=batch, seq=seq)
    mix_p, xm_p, w_gate_up, w_gates = _attn_prompt(
        slopes, q_p, k_p, v_p, qx_p, mk.reshape(batch, N_MEM, W_QX), mv.reshape(batch, N_MEM, W_QX),
        w_ffn_gate[l], w_ffn_up[l], w_in[l], batch=batch, seq=seq)

    c_s, conv_s = _conv_sample(state_conv[l].transpose(1, 0, 2), u_s, conv_w[l], cb, lng, lnb)
    mix_s, xm_s = _attn_sample(slopes, q_s, k_s, v_s, qx_s, cache_win_k[l], cache_win_v[l],
                               cache_mem_k[l], cache_mem_v[l])

    z_p, z_s, w_down = _merge(h_p, c_p, mix_p.reshape(batch * seq, W_KV), xm_p.reshape(batch * seq, W_QX),
                              h_s, c_s, mix_s.reshape(n_seq, W_KV), xm_s.reshape(n_seq, W_QX),
                              w_gates, b_gate[l], *branch_weights, w_ffn_down[l], tm=TM_DENSE)
    y_p, y_s = _ffn(xp, z_p, xs, z_s, g_post, g_ffn_pre, w_gate_up, w_down, g_ffn_post, tm=TM_FFN)

    win = min(DIL_GROUPS[-1][0], seq)
    kv_shape = (1, batch, seq, N_KV, HEAD_DIM)
    mem_shape = (1, batch, N_MEM, N_XHEADS, HEAD_DIM)
    new_shape = (1, n_seq, 1, N_KV, HEAD_DIM)
    return (y_p.reshape(batch, seq, D_MODEL), y_s.reshape(n_seq, 1, D_MODEL),
            k_rows.reshape(kv_shape)[:, :, seq - win:], v_rows.reshape(kv_shape)[:, :, seq - win:], conv_p,
            mk.reshape(mem_shape), mv.reshape(mem_shape),
            k_s.reshape(new_shape), v_s.reshape(new_shape), conv_s.transpose(1, 0, 2)[None])
```

```python
import functools

import jax
import jax.numpy as jnp
from jax import lax
from jax.experimental import pallas as pl
from jax.experimental.pallas import tpu as pltpu

F32 = jnp.float32
BF16 = jnp.bfloat16

D_MODEL = 2048
C_CONV = D_MODEL // 2
CONV_WIDTH = 31
CONV_PREFIX = CONV_WIDTH - 1
HEAD_DIM = 128
N_KV = 4
DIL_GROUPS = ((128, 1), (512, 4), (2048, 16))
N_GROUPS = len(DIL_GROUPS)
N_QH = N_GROUPS * N_KV
SPAN = 128
N_MEM = 256
N_XHEADS = 4
W_QD = N_QH * HEAD_DIM
W_KV = N_KV * HEAD_DIM
W_QX = N_XHEADS * HEAD_DIM
D_FF = 5632
EPS = 1e-6
NEG = -1e30
SCALE = HEAD_DIM ** -0.5
LANES = 128
SUBLANES = 8
N_SLABS = C_CONV // LANES
MIB = 1024 * 1024

TN = 512
_A_TILES = C_CONV // TN
_Q_TILES = W_QD // TN
_KV_TILES = W_KV // TN
_QX_TILES = W_QX // TN
_STEP_Q0 = _A_TILES
_STEP_K0 = _STEP_Q0 + _Q_TILES
_STEP_V0 = _STEP_K0 + _KV_TILES
_STEP_X0 = _STEP_V0 + _KV_TILES
_IN_STEPS = _STEP_X0 + _QX_TILES
_W_SLOTS = 3
_GATE_TILE0 = 2 * _A_TILES + _Q_TILES + 2 * _KV_TILES + _QX_TILES
_D_TILES = D_MODEL // TN
_CONV_PAD = -(-CONV_PREFIX // SUBLANES) * SUBLANES
_ATTN_UNITS = 8
_RIDER_STEPS = 2
_GATE_CAST_COLS = 1024
TN_OUT = 512
TF = 512
_FFN_SPLIT = 2
TM_DENSE = 1024
TM_FFN = 512
TM_MEM = 512


def _params(semantics, vmem_mib):
    return pltpu.CompilerParams(dimension_semantics=semantics, vmem_limit_bytes=vmem_mib * MIB)


def _rms_rows(x, g):
    return x * lax.rsqrt(jnp.mean(x * x, axis=-1, keepdims=True) + EPS) * g


def _dot(a, b):
    return jnp.dot(a, b, preferred_element_type=F32)


def _row_ahead(i, s, free_from, n_rows):
    return jnp.minimum(i + (s >= free_from).astype(jnp.int32), n_rows - 1)


def _in_proj_kernel(x_ref, xs_ref, g_ref, w_hbm, w2_ref, h_ref, u_ref, q_ref, k_ref, v_ref, qx_ref,
                    hs_ref, tiles_ref, ring_ref, sem_ref):
    i, s = pl.program_id(0), pl.program_id(1)

    t = i * _IN_STEPS + s
    total = pl.num_programs(0) * _IN_STEPS

    def tile_copy(step):
        s_of = step % _IN_STEPS
        col = jnp.where(s_of < _A_TILES, s_of, s_of + _A_TILES)
        slot = step % _W_SLOTS
        return pltpu.make_async_copy(w_hbm.at[:, pl.ds(pl.multiple_of(col * TN, TN), TN)],
                                     ring_ref.at[slot], sem_ref.at[slot])

    @pl.when(t == 0)
    def _():
        tile_copy(t).start()
        tile_copy(t + 1).start()

    @pl.when(t + 2 < total)
    def _():
        tile_copy(t + 2).start()

    tile_copy(t).wait()
    slot = t % _W_SLOTS

    def project(on, x_ref, h_ref, store, u_ref, q_ref, k_ref, v_ref, qx_ref):
        def norm():
            h_ref[...] = _rms_rows(x_ref[...], g_ref[...]).astype(BF16)

        def glu():
            h = h_ref[...]
            store(u_ref, _dot(h, ring_ref[slot]) * jax.nn.sigmoid(_dot(h, w2_ref[...])))

        def into(out_ref):
            return lambda: store(out_ref, _dot(h_ref[...], ring_ref[slot]))

        pl.when(on & (s == 0))(norm)
        pl.when(on & (s < _STEP_Q0))(glu)
        pl.when(on & (s >= _STEP_Q0) & (s < _STEP_K0))(into(q_ref))
        pl.when(on & (s >= _STEP_K0) & (s < _STEP_V0))(into(k_ref))
        pl.when(on & (s >= _STEP_V0) & (s < _STEP_X0))(into(v_ref))
        pl.when(on & (s >= _STEP_X0))(into(qx_ref))

    def store_slabs(out_ref, tile):
        for c in range(TN // LANES):
            out_ref[c] = tile[:, c * LANES:(c + 1) * LANES].astype(out_ref.dtype)

    def store_rows(out_ref, tile):
        out_ref[...] = tile

    project(True, x_ref, h_ref, store_slabs, u_ref, q_ref, k_ref, v_ref, qx_ref)
    project(i == 0, xs_ref, hs_ref, store_rows, *([tiles_ref] * 5))


def _in_proj(x, xs, g, w_in_t, *, tm):
    m, ms = x.shape[0], xs.shape[0]
    last_a = _A_TILES - 1
    per_tile = TN // LANES

    def out_tiles(step0, n_tiles):
        def index(i, s):
            moved = (s >= step0 + n_tiles) & (i < m // tm - 1)
            return (jnp.where(moved, 0, jnp.clip(s - step0, 0, n_tiles - 1)), i + moved.astype(jnp.int32), 0)
        return pl.BlockSpec((per_tile, tm, LANES), index)

    def slabs(width, dtype):
        return jax.ShapeDtypeStruct((width // LANES, m, LANES), dtype)

    outs = pl.pallas_call(
        _in_proj_kernel,
        grid=(m // tm, _IN_STEPS),
        in_specs=[
            pl.BlockSpec((tm, D_MODEL), lambda i, s: (_row_ahead(i, s, _A_TILES, m // tm), 0)),
            pl.BlockSpec((ms, D_MODEL), lambda i, s: (0, 0)),
            pl.BlockSpec((1, D_MODEL), lambda i, s: (0, 0)),
            pl.BlockSpec(memory_space=pl.ANY),
            pl.BlockSpec((D_MODEL, TN), lambda i, s: (0, jnp.where(s < _A_TILES, s, 0) + _A_TILES)),
        ],
        out_specs=[
            pl.BlockSpec((tm, D_MODEL), lambda i, s: (i, 0)),
            pl.BlockSpec((per_tile, tm, LANES), lambda i, s: (jnp.minimum(s, last_a), i, 0)),
            out_tiles(_STEP_Q0, _Q_TILES),
            out_tiles(_STEP_K0, _KV_TILES),
            out_tiles(_STEP_V0, _KV_TILES),
            out_tiles(_STEP_X0, _QX_TILES),
            pl.BlockSpec((ms, D_MODEL), lambda i, s: (0, 0)),
            pl.BlockSpec((ms, TN), lambda i, s: (0, jnp.where(i == 0, s, _IN_STEPS - 1))),
        ],
        out_shape=[
            jax.ShapeDtypeStruct((m, D_MODEL), BF16),
            slabs(C_CONV, F32), slabs(W_QD, F32), slabs(W_KV, F32), slabs(W_KV, F32), slabs(W_QX, BF16),
            jax.ShapeDtypeStruct((ms, D_MODEL), BF16),
            jax.ShapeDtypeStruct((ms, _IN_STEPS * TN), F32),
        ],
        scratch_shapes=[pltpu.VMEM((_W_SLOTS, D_MODEL, TN), BF16), pltpu.SemaphoreType.DMA((_W_SLOTS,))],
        compiler_params=_params(("arbitrary", "arbitrary"), 60),
        name="in_proj",
    )(x, xs, g, w_in_t, w_in_t)
    *prompt, h_s, tiles = outs
    bounds = [step * TN for step in (0, _STEP_Q0, _STEP_K0, _STEP_V0, _STEP_X0, _IN_STEPS)]
    return (*prompt, h_s, *(tiles[:, lo:hi] for lo, hi in zip(bounds, bounds[1:])))


def _mem_kv_kernel(x_ref, g_ref, w_ref, mk_ref, mv_ref, h_ref):
    s = pl.program_id(1)

    @pl.when(s == 0)
    def _():
        h_ref[...] = _rms_rows(x_ref[...], g_ref[...]).astype(BF16)
        mk_ref[...] = _dot(h_ref[...], w_ref[...].astype(BF16))

    @pl.when(s == 1)
    def _():
        mv_ref[...] = _dot(h_ref[...], w_ref[...].astype(BF16))


def _mem_kv(mem, g, w_t, *, tm):
    m = mem.shape[0]
    return pl.pallas_call(
        _mem_kv_kernel,
        grid=(m // tm, 2),
        in_specs=[
            pl.BlockSpec((tm, D_MODEL), lambda i, s: (i, 0)),
            pl.BlockSpec((1, D_MODEL), lambda i, s: (0, 0)),
            pl.BlockSpec((D_MODEL, W_QX), lambda i, s: (0, s)),
        ],
        out_specs=[
            pl.BlockSpec((tm, W_QX), lambda i, s: (i, 0)),
            pl.BlockSpec((tm, W_QX), lambda i, s: (i, 0)),
        ],
        out_shape=[jax.ShapeDtypeStruct((m, W_QX), F32)] * 2,
        scratch_shapes=[pltpu.VMEM((tm, D_MODEL), BF16)],
        compiler_params=_params(("parallel", "arbitrary"), 32),
        name="mem_kv",
    )(mem, g, w_t)


def _ln_swish_slabs(y_slabs, cb_ref, lng_ref, lnb_ref, out_dtype):
    ys = [y + cb_ref[:, c * LANES:(c + 1) * LANES] for c, y in enumerate(y_slabs)]
    tot = ys[0]
    for y in ys[1:]:
        tot = tot + y
    mu = jnp.sum(tot, axis=-1, keepdims=True) * (1.0 / C_CONV)
    ds = [y - mu for y in ys]
    sq = ds[0] * ds[0]
    for d in ds[1:]:
        sq = sq + d * d
    rs = lax.rsqrt(jnp.sum(sq, axis=-1, keepdims=True) * (1.0 / C_CONV) + EPS)
    outs = []
    for c, d in enumerate(ds):
        z = d * rs * lng_ref[:, c * LANES:(c + 1) * LANES] + lnb_ref[:, c * LANES:(c + 1) * LANES]
        outs.append((z * jax.nn.sigmoid(z)).astype(out_dtype))
    return outs


def _conv_prompt_kernel(u_ref, w_ref, cb_ref, lng_ref, lnb_ref, k_ref, v_ref, *rest, seq, tt, n_cast):
    cast_in, (c_ref, st_ref, kout_ref, vout_ref) = rest[:n_cast], rest[n_cast:n_cast + 4]
    cast_out, (head_ref, y_ref) = rest[n_cast + 4:2 * n_cast + 4], rest[2 * n_cast + 4:]
    half = pl.program_id(1)
    n_tiles = seq // tt // 2
    pad = _CONV_PAD

    for src_ref, dst_ref in ((k_ref, kout_ref), (v_ref, vout_ref)):
        for h in range(N_KV):
            dst_ref[pl.ds(h, src_ref.shape[1], stride=N_KV), :] = src_ref[h]
    for src_ref, dst_ref in zip(cast_in, cast_out):
        dst_ref[...] = src_ref[...].astype(BF16)

    def tile(src_ref, base, t0):
        def slab(c, carry):
            acc = jnp.zeros((tt, LANES), F32)
            for j in range(SUBLANES):
                taps = range(j, CONV_WIDTH, SUBLANES)
                win = src_ref[c, pl.ds(base + j, tt + taps[-1] - j), :]
                for k in taps:
                    acc = acc + win[k - j:k - j + tt] * w_ref[c, k:k + 1, :]
            y_ref[c] = acc
            return carry

        lax.fori_loop(0, N_SLABS, slab, 0, unroll=4)
        ys = [y_ref[c] for c in range(N_SLABS)]
        outs = _ln_swish_slabs(ys, cb_ref, lng_ref, lnb_ref, BF16)
        for c in range(N_SLABS):
            c_ref[pl.ds(t0, tt), c * LANES:(c + 1) * LANES] = outs[c]

    @pl.when(half == 0)
    def _():
        head_ref[:, 0:pad, :] = jnp.zeros((N_SLABS, pad, LANES), F32)
        head_ref[:, pad:pad + tt, :] = u_ref[:, 0:tt, :]
        tile(head_ref, pad - CONV_PREFIX, 0)

    @pl.when(half > 0)
    def _():
        tile(u_ref, pl.multiple_of(half * (n_tiles * tt), tt) - CONV_PREFIX, 0)

    def body(i, carry):
        t0 = pl.multiple_of((half * n_tiles + i) * tt, tt)
        tile(u_ref, t0 - CONV_PREFIX, pl.multiple_of(i * tt, tt))
        return carry

    lax.fori_loop(1, n_tiles, body, 0)
    for c in range(N_SLABS):
        st_ref[:, c * LANES:(c + 1) * LANES] = u_ref[c, seq - CONV_PREFIX:seq, :]


def _conv_prompt(u_slabs, w_slabs, conv_b, ln_g, ln_b, k, v, cast_weights, *, batch, seq):
    tt = 128
    halves = 2
    n_steps = batch * halves
    step = lambda b, j: b * halves + j
    const = lambda *idx: (lambda b, j: idx)

    def row_slice(rows, cols):
        return pl.BlockSpec((rows // n_steps, cols), lambda b, j: (step(b, j), 0))

    kv_in = pl.BlockSpec((N_KV, batch * seq // n_steps, LANES), lambda b, j: (0, step(b, j), 0))
    kv_out = row_slice(batch * seq * N_KV, HEAD_DIM)
    w_specs = [row_slice(*w.shape) for w in cast_weights]
    return pl.pallas_call(
        functools.partial(_conv_prompt_kernel, seq=seq, tt=tt, n_cast=len(cast_weights)),
        grid=(batch, halves),
        in_specs=[
            pl.BlockSpec((N_SLABS, seq, LANES), lambda b, j: (0, b, 0)),
            pl.BlockSpec((N_SLABS, CONV_WIDTH, LANES), const(0, 0, 0)),
            pl.BlockSpec((1, C_CONV), const(0, 0)),
            pl.BlockSpec((1, C_CONV), const(0, 0)),
            pl.BlockSpec((1, C_CONV), const(0, 0)),
            kv_in, kv_in,
        ] + w_specs,
        out_specs=[
            pl.BlockSpec((seq // halves, C_CONV), lambda b, j: (step(b, j), 0)),
            pl.BlockSpec((None, None, CONV_PREFIX, C_CONV), lambda b, j: (0, b, 0, 0)),
            kv_out, kv_out,
        ] + w_specs,
        out_shape=[
            jax.ShapeDtypeStruct((batch * seq, C_CONV), BF16),
            jax.ShapeDtypeStruct((1, batch, CONV_PREFIX, C_CONV), F32),
            jax.ShapeDtypeStruct((batch * seq * N_KV, HEAD_DIM), F32),
            jax.ShapeDtypeStruct((batch * seq * N_KV, HEAD_DIM), F32),
        ] + [jax.ShapeDtypeStruct(w.shape, BF16) for w in cast_weights],
        scratch_shapes=[pltpu.VMEM((N_SLABS, _CONV_PAD + tt, LANES), F32), pltpu.VMEM((N_SLABS, tt, LANES), F32)],
        compiler_params=_params(("parallel", "arbitrary"), 56),
        name="conv_prompt",
    )(u_slabs, w_slabs, conv_b, ln_g, ln_b, k, v, *cast_weights)


def _conv_sample_kernel(state_ref, u_ref, w_ref, cb_ref, lng_ref, lnb_ref, c_ref, st_ref):
    new = u_ref[...]
    y = new * w_ref[CONV_PREFIX:CONV_WIDTH, :]
    for k in range(CONV_PREFIX):
        y = y + state_ref[k] * w_ref[k:k + 1, :]
    ys = [y[:, c * LANES:(c + 1) * LANES] for c in range(N_SLABS)]
    outs = _ln_swish_slabs(ys, cb_ref, lng_ref, lnb_ref, F32)
    for c in range(N_SLABS):
        c_ref[:, c * LANES:(c + 1) * LANES] = outs[c]
    st_ref[0:CONV_PREFIX - 1] = state_ref[1:CONV_PREFIX]
    st_ref[CONV_PREFIX - 1] = new


def _conv_sample(state_t, u, conv_w, conv_b, ln_g, ln_b):
    n_seq = state_t.shape[1]
    return pl.pallas_call(
        _conv_sample_kernel,
        out_shape=[
            jax.ShapeDtypeStruct((n_seq, C_CONV), F32),
            jax.ShapeDtypeStruct((CONV_PREFIX, n_seq, C_CONV), F32),
        ],
        compiler_params=pltpu.CompilerParams(vmem_limit_bytes=32 * MIB),
        name="conv_sample",
    )(state_t, u, conv_w, conv_b, ln_g, ln_b)


def _qk(q, k):
    return lax.dot_general(q, k, (((1,), (1,)), ((), ())), preferred_element_type=F32)


def _attn_prompt_kernel(slopes_ref, q0_ref, q1_ref, q2_ref, k_ref, v_ref, qx_ref, mk_ref, mv_ref,
                        wg_ref, wu_ref, *rest, seq):
    n_cols = len(rest) - 11
    wcol_refs, (mix_ref, xm_ref, wgu_out_ref, wgate_ref) = rest[:n_cols], rest[n_cols:n_cols + 4]
    bias_ref, o_ref, lse_ref, s_ref, p_ref, m_ref, ones_ref = rest[n_cols + 4:]
    head = pl.program_id(1)
    ones_ref[...] = jnp.ones(ones_ref.shape, BF16)
    for c, wcol_ref in enumerate(wcol_refs):
        wgate_ref[:, c * _GATE_CAST_COLS:(c + 1) * _GATE_CAST_COLS] = wcol_ref[...].astype(BF16)
    sub = TF // _FFN_SPLIT
    for j in range(D_FF // sub):
        cols = slice(j * sub, (j + 1) * sub)
        wgu_out_ref[:, 2 * j * sub:(2 * j + 1) * sub] = wg_ref[:, cols].astype(BF16)
        wgu_out_ref[:, (2 * j + 1) * sub:(2 * j + 2) * sub] = wu_ref[:, cols].astype(BF16)
    qi = lax.broadcasted_iota(jnp.int32, (SPAN, 2 * SPAN), 0)
    ci = lax.broadcasted_iota(jnp.int32, (SPAN, 2 * SPAN), 1)
    dist = qi + SPAN - ci
    valid = (dist >= 0) & (dist <= SPAN)
    for g, (_, dil) in enumerate(DIL_GROUPS):
        bias = -slopes_ref[g, head] * (dist * dil).astype(F32)
        bias_ref[g] = jnp.where(valid, bias, NEG)

    def rows(ref, start, size, dil):
        if dil == 1:
            return ref[pl.ds(start, size), :]
        return ref[pl.ds(start, size, stride=dil), :]

    def put(ref, g, start, dil, val):
        if dil == 1:
            ref[g, pl.ds(start, SPAN), :] = val
        else:
            ref[g, pl.ds(start, SPAN, stride=dil), :] = val

    def staged(items, n_keys_of, scores, values, emit):
        for u, item in enumerate(items):
            s_ref[u, :, 0:n_keys_of(item)] = scores(item)
        for u, item in enumerate(items):
            s = s_ref[u, :, 0:n_keys_of(item)]
            m = jnp.max(s, axis=-1, keepdims=True)
            p_ref[u, :, 0:n_keys_of(item)] = jnp.exp(s - m).astype(BF16)
            m_ref[u] = jnp.broadcast_to(m, (SPAN, LANES))
        for u, item in enumerate(items):
            p = p_ref[u, :, 0:n_keys_of(item)]
            l = _dot(p, ones_ref[0:n_keys_of(item), :])
            emit(item, _dot(p, values(item)) * (1.0 / l), m_ref[u] + jnp.log(l))

    def units(g, dil, q_ref, starts_keys):
        def scores(item):
            start, n_keys = item
            q = rows(q_ref, start, SPAN, dil).astype(BF16)
            k = rows(k_ref, start - (n_keys - SPAN) * dil, n_keys, dil).astype(BF16)
            return _qk(q, k) * SCALE + bias_ref[g, :, 2 * SPAN - n_keys:2 * SPAN]

        def values(item):
            start, n_keys = item
            return rows(v_ref, start - (n_keys - SPAN) * dil, n_keys, dil).astype(BF16)

        def emit(item, o, lse):
            put(o_ref, g, item[0], dil, o)
            put(lse_ref, g, item[0], dil, lse)

        staged(starts_keys, lambda item: item[1], scores, values, emit)

    q_refs = (q0_ref, q1_ref, q2_ref)
    for g, (_, dil) in enumerate(DIL_GROUPS):
        n_blocks = seq // dil // SPAN
        q_ref = q_refs[g]
        if n_blocks == 1:
            per_body = _ATTN_UNITS

            def first_blocks(j, carry, g=g, dil=dil, q_ref=q_ref):
                units(g, dil, q_ref, [(j * per_body + u, SPAN) for u in range(per_body)])
                return carry
            lax.fori_loop(0, dil // per_body, first_blocks, 0)
        elif dil == 1:
            per_body = _ATTN_UNITS

            def later_blocks(j, carry, g=g, dil=dil, q_ref=q_ref):
                units(g, dil, q_ref, [(pl.multiple_of((j * per_body + u) * SPAN, SPAN), 2 * SPAN)
                                      for u in range(per_body)])
                return carry
            units(g, dil, q_ref, [(0, SPAN)] + [(n * SPAN, 2 * SPAN) for n in range(1, per_body)])
            lax.fori_loop(1, n_blocks // per_body, later_blocks, 0)
        else:
            streams = _ATTN_UNITS // n_blocks

            def residue(j, carry, g=g, dil=dil, q_ref=q_ref, n_blocks=n_blocks, streams=streams):
                work = []
                for r in range(streams):
                    start = j * streams + r
                    work.append((start, SPAN))
                    work += [(start + n * (SPAN * dil), 2 * SPAN) for n in range(1, n_blocks)]
                units(g, dil, q_ref, work)
                return carry
            lax.fori_loop(0, dil // streams, residue, 0)

    tq = 256

    def mix(j, carry):
        t0 = pl.multiple_of(j * tq, tq)
        l0 = lse_ref[0, pl.ds(t0, tq), :]
        l1 = lse_ref[1, pl.ds(t0, tq), :]
        l2 = lse_ref[2, pl.ds(t0, tq), :]
        top = jnp.maximum(jnp.maximum(l0, l1), l2)
        w0 = jnp.exp(l0 - top)
        w1 = jnp.exp(l1 - top)
        w2 = jnp.exp(l2 - top)
        num = (w0 * o_ref[0, pl.ds(t0, tq), :] + w1 * o_ref[1, pl.ds(t0, tq), :]
               + w2 * o_ref[2, pl.ds(t0, tq), :])
        mix_ref[pl.ds(t0, tq), :] = (num / (w0 + w1 + w2)).astype(BF16)
        return carry

    lax.fori_loop(0, seq // tq, mix, 0, unroll=2)

    def memory(j, carry):
        def scores(t0):
            return _qk(qx_ref[pl.ds(t0, SPAN), :], mk_ref[...].astype(BF16)) * SCALE

        def emit(t0, o, lse):
            xm_ref[pl.ds(t0, SPAN), :] = o.astype(BF16)

        staged([pl.multiple_of((j * _ATTN_UNITS + u) * SPAN, SPAN) for u in range(_ATTN_UNITS)],
               lambda t0: N_MEM, scores, lambda t0: mv_ref[...].astype(BF16), emit)
        return carry

    lax.fori_loop(0, seq // SPAN // _ATTN_UNITS, memory, 0)


def _attn_prompt(slopes, q, k, v, qx, mk, mv, w_gate, w_up, w_in, *, batch, seq):
    def col(c):
        return pl.BlockSpec((None, seq, LANES), lambda b, h, c=c: (c * N_KV + h, b, 0))

    slab_spec = col(0)
    head_spec = pl.BlockSpec((None, seq, LANES), lambda b, h: (b, 0, h))
    mem_spec = pl.BlockSpec((None, N_MEM, LANES), lambda b, h: (b, 0, h))
    n_steps = batch * N_KV

    def row_slice(rows, cols):
        return pl.BlockSpec((rows // n_steps, cols), lambda b, h: (b * N_KV + h, 0))

    weights = (w_gate, w_up)
    gu_shape = (D_MODEL, 2 * D_FF)
    gates_shape = (w_in.shape[0], 3 * D_MODEL)
    gate_col0 = _GATE_TILE0 * TN // _GATE_CAST_COLS
    wcol_specs = [pl.BlockSpec((w_in.shape[0] // n_steps, _GATE_CAST_COLS),
                               lambda b, h, c=c: (b * N_KV + h, gate_col0 + c))
                  for c in range(gates_shape[1] // _GATE_CAST_COLS)]
    return pl.pallas_call(
        functools.partial(_attn_prompt_kernel, seq=seq),
        grid=(batch, N_KV),
        in_specs=[
            pl.BlockSpec(memory_space=pltpu.SMEM),
            col(0), col(1), col(2), slab_spec, slab_spec, slab_spec, mem_spec, mem_spec,
        ] + [row_slice(*w.shape) for w in weights] + wcol_specs,
        out_specs=[head_spec, head_spec, row_slice(*gu_shape), row_slice(*gates_shape)],
        out_shape=[jax.ShapeDtypeStruct((batch, seq, W_KV), BF16),
                   jax.ShapeDtypeStruct((batch, seq, W_QX), BF16),
                   jax.ShapeDtypeStruct(gu_shape, BF16), jax.ShapeDtypeStruct(gates_shape, BF16)],
        scratch_shapes=[
            pltpu.VMEM((N_GROUPS, SPAN, 2 * SPAN), F32),
            pltpu.VMEM((N_GROUPS, seq, LANES), F32),
            pltpu.VMEM((N_GROUPS, seq, LANES), F32),
            pltpu.VMEM((_ATTN_UNITS, SPAN, 2 * SPAN), F32),
            pltpu.VMEM((_ATTN_UNITS, SPAN, 2 * SPAN), BF16),
            pltpu.VMEM((_ATTN_UNITS, SPAN, LANES), F32),
            pltpu.VMEM((2 * SPAN, LANES), BF16),
        ],
        compiler_params=_params(("parallel", "parallel"), 56),
        name="attn_prompt",
    )(slopes, q, q, q, k, v, qx, mk, mv, *weights, *([w_in] * len(wcol_specs)))


def _attn_sample_kernel(slopes_ref, q_ref, kn_ref, vn_ref, qx_ref, ck0_ref, ck1_ref, ck2_ref,
                        cv0_ref, cv1_ref, cv2_ref, mk_ref, mv_ref, mix_ref, xm_ref, *, bs):

    def by_row(ref):
        return ref if len(ref.shape) == 3 else ref.reshape(ref.shape[0], ref.shape[1] * ref.shape[2], LANES)

    def head_rows(ref, n, h, n_keys):
        return ref[n, pl.ds(h, n_keys, stride=ref.shape[1] // n_keys), :]

    ck_refs = tuple(by_row(r) for r in (ck0_ref, ck1_ref, ck2_ref))
    cv_refs = tuple(by_row(r) for r in (cv0_ref, cv1_ref, cv2_ref))
    back = (SPAN - lax.broadcasted_iota(jnp.int32, (SPAN, 1), 0)).astype(F32)

    def one(n, carry):
        for h in range(N_KV):
            hs = slice(h * LANES, (h + 1) * LANES)
            kn = kn_ref[n, :, hs]
            vn = vn_ref[n, :, hs]
            scores, s_new = [], []
            for g, (_, dil) in enumerate(DIL_GROUPS):
                qs = slice(g * W_KV + h * LANES, g * W_KV + (h + 1) * LANES)
                q = q_ref[n, :, qs]
                s = jnp.sum(head_rows(ck_refs[g], n, h, SPAN) * q, axis=-1, keepdims=True) * SCALE
                scores.append(s - slopes_ref[g, h] * (back * float(dil)))
                s_new.append(jnp.sum(kn * q, axis=-1, keepdims=True) * SCALE)
            top = s_new[0]
            for g in range(N_GROUPS):
                top = jnp.maximum(top, jnp.maximum(s_new[g], jnp.max(scores[g], axis=0, keepdims=True)))
            num = jnp.zeros((1, LANES), F32)
            den = jnp.zeros((1, 1), F32)
            for g in range(N_GROUPS):
                p = jnp.exp(scores[g] - top)
                p_new = jnp.exp(s_new[g] - top)
                num = num + jnp.sum(p * head_rows(cv_refs[g], n, h, SPAN), axis=0, keepdims=True) + p_new * vn
                den = den + jnp.sum(p, axis=0, keepdims=True) + p_new
            mix_ref[n, :, hs] = num / den
            qx = qx_ref[n, :, hs]
            sx = jnp.sum(head_rows(mk_ref, n, h, N_MEM) * qx, axis=-1, keepdims=True) * SCALE
            px = jnp.exp(sx - jnp.max(sx, axis=0, keepdims=True))
            ox = (jnp.sum(px * head_rows(mv_ref, n, h, N_MEM), axis=0, keepdims=True)
                  / jnp.sum(px, axis=0, keepdims=True))
            xm_ref[n, :, hs] = ox
        return carry

    lax.fori_loop(0, bs, one, 0)


def _attn_sample(slopes, q, kn, vn, qx, cache_k, cache_v, mem_k, mem_v):
    n_seq, win = cache_k.shape[0], cache_k.shape[1]
    bs = 4
    row = lambda w: pl.BlockSpec((bs, 1, w), lambda i: (i, 0, 0))
    as_rows = lambda a: a.reshape(n_seq, 1, a.shape[-1])
    views_k, views_v, cache_specs = [], [], []
    for _, dil in DIL_GROUPS:
        nb = win // dil // SPAN
        if dil == 1:
            shape = (n_seq, win * N_KV, HEAD_DIM)
            cache_specs.append(pl.BlockSpec((bs, SPAN * N_KV, HEAD_DIM), lambda i, nb=nb: (i, nb - 1, 0)))
        else:
            shape = (n_seq, win // dil, dil, N_KV, HEAD_DIM)
            cache_specs.append(pl.BlockSpec((bs, SPAN, None, N_KV, HEAD_DIM),
                                            lambda i, nb=nb: (i, nb - 1, 0, 0, 0)))
        views_k.append(cache_k.reshape(shape))
        views_v.append(cache_v.reshape(shape))
    mem_spec = pl.BlockSpec((bs, N_MEM * N_XHEADS, HEAD_DIM), lambda i: (i, 0, 0))
    mem_rows = lambda a: a.reshape(n_seq, N_MEM * N_XHEADS, HEAD_DIM)
    return pl.pallas_call(
        functools.partial(_attn_sample_kernel, bs=bs),
        grid=(n_seq // bs,),
        in_specs=[pl.BlockSpec(memory_space=pltpu.SMEM), row(W_QD), row(W_KV), row(W_KV), row(W_QX)]
        + cache_specs + cache_specs + [mem_spec, mem_spec],
        out_specs=[row(W_KV), row(W_QX)],
        out_shape=[jax.ShapeDtypeStruct((n_seq, 1, W_KV), F32), jax.ShapeDtypeStruct((n_seq, 1, W_QX), F32)],
        compiler_params=_params(("parallel",), 40),
        name="attn_sample",
    )(slopes, as_rows(q), as_rows(kn), as_rows(vn), as_rows(qx), *views_k, *views_v, mem_rows(mem_k),
      mem_rows(mem_v))


def _merge_kernel(h_ref, c_ref, mix_ref, xm_ref, hs_ref, cs_ref, mixs_ref, xms_ref, wg0_ref, wg1_ref, wg2_ref,
                  bg_ref, wc_ref, wd_ref, wx_ref, wo_ref, wffn_ref, z_ref, zs_ref, wffn_out_ref,
                  merged_ref, mergeds_ref):
    i, s = pl.program_id(0), pl.program_id(1)

    @pl.when(s % _RIDER_STEPS == 0)
    def _():
        wffn_out_ref[...] = wffn_ref[...].astype(BF16)

    def gated_sum(h_ref, branches, merged_ref):
        h = h_ref[...]
        acc = None
        for br, (wg_ref, y_ref, wy_ref) in enumerate(zip((wg0_ref, wg1_ref, wg2_ref), branches,
                                                         (wc_ref, wd_ref, wx_ref))):
            gate = jax.nn.sigmoid(_dot(h, wg_ref[...]) + bg_ref[br])
            term = gate * _dot(y_ref[...].astype(BF16), wy_ref[...])
            acc = term if acc is None else acc + term
        merged_ref[s] = acc.astype(BF16)

    def project(merged_ref, z_ref):
        z = None
        for j in range(_D_TILES):
            part = _dot(merged_ref[j], wo_ref[j * TN:(j + 1) * TN, :])
            z = part if z is None else z + part
        z_ref[...] = z

    first, second = s < _D_TILES, s >= _D_TILES
    pl.when(first)(lambda: gated_sum(h_ref, (c_ref, mix_ref, xm_ref), merged_ref))
    pl.when(second)(lambda: project(merged_ref, z_ref))
    pl.when(first & (i == 0))(lambda: gated_sum(hs_ref, (cs_ref, mixs_ref, xms_ref), mergeds_ref))
    pl.when(second & (i == 0))(lambda: project(mergeds_ref, zs_ref))


def _merge(h, c, mix, xm, hs, cs, mixs, xms, w_gates, b_gate, w_conv_out_t, w_dil_o_t, w_x_o_t, w_out_t, w_ffn,
           *, tm):
    m, ms = h.shape[0], hs.shape[0]
    n1, n2 = _D_TILES, D_MODEL // TN_OUT

    n_rows = m // tm
    whole = lambda a: pl.BlockSpec(a.shape, lambda i, s: (0, 0))
    rider_blocks = n_rows * (n1 + n2) // _RIDER_STEPS
    rider = pl.BlockSpec((w_ffn.shape[0] // rider_blocks, w_ffn.shape[1]),
                         lambda i, s: ((i * (n1 + n2) + s) // _RIDER_STEPS, 0))

    def tile1(s):
        return jnp.where(s < n1, s, 0)

    def gate_spec(br):
        return pl.BlockSpec((D_MODEL, TN), lambda i, s, br=br: (0, br * _D_TILES + tile1(s)))

    def colw(kdim):
        return pl.BlockSpec((kdim, TN), lambda i, s: (0, tile1(s)))

    def rowblk(w):
        return pl.BlockSpec((tm, w), lambda i, s: (_row_ahead(i, s, n1, n_rows), 0))

    return pl.pallas_call(
        _merge_kernel,
        grid=(m // tm, n1 + n2),
        in_specs=[
            rowblk(D_MODEL), rowblk(C_CONV), rowblk(W_KV), rowblk(W_QX),
            whole(hs), whole(cs), whole(mixs), whole(xms),
            gate_spec(0), gate_spec(1), gate_spec(2),
            pl.BlockSpec((3, 1, TN), lambda i, s: (0, 0, tile1(s))),
            colw(C_CONV), colw(W_KV), colw(W_QX),
            pl.BlockSpec((D_MODEL, TN_OUT), lambda i, s: (0, jnp.maximum(s - n1, 0))),
            rider,
        ],
        out_specs=[
            pl.BlockSpec((tm, TN_OUT), lambda i, s: (i, jnp.maximum(s - n1, 0))),
            pl.BlockSpec((ms, TN_OUT), lambda i, s: (0, jnp.where(i == 0, jnp.maximum(s - n1, 0), n2 - 1))),
            rider,
        ],
        out_shape=[jax.ShapeDtypeStruct((m, D_MODEL), F32), jax.ShapeDtypeStruct((ms, D_MODEL), F32),
                   jax.ShapeDtypeStruct(w_ffn.shape, BF16)],
        scratch_shapes=[pltpu.VMEM((n1, tm, TN), BF16), pltpu.VMEM((n1, ms, TN), BF16)],
        compiler_params=_params(("arbitrary", "arbitrary"), 60),
        name="merge",
    )(h, c, mix, xm, hs, cs, mixs, xms, w_gates, w_gates, w_gates, b_gate.reshape(3, 1, D_MODEL), w_conv_out_t,
      w_dil_o_t, w_x_o_t, w_out_t, w_ffn)


def _ffn_kernel(x_ref, z_ref, xs_ref, zs_ref, gmix_ref, gpre_ref, wgu_ref, wd_ref, gpost_ref, out_ref, outs_ref,
                x1_ref, h_ref, x1s_ref, *, n_steps):
    i, s = pl.program_id(0), pl.program_id(1)
    tm, ms = x_ref.shape[0], xs_ref.shape[0]

    def prologue(x_ref, z_ref, x1_ref, row0, out_ref):
        x1 = x_ref[...] + _rms_rows(z_ref[...], gmix_ref[...])
        x1_ref[...] = x1
        h_ref[row0:row0 + x_ref.shape[0], :] = _rms_rows(x1, gpre_ref[...]).astype(BF16)
        out_ref[...] = jnp.zeros_like(out_ref)

    def step(rows):
        h = h_ref[0:rows, :]
        sub = TF // _FFN_SPLIT
        part = None
        for c in range(_FFN_SPLIT):
            gate_up = _dot(h, wgu_ref[:, 2 * c * sub:2 * (c + 1) * sub])
            gate, up = gate_up[:, :sub], gate_up[:, sub:]
            act = (gate * jax.nn.sigmoid(gate) * up).astype(BF16)
            down = _dot(act, wd_ref[c * sub:(c + 1) * sub, :])
            part = down if part is None else part + down
        out_ref[...] += part[0:tm]
        if rows > tm:
            outs_ref[...] += part[tm:rows]

    def epilogue(x1_ref, out_ref):
        out_ref[...] = x1_ref[...] + _rms_rows(out_ref[...], gpost_ref[...])

    pl.when(s == 0)(lambda: prologue(x_ref, z_ref, x1_ref, 0, out_ref))
    pl.when((i == 0) & (s == 0))(lambda: prologue(xs_ref, zs_ref, x1s_ref, tm, outs_ref))
    pl.when(i == 0)(lambda: step(tm + ms))
    pl.when(i > 0)(lambda: step(tm))
    pl.when(s == n_steps - 1)(lambda: epilogue(x1_ref, out_ref))
    pl.when((i == 0) & (s == n_steps - 1))(lambda: epilogue(x1s_ref, outs_ref))


def _ffn(x, z, xs, zs, g_mix, g_pre, w_gate_up, w_down, g_post, *, tm):
    m, ms = x.shape[0], xs.shape[0]
    n_steps = D_FF // TF
    rowblk = pl.BlockSpec((tm, D_MODEL), lambda i, s: (i, 0))
    sample = pl.BlockSpec((ms, D_MODEL), lambda i, s: (0, 0))
    gain = pl.BlockSpec((1, D_MODEL), lambda i, s: (0, 0))
    return pl.pallas_call(
        functools.partial(_ffn_kernel, n_steps=n_steps),
        grid=(m // tm, n_steps),
        in_specs=[
            rowblk, rowblk, sample, sample, gain, gain,
            pl.BlockSpec((D_MODEL, 2 * TF), lambda i, s: (0, s)),
            pl.BlockSpec((TF, D_MODEL), lambda i, s: (s, 0)),
            gain,
        ],
        out_specs=[rowblk, sample],
        out_shape=[jax.ShapeDtypeStruct((m, D_MODEL), F32), jax.ShapeDtypeStruct((ms, D_MODEL), F32)],
        scratch_shapes=[pltpu.VMEM((tm, D_MODEL), F32), pltpu.VMEM((tm + ms, D_MODEL), BF16),
                        pltpu.VMEM((ms, D_MODEL), F32)],
        compiler_params=_params(("arbitrary", "arbitrary"), 54),
        name="ffn",
    )(x, z, xs, zs, g_mix, g_pre, w_gate_up, w_down, g_post)


def kernel(x_prompt, x_sample, cache_win_k, cache_win_v, state_conv, cache_mem_k, cache_mem_v, mem_prompt,
           g_pre_mix, w_in, b_gate, conv_w, conv_b, conv_ln_g, conv_ln_b, w_conv_out, w_dil_o, g_mem, w_mem_kv,
           w_x_o, w_out, g_post_mix, g_pre_ffn, w_ffn_gate, w_ffn_up, w_ffn_down, g_post_ffn):
    batch, seq, _ = x_prompt.shape
    n_seq = x_sample.shape[0]
    depth = w_in.shape[0]
    assert depth == 1 and x_sample.shape[1] == 1 and seq % (DIL_GROUPS[-1][1] * SPAN) == 0
    assert cache_win_k.shape[2] == DIL_GROUPS[-1][0]

    idx = jnp.arange(1, N_QH + 1, dtype=F32)
    slopes = jnp.exp2(-8.0 * idx / N_QH).reshape(N_GROUPS, N_KV)

    xp, xs = x_prompt.reshape(batch * seq, D_MODEL), x_sample.reshape(n_seq, D_MODEL)
    l = 0
    w_in_t = w_in[l][:, :_GATE_TILE0 * TN].astype(BF16)
    row = lambda a: a[l].reshape(1, -1)
    g_pre, g_post, g_ffn_pre, g_ffn_post = row(g_pre_mix), row(g_post_mix), row(g_pre_ffn), row(g_post_ffn)
    cb, lng, lnb = row(conv_b), row(conv_ln_g), row(conv_ln_b)
    conv_w_slabs = conv_w[l].reshape(CONV_WIDTH, N_SLABS, LANES).transpose(1, 0, 2)

    mk, mv = _mem_kv(mem_prompt.reshape(batch * N_MEM, D_MODEL), row(g_mem), w_mem_kv[l], tm=TM_MEM)
    h_p, u_p, q_p, k_p, v_p, qx_p, h_s, u_s, q_s, k_s, v_s, qx_s = _in_proj(xp, xs, g_pre, w_in_t, tm=TM_DENSE)

    c_p, conv_p, k_rows, v_rows, *branch_weights = _conv_prompt(
        u_p, conv_w_slabs, cb, lng, lnb, k_p, v_p, (w_conv_out[l], w_dil_o[l], w_x_o[l], w_out[l]),
        batch=batch, seq=seq)
    mix_p, xm_p, w_gate_up, w_gates = _attn_prompt(
        slopes, q_p, k_p, v_p, qx_p, mk.reshape(batch, N_MEM, W_QX), mv.reshape(batch, N_MEM, W_QX),
        w_ffn_gate[l], w_ffn_up[l], w_in[l], batch=batch, seq=seq)

    c_s, conv_s = _conv_sample(state_conv[l].transpose(1, 0, 2), u_s, conv_w[l], cb, lng, lnb)
    mix_s, xm_s = _attn_sample(slopes, q_s, k_s, v_s, qx_s, cache_win_k[l], cache_win_v[l],
                               cache_mem_k[l], cache_mem_v[l])

    z_p, z_s, w_down = _merge(h_p, c_p, mix_p.reshape(batch * seq, W_KV), xm_p.reshape(batch * seq, W_QX),
                              h_s, c_s, mix_s.reshape(n_seq, W_KV), xm_s.reshape(n_seq, W_QX),
                              w_gates, b_gate[l], *branch_weights, w_ffn_down[l], tm=TM_DENSE)
    y_p, y_s = _ffn(xp, z_p, xs, z_s, g_post, g_ffn_pre, w_gate_up, w_down, g_ffn_post, tm=TM_FFN)

    win = min(DIL_GROUPS[-1][0], seq)
    kv_shape = (1, batch, seq, N_KV, HEAD_DIM)
    mem_shape = (1, batch, N_MEM, N_XHEADS, HEAD_DIM)
    new_shape = (1, n_seq, 1, N_KV, HEAD_DIM)
    return (y_p.reshape(batch, seq, D_MODEL), y_s.reshape(n_seq, 1, D_MODEL),
            k_rows.reshape(kv_shape)[:, :, seq - win:], v_rows.reshape(kv_shape)[:, :, seq - win:], conv_p,
            mk.reshape(mem_shape), mv.reshape(mem_shape),
            k_s.reshape(new_shape), v_s.reshape(new_shape), conv_s.transpose(1, 0, 2)[None])
```

```python
import functools

import jax
import jax.numpy as jnp
from jax import lax
from jax.experimental import pallas as pl
from jax.experimental.pallas import tpu as pltpu

F32 = jnp.float32
BF16 = jnp.bfloat16

D_MODEL = 2048
C_CONV = D_MODEL // 2
CONV_WIDTH = 31
CONV_PREFIX = CONV_WIDTH - 1
HEAD_DIM = 128
N_KV = 4
DIL_GROUPS = ((128, 1), (512, 4), (2048, 16))
N_GROUPS = len(DIL_GROUPS)
N_QH = N_GROUPS * N_KV
SPAN = 128
N_MEM = 256
N_XHEADS = 4
W_QD = N_QH * HEAD_DIM
W_KV = N_KV * HEAD_DIM
W_QX = N_XHEADS * HEAD_DIM
D_FF = 5632
EPS = 1e-6
NEG = -1e30
SCALE = HEAD_DIM ** -0.5
LANES = 128
SUBLANES = 8
N_SLABS = C_CONV // LANES
MIB = 1024 * 1024

TN = 512
_A_TILES = C_CONV // TN
_Q_TILES = W_QD // TN
_KV_TILES = W_KV // TN
_QX_TILES = W_QX // TN
_STEP_Q0 = _A_TILES
_STEP_K0 = _STEP_Q0 + _Q_TILES
_STEP_V0 = _STEP_K0 + _KV_TILES
_STEP_X0 = _STEP_V0 + _KV_TILES
_IN_STEPS = _STEP_X0 + _QX_TILES
_W_SLOTS = 3
_GATE_TILE0 = 2 * _A_TILES + _Q_TILES + 2 * _KV_TILES + _QX_TILES
_D_TILES = D_MODEL // TN
_CONV_PAD = -(-CONV_PREFIX // SUBLANES) * SUBLANES
_ATTN_UNITS = 8
_RIDER_STEPS = 2
_GATE_CAST_COLS = 1024
TN_OUT = 512
TF = 512
_FFN_SPLIT = 2
TM_DENSE = 1024
TM_FFN = 512
TM_MEM = 512


def _params(semantics, vmem_mib):
    return pltpu.CompilerParams(dimension_semantics=semantics, vmem_limit_bytes=vmem_mib * MIB)


def _rms_rows(x, g):
    return x * lax.rsqrt(jnp.mean(x * x, axis=-1, keepdims=True) + EPS) * g


def _dot(a, b):
    return jnp.dot(a, b, preferred_element_type=F32)


def _row_ahead(i, s, free_from, n_rows):
    return jnp.minimum(i + (s >= free_from).astype(jnp.int32), n_rows - 1)


def _in_proj_kernel(x_ref, xs_ref, g_ref, w_hbm, w2_ref, h_ref, u_ref, q_ref, k_ref, v_ref, qx_ref,
                    hs_ref, tiles_ref, ring_ref, sem_ref):
    i, s = pl.program_id(0), pl.program_id(1)

    t = i * _IN_STEPS + s
    total = pl.num_programs(0) * _IN_STEPS

    def tile_copy(step):
        s_of = step % _IN_STEPS
        col = jnp.where(s_of < _A_TILES, s_of, s_of + _A_TILES)
        slot = step % _W_SLOTS
        return pltpu.make_async_copy(w_hbm.at[:, pl.ds(pl.multiple_of(col * TN, TN), TN)],
                                     ring_ref.at[slot], sem_ref.at[slot])

    @pl.when(t == 0)
    def _():
        tile_copy(t).start(priority=1)
        tile_copy(t + 1).start(priority=1)

    @pl.when(t + 2 < total)
    def _():
        tile_copy(t + 2).start(priority=1)

    tile_copy(t).wait()
    slot = t % _W_SLOTS

    def project(on, x_ref, h_ref, store, u_ref, q_ref, k_ref, v_ref, qx_ref):
        def norm():
            h_ref[...] = _rms_rows(x_ref[...], g_ref[...]).astype(BF16)

        def glu():
            h = h_ref[...]
            store(u_ref, _dot(h, ring_ref[slot]) * jax.nn.sigmoid(_dot(h, w2_ref[...])))

        def into(out_ref):
            return lambda: store(out_ref, _dot(h_ref[...], ring_ref[slot]))

        pl.when(on & (s == 0))(norm)
        pl.when(on & (s < _STEP_Q0))(glu)
        pl.when(on & (s >= _STEP_Q0) & (s < _STEP_K0))(into(q_ref))
        pl.when(on & (s >= _STEP_K0) & (s < _STEP_V0))(into(k_ref))
        pl.when(on & (s >= _STEP_V0) & (s < _STEP_X0))(into(v_ref))
        pl.when(on & (s >= _STEP_X0))(into(qx_ref))

    def store_slabs(out_ref, tile):
        for c in range(TN // LANES):
            out_ref[c] = tile[:, c * LANES:(c + 1) * LANES].astype(out_ref.dtype)

    def store_rows(out_ref, tile):
        out_ref[...] = tile

    project(True, x_ref, h_ref, store_slabs, u_ref, q_ref, k_ref, v_ref, qx_ref)
    project(i == 0, xs_ref, hs_ref, store_rows, *([tiles_ref] * 5))


def _in_proj(x, xs, g, w_in_t, *, tm):
    m, ms = x.shape[0], xs.shape[0]
    last_a = _A_TILES - 1
    per_tile = TN // LANES

    def out_tiles(step0, n_tiles):
        def index(i, s):
            moved = (s >= step0 + n_tiles) & (i < m // tm - 1)
            return (jnp.where(moved, 0, jnp.clip(s - step0, 0, n_tiles - 1)), i + moved.astype(jnp.int32), 0)
        return pl.BlockSpec((per_tile, tm, LANES), index)

    def slabs(width, dtype):
        return jax.ShapeDtypeStruct((width // LANES, m, LANES), dtype)

    outs = pl.pallas_call(
        _in_proj_kernel,
        grid=(m // tm, _IN_STEPS),
        in_specs=[
            pl.BlockSpec((tm, D_MODEL), lambda i, s: (_row_ahead(i, s, _A_TILES, m // tm), 0)),
            pl.BlockSpec((ms, D_MODEL), lambda i, s: (0, 0)),
            pl.BlockSpec((1, D_MODEL), lambda i, s: (0, 0)),
            pl.BlockSpec(memory_space=pl.ANY),
            pl.BlockSpec((D_MODEL, TN), lambda i, s: (0, jnp.where(s < _A_TILES, s, 0) + _A_TILES)),
        ],
        out_specs=[
            pl.BlockSpec((tm, D_MODEL), lambda i, s: (i, 0)),
            pl.BlockSpec((per_tile, tm, LANES), lambda i, s: (jnp.minimum(s, last_a), i, 0)),
            out_tiles(_STEP_Q0, _Q_TILES),
            out_tiles(_STEP_K0, _KV_TILES),
            out_tiles(_STEP_V0, _KV_TILES),
            out_tiles(_STEP_X0, _QX_TILES),
            pl.BlockSpec((ms, D_MODEL), lambda i, s: (0, 0)),
            pl.BlockSpec((ms, TN), lambda i, s: (0, jnp.where(i == 0, s, _IN_STEPS - 1))),
        ],
        out_shape=[
            jax.ShapeDtypeStruct((m, D_MODEL), BF16),
            slabs(C_CONV, F32), slabs(W_QD, F32), slabs(W_KV, F32), slabs(W_KV, F32), slabs(W_QX, BF16),
            jax.ShapeDtypeStruct((ms, D_MODEL), BF16),
            jax.ShapeDtypeStruct((ms, _IN_STEPS * TN), F32),
        ],
        scratch_shapes=[pltpu.VMEM((_W_SLOTS, D_MODEL, TN), BF16), pltpu.SemaphoreType.DMA((_W_SLOTS,))],
        compiler_params=_params(("arbitrary", "arbitrary"), 60),
        name="in_proj",
    )(x, xs, g, w_in_t, w_in_t)
    *prompt, h_s, tiles = outs
    bounds = [step * TN for step in (0, _STEP_Q0, _STEP_K0, _STEP_V0, _STEP_X0, _IN_STEPS)]
    return (*prompt, h_s, *(tiles[:, lo:hi] for lo, hi in zip(bounds, bounds[1:])))


def _mem_kv_kernel(x_ref, g_ref, w_ref, mk_ref, mv_ref, h_ref):
    s = pl.program_id(1)

    @pl.when(s == 0)
    def _():
        h_ref[...] = _rms_rows(x_ref[...], g_ref[...]).astype(BF16)
        mk_ref[...] = _dot(h_ref[...], w_ref[...].astype(BF16))

    @pl.when(s == 1)
    def _():
        mv_ref[...] = _dot(h_ref[...], w_ref[...].astype(BF16))


def _mem_kv(mem, g, w_t, *, tm):
    m = mem.shape[0]
    return pl.pallas_call(
        _mem_kv_kernel,
        grid=(m // tm, 2),
        in_specs=[
            pl.BlockSpec((tm, D_MODEL), lambda i, s: (i, 0)),
            pl.BlockSpec((1, D_MODEL), lambda i, s: (0, 0)),
            pl.BlockSpec((D_MODEL, W_QX), lambda i, s: (0, s)),
        ],
        out_specs=[
            pl.BlockSpec((tm, W_QX), lambda i, s: (i, 0)),
            pl.BlockSpec((tm, W_QX), lambda i, s: (i, 0)),
        ],
        out_shape=[jax.ShapeDtypeStruct((m, W_QX), F32)] * 2,
        scratch_shapes=[pltpu.VMEM((tm, D_MODEL), BF16)],
        compiler_params=_params(("parallel", "arbitrary"), 32),
        name="mem_kv",
    )(mem, g, w_t)


def _ln_swish_slabs(y_slabs, cb_ref, lng_ref, lnb_ref, out_dtype):
    ys = [y + cb_ref[:, c * LANES:(c + 1) * LANES] for c, y in enumerate(y_slabs)]
    tot = ys[0]
    for y in ys[1:]:
        tot = tot + y
    mu = jnp.sum(tot, axis=-1, keepdims=True) * (1.0 / C_CONV)
    ds = [y - mu for y in ys]
    sq = ds[0] * ds[0]
    for d in ds[1:]:
        sq = sq + d * d
    rs = lax.rsqrt(jnp.sum(sq, axis=-1, keepdims=True) * (1.0 / C_CONV) + EPS)
    outs = []
    for c, d in enumerate(ds):
        z = d * rs * lng_ref[:, c * LANES:(c + 1) * LANES] + lnb_ref[:, c * LANES:(c + 1) * LANES]
        outs.append((z * jax.nn.sigmoid(z)).astype(out_dtype))
    return outs


def _conv_prompt_kernel(u_ref, w_ref, cb_ref, lng_ref, lnb_ref, k_ref, v_ref, *rest, seq, tt, n_cast):
    cast_in, (c_ref, st_ref, kout_ref, vout_ref) = rest[:n_cast], rest[n_cast:n_cast + 4]
    cast_out, (head_ref, y_ref) = rest[n_cast + 4:2 * n_cast + 4], rest[2 * n_cast + 4:]
    half = pl.program_id(1)
    n_tiles = seq // tt // 2
    pad = _CONV_PAD

    for src_ref, dst_ref in ((k_ref, kout_ref), (v_ref, vout_ref)):
        for h in range(N_KV):
            dst_ref[pl.ds(h, src_ref.shape[1], stride=N_KV), :] = src_ref[h]
    for src_ref, dst_ref in zip(cast_in, cast_out):
        dst_ref[...] = src_ref[...].astype(BF16)

    def tile(src_ref, base, t0):
        def slab(c, carry):
            acc = jnp.zeros((tt, LANES), F32)
            for j in range(SUBLANES):
                taps = range(j, CONV_WIDTH, SUBLANES)
                win = src_ref[c, pl.ds(base + j, tt + taps[-1] - j), :]
                for k in taps:
                    acc = acc + win[k - j:k - j + tt] * w_ref[c, k:k + 1, :]
            y_ref[c] = acc
            return carry

        lax.fori_loop(0, N_SLABS, slab, 0, unroll=4)
        ys = [y_ref[c] for c in range(N_SLABS)]
        outs = _ln_swish_slabs(ys, cb_ref, lng_ref, lnb_ref, BF16)
        for c in range(N_SLABS):
            c_ref[pl.ds(t0, tt), c * LANES:(c + 1) * LANES] = outs[c]

    @pl.when(half == 0)
    def _():
        head_ref[:, 0:pad, :] = jnp.zeros((N_SLABS, pad, LANES), F32)
        head_ref[:, pad:pad + tt, :] = u_ref[:, 0:tt, :]
        tile(head_ref, pad - CONV_PREFIX, 0)

    @pl.when(half > 0)
    def _():
        tile(u_ref, pl.multiple_of(half * (n_tiles * tt), tt) - CONV_PREFIX, 0)

    def body(i, carry):
        t0 = pl.multiple_of((half * n_tiles + i) * tt, tt)
        tile(u_ref, t0 - CONV_PREFIX, pl.multiple_of(i * tt, tt))
        return carry

    lax.fori_loop(1, n_tiles, body, 0)
    for c in range(N_SLABS):
        st_ref[:, c * LANES:(c + 1) * LANES] = u_ref[c, seq - CONV_PREFIX:seq, :]


def _conv_prompt(u_slabs, w_slabs, conv_b, ln_g, ln_b, k, v, cast_weights, *, batch, seq):
    tt = 128
    halves = 2
    n_steps = batch * halves
    step = lambda b, j: b * halves + j
    const = lambda *idx: (lambda b, j: idx)

    def row_slice(rows, cols):
        return pl.BlockSpec((rows // n_steps, cols), lambda b, j: (step(b, j), 0))

    kv_in = pl.BlockSpec((N_KV, batch * seq // n_steps, LANES), lambda b, j: (0, step(b, j), 0))
    kv_out = row_slice(batch * seq * N_KV, HEAD_DIM)
    w_specs = [row_slice(*w.shape) for w in cast_weights]
    return pl.pallas_call(
        functools.partial(_conv_prompt_kernel, seq=seq, tt=tt, n_cast=len(cast_weights)),
        grid=(batch, halves),
        in_specs=[
            pl.BlockSpec((N_SLABS, seq, LANES), lambda b, j: (0, b, 0)),
            pl.BlockSpec((N_SLABS, CONV_WIDTH, LANES), const(0, 0, 0)),
            pl.BlockSpec((1, C_CONV), const(0, 0)),
            pl.BlockSpec((1, C_CONV), const(0, 0)),
            pl.BlockSpec((1, C_CONV), const(0, 0)),
            kv_in, kv_in,
        ] + w_specs,
        out_specs=[
            pl.BlockSpec((seq // halves, C_CONV), lambda b, j: (step(b, j), 0)),
            pl.BlockSpec((None, None, CONV_PREFIX, C_CONV), lambda b, j: (0, b, 0, 0)),
            kv_out, kv_out,
        ] + w_specs,
        out_shape=[
            jax.ShapeDtypeStruct((batch * seq, C_CONV), BF16),
            jax.ShapeDtypeStruct((1, batch, CONV_PREFIX, C_CONV), F32),
            jax.ShapeDtypeStruct((batch * seq * N_KV, HEAD_DIM), F32),
            jax.ShapeDtypeStruct((batch * seq * N_KV, HEAD_DIM), F32),
        ] + [jax.ShapeDtypeStruct(w.shape, BF16) for w in cast_weights],
        scratch_shapes=[pltpu.VMEM((N_SLABS, _CONV_PAD + tt, LANES), F32), pltpu.VMEM((N_SLABS, tt, LANES), F32)],
        compiler_params=_params(("parallel", "arbitrary"), 56),
        name="conv_prompt",
    )(u_slabs, w_slabs, conv_b, ln_g, ln_b, k, v, *cast_weights)


def _conv_sample_kernel(state_ref, u_ref, w_ref, cb_ref, lng_ref, lnb_ref, c_ref, st_ref):
    new = u_ref[...]
    y = new * w_ref[CONV_PREFIX:CONV_WIDTH, :]
    for k in range(CONV_PREFIX):
        y = y + state_ref[k] * w_ref[k:k + 1, :]
    ys = [y[:, c * LANES:(c + 1) * LANES] for c in range(N_SLABS)]
    outs = _ln_swish_slabs(ys, cb_ref, lng_ref, lnb_ref, F32)
    for c in range(N_SLABS):
        c_ref[:, c * LANES:(c + 1) * LANES] = outs[c]
    st_ref[0:CONV_PREFIX - 1] = state_ref[1:CONV_PREFIX]
    st_ref[CONV_PREFIX - 1] = new


def _conv_sample(state_t, u, conv_w, conv_b, ln_g, ln_b):
    n_seq = state_t.shape[1]
    return pl.pallas_call(
        _conv_sample_kernel,
        out_shape=[
            jax.ShapeDtypeStruct((n_seq, C_CONV), F32),
            jax.ShapeDtypeStruct((CONV_PREFIX, n_seq, C_CONV), F32),
        ],
        compiler_params=pltpu.CompilerParams(vmem_limit_bytes=32 * MIB),
        name="conv_sample",
    )(state_t, u, conv_w, conv_b, ln_g, ln_b)


def _qk(q, k):
    return lax.dot_general(q, k, (((1,), (1,)), ((), ())), preferred_element_type=F32)


def _attn_prompt_kernel(slopes_ref, q0_ref, q1_ref, q2_ref, k_ref, v_ref, qx_ref, mk_ref, mv_ref,
                        wg_ref, wu_ref, *rest, seq):
    n_cols = len(rest) - 11
    wcol_refs, (mix_ref, xm_ref, wgu_out_ref, wgate_ref) = rest[:n_cols], rest[n_cols:n_cols + 4]
    bias_ref, o_ref, lse_ref, s_ref, p_ref, m_ref, ones_ref = rest[n_cols + 4:]
    head = pl.program_id(1)
    ones_ref[...] = jnp.ones(ones_ref.shape, BF16)
    for c, wcol_ref in enumerate(wcol_refs):
        wgate_ref[:, c * _GATE_CAST_COLS:(c + 1) * _GATE_CAST_COLS] = wcol_ref[...].astype(BF16)
    sub = TF // _FFN_SPLIT
    for j in range(D_FF // sub):
        cols = slice(j * sub, (j + 1) * sub)
        wgu_out_ref[:, 2 * j * sub:(2 * j + 1) * sub] = wg_ref[:, cols].astype(BF16)
        wgu_out_ref[:, (2 * j + 1) * sub:(2 * j + 2) * sub] = wu_ref[:, cols].astype(BF16)
    qi = lax.broadcasted_iota(jnp.int32, (SPAN, 2 * SPAN), 0)
    ci = lax.broadcasted_iota(jnp.int32, (SPAN, 2 * SPAN), 1)
    dist = qi + SPAN - ci
    valid = (dist >= 0) & (dist <= SPAN)
    for g, (_, dil) in enumerate(DIL_GROUPS):
        bias = -slopes_ref[g, head] * (dist * dil).astype(F32)
        bias_ref[g] = jnp.where(valid, bias, NEG)

    def rows(ref, start, size, dil):
        if dil == 1:
            return ref[pl.ds(start, size), :]
        return ref[pl.ds(start, size, stride=dil), :]

    def put(ref, g, start, dil, val):
        if dil == 1:
            ref[g, pl.ds(start, SPAN), :] = val
        else:
            ref[g, pl.ds(start, SPAN, stride=dil), :] = val

    def staged(items, n_keys_of, scores, values, emit):
        for u, item in enumerate(items):
            s_ref[u, :, 0:n_keys_of(item)] = scores(item)
        for u, item in enumerate(items):
            s = s_ref[u, :, 0:n_keys_of(item)]
            m = jnp.max(s, axis=-1, keepdims=True)
            p_ref[u, :, 0:n_keys_of(item)] = jnp.exp(s - m).astype(BF16)
            m_ref[u] = jnp.broadcast_to(m, (SPAN, LANES))
        for u, item in enumerate(items):
            p = p_ref[u, :, 0:n_keys_of(item)]
            l = _dot(p, ones_ref[0:n_keys_of(item), :])
            emit(item, _dot(p, values(item)) * (1.0 / l), m_ref[u] + jnp.log(l))

    def units(g, dil, q_ref, starts_keys):
        def scores(item):
            start, n_keys = item
            q = rows(q_ref, start, SPAN, dil).astype(BF16)
            k = rows(k_ref, start - (n_keys - SPAN) * dil, n_keys, dil).astype(BF16)
            return _qk(q, k) * SCALE + bias_ref[g, :, 2 * SPAN - n_keys:2 * SPAN]

        def values(item):
            start, n_keys = item
            return rows(v_ref, start - (n_keys - SPAN) * dil, n_keys, dil).astype(BF16)

        def emit(item, o, lse):
            put(o_ref, g, item[0], dil, o)
            put(lse_ref, g, item[0], dil, lse)

        staged(starts_keys, lambda item: item[1], scores, values, emit)

    q_refs = (q0_ref, q1_ref, q2_ref)
    for g, (_, dil) in enumerate(DIL_GROUPS):
        n_blocks = seq // dil // SPAN
        q_ref = q_refs[g]
        if n_blocks == 1:
            per_body = _ATTN_UNITS

            def first_blocks(j, carry, g=g, dil=dil, q_ref=q_ref):
                units(g, dil, q_ref, [(j * per_body + u, SPAN) for u in range(per_body)])
                return carry
            lax.fori_loop(0, dil // per_body, first_blocks, 0)
        elif dil == 1:
            per_body = _ATTN_UNITS

            def later_blocks(j, carry, g=g, dil=dil, q_ref=q_ref):
                units(g, dil, q_ref, [(pl.multiple_of((j * per_body + u) * SPAN, SPAN), 2 * SPAN)
                                      for u in range(per_body)])
                return carry
            units(g, dil, q_ref, [(0, SPAN)] + [(n * SPAN, 2 * SPAN) for n in range(1, per_body)])
            lax.fori_loop(1, n_blocks // per_body, later_blocks, 0)
        else:
            streams = _ATTN_UNITS // n_blocks

            def residue(j, carry, g=g, dil=dil, q_ref=q_ref, n_blocks=n_blocks, streams=streams):
                work = []
                for r in range(streams):
                    start = j * streams + r
                    work.append((start, SPAN))
                    work += [(start + n * (SPAN * dil), 2 * SPAN) for n in range(1, n_blocks)]
                units(g, dil, q_ref, work)
                return carry
            lax.fori_loop(0, dil // streams, residue, 0)

    tq = 256

    def mix(j, carry):
        t0 = pl.multiple_of(j * tq, tq)
        l0 = lse_ref[0, pl.ds(t0, tq), :]
        l1 = lse_ref[1, pl.ds(t0, tq), :]
        l2 = lse_ref[2, pl.ds(t0, tq), :]
        top = jnp.maximum(jnp.maximum(l0, l1), l2)
        w0 = jnp.exp(l0 - top)
        w1 = jnp.exp(l1 - top)
        w2 = jnp.exp(l2 - top)
        num = (w0 * o_ref[0, pl.ds(t0, tq), :] + w1 * o_ref[1, pl.ds(t0, tq), :]
               + w2 * o_ref[2, pl.ds(t0, tq), :])
        mix_ref[pl.ds(t0, tq), :] = (num / (w0 + w1 + w2)).astype(BF16)
        return carry

    lax.fori_loop(0, seq // tq, mix, 0, unroll=2)

    def memory(j, carry):
        def scores(t0):
            return _qk(qx_ref[pl.ds(t0, SPAN), :], mk_ref[...].astype(BF16)) * SCALE

        def emit(t0, o, lse):
            xm_ref[pl.ds(t0, SPAN), :] = o.astype(BF16)

        staged([pl.multiple_of((j * _ATTN_UNITS + u) * SPAN, SPAN) for u in range(_ATTN_UNITS)],
               lambda t0: N_MEM, scores, lambda t0: mv_ref[...].astype(BF16), emit)
        return carry

    lax.fori_loop(0, seq // SPAN // _ATTN_UNITS, memory, 0)


def _attn_prompt(slopes, q, k, v, qx, mk, mv, w_gate, w_up, w_in, *, batch, seq):
    def col(c):
        return pl.BlockSpec((None, seq, LANES), lambda b, h, c=c: (c * N_KV + h, b, 0))

    slab_spec = col(0)
    head_spec = pl.BlockSpec((None, seq, LANES), lambda b, h: (b, 0, h))
    mem_spec = pl.BlockSpec((None, N_MEM, LANES), lambda b, h: (b, 0, h))
    n_steps = batch * N_KV

    def row_slice(rows, cols):
        return pl.BlockSpec((rows // n_steps, cols), lambda b, h: (b * N_KV + h, 0))

    weights = (w_gate, w_up)
    gu_shape = (D_MODEL, 2 * D_FF)
    gates_shape = (w_in.shape[0], 3 * D_MODEL)
    gate_col0 = _GATE_TILE0 * TN // _GATE_CAST_COLS
    wcol_specs = [pl.BlockSpec((w_in.shape[0] // n_steps, _GATE_CAST_COLS),
                               lambda b, h, c=c: (b * N_KV + h, gate_col0 + c))
                  for c in range(gates_shape[1] // _GATE_CAST_COLS)]
    return pl.pallas_call(
        functools.partial(_attn_prompt_kernel, seq=seq),
        grid=(batch, N_KV),
        in_specs=[
            pl.BlockSpec(memory_space=pltpu.SMEM),
            col(0), col(1), col(2), slab_spec, slab_spec, slab_spec, mem_spec, mem_spec,
        ] + [row_slice(*w.shape) for w in weights] + wcol_specs,
        out_specs=[head_spec, head_spec, row_slice(*gu_shape), row_slice(*gates_shape)],
        out_shape=[jax.ShapeDtypeStruct((batch, seq, W_KV), BF16),
                   jax.ShapeDtypeStruct((batch, seq, W_QX), BF16),
                   jax.ShapeDtypeStruct(gu_shape, BF16), jax.ShapeDtypeStruct(gates_shape, BF16)],
        scratch_shapes=[
            pltpu.VMEM((N_GROUPS, SPAN, 2 * SPAN), F32),
            pltpu.VMEM((N_GROUPS, seq, LANES), F32),
            pltpu.VMEM((N_GROUPS, seq, LANES), F32),
            pltpu.VMEM((_ATTN_UNITS, SPAN, 2 * SPAN), F32),
            pltpu.VMEM((_ATTN_UNITS, SPAN, 2 * SPAN), BF16),
            pltpu.VMEM((_ATTN_UNITS, SPAN, LANES), F32),
            pltpu.VMEM((2 * SPAN, LANES), BF16),
        ],
        compiler_params=_params(("parallel", "parallel"), 56),
        name="attn_prompt",
    )(slopes, q, q, q, k, v, qx, mk, mv, *weights, *([w_in] * len(wcol_specs)))


def _attn_sample_kernel(slopes_ref, q_ref, kn_ref, vn_ref, qx_ref, ck0_ref, ck1_ref, ck2_ref,
                        cv0_ref, cv1_ref, cv2_ref, mk_ref, mv_ref, mix_ref, xm_ref, *, bs):

    def by_row(ref):
        return ref if len(ref.shape) == 3 else ref.reshape(ref.shape[0], ref.shape[1] * ref.shape[2], LANES)

    def head_rows(ref, n, h, n_keys):
        return ref[n, pl.ds(h, n_keys, stride=ref.shape[1] // n_keys), :]

    ck_refs = tuple(by_row(r) for r in (ck0_ref, ck1_ref, ck2_ref))
    cv_refs = tuple(by_row(r) for r in (cv0_ref, cv1_ref, cv2_ref))
    back = (SPAN - lax.broadcasted_iota(jnp.int32, (SPAN, 1), 0)).astype(F32)

    def one(n, carry):
        for h in range(N_KV):
            hs = slice(h * LANES, (h + 1) * LANES)
            kn = kn_ref[n, :, hs]
            vn = vn_ref[n, :, hs]
            scores, s_new = [], []
            for g, (_, dil) in enumerate(DIL_GROUPS):
                qs = slice(g * W_KV + h * LANES, g * W_KV + (h + 1) * LANES)
                q = q_ref[n, :, qs]
                s = jnp.sum(head_rows(ck_refs[g], n, h, SPAN) * q, axis=-1, keepdims=True) * SCALE
                scores.append(s - slopes_ref[g, h] * (back * float(dil)))
                s_new.append(jnp.sum(kn * q, axis=-1, keepdims=True) * SCALE)
            top = s_new[0]
            for g in range(N_GROUPS):
                top = jnp.maximum(top, jnp.maximum(s_new[g], jnp.max(scores[g], axis=0, keepdims=True)))
            num = jnp.zeros((1, LANES), F32)
            den = jnp.zeros((1, 1), F32)
            for g in range(N_GROUPS):
                p = jnp.exp(scores[g] - top)
                p_new = jnp.exp(s_new[g] - top)
                num = num + jnp.sum(p * head_rows(cv_refs[g], n, h, SPAN), axis=0, keepdims=True) + p_new * vn
                den = den + jnp.sum(p, axis=0, keepdims=True) + p_new
            mix_ref[n, :, hs] = num / den
            qx = qx_ref[n, :, hs]
            sx = jnp.sum(head_rows(mk_ref, n, h, N_MEM) * qx, axis=-1, keepdims=True) * SCALE
            px = jnp.exp(sx - jnp.max(sx, axis=0, keepdims=True))
            ox = (jnp.sum(px * head_rows(mv_ref, n, h, N_MEM), axis=0, keepdims=True)
                  / jnp.sum(px, axis=0, keepdims=True))
            xm_ref[n, :, hs] = ox
        return carry

    lax.fori_loop(0, bs, one, 0)


def _attn_sample(slopes, q, kn, vn, qx, cache_k, cache_v, mem_k, mem_v):
    n_seq, win = cache_k.shape[0], cache_k.shape[1]
    bs = 4
    row = lambda w: pl.BlockSpec((bs, 1, w), lambda i: (i, 0, 0))
    as_rows = lambda a: a.reshape(n_seq, 1, a.shape[-1])
    views_k, views_v, cache_specs = [], [], []
    for _, dil in DIL_GROUPS:
        nb = win // dil // SPAN
        if dil == 1:
            shape = (n_seq, win * N_KV, HEAD_DIM)
            cache_specs.append(pl.BlockSpec((bs, SPAN * N_KV, HEAD_DIM), lambda i, nb=nb: (i, nb - 1, 0)))
        else:
            shape = (n_seq, win // dil, dil, N_KV, HEAD_DIM)
            cache_specs.append(pl.BlockSpec((bs, SPAN, None, N_KV, HEAD_DIM),
                                            lambda i, nb=nb: (i, nb - 1, 0, 0, 0)))
        views_k.append(cache_k.reshape(shape))
        views_v.append(cache_v.reshape(shape))
    mem_spec = pl.BlockSpec((bs, N_MEM * N_XHEADS, HEAD_DIM), lambda i: (i, 0, 0))
    mem_rows = lambda a: a.reshape(n_seq, N_MEM * N_XHEADS, HEAD_DIM)
    return pl.pallas_call(
        functools.partial(_attn_sample_kernel, bs=bs),
        grid=(n_seq // bs,),
        in_specs=[pl.BlockSpec(memory_space=pltpu.SMEM), row(W_QD), row(W_KV), row(W_KV), row(W_QX)]
        + cache_specs + cache_specs + [mem_spec, mem_spec],
        out_specs=[row(W_KV), row(W_QX)],
        out_shape=[jax.ShapeDtypeStruct((n_seq, 1, W_KV), F32), jax.ShapeDtypeStruct((n_seq, 1, W_QX), F32)],
        compiler_params=_params(("parallel",), 40),
        name="attn_sample",
    )(slopes, as_rows(q), as_rows(kn), as_rows(vn), as_rows(qx), *views_k, *views_v, mem_rows(mem_k),
      mem_rows(mem_v))


def _merge_kernel(h_ref, c_ref, mix_ref, xm_ref, hs_ref, cs_ref, mixs_ref, xms_ref, wg0_ref, wg1_ref, wg2_ref,
                  bg_ref, wc_ref, wd_ref, wx_ref, wo_ref, wffn_ref, z_ref, zs_ref, wffn_out_ref,
                  merged_ref, mergeds_ref):
    i, s = pl.program_id(0), pl.program_id(1)

    @pl.when(s % _RIDER_STEPS == 0)
    def _():
        wffn_out_ref[...] = wffn_ref[...].astype(BF16)

    def gated_sum(h_ref, branches, merged_ref):
        h = h_ref[...]
        acc = None
        for br, (wg_ref, y_ref, wy_ref) in enumerate(zip((wg0_ref, wg1_ref, wg2_ref), branches,
                                                         (wc_ref, wd_ref, wx_ref))):
            gate = jax.nn.sigmoid(_dot(h, wg_ref[...]) + bg_ref[br])
            term = gate * _dot(y_ref[...].astype(BF16), wy_ref[...])
            acc = term if acc is None else acc + term
        merged_ref[s] = acc.astype(BF16)

    def project(merged_ref, z_ref):
        z = None
        for j in range(_D_TILES):
            part = _dot(merged_ref[j], wo_ref[j * TN:(j + 1) * TN, :])
            z = part if z is None else z + part
        z_ref[...] = z

    first, second = s < _D_TILES, s >= _D_TILES
    pl.when(first)(lambda: gated_sum(h_ref, (c_ref, mix_ref, xm_ref), merged_ref))
    pl.when(second)(lambda: project(merged_ref, z_ref))
    pl.when(first & (i == 0))(lambda: gated_sum(hs_ref, (cs_ref, mixs_ref, xms_ref), mergeds_ref))
    pl.when(second & (i == 0))(lambda: project(mergeds_ref, zs_ref))


def _merge(h, c, mix, xm, hs, cs, mixs, xms, w_gates, b_gate, w_conv_out_t, w_dil_o_t, w_x_o_t, w_out_t, w_ffn,
           *, tm):
    m, ms = h.shape[0], hs.shape[0]
    n1, n2 = _D_TILES, D_MODEL // TN_OUT

    n_rows = m // tm
    whole = lambda a: pl.BlockSpec(a.shape, lambda i, s: (0, 0))
    rider_blocks = n_rows * (n1 + n2) // _RIDER_STEPS
    rider = pl.BlockSpec((w_ffn.shape[0] // rider_blocks, w_ffn.shape[1]),
                         lambda i, s: ((i * (n1 + n2) + s) // _RIDER_STEPS, 0))

    def tile1(s):
        return jnp.where(s < n1, s, 0)

    def gate_spec(br):
        return pl.BlockSpec((D_MODEL, TN), lambda i, s, br=br: (0, br * _D_TILES + tile1(s)))

    def colw(kdim):
        return pl.BlockSpec((kdim, TN), lambda i, s: (0, tile1(s)))

    def rowblk(w):
        return pl.BlockSpec((tm, w), lambda i, s: (_row_ahead(i, s, n1, n_rows), 0))

    return pl.pallas_call(
        _merge_kernel,
        grid=(m // tm, n1 + n2),
        in_specs=[
            rowblk(D_MODEL), rowblk(C_CONV), rowblk(W_KV), rowblk(W_QX),
            whole(hs), whole(cs), whole(mixs), whole(xms),
            gate_spec(0), gate_spec(1), gate_spec(2),
            pl.BlockSpec((3, 1, TN), lambda i, s: (0, 0, tile1(s))),
            colw(C_CONV), colw(W_KV), colw(W_QX),
            pl.BlockSpec((D_MODEL, TN_OUT), lambda i, s: (0, jnp.maximum(s - n1, 0))),
            rider,
        ],
        out_specs=[
            pl.BlockSpec((tm, TN_OUT), lambda i, s: (i, jnp.maximum(s - n1, 0))),
            pl.BlockSpec((ms, TN_OUT), lambda i, s: (0, jnp.where(i == 0, jnp.maximum(s - n1, 0), n2 - 1))),
            rider,
        ],
        out_shape=[jax.ShapeDtypeStruct((m, D_MODEL), F32), jax.ShapeDtypeStruct((ms, D_MODEL), F32),
                   jax.ShapeDtypeStruct(w_ffn.shape, BF16)],
        scratch_shapes=[pltpu.VMEM((n1, tm, TN), BF16), pltpu.VMEM((n1, ms, TN), BF16)],
        compiler_params=_params(("arbitrary", "arbitrary"), 60),
        name="merge",
    )(h, c, mix, xm, hs, cs, mixs, xms, w_gates, w_gates, w_gates, b_gate.reshape(3, 1, D_MODEL), w_conv_out_t,
      w_dil_o_t, w_x_o_t, w_out_t, w_ffn)


def _ffn_kernel(x_ref, z_ref, xs_ref, zs_ref, gmix_ref, gpre_ref, wgu_ref, wd_ref, gpost_ref, out_ref, outs_ref,
                x1_ref, h_ref, x1s_ref, *, n_steps):
    i, s = pl.program_id(0), pl.program_id(1)
    tm, ms = x_ref.shape[0], xs_ref.shape[0]

    def prologue(x_ref, z_ref, x1_ref, row0, out_ref):
        x1 = x_ref[...] + _rms_rows(z_ref[...], gmix_ref[...])
        x1_ref[...] = x1
        h_ref[row0:row0 + x_ref.shape[0], :] = _rms_rows(x1, gpre_ref[...]).astype(BF16)
        out_ref[...] = jnp.zeros_like(out_ref)

    def step(rows):
        h = h_ref[0:rows, :]
        sub = TF // _FFN_SPLIT
        part = None
        for c in range(_FFN_SPLIT):
            gate_up = _dot(h, wgu_ref[:, 2 * c * sub:2 * (c + 1) * sub])
            gate, up = gate_up[:, :sub], gate_up[:, sub:]
            act = (gate * jax.nn.sigmoid(gate) * up).astype(BF16)
            down = _dot(act, wd_ref[c * sub:(c + 1) * sub, :])
            part = down if part is None else part + down
        out_ref[...] += part[0:tm]
        if rows > tm:
            outs_ref[...] += part[tm:rows]

    def epilogue(x1_ref, out_ref):
        out_ref[...] = x1_ref[...] + _rms_rows(out_ref[...], gpost_ref[...])

    pl.when(s == 0)(lambda: prologue(x_ref, z_ref, x1_ref, 0, out_ref))
    pl.when((i == 0) & (s == 0))(lambda: prologue(xs_ref, zs_ref, x1s_ref, tm, outs_ref))
    pl.when(i == 0)(lambda: step(tm + ms))
    pl.when(i > 0)(lambda: step(tm))
    pl.when(s == n_steps - 1)(lambda: epilogue(x1_ref, out_ref))
    pl.when((i == 0) & (s == n_steps - 1))(lambda: epilogue(x1s_ref, outs_ref))


def _ffn(x, z, xs, zs, g_mix, g_pre, w_gate_up, w_down, g_post, *, tm):
    m, ms = x.shape[0], xs.shape[0]
    n_steps = D_FF // TF
    rowblk = pl.BlockSpec((tm, D_MODEL), lambda i, s: (i, 0))
    sample = pl.BlockSpec((ms, D_MODEL), lambda i, s: (0, 0))
    gain = pl.BlockSpec((1, D_MODEL), lambda i, s: (0, 0))
    return pl.pallas_call(
        functools.partial(_ffn_kernel, n_steps=n_steps),
        grid=(m // tm, n_steps),
        in_specs=[
            rowblk, rowblk, sample, sample, gain, gain,
            pl.BlockSpec((D_MODEL, 2 * TF), lambda i, s: (0, s)),
            pl.BlockSpec((TF, D_MODEL), lambda i, s: (s, 0)),
            gain,
        ],
        out_specs=[rowblk, sample],
        out_shape=[jax.ShapeDtypeStruct((m, D_MODEL), F32), jax.ShapeDtypeStruct((ms, D_MODEL), F32)],
        scratch_shapes=[pltpu.VMEM((tm, D_MODEL), F32), pltpu.VMEM((tm + ms, D_MODEL), BF16),
                        pltpu.VMEM((ms, D_MODEL), F32)],
        compiler_params=_params(("arbitrary", "arbitrary"), 54),
        name="ffn",
    )(x, z, xs, zs, g_mix, g_pre, w_gate_up, w_down, g_post)


def kernel(x_prompt, x_sample, cache_win_k, cache_win_v, state_conv, cache_mem_k, cache_mem_v, mem_prompt,
           g_pre_mix, w_in, b_gate, conv_w, conv_b, conv_ln_g, conv_ln_b, w_conv_out, w_dil_o, g_mem, w_mem_kv,
           w_x_o, w_out, g_post_mix, g_pre_ffn, w_ffn_gate, w_ffn_up, w_ffn_down, g_post_ffn):
    batch, seq, _ = x_prompt.shape
    n_seq = x_sample.shape[0]
    depth = w_in.shape[0]
    assert depth == 1 and x_sample.shape[1] == 1 and seq % (DIL_GROUPS[-1][1] * SPAN) == 0
    assert cache_win_k.shape[2] == DIL_GROUPS[-1][0]

    idx = jnp.arange(1, N_QH + 1, dtype=F32)
    slopes = jnp.exp2(-8.0 * idx / N_QH).reshape(N_GROUPS, N_KV)

    xp, xs = x_prompt.reshape(batch * seq, D_MODEL), x_sample.reshape(n_seq, D_MODEL)
    l = 0
    w_in_t = w_in[l][:, :_GATE_TILE0 * TN].astype(BF16)
    row = lambda a: a[l].reshape(1, -1)
    g_pre, g_post, g_ffn_pre, g_ffn_post = row(g_pre_mix), row(g_post_mix), row(g_pre_ffn), row(g_post_ffn)
    cb, lng, lnb = row(conv_b), row(conv_ln_g), row(conv_ln_b)
    conv_w_slabs = conv_w[l].reshape(CONV_WIDTH, N_SLABS, LANES).transpose(1, 0, 2)

    mk, mv = _mem_kv(mem_prompt.reshape(batch * N_MEM, D_MODEL), row(g_mem), w_mem_kv[l], tm=TM_MEM)
    h_p, u_p, q_p, k_p, v_p, qx_p, h_s, u_s, q_s, k_s, v_s, qx_s = _in_proj(xp, xs, g_pre, w_in_t, tm=TM_DENSE)

    c_p, conv_p, k_rows, v_rows, *branch_weights = _conv_prompt(
        u_p, conv_w_slabs, cb, lng, lnb, k_p, v_p, (w_conv_out[l], w_dil_o[l], w_x_o[l], w_out[l]),
        batch=batch, seq=seq)
    mix_p, xm_p, w_gate_up, w_gates = _attn_prompt(
        slopes, q_p, k_p, v_p, qx_p, mk.reshape(batch, N_MEM, W_QX), mv.reshape(batch, N_MEM, W_QX),
        w_ffn_gate[l], w_ffn_up[l], w_in[l], batch=batch, seq=seq)

    c_s, conv_s = _conv_sample(state_conv[l].transpose(1, 0, 2), u_s, conv_w[l], cb, lng, lnb)
    mix_s, xm_s = _attn_sample(slopes, q_s, k_s, v_s, qx_s, cache_win_k[l], cache_win_v[l],
                               cache_mem_k[l], cache_mem_v[l])

    z_p, z_s, w_down = _merge(h_p, c_p, mix_p.reshape(batch * seq, W_KV), xm_p.reshape(batch * seq, W_QX),
                              h_s, c_s, mix_s.reshape(n_seq, W_KV), xm_s.reshape(n_seq, W_QX),
                              w_gates, b_gate[l], *branch_weights, w_ffn_down[l], tm=TM_DENSE)
    y_p, y_s = _ffn(xp, z_p, xs, z_s, g_post, g_ffn_pre, w_gate_up, w_down, g_ffn_post, tm=TM_FFN)

    win = min(DIL_GROUPS[-1][0], seq)
    kv_shape = (1, batch, seq, N_KV, HEAD_DIM)
    mem_shape = (1, batch, N_MEM, N_XHEADS, HEAD_DIM)
    new_shape = (1, n_seq, 1, N_KV, HEAD_DIM)
    return (y_p.reshape(batch, seq, D_MODEL), y_s.reshape(n_seq, 1, D_MODEL),
            k_rows.reshape(kv_shape)[:, :, seq - win:], v_rows.reshape(kv_shape)[:, :, seq - win:], conv_p,
            mk.reshape(mem_shape), mv.reshape(mem_shape),
            k_s.reshape(new_shape), v_s.reshape(new_shape), conv_s.transpose(1, 0, 2)[None])
```
